```python
import jax, jax.numpy as jnp
from jax import lax
import numpy as np

D_MODEL = 1024
BATCH = 8
SEQ = 4096
DEPTH = 1

N_MEM = 256
CONV_WIDTH = D_MODEL
CONV_K = 3
POOL_WINDOWS = (2, 4, 8, 16)
POOL_GROUPS = len(POOL_WINDOWS)
POOL_WIDTH = D_MODEL
POOL_GROUP_DIM = POOL_WIDTH // POOL_GROUPS
X_HEADS = 4
X_HEAD_DIM = D_MODEL // X_HEADS
X_WIDTH = X_HEADS * X_HEAD_DIM
N_BRANCH = 3
D_FF = ((8 * D_MODEL // 3 + 255) // 256) * 256
EPS = 1e-6
IN_SPLITS = (CONV_WIDTH, CONV_WIDTH, CONV_WIDTH, POOL_WIDTH, X_WIDTH, D_MODEL, D_MODEL, D_MODEL)
D_IN = sum(IN_SPLITS)

kernel_name = "hybrid_gated_conv_pool_memxattn_block"


def rms_norm(x, g):
    xf = x.astype(jnp.float32)
    y = xf * lax.rsqrt(jnp.mean(xf * xf, axis=-1, keepdims=True) + EPS)
    return (y * g.astype(jnp.float32)).astype(x.dtype)


def causal_depthwise_conv(u, w):
    c = u.shape[-1]
    return lax.conv_general_dilated(
        u, w[:, None, :].astype(u.dtype), window_strides=(1,), padding=[(CONV_K - 1, 0)],
        dimension_numbers=("NWC", "WIO", "NWC"), feature_group_count=c)


def multiscale_causal_pool(u):
    b, s, _ = u.shape
    uf = u.astype(jnp.float32).reshape(b, s, POOL_GROUPS, POOL_GROUP_DIM)
    csum = jnp.cumsum(uf, axis=1)
    pos = jnp.arange(1, s + 1, dtype=jnp.int32)
    outs = []
    for g, w in enumerate(POOL_WINDOWS):
        cg = csum[:, :, g]
        lower = jnp.pad(cg, ((0, 0), (w, 0), (0, 0)))[:, :s]
        cnt = jnp.minimum(pos, w).astype(jnp.float32)[None, :, None]
        outs.append((cg - lower) / cnt - uf[:, :, g])
    return jnp.stack(outs, axis=2).astype(u.dtype)


def memory_cross_attention(q, mem_n, w_kv):
    b, s, _ = q.shape
    m = mem_n.shape[1]
    qh = q.reshape(b, s, X_HEADS, X_HEAD_DIM)
    kv = mem_n @ w_kv
    k, v = jnp.split(kv, 2, axis=-1)
    k = k.reshape(b, m, X_HEADS, X_HEAD_DIM)
    v = v.reshape(b, m, X_HEADS, X_HEAD_DIM)
    scores = jnp.einsum("bshd,bmhd->bhsm", qh, k).astype(jnp.float32) * (X_HEAD_DIM ** -0.5)
    probs = jax.nn.softmax(scores, axis=-1).astype(v.dtype)
    o = jnp.einsum("bhsm,bmhd->bshd", probs, v)
    return o.reshape(b, s, X_WIDTH)


def _fwd_setup_inputs(seed: int = 0) -> dict:
    key = jax.random.key(seed)
    ks = jax.random.split(key, 20)
    f32 = jnp.float32
    L = DEPTH

    def nrm(k, shape, fan_in):
        return jax.random.normal(k, shape, f32) * (fan_in ** -0.5)

    def gain(k, shape):
        return 1.0 + 0.05 * jax.random.normal(k, shape, f32)

    return {
        "x": jax.random.normal(ks[0], (BATCH, SEQ, D_MODEL), f32),
        "mem": jax.random.normal(ks[1], (BATCH, N_MEM, D_MODEL), f32),
        "norm_mix": gain(ks[2], (L, D_MODEL)),
        "w_in": nrm(ks[3], (L, D_MODEL, D_IN), D_MODEL),
        "conv_w": nrm(ks[4], (L, CONV_K, CONV_WIDTH), CONV_K),
        "w_conv_out": nrm(ks[5], (L, CONV_WIDTH, D_MODEL), CONV_WIDTH),
        "w_pool": nrm(ks[6], (L, POOL_GROUPS, POOL_GROUP_DIM, POOL_GROUP_DIM), POOL_GROUP_DIM),
        "pool_scale": gain(ks[7], (L, POOL_WIDTH)),
        "norm_mem": gain(ks[8], (L, D_MODEL)),
        "w_kv": nrm(ks[9], (L, D_MODEL, 2 * X_WIDTH), D_MODEL),
        "w_xattn_out": nrm(ks[10], (L, X_WIDTH, D_MODEL), X_WIDTH),
        "w_out": nrm(ks[11], (L, D_MODEL, D_MODEL), D_MODEL),
        "norm_ffn": gain(ks[12], (L, D_MODEL)),
        "w_gate": nrm(ks[13], (L, D_MODEL, D_FF), D_MODEL),
        "w_up": nrm(ks[14], (L, D_MODEL, D_FF), D_MODEL),
        "w_down": nrm(ks[15], (L, D_FF, D_MODEL), D_FF),
        "norm_final": gain(ks[16], (D_MODEL,)),
    }


def _fwd_reference(x, mem, norm_mix, w_in, conv_w, w_conv_out, w_pool, pool_scale, norm_mem,
              w_kv, w_xattn_out, w_out, norm_ffn, w_gate, w_up, w_down, norm_final):
    b, s, _ = x.shape
    offsets = np.cumsum((0,) + IN_SPLITS)
    for l in range(DEPTH):
        h = rms_norm(x, norm_mix[l])
        proj = h @ w_in[l]
        b_a, c_a, u_a, u_p, q_x, g_a, g_p, g_x = [
            proj[..., int(offsets[i]):int(offsets[i + 1])] for i in range(len(IN_SPLITS))]

        y_a = (b_a * causal_depthwise_conv(c_a * u_a, conv_w[l])) @ w_conv_out[l]

        pooled = multiscale_causal_pool(u_p)
        y_p = jnp.einsum("bsgc,gcd->bsgd", pooled, w_pool[l]).reshape(b, s, POOL_WIDTH) * pool_scale[l]

        mem_n = rms_norm(mem, norm_mem[l])
        y_x = memory_cross_attention(q_x, mem_n, w_kv[l]) @ w_xattn_out[l]

        merged = (jax.nn.sigmoid(g_a) * y_a + jax.nn.sigmoid(g_p) * y_p
                  + jax.nn.sigmoid(g_x) * y_x)
        x = x + merged @ w_out[l]

        h = rms_norm(x, norm_ffn[l])
        x = x + (jax.nn.silu(h @ w_gate[l]) * (h @ w_up[l])) @ w_down[l]
    return rms_norm(x, norm_final)


import jax as _jax
import jax.numpy as _jnp

TWIN_FORMAT = 'train_step'
FWD_PARAMS = ['x', 'mem', 'norm_mix', 'w_in', 'conv_w', 'w_conv_out', 'w_pool', 'pool_scale', 'norm_mem', 'w_kv', 'w_xattn_out', 'w_out', 'norm_ffn', 'w_gate', 'w_up', 'w_down', 'norm_final']
TWIN_WEIGHTS = ['norm_mix', 'w_in', 'conv_w', 'w_conv_out', 'w_pool', 'pool_scale', 'norm_mem', 'w_kv', 'w_xattn_out', 'w_out', 'norm_ffn', 'w_gate', 'w_up', 'w_down', 'norm_final']
TWIN_DIFF_INPUT = 'x'
TWIN_INPUTS = ['x', 'mem', 'norm_mix', 'w_in', 'conv_w', 'w_conv_out', 'w_pool', 'pool_scale', 'norm_mem', 'w_kv', 'w_xattn_out', 'w_out', 'norm_ffn', 'w_gate', 'w_up', 'w_down', 'norm_final', 'loss_target', 'm_norm_mix', 'm_w_in', 'm_conv_w', 'm_w_conv_out', 'm_w_pool', 'm_pool_scale', 'm_norm_mem', 'm_w_kv', 'm_w_xattn_out', 'm_w_out', 'm_norm_ffn', 'm_w_gate', 'm_w_up', 'm_w_down', 'm_norm_final', 'v_norm_mix', 'v_w_in', 'v_conv_w', 'v_w_conv_out', 'v_w_pool', 'v_pool_scale', 'v_norm_mem', 'v_w_kv', 'v_w_xattn_out', 'v_w_out', 'v_norm_ffn', 'v_w_gate', 'v_w_up', 'v_w_down', 'v_norm_final']
TWIN_OUTPUTS = ['loss', 'grad_x', 'grad_norm_mix', 'grad_w_in', 'grad_conv_w', 'grad_w_conv_out', 'grad_w_pool', 'grad_pool_scale', 'grad_norm_mem', 'grad_w_kv', 'grad_w_xattn_out', 'grad_w_out', 'grad_norm_ffn', 'grad_w_gate', 'grad_w_up', 'grad_w_down', 'grad_norm_final', 'delta_norm_mix', 'delta_w_in', 'delta_conv_w', 'delta_w_conv_out', 'delta_w_pool', 'delta_pool_scale', 'delta_norm_mem', 'delta_w_kv', 'delta_w_xattn_out', 'delta_w_out', 'delta_norm_ffn', 'delta_w_gate', 'delta_w_up', 'delta_w_down', 'delta_norm_final', 'new_m_norm_mix', 'new_m_w_in', 'new_m_conv_w', 'new_m_w_conv_out', 'new_m_w_pool', 'new_m_pool_scale', 'new_m_norm_mem', 'new_m_w_kv', 'new_m_w_xattn_out', 'new_m_w_out', 'new_m_norm_ffn', 'new_m_w_gate', 'new_m_w_up', 'new_m_w_down', 'new_m_norm_final', 'new_v_norm_mix', 'new_v_w_in', 'new_v_conv_w', 'new_v_w_conv_out', 'new_v_w_pool', 'new_v_pool_scale', 'new_v_norm_mem', 'new_v_w_kv', 'new_v_w_xattn_out', 'new_v_w_out', 'new_v_norm_ffn', 'new_v_w_gate', 'new_v_w_up', 'new_v_w_down', 'new_v_norm_final']
TWIN_LEAF_KINDS = {'loss': 'loss', 'grad_x': 'grad_x', 'grad_norm_mix': 'grad_w', 'grad_w_in': 'grad_w', 'grad_conv_w': 'grad_w', 'grad_w_conv_out': 'grad_w', 'grad_w_pool': 'grad_w', 'grad_pool_scale': 'grad_w', 'grad_norm_mem': 'grad_w', 'grad_w_kv': 'grad_w', 'grad_w_xattn_out': 'grad_w', 'grad_w_out': 'grad_w', 'grad_norm_ffn': 'grad_w', 'grad_w_gate': 'grad_w', 'grad_w_up': 'grad_w', 'grad_w_down': 'grad_w', 'grad_norm_final': 'grad_w', 'delta_norm_mix': 'delta_w', 'delta_w_in': 'delta_w', 'delta_conv_w': 'delta_w', 'delta_w_conv_out': 'delta_w', 'delta_w_pool': 'delta_w', 'delta_pool_scale': 'delta_w', 'delta_norm_mem': 'delta_w', 'delta_w_kv': 'delta_w', 'delta_w_xattn_out': 'delta_w', 'delta_w_out': 'delta_w', 'delta_norm_ffn': 'delta_w', 'delta_w_gate': 'delta_w', 'delta_w_up': 'delta_w', 'delta_w_down': 'delta_w', 'delta_norm_final': 'delta_w', 'new_m_norm_mix': 'new_m', 'new_m_w_in': 'new_m', 'new_m_conv_w': 'new_m', 'new_m_w_conv_out': 'new_m', 'new_m_w_pool': 'new_m', 'new_m_pool_scale': 'new_m', 'new_m_norm_mem': 'new_m', 'new_m_w_kv': 'new_m', 'new_m_w_xattn_out': 'new_m', 'new_m_w_out': 'new_m', 'new_m_norm_ffn': 'new_m', 'new_m_w_gate': 'new_m', 'new_m_w_up': 'new_m', 'new_m_w_down': 'new_m', 'new_m_norm_final': 'new_m', 'new_v_norm_mix': 'new_v', 'new_v_w_in': 'new_v', 'new_v_conv_w': 'new_v', 'new_v_w_conv_out': 'new_v', 'new_v_w_pool': 'new_v', 'new_v_pool_scale': 'new_v', 'new_v_norm_mem': 'new_v', 'new_v_w_kv': 'new_v', 'new_v_w_xattn_out': 'new_v', 'new_v_w_out': 'new_v', 'new_v_norm_ffn': 'new_v', 'new_v_w_gate': 'new_v', 'new_v_w_up': 'new_v', 'new_v_w_down': 'new_v', 'new_v_norm_final': 'new_v'}


def _forward(args):
    return _fwd_reference(*[args[k] for k in FWD_PARAMS])


def _output_shape():
    def fwd():
        inp = _fwd_setup_inputs(0)
        return _fwd_reference(*[inp[k] for k in FWD_PARAMS])
    out = _jax.eval_shape(fwd)
    return out.shape, out.dtype

N_MICROBATCH = 1
ADAM_LR = 0.001
ADAM_B1 = 0.9
ADAM_B2 = 0.999
ADAM_EPS = 1e-08
ADAM_WD = 0.01
ADAM_STEP = 10
PER_EXAMPLE_BATCH_AXIS = {'x': 0, 'mem': 0, 'loss_target': 0}
SHARED_INPUTS = []
_WEIGHT_DTYPES = {'norm_mix': _jnp.float32, 'w_in': _jnp.float32, 'conv_w': _jnp.float32, 'w_conv_out': _jnp.float32, 'w_pool': _jnp.float32, 'pool_scale': _jnp.float32, 'norm_mem': _jnp.float32, 'w_kv': _jnp.float32, 'w_xattn_out': _jnp.float32, 'w_out': _jnp.float32, 'norm_ffn': _jnp.float32, 'w_gate': _jnp.float32, 'w_up': _jnp.float32, 'w_down': _jnp.float32, 'norm_final': _jnp.float32}
MOMENT_SCALE = {'norm_mix': 1.755754e-01, 'w_in': 6.165413e-02, 'conv_w': 8.842069e-02, 'w_conv_out': 8.675187e-02, 'w_pool': 7.769635e-02, 'pool_scale': 7.751295e-02, 'norm_mem': 1.357964e-02, 'w_kv': 9.126781e-03, 'w_xattn_out': 9.183902e-03, 'w_out': 1.178372e-01, 'norm_ffn': 1.148013e-01, 'w_gate': 4.874681e-02, 'w_up': 4.760578e-02, 'w_down': 7.935244e-02, 'norm_final': 3.210993e+01}


def _to_microbatches(a, axis):
    t = _jnp.moveaxis(a, axis, 0)
    t = t.reshape((N_MICROBATCH, t.shape[0] // N_MICROBATCH) + t.shape[1:])
    return _jnp.moveaxis(t, 1, axis + 1)


def setup_inputs(seed: int = 0) -> dict:
    inp = _fwd_setup_inputs(seed)
    key = _jax.random.fold_in(_jax.random.key(seed), 7919)
    shape, _ = _output_shape()
    out = dict(inp)
    out["loss_target"] = _jax.random.normal(_jax.random.fold_in(key, 0), shape, _jnp.float32)
    for i, name in enumerate(TWIN_WEIGHTS):
        w = inp[name].astype(_jnp.float32)
        if MOMENT_SCALE is None:
            s = _jnp.sqrt(_jnp.mean(_jnp.square(w)) + 1e-30)
        else:
            s = MOMENT_SCALE[name]
        km, kv = _jax.random.split(_jax.random.fold_in(key, i + 1))
        out[name] = w
        out["m_" + name] = s * _jax.random.normal(km, w.shape, _jnp.float32)
        out["v_" + name] = (s * s) * _jax.random.uniform(kv, w.shape, _jnp.float32, 0.5, 1.5)
    if N_MICROBATCH > 1:
        for name, axis in PER_EXAMPLE_BATCH_AXIS.items():
            out[name] = _to_microbatches(out[name], axis)
    return {'x': out['x'], 'mem': out['mem'], 'norm_mix': out['norm_mix'], 'w_in': out['w_in'], 'conv_w': out['conv_w'], 'w_conv_out': out['w_conv_out'], 'w_pool': out['w_pool'], 'pool_scale': out['pool_scale'], 'norm_mem': out['norm_mem'], 'w_kv': out['w_kv'], 'w_xattn_out': out['w_xattn_out'], 'w_out': out['w_out'], 'norm_ffn': out['norm_ffn'], 'w_gate': out['w_gate'], 'w_up': out['w_up'], 'w_down': out['w_down'], 'norm_final': out['norm_final'], 'loss_target': out['loss_target'], 'm_norm_mix': out['m_norm_mix'], 'm_w_in': out['m_w_in'], 'm_conv_w': out['m_conv_w'], 'm_w_conv_out': out['m_w_conv_out'], 'm_w_pool': out['m_w_pool'], 'm_pool_scale': out['m_pool_scale'], 'm_norm_mem': out['m_norm_mem'], 'm_w_kv': out['m_w_kv'], 'm_w_xattn_out': out['m_w_xattn_out'], 'm_w_out': out['m_w_out'], 'm_norm_ffn': out['m_norm_ffn'], 'm_w_gate': out['m_w_gate'], 'm_w_up': out['m_w_up'], 'm_w_down': out['m_w_down'], 'm_norm_final': out['m_norm_final'], 'v_norm_mix': out['v_norm_mix'], 'v_w_in': out['v_w_in'], 'v_conv_w': out['v_conv_w'], 'v_w_conv_out': out['v_w_conv_out'], 'v_w_pool': out['v_w_pool'], 'v_pool_scale': out['v_pool_scale'], 'v_norm_mem': out['v_norm_mem'], 'v_w_kv': out['v_w_kv'], 'v_w_xattn_out': out['v_w_xattn_out'], 'v_w_out': out['v_w_out'], 'v_norm_ffn': out['v_norm_ffn'], 'v_w_gate': out['v_w_gate'], 'v_w_up': out['v_w_up'], 'v_w_down': out['v_w_down'], 'v_norm_final': out['v_norm_final']}


def _loss(weights, diff, rest, loss_target):
    with _jax.named_scope("forward"):
        args = {**rest, TWIN_DIFF_INPUT: diff, **{k: w.astype(_WEIGHT_DTYPES[k]) for k, w in weights.items()}}
        y = _forward(args)
    with _jax.named_scope("loss_head"):
        err = _jnp.square(y.astype(_jnp.float32) - loss_target)
        return 0.5 * _jnp.sum(_jnp.mean(err, axis=-1)) if err.ndim else 0.5 * err


def _adamw(w, g, m, v):
    m = ADAM_B1 * m + (1.0 - ADAM_B1) * g
    v = ADAM_B2 * v + (1.0 - ADAM_B2) * _jnp.square(g)
    m_hat = m / (1.0 - ADAM_B1 ** ADAM_STEP)
    v_hat = v / (1.0 - ADAM_B2 ** ADAM_STEP)
    delta = -ADAM_LR * (m_hat / (_jnp.sqrt(v_hat) + ADAM_EPS) + ADAM_WD * w)
    return delta, m, v


def reference(x, mem, norm_mix, w_in, conv_w, w_conv_out, w_pool, pool_scale, norm_mem, w_kv, w_xattn_out, w_out, norm_ffn, w_gate, w_up, w_down, norm_final, loss_target, m_norm_mix, m_w_in, m_conv_w, m_w_conv_out, m_w_pool, m_pool_scale, m_norm_mem, m_w_kv, m_w_xattn_out, m_w_out, m_norm_ffn, m_w_gate, m_w_up, m_w_down, m_norm_final, v_norm_mix, v_w_in, v_conv_w, v_w_conv_out, v_w_pool, v_pool_scale, v_norm_mem, v_w_kv, v_w_xattn_out, v_w_out, v_norm_ffn, v_w_gate, v_w_up, v_w_down, v_norm_final):
    given = dict(x=x, mem=mem, norm_mix=norm_mix, w_in=w_in, conv_w=conv_w, w_conv_out=w_conv_out, w_pool=w_pool, pool_scale=pool_scale, norm_mem=norm_mem, w_kv=w_kv, w_xattn_out=w_xattn_out, w_out=w_out, norm_ffn=norm_ffn, w_gate=w_gate, w_up=w_up, w_down=w_down, norm_final=norm_final, loss_target=loss_target, m_norm_mix=m_norm_mix, m_w_in=m_w_in, m_conv_w=m_conv_w, m_w_conv_out=m_w_conv_out, m_w_pool=m_w_pool, m_pool_scale=m_pool_scale, m_norm_mem=m_norm_mem, m_w_kv=m_w_kv, m_w_xattn_out=m_w_xattn_out, m_w_out=m_w_out, m_norm_ffn=m_norm_ffn, m_w_gate=m_w_gate, m_w_up=m_w_up, m_w_down=m_w_down, m_norm_final=m_norm_final, v_norm_mix=v_norm_mix, v_w_in=v_w_in, v_conv_w=v_conv_w, v_w_conv_out=v_w_conv_out, v_w_pool=v_w_pool, v_pool_scale=v_pool_scale, v_norm_mem=v_norm_mem, v_w_kv=v_w_kv, v_w_xattn_out=v_w_xattn_out, v_w_out=v_w_out, v_norm_ffn=v_norm_ffn, v_w_gate=v_w_gate, v_w_up=v_w_up, v_w_down=v_w_down, v_norm_final=v_norm_final)
    weights = {n: given[n] for n in TWIN_WEIGHTS}
    shared = {n: given[n] for n in SHARED_INPUTS}
    per_example = {n: given[n] for n in ['x', 'mem']}
    grad_fn = _jax.value_and_grad(_loss, argnums=(0, 1))

    def one_microbatch(ex, loss_target):
        ex = dict(ex)
        diff = ex.pop(TWIN_DIFF_INPUT)
        return grad_fn(weights, diff, {**shared, **ex}, loss_target)

    if N_MICROBATCH == 1:
        loss, (grad_w, grad_x) = one_microbatch(per_example, given["loss_target"])
    else:
        def body(carry, xs):
            loss_sum, grad_sum = carry
            l_k, (gw_k, gx_k) = one_microbatch(xs[0], xs[1])
            with _jax.named_scope("update"):
                return (loss_sum + l_k, _jax.tree.map(_jnp.add, grad_sum, gw_k)), gx_k

        init = (_jnp.zeros((), _jnp.float32), _jax.tree.map(_jnp.zeros_like, weights))
        (loss, grad_w), grad_x = _jax.lax.scan(body, init, (per_example, given["loss_target"]))
    with _jax.named_scope("update"):
        delta_w, new_m, new_v = {}, {}, {}
        for n in TWIN_WEIGHTS:
            delta_w[n], new_m[n], new_v[n] = _adamw(weights[n], grad_w[n], given["m_" + n], given["v_" + n])
    return (loss, grad_x, *[grad_w[n] for n in TWIN_WEIGHTS], *[delta_w[n] for n in TWIN_WEIGHTS],
            *[new_m[n] for n in TWIN_WEIGHTS], *[new_v[n] for n in TWIN_WEIGHTS])
```

```python
import jax
import jax.numpy as jnp
from jax import lax
from jax.experimental import pallas as pl
from jax.experimental.pallas import tpu as pltpu

F32 = jnp.float32
BF16 = jnp.bfloat16
MESH = pl.DeviceIdType.MESH

D = 1024
N_MEM = 256
HEADS = 4
HEAD_DIM = 256
GROUPS = 4
GROUP_DIM = 256
POOL_WINDOWS = (2, 4, 8, 16)
D_FF = 2816
D_IN = 8192
N_CHIPS = 4
EPS = 1e-6
HALO = 16
ATT_SCALE = HEAD_DIM ** -0.5

ADAM_LR = 0.001
ADAM_B1 = 0.9
ADAM_B2 = 0.999
ADAM_EPS = 1e-08
ADAM_WD = 0.01
ADAM_STEP = 10

VMEM_LIMIT = 56 * 1024 * 1024

O_BA, O_CA, O_UA, O_UP, O_QX, O_GA, O_GP, O_GX = (k * D for k in range(8))

NT_DIMS = (((1,), (1,)), ((), ()))
TN_DIMS = (((0,), (0,)), ((), ()))


def _dot(a, b):
    return jnp.dot(a, b, preferred_element_type=F32)


def _dot_nt(a, b):
    return lax.dot_general(a, b, NT_DIMS, preferred_element_type=F32)


def _dot_tn(a, b):
    return lax.dot_general(a, b, TN_DIMS, preferred_element_type=F32)


def _sigmoid(z):
    return 1.0 / (1.0 + jnp.exp(-z))


def _params(semantics=None):
    return pltpu.CompilerParams(dimension_semantics=semantics, vmem_limit_bytes=VMEM_LIMIT)


def _resident(shape):
    zeros = (0,) * len(shape)
    return pl.BlockSpec(shape, lambda *_: zeros, pipeline_mode=pl.Buffered(1))


def _const(shape):
    zeros = (0,) * len(shape)
    return pl.BlockSpec(shape, lambda *_: zeros)


def _rows(tm, width):
    return pl.BlockSpec((tm, width), lambda i: (i, 0))


def _inv_count(tile, tm, window):
    t = tile * tm + lax.broadcasted_iota(jnp.int32, (tm, 1), 0)
    return 1.0 / jnp.minimum(t + 1, window).astype(F32)


def proj_fwd(x, g_mix, w_in4):
    t_len = x.shape[0]
    tm = min(512, t_len)
    tn = D_IN // N_CHIPS

    def body(x_ref, g_ref, w_ref, proj_ref, h_ref):
        @pl.when(pl.program_id(1) == 0)
        def _():
            xv = x_ref[...]
            r = lax.rsqrt(jnp.mean(xv * xv, axis=-1, keepdims=True) + EPS)
            h_ref[...] = (xv * r * g_ref[...]).astype(BF16)

        proj_ref[...] = _dot(h_ref[...], w_ref[...]).astype(BF16)

    return pl.pallas_call(
        body, name="proj_fwd",
        grid=(t_len // tm, N_CHIPS),
        in_specs=[pl.BlockSpec((tm, D), lambda i, j: (i, 0)),
                  pl.BlockSpec((1, D), lambda i, j: (0, 0)),
                  pl.BlockSpec((None, D, tn), lambda i, j: (j, 0, 0))],
        out_specs=[pl.BlockSpec((tm, tn), lambda i, j: (i, j)),
                   pl.BlockSpec((tm, D), lambda i, j: (i, 0))],
        out_shape=[jax.ShapeDtypeStruct((t_len, D_IN), BF16), jax.ShapeDtypeStruct((t_len, D), BF16)],
        compiler_params=_params(("arbitrary", "arbitrary")),
    )(x, g_mix, w_in4)


def kv_fwd(mem, g_mem, w_kv4):
    half = D // 2

    def body(mem_ref, g_ref, w_ref, memn_ref, k_ref, v_ref):
        mv = mem_ref[...]
        r = lax.rsqrt(jnp.mean(mv * mv, axis=-1, keepdims=True) + EPS)
        mn = (mv * r * g_ref[...]).astype(BF16)
        memn_ref[...] = mn
        k_ref[:, 0:half] = _dot(mn, w_ref[0]).astype(BF16)
        k_ref[:, half:D] = _dot(mn, w_ref[1]).astype(BF16)
        v_ref[:, 0:half] = _dot(mn, w_ref[2]).astype(BF16)
        v_ref[:, half:D] = _dot(mn, w_ref[3]).astype(BF16)

    out = jax.ShapeDtypeStruct((N_MEM, D), BF16)
    return pl.pallas_call(body, name="kv_fwd", out_shape=[out, out, out], compiler_params=_params())(mem, g_mem, w_kv4)


def _softmax_rows(s):
    m = jnp.max(s, axis=-1, keepdims=True)
    e = jnp.exp(s - m)
    return e * (1.0 / jnp.sum(e, axis=-1, keepdims=True))


def mixer_fwd(proj, x, conv_w8, w_co, w_pool, pool_scale, k, v, w_xo, w_out):
    t_len = x.shape[0]
    tm = min(256, t_len)

    def body(proj_ref, x_ref, cw_ref, wco_ref, wpool_ref, ps_ref, k_ref, v_ref, wxo_ref, wout_ref,
             a_ref, pooled_ref, ya_ref, pp_ref, yx_ref, o_ref, x1_ref, cu_ext, up_ext):
        i = pl.program_id(0)

        @pl.when(i == 0)
        def _():
            cu_ext[0:HALO, :] = jnp.zeros((HALO, D), F32)
            up_ext[0:HALO, :] = jnp.zeros((HALO, D), F32)

        cu = proj_ref[:, O_CA:O_CA + D].astype(F32) * proj_ref[:, O_UA:O_UA + D].astype(F32)
        cu_ext[HALO:HALO + tm, :] = cu
        conv = (cw_ref[2:3, :] * cu + cw_ref[1:2, :] * cu_ext[HALO - 1:HALO - 1 + tm, :]
                + cw_ref[0:1, :] * cu_ext[HALO - 2:HALO - 2 + tm, :])
        a = (proj_ref[:, O_BA:O_BA + D].astype(F32) * conv).astype(BF16)
        a_ref[...] = a
        ya = _dot(a, wco_ref[...])
        ya_ref[...] = ya.astype(BF16)

        up_ext[HALO:HALO + tm, :] = proj_ref[:, O_UP:O_UP + D].astype(F32)
        for g, window in enumerate(POOL_WINDOWS):
            cols = slice(g * GROUP_DIM, (g + 1) * GROUP_DIM)
            tok = up_ext[HALO:HALO + tm, cols]
            acc = tok
            for j in range(1, window):
                acc = acc + up_ext[HALO - j:HALO - j + tm, cols]
            pooled = (acc * _inv_count(i, tm, window) - tok).astype(BF16)
            pooled_ref[:, cols] = pooled
            pp_ref[:, cols] = _dot(pooled, wpool_ref[g]).astype(BF16)

        for hd in range(HEADS):
            cols = slice(hd * HEAD_DIM, (hd + 1) * HEAD_DIM)
            q = proj_ref[:, O_QX + hd * HEAD_DIM:O_QX + (hd + 1) * HEAD_DIM]
            p = _softmax_rows(_dot_nt(q, k_ref[:, cols]) * ATT_SCALE)
            o_ref[:, cols] = _dot(p.astype(BF16), v_ref[:, cols]).astype(BF16)
        yx = _dot(o_ref[...], wxo_ref[...])
        yx_ref[...] = yx.astype(BF16)

        merged = (_sigmoid(proj_ref[:, O_GA:O_GA + D].astype(F32)) * ya
                  + _sigmoid(proj_ref[:, O_GP:O_GP + D].astype(F32)) * (pp_ref[...].astype(F32) * ps_ref[...])
                  + _sigmoid(proj_ref[:, O_GX:O_GX + D].astype(F32)) * yx)
        x1_ref[...] = x_ref[...] + _dot(merged.astype(BF16), wout_ref[...])

        cu_ext[0:HALO, :] = cu_ext[tm:tm + HALO, :]
        up_ext[0:HALO, :] = up_ext[tm:tm + HALO, :]

    act = jax.ShapeDtypeStruct((t_len, D), BF16)
    return pl.pallas_call(
        body, name="mixer_fwd",
        grid=(t_len // tm,),
        in_specs=[_rows(tm, D_IN), _rows(tm, D), _resident((8, D)), _resident((D, D)),
                  _resident((GROUPS, GROUP_DIM, GROUP_DIM)), _resident((1, D)),
                  _resident((N_MEM, D)), _resident((N_MEM, D)), _resident((D, D)), _resident((D, D))],
        out_specs=[_rows(tm, D)] * 7,
        out_shape=[act] * 6 + [jax.ShapeDtypeStruct((t_len, D), F32)],
        scratch_shapes=[pltpu.VMEM((tm + HALO, D), F32), pltpu.VMEM((tm + HALO, D), F32)],
        compiler_params=_params(("arbitrary",)),
    )(proj, x, conv_w8, w_co, w_pool, pool_scale, k, v, w_xo, w_out)


def ffn_fwd(x1, target, g_ffn, w_gate, w_up, w_down, g_final):
    t_len = x1.shape[0]
    tm = min(256, t_len)

    def body(x1_ref, tgt_ref, g_ref, wg_ref, wu_ref, wd_ref, gf_ref, gate_ref, up_ref, dx2_ref, stat_ref):
        @pl.when(pl.program_id(0) == 0)
        def _():
            stat_ref[...] = jnp.zeros((8, D), F32)

        x1v = x1_ref[...]
        r2 = lax.rsqrt(jnp.mean(x1v * x1v, axis=-1, keepdims=True) + EPS)
        h2 = (x1v * r2 * g_ref[...]).astype(BF16)
        gate = _dot(h2, wg_ref[...])
        up = _dot(h2, wu_ref[...])
        gate_ref[...] = gate.astype(BF16)
        up_ref[...] = up.astype(BF16)
        act = (gate * _sigmoid(gate) * up).astype(BF16)
        x2 = x1v + _dot(act, wd_ref[...])
        r3 = lax.rsqrt(jnp.mean(x2 * x2, axis=-1, keepdims=True) + EPS)
        xh = x2 * r3
        diff = xh * gf_ref[...] - tgt_ref[...]
        dy = diff * (1.0 / D)
        stat_ref[0:1, :] += jnp.sum(dy * xh, axis=0, keepdims=True)
        stat_ref[1:2, :] += (0.5 / D) * jnp.sum(diff * diff, axis=0, keepdims=True)
        dxh = dy * gf_ref[...]
        dx2_ref[...] = r3 * (dxh - xh * jnp.mean(dxh * xh, axis=-1, keepdims=True))

    return pl.pallas_call(
        body, name="ffn_fwd",
        grid=(t_len // tm,),
        in_specs=[_rows(tm, D), _rows(tm, D), _resident((1, D)), _resident((D, D_FF)), _resident((D, D_FF)),
                  _resident((D_FF, D)), _resident((1, D))],
        out_specs=[_rows(tm, D_FF), _rows(tm, D_FF), _rows(tm, D), _const((8, D))],
        out_shape=[jax.ShapeDtypeStruct((t_len, D_FF), BF16), jax.ShapeDtypeStruct((t_len, D_FF), BF16),
                   jax.ShapeDtypeStruct((t_len, D), F32), jax.ShapeDtypeStruct((8, D), F32)],
        compiler_params=_params(("arbitrary",)),
    )(x1, target, g_ffn, w_gate, w_up, w_down, g_final)


def ffn_bwd(dx2, x1, gate, up, g_ffn, w_gate, w_up, w_down):
    t_len = x1.shape[0]
    tm = min(256, t_len)

    def body(dx2_ref, x1_ref, gate_ref, up_ref, g_ref, wg_ref, wu_ref, wd_ref,
             dx1_ref, dgate_ref, dup_ref, act_ref, h2_ref, stat_ref):
        @pl.when(pl.program_id(0) == 0)
        def _():
            stat_ref[...] = jnp.zeros((8, D), F32)

        dx2v = dx2_ref[...]
        gate = gate_ref[...].astype(F32)
        upv = up_ref[...].astype(F32)
        sg = _sigmoid(gate)
        silu = gate * sg
        act_ref[...] = (silu * upv).astype(BF16)
        dact = _dot_nt(dx2v.astype(BF16), wd_ref[...])
        dup = (dact * silu).astype(BF16)
        dgate = (dact * upv * (sg * (1.0 + gate * (1.0 - sg)))).astype(BF16)
        dup_ref[...] = dup
        dgate_ref[...] = dgate
        dh2 = _dot_nt(dgate, wg_ref[...]) + _dot_nt(dup, wu_ref[...])
        x1v = x1_ref[...]
        r2 = lax.rsqrt(jnp.mean(x1v * x1v, axis=-1, keepdims=True) + EPS)
        xh = x1v * r2
        h2_ref[...] = (xh * g_ref[...]).astype(BF16)
        stat_ref[0:1, :] += jnp.sum(dh2 * xh, axis=0, keepdims=True)
        dxh = dh2 * g_ref[...]
        dx1_ref[...] = dx2v + r2 * (dxh - xh * jnp.mean(dxh * xh, axis=-1, keepdims=True))

    ff = jax.ShapeDtypeStruct((t_len, D_FF), BF16)
    return pl.pallas_call(
        body, name="ffn_bwd",
        grid=(t_len // tm,),
        in_specs=[_rows(tm, D), _rows(tm, D), _rows(tm, D_FF), _rows(tm, D_FF), _resident((1, D)),
                  _resident((D, D_FF)), _resident((D, D_FF)), _resident((D_FF, D))],
        out_specs=[_rows(tm, D), _rows(tm, D_FF), _rows(tm, D_FF), _rows(tm, D_FF), _rows(tm, D), _const((8, D))],
        out_shape=[jax.ShapeDtypeStruct((t_len, D), F32), ff, ff, ff, jax.ShapeDtypeStruct((t_len, D), BF16),
                   jax.ShapeDtypeStruct((8, D), F32)],
        compiler_params=_params(("arbitrary",)),
    )(dx2, x1, gate, up, g_ffn, w_gate, w_up, w_down)


def mixer_bwd(dx1, proj, ya, pp, yx, o, conv_w8, w_co, w_pool, pool_scale, k, v, w_xo, w_out):
    t_len = dx1.shape[0]
    tm = min(256, t_len)
    n_tiles = t_len // tm
    halo_blocks = tm // HALO

    def body(dx1_ref, proj_ref, halo_ref, ya_ref, pp_ref, yx_ref, o_ref,
             cw_ref, wco_ref, wpool_ref, ps_ref, k_ref, v_ref, wxo_ref, wout_ref,
             dproj_ref, merged_ref, dya_ref, dpp_ref, dyx_ref, dk_ref, dv_ref, stat_ref,
             cu_ext, dconv_ext, dpn_ext):
        step = pl.program_id(0)
        tile = n_tiles - 1 - step

        @pl.when(step == 0)
        def _():
            dk_ref[...] = jnp.zeros((N_MEM, D), F32)
            dv_ref[...] = jnp.zeros((N_MEM, D), F32)
            stat_ref[...] = jnp.zeros((8, D), F32)
            dconv_ext[tm:tm + HALO, :] = jnp.zeros((HALO, D), F32)
            dpn_ext[tm:tm + HALO, :] = jnp.zeros((HALO, D), F32)

        dmerged = _dot_nt(dx1_ref[...].astype(BF16), wout_ref[...])
        sa = _sigmoid(proj_ref[:, O_GA:O_GA + D].astype(F32))
        sp = _sigmoid(proj_ref[:, O_GP:O_GP + D].astype(F32))
        sx = _sigmoid(proj_ref[:, O_GX:O_GX + D].astype(F32))
        ya = ya_ref[...].astype(F32)
        ppv = pp_ref[...].astype(F32)
        yp = ppv * ps_ref[...]
        yx = yx_ref[...].astype(F32)
        merged_ref[...] = (sa * ya + sp * yp + sx * yx).astype(BF16)
        dproj_ref[:, O_GA:O_GA + D] = (dmerged * ya * (sa * (1.0 - sa))).astype(BF16)
        dproj_ref[:, O_GP:O_GP + D] = (dmerged * yp * (sp * (1.0 - sp))).astype(BF16)
        dproj_ref[:, O_GX:O_GX + D] = (dmerged * yx * (sx * (1.0 - sx))).astype(BF16)
        dya = (dmerged * sa).astype(BF16)
        dyp = dmerged * sp
        dyx = (dmerged * sx).astype(BF16)
        dya_ref[...] = dya
        dyx_ref[...] = dyx
        stat_ref[1:2, :] += jnp.sum(dyp * ppv, axis=0, keepdims=True)
        dpp = (dyp * ps_ref[...]).astype(BF16)
        dpp_ref[...] = dpp

        da = _dot_nt(dya, wco_ref[...])
        c_a = proj_ref[:, O_CA:O_CA + D].astype(F32)
        u_a = proj_ref[:, O_UA:O_UA + D].astype(F32)
        cu = c_a * u_a
        halo_cu = halo_ref[:, O_CA:O_CA + D].astype(F32) * halo_ref[:, O_UA:O_UA + D].astype(F32)
        cu_ext[0:HALO, :] = jnp.where(tile > 0, halo_cu, 0.0)
        cu_ext[HALO:HALO + tm, :] = cu
        cu1 = cu_ext[HALO - 1:HALO - 1 + tm, :]
        cu2 = cu_ext[HALO - 2:HALO - 2 + tm, :]
        conv = cw_ref[2:3, :] * cu + cw_ref[1:2, :] * cu1 + cw_ref[0:1, :] * cu2
        dproj_ref[:, O_BA:O_BA + D] = (da * conv).astype(BF16)
        dconv = da * proj_ref[:, O_BA:O_BA + D].astype(F32)
        stat_ref[5:6, :] += jnp.sum(dconv * cu2, axis=0, keepdims=True)
        stat_ref[6:7, :] += jnp.sum(dconv * cu1, axis=0, keepdims=True)
        stat_ref[7:8, :] += jnp.sum(dconv * cu, axis=0, keepdims=True)
        dconv_ext[0:tm, :] = dconv
        dcu = (cw_ref[2:3, :] * dconv + cw_ref[1:2, :] * dconv_ext[1:1 + tm, :]
               + cw_ref[0:1, :] * dconv_ext[2:2 + tm, :])
        dproj_ref[:, O_CA:O_CA + D] = (dcu * u_a).astype(BF16)
        dproj_ref[:, O_UA:O_UA + D] = (dcu * c_a).astype(BF16)

        for g, window in enumerate(POOL_WINDOWS):
            cols = slice(g * GROUP_DIM, (g + 1) * GROUP_DIM)
            dpooled = _dot_nt(dpp[:, cols], wpool_ref[g])
            dpn_ext[0:tm, cols] = dpooled * _inv_count(tile, tm, window)
            acc = dpn_ext[0:tm, cols]
            for j in range(1, window):
                acc = acc + dpn_ext[j:j + tm, cols]
            dproj_ref[:, O_UP + g * GROUP_DIM:O_UP + (g + 1) * GROUP_DIM] = (acc - dpooled).astype(BF16)

        do = _dot_nt(dyx, wxo_ref[...])
        for hd in range(HEADS):
            cols = slice(hd * HEAD_DIM, (hd + 1) * HEAD_DIM)
            q = proj_ref[:, O_QX + hd * HEAD_DIM:O_QX + (hd + 1) * HEAD_DIM]
            kh = k_ref[:, cols]
            p = _softmax_rows(_dot_nt(q, kh) * ATT_SCALE)
            doh = do[:, cols].astype(BF16)
            dp = _dot_nt(doh, v_ref[:, cols])
            dv_ref[:, cols] += _dot_tn(p.astype(BF16), doh)
            ds = (p * (dp - jnp.sum(dp * p, axis=-1, keepdims=True)) * ATT_SCALE).astype(BF16)
            dproj_ref[:, O_QX + hd * HEAD_DIM:O_QX + (hd + 1) * HEAD_DIM] = _dot(ds, kh).astype(BF16)
            dk_ref[:, cols] += _dot_tn(ds, q)

        dconv_ext[tm:tm + HALO, :] = dconv_ext[0:HALO, :]
        dpn_ext[tm:tm + HALO, :] = dpn_ext[0:HALO, :]

    def rev(width):
        return pl.BlockSpec((tm, width), lambda s: (n_tiles - 1 - s, 0))

    halo_spec = pl.BlockSpec((HALO, D_IN), lambda s: (jnp.maximum((n_tiles - 1 - s) * halo_blocks - 1, 0), 0))
    act = jax.ShapeDtypeStruct((t_len, D), BF16)
    kv_grad = jax.ShapeDtypeStruct((N_MEM, D), F32)
    return pl.pallas_call(
        body, name="mixer_bwd",
        grid=(n_tiles,),
        in_specs=[rev(D), rev(D_IN), halo_spec, rev(D), rev(D), rev(D), rev(D),
                  _resident((8, D)), _resident((D, D)), _resident((GROUPS, GROUP_DIM, GROUP_DIM)), _resident((1, D)),
                  _resident((N_MEM, D)), _resident((N_MEM, D)), _resident((D, D)), _resident((D, D))],
        out_specs=[rev(D_IN), rev(D), rev(D), rev(D), rev(D),
                   _const((N_MEM, D)), _const((N_MEM, D)), _const((8, D))],
        out_shape=[jax.ShapeDtypeStruct((t_len, D_IN), BF16), act, act, act, act, kv_grad, kv_grad,
                   jax.ShapeDtypeStruct((8, D), F32)],
        scratch_shapes=[pltpu.VMEM((tm + HALO, D), F32)] * 3,
        compiler_params=_params(("arbitrary",)),
    )(dx1, proj, proj, ya, pp, yx, o, conv_w8, w_co, w_pool, pool_scale, k, v, w_xo, w_out)


def in_bwd(dproj, w_in4, x, dx1, g_mix):
    t_len = x.shape[0]
    tm = min(512, t_len)
    tk = D_IN // N_CHIPS

    def body(dproj_ref, w_ref, x_ref, dx1_ref, g_ref, gx_ref, stat_ref, acc_ref):
        i, j = pl.program_id(0), pl.program_id(1)

        @pl.when((i == 0) & (j == 0))
        def _():
            stat_ref[...] = jnp.zeros((8, D), F32)

        @pl.when(j == 0)
        def _():
            acc_ref[...] = jnp.zeros((tm, D), F32)

        acc_ref[...] += _dot_nt(dproj_ref[...], w_ref[...])

        @pl.when(j == N_CHIPS - 1)
        def _():
            dh = acc_ref[...]
            xv = x_ref[...]
            r = lax.rsqrt(jnp.mean(xv * xv, axis=-1, keepdims=True) + EPS)
            xh = xv * r
            stat_ref[0:1, :] += jnp.sum(dh * xh, axis=0, keepdims=True)
            dxh = dh * g_ref[...]
            gx_ref[...] = dx1_ref[...] + r * (dxh - xh * jnp.mean(dxh * xh, axis=-1, keepdims=True))

    return pl.pallas_call(
        body, name="in_bwd",
        grid=(t_len // tm, N_CHIPS),
        in_specs=[pl.BlockSpec((tm, tk), lambda i, j: (i, j)),
                  pl.BlockSpec((None, D, tk), lambda i, j: (j, 0, 0)),
                  pl.BlockSpec((tm, D), lambda i, j: (i, 0)),
                  pl.BlockSpec((tm, D), lambda i, j: (i, 0)),
                  pl.BlockSpec((1, D), lambda i, j: (0, 0))],
        out_specs=[pl.BlockSpec((tm, D), lambda i, j: (i, 0)), pl.BlockSpec((8, D), lambda i, j: (0, 0))],
        out_shape=[jax.ShapeDtypeStruct((t_len, D), F32), jax.ShapeDtypeStruct((8, D), F32)],
        scratch_shapes=[pltpu.VMEM((tm, D), F32)],
        compiler_params=_params(("arbitrary", "arbitrary")),
    )(dproj, w_in4, x, dx1, g_mix)


def kv_bwd(dk, dv, memn, mem, g_mem, w_kv4):
    half = D // 2

    def body(dk_ref, dv_ref, memn_ref, mem_ref, g_ref, w_ref, gw_ref, stat_ref):
        mn = memn_ref[...]
        parts = (dk_ref[:, 0:half], dk_ref[:, half:D], dv_ref[:, 0:half], dv_ref[:, half:D])
        dmemn = jnp.zeros((N_MEM, D), F32)
        for j, part in enumerate(parts):
            part = part.astype(BF16)
            gw_ref[j] = _dot_tn(mn, part)
            dmemn = dmemn + _dot_nt(part, w_ref[j])
        mv = mem_ref[...]
        r = lax.rsqrt(jnp.mean(mv * mv, axis=-1, keepdims=True) + EPS)
        stat_ref[...] = jnp.zeros((8, D), F32)
        stat_ref[0:1, :] = jnp.sum(dmemn * (mv * r), axis=0, keepdims=True)

    return pl.pallas_call(
        body, name="kv_bwd",
        out_shape=[jax.ShapeDtypeStruct((N_CHIPS, D, half), F32), jax.ShapeDtypeStruct((8, D), F32)],
        compiler_params=_params(),
    )(dk, dv, memn, mem, g_mem, w_kv4)


def matmul_tn(name, a, b, tn, col_blocks=1, k_dim=None, n_dim=None, a_col=0, b_col=0):
    t_len = a.shape[0]
    k_dim = a.shape[1] if k_dim is None else k_dim
    n_dim = b.shape[1] if n_dim is None else n_dim
    b_off = b_col * (n_dim // tn)
    tt = min(512, t_len)
    per_block = n_dim // col_blocks // tn

    def body(a_ref, b_ref, out_ref):
        @pl.when(pl.program_id(1) == 0)
        def _():
            out_ref[...] = jnp.zeros((k_dim, tn), F32)

        out_ref[...] += _dot_tn(a_ref[...].astype(BF16), b_ref[...].astype(BF16))

    if col_blocks == 1:
        out_spec = pl.BlockSpec((k_dim, tn), lambda n, t: (0, n))
        out_shape = jax.ShapeDtypeStruct((k_dim, n_dim), F32)
    else:
        out_spec = pl.BlockSpec((None, k_dim, tn), lambda n, t: (n // per_block, 0, n % per_block))
        out_shape = jax.ShapeDtypeStruct((col_blocks, k_dim, n_dim // col_blocks), F32)
    return pl.pallas_call(
        body, name=name,
        grid=(n_dim // tn, t_len // tt),
        in_specs=[pl.BlockSpec((tt, k_dim), lambda n, t: (t, a_col)), pl.BlockSpec((tt, tn), lambda n, t: (t, b_off + n))],
        out_specs=out_spec, out_shape=out_shape,
        compiler_params=_params(("arbitrary", "arbitrary")),
    )(a, b)


def _place():
    x, y, c = lax.axis_index("x"), lax.axis_index("y"), lax.axis_index("c")
    return x, y, c


def _other_chips(x, y):
    return [(1 - x, y), (x, 1 - y), (1 - x, 1 - y)]


def _any_specs(n):
    return [pl.BlockSpec(memory_space=pl.ANY)] * n


def gather_weights(shards):
    n = len(shards)

    def body(*refs):
        ins, outs = refs[:n], refs[n:2 * n]
        send_sems, recv_sems, local_sems = refs[2 * n:]
        x, y, c = _place()
        me = 2 * x + y
        sibling = (x, y, 1 - c)
        chips = _other_chips(x, y)

        def half(ref, which, rows):
            return ref.at[:, pl.ds(which * (rows // 2), rows // 2), :]

        def remote(src, dst, a, slot, to):
            return pltpu.make_async_remote_copy(src_ref=src, dst_ref=dst, send_sem=send_sems.at[a, slot],
                                                recv_sem=recv_sems.at[a, slot], device_id=to, device_id_type=MESH)

        started = []
        for a in range(n):
            rows = ins[a].shape[1]
            own = pltpu.make_async_copy(ins[a], outs[a].at[me], local_sems.at[a])
            own.start()
            started.append(own)
        sends = []
        for a in range(n):
            rows = ins[a].shape[1]
            for slot, (px, py) in enumerate(chips):
                cp = remote(half(ins[a], c, rows), half(outs[a].at[me], c, rows), a, slot, (px, py, c))
                cp.start()
                sends.append(cp)
        for a in range(n):
            rows = ins[a].shape[1]
            for slot, (px, py) in enumerate(chips):
                landed = half(outs[a].at[2 * px + py], c, rows)
                remote(landed, landed, a, slot, (px, py, c)).wait_recv()
                cp = remote(landed, landed, a, 3 + slot, sibling)
                cp.start()
                sends.append(cp)
        for a in range(n):
            rows = ins[a].shape[1]
            for slot, (px, py) in enumerate(chips):
                passed = half(outs[a].at[2 * px + py], 1 - c, rows)
                remote(passed, passed, a, 3 + slot, sibling).wait_recv()
        for cp in sends:
            cp.wait_send()
        for cp in started:
            cp.wait()

    return pl.pallas_call(
        body, name="gather_weights",
        in_specs=_any_specs(n), out_specs=_any_specs(n),
        out_shape=[jax.ShapeDtypeStruct((N_CHIPS,) + s.shape, s.dtype) for s in shards],
        scratch_shapes=[pltpu.SemaphoreType.DMA((n, 6)), pltpu.SemaphoreType.DMA((n, 6)), pltpu.SemaphoreType.DMA((n,))],
    )(*shards)


def exchange_sibling_halves(grads):
    n = len(grads)

    def body(*refs):
        ins, outs = refs[:n], refs[n:2 * n]
        send_sems, recv_sems = refs[2 * n:]
        x, y, c = _place()
        copies = []
        for a in range(n):
            h = ins[a].shape[1] // 2
            cp = pltpu.make_async_remote_copy(
                src_ref=ins[a].at[:, pl.ds((1 - c) * h, h), :], dst_ref=outs[a],
                send_sem=send_sems.at[a], recv_sem=recv_sems.at[a], device_id=(x, y, 1 - c), device_id_type=MESH)
            cp.start()
            copies.append(cp)
        for cp in copies:
            cp.wait()

    return pl.pallas_call(
        body, name="exchange_sibling_halves",
        in_specs=_any_specs(n), out_specs=_any_specs(n),
        out_shape=[jax.ShapeDtypeStruct((N_CHIPS, g.shape[1] // 2, g.shape[2]), F32) for g in grads],
        scratch_shapes=[pltpu.SemaphoreType.DMA((n,)), pltpu.SemaphoreType.DMA((n,))],
    )(*grads)


def _row_tile(rows, cols, budget=1 << 20):
    best = 8
    for tr in range(8, rows + 1, 8):
        if rows % tr == 0 and tr * cols * 4 <= budget:
            best = tr
    return best


def add_sibling_half(name, grad, got, core):
    _, rows, cols = grad.shape
    h = rows // 2
    tr = _row_tile(h, cols)
    per_half = h // tr

    def body(core_ref, g_ref, o_ref, out_ref):
        out_ref[...] = g_ref[...] + o_ref[...]

    return pl.pallas_call(
        body, name=name,
        grid_spec=pltpu.PrefetchScalarGridSpec(
            num_scalar_prefetch=1, grid=(N_CHIPS, per_half),
            in_specs=[pl.BlockSpec((None, tr, cols), lambda j, r, cr: (j, cr[0] * per_half + r, 0)),
                      pl.BlockSpec((None, tr, cols), lambda j, r, cr: (j, r, 0))],
            out_specs=pl.BlockSpec((None, tr, cols), lambda j, r, cr: (j, r, 0))),
        out_shape=jax.ShapeDtypeStruct((N_CHIPS, h, cols), F32),
        compiler_params=_params(("arbitrary", "arbitrary")),
    )(core, grad, got)


def exchange_chip_blocks(partials):
    n = len(partials)

    def body(*refs):
        ins, outs = refs[:n], refs[n:2 * n]
        send_sems, recv_sems = refs[2 * n:]
        x, y, c = _place()
        copies = []
        for a in range(n):
            for slot, (px, py) in enumerate(_other_chips(x, y)):
                cp = pltpu.make_async_remote_copy(
                    src_ref=ins[a].at[2 * px + py], dst_ref=outs[a].at[slot],
                    send_sem=send_sems.at[a, slot], recv_sem=recv_sems.at[a, slot],
                    device_id=(px, py, c), device_id_type=MESH)
                cp.start()
                copies.append(cp)
        for cp in copies:
            cp.wait()

    return pl.pallas_call(
        body, name="exchange_chip_blocks",
        in_specs=_any_specs(n), out_specs=_any_specs(n),
        out_shape=[jax.ShapeDtypeStruct((3,) + p.shape[1:], F32) for p in partials],
        scratch_shapes=[pltpu.SemaphoreType.DMA((n, 3)), pltpu.SemaphoreType.DMA((n, 3))],
    )(*partials)


def add_chip_blocks(name, partial, got, chip):
    _, h, cols = partial.shape
    tr = _row_tile(h, cols)

    def body(chip_ref, p_ref, g0_ref, g1_ref, g2_ref, out_ref):
        out_ref[...] = ((p_ref[...] + g0_ref[...]) + g1_ref[...]) + g2_ref[...]

    def got_spec(slot):
        return pl.BlockSpec((None, tr, cols), lambda r, ch: (slot, r, 0))

    return pl.pallas_call(
        body, name=name,
        grid_spec=pltpu.PrefetchScalarGridSpec(
            num_scalar_prefetch=1, grid=(h // tr,),
            in_specs=[pl.BlockSpec((None, tr, cols), lambda r, ch: (ch[0], r, 0)), got_spec(0), got_spec(1), got_spec(2)],
            out_specs=pl.BlockSpec((tr, cols), lambda r, ch: (r, 0))),
        out_shape=jax.ShapeDtypeStruct((h, cols), F32),
        compiler_params=_params(("arbitrary",)),
    )(chip, partial, got, got, got)


def join_sibling_halves(halves):
    n = len(halves)

    def body(*refs):
        ins, outs = refs[:n], refs[n:2 * n]
        send_sems, recv_sems, local_sems = refs[2 * n:]
        x, y, c = _place()
        copies = []
        for a in range(n):
            h = ins[a].shape[0]
            mine = outs[a].at[pl.ds(c * h, h), :]
            own = pltpu.make_async_copy(ins[a], mine, local_sems.at[a])
            own.start()
            cp = pltpu.make_async_remote_copy(
                src_ref=ins[a], dst_ref=mine, send_sem=send_sems.at[a], recv_sem=recv_sems.at[a],
                device_id=(x, y, 1 - c), device_id_type=MESH)
            cp.start()
            copies.append((own, cp))
        for a, (own, cp) in enumerate(copies):
            h = ins[a].shape[0]
            theirs = outs[a].at[pl.ds((1 - c) * h, h), :]
            cp.wait_send()
            pltpu.make_async_remote_copy(
                src_ref=ins[a], dst_ref=theirs, send_sem=send_sems.at[a], recv_sem=recv_sems.at[a],
                device_id=(x, y, 1 - c), device_id_type=MESH).wait_recv()
            own.wait()

    return pl.pallas_call(
        body, name="join_sibling_halves",
        in_specs=_any_specs(n), out_specs=_any_specs(n),
        out_shape=[jax.ShapeDtypeStruct((2 * v.shape[0], v.shape[1]), F32) for v in halves],
        scratch_shapes=[pltpu.SemaphoreType.DMA((n,)), pltpu.SemaphoreType.DMA((n,)), pltpu.SemaphoreType.DMA((n,))],
    )(*halves)


def all_reduce_small(pack):
    rows = pack.shape[0]

    def body(pack_ref, out_ref, gathered, send_sems, recv_sems):
        x, y, c = _place()
        me = 4 * x + 2 * y + c
        gathered[me] = pack_ref[...]
        copies = []
        for rel in range(1, 8):
            fx, fy, fc = (rel >> 2) & 1, (rel >> 1) & 1, rel & 1
            peer = (x ^ fx, y ^ fy, c ^ fc)
            cp = pltpu.make_async_remote_copy(
                src_ref=pack_ref, dst_ref=gathered.at[me], send_sem=send_sems.at[rel - 1], recv_sem=recv_sems.at[rel - 1],
                device_id=peer, device_id_type=MESH)
            cp.start()
            copies.append(cp)
        for rel in range(1, 8):
            fx, fy, fc = (rel >> 2) & 1, (rel >> 1) & 1, rel & 1
            src = 4 * (x ^ fx) + 2 * (y ^ fy) + (c ^ fc)
            pltpu.make_async_remote_copy(
                src_ref=pack_ref, dst_ref=gathered.at[src], send_sem=send_sems.at[rel - 1], recv_sem=recv_sems.at[rel - 1],
                device_id=(x, y, c), device_id_type=MESH).wait_recv()
        for cp in copies:
            cp.wait_send()
        total = gathered[0]
        for dev in range(1, 8):
            total = total + gathered[dev]
        out_ref[...] = total

    return pl.pallas_call(
        body, name="all_reduce_small",
        in_specs=[pl.BlockSpec(memory_space=pltpu.VMEM)], out_specs=pl.BlockSpec(memory_space=pltpu.VMEM),
        out_shape=jax.ShapeDtypeStruct((rows, D), F32),
        scratch_shapes=[pltpu.VMEM((8, rows, D), F32), pltpu.SemaphoreType.DMA((7,)), pltpu.SemaphoreType.DMA((7,))],
    )(pack)


def adamw(name, w, g, m, v):
    rows, cols = w.shape
    tr = _row_tile(rows, cols) if rows % 8 == 0 else rows

    def body(w_ref, g_ref, m_ref, v_ref, d_ref, nm_ref, nv_ref):
        gv = g_ref[...]
        nm = ADAM_B1 * m_ref[...] + (1.0 - ADAM_B1) * gv
        nv = ADAM_B2 * v_ref[...] + (1.0 - ADAM_B2) * (gv * gv)
        m_hat = nm / (1.0 - ADAM_B1 ** ADAM_STEP)
        v_hat = nv / (1.0 - ADAM_B2 ** ADAM_STEP)
        d_ref[...] = -ADAM_LR * (m_hat / (jnp.sqrt(v_hat) + ADAM_EPS) + ADAM_WD * w_ref[...])
        nm_ref[...] = nm
        nv_ref[...] = nv

    spec = pl.BlockSpec((tr, cols), lambda r: (r, 0))
    out = jax.ShapeDtypeStruct((rows, cols), F32)
    return pl.pallas_call(
        body, name=name, grid=(rows // tr,), in_specs=[spec] * 4, out_specs=[spec] * 3, out_shape=[out] * 3,
        compiler_params=_params(("arbitrary",)),
    )(w, g, m, v)


BIG = ("w_in", "w_conv_out", "w_pool", "w_kv", "w_xattn_out", "w_out", "w_gate", "w_up", "w_down")


def kernel(x, mem, norm_mix, w_in, conv_w, w_conv_out, w_pool, pool_scale, norm_mem, w_kv, w_xattn_out, w_out, norm_ffn, w_gate, w_up, w_down, norm_final, loss_target, m_norm_mix, m_w_in, m_conv_w, m_w_conv_out, m_w_pool, m_pool_scale, m_norm_mem, m_w_kv, m_w_xattn_out, m_w_out, m_norm_ffn, m_w_gate, m_w_up, m_w_down, m_norm_final, v_norm_mix, v_w_in, v_conv_w, v_w_conv_out, v_w_pool, v_pool_scale, v_norm_mem, v_w_kv, v_w_xattn_out, v_w_out, v_norm_ffn, v_w_gate, v_w_up, v_w_down, v_norm_final):
    t_len = x.shape[1]
    xi, yi, ci = lax.axis_index("x"), lax.axis_index("y"), lax.axis_index("c")
    chip = 2 * xi + yi
    chip_arr = jnp.reshape(chip, (1,)).astype(jnp.int32)
    core_arr = jnp.reshape(ci, (1,)).astype(jnp.int32)

    conv_pad = jnp.concatenate([conv_w, jnp.zeros((1, 13, 256), F32)], axis=1)
    shards = [
        w_in.astype(BF16),
        jnp.concatenate([w_conv_out, w_xattn_out, w_out], axis=0).astype(BF16),
        w_pool[0].astype(BF16),
        w_kv.astype(BF16),
        jnp.concatenate([w_gate, w_up], axis=0).astype(BF16),
        w_down.astype(BF16),
        conv_pad,
    ]
    g_in, g_trio, g_pool, g_kv, g_gu, g_down, g_conv_w = gather_weights(shards)
    w_in4 = g_in[:, 0]
    w_co_f = g_trio[:, 0].reshape(D, D)
    w_xo_f = g_trio[:, 1].reshape(D, D)
    w_out_f = g_trio[:, 2].reshape(D, D)
    w_pool_f = jnp.transpose(g_pool, (1, 0, 2, 3)).reshape(GROUPS, GROUP_DIM, GROUP_DIM)
    w_kv4 = g_kv[:, 0]
    w_gate_f = jnp.transpose(g_gu[:, 0], (1, 0, 2)).reshape(D, D_FF)
    w_up_f = jnp.transpose(g_gu[:, 1], (1, 0, 2)).reshape(D, D_FF)
    w_down_f = g_down[:, 0].reshape(D_FF, D)
    conv_full = jnp.transpose(g_conv_w[:, 0, 0:8, :], (1, 0, 2)).reshape(8, D)

    x2d = x[0]
    tgt = loss_target[0]
    proj, h = proj_fwd(x2d, norm_mix, w_in4)
    memn, k, v = kv_fwd(mem[0], norm_mem, w_kv4)
    a, pooled, ya, pp, yx, o, x1 = mixer_fwd(proj, x2d, conv_full, w_co_f, w_pool_f, pool_scale, k, v, w_xo_f, w_out_f)
    gate, up, dx2, stat_f = ffn_fwd(x1, tgt, norm_ffn, w_gate_f, w_up_f, w_down_f, norm_final.reshape(1, D))

    dx1, dgate, dup, act, h2, stat_b1 = ffn_bwd(dx2, x1, gate, up, norm_ffn, w_gate_f, w_up_f, w_down_f)
    dproj, merged, dya, dpp, dyx, dk, dv, stat_b2 = mixer_bwd(
        dx1, proj, ya, pp, yx, o, conv_full, w_co_f, w_pool_f, pool_scale, k, v, w_xo_f, w_out_f)
    grad_x, stat_b3 = in_bwd(dproj, w_in4, x2d, dx1, norm_mix)
    gw_kv, stat_kv = kv_bwd(dk, dv, memn, mem[0], norm_mem, w_kv4)

    gw_in = matmul_tn("grad_w_in", h, dproj, 2048, col_blocks=N_CHIPS)
    gw_co = matmul_tn("grad_w_conv_out", a, dya, 1024)
    gw_xo = matmul_tn("grad_w_xattn_out", o, dyx, 1024)
    gw_out = matmul_tn("grad_w_out", merged, dx1, 1024)
    gw_gate = matmul_tn("grad_w_gate", h2, dgate, D_FF // 2)
    gw_up = matmul_tn("grad_w_up", h2, dup, D_FF // 2)
    gw_down = matmul_tn("grad_w_down", act, dx2, 512)
    gw_pool = jnp.stack([matmul_tn("grad_w_pool_%d" % g, pooled, dpp, GROUP_DIM, k_dim=GROUP_DIM, n_dim=GROUP_DIM,
                                   a_col=g, b_col=g) for g in range(GROUPS)])

    def by_chip_cols(gw):
        return jnp.transpose(gw.reshape(D, N_CHIPS, D_FF // N_CHIPS), (1, 0, 2))

    full = {
        "w_in": gw_in,
        "w_conv_out": gw_co.reshape(N_CHIPS, D // N_CHIPS, D),
        "w_pool": jnp.transpose(gw_pool.reshape(GROUPS, N_CHIPS, 64, GROUP_DIM), (1, 0, 2, 3)).reshape(N_CHIPS, 256, GROUP_DIM),
        "w_kv": gw_kv,
        "w_xattn_out": gw_xo.reshape(N_CHIPS, D // N_CHIPS, D),
        "w_out": gw_out.reshape(N_CHIPS, D // N_CHIPS, D),
        "w_gate": by_chip_cols(gw_gate),
        "w_up": by_chip_cols(gw_up),
        "w_down": gw_down.reshape(N_CHIPS, D_FF // N_CHIPS, D),
    }

    grads = [full[n] for n in BIG]
    got = exchange_sibling_halves(grads)
    partials = [add_sibling_half("add_sibling_" + n, g, o_, core_arr) for n, g, o_ in zip(BIG, grads, got)]
    got2 = exchange_chip_blocks(partials)
    halves = [add_chip_blocks("add_chips_" + n, p, g2, chip_arr) for n, p, g2 in zip(BIG, partials, got2)]
    reduced = dict(zip(BIG, join_sibling_halves(halves)))

    pack = jnp.zeros((16, D), F32)
    pack = pack.at[0].set(stat_b3[0]).at[1].set(stat_b2[1]).at[2].set(stat_kv[0]).at[3].set(stat_b1[0]).at[4].set(stat_f[0])
    pack = pack.at[5:8].set(stat_b2[5:8]).at[8].set(stat_f[1])
    total = all_reduce_small(pack)
    loss = jnp.sum(total[8])
    g_conv_full = total[5:8]
    g_conv = lax.dynamic_slice_in_dim(g_conv_full, chip * 256, 256, axis=1)

    shard_shape = {n: s.shape for n, s in (("w_in", w_in), ("w_conv_out", w_conv_out), ("w_pool", w_pool), ("w_kv", w_kv),
                                           ("w_xattn_out", w_xattn_out), ("w_out", w_out), ("w_gate", w_gate),
                                           ("w_up", w_up), ("w_down", w_down))}
    given = dict(w_in=(w_in, m_w_in, v_w_in), w_conv_out=(w_conv_out, m_w_conv_out, v_w_conv_out),
                 w_pool=(w_pool, m_w_pool, v_w_pool), w_kv=(w_kv, m_w_kv, v_w_kv),
                 w_xattn_out=(w_xattn_out, m_w_xattn_out, v_w_xattn_out), w_out=(w_out, m_w_out, v_w_out),
                 w_gate=(w_gate, m_w_gate, v_w_gate), w_up=(w_up, m_w_up, v_w_up), w_down=(w_down, m_w_down, v_w_down))
    out_g, out_d, out_m, out_v = {}, {}, {}, {}
    for n in BIG:
        g2d = reduced[n]
        w_, m_, v_ = (t.reshape(g2d.shape) for t in given[n])
        d_, nm_, nv_ = adamw("adamw_" + n, w_, g2d, m_, v_)
        out_g[n], out_d[n], out_m[n], out_v[n] = (t.reshape(shard_shape[n]) for t in (g2d, d_, nm_, nv_))

    def small_pack(vals, conv_part):
        p = jnp.zeros((8, D), F32)
        for r, val in enumerate(vals):
            p = p.at[r].set(val.reshape(D))
        return p.at[5:8, 0:256].set(conv_part.reshape(3, 256))

    sw = small_pack([norm_mix, pool_scale, norm_mem, norm_ffn, norm_final], conv_w)
    sm = small_pack([m_norm_mix, m_pool_scale, m_norm_mem, m_norm_ffn, m_norm_final], m_conv_w)
    sv = small_pack([v_norm_mix, v_pool_scale, v_norm_mem, v_norm_ffn, v_norm_final], v_conv_w)
    sg = jnp.zeros((8, D), F32).at[0:5].set(total[0:5]).at[5:8, 0:256].set(g_conv)
    sd, snm, snv = adamw("adamw_small", sw, sg, sm, sv)
    small_names = ("norm_mix", "pool_scale", "norm_mem", "norm_ffn", "norm_final")
    small_shapes = dict(norm_mix=(1, D), pool_scale=(1, D), norm_mem=(1, D), norm_ffn=(1, D), norm_final=(D,))
    for r, n in enumerate(small_names):
        out_g[n], out_d[n], out_m[n], out_v[n] = (t[r].reshape(small_shapes[n]) for t in (sg, sd, snm, snv))
    out_g["conv_w"], out_d["conv_w"], out_m["conv_w"], out_v["conv_w"] = (
        t[5:8, 0:256].reshape(1, 3, 256) for t in (sg, sd, snm, snv))

    order = ("norm_mix", "w_in", "conv_w", "w_conv_out", "w_pool", "pool_scale", "norm_mem", "w_kv", "w_xattn_out",
             "w_out", "norm_ffn", "w_gate", "w_up", "w_down", "norm_final")
    return (loss, grad_x.reshape(1, t_len, D), *[out_g[n] for n in order], *[out_d[n] for n in order],
            *[out_m[n] for n in order], *[out_v[n] for n in order])
```

```python
import jax
import jax.numpy as jnp
from jax import lax
from jax.experimental import pallas as pl
from jax.experimental.pallas import tpu as pltpu

F32 = jnp.float32
BF16 = jnp.bfloat16
MESH = pl.DeviceIdType.MESH

D = 1024
N_MEM = 256
HEADS = 4
HEAD_DIM = 256
GROUPS = 4
GROUP_DIM = 256
POOL_WINDOWS = (2, 4, 8, 16)
D_FF = 2816
D_IN = 8192
N_CHIPS = 4
EPS = 1e-6
HALO = 16
ATT_SCALE = HEAD_DIM ** -0.5

ADAM_LR = 0.001
ADAM_B1 = 0.9
ADAM_B2 = 0.999
ADAM_EPS = 1e-08
ADAM_WD = 0.01
ADAM_STEP = 10

VMEM_LIMIT = 56 * 1024 * 1024

O_BA, O_CA, O_UA, O_UP, O_QX, O_GA, O_GP, O_GX = (k * D for k in range(8))

NT_DIMS = (((1,), (1,)), ((), ()))
TN_DIMS = (((0,), (0,)), ((), ()))


def _dot(a, b):
    return jnp.dot(a, b, preferred_element_type=F32)


def _dot_nt(a, b):
    return lax.dot_general(a, b, NT_DIMS, preferred_element_type=F32)


def _dot_tn(a, b):
    return lax.dot_general(a, b, TN_DIMS, preferred_element_type=F32)


def _sigmoid(z):
    return 1.0 / (1.0 + jnp.exp(-z))


def _params(semantics=None):
    return pltpu.CompilerParams(dimension_semantics=semantics, vmem_limit_bytes=VMEM_LIMIT)


def _resident(shape):
    zeros = (0,) * len(shape)
    return pl.BlockSpec(shape, lambda *_: zeros, pipeline_mode=pl.Buffered(1))


def _const(shape):
    zeros = (0,) * len(shape)
    return pl.BlockSpec(shape, lambda *_: zeros)


def _rows(tm, width):
    return pl.BlockSpec((tm, width), lambda i: (i, 0))


def _inv_count(tile, tm, window):
    t = tile * tm + lax.broadcasted_iota(jnp.int32, (tm, 1), 0)
    return 1.0 / jnp.minimum(t + 1, window).astype(F32)


def proj_fwd(x, g_mix, w_in4):
    t_len = x.shape[0]
    tm = min(512, t_len)
    tn = D_IN // N_CHIPS

    def body(x_ref, g_ref, w_ref, proj_ref, h_ref):
        @pl.when(pl.program_id(1) == 0)
        def _():
            xv = x_ref[...]
            r = lax.rsqrt(jnp.mean(xv * xv, axis=-1, keepdims=True) + EPS)
            h_ref[...] = (xv * r * g_ref[...]).astype(BF16)

        proj_ref[...] = _dot(h_ref[...], w_ref[...]).astype(BF16)

    return pl.pallas_call(
        body, name="proj_fwd",
        grid=(t_len // tm, N_CHIPS),
        in_specs=[pl.BlockSpec((tm, D), lambda i, j: (i, 0)),
                  pl.BlockSpec((1, D), lambda i, j: (0, 0)),
                  pl.BlockSpec((None, D, tn), lambda i, j: (j, 0, 0))],
        out_specs=[pl.BlockSpec((tm, tn), lambda i, j: (i, j)),
                   pl.BlockSpec((tm, D), lambda i, j: (i, 0))],
        out_shape=[jax.ShapeDtypeStruct((t_len, D_IN), BF16), jax.ShapeDtypeStruct((t_len, D), BF16)],
        compiler_params=_params(("arbitrary", "arbitrary")),
    )(x, g_mix, w_in4)


def kv_fwd(mem, g_mem, w_kv4):
    half = D // 2

    def body(mem_ref, g_ref, w_ref, memn_ref, k_ref, v_ref):
        mv = mem_ref[...]
        r = lax.rsqrt(jnp.mean(mv * mv, axis=-1, keepdims=True) + EPS)
        mn = (mv * r * g_ref[...]).astype(BF16)
        memn_ref[...] = mn
        k_ref[:, 0:half] = _dot(mn, w_ref[0]).astype(BF16)
        k_ref[:, half:D] = _dot(mn, w_ref[1]).astype(BF16)
        v_ref[:, 0:half] = _dot(mn, w_ref[2]).astype(BF16)
        v_ref[:, half:D] = _dot(mn, w_ref[3]).astype(BF16)

    out = jax.ShapeDtypeStruct((N_MEM, D), BF16)
    return pl.pallas_call(body, name="kv_fwd", out_shape=[out, out, out], compiler_params=_params())(mem, g_mem, w_kv4)


def _softmax_rows(s):
    m = jnp.max(s, axis=-1, keepdims=True)
    e = jnp.exp(s - m)
    return e * (1.0 / jnp.sum(e, axis=-1, keepdims=True))


def mixer_fwd(proj, x, conv_w8, w_co, w_pool, pool_scale, k, v, w_xo, w_out):
    t_len = x.shape[0]
    tm = min(256, t_len)

    def body(proj_ref, x_ref, cw_ref, wco_ref, wpool_ref, ps_ref, k_ref, v_ref, wxo_ref, wout_ref,
             a_ref, pooled_ref, ya_ref, pp_ref, yx_ref, o_ref, x1_ref, cu_ext, up_ext):
        i = pl.program_id(0)

        @pl.when(i == 0)
        def _():
            cu_ext[0:HALO, :] = jnp.zeros((HALO, D), F32)
            up_ext[0:HALO, :] = jnp.zeros((HALO, D), F32)

        cu = proj_ref[:, O_CA:O_CA + D].astype(F32) * proj_ref[:, O_UA:O_UA + D].astype(F32)
        cu_ext[HALO:HALO + tm, :] = cu
        conv = (cw_ref[2:3, :] * cu + cw_ref[1:2, :] * cu_ext[HALO - 1:HALO - 1 + tm, :]
                + cw_ref[0:1, :] * cu_ext[HALO - 2:HALO - 2 + tm, :])
        a = (proj_ref[:, O_BA:O_BA + D].astype(F32) * conv).astype(BF16)
        a_ref[...] = a
        ya = _dot(a, wco_ref[...])
        ya_ref[...] = ya.astype(BF16)

        up_ext[HALO:HALO + tm, :] = proj_ref[:, O_UP:O_UP + D].astype(F32)
        for g, window in enumerate(POOL_WINDOWS):
            cols = slice(g * GROUP_DIM, (g + 1) * GROUP_DIM)
            tok = up_ext[HALO:HALO + tm, cols]
            acc = tok
            for j in range(1, window):
                acc = acc + up_ext[HALO - j:HALO - j + tm, cols]
            pooled = (acc * _inv_count(i, tm, window) - tok).astype(BF16)
            pooled_ref[:, cols] = pooled
            pp_ref[:, cols] = _dot(pooled, wpool_ref[g]).astype(BF16)

        for hd in range(HEADS):
            cols = slice(hd * HEAD_DIM, (hd + 1) * HEAD_DIM)
            q = proj_ref[:, O_QX + hd * HEAD_DIM:O_QX + (hd + 1) * HEAD_DIM]
            p = _softmax_rows(_dot_nt(q, k_ref[:, cols]) * ATT_SCALE)
            o_ref[:, cols] = _dot(p.astype(BF16), v_ref[:, cols]).astype(BF16)
        yx = _dot(o_ref[...], wxo_ref[...])
        yx_ref[...] = yx.astype(BF16)

        merged = (_sigmoid(proj_ref[:, O_GA:O_GA + D].astype(F32)) * ya
                  + _sigmoid(proj_ref[:, O_GP:O_GP + D].astype(F32)) * (pp_ref[...].astype(F32) * ps_ref[...])
                  + _sigmoid(proj_ref[:, O_GX:O_GX + D].astype(F32)) * yx)
        x1_ref[...] = x_ref[...] + _dot(merged.astype(BF16), wout_ref[...])

        cu_ext[0:HALO, :] = cu_ext[tm:tm + HALO, :]
        up_ext[0:HALO, :] = up_ext[tm:tm + HALO, :]

    act = jax.ShapeDtypeStruct((t_len, D), BF16)
    return pl.pallas_call(
        body, name="mixer_fwd",
        grid=(t_len // tm,),
        in_specs=[_rows(tm, D_IN), _rows(tm, D), _resident((8, D)), _resident((D, D)),
                  _resident((GROUPS, GROUP_DIM, GROUP_DIM)), _resident((1, D)),
                  _resident((N_MEM, D)), _resident((N_MEM, D)), _resident((D, D)), _resident((D, D))],
        out_specs=[_rows(tm, D)] * 7,
        out_shape=[act] * 6 + [jax.ShapeDtypeStruct((t_len, D), F32)],
        scratch_shapes=[pltpu.VMEM((tm + HALO, D), F32), pltpu.VMEM((tm + HALO, D), F32)],
        compiler_params=_params(("arbitrary",)),
    )(proj, x, conv_w8, w_co, w_pool, pool_scale, k, v, w_xo, w_out)


def ffn_fwd(x1, target, g_ffn, w_gate, w_up, w_down, g_final):
    t_len = x1.shape[0]
    tm = min(256, t_len)

    def body(x1_ref, tgt_ref, g_ref, wg_ref, wu_ref, wd_ref, gf_ref, gate_ref, up_ref, dx2_ref, stat_ref):
        @pl.when(pl.program_id(0) == 0)
        def _():
            stat_ref[...] = jnp.zeros((8, D), F32)

        x1v = x1_ref[...]
        r2 = lax.rsqrt(jnp.mean(x1v * x1v, axis=-1, keepdims=True) + EPS)
        h2 = (x1v * r2 * g_ref[...]).astype(BF16)
        gate = _dot_nt(h2, wg_ref[...])
        up = _dot_nt(h2, wu_ref[...])
        gate_ref[...] = gate.astype(BF16)
        up_ref[...] = up.astype(BF16)
        act = (gate * _sigmoid(gate) * up).astype(BF16)
        x2 = x1v + _dot(act, wd_ref[...])
        r3 = lax.rsqrt(jnp.mean(x2 * x2, axis=-1, keepdims=True) + EPS)
        xh = x2 * r3
        diff = xh * gf_ref[...] - tgt_ref[...]
        dy = diff * (1.0 / D)
        stat_ref[0:1, :] += jnp.sum(dy * xh, axis=0, keepdims=True)
        stat_ref[1:2, :] += (0.5 / D) * jnp.sum(diff * diff, axis=0, keepdims=True)
        dxh = dy * gf_ref[...]
        dx2_ref[...] = r3 * (dxh - xh * jnp.mean(dxh * xh, axis=-1, keepdims=True))

    return pl.pallas_call(
        body, name="ffn_fwd",
        grid=(t_len // tm,),
        in_specs=[_rows(tm, D), _rows(tm, D), _resident((1, D)), _resident((D_FF, D)), _resident((D_FF, D)),
                  _resident((D_FF, D)), _resident((1, D))],
        out_specs=[_rows(tm, D_FF), _rows(tm, D_FF), _rows(tm, D), _const((8, D))],
        out_shape=[jax.ShapeDtypeStruct((t_len, D_FF), BF16), jax.ShapeDtypeStruct((t_len, D_FF), BF16),
                   jax.ShapeDtypeStruct((t_len, D), F32), jax.ShapeDtypeStruct((8, D), F32)],
        compiler_params=_params(("arbitrary",)),
    )(x1, target, g_ffn, w_gate, w_up, w_down, g_final)


def ffn_bwd(dx2, x1, gate, up, g_ffn, w_gate, w_up, w_down):
    t_len = x1.shape[0]
    tm = min(256, t_len)

    def body(dx2_ref, x1_ref, gate_ref, up_ref, g_ref, wg_ref, wu_ref, wd_ref,
             dx1_ref, dgate_ref, dup_ref, act_ref, h2_ref, stat_ref):
        @pl.when(pl.program_id(0) == 0)
        def _():
            stat_ref[...] = jnp.zeros((8, D), F32)

        dx2v = dx2_ref[...]
        gate = gate_ref[...].astype(F32)
        upv = up_ref[...].astype(F32)
        sg = _sigmoid(gate)
        silu = gate * sg
        act_ref[...] = (silu * upv).astype(BF16)
        dact = _dot_nt(dx2v.astype(BF16), wd_ref[...])
        dup = (dact * silu).astype(BF16)
        dgate = (dact * upv * (sg * (1.0 + gate * (1.0 - sg)))).astype(BF16)
        dup_ref[...] = dup
        dgate_ref[...] = dgate
        dh2 = _dot(dgate, wg_ref[...]) + _dot(dup, wu_ref[...])
        x1v = x1_ref[...]
        r2 = lax.rsqrt(jnp.mean(x1v * x1v, axis=-1, keepdims=True) + EPS)
        xh = x1v * r2
        h2_ref[...] = (xh * g_ref[...]).astype(BF16)
        stat_ref[0:1, :] += jnp.sum(dh2 * xh, axis=0, keepdims=True)
        dxh = dh2 * g_ref[...]
        dx1_ref[...] = dx2v + r2 * (dxh - xh * jnp.mean(dxh * xh, axis=-1, keepdims=True))

    ff = jax.ShapeDtypeStruct((t_len, D_FF), BF16)
    return pl.pallas_call(
        body, name="ffn_bwd",
        grid=(t_len // tm,),
        in_specs=[_rows(tm, D), _rows(tm, D), _rows(tm, D_FF), _rows(tm, D_FF), _resident((1, D)),
                  _resident((D_FF, D)), _resident((D_FF, D)), _resident((D_FF, D))],
        out_specs=[_rows(tm, D), _rows(tm, D_FF), _rows(tm, D_FF), _rows(tm, D_FF), _rows(tm, D), _const((8, D))],
        out_shape=[jax.ShapeDtypeStruct((t_len, D), F32), ff, ff, ff, jax.ShapeDtypeStruct((t_len, D), BF16),
                   jax.ShapeDtypeStruct((8, D), F32)],
        compiler_params=_params(("arbitrary",)),
    )(dx2, x1, gate, up, g_ffn, w_gate, w_up, w_down)


def mixer_bwd(dx1, proj, ya, pp, yx, o, conv_w8, w_co, w_pool, pool_scale, k, v, w_xo, w_out):
    t_len = dx1.shape[0]
    tm = min(256, t_len)
    n_tiles = t_len // tm
    halo_blocks = tm // HALO

    def body(dx1_ref, proj_ref, halo_ref, ya_ref, pp_ref, yx_ref, o_ref,
             cw_ref, wco_ref, wpool_ref, ps_ref, k_ref, v_ref, wxo_ref, wout_ref,
             dproj_ref, merged_ref, dya_ref, dpp_ref, dyx_ref, dk_ref, dv_ref, stat_ref,
             cu_ext, dconv_ext, dpn_ext):
        step = pl.program_id(0)
        tile = n_tiles - 1 - step

        @pl.when(step == 0)
        def _():
            dk_ref[...] = jnp.zeros((N_MEM, D), F32)
            dv_ref[...] = jnp.zeros((N_MEM, D), F32)
            stat_ref[...] = jnp.zeros((8, D), F32)
            dconv_ext[tm:tm + HALO, :] = jnp.zeros((HALO, D), F32)
            dpn_ext[tm:tm + HALO, :] = jnp.zeros((HALO, D), F32)

        dmerged = _dot_nt(dx1_ref[...].astype(BF16), wout_ref[...])
        sa = _sigmoid(proj_ref[:, O_GA:O_GA + D].astype(F32))
        sp = _sigmoid(proj_ref[:, O_GP:O_GP + D].astype(F32))
        sx = _sigmoid(proj_ref[:, O_GX:O_GX + D].astype(F32))
        ya = ya_ref[...].astype(F32)
        ppv = pp_ref[...].astype(F32)
        yp = ppv * ps_ref[...]
        yx = yx_ref[...].astype(F32)
        merged_ref[...] = (sa * ya + sp * yp + sx * yx).astype(BF16)
        dproj_ref[:, O_GA:O_GA + D] = (dmerged * ya * (sa * (1.0 - sa))).astype(BF16)
        dproj_ref[:, O_GP:O_GP + D] = (dmerged * yp * (sp * (1.0 - sp))).astype(BF16)
        dproj_ref[:, O_GX:O_GX + D] = (dmerged * yx * (sx * (1.0 - sx))).astype(BF16)
        dya = (dmerged * sa).astype(BF16)
        dyp = dmerged * sp
        dyx = (dmerged * sx).astype(BF16)
        dya_ref[...] = dya
        dyx_ref[...] = dyx
        stat_ref[1:2, :] += jnp.sum(dyp * ppv, axis=0, keepdims=True)
        dpp = (dyp * ps_ref[...]).astype(BF16)
        dpp_ref[...] = dpp

        da = _dot_nt(dya, wco_ref[...])
        c_a = proj_ref[:, O_CA:O_CA + D].astype(F32)
        u_a = proj_ref[:, O_UA:O_UA + D].astype(F32)
        cu = c_a * u_a
        halo_cu = halo_ref[:, O_CA:O_CA + D].astype(F32) * halo_ref[:, O_UA:O_UA + D].astype(F32)
        cu_ext[0:HALO, :] = jnp.where(tile > 0, halo_cu, 0.0)
        cu_ext[HALO:HALO + tm, :] = cu
        cu1 = cu_ext[HALO - 1:HALO - 1 + tm, :]
        cu2 = cu_ext[HALO - 2:HALO - 2 + tm, :]
        conv = cw_ref[2:3, :] * cu + cw_ref[1:2, :] * cu1 + cw_ref[0:1, :] * cu2
        dproj_ref[:, O_BA:O_BA + D] = (da * conv).astype(BF16)
        dconv = da * proj_ref[:, O_BA:O_BA + D].astype(F32)
        stat_ref[5:6, :] += jnp.sum(dconv * cu2, axis=0, keepdims=True)
        stat_ref[6:7, :] += jnp.sum(dconv * cu1, axis=0, keepdims=True)
        stat_ref[7:8, :] += jnp.sum(dconv * cu, axis=0, keepdims=True)
        dconv_ext[0:tm, :] = dconv
        dcu = (cw_ref[2:3, :] * dconv + cw_ref[1:2, :] * dconv_ext[1:1 + tm, :]
               + cw_ref[0:1, :] * dconv_ext[2:2 + tm, :])
        dproj_ref[:, O_CA:O_CA + D] = (dcu * u_a).astype(BF16)
        dproj_ref[:, O_UA:O_UA + D] = (dcu * c_a).astype(BF16)

        for g, window in enumerate(POOL_WINDOWS):
            cols = slice(g * GROUP_DIM, (g + 1) * GROUP_DIM)
            dpooled = _dot_nt(dpp[:, cols], wpool_ref[g])
            dpn_ext[0:tm, cols] = dpooled * _inv_count(tile, tm, window)
            acc = dpn_ext[0:tm, cols]
            for j in range(1, window):
                acc = acc + dpn_ext[j:j + tm, cols]
            dproj_ref[:, O_UP + g * GROUP_DIM:O_UP + (g + 1) * GROUP_DIM] = (acc - dpooled).astype(BF16)

        do = _dot_nt(dyx, wxo_ref[...])
        for hd in range(HEADS):
            cols = slice(hd * HEAD_DIM, (hd + 1) * HEAD_DIM)
            q = proj_ref[:, O_QX + hd * HEAD_DIM:O_QX + (hd + 1) * HEAD_DIM]
            kh = k_ref[:, cols]
            p = _softmax_rows(_dot_nt(q, kh) * ATT_SCALE)
            doh = do[:, cols].astype(BF16)
            dp = _dot_nt(doh, v_ref[:, cols])
            dv_ref[:, cols] += _dot_tn(p.astype(BF16), doh)
            ds = (p * (dp - jnp.sum(dp * p, axis=-1, keepdims=True)) * ATT_SCALE).astype(BF16)
            dproj_ref[:, O_QX + hd * HEAD_DIM:O_QX + (hd + 1) * HEAD_DIM] = _dot(ds, kh).astype(BF16)
            dk_ref[:, cols] += _dot_tn(ds, q)

        dconv_ext[tm:tm + HALO, :] = dconv_ext[0:HALO, :]
        dpn_ext[tm:tm + HALO, :] = dpn_ext[0:HALO, :]

    def rev(width):
        return pl.BlockSpec((tm, width), lambda s: (n_tiles - 1 - s, 0))

    halo_spec = pl.BlockSpec((HALO, D_IN), lambda s: (jnp.maximum((n_tiles - 1 - s) * halo_blocks - 1, 0), 0))
    act = jax.ShapeDtypeStruct((t_len, D), BF16)
    kv_grad = jax.ShapeDtypeStruct((N_MEM, D), F32)
    return pl.pallas_call(
        body, name="mixer_bwd",
        grid=(n_tiles,),
        in_specs=[rev(D), rev(D_IN), halo_spec, rev(D), rev(D), rev(D), rev(D),
                  _resident((8, D)), _resident((D, D)), _resident((GROUPS, GROUP_DIM, GROUP_DIM)), _resident((1, D)),
                  _resident((N_MEM, D)), _resident((N_MEM, D)), _resident((D, D)), _resident((D, D))],
        out_specs=[rev(D_IN), rev(D), rev(D), rev(D), rev(D),
                   _const((N_MEM, D)), _const((N_MEM, D)), _const((8, D))],
        out_shape=[jax.ShapeDtypeStruct((t_len, D_IN), BF16), act, act, act, act, kv_grad, kv_grad,
                   jax.ShapeDtypeStruct((8, D), F32)],
        scratch_shapes=[pltpu.VMEM((tm + HALO, D), F32)] * 3,
        compiler_params=_params(("arbitrary",)),
    )(dx1, proj, proj, ya, pp, yx, o, conv_w8, w_co, w_pool, pool_scale, k, v, w_xo, w_out)


def in_bwd(dproj, w_in4, x, dx1, g_mix):
    t_len = x.shape[0]
    tm = min(512, t_len)
    tk = D_IN // N_CHIPS

    def body(dproj_ref, w_ref, x_ref, dx1_ref, g_ref, gx_ref, stat_ref, acc_ref):
        i, j = pl.program_id(0), pl.program_id(1)

        @pl.when((i == 0) & (j == 0))
        def _():
            stat_ref[...] = jnp.zeros((8, D), F32)

        @pl.when(j == 0)
        def _():
            acc_ref[...] = jnp.zeros((tm, D), F32)

        acc_ref[...] += _dot_nt(dproj_ref[...], w_ref[...])

        @pl.when(j == N_CHIPS - 1)
        def _():
            dh = acc_ref[...]
            xv = x_ref[...]
            r = lax.rsqrt(jnp.mean(xv * xv, axis=-1, keepdims=True) + EPS)
            xh = xv * r
            stat_ref[0:1, :] += jnp.sum(dh * xh, axis=0, keepdims=True)
            dxh = dh * g_ref[...]
            gx_ref[...] = dx1_ref[...] + r * (dxh - xh * jnp.mean(dxh * xh, axis=-1, keepdims=True))

    return pl.pallas_call(
        body, name="in_bwd",
        grid=(t_len // tm, N_CHIPS),
        in_specs=[pl.BlockSpec((tm, tk), lambda i, j: (i, j)),
                  pl.BlockSpec((None, D, tk), lambda i, j: (j, 0, 0)),
                  pl.BlockSpec((tm, D), lambda i, j: (i, 0)),
                  pl.BlockSpec((tm, D), lambda i, j: (i, 0)),
                  pl.BlockSpec((1, D), lambda i, j: (0, 0))],
        out_specs=[pl.BlockSpec((tm, D), lambda i, j: (i, 0)), pl.BlockSpec((8, D), lambda i, j: (0, 0))],
        out_shape=[jax.ShapeDtypeStruct((t_len, D), F32), jax.ShapeDtypeStruct((8, D), F32)],
        scratch_shapes=[pltpu.VMEM((tm, D), F32)],
        compiler_params=_params(("arbitrary", "arbitrary")),
    )(dproj, w_in4, x, dx1, g_mix)


def kv_bwd(dk, dv, memn, mem, g_mem, w_kv4):
    half = D // 2

    def body(dk_ref, dv_ref, memn_ref, mem_ref, g_ref, w_ref, gw_ref, stat_ref):
        mn = memn_ref[...]
        parts = (dk_ref[:, 0:half], dk_ref[:, half:D], dv_ref[:, 0:half], dv_ref[:, half:D])
        dmemn = jnp.zeros((N_MEM, D), F32)
        for j, part in enumerate(parts):
            part = part.astype(BF16)
            gw_ref[j] = _dot_tn(mn, part)
            dmemn = dmemn + _dot_nt(part, w_ref[j])
        mv = mem_ref[...]
        r = lax.rsqrt(jnp.mean(mv * mv, axis=-1, keepdims=True) + EPS)
        stat_ref[...] = jnp.zeros((8, D), F32)
        stat_ref[0:1, :] = jnp.sum(dmemn * (mv * r), axis=0, keepdims=True)

    return pl.pallas_call(
        body, name="kv_bwd",
        out_shape=[jax.ShapeDtypeStruct((N_CHIPS, D, half), F32), jax.ShapeDtypeStruct((8, D), F32)],
        compiler_params=_params(),
    )(dk, dv, memn, mem, g_mem, w_kv4)


def matmul_tn(name, a, b, tn, col_blocks=1, k_dim=None, n_dim=None, a_col=0, b_col=0):
    t_len = a.shape[0]
    k_dim = a.shape[1] if k_dim is None else k_dim
    n_dim = b.shape[1] if n_dim is None else n_dim
    b_off = b_col * (n_dim // tn)
    tt = min(512, t_len)
    per_block = n_dim // col_blocks // tn

    def body(a_ref, b_ref, out_ref):
        @pl.when(pl.program_id(1) == 0)
        def _():
            out_ref[...] = jnp.zeros((k_dim, tn), F32)

        out_ref[...] += _dot_tn(a_ref[...].astype(BF16), b_ref[...].astype(BF16))

    if col_blocks == 1:
        out_spec = pl.BlockSpec((k_dim, tn), lambda n, t: (0, n))
        out_shape = jax.ShapeDtypeStruct((k_dim, n_dim), F32)
    else:
        out_spec = pl.BlockSpec((None, k_dim, tn), lambda n, t: (n // per_block, 0, n % per_block))
        out_shape = jax.ShapeDtypeStruct((col_blocks, k_dim, n_dim // col_blocks), F32)
    return pl.pallas_call(
        body, name=name,
        grid=(n_dim // tn, t_len // tt),
        in_specs=[pl.BlockSpec((tt, k_dim), lambda n, t: (t, a_col)), pl.BlockSpec((tt, tn), lambda n, t: (t, b_off + n))],
        out_specs=out_spec, out_shape=out_shape,
        compiler_params=_params(("arbitrary", "arbitrary")),
    )(a, b)


def _place():
    x, y, c = lax.axis_index("x"), lax.axis_index("y"), lax.axis_index("c")
    return x, y, c


def _other_chips(x, y):
    return [(1 - x, y), (x, 1 - y), (1 - x, 1 - y)]


def _any_specs(n):
    return [pl.BlockSpec(memory_space=pl.ANY)] * n


def gather_weights(shards):
    n = len(shards)

    def body(*refs):
        ins, outs = refs[:n], refs[n:2 * n]
        send_sems, recv_sems = refs[2 * n:]
        x, y, c = _place()
        me = 2 * x + y
        sibling = (x, y, 1 - c)
        chips = _other_chips(x, y)

        def half(ref, which, rows):
            return ref.at[:, pl.ds(which * (rows // 2), rows // 2), :]

        def remote(src, dst, a, slot, to):
            return pltpu.make_async_remote_copy(src_ref=src, dst_ref=dst, send_sem=send_sems.at[a, slot],
                                                recv_sem=recv_sems.at[a, slot], device_id=to, device_id_type=MESH)

        sends = []
        for a in range(n):
            cp = remote(ins[a], outs[a].at[me], a, 6, sibling)
            cp.start()
            sends.append(cp)
        for a in range(n):
            rows = ins[a].shape[1]
            for slot, (px, py) in enumerate(chips):
                cp = remote(half(ins[a], c, rows), half(outs[a].at[me], c, rows), a, slot, (px, py, c))
                cp.start()
                sends.append(cp)
        for a in range(n):
            rows = ins[a].shape[1]
            for slot, (px, py) in enumerate(chips):
                landed = half(outs[a].at[2 * px + py], c, rows)
                remote(landed, landed, a, slot, (px, py, c)).wait_recv()
                cp = remote(landed, landed, a, 3 + slot, sibling)
                cp.start()
                sends.append(cp)
        for a in range(n):
            rows = ins[a].shape[1]
            for slot, (px, py) in enumerate(chips):
                passed = half(outs[a].at[2 * px + py], 1 - c, rows)
                remote(passed, passed, a, 3 + slot, sibling).wait_recv()
        for a in range(n):
            remote(ins[a], outs[a].at[me], a, 6, sibling).wait_recv()
        for cp in sends:
            cp.wait_send()

    return pl.pallas_call(
        body, name="gather_weights",
        in_specs=_any_specs(n), out_specs=_any_specs(n),
        out_shape=[jax.ShapeDtypeStruct((N_CHIPS,) + s.shape, s.dtype) for s in shards],
        scratch_shapes=[pltpu.SemaphoreType.DMA((n, 7)), pltpu.SemaphoreType.DMA((n, 7))],
    )(*shards)


def exchange_sibling_halves(grads):
    n = len(grads)

    def body(*refs):
        ins, outs = refs[:n], refs[n:2 * n]
        send_sems, recv_sems = refs[2 * n:]
        x, y, c = _place()
        copies = []
        for a in range(n):
            h = ins[a].shape[1] // 2
            cp = pltpu.make_async_remote_copy(
                src_ref=ins[a].at[:, pl.ds((1 - c) * h, h), :], dst_ref=outs[a],
                send_sem=send_sems.at[a], recv_sem=recv_sems.at[a], device_id=(x, y, 1 - c), device_id_type=MESH)
            cp.start()
            copies.append(cp)
        for cp in copies:
            cp.wait()

    return pl.pallas_call(
        body, name="exchange_sibling_halves",
        in_specs=_any_specs(n), out_specs=_any_specs(n),
        out_shape=[jax.ShapeDtypeStruct((N_CHIPS, g.shape[1] // 2, g.shape[2]), F32) for g in grads],
        scratch_shapes=[pltpu.SemaphoreType.DMA((n,)), pltpu.SemaphoreType.DMA((n,))],
    )(*grads)


def _row_tile(rows, cols, budget=1 << 20):
    best = 16
    for tr in range(16, rows + 1, 16):
        if rows % tr == 0 and tr * cols * 4 <= budget:
            best = tr
    return best


def add_sibling_half(name, grad, got, core):
    _, rows, cols = grad.shape
    h = rows // 2
    tr = _row_tile(h, cols)
    per_half = h // tr

    def body(core_ref, g_ref, o_ref, out_ref, out16_ref):
        total = g_ref[...] + o_ref[...]
        out_ref[...] = total
        out16_ref[...] = total.astype(BF16)

    out_spec = pl.BlockSpec((None, tr, cols), lambda j, r, cr: (j, r, 0))
    return pl.pallas_call(
        body, name=name,
        grid_spec=pltpu.PrefetchScalarGridSpec(
            num_scalar_prefetch=1, grid=(N_CHIPS, per_half),
            in_specs=[pl.BlockSpec((None, tr, cols), lambda j, r, cr: (j, cr[0] * per_half + r, 0)),
                      pl.BlockSpec((None, tr, cols), lambda j, r, cr: (j, r, 0))],
            out_specs=[out_spec, out_spec]),
        out_shape=[jax.ShapeDtypeStruct((N_CHIPS, h, cols), F32), jax.ShapeDtypeStruct((N_CHIPS, h, cols), BF16)],
        compiler_params=_params(("arbitrary", "arbitrary")),
    )(core, grad, got)


def exchange_chip_blocks(partials):
    n = len(partials)

    def body(*refs):
        ins, outs = refs[:n], refs[n:2 * n]
        send_sems, recv_sems = refs[2 * n:]
        x, y, c = _place()
        copies = []
        for a in range(n):
            for slot, (px, py) in enumerate(_other_chips(x, y)):
                cp = pltpu.make_async_remote_copy(
                    src_ref=ins[a].at[2 * px + py], dst_ref=outs[a].at[slot],
                    send_sem=send_sems.at[a, slot], recv_sem=recv_sems.at[a, slot],
                    device_id=(px, py, c), device_id_type=MESH)
                cp.start()
                copies.append(cp)
        for cp in copies:
            cp.wait()

    return pl.pallas_call(
        body, name="exchange_chip_blocks",
        in_specs=_any_specs(n), out_specs=_any_specs(n),
        out_shape=[jax.ShapeDtypeStruct((3,) + p.shape[1:], p.dtype) for p in partials],
        scratch_shapes=[pltpu.SemaphoreType.DMA((n, 3)), pltpu.SemaphoreType.DMA((n, 3))],
    )(*partials)


def add_chip_blocks(name, partial, got, chip):
    _, h, cols = partial.shape
    tr = _row_tile(h, cols)

    def body(chip_ref, p_ref, g0_ref, g1_ref, g2_ref, out_ref):
        out_ref[...] = ((p_ref[...] + g0_ref[...].astype(F32)) + g1_ref[...].astype(F32)) + g2_ref[...].astype(F32)

    def got_spec(slot):
        return pl.BlockSpec((None, tr, cols), lambda r, ch: (slot, r, 0))

    return pl.pallas_call(
        body, name=name,
        grid_spec=pltpu.PrefetchScalarGridSpec(
            num_scalar_prefetch=1, grid=(h // tr,),
            in_specs=[pl.BlockSpec((None, tr, cols), lambda r, ch: (ch[0], r, 0)), got_spec(0), got_spec(1), got_spec(2)],
            out_specs=pl.BlockSpec((tr, cols), lambda r, ch: (r, 0))),
        out_shape=jax.ShapeDtypeStruct((h, cols), F32),
        compiler_params=_params(("arbitrary",)),
    )(chip, partial, got, got, got)


def swap_sibling_halves(halves):
    n = len(halves)

    def body(*refs):
        ins, outs = refs[:n], refs[n:2 * n]
        send_sems, recv_sems = refs[2 * n:]
        x, y, c = _place()
        copies = []
        for a in range(n):
            cp = pltpu.make_async_remote_copy(
                src_ref=ins[a], dst_ref=outs[a], send_sem=send_sems.at[a], recv_sem=recv_sems.at[a],
                device_id=(x, y, 1 - c), device_id_type=MESH)
            cp.start()
            copies.append(cp)
        for cp in copies:
            cp.wait()

    return pl.pallas_call(
        body, name="swap_sibling_halves",
        in_specs=_any_specs(n), out_specs=_any_specs(n),
        out_shape=[jax.ShapeDtypeStruct(v.shape, F32) for v in halves],
        scratch_shapes=[pltpu.SemaphoreType.DMA((n,)), pltpu.SemaphoreType.DMA((n,))],
    )(*halves)


def all_reduce_small(pack):
    rows = pack.shape[0]

    def body(pack_ref, out_ref, gathered, send_sems, recv_sems):
        x, y, c = _place()
        me = 4 * x + 2 * y + c
        gathered[me] = pack_ref[...]
        copies = []
        for rel in range(1, 8):
            fx, fy, fc = (rel >> 2) & 1, (rel >> 1) & 1, rel & 1
            peer = (x ^ fx, y ^ fy, c ^ fc)
            cp = pltpu.make_async_remote_copy(
                src_ref=pack_ref, dst_ref=gathered.at[me], send_sem=send_sems.at[rel - 1], recv_sem=recv_sems.at[rel - 1],
                device_id=peer, device_id_type=MESH)
            cp.start()
            copies.append(cp)
        for rel in range(1, 8):
            fx, fy, fc = (rel >> 2) & 1, (rel >> 1) & 1, rel & 1
            src = 4 * (x ^ fx) + 2 * (y ^ fy) + (c ^ fc)
            pltpu.make_async_remote_copy(
                src_ref=pack_ref, dst_ref=gathered.at[src], send_sem=send_sems.at[rel - 1], recv_sem=recv_sems.at[rel - 1],
                device_id=(x, y, c), device_id_type=MESH).wait_recv()
        for cp in copies:
            cp.wait_send()
        total = gathered[0]
        for dev in range(1, 8):
            total = total + gathered[dev]
        out_ref[...] = total

    return pl.pallas_call(
        body, name="all_reduce_small",
        in_specs=[pl.BlockSpec(memory_space=pltpu.VMEM)], out_specs=pl.BlockSpec(memory_space=pltpu.VMEM),
        out_shape=jax.ShapeDtypeStruct((rows, D), F32),
        scratch_shapes=[pltpu.VMEM((8, rows, D), F32), pltpu.SemaphoreType.DMA((7,)), pltpu.SemaphoreType.DMA((7,))],
    )(pack)


def _adamw_update(w, g, m, v):
    nm = ADAM_B1 * m + (1.0 - ADAM_B1) * g
    nv = ADAM_B2 * v + (1.0 - ADAM_B2) * (g * g)
    m_hat = nm / (1.0 - ADAM_B1 ** ADAM_STEP)
    v_hat = nv / (1.0 - ADAM_B2 ** ADAM_STEP)
    delta = -ADAM_LR * (m_hat / (jnp.sqrt(v_hat) + ADAM_EPS) + ADAM_WD * w)
    return delta, nm, nv


def adamw(name, w, g, m, v):
    def body(w_ref, g_ref, m_ref, v_ref, d_ref, nm_ref, nv_ref):
        d_ref[...], nm_ref[...], nv_ref[...] = _adamw_update(w_ref[...], g_ref[...], m_ref[...], v_ref[...])

    out = jax.ShapeDtypeStruct(w.shape, F32)
    return pl.pallas_call(body, name=name, out_shape=[out] * 3, compiler_params=_params())(w, g, m, v)


def adamw_halves(name, w, mine, theirs, m, v, core):
    h, cols = mine.shape
    tr = _row_tile(h, cols)
    per_half = h // tr

    def body(core_ref, w_ref, mine_ref, theirs_ref, m_ref, v_ref, g_ref, d_ref, nm_ref, nv_ref):
        g = jnp.where(pl.program_id(0) == core_ref[0], mine_ref[...], theirs_ref[...])
        g_ref[...] = g
        d_ref[...], nm_ref[...], nv_ref[...] = _adamw_update(w_ref[...], g, m_ref[...], v_ref[...])

    full = pl.BlockSpec((tr, cols), lambda hh, r, cr: (hh * per_half + r, 0))
    mine_spec = pl.BlockSpec((tr, cols), lambda hh, r, cr: (jnp.where(hh == cr[0], r, 0), 0))
    theirs_spec = pl.BlockSpec((tr, cols), lambda hh, r, cr: (jnp.where(hh == cr[0], 0, r), 0))
    out = jax.ShapeDtypeStruct((2 * h, cols), F32)
    return pl.pallas_call(
        body, name=name,
        grid_spec=pltpu.PrefetchScalarGridSpec(
            num_scalar_prefetch=1, grid=(2, per_half),
            in_specs=[full, mine_spec, theirs_spec, full, full], out_specs=[full] * 4),
        out_shape=[out] * 4,
        compiler_params=_params(("arbitrary", "arbitrary")),
    )(core, w, mine, theirs, m, v)


BIG = ("w_in", "w_conv_out", "w_pool", "w_kv", "w_xattn_out", "w_out", "w_gate", "w_up", "w_down")


def kernel(x, mem, norm_mix, w_in, conv_w, w_conv_out, w_pool, pool_scale, norm_mem, w_kv, w_xattn_out, w_out, norm_ffn, w_gate, w_up, w_down, norm_final, loss_target, m_norm_mix, m_w_in, m_conv_w, m_w_conv_out, m_w_pool, m_pool_scale, m_norm_mem, m_w_kv, m_w_xattn_out, m_w_out, m_norm_ffn, m_w_gate, m_w_up, m_w_down, m_norm_final, v_norm_mix, v_w_in, v_conv_w, v_w_conv_out, v_w_pool, v_pool_scale, v_norm_mem, v_w_kv, v_w_xattn_out, v_w_out, v_norm_ffn, v_w_gate, v_w_up, v_w_down, v_norm_final):
    t_len = x.shape[1]
    xi, yi, ci = lax.axis_index("x"), lax.axis_index("y"), lax.axis_index("c")
    chip = 2 * xi + yi
    chip_arr = jnp.reshape(chip, (1,)).astype(jnp.int32)
    core_arr = jnp.reshape(ci, (1,)).astype(jnp.int32)

    conv_pad = jnp.concatenate([conv_w, jnp.zeros((1, 13, 256), F32)], axis=1)
    def t2(w):
        return jnp.swapaxes(w, 1, 2)

    shards = [
        w_in.astype(BF16),
        w_conv_out.astype(BF16), w_xattn_out.astype(BF16), w_out.astype(BF16),
        w_pool[0].astype(BF16),
        w_kv.astype(BF16),
        t2(w_gate).astype(BF16), t2(w_up).astype(BF16), w_down.astype(BF16),
        conv_pad,
    ]
    g_in, g_co, g_xo, g_out, g_pool, g_kv, g_gate, g_up, g_down, g_conv_w = gather_weights(shards)
    w_in4 = g_in.reshape(N_CHIPS, D, D_IN // N_CHIPS)
    w_co_f, w_xo_f, w_out_f = g_co.reshape(D, D), g_xo.reshape(D, D), g_out.reshape(D, D)
    w_pool_f = jnp.transpose(g_pool, (1, 0, 2, 3)).reshape(GROUPS, GROUP_DIM, GROUP_DIM)
    w_kv4 = g_kv.reshape(N_CHIPS, D, D // 2)
    w_gate_f, w_up_f, w_down_f = g_gate.reshape(D_FF, D), g_up.reshape(D_FF, D), g_down.reshape(D_FF, D)
    conv_full = jnp.transpose(g_conv_w[:, 0, 0:8, :], (1, 0, 2)).reshape(8, D)

    x2d = x[0]
    tgt = loss_target[0]
    proj, h = proj_fwd(x2d, norm_mix, w_in4)
    memn, k, v = kv_fwd(mem[0], norm_mem, w_kv4)
    a, pooled, ya, pp, yx, o, x1 = mixer_fwd(proj, x2d, conv_full, w_co_f, w_pool_f, pool_scale, k, v, w_xo_f, w_out_f)
    gate, up, dx2, stat_f = ffn_fwd(x1, tgt, norm_ffn, w_gate_f, w_up_f, w_down_f, norm_final.reshape(1, D))

    dx1, dgate, dup, act, h2, stat_b1 = ffn_bwd(dx2, x1, gate, up, norm_ffn, w_gate_f, w_up_f, w_down_f)
    dproj, merged, dya, dpp, dyx, dk, dv, stat_b2 = mixer_bwd(
        dx1, proj, ya, pp, yx, o, conv_full, w_co_f, w_pool_f, pool_scale, k, v, w_xo_f, w_out_f)
    grad_x, stat_b3 = in_bwd(dproj, w_in4, x2d, dx1, norm_mix)
    gw_kv, stat_kv = kv_bwd(dk, dv, memn, mem[0], norm_mem, w_kv4)

    gw_in = matmul_tn("grad_w_in", h, dproj, 2048, col_blocks=N_CHIPS)
    gw_co = matmul_tn("grad_w_conv_out", a, dya, 1024)
    gw_xo = matmul_tn("grad_w_xattn_out", o, dyx, 1024)
    gw_out = matmul_tn("grad_w_out", merged, dx1, 1024)
    gw_gate = matmul_tn("grad_w_gate", dgate, h2, 512)
    gw_up = matmul_tn("grad_w_up", dup, h2, 512)
    gw_down = matmul_tn("grad_w_down", act, dx2, 512)
    gw_pool = jnp.stack([matmul_tn("grad_w_pool_%d" % g, pooled, dpp, GROUP_DIM, k_dim=GROUP_DIM, n_dim=GROUP_DIM,
                                   a_col=g, b_col=g) for g in range(GROUPS)])

    full = {
        "w_in": gw_in,
        "w_conv_out": gw_co.reshape(N_CHIPS, D // N_CHIPS, D),
        "w_pool": jnp.transpose(gw_pool.reshape(GROUPS, N_CHIPS, 64, GROUP_DIM), (1, 0, 2, 3)).reshape(N_CHIPS, 256, GROUP_DIM),
        "w_kv": gw_kv,
        "w_xattn_out": gw_xo.reshape(N_CHIPS, D // N_CHIPS, D),
        "w_out": gw_out.reshape(N_CHIPS, D // N_CHIPS, D),
        "w_gate": gw_gate.reshape(N_CHIPS, D_FF // N_CHIPS, D),
        "w_up": gw_up.reshape(N_CHIPS, D_FF // N_CHIPS, D),
        "w_down": gw_down.reshape(N_CHIPS, D_FF // N_CHIPS, D),
    }

    grads = [full[n] for n in BIG]
    got = exchange_sibling_halves(grads)
    partials = [add_sibling_half("add_sibling_" + n, g, o_, core_arr) for n, g, o_ in zip(BIG, grads, got)]
    got2 = exchange_chip_blocks([p16 for _, p16 in partials])
    mine = [add_chip_blocks("add_chips_" + n, p32, g2, chip_arr) for n, (p32, _), g2 in zip(BIG, partials, got2)]
    theirs = swap_sibling_halves(mine)

    pack = jnp.concatenate([stat_b3[0:1], stat_b2[1:2], stat_kv[0:1], stat_b1[0:1], stat_f[0:1], stat_b2[5:8],
                            stat_f[1:2], jnp.zeros((7, D), F32)], axis=0)
    total = all_reduce_small(pack)
    loss = jnp.sum(total[8])
    g_conv_full = total[5:8]
    g_conv = lax.dynamic_slice_in_dim(g_conv_full, chip * 256, 256, axis=1)

    given = dict(w_in=(w_in, m_w_in, v_w_in), w_conv_out=(w_conv_out, m_w_conv_out, v_w_conv_out),
                 w_pool=(w_pool, m_w_pool, v_w_pool), w_kv=(w_kv, m_w_kv, v_w_kv),
                 w_xattn_out=(w_xattn_out, m_w_xattn_out, v_w_xattn_out), w_out=(w_out, m_w_out, v_w_out),
                 w_gate=(w_gate, m_w_gate, v_w_gate), w_up=(w_up, m_w_up, v_w_up), w_down=(w_down, m_w_down, v_w_down))
    out_g, out_d, out_m, out_v = {}, {}, {}, {}
    for n, mine_n, theirs_n in zip(BIG, mine, theirs):
        transposed = n in ("w_gate", "w_up")
        rows2d = (2 * mine_n.shape[0], mine_n.shape[1])
        w_, m_, v_ = ((t2(t) if transposed else t).reshape(rows2d) for t in given[n])
        res = adamw_halves("adamw_" + n, w_, mine_n, theirs_n, m_, v_, core_arr)
        if transposed:
            res = [t2(t.reshape(1, D_FF // N_CHIPS, D)) for t in res]
        out_g[n], out_d[n], out_m[n], out_v[n] = (t.reshape(given[n][0].shape) for t in res)

    def small_pack(vals, conv_part):
        conv_rows = jnp.concatenate([conv_part.reshape(3, 256), jnp.zeros((3, D - 256), F32)], axis=1)
        return jnp.concatenate([val.reshape(1, D) for val in vals] + [conv_rows], axis=0)

    sw = small_pack([norm_mix, pool_scale, norm_mem, norm_ffn, norm_final], conv_w)
    sm = small_pack([m_norm_mix, m_pool_scale, m_norm_mem, m_norm_ffn, m_norm_final], m_conv_w)
    sv = small_pack([v_norm_mix, v_pool_scale, v_norm_mem, v_norm_ffn, v_norm_final], v_conv_w)
    sg = small_pack([total[r] for r in range(5)], g_conv)
    sd, snm, snv = adamw("adamw_small", sw, sg, sm, sv)
    small_names = ("norm_mix", "pool_scale", "norm_mem", "norm_ffn", "norm_final")
    small_shapes = dict(norm_mix=(1, D), pool_scale=(1, D), norm_mem=(1, D), norm_ffn=(1, D), norm_final=(D,))
    for r, n in enumerate(small_names):
        out_g[n], out_d[n], out_m[n], out_v[n] = (t[r].reshape(small_shapes[n]) for t in (sg, sd, snm, snv))
    out_g["conv_w"], out_d["conv_w"], out_m["conv_w"], out_v["conv_w"] = (
        t[5:8, 0:256].reshape(1, 3, 256) for t in (sg, sd, snm, snv))

    order = ("norm_mix", "w_in", "conv_w", "w_conv_out", "w_pool", "pool_scale", "norm_mem", "w_kv", "w_xattn_out",
             "w_out", "norm_ffn", "w_gate", "w_up", "w_down", "norm_final")
    return (loss, grad_x.reshape(1, t_len, D), *[out_g[n] for n in order], *[out_d[n] for n in order],
            *[out_m[n] for n in order], *[out_v[n] for n in order])
```

```python
import functools

import jax
import jax.numpy as jnp
from jax import lax
from jax.experimental import pallas as pl
from jax.experimental.pallas import tpu as pltpu

F32 = jnp.float32
BF16 = jnp.bfloat16
MESH = pl.DeviceIdType.MESH

D = 1024
N_MEM = 256
HEADS = 4
HEAD_DIM = 256
GROUPS = 4
GROUP_DIM = 256
POOL_WINDOWS = (2, 4, 8, 16)
D_FF = 2816
D_IN = 8192
N_CHIPS = 4
EPS = 1e-6
HALO = 16
ATT_SCALE = HEAD_DIM ** -0.5

ADAM_LR = 0.001
ADAM_B1 = 0.9
ADAM_B2 = 0.999
ADAM_EPS = 1e-08
ADAM_WD = 0.01
ADAM_STEP = 10

VMEM_LIMIT = 56 * 1024 * 1024

O_BA, O_CA, O_UA, O_UP, O_QX, O_GA, O_GP, O_GX = (k * D for k in range(8))

NT_DIMS = (((1,), (1,)), ((), ()))
TN_DIMS = (((0,), (0,)), ((), ()))


def _dot(a, b):
    return jnp.dot(a, b, preferred_element_type=F32)


def _dot_nt(a, b):
    return lax.dot_general(a, b, NT_DIMS, preferred_element_type=F32)


def _dot_tn(a, b):
    return lax.dot_general(a, b, TN_DIMS, preferred_element_type=F32)


def _sigmoid(z):
    return 1.0 / (1.0 + jnp.exp(-z))


def _params(semantics=None):
    return pltpu.CompilerParams(dimension_semantics=semantics, vmem_limit_bytes=VMEM_LIMIT)


def _resident(shape):
    zeros = (0,) * len(shape)
    return pl.BlockSpec(shape, lambda *_: zeros, pipeline_mode=pl.Buffered(1))


def _const(shape):
    zeros = (0,) * len(shape)
    return pl.BlockSpec(shape, lambda *_: zeros)


def _rows(tm, width):
    return pl.BlockSpec((tm, width), lambda i: (i, 0))


def _inv_count(tile, tm, window):
    t = tile * tm + lax.broadcasted_iota(jnp.int32, (tm, 1), 0)
    return 1.0 / jnp.minimum(t + 1, window).astype(F32)


def _carried_call(body, name, grid, in_specs, out_specs, out_shape, scratch_shapes, semantics, args, carry):
    if carry is None:
        res = pl.pallas_call(body, name=name, grid=grid, in_specs=in_specs, out_specs=out_specs, out_shape=out_shape,
                             scratch_shapes=scratch_shapes, compiler_params=_params(semantics))(*args)
        return res, []
    comm_cls, comm_args = carry
    n, n_in, n_out, n_scratch = len(comm_args), len(in_specs), len(out_specs), len(scratch_shapes)
    comm_shapes = _gathered_shapes(comm_args) if comm_cls is _Gather else _exchanged_shapes(comm_args)
    comm_sems = _gather_sems(n) if comm_cls is _Gather else _exchange_sems(n)

    def carrying(*refs):
        ins, comm_ins = refs[:n_in], refs[n_in:n_in + n]
        outs, comm_outs = refs[n_in + n:n_in + n + n_out], refs[n_in + n + n_out:n_in + 2 * n + n_out]
        scratch, sems = refs[n_in + 2 * n + n_out:n_in + 2 * n + n_out + n_scratch], refs[n_in + 2 * n + n_out + n_scratch:]
        steps = [pl.program_id(d) for d in range(len(grid))]
        first = functools.reduce(jnp.logical_and, [s == 0 for s in steps])
        last = functools.reduce(jnp.logical_and, [s == g - 1 for s, g in zip(steps, grid)])

        @pl.when(first)
        def _():
            comm_cls(comm_ins, comm_outs, *sems).start()

        body(*ins, *outs, *scratch)

        @pl.when(last)
        def _():
            comm_cls(comm_ins, comm_outs, *sems).finish()

    res = pl.pallas_call(
        carrying, name=name, grid=grid, in_specs=list(in_specs) + _any_specs(n), out_specs=list(out_specs) + _any_specs(n),
        out_shape=list(out_shape) + comm_shapes, scratch_shapes=list(scratch_shapes) + comm_sems,
        compiler_params=_params(semantics))(*args, *comm_args)
    return res[:n_out], res[n_out:]


def proj_fwd(x, g_mix, w_in4, carry=None):
    t_len = x.shape[0]
    tm = min(512, t_len)
    tn = D_IN // N_CHIPS

    def body(x_ref, g_ref, w_ref, proj_ref, h_ref):
        @pl.when(pl.program_id(1) == 0)
        def _():
            xv = x_ref[...]
            r = lax.rsqrt(jnp.mean(xv * xv, axis=-1, keepdims=True) + EPS)
            h_ref[...] = (xv * r * g_ref[...]).astype(BF16)

        proj_ref[...] = _dot(h_ref[...], w_ref[...]).astype(BF16)

    return _carried_call(
        body, "proj_fwd", (t_len // tm, N_CHIPS),
        in_specs=[pl.BlockSpec((tm, D), lambda i, j: (i, 0)),
                  pl.BlockSpec((1, D), lambda i, j: (0, 0)),
                  pl.BlockSpec((None, D, tn), lambda i, j: (j, 0, 0))],
        out_specs=[pl.BlockSpec((tm, tn), lambda i, j: (i, j)),
                   pl.BlockSpec((tm, D), lambda i, j: (i, 0))],
        out_shape=[jax.ShapeDtypeStruct((t_len, D_IN), BF16), jax.ShapeDtypeStruct((t_len, D), BF16)],
        scratch_shapes=[], semantics=("arbitrary", "arbitrary"), args=(x, g_mix, w_in4), carry=carry)


def kv_fwd(mem, g_mem, w_kv4):
    half = D // 2

    def body(mem_ref, g_ref, w_ref, memn_ref, k_ref, v_ref):
        mv = mem_ref[...]
        r = lax.rsqrt(jnp.mean(mv * mv, axis=-1, keepdims=True) + EPS)
        mn = (mv * r * g_ref[...]).astype(BF16)
        memn_ref[...] = mn
        k_ref[:, 0:half] = _dot(mn, w_ref[0]).astype(BF16)
        k_ref[:, half:D] = _dot(mn, w_ref[1]).astype(BF16)
        v_ref[:, 0:half] = _dot(mn, w_ref[2]).astype(BF16)
        v_ref[:, half:D] = _dot(mn, w_ref[3]).astype(BF16)

    out = jax.ShapeDtypeStruct((N_MEM, D), BF16)
    return pl.pallas_call(body, name="kv_fwd", out_shape=[out, out, out], compiler_params=_params())(mem, g_mem, w_kv4)


def _softmax_rows(s):
    m = jnp.max(s, axis=-1, keepdims=True)
    e = jnp.exp(s - m)
    return e * (1.0 / jnp.sum(e, axis=-1, keepdims=True))


def mixer_fwd(proj, x, conv_w8, w_co, w_pool, pool_scale, k, v, w_xo, w_out, carry=None):
    t_len = x.shape[0]
    tm = min(256, t_len)

    def body(proj_ref, x_ref, cw_ref, wco_ref, wpool_ref, ps_ref, k_ref, v_ref, wxo_ref, wout_ref,
             a_ref, pooled_ref, ya_ref, pp_ref, yx_ref, o_ref, x1_ref, cu_ext, up_ext):
        i = pl.program_id(0)

        @pl.when(i == 0)
        def _():
            cu_ext[0:HALO, :] = jnp.zeros((HALO, D), F32)
            up_ext[0:HALO, :] = jnp.zeros((HALO, D), F32)

        cu = proj_ref[:, O_CA:O_CA + D].astype(F32) * proj_ref[:, O_UA:O_UA + D].astype(F32)
        cu_ext[HALO:HALO + tm, :] = cu
        conv = (cw_ref[2:3, :] * cu + cw_ref[1:2, :] * cu_ext[HALO - 1:HALO - 1 + tm, :]
                + cw_ref[0:1, :] * cu_ext[HALO - 2:HALO - 2 + tm, :])
        a = (proj_ref[:, O_BA:O_BA + D].astype(F32) * conv).astype(BF16)
        a_ref[...] = a
        ya = _dot(a, wco_ref[...])
        ya_ref[...] = ya.astype(BF16)

        up_ext[HALO:HALO + tm, :] = proj_ref[:, O_UP:O_UP + D].astype(F32)
        for g, window in enumerate(POOL_WINDOWS):
            cols = slice(g * GROUP_DIM, (g + 1) * GROUP_DIM)
            tok = up_ext[HALO:HALO + tm, cols]
            acc = tok
            for j in range(1, window):
                acc = acc + up_ext[HALO - j:HALO - j + tm, cols]
            pooled = (acc * _inv_count(i, tm, window) - tok).astype(BF16)
            pooled_ref[:, cols] = pooled
            pp_ref[:, cols] = _dot(pooled, wpool_ref[g]).astype(BF16)

        for hd in range(HEADS):
            cols = slice(hd * HEAD_DIM, (hd + 1) * HEAD_DIM)
            q = proj_ref[:, O_QX + hd * HEAD_DIM:O_QX + (hd + 1) * HEAD_DIM]
            p = _softmax_rows(_dot_nt(q, k_ref[:, cols]) * ATT_SCALE)
            o_ref[:, cols] = _dot(p.astype(BF16), v_ref[:, cols]).astype(BF16)
        yx = _dot(o_ref[...], wxo_ref[...])
        yx_ref[...] = yx.astype(BF16)

        merged = (_sigmoid(proj_ref[:, O_GA:O_GA + D].astype(F32)) * ya
                  + _sigmoid(proj_ref[:, O_GP:O_GP + D].astype(F32)) * (pp_ref[...].astype(F32) * ps_ref[...])
                  + _sigmoid(proj_ref[:, O_GX:O_GX + D].astype(F32)) * yx)
        x1_ref[...] = x_ref[...] + _dot(merged.astype(BF16), wout_ref[...])

        cu_ext[0:HALO, :] = cu_ext[tm:tm + HALO, :]
        up_ext[0:HALO, :] = up_ext[tm:tm + HALO, :]

    act = jax.ShapeDtypeStruct((t_len, D), BF16)
    return _carried_call(
        body, "mixer_fwd", (t_len // tm,),
        in_specs=[_rows(tm, D_IN), _rows(tm, D), _resident((8, D)), _resident((D, D)),
                  _resident((GROUPS, GROUP_DIM, GROUP_DIM)), _resident((1, D)),
                  _resident((N_MEM, D)), _resident((N_MEM, D)), _resident((D, D)), _resident((D, D))],
        out_specs=[_rows(tm, D)] * 7,
        out_shape=[act] * 6 + [jax.ShapeDtypeStruct((t_len, D), F32)],
        scratch_shapes=[pltpu.VMEM((tm + HALO, D), F32), pltpu.VMEM((tm + HALO, D), F32)],
        semantics=("arbitrary",), args=(proj, x, conv_w8, w_co, w_pool, pool_scale, k, v, w_xo, w_out), carry=carry)


def ffn_fwd(x1, target, g_ffn, w_gate, w_up, w_down, g_final):
    t_len = x1.shape[0]
    tm = min(256, t_len)

    def body(x1_ref, tgt_ref, g_ref, wg_ref, wu_ref, wd_ref, gf_ref, gate_ref, up_ref, dx2_ref, stat_ref):
        @pl.when(pl.program_id(0) == 0)
        def _():
            stat_ref[...] = jnp.zeros((8, D), F32)

        x1v = x1_ref[...]
        r2 = lax.rsqrt(jnp.mean(x1v * x1v, axis=-1, keepdims=True) + EPS)
        h2 = (x1v * r2 * g_ref[...]).astype(BF16)
        gate = _dot_nt(h2, wg_ref[...])
        up = _dot_nt(h2, wu_ref[...])
        gate_ref[...] = gate.astype(BF16)
        up_ref[...] = up.astype(BF16)
        act = (gate * _sigmoid(gate) * up).astype(BF16)
        x2 = x1v + _dot(act, wd_ref[...])
        r3 = lax.rsqrt(jnp.mean(x2 * x2, axis=-1, keepdims=True) + EPS)
        xh = x2 * r3
        diff = xh * gf_ref[...] - tgt_ref[...]
        dy = diff * (1.0 / D)
        stat_ref[0:1, :] += jnp.sum(dy * xh, axis=0, keepdims=True)
        stat_ref[1:2, :] += (0.5 / D) * jnp.sum(diff * diff, axis=0, keepdims=True)
        dxh = dy * gf_ref[...]
        dx2_ref[...] = r3 * (dxh - xh * jnp.mean(dxh * xh, axis=-1, keepdims=True))

    return pl.pallas_call(
        body, name="ffn_fwd",
        grid=(t_len // tm,),
        in_specs=[_rows(tm, D), _rows(tm, D), _resident((1, D)), _resident((D_FF, D)), _resident((D_FF, D)),
                  _resident((D_FF, D)), _resident((1, D))],
        out_specs=[_rows(tm, D_FF), _rows(tm, D_FF), _rows(tm, D), _const((8, D))],
        out_shape=[jax.ShapeDtypeStruct((t_len, D_FF), BF16), jax.ShapeDtypeStruct((t_len, D_FF), BF16),
                   jax.ShapeDtypeStruct((t_len, D), F32), jax.ShapeDtypeStruct((8, D), F32)],
        compiler_params=_params(("arbitrary",)),
    )(x1, target, g_ffn, w_gate, w_up, w_down, g_final)


def ffn_bwd(dx2, x1, gate, up, g_ffn, w_gate, w_up, w_down):
    t_len = x1.shape[0]
    tm = min(256, t_len)

    def body(dx2_ref, x1_ref, gate_ref, up_ref, g_ref, wg_ref, wu_ref, wd_ref,
             dx1_ref, dgate_ref, dup_ref, act_ref, h2_ref, stat_ref):
        @pl.when(pl.program_id(0) == 0)
        def _():
            stat_ref[...] = jnp.zeros((8, D), F32)

        dx2v = dx2_ref[...]
        gate = gate_ref[...].astype(F32)
        upv = up_ref[...].astype(F32)
        sg = _sigmoid(gate)
        silu = gate * sg
        act_ref[...] = (silu * upv).astype(BF16)
        dact = _dot_nt(dx2v.astype(BF16), wd_ref[...])
        dup = (dact * silu).astype(BF16)
        dgate = (dact * upv * (sg * (1.0 + gate * (1.0 - sg)))).astype(BF16)
        dup_ref[...] = dup
        dgate_ref[...] = dgate
        dh2 = _dot(dgate, wg_ref[...]) + _dot(dup, wu_ref[...])
        x1v = x1_ref[...]
        r2 = lax.rsqrt(jnp.mean(x1v * x1v, axis=-1, keepdims=True) + EPS)
        xh = x1v * r2
        h2_ref[...] = (xh * g_ref[...]).astype(BF16)
        stat_ref[0:1, :] += jnp.sum(dh2 * xh, axis=0, keepdims=True)
        dxh = dh2 * g_ref[...]
        dx1_ref[...] = dx2v + r2 * (dxh - xh * jnp.mean(dxh * xh, axis=-1, keepdims=True))

    ff = jax.ShapeDtypeStruct((t_len, D_FF), BF16)
    return pl.pallas_call(
        body, name="ffn_bwd",
        grid=(t_len // tm,),
        in_specs=[_rows(tm, D), _rows(tm, D), _rows(tm, D_FF), _rows(tm, D_FF), _resident((1, D)),
                  _resident((D_FF, D)), _resident((D_FF, D)), _resident((D_FF, D))],
        out_specs=[_rows(tm, D), _rows(tm, D_FF), _rows(tm, D_FF), _rows(tm, D_FF), _rows(tm, D), _const((8, D))],
        out_shape=[jax.ShapeDtypeStruct((t_len, D), F32), ff, ff, ff, jax.ShapeDtypeStruct((t_len, D), BF16),
                   jax.ShapeDtypeStruct((8, D), F32)],
        compiler_params=_params(("arbitrary",)),
    )(dx2, x1, gate, up, g_ffn, w_gate, w_up, w_down)


def mixer_bwd(dx1, proj, ya, pp, yx, o, conv_w8, w_co, w_pool, pool_scale, k, v, w_xo, w_out, carry=None):
    t_len = dx1.shape[0]
    tm = min(256, t_len)
    n_tiles = t_len // tm
    halo_blocks = tm // HALO

    def body(dx1_ref, proj_ref, halo_ref, ya_ref, pp_ref, yx_ref, o_ref,
             cw_ref, wco_ref, wpool_ref, ps_ref, k_ref, v_ref, wxo_ref, wout_ref,
             dproj_ref, merged_ref, dya_ref, dpp_ref, dyx_ref, dk_ref, dv_ref, stat_ref,
             cu_ext, dconv_ext, dpn_ext):
        step = pl.program_id(0)
        tile = n_tiles - 1 - step

        @pl.when(step == 0)
        def _():
            dk_ref[...] = jnp.zeros((N_MEM, D), F32)
            dv_ref[...] = jnp.zeros((N_MEM, D), F32)
            stat_ref[...] = jnp.zeros((8, D), F32)
            dconv_ext[tm:tm + HALO, :] = jnp.zeros((HALO, D), F32)
            dpn_ext[tm:tm + HALO, :] = jnp.zeros((HALO, D), F32)

        dmerged = _dot_nt(dx1_ref[...].astype(BF16), wout_ref[...])
        sa = _sigmoid(proj_ref[:, O_GA:O_GA + D].astype(F32))
        sp = _sigmoid(proj_ref[:, O_GP:O_GP + D].astype(F32))
        sx = _sigmoid(proj_ref[:, O_GX:O_GX + D].astype(F32))
        ya = ya_ref[...].astype(F32)
        ppv = pp_ref[...].astype(F32)
        yp = ppv * ps_ref[...]
        yx = yx_ref[...].astype(F32)
        merged_ref[...] = (sa * ya + sp * yp + sx * yx).astype(BF16)
        dproj_ref[:, O_GA:O_GA + D] = (dmerged * ya * (sa * (1.0 - sa))).astype(BF16)
        dproj_ref[:, O_GP:O_GP + D] = (dmerged * yp * (sp * (1.0 - sp))).astype(BF16)
        dproj_ref[:, O_GX:O_GX + D] = (dmerged * yx * (sx * (1.0 - sx))).astype(BF16)
        dya = (dmerged * sa).astype(BF16)
        dyp = dmerged * sp
        dyx = (dmerged * sx).astype(BF16)
        dya_ref[...] = dya
        dyx_ref[...] = dyx
        stat_ref[1:2, :] += jnp.sum(dyp * ppv, axis=0, keepdims=True)
        dpp = (dyp * ps_ref[...]).astype(BF16)
        dpp_ref[...] = dpp

        da = _dot_nt(dya, wco_ref[...])
        c_a = proj_ref[:, O_CA:O_CA + D].astype(F32)
        u_a = proj_ref[:, O_UA:O_UA + D].astype(F32)
        cu = c_a * u_a
        halo_cu = halo_ref[:, O_CA:O_CA + D].astype(F32) * halo_ref[:, O_UA:O_UA + D].astype(F32)
        cu_ext[0:HALO, :] = jnp.where(tile > 0, halo_cu, 0.0)
        cu_ext[HALO:HALO + tm, :] = cu
        cu1 = cu_ext[HALO - 1:HALO - 1 + tm, :]
        cu2 = cu_ext[HALO - 2:HALO - 2 + tm, :]
        conv = cw_ref[2:3, :] * cu + cw_ref[1:2, :] * cu1 + cw_ref[0:1, :] * cu2
        dproj_ref[:, O_BA:O_BA + D] = (da * conv).astype(BF16)
        dconv = da * proj_ref[:, O_BA:O_BA + D].astype(F32)
        stat_ref[5:6, :] += jnp.sum(dconv * cu2, axis=0, keepdims=True)
        stat_ref[6:7, :] += jnp.sum(dconv * cu1, axis=0, keepdims=True)
        stat_ref[7:8, :] += jnp.sum(dconv * cu, axis=0, keepdims=True)
        dconv_ext[0:tm, :] = dconv
        dcu = (cw_ref[2:3, :] * dconv + cw_ref[1:2, :] * dconv_ext[1:1 + tm, :]
               + cw_ref[0:1, :] * dconv_ext[2:2 + tm, :])
        dproj_ref[:, O_CA:O_CA + D] = (dcu * u_a).astype(BF16)
        dproj_ref[:, O_UA:O_UA + D] = (dcu * c_a).astype(BF16)

        for g, window in enumerate(POOL_WINDOWS):
            cols = slice(g * GROUP_DIM, (g + 1) * GROUP_DIM)
            dpooled = _dot_nt(dpp[:, cols], wpool_ref[g])
            dpn_ext[0:tm, cols] = dpooled * _inv_count(tile, tm, window)
            acc = dpn_ext[0:tm, cols]
            for j in range(1, window):
                acc = acc + dpn_ext[j:j + tm, cols]
            dproj_ref[:, O_UP + g * GROUP_DIM:O_UP + (g + 1) * GROUP_DIM] = (acc - dpooled).astype(BF16)

        do = _dot_nt(dyx, wxo_ref[...])
        for hd in range(HEADS):
            cols = slice(hd * HEAD_DIM, (hd + 1) * HEAD_DIM)
            q = proj_ref[:, O_QX + hd * HEAD_DIM:O_QX + (hd + 1) * HEAD_DIM]
            kh = k_ref[:, cols]
            p = _softmax_rows(_dot_nt(q, kh) * ATT_SCALE)
            doh = do[:, cols].astype(BF16)
            dp = _dot_nt(doh, v_ref[:, cols])
            dv_ref[:, cols] += _dot_tn(p.astype(BF16), doh)
            ds = (p * (dp - jnp.sum(dp * p, axis=-1, keepdims=True)) * ATT_SCALE).astype(BF16)
            dproj_ref[:, O_QX + hd * HEAD_DIM:O_QX + (hd + 1) * HEAD_DIM] = _dot(ds, kh).astype(BF16)
            dk_ref[:, cols] += _dot_tn(ds, q)

        dconv_ext[tm:tm + HALO, :] = dconv_ext[0:HALO, :]
        dpn_ext[tm:tm + HALO, :] = dpn_ext[0:HALO, :]

    def rev(width):
        return pl.BlockSpec((tm, width), lambda s: (n_tiles - 1 - s, 0))

    halo_spec = pl.BlockSpec((HALO, D_IN), lambda s: (jnp.maximum((n_tiles - 1 - s) * halo_blocks - 1, 0), 0))
    act = jax.ShapeDtypeStruct((t_len, D), BF16)
    kv_grad = jax.ShapeDtypeStruct((N_MEM, D), F32)
    return _carried_call(
        body, "mixer_bwd", (n_tiles,),
        in_specs=[rev(D), rev(D_IN), halo_spec, rev(D), rev(D), rev(D), rev(D),
                  _resident((8, D)), _resident((D, D)), _resident((GROUPS, GROUP_DIM, GROUP_DIM)), _resident((1, D)),
                  _resident((N_MEM, D)), _resident((N_MEM, D)), _resident((D, D)), _resident((D, D))],
        out_specs=[rev(D_IN), rev(D), rev(D), rev(D), rev(D),
                   _const((N_MEM, D)), _const((N_MEM, D)), _const((8, D))],
        out_shape=[jax.ShapeDtypeStruct((t_len, D_IN), BF16), act, act, act, act, kv_grad, kv_grad,
                   jax.ShapeDtypeStruct((8, D), F32)],
        scratch_shapes=[pltpu.VMEM((tm + HALO, D), F32)] * 3, semantics=("arbitrary",),
        args=(dx1, proj, proj, ya, pp, yx, o, conv_w8, w_co, w_pool, pool_scale, k, v, w_xo, w_out), carry=carry)


def in_bwd(dproj, w_in4, x, dx1, g_mix, carry=None):
    t_len = x.shape[0]
    tm = min(512, t_len)
    tk = D_IN // N_CHIPS

    def body(dproj_ref, w_ref, x_ref, dx1_ref, g_ref, gx_ref, stat_ref, acc_ref):
        i, j = pl.program_id(0), pl.program_id(1)

        @pl.when((i == 0) & (j == 0))
        def _():
            stat_ref[...] = jnp.zeros((8, D), F32)

        @pl.when(j == 0)
        def _():
            acc_ref[...] = jnp.zeros((tm, D), F32)

        acc_ref[...] += _dot_nt(dproj_ref[...], w_ref[...])

        @pl.when(j == N_CHIPS - 1)
        def _():
            dh = acc_ref[...]
            xv = x_ref[...]
            r = lax.rsqrt(jnp.mean(xv * xv, axis=-1, keepdims=True) + EPS)
            xh = xv * r
            stat_ref[0:1, :] += jnp.sum(dh * xh, axis=0, keepdims=True)
            dxh = dh * g_ref[...]
            gx_ref[...] = dx1_ref[...] + r * (dxh - xh * jnp.mean(dxh * xh, axis=-1, keepdims=True))

    return _carried_call(
        body, "in_bwd", (t_len // tm, N_CHIPS),
        in_specs=[pl.BlockSpec((tm, tk), lambda i, j: (i, j)),
                  pl.BlockSpec((None, D, tk), lambda i, j: (j, 0, 0)),
                  pl.BlockSpec((tm, D), lambda i, j: (i, 0)),
                  pl.BlockSpec((tm, D), lambda i, j: (i, 0)),
                  pl.BlockSpec((1, D), lambda i, j: (0, 0))],
        out_specs=[pl.BlockSpec((tm, D), lambda i, j: (i, 0)), pl.BlockSpec((8, D), lambda i, j: (0, 0))],
        out_shape=[jax.ShapeDtypeStruct((t_len, D), F32), jax.ShapeDtypeStruct((8, D), F32)],
        scratch_shapes=[pltpu.VMEM((tm, D), F32)], semantics=("arbitrary", "arbitrary"),
        args=(dproj, w_in4, x, dx1, g_mix), carry=carry)


def kv_bwd(dk, dv, memn, mem, g_mem, w_kv4):
    half = D // 2

    def body(dk_ref, dv_ref, memn_ref, mem_ref, g_ref, w_ref, gw_ref, stat_ref):
        mn = memn_ref[...]
        parts = (dk_ref[:, 0:half], dk_ref[:, half:D], dv_ref[:, 0:half], dv_ref[:, half:D])
        dmemn = jnp.zeros((N_MEM, D), F32)
        for j, part in enumerate(parts):
            part = part.astype(BF16)
            gw_ref[j] = _dot_tn(mn, part)
            dmemn = dmemn + _dot_nt(part, w_ref[j])
        mv = mem_ref[...]
        r = lax.rsqrt(jnp.mean(mv * mv, axis=-1, keepdims=True) + EPS)
        stat_ref[...] = jnp.zeros((8, D), F32)
        stat_ref[0:1, :] = jnp.sum(dmemn * (mv * r), axis=0, keepdims=True)

    return pl.pallas_call(
        body, name="kv_bwd",
        out_shape=[jax.ShapeDtypeStruct((N_CHIPS, D, half), F32), jax.ShapeDtypeStruct((8, D), F32)],
        compiler_params=_params(),
    )(dk, dv, memn, mem, g_mem, w_kv4)


def matmul_tn(name, a, b, tn, col_blocks=1, k_dim=None, n_dim=None, a_col=0, b_col=0, carry=None):
    t_len = a.shape[0]
    k_dim = a.shape[1] if k_dim is None else k_dim
    n_dim = b.shape[1] if n_dim is None else n_dim
    b_off = b_col * (n_dim // tn)
    tt = min(512, t_len)
    per_block = n_dim // col_blocks // tn

    def body(a_ref, b_ref, out_ref):
        @pl.when(pl.program_id(1) == 0)
        def _():
            out_ref[...] = jnp.zeros((k_dim, tn), F32)

        out_ref[...] += _dot_tn(a_ref[...].astype(BF16), b_ref[...].astype(BF16))

    if col_blocks == 1:
        out_spec = pl.BlockSpec((k_dim, tn), lambda n, t: (0, n))
        out_shape = jax.ShapeDtypeStruct((k_dim, n_dim), F32)
    else:
        out_spec = pl.BlockSpec((None, k_dim, tn), lambda n, t: (n // per_block, 0, n % per_block))
        out_shape = jax.ShapeDtypeStruct((col_blocks, k_dim, n_dim // col_blocks), F32)
    (out,), carried = _carried_call(
        body, name, (n_dim // tn, t_len // tt),
        in_specs=[pl.BlockSpec((tt, k_dim), lambda n, t: (t, a_col)), pl.BlockSpec((tt, tn), lambda n, t: (t, b_off + n))],
        out_specs=[out_spec], out_shape=[out_shape], scratch_shapes=[], semantics=("arbitrary", "arbitrary"),
        args=(a, b), carry=carry)
    return (out, carried) if carry is not None else out


def _place():
    x, y, c = lax.axis_index("x"), lax.axis_index("y"), lax.axis_index("c")
    return x, y, c


def _other_chips(x, y):
    return [(1 - x, y), (x, 1 - y), (1 - x, 1 - y)]


def _any_specs(n):
    return [pl.BlockSpec(memory_space=pl.ANY)] * n


def gather_weights(shards):
    n = len(shards)

    def body(*refs):
        gather = _Gather(refs[:n], refs[n:2 * n], *refs[2 * n:])
        gather.start()
        gather.finish()

    return pl.pallas_call(
        body, name="gather_weights",
        in_specs=_any_specs(n), out_specs=_any_specs(n),
        out_shape=_gathered_shapes(shards), scratch_shapes=_gather_sems(n),
    )(*shards)


def _gathered_shapes(shards):
    return [jax.ShapeDtypeStruct((N_CHIPS,) + s.shape, s.dtype) for s in shards]


def _gather_sems(n):
    return [pltpu.SemaphoreType.DMA((n, 7)), pltpu.SemaphoreType.DMA((n, 7))]


class _Gather:
    def __init__(self, ins, outs, send_sems, recv_sems):
        self.ins, self.outs, self.send_sems, self.recv_sems = ins, outs, send_sems, recv_sems
        x, y, c = _place()
        self.c, self.me, self.sibling, self.chips = c, 2 * x + y, (x, y, 1 - c), _other_chips(x, y)

    def _half(self, ref, which):
        rows = ref.shape[1]
        return ref.at[:, pl.ds(which * (rows // 2), rows // 2), :]

    def _remote(self, src, dst, a, slot, to):
        return pltpu.make_async_remote_copy(src_ref=src, dst_ref=dst, send_sem=self.send_sems.at[a, slot],
                                            recv_sem=self.recv_sems.at[a, slot], device_id=to, device_id_type=MESH)

    def _own(self, a):
        return self._remote(self.ins[a], self.outs[a].at[self.me], a, 6, self.sibling)

    def _sent(self, a, slot):
        px, py = self.chips[slot]
        return self._remote(self._half(self.ins[a], self.c), self._half(self.outs[a].at[self.me], self.c), a, slot,
                            (px, py, self.c))

    def _passed_on(self, a, slot, which):
        px, py = self.chips[slot]
        block = self._half(self.outs[a].at[2 * px + py], which)
        return self._remote(block, block, a, 3 + slot, self.sibling)

    def start(self):
        for a in range(len(self.ins)):
            self._own(a).start()
        for a in range(len(self.ins)):
            for slot in range(3):
                self._sent(a, slot).start()

    def finish(self):
        n = len(self.ins)
        for a in range(n):
            for slot in range(3):
                px, py = self.chips[slot]
                landed = self._half(self.outs[a].at[2 * px + py], self.c)
                self._remote(landed, landed, a, slot, (px, py, self.c)).wait_recv()
                self._passed_on(a, slot, self.c).start()
        for a in range(n):
            for slot in range(3):
                self._passed_on(a, slot, 1 - self.c).wait_recv()
        for a in range(n):
            self._own(a).wait_recv()
        for a in range(n):
            self._own(a).wait_send()
            for slot in range(3):
                self._sent(a, slot).wait_send()
                self._passed_on(a, slot, self.c).wait_send()


def exchange_sibling_halves(name, grads):
    n = len(grads)

    def body(*refs):
        ins, outs = refs[:n], refs[n:2 * n]
        send_sems, recv_sems = refs[2 * n:]
        x, y, c = _place()
        copies = []
        for a in range(n):
            h = ins[a].shape[1] // 2
            cp = pltpu.make_async_remote_copy(
                src_ref=ins[a].at[:, pl.ds((1 - c) * h, h), :], dst_ref=outs[a],
                send_sem=send_sems.at[a], recv_sem=recv_sems.at[a], device_id=(x, y, 1 - c), device_id_type=MESH)
            cp.start()
            copies.append(cp)
        for cp in copies:
            cp.wait()

    return pl.pallas_call(
        body, name=name,
        in_specs=_any_specs(n), out_specs=_any_specs(n),
        out_shape=[jax.ShapeDtypeStruct((N_CHIPS, g.shape[1] // 2, g.shape[2]), F32) for g in grads],
        scratch_shapes=[pltpu.SemaphoreType.DMA((n,)), pltpu.SemaphoreType.DMA((n,))],
    )(*grads)


def _row_tile(rows, cols, budget=1 << 20):
    best = 16
    for tr in range(16, rows + 1, 16):
        if rows % tr == 0 and tr * cols * 4 <= budget:
            best = tr
    return best


def add_sibling_half(name, grad, got, core):
    _, rows, cols = grad.shape
    h = rows // 2
    tr = _row_tile(h, cols)
    per_half = h // tr

    def body(core_ref, g_ref, o_ref, out_ref, out16_ref):
        total = g_ref[...] + o_ref[...]
        out_ref[...] = total
        out16_ref[...] = total.astype(BF16)

    out_spec = pl.BlockSpec((None, tr, cols), lambda j, r, cr: (j, r, 0))
    return pl.pallas_call(
        body, name=name,
        grid_spec=pltpu.PrefetchScalarGridSpec(
            num_scalar_prefetch=1, grid=(N_CHIPS, per_half),
            in_specs=[pl.BlockSpec((None, tr, cols), lambda j, r, cr: (j, cr[0] * per_half + r, 0)),
                      pl.BlockSpec((None, tr, cols), lambda j, r, cr: (j, r, 0))],
            out_specs=[out_spec, out_spec]),
        out_shape=[jax.ShapeDtypeStruct((N_CHIPS, h, cols), F32), jax.ShapeDtypeStruct((N_CHIPS, h, cols), BF16)],
        compiler_params=_params(("arbitrary", "arbitrary")),
    )(core, grad, got)


def exchange_chip_blocks(partials):
    n = len(partials)

    def body(*refs):
        exchange = _ChipExchange(refs[:n], refs[n:2 * n], *refs[2 * n:])
        exchange.start()
        exchange.finish()

    return pl.pallas_call(
        body, name="exchange_chip_blocks",
        in_specs=_any_specs(n), out_specs=_any_specs(n),
        out_shape=_exchanged_shapes(partials), scratch_shapes=_exchange_sems(n),
    )(*partials)


def _exchanged_shapes(partials):
    return [jax.ShapeDtypeStruct((3,) + p.shape[1:], p.dtype) for p in partials]


def _exchange_sems(n):
    return [pltpu.SemaphoreType.DMA((n, 3)), pltpu.SemaphoreType.DMA((n, 3))]


class _ChipExchange:
    def __init__(self, ins, outs, send_sems, recv_sems):
        self.ins, self.outs, self.send_sems, self.recv_sems = ins, outs, send_sems, recv_sems

    def _copies(self):
        x, y, c = _place()
        for a in range(len(self.ins)):
            for slot, (px, py) in enumerate(_other_chips(x, y)):
                yield pltpu.make_async_remote_copy(
                    src_ref=self.ins[a].at[2 * px + py], dst_ref=self.outs[a].at[slot],
                    send_sem=self.send_sems.at[a, slot], recv_sem=self.recv_sems.at[a, slot],
                    device_id=(px, py, c), device_id_type=MESH)

    def start(self):
        for cp in self._copies():
            cp.start()

    def finish(self):
        for cp in self._copies():
            cp.wait()


def add_chip_blocks(name, partial, got, chip):
    _, h, cols = partial.shape
    tr = _row_tile(h, cols)

    def body(chip_ref, p_ref, g0_ref, g1_ref, g2_ref, out_ref):
        out_ref[...] = ((p_ref[...] + g0_ref[...].astype(F32)) + g1_ref[...].astype(F32)) + g2_ref[...].astype(F32)

    def got_spec(slot):
        return pl.BlockSpec((None, tr, cols), lambda r, ch: (slot, r, 0))

    return pl.pallas_call(
        body, name=name,
        grid_spec=pltpu.PrefetchScalarGridSpec(
            num_scalar_prefetch=1, grid=(h // tr,),
            in_specs=[pl.BlockSpec((None, tr, cols), lambda r, ch: (ch[0], r, 0)), got_spec(0), got_spec(1), got_spec(2)],
            out_specs=pl.BlockSpec((tr, cols), lambda r, ch: (r, 0))),
        out_shape=jax.ShapeDtypeStruct((h, cols), F32),
        compiler_params=_params(("arbitrary",)),
    )(chip, partial, got, got, got)


def swap_sibling_halves(halves):
    n = len(halves)

    def body(*refs):
        ins, outs = refs[:n], refs[n:2 * n]
        send_sems, recv_sems = refs[2 * n:]
        x, y, c = _place()
        copies = []
        for a in range(n):
            cp = pltpu.make_async_remote_copy(
                src_ref=ins[a], dst_ref=outs[a], send_sem=send_sems.at[a], recv_sem=recv_sems.at[a],
                device_id=(x, y, 1 - c), device_id_type=MESH)
            cp.start()
            copies.append(cp)
        for cp in copies:
            cp.wait()

    return pl.pallas_call(
        body, name="swap_sibling_halves",
        in_specs=_any_specs(n), out_specs=_any_specs(n),
        out_shape=[jax.ShapeDtypeStruct(v.shape, F32) for v in halves],
        scratch_shapes=[pltpu.SemaphoreType.DMA((n,)), pltpu.SemaphoreType.DMA((n,))],
    )(*halves)


def all_reduce_small(pack):
    rows = pack.shape[0]

    def body(pack_ref, out_ref, gathered, send_sems, recv_sems):
        x, y, c = _place()
        me = 4 * x + 2 * y + c
        gathered[me] = pack_ref[...]
        copies = []
        for rel in range(1, 8):
            fx, fy, fc = (rel >> 2) & 1, (rel >> 1) & 1, rel & 1
            peer = (x ^ fx, y ^ fy, c ^ fc)
            cp = pltpu.make_async_remote_copy(
                src_ref=pack_ref, dst_ref=gathered.at[me], send_sem=send_sems.at[rel - 1], recv_sem=recv_sems.at[rel - 1],
                device_id=peer, device_id_type=MESH)
            cp.start()
            copies.append(cp)
        for rel in range(1, 8):
            fx, fy, fc = (rel >> 2) & 1, (rel >> 1) & 1, rel & 1
            src = 4 * (x ^ fx) + 2 * (y ^ fy) + (c ^ fc)
            pltpu.make_async_remote_copy(
                src_ref=pack_ref, dst_ref=gathered.at[src], send_sem=send_sems.at[rel - 1], recv_sem=recv_sems.at[rel - 1],
                device_id=(x, y, c), device_id_type=MESH).wait_recv()
        for cp in copies:
            cp.wait_send()
        total = gathered[0]
        for dev in range(1, 8):
            total = total + gathered[dev]
        out_ref[...] = total

    return pl.pallas_call(
        body, name="all_reduce_small",
        in_specs=[pl.BlockSpec(memory_space=pltpu.VMEM)], out_specs=pl.BlockSpec(memory_space=pltpu.VMEM),
        out_shape=jax.ShapeDtypeStruct((rows, D), F32),
        scratch_shapes=[pltpu.VMEM((8, rows, D), F32), pltpu.SemaphoreType.DMA((7,)), pltpu.SemaphoreType.DMA((7,))],
    )(pack)


def _adamw_update(w, g, m, v):
    nm = ADAM_B1 * m + (1.0 - ADAM_B1) * g
    nv = ADAM_B2 * v + (1.0 - ADAM_B2) * (g * g)
    m_hat = nm / (1.0 - ADAM_B1 ** ADAM_STEP)
    v_hat = nv / (1.0 - ADAM_B2 ** ADAM_STEP)
    delta = -ADAM_LR * (m_hat / (jnp.sqrt(v_hat) + ADAM_EPS) + ADAM_WD * w)
    return delta, nm, nv


def adamw(name, w, g, m, v):
    def body(w_ref, g_ref, m_ref, v_ref, d_ref, nm_ref, nv_ref):
        d_ref[...], nm_ref[...], nv_ref[...] = _adamw_update(w_ref[...], g_ref[...], m_ref[...], v_ref[...])

    out = jax.ShapeDtypeStruct(w.shape, F32)
    return pl.pallas_call(body, name=name, out_shape=[out] * 3, compiler_params=_params())(w, g, m, v)


def adamw_halves(name, w, mine, theirs, m, v, core):
    h, cols = mine.shape
    tr = _row_tile(h, cols)
    per_half = h // tr

    def body(core_ref, w_ref, mine_ref, theirs_ref, m_ref, v_ref, g_ref, d_ref, nm_ref, nv_ref):
        g = jnp.where(pl.program_id(0) == core_ref[0], mine_ref[...], theirs_ref[...])
        g_ref[...] = g
        d_ref[...], nm_ref[...], nv_ref[...] = _adamw_update(w_ref[...], g, m_ref[...], v_ref[...])

    full = pl.BlockSpec((tr, cols), lambda hh, r, cr: (hh * per_half + r, 0))
    mine_spec = pl.BlockSpec((tr, cols), lambda hh, r, cr: (jnp.where(hh == cr[0], r, 0), 0))
    theirs_spec = pl.BlockSpec((tr, cols), lambda hh, r, cr: (jnp.where(hh == cr[0], 0, r), 0))
    out = jax.ShapeDtypeStruct((2 * h, cols), F32)
    return pl.pallas_call(
        body, name=name,
        grid_spec=pltpu.PrefetchScalarGridSpec(
            num_scalar_prefetch=1, grid=(2, per_half),
            in_specs=[full, mine_spec, theirs_spec, full, full], out_specs=[full] * 4),
        out_shape=[out] * 4,
        compiler_params=_params(("arbitrary", "arbitrary")),
    )(core, w, mine, theirs, m, v)


BIG = ("w_in", "w_conv_out", "w_pool", "w_kv", "w_xattn_out", "w_out", "w_gate", "w_up", "w_down")


def kernel(x, mem, norm_mix, w_in, conv_w, w_conv_out, w_pool, pool_scale, norm_mem, w_kv, w_xattn_out, w_out, norm_ffn, w_gate, w_up, w_down, norm_final, loss_target, m_norm_mix, m_w_in, m_conv_w, m_w_conv_out, m_w_pool, m_pool_scale, m_norm_mem, m_w_kv, m_w_xattn_out, m_w_out, m_norm_ffn, m_w_gate, m_w_up, m_w_down, m_norm_final, v_norm_mix, v_w_in, v_conv_w, v_w_conv_out, v_w_pool, v_pool_scale, v_norm_mem, v_w_kv, v_w_xattn_out, v_w_out, v_norm_ffn, v_w_gate, v_w_up, v_w_down, v_norm_final):
    t_len = x.shape[1]
    xi, yi, ci = lax.axis_index("x"), lax.axis_index("y"), lax.axis_index("c")
    chip = 2 * xi + yi
    chip_arr = jnp.reshape(chip, (1,)).astype(jnp.int32)
    core_arr = jnp.reshape(ci, (1,)).astype(jnp.int32)

    conv_pad = jnp.concatenate([conv_w, jnp.zeros((1, 13, 256), F32)], axis=1)
    def t2(w):
        return jnp.swapaxes(w, 1, 2)

    g_in, g_kv, g_conv_w = gather_weights([w_in.astype(BF16), w_kv.astype(BF16), conv_pad])
    w_in4 = g_in.reshape(N_CHIPS, D, D_IN // N_CHIPS)
    w_kv4 = g_kv.reshape(N_CHIPS, D, D // 2)
    conv_full = jnp.transpose(g_conv_w[:, 0, 0:8, :], (1, 0, 2)).reshape(8, D)

    x2d = x[0]
    tgt = loss_target[0]
    (proj, h), (g_co, g_xo, g_out, g_pool, g_gate) = proj_fwd(
        x2d, norm_mix, w_in4,
        carry=(_Gather, [w_conv_out.astype(BF16), w_xattn_out.astype(BF16), w_out.astype(BF16),
                         w_pool[0].astype(BF16),
                         t2(w_gate).astype(BF16)]))
    w_co_f, w_xo_f, w_out_f = g_co.reshape(D, D), g_xo.reshape(D, D), g_out.reshape(D, D)
    w_pool_f = jnp.transpose(g_pool, (1, 0, 2, 3)).reshape(GROUPS, GROUP_DIM, GROUP_DIM)
    memn, k, v = kv_fwd(mem[0], norm_mem, w_kv4)
    (a, pooled, ya, pp, yx, o, x1), (g_up, g_down) = mixer_fwd(
        proj, x2d, conv_full, w_co_f, w_pool_f, pool_scale, k, v, w_xo_f, w_out_f,
        carry=(_Gather, [t2(w_up).astype(BF16), w_down.astype(BF16)]))
    w_gate_f, w_up_f, w_down_f = g_gate.reshape(D_FF, D), g_up.reshape(D_FF, D), g_down.reshape(D_FF, D)
    gate, up, dx2, stat_f = ffn_fwd(x1, tgt, norm_ffn, w_gate_f, w_up_f, w_down_f, norm_final.reshape(1, D))

    def by_chip(gw):
        return gw.reshape(N_CHIPS, gw.shape[0] // N_CHIPS, gw.shape[1])

    def chip_partials(tag, names, grads):
        got = exchange_sibling_halves("exchange_sibling_halves_" + tag, grads)
        return [add_sibling_half("add_sibling_" + n, g, o_, core_arr) for n, g, o_ in zip(names, grads, got)]

    dx1, dgate, dup, act, h2, stat_b1 = ffn_bwd(dx2, x1, gate, up, norm_ffn, w_gate_f, w_up_f, w_down_f)
    gw_gate = matmul_tn("grad_w_gate", dgate, h2, 512)
    gw_up = matmul_tn("grad_w_up", dup, h2, 512)
    gw_down = matmul_tn("grad_w_down", act, dx2, 512)
    names_ffn = ("w_gate", "w_up", "w_down")
    part_ffn = chip_partials("ffn", names_ffn, [by_chip(gw_gate), by_chip(gw_up), by_chip(gw_down)])

    (dproj, merged, dya, dpp, dyx, dk, dv, stat_b2), got_ffn = mixer_bwd(
        dx1, proj, ya, pp, yx, o, conv_full, w_co_f, w_pool_f, pool_scale, k, v, w_xo_f, w_out_f,
        carry=(_ChipExchange, [p16 for _, p16 in part_ffn]))
    gw_kv, stat_kv = kv_bwd(dk, dv, memn, mem[0], norm_mem, w_kv4)
    gw_co = matmul_tn("grad_w_conv_out", a, dya, 1024)
    gw_xo = matmul_tn("grad_w_xattn_out", o, dyx, 1024)
    gw_out = matmul_tn("grad_w_out", merged, dx1, 1024)
    gw_pool = jnp.stack([matmul_tn("grad_w_pool_%d" % g, pooled, dpp, GROUP_DIM, k_dim=GROUP_DIM, n_dim=GROUP_DIM,
                                   a_col=g, b_col=g) for g in range(GROUPS)])
    gw_pool = jnp.transpose(gw_pool.reshape(GROUPS, N_CHIPS, 64, GROUP_DIM), (1, 0, 2, 3)).reshape(N_CHIPS, 256, GROUP_DIM)
    names_mix = ("w_conv_out", "w_pool", "w_kv", "w_xattn_out", "w_out")
    part_mix = chip_partials("mixer", names_mix, [by_chip(gw_co), gw_pool, gw_kv, by_chip(gw_xo), by_chip(gw_out)])

    gw_in, got_mix = matmul_tn("grad_w_in", h, dproj, 2048, col_blocks=N_CHIPS,
                               carry=(_ChipExchange, [p16 for _, p16 in part_mix]))
    part_in = chip_partials("in", ("w_in",), [gw_in])
    (grad_x, stat_b3), got_in = in_bwd(dproj, w_in4, x2d, dx1, norm_mix,
                                       carry=(_ChipExchange, [p16 for _, p16 in part_in]))

    partials = dict(zip(names_ffn + names_mix + ("w_in",), part_ffn + part_mix + part_in))
    got2 = dict(zip(names_ffn + names_mix + ("w_in",), list(got_ffn) + list(got_mix) + list(got_in)))
    mine = [add_chip_blocks("add_chips_" + n, partials[n][0], got2[n], chip_arr) for n in BIG]
    theirs = swap_sibling_halves(mine)

    pack = jnp.concatenate([stat_b3[0:1], stat_b2[1:2], stat_kv[0:1], stat_b1[0:1], stat_f[0:1], stat_b2[5:8],
                            stat_f[1:2], jnp.zeros((7, D), F32)], axis=0)
    total = all_reduce_small(pack)
    loss = jnp.sum(total[8])
    g_conv_full = total[5:8]
    g_conv = lax.dynamic_slice_in_dim(g_conv_full, chip * 256, 256, axis=1)

    given = dict(w_in=(w_in, m_w_in, v_w_in), w_conv_out=(w_conv_out, m_w_conv_out, v_w_conv_out),
                 w_pool=(w_pool, m_w_pool, v_w_pool), w_kv=(w_kv, m_w_kv, v_w_kv),
                 w_xattn_out=(w_xattn_out, m_w_xattn_out, v_w_xattn_out), w_out=(w_out, m_w_out, v_w_out),
                 w_gate=(w_gate, m_w_gate, v_w_gate), w_up=(w_up, m_w_up, v_w_up), w_down=(w_down, m_w_down, v_w_down))
    out_g, out_d, out_m, out_v = {}, {}, {}, {}
    for n, mine_n, theirs_n in zip(BIG, mine, theirs):
        transposed = n in ("w_gate", "w_up")
        rows2d = (2 * mine_n.shape[0], mine_n.shape[1])
        w_, m_, v_ = ((t2(t) if transposed else t).reshape(rows2d) for t in given[n])
        res = adamw_halves("adamw_" + n, w_, mine_n, theirs_n, m_, v_, core_arr)
        if transposed:
            res = [t2(t.reshape(1, D_FF // N_CHIPS, D)) for t in res]
        out_g[n], out_d[n], out_m[n], out_v[n] = (t.reshape(given[n][0].shape) for t in res)

    def small_pack(vals, conv_part):
        conv_rows = jnp.concatenate([conv_part.reshape(3, 256), jnp.zeros((3, D - 256), F32)], axis=1)
        return jnp.concatenate([val.reshape(1, D) for val in vals] + [conv_rows], axis=0)

    sw = small_pack([norm_mix, pool_scale, norm_mem, norm_ffn, norm_final], conv_w)
    sm = small_pack([m_norm_mix, m_pool_scale, m_norm_mem, m_norm_ffn, m_norm_final], m_conv_w)
    sv = small_pack([v_norm_mix, v_pool_scale, v_norm_mem, v_norm_ffn, v_norm_final], v_conv_w)
    sg = small_pack([total[r] for r in range(5)], g_conv)
    sd, snm, snv = adamw("adamw_small", sw, sg, sm, sv)
    small_names = ("norm_mix", "pool_scale", "norm_mem", "norm_ffn", "norm_final")
    small_shapes = dict(norm_mix=(1, D), pool_scale=(1, D), norm_mem=(1, D), norm_ffn=(1, D), norm_final=(D,))
    for r, n in enumerate(small_names):
        out_g[n], out_d[n], out_m[n], out_v[n] = (t[r].reshape(small_shapes[n]) for t in (sg, sd, snm, snv))
    out_g["conv_w"], out_d["conv_w"], out_m["conv_w"], out_v["conv_w"] = (
        t[5:8, 0:256].reshape(1, 3, 256) for t in (sg, sd, snm, snv))

    order = ("norm_mix", "w_in", "conv_w", "w_conv_out", "w_pool", "pool_scale", "norm_mem", "w_kv", "w_xattn_out",
             "w_out", "norm_ffn", "w_gate", "w_up", "w_down", "norm_final")
    return (loss, grad_x.reshape(1, t_len, D), *[out_g[n] for n in order], *[out_d[n] for n in order],
            *[out_m[n] for n in order], *[out_v[n] for n in order])
```

```python
import functools

import jax
import jax.numpy as jnp
from jax import lax
from jax.experimental import pallas as pl
from jax.experimental.pallas import tpu as pltpu

F32 = jnp.float32
BF16 = jnp.bfloat16
MESH = pl.DeviceIdType.MESH

D = 1024
N_MEM = 256
HEADS = 4
HEAD_DIM = 256
GROUPS = 4
GROUP_DIM = 256
POOL_WINDOWS = (2, 4, 8, 16)
D_FF = 2816
D_IN = 8192
N_CHIPS = 4
EPS = 1e-6
HALO = 16
POOL_PAD = 128
ATT_SCALE = HEAD_DIM ** -0.5

ADAM_LR = 0.001
ADAM_B1 = 0.9
ADAM_B2 = 0.999
ADAM_EPS = 1e-08
ADAM_WD = 0.01
ADAM_STEP = 10

VMEM_LIMIT = 56 * 1024 * 1024

O_BA, O_CA, O_UA, O_UP, O_QX, O_GA, O_GP, O_GX = (k * D for k in range(8))

NT_DIMS = (((1,), (1,)), ((), ()))
TN_DIMS = (((0,), (0,)), ((), ()))


def _dot(a, b):
    return jnp.dot(a, b, preferred_element_type=F32)


def _dot_nt(a, b):
    return lax.dot_general(a, b, NT_DIMS, preferred_element_type=F32)


def _dot_tn(a, b):
    return lax.dot_general(a, b, TN_DIMS, preferred_element_type=F32)


def _sigmoid(z):
    return pl.reciprocal(1.0 + jnp.exp(-z), approx=True)


def _params(semantics=None):
    return pltpu.CompilerParams(dimension_semantics=semantics, vmem_limit_bytes=VMEM_LIMIT)


def _resident(shape):
    zeros = (0,) * len(shape)
    return pl.BlockSpec(shape, lambda *_: zeros, pipeline_mode=pl.Buffered(1))


def _const(shape):
    zeros = (0,) * len(shape)
    return pl.BlockSpec(shape, lambda *_: zeros)


def _rows(tm, width):
    return pl.BlockSpec((tm, width), lambda i: (i, 0))


def _inv_count(tile, tm, window):
    t = tile * tm + lax.broadcasted_iota(jnp.int32, (tm, 1), 0)
    return 1.0 / jnp.minimum(t + 1, window).astype(F32)


def _carried_call(body, name, grid, in_specs, out_specs, out_shape, scratch_shapes, semantics, args, carry):
    if carry is None:
        res = pl.pallas_call(body, name=name, grid=grid, in_specs=in_specs, out_specs=out_specs, out_shape=out_shape,
                             scratch_shapes=scratch_shapes, compiler_params=_params(semantics))(*args)
        return res, []
    comm_cls, comm_args = carry
    n, n_in, n_out, n_scratch = len(comm_args), len(in_specs), len(out_specs), len(scratch_shapes)
    comm_shapes = _gathered_shapes(comm_args) if comm_cls is _Gather else _exchanged_shapes(comm_args)
    comm_sems = _gather_sems(n) if comm_cls is _Gather else _exchange_sems(n)

    def carrying(*refs):
        ins, comm_ins = refs[:n_in], refs[n_in:n_in + n]
        outs, comm_outs = refs[n_in + n:n_in + n + n_out], refs[n_in + n + n_out:n_in + 2 * n + n_out]
        scratch, sems = refs[n_in + 2 * n + n_out:n_in + 2 * n + n_out + n_scratch], refs[n_in + 2 * n + n_out + n_scratch:]
        steps = [pl.program_id(d) for d in range(len(grid))]
        first = functools.reduce(jnp.logical_and, [s == 0 for s in steps])
        last = functools.reduce(jnp.logical_and, [s == g - 1 for s, g in zip(steps, grid)])

        @pl.when(first)
        def _():
            comm_cls(comm_ins, comm_outs, *sems).start()

        body(*ins, *outs, *scratch)

        @pl.when(last)
        def _():
            comm_cls(comm_ins, comm_outs, *sems).finish()

    res = pl.pallas_call(
        carrying, name=name, grid=grid, in_specs=list(in_specs) + _any_specs(n), out_specs=list(out_specs) + _any_specs(n),
        out_shape=list(out_shape) + comm_shapes, scratch_shapes=list(scratch_shapes) + comm_sems,
        compiler_params=_params(semantics))(*args, *comm_args)
    return res[:n_out], res[n_out:]


def proj_fwd(x, g_mix, w_in4, carry=None):
    t_len = x.shape[0]
    tm = min(1024, t_len)
    tn = D_IN // N_CHIPS

    def body(x_ref, g_ref, w_ref, proj_ref, h_ref):
        @pl.when(pl.program_id(1) == 0)
        def _():
            xv = x_ref[...]
            r = lax.rsqrt(jnp.mean(xv * xv, axis=-1, keepdims=True) + EPS)
            h_ref[...] = (xv * r * g_ref[...]).astype(BF16)

        proj_ref[...] = _dot(h_ref[...], w_ref[...]).astype(BF16)

    return _carried_call(
        body, "proj_fwd", (t_len // tm, N_CHIPS),
        in_specs=[pl.BlockSpec((tm, D), lambda i, j: (i, 0)),
                  pl.BlockSpec((1, D), lambda i, j: (0, 0)),
                  pl.BlockSpec((None, D, tn), lambda i, j: (j, 0, 0))],
        out_specs=[pl.BlockSpec((tm, tn), lambda i, j: (i, j)),
                   pl.BlockSpec((tm, D), lambda i, j: (i, 0))],
        out_shape=[jax.ShapeDtypeStruct((t_len, D_IN), BF16), jax.ShapeDtypeStruct((t_len, D), BF16)],
        scratch_shapes=[], semantics=("arbitrary", "arbitrary"), args=(x, g_mix, w_in4), carry=carry)


def kv_fwd(mem, g_mem, w_kv4):
    half = D // 2

    def body(mem_ref, g_ref, w_ref, memn_ref, k_ref, v_ref):
        mv = mem_ref[...]
        r = lax.rsqrt(jnp.mean(mv * mv, axis=-1, keepdims=True) + EPS)
        mn = (mv * r * g_ref[...]).astype(BF16)
        memn_ref[...] = mn
        k_ref[:, 0:half] = _dot(mn, w_ref[0]).astype(BF16)
        k_ref[:, half:D] = _dot(mn, w_ref[1]).astype(BF16)
        v_ref[:, 0:half] = _dot(mn, w_ref[2]).astype(BF16)
        v_ref[:, half:D] = _dot(mn, w_ref[3]).astype(BF16)

    out = jax.ShapeDtypeStruct((N_MEM, D), BF16)
    return pl.pallas_call(body, name="kv_fwd", out_shape=[out, out, out], compiler_params=_params())(mem, g_mem, w_kv4)


def _softmax_rows(s):
    m = jnp.max(s, axis=-1, keepdims=True)
    e = jnp.exp(s - m)
    return e * pl.reciprocal(jnp.sum(e, axis=-1, keepdims=True), approx=True)


def _window_bands(tm, causal):
    t = lax.broadcasted_iota(jnp.int32, (tm, tm + POOL_PAD), 0)
    s = lax.broadcasted_iota(jnp.int32, (tm, tm + POOL_PAD), 1)
    d = (t + POOL_PAD - s) if causal else (s - t)
    return jnp.stack([((d >= 0) & (d < w)).astype(BF16) for w in POOL_WINDOWS])


def mixer_fwd(proj, x, conv_w8, w_co, w_pool, pool_scale, k, v, w_xo, w_out, carry=None):
    t_len = x.shape[0]
    tm = min(256, t_len)

    def body(proj_ref, x_ref, cw_ref, wco_ref, wpool_ref, ps_ref, k_ref, v_ref, wxo_ref, wout_ref,
             a_ref, pooled_ref, ya_ref, pp_ref, yx_ref, o_ref, p_ref, x1_ref, cu_ext, up_ext):
        i = pl.program_id(0)

        @pl.when(i == 0)
        def _():
            cu_ext[0:HALO, :] = jnp.zeros((HALO, D), F32)
            up_ext[0:HALO, :] = jnp.zeros((HALO, D), F32)

        cu = proj_ref[:, O_CA:O_CA + D].astype(F32) * proj_ref[:, O_UA:O_UA + D].astype(F32)
        cu_ext[HALO:HALO + tm, :] = cu
        conv = (cw_ref[2:3, :] * cu + cw_ref[1:2, :] * cu_ext[HALO - 1:HALO - 1 + tm, :]
                + cw_ref[0:1, :] * cu_ext[HALO - 2:HALO - 2 + tm, :])
        a = (proj_ref[:, O_BA:O_BA + D].astype(F32) * conv).astype(BF16)
        a_ref[...] = a
        ya = _dot(a, wco_ref[...])
        ya_ref[...] = ya.astype(BF16)

        up_ext[HALO:HALO + tm, :] = proj_ref[:, O_UP:O_UP + D].astype(F32)
        for g, window in enumerate(POOL_WINDOWS):
            cols = slice(g * GROUP_DIM, (g + 1) * GROUP_DIM)
            tok = up_ext[HALO:HALO + tm, cols]
            acc = tok
            for j in range(1, window):
                acc = acc + up_ext[HALO - j:HALO - j + tm, cols]
            pooled = (acc * _inv_count(i, tm, window) - tok).astype(BF16)
            pooled_ref[:, cols] = pooled
            pp_ref[:, cols] = _dot(pooled, wpool_ref[g]).astype(BF16)

        for hd in range(HEADS):
            cols = slice(hd * HEAD_DIM, (hd + 1) * HEAD_DIM)
            q = proj_ref[:, O_QX + hd * HEAD_DIM:O_QX + (hd + 1) * HEAD_DIM]
            p = _softmax_rows(_dot_nt(q, k_ref[:, cols]) * ATT_SCALE).astype(BF16)
            p_ref[:, hd * N_MEM:(hd + 1) * N_MEM] = p
            o_ref[:, cols] = _dot(p, v_ref[:, cols]).astype(BF16)
        yx = _dot(o_ref[...], wxo_ref[...])
        yx_ref[...] = yx.astype(BF16)

        merged = (_sigmoid(proj_ref[:, O_GA:O_GA + D].astype(F32)) * ya
                  + _sigmoid(proj_ref[:, O_GP:O_GP + D].astype(F32)) * (pp_ref[...].astype(F32) * ps_ref[...])
                  + _sigmoid(proj_ref[:, O_GX:O_GX + D].astype(F32)) * yx)
        x1_ref[...] = x_ref[...] + _dot(merged.astype(BF16), wout_ref[...])

        cu_ext[0:HALO, :] = cu_ext[tm:tm + HALO, :]
        up_ext[0:HALO, :] = up_ext[tm:tm + HALO, :]

    act = jax.ShapeDtypeStruct((t_len, D), BF16)
    return _carried_call(
        body, "mixer_fwd", (t_len // tm,),
        in_specs=[_rows(tm, D_IN), _rows(tm, D), _resident((8, D)), _resident((D, D)),
                  _resident((GROUPS, GROUP_DIM, GROUP_DIM)), _resident((1, D)),
                  _resident((N_MEM, D)), _resident((N_MEM, D)), _resident((D, D)), _resident((D, D))],
        out_specs=[_rows(tm, D)] * 7 + [_rows(tm, D)],
        out_shape=[act] * 6 + [jax.ShapeDtypeStruct((t_len, HEADS * N_MEM), BF16), jax.ShapeDtypeStruct((t_len, D), F32)],
        scratch_shapes=[pltpu.VMEM((tm + HALO, D), F32), pltpu.VMEM((tm + HALO, D), F32)],
        semantics=("arbitrary",), args=(proj, x, conv_w8, w_co, w_pool, pool_scale, k, v, w_xo, w_out), carry=carry)


def ffn_fwd(x1, target, g_ffn, w_gate, w_up, w_down, g_final):
    t_len = x1.shape[0]
    tm = min(256, t_len)

    def body(x1_ref, tgt_ref, g_ref, wg_ref, wu_ref, wd_ref, gf_ref, gate_ref, up_ref, dx2_ref, stat_ref):
        @pl.when(pl.program_id(0) == 0)
        def _():
            stat_ref[...] = jnp.zeros((8, D), F32)

        x1v = x1_ref[...]
        r2 = lax.rsqrt(jnp.mean(x1v * x1v, axis=-1, keepdims=True) + EPS)
        h2 = (x1v * r2 * g_ref[...]).astype(BF16)
        gate = _dot_nt(h2, wg_ref[...])
        up = _dot_nt(h2, wu_ref[...])
        gate_ref[...] = gate.astype(BF16)
        up_ref[...] = up.astype(BF16)
        act = (gate * _sigmoid(gate) * up).astype(BF16)
        x2 = x1v + _dot(act, wd_ref[...])
        r3 = lax.rsqrt(jnp.mean(x2 * x2, axis=-1, keepdims=True) + EPS)
        xh = x2 * r3
        diff = xh * gf_ref[...] - tgt_ref[...]
        dy = diff * (1.0 / D)
        stat_ref[0:1, :] += jnp.sum(dy * xh, axis=0, keepdims=True)
        stat_ref[1:2, :] += (0.5 / D) * jnp.sum(diff * diff, axis=0, keepdims=True)
        dxh = dy * gf_ref[...]
        dx2_ref[...] = r3 * (dxh - xh * jnp.mean(dxh * xh, axis=-1, keepdims=True))

    return pl.pallas_call(
        body, name="ffn_fwd",
        grid=(t_len // tm,),
        in_specs=[_rows(tm, D), _rows(tm, D), _resident((1, D)), _resident((D_FF, D)), _resident((D_FF, D)),
                  _resident((D_FF, D)), _resident((1, D))],
        out_specs=[_rows(tm, D_FF), _rows(tm, D_FF), _rows(tm, D), _const((8, D))],
        out_shape=[jax.ShapeDtypeStruct((t_len, D_FF), BF16), jax.ShapeDtypeStruct((t_len, D_FF), BF16),
                   jax.ShapeDtypeStruct((t_len, D), F32), jax.ShapeDtypeStruct((8, D), F32)],
        compiler_params=_params(("arbitrary",)),
    )(x1, target, g_ffn, w_gate, w_up, w_down, g_final)


def ffn_bwd(dx2, x1, gate, up, g_ffn, w_gate, w_up, w_down):
    t_len = x1.shape[0]
    tm = min(256, t_len)

    def body(dx2_ref, x1_ref, gate_ref, up_ref, g_ref, wg_ref, wu_ref, wd_ref,
             dx1_ref, dgate_ref, dup_ref, act_ref, h2_ref, stat_ref):
        @pl.when(pl.program_id(0) == 0)
        def _():
            stat_ref[...] = jnp.zeros((8, D), F32)

        dx2v = dx2_ref[...]
        gate = gate_ref[...].astype(F32)
        upv = up_ref[...].astype(F32)
        sg = _sigmoid(gate)
        silu = gate * sg
        act_ref[...] = (silu * upv).astype(BF16)
        dact = _dot_nt(dx2v.astype(BF16), wd_ref[...])
        dup = (dact * silu).astype(BF16)
        dgate = (dact * upv * (sg * (1.0 + gate * (1.0 - sg)))).astype(BF16)
        dup_ref[...] = dup
        dgate_ref[...] = dgate
        dh2 = _dot(dgate, wg_ref[...]) + _dot(dup, wu_ref[...])
        x1v = x1_ref[...]
        r2 = lax.rsqrt(jnp.mean(x1v * x1v, axis=-1, keepdims=True) + EPS)
        xh = x1v * r2
        h2_ref[...] = (xh * g_ref[...]).astype(BF16)
        stat_ref[0:1, :] += jnp.sum(dh2 * xh, axis=0, keepdims=True)
        dxh = dh2 * g_ref[...]
        dx1_ref[...] = dx2v + r2 * (dxh - xh * jnp.mean(dxh * xh, axis=-1, keepdims=True))

    ff = jax.ShapeDtypeStruct((t_len, D_FF), BF16)
    return pl.pallas_call(
        body, name="ffn_bwd",
        grid=(t_len // tm,),
        in_specs=[_rows(tm, D), _rows(tm, D), _rows(tm, D_FF), _rows(tm, D_FF), _resident((1, D)),
                  _resident((D_FF, D)), _resident((D_FF, D)), _resident((D_FF, D))],
        out_specs=[_rows(tm, D), _rows(tm, D_FF), _rows(tm, D_FF), _rows(tm, D_FF), _rows(tm, D), _const((8, D))],
        out_shape=[jax.ShapeDtypeStruct((t_len, D), F32), ff, ff, ff, jax.ShapeDtypeStruct((t_len, D), BF16),
                   jax.ShapeDtypeStruct((8, D), F32)],
        compiler_params=_params(("arbitrary",)),
    )(dx2, x1, gate, up, g_ffn, w_gate, w_up, w_down)


def mixer_bwd(dx1, proj, ya, pp, yx, probs, conv_w8, w_co, w_pool, pool_scale, k, v, w_xo, w_out, carry=None):
    t_len = dx1.shape[0]
    tm = min(256, t_len)
    n_tiles = t_len // tm
    halo_blocks = tm // HALO

    def body(dx1_ref, proj_ref, halo_ref, ya_ref, pp_ref, yx_ref, p_ref,
             cw_ref, wco_ref, wpool_ref, ps_ref, k_ref, v_ref, wxo_ref, wout_ref, band_ref,
             dproj_ref, merged_ref, dya_ref, dpp_ref, dyx_ref, dk_ref, dv_ref, stat_ref,
             cu_ext, dconv_ext, dpn_ext):
        step = pl.program_id(0)
        tile = n_tiles - 1 - step

        @pl.when(step == 0)
        def _():
            dk_ref[...] = jnp.zeros((N_MEM, D), F32)
            dv_ref[...] = jnp.zeros((N_MEM, D), F32)
            stat_ref[...] = jnp.zeros((8, D), F32)
            dconv_ext[tm:tm + HALO, :] = jnp.zeros((HALO, D), F32)
            dpn_ext[tm:tm + POOL_PAD, :] = jnp.zeros((POOL_PAD, D), BF16)

        dmerged = _dot_nt(dx1_ref[...].astype(BF16), wout_ref[...])
        sa = _sigmoid(proj_ref[:, O_GA:O_GA + D].astype(F32))
        sp = _sigmoid(proj_ref[:, O_GP:O_GP + D].astype(F32))
        sx = _sigmoid(proj_ref[:, O_GX:O_GX + D].astype(F32))
        ya = ya_ref[...].astype(F32)
        ppv = pp_ref[...].astype(F32)
        yp = ppv * ps_ref[...]
        yx = yx_ref[...].astype(F32)
        merged_ref[...] = (sa * ya + sp * yp + sx * yx).astype(BF16)
        dproj_ref[:, O_GA:O_GA + D] = (dmerged * ya * (sa * (1.0 - sa))).astype(BF16)
        dproj_ref[:, O_GP:O_GP + D] = (dmerged * yp * (sp * (1.0 - sp))).astype(BF16)
        dproj_ref[:, O_GX:O_GX + D] = (dmerged * yx * (sx * (1.0 - sx))).astype(BF16)
        dya = (dmerged * sa).astype(BF16)
        dyp = dmerged * sp
        dyx = (dmerged * sx).astype(BF16)
        dya_ref[...] = dya
        dyx_ref[...] = dyx
        stat_ref[1:2, :] += jnp.sum(dyp * ppv, axis=0, keepdims=True)
        dpp = (dyp * ps_ref[...]).astype(BF16)
        dpp_ref[...] = dpp

        da = _dot_nt(dya, wco_ref[...])
        c_a = proj_ref[:, O_CA:O_CA + D].astype(F32)
        u_a = proj_ref[:, O_UA:O_UA + D].astype(F32)
        cu = c_a * u_a
        halo_cu = halo_ref[:, O_CA:O_CA + D].astype(F32) * halo_ref[:, O_UA:O_UA + D].astype(F32)
        cu_ext[0:HALO, :] = jnp.where(tile > 0, halo_cu, 0.0)
        cu_ext[HALO:HALO + tm, :] = cu
        cu1 = cu_ext[HALO - 1:HALO - 1 + tm, :]
        cu2 = cu_ext[HALO - 2:HALO - 2 + tm, :]
        conv = cw_ref[2:3, :] * cu + cw_ref[1:2, :] * cu1 + cw_ref[0:1, :] * cu2
        dproj_ref[:, O_BA:O_BA + D] = (da * conv).astype(BF16)
        dconv = da * proj_ref[:, O_BA:O_BA + D].astype(F32)
        stat_ref[5:6, :] += jnp.sum(dconv * cu2, axis=0, keepdims=True)
        stat_ref[6:7, :] += jnp.sum(dconv * cu1, axis=0, keepdims=True)
        stat_ref[7:8, :] += jnp.sum(dconv * cu, axis=0, keepdims=True)
        dconv_ext[0:tm, :] = dconv
        dcu = (cw_ref[2:3, :] * dconv + cw_ref[1:2, :] * dconv_ext[1:1 + tm, :]
               + cw_ref[0:1, :] * dconv_ext[2:2 + tm, :])
        dproj_ref[:, O_CA:O_CA + D] = (dcu * u_a).astype(BF16)
        dproj_ref[:, O_UA:O_UA + D] = (dcu * c_a).astype(BF16)

        for g, window in enumerate(POOL_WINDOWS):
            cols = slice(g * GROUP_DIM, (g + 1) * GROUP_DIM)
            dpooled = _dot_nt(dpp[:, cols], wpool_ref[g])
            dpn_ext[0:tm, cols] = (dpooled * _inv_count(tile, tm, window)).astype(BF16)
            acc = _dot(band_ref[g], dpn_ext[:, cols])
            dproj_ref[:, O_UP + g * GROUP_DIM:O_UP + (g + 1) * GROUP_DIM] = (acc - dpooled).astype(BF16)

        do = _dot_nt(dyx, wxo_ref[...])
        for hd in range(HEADS):
            cols = slice(hd * HEAD_DIM, (hd + 1) * HEAD_DIM)
            q = proj_ref[:, O_QX + hd * HEAD_DIM:O_QX + (hd + 1) * HEAD_DIM]
            kh = k_ref[:, cols]
            p16 = p_ref[:, hd * N_MEM:(hd + 1) * N_MEM]
            p = p16.astype(F32)
            doh = do[:, cols].astype(BF16)
            dp = _dot_nt(doh, v_ref[:, cols])
            dv_ref[:, cols] += _dot_tn(p16, doh)
            ds = (p * (dp - jnp.sum(dp * p, axis=-1, keepdims=True)) * ATT_SCALE).astype(BF16)
            dproj_ref[:, O_QX + hd * HEAD_DIM:O_QX + (hd + 1) * HEAD_DIM] = _dot(ds, kh).astype(BF16)
            dk_ref[:, cols] += _dot_tn(ds, q)

        dconv_ext[tm:tm + HALO, :] = dconv_ext[0:HALO, :]
        dpn_ext[tm:tm + HALO, :] = dpn_ext[0:HALO, :]

    def rev(width):
        return pl.BlockSpec((tm, width), lambda s: (n_tiles - 1 - s, 0))

    halo_spec = pl.BlockSpec((HALO, D_IN), lambda s: (jnp.maximum((n_tiles - 1 - s) * halo_blocks - 1, 0), 0))
    act = jax.ShapeDtypeStruct((t_len, D), BF16)
    kv_grad = jax.ShapeDtypeStruct((N_MEM, D), F32)
    return _carried_call(
        body, "mixer_bwd", (n_tiles,),
        in_specs=[rev(D), rev(D_IN), halo_spec, rev(D), rev(D), rev(D), rev(D),
                  _resident((8, D)), _resident((D, D)), _resident((GROUPS, GROUP_DIM, GROUP_DIM)), _resident((1, D)),
                  _resident((N_MEM, D)), _resident((N_MEM, D)), _resident((D, D)), _resident((D, D)),
                  _resident((GROUPS, tm, tm + POOL_PAD))],
        out_specs=[rev(D_IN), rev(D), rev(D), rev(D), rev(D),
                   _const((N_MEM, D)), _const((N_MEM, D)), _const((8, D))],
        out_shape=[jax.ShapeDtypeStruct((t_len, D_IN), BF16), act, act, act, act, kv_grad, kv_grad,
                   jax.ShapeDtypeStruct((8, D), F32)],
        scratch_shapes=[pltpu.VMEM((tm + HALO, D), F32)] * 2 + [pltpu.VMEM((tm + POOL_PAD, D), BF16)],
        semantics=("arbitrary",),
        args=(dx1, proj, proj, ya, pp, yx, probs, conv_w8, w_co, w_pool, pool_scale, k, v, w_xo, w_out,
              _window_bands(tm, False)), carry=carry)


def in_bwd(dproj, w_in4, x, dx1, g_mix, carry=None):
    t_len = x.shape[0]
    tm = min(1024, t_len)
    tk = D_IN // N_CHIPS

    def body(dproj_ref, w_ref, x_ref, dx1_ref, g_ref, gx_ref, stat_ref, acc_ref):
        i, j = pl.program_id(0), pl.program_id(1)

        @pl.when((i == 0) & (j == 0))
        def _():
            stat_ref[...] = jnp.zeros((8, D), F32)

        @pl.when(j == 0)
        def _():
            acc_ref[...] = jnp.zeros((tm, D), F32)

        acc_ref[...] += _dot_nt(dproj_ref[...], w_ref[...])

        @pl.when(j == N_CHIPS - 1)
        def _():
            dh = acc_ref[...]
            xv = x_ref[...]
            r = lax.rsqrt(jnp.mean(xv * xv, axis=-1, keepdims=True) + EPS)
            xh = xv * r
            stat_ref[0:1, :] += jnp.sum(dh * xh, axis=0, keepdims=True)
            dxh = dh * g_ref[...]
            gx_ref[...] = dx1_ref[...] + r * (dxh - xh * jnp.mean(dxh * xh, axis=-1, keepdims=True))

    return _carried_call(
        body, "in_bwd", (t_len // tm, N_CHIPS),
        in_specs=[pl.BlockSpec((tm, tk), lambda i, j: (i, j)),
                  pl.BlockSpec((None, D, tk), lambda i, j: (j, 0, 0)),
                  pl.BlockSpec((tm, D), lambda i, j: (i, 0)),
                  pl.BlockSpec((tm, D), lambda i, j: (i, 0)),
                  pl.BlockSpec((1, D), lambda i, j: (0, 0))],
        out_specs=[pl.BlockSpec((tm, D), lambda i, j: (i, 0)), pl.BlockSpec((8, D), lambda i, j: (0, 0))],
        out_shape=[jax.ShapeDtypeStruct((t_len, D), F32), jax.ShapeDtypeStruct((8, D), F32)],
        scratch_shapes=[pltpu.VMEM((tm, D), F32)], semantics=("arbitrary", "arbitrary"),
        args=(dproj, w_in4, x, dx1, g_mix), carry=carry)


def kv_bwd(dk, dv, memn, mem, g_mem, w_kv4):
    half = D // 2

    def body(dk_ref, dv_ref, memn_ref, mem_ref, g_ref, w_ref, gw_ref, stat_ref):
        mn = memn_ref[...]
        parts = (dk_ref[:, 0:half], dk_ref[:, half:D], dv_ref[:, 0:half], dv_ref[:, half:D])
        dmemn = jnp.zeros((N_MEM, D), F32)
        for j, part in enumerate(parts):
            part = part.astype(BF16)
            gw_ref[j] = _dot_tn(mn, part)
            dmemn = dmemn + _dot_nt(part, w_ref[j])
        mv = mem_ref[...]
        r = lax.rsqrt(jnp.mean(mv * mv, axis=-1, keepdims=True) + EPS)
        stat_ref[...] = jnp.zeros((8, D), F32)
        stat_ref[0:1, :] = jnp.sum(dmemn * (mv * r), axis=0, keepdims=True)

    return pl.pallas_call(
        body, name="kv_bwd",
        out_shape=[jax.ShapeDtypeStruct((N_CHIPS, D, half), F32), jax.ShapeDtypeStruct((8, D), F32)],
        compiler_params=_params(),
    )(dk, dv, memn, mem, g_mem, w_kv4)


def matmul_tn(name, a, b, tn, col_blocks=1, k_dim=None, n_dim=None, a_col=0, b_col=0, carry=None):
    t_len = a.shape[0]
    k_dim = a.shape[1] if k_dim is None else k_dim
    n_dim = b.shape[1] if n_dim is None else n_dim
    b_off = b_col * (n_dim // tn)
    tt = min(512, t_len)
    per_block = n_dim // col_blocks // tn

    def body(a_ref, b_ref, out_ref):
        @pl.when(pl.program_id(1) == 0)
        def _():
            out_ref[...] = jnp.zeros((k_dim, tn), F32)

        out_ref[...] += _dot_tn(a_ref[...].astype(BF16), b_ref[...].astype(BF16))

    if col_blocks == 1:
        out_spec = pl.BlockSpec((k_dim, tn), lambda n, t: (0, n))
        out_shape = jax.ShapeDtypeStruct((k_dim, n_dim), F32)
    else:
        out_spec = pl.BlockSpec((None, k_dim, tn), lambda n, t: (n // per_block, 0, n % per_block))
        out_shape = jax.ShapeDtypeStruct((col_blocks, k_dim, n_dim // col_blocks), F32)
    (out,), carried = _carried_call(
        body, name, (n_dim // tn, t_len // tt),
        in_specs=[pl.BlockSpec((tt, k_dim), lambda n, t: (t, a_col)), pl.BlockSpec((tt, tn), lambda n, t: (t, b_off + n))],
        out_specs=[out_spec], out_shape=[out_shape], scratch_shapes=[], semantics=("arbitrary", "arbitrary"),
        args=(a, b), carry=carry)
    return (out, carried) if carry is not None else out


def _place():
    x, y, c = lax.axis_index("x"), lax.axis_index("y"), lax.axis_index("c")
    return x, y, c


def _other_chips(x, y):
    return [(1 - x, y), (x, 1 - y), (1 - x, 1 - y)]


def _any_specs(n):
    return [pl.BlockSpec(memory_space=pl.ANY)] * n


def gather_weights(shards):
    n = len(shards)

    def body(*refs):
        gather = _Gather(refs[:n], refs[n:2 * n], *refs[2 * n:])
        gather.start()
        gather.finish()

    return pl.pallas_call(
        body, name="gather_weights",
        in_specs=_any_specs(n), out_specs=_any_specs(n),
        out_shape=_gathered_shapes(shards), scratch_shapes=_gather_sems(n),
    )(*shards)


def _gathered_shapes(shards):
    return [jax.ShapeDtypeStruct((N_CHIPS,) + s.shape, s.dtype) for s in shards]


def _gather_sems(n):
    return [pltpu.SemaphoreType.DMA((n, 7)), pltpu.SemaphoreType.DMA((n, 7))]


class _Gather:
    def __init__(self, ins, outs, send_sems, recv_sems):
        self.ins, self.outs, self.send_sems, self.recv_sems = ins, outs, send_sems, recv_sems
        x, y, c = _place()
        self.c, self.me, self.sibling, self.chips = c, 2 * x + y, (x, y, 1 - c), _other_chips(x, y)

    def _half(self, ref, which):
        rows = ref.shape[1]
        return ref.at[:, pl.ds(which * (rows // 2), rows // 2), :]

    def _remote(self, src, dst, a, slot, to):
        return pltpu.make_async_remote_copy(src_ref=src, dst_ref=dst, send_sem=self.send_sems.at[a, slot],
                                            recv_sem=self.recv_sems.at[a, slot], device_id=to, device_id_type=MESH)

    def _own(self, a):
        return self._remote(self.ins[a], self.outs[a].at[self.me], a, 6, self.sibling)

    def _sent(self, a, slot):
        px, py = self.chips[slot]
        return self._remote(self._half(self.ins[a], self.c), self._half(self.outs[a].at[self.me], self.c), a, slot,
                            (px, py, self.c))

    def _passed_on(self, a, slot, which):
        px, py = self.chips[slot]
        block = self._half(self.outs[a].at[2 * px + py], which)
        return self._remote(block, block, a, 3 + slot, self.sibling)

    def start(self):
        for a in range(len(self.ins)):
            self._own(a).start()
        for a in range(len(self.ins)):
            for slot in range(3):
                self._sent(a, slot).start()

    def finish(self):
        n = len(self.ins)
        for a in range(n):
            for slot in range(3):
                px, py = self.chips[slot]
                landed = self._half(self.outs[a].at[2 * px + py], self.c)
                self._remote(landed, landed, a, slot, (px, py, self.c)).wait_recv()
                self._passed_on(a, slot, self.c).start()
        for a in range(n):
            for slot in range(3):
                self._passed_on(a, slot, 1 - self.c).wait_recv()
        for a in range(n):
            self._own(a).wait_recv()
        for a in range(n):
            self._own(a).wait_send()
            for slot in range(3):
                self._sent(a, slot).wait_send()
                self._passed_on(a, slot, self.c).wait_send()


def exchange_sibling_halves(name, grads):
    n = len(grads)

    def body(*refs):
        ins, outs = refs[:n], refs[n:2 * n]
        send_sems, recv_sems = refs[2 * n:]
        x, y, c = _place()
        copies = []
        for a in range(n):
            h = ins[a].shape[1] // 2
            cp = pltpu.make_async_remote_copy(
                src_ref=ins[a].at[:, pl.ds((1 - c) * h, h), :], dst_ref=outs[a],
                send_sem=send_sems.at[a], recv_sem=recv_sems.at[a], device_id=(x, y, 1 - c), device_id_type=MESH)
            cp.start()
            copies.append(cp)
        for cp in copies:
            cp.wait()

    return pl.pallas_call(
        body, name=name,
        in_specs=_any_specs(n), out_specs=_any_specs(n),
        out_shape=[jax.ShapeDtypeStruct((N_CHIPS, g.shape[1] // 2, g.shape[2]), F32) for g in grads],
        scratch_shapes=[pltpu.SemaphoreType.DMA((n,)), pltpu.SemaphoreType.DMA((n,))],
    )(*grads)


def _row_tile(rows, cols, budget=1 << 20):
    best = 16
    for tr in range(16, rows + 1, 16):
        if rows % tr == 0 and tr * cols * 4 <= budget:
            best = tr
    return best


def add_sibling_half(name, grad, got, core):
    _, rows, cols = grad.shape
    h = rows // 2
    tr = _row_tile(h, cols)
    per_half = h // tr

    def body(core_ref, g_ref, o_ref, out_ref, out16_ref):
        total = g_ref[...] + o_ref[...]
        out_ref[...] = total
        out16_ref[...] = total.astype(BF16)

    out_spec = pl.BlockSpec((None, tr, cols), lambda j, r, cr: (j, r, 0))
    return pl.pallas_call(
        body, name=name,
        grid_spec=pltpu.PrefetchScalarGridSpec(
            num_scalar_prefetch=1, grid=(N_CHIPS, per_half),
            in_specs=[pl.BlockSpec((None, tr, cols), lambda j, r, cr: (j, cr[0] * per_half + r, 0)),
                      pl.BlockSpec((None, tr, cols), lambda j, r, cr: (j, r, 0))],
            out_specs=[out_spec, out_spec]),
        out_shape=[jax.ShapeDtypeStruct((N_CHIPS, h, cols), F32), jax.ShapeDtypeStruct((N_CHIPS, h, cols), BF16)],
        compiler_params=_params(("arbitrary", "arbitrary")),
    )(core, grad, got)


def exchange_chip_blocks(partials):
    n = len(partials)

    def body(*refs):
        exchange = _ChipExchange(refs[:n], refs[n:2 * n], *refs[2 * n:])
        exchange.start()
        exchange.finish()

    return pl.pallas_call(
        body, name="exchange_chip_blocks",
        in_specs=_any_specs(n), out_specs=_any_specs(n),
        out_shape=_exchanged_shapes(partials), scratch_shapes=_exchange_sems(n),
    )(*partials)


def _exchanged_shapes(partials):
    return [jax.ShapeDtypeStruct((3,) + p.shape[1:], p.dtype) for p in partials]


def _exchange_sems(n):
    return [pltpu.SemaphoreType.DMA((n, 3)), pltpu.SemaphoreType.DMA((n, 3))]


class _ChipExchange:
    def __init__(self, ins, outs, send_sems, recv_sems):
        self.ins, self.outs, self.send_sems, self.recv_sems = ins, outs, send_sems, recv_sems

    def _copies(self):
        x, y, c = _place()
        for a in range(len(self.ins)):
            for slot, (px, py) in enumerate(_other_chips(x, y)):
                yield pltpu.make_async_remote_copy(
                    src_ref=self.ins[a].at[2 * px + py], dst_ref=self.outs[a].at[slot],
                    send_sem=self.send_sems.at[a, slot], recv_sem=self.recv_sems.at[a, slot],
                    device_id=(px, py, c), device_id_type=MESH)

    def start(self):
        for cp in self._copies():
            cp.start()

    def finish(self):
        for cp in self._copies():
            cp.wait()


def add_chip_blocks(name, partial, got, chip):
    _, h, cols = partial.shape
    tr = _row_tile(h, cols)

    def body(chip_ref, p_ref, g0_ref, g1_ref, g2_ref, out_ref):
        out_ref[...] = ((p_ref[...] + g0_ref[...].astype(F32)) + g1_ref[...].astype(F32)) + g2_ref[...].astype(F32)

    def got_spec(slot):
        return pl.BlockSpec((None, tr, cols), lambda r, ch: (slot, r, 0))

    return pl.pallas_call(
        body, name=name,
        grid_spec=pltpu.PrefetchScalarGridSpec(
            num_scalar_prefetch=1, grid=(h // tr,),
            in_specs=[pl.BlockSpec((None, tr, cols), lambda r, ch: (ch[0], r, 0)), got_spec(0), got_spec(1), got_spec(2)],
            out_specs=pl.BlockSpec((tr, cols), lambda r, ch: (r, 0))),
        out_shape=jax.ShapeDtypeStruct((h, cols), F32),
        compiler_params=_params(("arbitrary",)),
    )(chip, partial, got, got, got)


def swap_sibling_halves(halves):
    n = len(halves)

    def body(*refs):
        ins, outs = refs[:n], refs[n:2 * n]
        send_sems, recv_sems = refs[2 * n:]
        x, y, c = _place()
        copies = []
        for a in range(n):
            cp = pltpu.make_async_remote_copy(
                src_ref=ins[a], dst_ref=outs[a], send_sem=send_sems.at[a], recv_sem=recv_sems.at[a],
                device_id=(x, y, 1 - c), device_id_type=MESH)
            cp.start()
            copies.append(cp)
        for cp in copies:
            cp.wait()

    return pl.pallas_call(
        body, name="swap_sibling_halves",
        in_specs=_any_specs(n), out_specs=_any_specs(n),
        out_shape=[jax.ShapeDtypeStruct(v.shape, F32) for v in halves],
        scratch_shapes=[pltpu.SemaphoreType.DMA((n,)), pltpu.SemaphoreType.DMA((n,))],
    )(*halves)


def all_reduce_small(pack):
    rows = pack.shape[0]

    def body(pack_ref, out_ref, gathered, send_sems, recv_sems):
        x, y, c = _place()
        me = 4 * x + 2 * y + c
        gathered[me] = pack_ref[...]
        copies = []
        for rel in range(1, 8):
            fx, fy, fc = (rel >> 2) & 1, (rel >> 1) & 1, rel & 1
            peer = (x ^ fx, y ^ fy, c ^ fc)
            cp = pltpu.make_async_remote_copy(
                src_ref=pack_ref, dst_ref=gathered.at[me], send_sem=send_sems.at[rel - 1], recv_sem=recv_sems.at[rel - 1],
                device_id=peer, device_id_type=MESH)
            cp.start()
            copies.append(cp)
        for rel in range(1, 8):
            fx, fy, fc = (rel >> 2) & 1, (rel >> 1) & 1, rel & 1
            src = 4 * (x ^ fx) + 2 * (y ^ fy) + (c ^ fc)
            pltpu.make_async_remote_copy(
                src_ref=pack_ref, dst_ref=gathered.at[src], send_sem=send_sems.at[rel - 1], recv_sem=recv_sems.at[rel - 1],
                device_id=(x, y, c), device_id_type=MESH).wait_recv()
        for cp in copies:
            cp.wait_send()
        total = gathered[0]
        for dev in range(1, 8):
            total = total + gathered[dev]
        out_ref[...] = total

    return pl.pallas_call(
        body, name="all_reduce_small",
        in_specs=[pl.BlockSpec(memory_space=pltpu.VMEM)], out_specs=pl.BlockSpec(memory_space=pltpu.VMEM),
        out_shape=jax.ShapeDtypeStruct((rows, D), F32),
        scratch_shapes=[pltpu.VMEM((8, rows, D), F32), pltpu.SemaphoreType.DMA((7,)), pltpu.SemaphoreType.DMA((7,))],
    )(pack)


def _adamw_update(w, g, m, v):
    nm = ADAM_B1 * m + (1.0 - ADAM_B1) * g
    nv = ADAM_B2 * v + (1.0 - ADAM_B2) * (g * g)
    m_hat = nm / (1.0 - ADAM_B1 ** ADAM_STEP)
    v_hat = nv / (1.0 - ADAM_B2 ** ADAM_STEP)
    delta = -ADAM_LR * (m_hat / (jnp.sqrt(v_hat) + ADAM_EPS) + ADAM_WD * w)
    return delta, nm, nv


def adamw(name, w, g, m, v):
    def body(w_ref, g_ref, m_ref, v_ref, d_ref, nm_ref, nv_ref):
        d_ref[...], nm_ref[...], nv_ref[...] = _adamw_update(w_ref[...], g_ref[...], m_ref[...], v_ref[...])

    out = jax.ShapeDtypeStruct(w.shape, F32)
    return pl.pallas_call(body, name=name, out_shape=[out] * 3, compiler_params=_params())(w, g, m, v)


def adamw_halves(name, w, mine, theirs, m, v, core):
    h, cols = mine.shape
    tr = _row_tile(h, cols)
    per_half = h // tr

    def body(core_ref, w_ref, mine_ref, theirs_ref, m_ref, v_ref, g_ref, d_ref, nm_ref, nv_ref):
        g = jnp.where(pl.program_id(0) == core_ref[0], mine_ref[...], theirs_ref[...])
        g_ref[...] = g
        d_ref[...], nm_ref[...], nv_ref[...] = _adamw_update(w_ref[...], g, m_ref[...], v_ref[...])

    full = pl.BlockSpec((tr, cols), lambda hh, r, cr: (hh * per_half + r, 0))
    mine_spec = pl.BlockSpec((tr, cols), lambda hh, r, cr: (jnp.where(hh == cr[0], r, 0), 0))
    theirs_spec = pl.BlockSpec((tr, cols), lambda hh, r, cr: (jnp.where(hh == cr[0], 0, r), 0))
    out = jax.ShapeDtypeStruct((2 * h, cols), F32)
    return pl.pallas_call(
        body, name=name,
        grid_spec=pltpu.PrefetchScalarGridSpec(
            num_scalar_prefetch=1, grid=(2, per_half),
            in_specs=[full, mine_spec, theirs_spec, full, full], out_specs=[full] * 4),
        out_shape=[out] * 4,
        compiler_params=_params(("arbitrary", "arbitrary")),
    )(core, w, mine, theirs, m, v)


BIG = ("w_in", "w_conv_out", "w_pool", "w_kv", "w_xattn_out", "w_out", "w_gate", "w_up", "w_down")


def kernel(x, mem, norm_mix, w_in, conv_w, w_conv_out, w_pool, pool_scale, norm_mem, w_kv, w_xattn_out, w_out, norm_ffn, w_gate, w_up, w_down, norm_final, loss_target, m_norm_mix, m_w_in, m_conv_w, m_w_conv_out, m_w_pool, m_pool_scale, m_norm_mem, m_w_kv, m_w_xattn_out, m_w_out, m_norm_ffn, m_w_gate, m_w_up, m_w_down, m_norm_final, v_norm_mix, v_w_in, v_conv_w, v_w_conv_out, v_w_pool, v_pool_scale, v_norm_mem, v_w_kv, v_w_xattn_out, v_w_out, v_norm_ffn, v_w_gate, v_w_up, v_w_down, v_norm_final):
    t_len = x.shape[1]
    xi, yi, ci = lax.axis_index("x"), lax.axis_index("y"), lax.axis_index("c")
    chip = 2 * xi + yi
    chip_arr = jnp.reshape(chip, (1,)).astype(jnp.int32)
    core_arr = jnp.reshape(ci, (1,)).astype(jnp.int32)

    conv_pad = jnp.concatenate([conv_w, jnp.zeros((1, 13, 256), F32)], axis=1)
    def t2(w):
        return jnp.swapaxes(w, 1, 2)

    g_in, g_kv, g_conv_w = gather_weights([w_in.astype(BF16), w_kv.astype(BF16), conv_pad])
    w_in4 = g_in.reshape(N_CHIPS, D, D_IN // N_CHIPS)
    w_kv4 = g_kv.reshape(N_CHIPS, D, D // 2)
    conv_full = jnp.transpose(g_conv_w[:, 0, 0:8, :], (1, 0, 2)).reshape(8, D)

    x2d = x[0]
    tgt = loss_target[0]
    (proj, h), (g_co, g_xo, g_out, g_pool, g_gate) = proj_fwd(
        x2d, norm_mix, w_in4,
        carry=(_Gather, [w_conv_out.astype(BF16), w_xattn_out.astype(BF16), w_out.astype(BF16),
                         w_pool[0].astype(BF16),
                         t2(w_gate).astype(BF16)]))
    w_co_f, w_xo_f, w_out_f = g_co.reshape(D, D), g_xo.reshape(D, D), g_out.reshape(D, D)
    w_pool_f = jnp.transpose(g_pool, (1, 0, 2, 3)).reshape(GROUPS, GROUP_DIM, GROUP_DIM)
    memn, k, v = kv_fwd(mem[0], norm_mem, w_kv4)
    (a, pooled, ya, pp, yx, o, probs, x1), (g_up, g_down) = mixer_fwd(
        proj, x2d, conv_full, w_co_f, w_pool_f, pool_scale, k, v, w_xo_f, w_out_f,
        carry=(_Gather, [t2(w_up).astype(BF16), w_down.astype(BF16)]))
    w_gate_f, w_up_f, w_down_f = g_gate.reshape(D_FF, D), g_up.reshape(D_FF, D), g_down.reshape(D_FF, D)
    gate, up, dx2, stat_f = ffn_fwd(x1, tgt, norm_ffn, w_gate_f, w_up_f, w_down_f, norm_final.reshape(1, D))

    def by_chip(gw):
        return gw.reshape(N_CHIPS, gw.shape[0] // N_CHIPS, gw.shape[1])

    def chip_partials(tag, names, grads):
        got = exchange_sibling_halves("exchange_sibling_halves_" + tag, grads)
        return [add_sibling_half("add_sibling_" + n, g, o_, core_arr) for n, g, o_ in zip(names, grads, got)]

    dx1, dgate, dup, act, h2, stat_b1 = ffn_bwd(dx2, x1, gate, up, norm_ffn, w_gate_f, w_up_f, w_down_f)
    gw_gate = matmul_tn("grad_w_gate", dgate, h2, 512)
    gw_up = matmul_tn("grad_w_up", dup, h2, 512)
    gw_down = matmul_tn("grad_w_down", act, dx2, 512)
    names_ffn = ("w_gate", "w_up", "w_down")
    part_ffn = chip_partials("ffn", names_ffn, [by_chip(gw_gate), by_chip(gw_up), by_chip(gw_down)])

    (dproj, merged, dya, dpp, dyx, dk, dv, stat_b2), got_ffn = mixer_bwd(
        dx1, proj, ya, pp, yx, probs, conv_full, w_co_f, w_pool_f, pool_scale, k, v, w_xo_f, w_out_f,
        carry=(_ChipExchange, [p16 for _, p16 in part_ffn]))
    gw_kv, stat_kv = kv_bwd(dk, dv, memn, mem[0], norm_mem, w_kv4)
    gw_co = matmul_tn("grad_w_conv_out", a, dya, 1024)
    gw_xo = matmul_tn("grad_w_xattn_out", o, dyx, 1024)
    gw_out = matmul_tn("grad_w_out", merged, dx1, 1024)
    gw_pool = jnp.stack([matmul_tn("grad_w_pool_%d" % g, pooled, dpp, GROUP_DIM, k_dim=GROUP_DIM, n_dim=GROUP_DIM,
                                   a_col=g, b_col=g) for g in range(GROUPS)])
    gw_pool = jnp.transpose(gw_pool.reshape(GROUPS, N_CHIPS, 64, GROUP_DIM), (1, 0, 2, 3)).reshape(N_CHIPS, 256, GROUP_DIM)
    names_mix = ("w_conv_out", "w_pool", "w_kv", "w_xattn_out", "w_out")
    part_mix = chip_partials("mixer", names_mix, [by_chip(gw_co), gw_pool, gw_kv, by_chip(gw_xo), by_chip(gw_out)])

    gw_in, got_mix = matmul_tn("grad_w_in", h, dproj, 2048, col_blocks=N_CHIPS,
                               carry=(_ChipExchange, [p16 for _, p16 in part_mix]))
    part_in = chip_partials("in", ("w_in",), [gw_in])
    (grad_x, stat_b3), got_in = in_bwd(dproj, w_in4, x2d, dx1, norm_mix,
                                       carry=(_ChipExchange, [p16 for _, p16 in part_in]))

    partials = dict(zip(names_ffn + names_mix + ("w_in",), part_ffn + part_mix + part_in))
    got2 = dict(zip(names_ffn + names_mix + ("w_in",), list(got_ffn) + list(got_mix) + list(got_in)))
    mine = [add_chip_blocks("add_chips_" + n, partials[n][0], got2[n], chip_arr) for n in BIG]
    theirs = swap_sibling_halves(mine)

    pack = jnp.concatenate([stat_b3[0:1], stat_b2[1:2], stat_kv[0:1], stat_b1[0:1], stat_f[0:1], stat_b2[5:8],
                            stat_f[1:2], jnp.zeros((7, D), F32)], axis=0)
    total = all_reduce_small(pack)
    loss = jnp.sum(total[8])
    g_conv_full = total[5:8]
    g_conv = lax.dynamic_slice_in_dim(g_conv_full, chip * 256, 256, axis=1)

    given = dict(w_in=(w_in, m_w_in, v_w_in), w_conv_out=(w_conv_out, m_w_conv_out, v_w_conv_out),
                 w_pool=(w_pool, m_w_pool, v_w_pool), w_kv=(w_kv, m_w_kv, v_w_kv),
                 w_xattn_out=(w_xattn_out, m_w_xattn_out, v_w_xattn_out), w_out=(w_out, m_w_out, v_w_out),
                 w_gate=(w_gate, m_w_gate, v_w_gate), w_up=(w_up, m_w_up, v_w_up), w_down=(w_down, m_w_down, v_w_down))
    out_g, out_d, out_m, out_v = {}, {}, {}, {}
    for n, mine_n, theirs_n in zip(BIG, mine, theirs):
        transposed = n in ("w_gate", "w_up")
        rows2d = (2 * mine_n.shape[0], mine_n.shape[1])
        w_, m_, v_ = ((t2(t) if transposed else t).reshape(rows2d) for t in given[n])
        res = adamw_halves("adamw_" + n, w_, mine_n, theirs_n, m_, v_, core_arr)
        if transposed:
            res = [t2(t.reshape(1, D_FF // N_CHIPS, D)) for t in res]
        out_g[n], out_d[n], out_m[n], out_v[n] = (t.reshape(given[n][0].shape) for t in res)

    def small_pack(vals, conv_part):
        conv_rows = jnp.concatenate([conv_part.reshape(3, 256), jnp.zeros((3, D - 256), F32)], axis=1)
        return jnp.concatenate([val.reshape(1, D) for val in vals] + [conv_rows], axis=0)

    sw = small_pack([norm_mix, pool_scale, norm_mem, norm_ffn, norm_final], conv_w)
    sm = small_pack([m_norm_mix, m_pool_scale, m_norm_mem, m_norm_ffn, m_norm_final], m_conv_w)
    sv = small_pack([v_norm_mix, v_pool_scale, v_norm_mem, v_norm_ffn, v_norm_final], v_conv_w)
    sg = small_pack([total[r] for r in range(5)], g_conv)
    sd, snm, snv = adamw("adamw_small", sw, sg, sm, sv)
    small_names = ("norm_mix", "pool_scale", "norm_mem", "norm_ffn", "norm_final")
    small_shapes = dict(norm_mix=(1, D), pool_scale=(1, D), norm_mem=(1, D), norm_ffn=(1, D), norm_final=(D,))
    for r, n in enumerate(small_names):
        out_g[n], out_d[n], out_m[n], out_v[n] = (t[r].reshape(small_shapes[n]) for t in (sg, sd, snm, snv))
    out_g["conv_w"], out_d["conv_w"], out_m["conv_w"], out_v["conv_w"] = (
        t[5:8, 0:256].reshape(1, 3, 256) for t in (sg, sd, snm, snv))

    order = ("norm_mix", "w_in", "conv_w", "w_conv_out", "w_pool", "pool_scale", "norm_mem", "w_kv", "w_xattn_out",
             "w_out", "norm_ffn", "w_gate", "w_up", "w_down", "norm_final")
    return (loss, grad_x.reshape(1, t_len, D), *[out_g[n] for n in order], *[out_d[n] for n in order],
            *[out_m[n] for n in order], *[out_v[n] for n in order])
```

```python
import functools

import jax
import jax.numpy as jnp
from jax import lax
from jax.experimental import pallas as pl
from jax.experimental.pallas import tpu as pltpu

F32 = jnp.float32
BF16 = jnp.bfloat16
MESH = pl.DeviceIdType.MESH

D = 1024
N_MEM = 256
HEADS = 4
HEAD_DIM = 256
GROUPS = 4
GROUP_DIM = 256
POOL_WINDOWS = (2, 4, 8, 16)
D_FF = 2816
D_IN = 8192
N_CHIPS = 4
EPS = 1e-6
HALO = 16
POOL_PAD = 128
ATT_SCALE = HEAD_DIM ** -0.5

ADAM_LR = 0.001
ADAM_B1 = 0.9
ADAM_B2 = 0.999
ADAM_EPS = 1e-08
ADAM_WD = 0.01
ADAM_STEP = 10

VMEM_LIMIT = 56 * 1024 * 1024

O_BA, O_CA, O_UA, O_UP, O_QX, O_GA, O_GP, O_GX = (k * D for k in range(8))

NT_DIMS = (((1,), (1,)), ((), ()))
TN_DIMS = (((0,), (0,)), ((), ()))


def _dot(a, b):
    return jnp.dot(a, b, preferred_element_type=F32)


def _dot_nt(a, b):
    return lax.dot_general(a, b, NT_DIMS, preferred_element_type=F32)


def _dot_tn(a, b):
    return lax.dot_general(a, b, TN_DIMS, preferred_element_type=F32)


def _sigmoid(z):
    return pl.reciprocal(1.0 + jnp.exp(-z), approx=True)


def _params(semantics=None):
    return pltpu.CompilerParams(dimension_semantics=semantics, vmem_limit_bytes=VMEM_LIMIT)


def _resident(shape):
    zeros = (0,) * len(shape)
    return pl.BlockSpec(shape, lambda *_: zeros, pipeline_mode=pl.Buffered(1))


def _const(shape):
    zeros = (0,) * len(shape)
    return pl.BlockSpec(shape, lambda *_: zeros)


def _rows(tm, width):
    return pl.BlockSpec((tm, width), lambda i: (i, 0))


def _inv_count(tile, tm, window):
    t = tile * tm + lax.broadcasted_iota(jnp.int32, (tm, 1), 0)
    return 1.0 / jnp.minimum(t + 1, window).astype(F32)


def _carried_call(body, name, grid, in_specs, out_specs, out_shape, scratch_shapes, semantics, args, carry):
    if carry is None:
        res = pl.pallas_call(body, name=name, grid=grid, in_specs=in_specs, out_specs=out_specs, out_shape=out_shape,
                             scratch_shapes=scratch_shapes, compiler_params=_params(semantics))(*args)
        return res, []
    carries = [carry] if isinstance(carry, tuple) else list(carry)
    comm_args = [arr for _, arrs in carries for arr in arrs]
    n, n_in, n_out, n_scratch = len(comm_args), len(in_specs), len(out_specs), len(scratch_shapes)
    comm_shapes = [s for cls, arrs in carries for s in cls.out_shapes(arrs)]
    comm_sems = [s for cls, arrs in carries for s in cls.sems(len(arrs))]

    def carrying(*refs):
        ins, comm_ins = refs[:n_in], refs[n_in:n_in + n]
        outs, comm_outs = refs[n_in + n:n_in + n + n_out], refs[n_in + n + n_out:n_in + 2 * n + n_out]
        scratch, sems = refs[n_in + 2 * n + n_out:n_in + 2 * n + n_out + n_scratch], refs[n_in + 2 * n + n_out + n_scratch:]
        steps = [pl.program_id(d) for d in range(len(grid))]
        first = functools.reduce(jnp.logical_and, [s == 0 for s in steps])
        last = functools.reduce(jnp.logical_and, [s == g - 1 for s, g in zip(steps, grid)])

        def exchanges():
            at = 0
            for k, (cls, arrs) in enumerate(carries):
                yield cls(comm_ins[at:at + len(arrs)], comm_outs[at:at + len(arrs)], sems[2 * k], sems[2 * k + 1])
                at += len(arrs)

        @pl.when(first)
        def _():
            for exchange in exchanges():
                exchange.start()

        body(*ins, *outs, *scratch)

        @pl.when(last)
        def _():
            for exchange in exchanges():
                exchange.finish()

    res = pl.pallas_call(
        carrying, name=name, grid=grid, in_specs=list(in_specs) + _any_specs(n), out_specs=list(out_specs) + _any_specs(n),
        out_shape=list(out_shape) + comm_shapes, scratch_shapes=list(scratch_shapes) + comm_sems,
        compiler_params=_params(semantics))(*args, *comm_args)
    comm_res, at = [], n_out
    for _, arrs in carries:
        comm_res.append(res[at:at + len(arrs)])
        at += len(arrs)
    return res[:n_out], (comm_res[0] if isinstance(carry, tuple) else comm_res)


def proj_fwd(x, g_mix, w_in4, carry=None):
    t_len = x.shape[0]
    tm = min(1024, t_len)
    tn = D_IN // N_CHIPS

    def body(x_ref, g_ref, w_ref, proj_ref, h_ref):
        @pl.when(pl.program_id(1) == 0)
        def _():
            xv = x_ref[...]
            r = lax.rsqrt(jnp.mean(xv * xv, axis=-1, keepdims=True) + EPS)
            h_ref[...] = (xv * r * g_ref[...]).astype(BF16)

        proj_ref[...] = _dot(h_ref[...], w_ref[...]).astype(BF16)

    return _carried_call(
        body, "proj_fwd", (t_len // tm, N_CHIPS),
        in_specs=[pl.BlockSpec((tm, D), lambda i, j: (i, 0)),
                  pl.BlockSpec((1, D), lambda i, j: (0, 0)),
                  pl.BlockSpec((None, D, tn), lambda i, j: (j, 0, 0))],
        out_specs=[pl.BlockSpec((tm, tn), lambda i, j: (i, j)),
                   pl.BlockSpec((tm, D), lambda i, j: (i, 0))],
        out_shape=[jax.ShapeDtypeStruct((t_len, D_IN), BF16), jax.ShapeDtypeStruct((t_len, D), BF16)],
        scratch_shapes=[], semantics=("arbitrary", "arbitrary"), args=(x, g_mix, w_in4), carry=carry)


def kv_fwd(mem, g_mem, w_kv4):
    half = D // 2

    def body(mem_ref, g_ref, w_ref, memn_ref, k_ref, v_ref):
        mv = mem_ref[...]
        r = lax.rsqrt(jnp.mean(mv * mv, axis=-1, keepdims=True) + EPS)
        mn = (mv * r * g_ref[...]).astype(BF16)
        memn_ref[...] = mn
        k_ref[:, 0:half] = _dot(mn, w_ref[0]).astype(BF16)
        k_ref[:, half:D] = _dot(mn, w_ref[1]).astype(BF16)
        v_ref[:, 0:half] = _dot(mn, w_ref[2]).astype(BF16)
        v_ref[:, half:D] = _dot(mn, w_ref[3]).astype(BF16)

    out = jax.ShapeDtypeStruct((N_MEM, D), BF16)
    return pl.pallas_call(body, name="kv_fwd", out_shape=[out, out, out], compiler_params=_params())(mem, g_mem, w_kv4)


def _softmax_rows(s):
    m = jnp.max(s, axis=-1, keepdims=True)
    e = jnp.exp(s - m)
    return e * pl.reciprocal(jnp.sum(e, axis=-1, keepdims=True), approx=True)


def _window_bands(tm, causal):
    t = lax.broadcasted_iota(jnp.int32, (tm, tm + POOL_PAD), 0)
    s = lax.broadcasted_iota(jnp.int32, (tm, tm + POOL_PAD), 1)
    d = (t + POOL_PAD - s) if causal else (s - t)
    return jnp.stack([((d >= 0) & (d < w)).astype(BF16) for w in POOL_WINDOWS])


def mixer_fwd(proj, x, conv_w8, w_co, w_pool, pool_scale, k, v, w_xo, w_out, carry=None):
    t_len = x.shape[0]
    tm = min(256, t_len)

    def body(proj_ref, x_ref, cw_ref, wco_ref, wpool_ref, ps_ref, k_ref, v_ref, wxo_ref, wout_ref,
             a_ref, pooled_ref, ya_ref, pp_ref, yx_ref, o_ref, p_ref, x1_ref, cu_ext, up_ext):
        i = pl.program_id(0)

        @pl.when(i == 0)
        def _():
            cu_ext[0:HALO, :] = jnp.zeros((HALO, D), F32)
            up_ext[0:HALO, :] = jnp.zeros((HALO, D), F32)

        cu = proj_ref[:, O_CA:O_CA + D].astype(F32) * proj_ref[:, O_UA:O_UA + D].astype(F32)
        cu_ext[HALO:HALO + tm, :] = cu
        conv = (cw_ref[2:3, :] * cu + cw_ref[1:2, :] * cu_ext[HALO - 1:HALO - 1 + tm, :]
                + cw_ref[0:1, :] * cu_ext[HALO - 2:HALO - 2 + tm, :])
        a = (proj_ref[:, O_BA:O_BA + D].astype(F32) * conv).astype(BF16)
        a_ref[...] = a
        ya = _dot(a, wco_ref[...])
        ya_ref[...] = ya.astype(BF16)

        up_ext[HALO:HALO + tm, :] = proj_ref[:, O_UP:O_UP + D].astype(F32)
        for g, window in enumerate(POOL_WINDOWS):
            cols = slice(g * GROUP_DIM, (g + 1) * GROUP_DIM)
            tok = up_ext[HALO:HALO + tm, cols]
            acc = tok
            for j in range(1, window):
                acc = acc + up_ext[HALO - j:HALO - j + tm, cols]
            pooled = (acc * _inv_count(i, tm, window) - tok).astype(BF16)
            pooled_ref[:, cols] = pooled
            pp_ref[:, cols] = _dot(pooled, wpool_ref[g]).astype(BF16)

        for hd in range(HEADS):
            cols = slice(hd * HEAD_DIM, (hd + 1) * HEAD_DIM)
            q = proj_ref[:, O_QX + hd * HEAD_DIM:O_QX + (hd + 1) * HEAD_DIM]
            p = _softmax_rows(_dot_nt(q, k_ref[:, cols]) * ATT_SCALE).astype(BF16)
            p_ref[:, hd * N_MEM:(hd + 1) * N_MEM] = p
            o_ref[:, cols] = _dot(p, v_ref[:, cols]).astype(BF16)
        yx = _dot(o_ref[...], wxo_ref[...])
        yx_ref[...] = yx.astype(BF16)

        merged = (_sigmoid(proj_ref[:, O_GA:O_GA + D].astype(F32)) * ya
                  + _sigmoid(proj_ref[:, O_GP:O_GP + D].astype(F32)) * (pp_ref[...].astype(F32) * ps_ref[...])
                  + _sigmoid(proj_ref[:, O_GX:O_GX + D].astype(F32)) * yx)
        x1_ref[...] = x_ref[...] + _dot(merged.astype(BF16), wout_ref[...])

        cu_ext[0:HALO, :] = cu_ext[tm:tm + HALO, :]
        up_ext[0:HALO, :] = up_ext[tm:tm + HALO, :]

    act = jax.ShapeDtypeStruct((t_len, D), BF16)
    return _carried_call(
        body, "mixer_fwd", (t_len // tm,),
        in_specs=[_rows(tm, D_IN), _rows(tm, D), _resident((8, D)), _resident((D, D)),
                  _resident((GROUPS, GROUP_DIM, GROUP_DIM)), _resident((1, D)),
                  _resident((N_MEM, D)), _resident((N_MEM, D)), _resident((D, D)), _resident((D, D))],
        out_specs=[_rows(tm, D)] * 7 + [_rows(tm, D)],
        out_shape=[act] * 6 + [jax.ShapeDtypeStruct((t_len, HEADS * N_MEM), BF16), jax.ShapeDtypeStruct((t_len, D), F32)],
        scratch_shapes=[pltpu.VMEM((tm + HALO, D), F32), pltpu.VMEM((tm + HALO, D), F32)],
        semantics=("arbitrary",), args=(proj, x, conv_w8, w_co, w_pool, pool_scale, k, v, w_xo, w_out), carry=carry)


def ffn_fwd(x1, target, g_ffn, w_gate, w_up, w_down, g_final):
    t_len = x1.shape[0]
    tm = min(256, t_len)

    def body(x1_ref, tgt_ref, g_ref, wg_ref, wu_ref, wd_ref, gf_ref, gate_ref, up_ref, dx2_ref, stat_ref):
        @pl.when(pl.program_id(0) == 0)
        def _():
            stat_ref[...] = jnp.zeros((8, D), F32)

        x1v = x1_ref[...]
        r2 = lax.rsqrt(jnp.mean(x1v * x1v, axis=-1, keepdims=True) + EPS)
        h2 = (x1v * r2 * g_ref[...]).astype(BF16)
        gate = _dot_nt(h2, wg_ref[...])
        up = _dot_nt(h2, wu_ref[...])
        gate_ref[...] = gate.astype(BF16)
        up_ref[...] = up.astype(BF16)
        act = (gate * _sigmoid(gate) * up).astype(BF16)
        x2 = x1v + _dot(act, wd_ref[...])
        r3 = lax.rsqrt(jnp.mean(x2 * x2, axis=-1, keepdims=True) + EPS)
        xh = x2 * r3
        diff = xh * gf_ref[...] - tgt_ref[...]
        dy = diff * (1.0 / D)
        stat_ref[0:1, :] += jnp.sum(dy * xh, axis=0, keepdims=True)
        stat_ref[1:2, :] += (0.5 / D) * jnp.sum(diff * diff, axis=0, keepdims=True)
        dxh = dy * gf_ref[...]
        dx2_ref[...] = r3 * (dxh - xh * jnp.mean(dxh * xh, axis=-1, keepdims=True))

    return pl.pallas_call(
        body, name="ffn_fwd",
        grid=(t_len // tm,),
        in_specs=[_rows(tm, D), _rows(tm, D), _resident((1, D)), _resident((D_FF, D)), _resident((D_FF, D)),
                  _resident((D_FF, D)), _resident((1, D))],
        out_specs=[_rows(tm, D_FF), _rows(tm, D_FF), _rows(tm, D), _const((8, D))],
        out_shape=[jax.ShapeDtypeStruct((t_len, D_FF), BF16), jax.ShapeDtypeStruct((t_len, D_FF), BF16),
                   jax.ShapeDtypeStruct((t_len, D), F32), jax.ShapeDtypeStruct((8, D), F32)],
        compiler_params=_params(("arbitrary",)),
    )(x1, target, g_ffn, w_gate, w_up, w_down, g_final)


def ffn_bwd(dx2, x1, gate, up, g_ffn, w_gate, w_up, w_down):
    t_len = x1.shape[0]
    tm = min(256, t_len)

    def body(dx2_ref, x1_ref, gate_ref, up_ref, g_ref, wg_ref, wu_ref, wd_ref,
             dx1_ref, dgate_ref, dup_ref, act_ref, h2_ref, stat_ref):
        @pl.when(pl.program_id(0) == 0)
        def _():
            stat_ref[...] = jnp.zeros((8, D), F32)

        dx2v = dx2_ref[...]
        gate = gate_ref[...].astype(F32)
        upv = up_ref[...].astype(F32)
        sg = _sigmoid(gate)
        silu = gate * sg
        act_ref[...] = (silu * upv).astype(BF16)
        dact = _dot_nt(dx2v.astype(BF16), wd_ref[...])
        dup = (dact * silu).astype(BF16)
        dgate = (dact * upv * (sg * (1.0 + gate * (1.0 - sg)))).astype(BF16)
        dup_ref[...] = dup
        dgate_ref[...] = dgate
        dh2 = _dot(dgate, wg_ref[...]) + _dot(dup, wu_ref[...])
        x1v = x1_ref[...]
        r2 = lax.rsqrt(jnp.mean(x1v * x1v, axis=-1, keepdims=True) + EPS)
        xh = x1v * r2
        h2_ref[...] = (xh * g_ref[...]).astype(BF16)
        stat_ref[0:1, :] += jnp.sum(dh2 * xh, axis=0, keepdims=True)
        dxh = dh2 * g_ref[...]
        dx1_ref[...] = dx2v + r2 * (dxh - xh * jnp.mean(dxh * xh, axis=-1, keepdims=True))

    ff = jax.ShapeDtypeStruct((t_len, D_FF), BF16)
    return pl.pallas_call(
        body, name="ffn_bwd",
        grid=(t_len // tm,),
        in_specs=[_rows(tm, D), _rows(tm, D), _rows(tm, D_FF), _rows(tm, D_FF), _resident((1, D)),
                  _resident((D_FF, D)), _resident((D_FF, D)), _resident((D_FF, D))],
        out_specs=[_rows(tm, D), _rows(tm, D_FF), _rows(tm, D_FF), _rows(tm, D_FF), _rows(tm, D), _const((8, D))],
        out_shape=[jax.ShapeDtypeStruct((t_len, D), F32), ff, ff, ff, jax.ShapeDtypeStruct((t_len, D), BF16),
                   jax.ShapeDtypeStruct((8, D), F32)],
        compiler_params=_params(("arbitrary",)),
    )(dx2, x1, gate, up, g_ffn, w_gate, w_up, w_down)


def mixer_bwd(dx1, proj, ya, pp, yx, probs, conv_w8, w_co, w_pool, pool_scale, k, v, w_xo, w_out, carry=None):
    t_len = dx1.shape[0]
    tm = min(256, t_len)
    n_tiles = t_len // tm
    halo_blocks = tm // HALO

    def body(dx1_ref, proj_ref, halo_ref, ya_ref, pp_ref, yx_ref, p_ref,
             cw_ref, wco_ref, wpool_ref, ps_ref, k_ref, v_ref, wxo_ref, wout_ref, band_ref,
             dproj_ref, merged_ref, dya_ref, dpp_ref, dyx_ref, dk_ref, dv_ref, stat_ref,
             cu_ext, dconv_ext, dpn_ext):
        step = pl.program_id(0)
        tile = n_tiles - 1 - step

        @pl.when(step == 0)
        def _():
            dk_ref[...] = jnp.zeros((N_MEM, D), F32)
            dv_ref[...] = jnp.zeros((N_MEM, D), F32)
            stat_ref[...] = jnp.zeros((8, D), F32)
            dconv_ext[tm:tm + HALO, :] = jnp.zeros((HALO, D), F32)
            dpn_ext[tm:tm + POOL_PAD, :] = jnp.zeros((POOL_PAD, D), BF16)

        dmerged = _dot_nt(dx1_ref[...].astype(BF16), wout_ref[...])
        sa = _sigmoid(proj_ref[:, O_GA:O_GA + D].astype(F32))
        sp = _sigmoid(proj_ref[:, O_GP:O_GP + D].astype(F32))
        sx = _sigmoid(proj_ref[:, O_GX:O_GX + D].astype(F32))
        ya = ya_ref[...].astype(F32)
        ppv = pp_ref[...].astype(F32)
        yp = ppv * ps_ref[...]
        yx = yx_ref[...].astype(F32)
        merged_ref[...] = (sa * ya + sp * yp + sx * yx).astype(BF16)
        dproj_ref[:, O_GA:O_GA + D] = (dmerged * ya * (sa * (1.0 - sa))).astype(BF16)
        dproj_ref[:, O_GP:O_GP + D] = (dmerged * yp * (sp * (1.0 - sp))).astype(BF16)
        dproj_ref[:, O_GX:O_GX + D] = (dmerged * yx * (sx * (1.0 - sx))).astype(BF16)
        dya = (dmerged * sa).astype(BF16)
        dyp = dmerged * sp
        dyx = (dmerged * sx).astype(BF16)
        dya_ref[...] = dya
        dyx_ref[...] = dyx
        stat_ref[1:2, :] += jnp.sum(dyp * ppv, axis=0, keepdims=True)
        dpp = (dyp * ps_ref[...]).astype(BF16)
        dpp_ref[...] = dpp

        da = _dot_nt(dya, wco_ref[...])
        c_a = proj_ref[:, O_CA:O_CA + D].astype(F32)
        u_a = proj_ref[:, O_UA:O_UA + D].astype(F32)
        cu = c_a * u_a
        halo_cu = halo_ref[:, O_CA:O_CA + D].astype(F32) * halo_ref[:, O_UA:O_UA + D].astype(F32)
        cu_ext[0:HALO, :] = jnp.where(tile > 0, halo_cu, 0.0)
        cu_ext[HALO:HALO + tm, :] = cu
        cu1 = cu_ext[HALO - 1:HALO - 1 + tm, :]
        cu2 = cu_ext[HALO - 2:HALO - 2 + tm, :]
        conv = cw_ref[2:3, :] * cu + cw_ref[1:2, :] * cu1 + cw_ref[0:1, :] * cu2
        dproj_ref[:, O_BA:O_BA + D] = (da * conv).astype(BF16)
        dconv = da * proj_ref[:, O_BA:O_BA + D].astype(F32)
        stat_ref[5:6, :] += jnp.sum(dconv * cu2, axis=0, keepdims=True)
        stat_ref[6:7, :] += jnp.sum(dconv * cu1, axis=0, keepdims=True)
        stat_ref[7:8, :] += jnp.sum(dconv * cu, axis=0, keepdims=True)
        dconv_ext[0:tm, :] = dconv
        dcu = (cw_ref[2:3, :] * dconv + cw_ref[1:2, :] * dconv_ext[1:1 + tm, :]
               + cw_ref[0:1, :] * dconv_ext[2:2 + tm, :])
        dproj_ref[:, O_CA:O_CA + D] = (dcu * u_a).astype(BF16)
        dproj_ref[:, O_UA:O_UA + D] = (dcu * c_a).astype(BF16)

        for g, window in enumerate(POOL_WINDOWS):
            cols = slice(g * GROUP_DIM, (g + 1) * GROUP_DIM)
            dpooled = _dot_nt(dpp[:, cols], wpool_ref[g])
            dpn_ext[0:tm, cols] = (dpooled * _inv_count(tile, tm, window)).astype(BF16)
            acc = _dot(band_ref[g], dpn_ext[:, cols])
            dproj_ref[:, O_UP + g * GROUP_DIM:O_UP + (g + 1) * GROUP_DIM] = (acc - dpooled).astype(BF16)

        do = _dot_nt(dyx, wxo_ref[...])
        for hd in range(HEADS):
            cols = slice(hd * HEAD_DIM, (hd + 1) * HEAD_DIM)
            q = proj_ref[:, O_QX + hd * HEAD_DIM:O_QX + (hd + 1) * HEAD_DIM]
            kh = k_ref[:, cols]
            p16 = p_ref[:, hd * N_MEM:(hd + 1) * N_MEM]
            p = p16.astype(F32)
            doh = do[:, cols].astype(BF16)
            dp = _dot_nt(doh, v_ref[:, cols])
            dv_ref[:, cols] += _dot_tn(p16, doh)
            ds = (p * (dp - jnp.sum(dp * p, axis=-1, keepdims=True)) * ATT_SCALE).astype(BF16)
            dproj_ref[:, O_QX + hd * HEAD_DIM:O_QX + (hd + 1) * HEAD_DIM] = _dot(ds, kh).astype(BF16)
            dk_ref[:, cols] += _dot_tn(ds, q)

        dconv_ext[tm:tm + HALO, :] = dconv_ext[0:HALO, :]
        dpn_ext[tm:tm + HALO, :] = dpn_ext[0:HALO, :]

    def rev(width):
        return pl.BlockSpec((tm, width), lambda s: (n_tiles - 1 - s, 0))

    halo_spec = pl.BlockSpec((HALO, D_IN), lambda s: (jnp.maximum((n_tiles - 1 - s) * halo_blocks - 1, 0), 0))
    act = jax.ShapeDtypeStruct((t_len, D), BF16)
    kv_grad = jax.ShapeDtypeStruct((N_MEM, D), F32)
    return _carried_call(
        body, "mixer_bwd", (n_tiles,),
        in_specs=[rev(D), rev(D_IN), halo_spec, rev(D), rev(D), rev(D), rev(D),
                  _resident((8, D)), _resident((D, D)), _resident((GROUPS, GROUP_DIM, GROUP_DIM)), _resident((1, D)),
                  _resident((N_MEM, D)), _resident((N_MEM, D)), _resident((D, D)), _resident((D, D)),
                  _resident((GROUPS, tm, tm + POOL_PAD))],
        out_specs=[rev(D_IN), rev(D), rev(D), rev(D), rev(D),
                   _const((N_MEM, D)), _const((N_MEM, D)), _const((8, D))],
        out_shape=[jax.ShapeDtypeStruct((t_len, D_IN), BF16), act, act, act, act, kv_grad, kv_grad,
                   jax.ShapeDtypeStruct((8, D), F32)],
        scratch_shapes=[pltpu.VMEM((tm + HALO, D), F32)] * 2 + [pltpu.VMEM((tm + POOL_PAD, D), BF16)],
        semantics=("arbitrary",),
        args=(dx1, proj, proj, ya, pp, yx, probs, conv_w8, w_co, w_pool, pool_scale, k, v, w_xo, w_out,
              _window_bands(tm, False)), carry=carry)


def in_bwd(dproj, w_in4, x, dx1, g_mix, carry=None):
    t_len = x.shape[0]
    tm = min(1024, t_len)
    tk = D_IN // N_CHIPS

    def body(dproj_ref, w_ref, x_ref, dx1_ref, g_ref, gx_ref, stat_ref, acc_ref):
        i, j = pl.program_id(0), pl.program_id(1)

        @pl.when((i == 0) & (j == 0))
        def _():
            stat_ref[...] = jnp.zeros((8, D), F32)

        @pl.when(j == 0)
        def _():
            acc_ref[...] = jnp.zeros((tm, D), F32)

        acc_ref[...] += _dot_nt(dproj_ref[...], w_ref[...])

        @pl.when(j == N_CHIPS - 1)
        def _():
            dh = acc_ref[...]
            xv = x_ref[...]
            r = lax.rsqrt(jnp.mean(xv * xv, axis=-1, keepdims=True) + EPS)
            xh = xv * r
            stat_ref[0:1, :] += jnp.sum(dh * xh, axis=0, keepdims=True)
            dxh = dh * g_ref[...]
            gx_ref[...] = dx1_ref[...] + r * (dxh - xh * jnp.mean(dxh * xh, axis=-1, keepdims=True))

    return _carried_call(
        body, "in_bwd", (t_len // tm, N_CHIPS),
        in_specs=[pl.BlockSpec((tm, tk), lambda i, j: (i, j)),
                  pl.BlockSpec((None, D, tk), lambda i, j: (j, 0, 0)),
                  pl.BlockSpec((tm, D), lambda i, j: (i, 0)),
                  pl.BlockSpec((tm, D), lambda i, j: (i, 0)),
                  pl.BlockSpec((1, D), lambda i, j: (0, 0))],
        out_specs=[pl.BlockSpec((tm, D), lambda i, j: (i, 0)), pl.BlockSpec((8, D), lambda i, j: (0, 0))],
        out_shape=[jax.ShapeDtypeStruct((t_len, D), F32), jax.ShapeDtypeStruct((8, D), F32)],
        scratch_shapes=[pltpu.VMEM((tm, D), F32)], semantics=("arbitrary", "arbitrary"),
        args=(dproj, w_in4, x, dx1, g_mix), carry=carry)


def kv_bwd(dk, dv, memn, mem, g_mem, w_kv4):
    half = D // 2

    def body(dk_ref, dv_ref, memn_ref, mem_ref, g_ref, w_ref, gw_ref, gw16_ref, stat_ref):
        mn = memn_ref[...]
        parts = (dk_ref[:, 0:half], dk_ref[:, half:D], dv_ref[:, 0:half], dv_ref[:, half:D])
        dmemn = jnp.zeros((N_MEM, D), F32)
        for j, part in enumerate(parts):
            part = part.astype(BF16)
            gw = _dot_tn(mn, part)
            gw_ref[j] = gw
            gw16_ref[j] = gw.astype(BF16)
            dmemn = dmemn + _dot_nt(part, w_ref[j])
        mv = mem_ref[...]
        r = lax.rsqrt(jnp.mean(mv * mv, axis=-1, keepdims=True) + EPS)
        stat_ref[...] = jnp.zeros((8, D), F32)
        stat_ref[0:1, :] = jnp.sum(dmemn * (mv * r), axis=0, keepdims=True)

    return pl.pallas_call(
        body, name="kv_bwd",
        out_shape=[jax.ShapeDtypeStruct((N_CHIPS, D, half), F32), jax.ShapeDtypeStruct((N_CHIPS, D, half), BF16),
                   jax.ShapeDtypeStruct((8, D), F32)],
        compiler_params=_params(),
    )(dk, dv, memn, mem, g_mem, w_kv4)


def matmul_tn(name, a, b, tn, col_blocks=1, k_dim=None, n_dim=None, a_col=0, b_col=0, carry=None):
    t_len = a.shape[0]
    k_dim = a.shape[1] if k_dim is None else k_dim
    n_dim = b.shape[1] if n_dim is None else n_dim
    b_off = b_col * (n_dim // tn)
    tt = min(512, t_len)
    per_block = n_dim // col_blocks // tn

    def body(a_ref, b_ref, out_ref, out16_ref):
        @pl.when(pl.program_id(1) == 0)
        def _():
            out_ref[...] = jnp.zeros((k_dim, tn), F32)

        out_ref[...] += _dot_tn(a_ref[...].astype(BF16), b_ref[...].astype(BF16))

        @pl.when(pl.program_id(1) == t_len // tt - 1)
        def _():
            out16_ref[...] = out_ref[...].astype(BF16)

    if col_blocks == 1:
        out_spec = pl.BlockSpec((k_dim, tn), lambda n, t: (0, n))
        shape = (k_dim, n_dim)
    else:
        out_spec = pl.BlockSpec((None, k_dim, tn), lambda n, t: (n // per_block, 0, n % per_block))
        shape = (col_blocks, k_dim, n_dim // col_blocks)
    outs, carried = _carried_call(
        body, name, (n_dim // tn, t_len // tt),
        in_specs=[pl.BlockSpec((tt, k_dim), lambda n, t: (t, a_col)), pl.BlockSpec((tt, tn), lambda n, t: (t, b_off + n))],
        out_specs=[out_spec, out_spec], out_shape=[jax.ShapeDtypeStruct(shape, F32), jax.ShapeDtypeStruct(shape, BF16)],
        scratch_shapes=[], semantics=("arbitrary", "arbitrary"), args=(a, b), carry=carry)
    return (tuple(outs), carried) if carry is not None else tuple(outs)


def _place():
    x, y, c = lax.axis_index("x"), lax.axis_index("y"), lax.axis_index("c")
    return x, y, c


def _other_chips(x, y):
    return [(1 - x, y), (x, 1 - y), (1 - x, 1 - y)]


def _any_specs(n):
    return [pl.BlockSpec(memory_space=pl.ANY)] * n


def gather_weights(shards):
    n = len(shards)

    def body(*refs):
        gather = _Gather(refs[:n], refs[n:2 * n], *refs[2 * n:])
        gather.start()
        gather.finish()

    return pl.pallas_call(
        body, name="gather_weights",
        in_specs=_any_specs(n), out_specs=_any_specs(n),
        out_shape=_Gather.out_shapes(shards), scratch_shapes=_Gather.sems(n),
    )(*shards)


class _Gather:
    @staticmethod
    def out_shapes(shards):
        return [jax.ShapeDtypeStruct((N_CHIPS,) + s.shape, s.dtype) for s in shards]

    @staticmethod
    def sems(n):
        return [pltpu.SemaphoreType.DMA((n, 7)), pltpu.SemaphoreType.DMA((n, 7))]

    def __init__(self, ins, outs, send_sems, recv_sems):
        self.ins, self.outs, self.send_sems, self.recv_sems = ins, outs, send_sems, recv_sems
        x, y, c = _place()
        self.c, self.me, self.sibling, self.chips = c, 2 * x + y, (x, y, 1 - c), _other_chips(x, y)

    def _half(self, ref, which):
        rows = ref.shape[1]
        return ref.at[:, pl.ds(which * (rows // 2), rows // 2), :]

    def _remote(self, src, dst, a, slot, to):
        return pltpu.make_async_remote_copy(src_ref=src, dst_ref=dst, send_sem=self.send_sems.at[a, slot],
                                            recv_sem=self.recv_sems.at[a, slot], device_id=to, device_id_type=MESH)

    def _own(self, a):
        return self._remote(self.ins[a], self.outs[a].at[self.me], a, 6, self.sibling)

    def _sent(self, a, slot):
        px, py = self.chips[slot]
        return self._remote(self._half(self.ins[a], self.c), self._half(self.outs[a].at[self.me], self.c), a, slot,
                            (px, py, self.c))

    def _passed_on(self, a, slot, which):
        px, py = self.chips[slot]
        block = self._half(self.outs[a].at[2 * px + py], which)
        return self._remote(block, block, a, 3 + slot, self.sibling)

    def start(self):
        for a in range(len(self.ins)):
            self._own(a).start()
        for a in range(len(self.ins)):
            for slot in range(3):
                self._sent(a, slot).start()

    def finish(self):
        n = len(self.ins)
        for a in range(n):
            for slot in range(3):
                px, py = self.chips[slot]
                landed = self._half(self.outs[a].at[2 * px + py], self.c)
                self._remote(landed, landed, a, slot, (px, py, self.c)).wait_recv()
                self._passed_on(a, slot, self.c).start()
        for a in range(n):
            for slot in range(3):
                self._passed_on(a, slot, 1 - self.c).wait_recv()
        for a in range(n):
            self._own(a).wait_recv()
        for a in range(n):
            self._own(a).wait_send()
            for slot in range(3):
                self._sent(a, slot).wait_send()
                self._passed_on(a, slot, self.c).wait_send()


def exchange_sibling_halves(name, grads):
    n = len(grads)

    def body(*refs):
        ins, outs = refs[:n], refs[n:2 * n]
        send_sems, recv_sems = refs[2 * n:]
        x, y, c = _place()
        copies = []
        for a in range(n):
            h = ins[a].shape[1] // 2
            cp = pltpu.make_async_remote_copy(
                src_ref=ins[a].at[:, pl.ds((1 - c) * h, h), :], dst_ref=outs[a],
                send_sem=send_sems.at[a], recv_sem=recv_sems.at[a], device_id=(x, y, 1 - c), device_id_type=MESH)
            cp.start()
            copies.append(cp)
        for cp in copies:
            cp.wait()

    return pl.pallas_call(
        body, name=name,
        in_specs=_any_specs(n), out_specs=_any_specs(n),
        out_shape=[jax.ShapeDtypeStruct((N_CHIPS, g.shape[1] // 2, g.shape[2]), g.dtype) for g in grads],
        scratch_shapes=[pltpu.SemaphoreType.DMA((n,)), pltpu.SemaphoreType.DMA((n,))],
    )(*grads)


def _row_tile(rows, cols, budget=1 << 20):
    best = 16
    for tr in range(16, rows + 1, 16):
        if rows % tr == 0 and tr * cols * 4 <= budget:
            best = tr
    return best


def add_sibling_half(name, grad, got, core):
    _, rows, cols = grad.shape
    h = rows // 2
    tr = _row_tile(h, cols)
    per_half = h // tr

    def body(core_ref, g_ref, o_ref, out_ref, out16_ref):
        total = g_ref[...] + o_ref[...].astype(F32)
        out_ref[...] = total
        out16_ref[...] = total.astype(BF16)

    out_spec = pl.BlockSpec((None, tr, cols), lambda j, r, cr: (j, r, 0))
    return pl.pallas_call(
        body, name=name,
        grid_spec=pltpu.PrefetchScalarGridSpec(
            num_scalar_prefetch=1, grid=(N_CHIPS, per_half),
            in_specs=[pl.BlockSpec((None, tr, cols), lambda j, r, cr: (j, cr[0] * per_half + r, 0)),
                      pl.BlockSpec((None, tr, cols), lambda j, r, cr: (j, r, 0))],
            out_specs=[out_spec, out_spec]),
        out_shape=[jax.ShapeDtypeStruct((N_CHIPS, h, cols), F32), jax.ShapeDtypeStruct((N_CHIPS, h, cols), BF16)],
        compiler_params=_params(("arbitrary", "arbitrary")),
    )(core, grad, got)


class _ChipExchange:
    @staticmethod
    def out_shapes(partials):
        return [jax.ShapeDtypeStruct((3,) + p.shape[1:], p.dtype) for p in partials]

    @staticmethod
    def sems(n):
        return [pltpu.SemaphoreType.DMA((n, 3)), pltpu.SemaphoreType.DMA((n, 3))]

    def __init__(self, ins, outs, send_sems, recv_sems):
        self.ins, self.outs, self.send_sems, self.recv_sems = ins, outs, send_sems, recv_sems

    def _copies(self):
        x, y, c = _place()
        for a in range(len(self.ins)):
            for slot, (px, py) in enumerate(_other_chips(x, y)):
                yield pltpu.make_async_remote_copy(
                    src_ref=self.ins[a].at[2 * px + py], dst_ref=self.outs[a].at[slot],
                    send_sem=self.send_sems.at[a, slot], recv_sem=self.recv_sems.at[a, slot],
                    device_id=(px, py, c), device_id_type=MESH)

    def start(self):
        for cp in self._copies():
            cp.start()

    def finish(self):
        for cp in self._copies():
            cp.wait()


def add_chip_blocks(name, partial, got, chip):
    _, h, cols = partial.shape
    tr = _row_tile(h, cols)

    def body(chip_ref, p_ref, g0_ref, g1_ref, g2_ref, out_ref):
        out_ref[...] = ((p_ref[...] + g0_ref[...].astype(F32)) + g1_ref[...].astype(F32)) + g2_ref[...].astype(F32)

    def got_spec(slot):
        return pl.BlockSpec((None, tr, cols), lambda r, ch: (slot, r, 0))

    return pl.pallas_call(
        body, name=name,
        grid_spec=pltpu.PrefetchScalarGridSpec(
            num_scalar_prefetch=1, grid=(h // tr,),
            in_specs=[pl.BlockSpec((None, tr, cols), lambda r, ch: (ch[0], r, 0)), got_spec(0), got_spec(1), got_spec(2)],
            out_specs=pl.BlockSpec((tr, cols), lambda r, ch: (r, 0))),
        out_shape=jax.ShapeDtypeStruct((h, cols), F32),
        compiler_params=_params(("arbitrary",)),
    )(chip, partial, got, got, got)


class _SiblingSwap:
    @staticmethod
    def out_shapes(halves):
        return [jax.ShapeDtypeStruct(v.shape, v.dtype) for v in halves]

    @staticmethod
    def sems(n):
        return [pltpu.SemaphoreType.DMA((n,)), pltpu.SemaphoreType.DMA((n,))]

    def __init__(self, ins, outs, send_sems, recv_sems):
        self.ins, self.outs, self.send_sems, self.recv_sems = ins, outs, send_sems, recv_sems

    def _copies(self):
        x, y, c = _place()
        for a in range(len(self.ins)):
            yield pltpu.make_async_remote_copy(
                src_ref=self.ins[a], dst_ref=self.outs[a], send_sem=self.send_sems.at[a], recv_sem=self.recv_sems.at[a],
                device_id=(x, y, 1 - c), device_id_type=MESH)

    def start(self):
        for cp in self._copies():
            cp.start()

    def finish(self):
        for cp in self._copies():
            cp.wait()


def swap_sibling_halves(halves):
    n = len(halves)

    def body(*refs):
        swap = _SiblingSwap(refs[:n], refs[n:2 * n], *refs[2 * n:])
        swap.start()
        swap.finish()

    return pl.pallas_call(
        body, name="swap_sibling_halves",
        in_specs=_any_specs(n), out_specs=_any_specs(n),
        out_shape=_SiblingSwap.out_shapes(halves), scratch_shapes=_SiblingSwap.sems(n),
    )(*halves)


def all_reduce_small(pack):
    rows = pack.shape[0]

    def body(pack_ref, out_ref, gathered, send_sems, recv_sems):
        x, y, c = _place()
        me = 4 * x + 2 * y + c
        gathered[me] = pack_ref[...]
        copies = []
        for rel in range(1, 8):
            fx, fy, fc = (rel >> 2) & 1, (rel >> 1) & 1, rel & 1
            peer = (x ^ fx, y ^ fy, c ^ fc)
            cp = pltpu.make_async_remote_copy(
                src_ref=pack_ref, dst_ref=gathered.at[me], send_sem=send_sems.at[rel - 1], recv_sem=recv_sems.at[rel - 1],
                device_id=peer, device_id_type=MESH)
            cp.start()
            copies.append(cp)
        for rel in range(1, 8):
            fx, fy, fc = (rel >> 2) & 1, (rel >> 1) & 1, rel & 1
            src = 4 * (x ^ fx) + 2 * (y ^ fy) + (c ^ fc)
            pltpu.make_async_remote_copy(
                src_ref=pack_ref, dst_ref=gathered.at[src], send_sem=send_sems.at[rel - 1], recv_sem=recv_sems.at[rel - 1],
                device_id=(x, y, c), device_id_type=MESH).wait_recv()
        for cp in copies:
            cp.wait_send()
        total = gathered[0]
        for dev in range(1, 8):
            total = total + gathered[dev]
        out_ref[...] = total

    return pl.pallas_call(
        body, name="all_reduce_small",
        in_specs=[pl.BlockSpec(memory_space=pltpu.VMEM)], out_specs=pl.BlockSpec(memory_space=pltpu.VMEM),
        out_shape=jax.ShapeDtypeStruct((rows, D), F32),
        scratch_shapes=[pltpu.VMEM((8, rows, D), F32), pltpu.SemaphoreType.DMA((7,)), pltpu.SemaphoreType.DMA((7,))],
    )(pack)


def _adamw_update(w, g, m, v):
    nm = ADAM_B1 * m + (1.0 - ADAM_B1) * g
    nv = ADAM_B2 * v + (1.0 - ADAM_B2) * (g * g)
    m_hat = nm / (1.0 - ADAM_B1 ** ADAM_STEP)
    v_hat = nv / (1.0 - ADAM_B2 ** ADAM_STEP)
    delta = -ADAM_LR * (m_hat / (jnp.sqrt(v_hat) + ADAM_EPS) + ADAM_WD * w)
    return delta, nm, nv


def adamw(name, w, g, m, v):
    def body(w_ref, g_ref, m_ref, v_ref, d_ref, nm_ref, nv_ref):
        d_ref[...], nm_ref[...], nv_ref[...] = _adamw_update(w_ref[...], g_ref[...], m_ref[...], v_ref[...])

    out = jax.ShapeDtypeStruct(w.shape, F32)
    return pl.pallas_call(body, name=name, out_shape=[out] * 3, compiler_params=_params())(w, g, m, v)


def adamw_halves(name, w, mine, theirs, m, v, core):
    h, cols = mine.shape
    tr = _row_tile(h, cols)
    per_half = h // tr

    def body(core_ref, w_ref, mine_ref, theirs_ref, m_ref, v_ref, g_ref, d_ref, nm_ref, nv_ref):
        g = jnp.where(pl.program_id(0) == core_ref[0], mine_ref[...], theirs_ref[...])
        g_ref[...] = g
        d_ref[...], nm_ref[...], nv_ref[...] = _adamw_update(w_ref[...], g, m_ref[...], v_ref[...])

    full = pl.BlockSpec((tr, cols), lambda hh, r, cr: (hh * per_half + r, 0))
    mine_spec = pl.BlockSpec((tr, cols), lambda hh, r, cr: (jnp.where(hh == cr[0], r, 0), 0))
    theirs_spec = pl.BlockSpec((tr, cols), lambda hh, r, cr: (jnp.where(hh == cr[0], 0, r), 0))
    out = jax.ShapeDtypeStruct((2 * h, cols), F32)
    return pl.pallas_call(
        body, name=name,
        grid_spec=pltpu.PrefetchScalarGridSpec(
            num_scalar_prefetch=1, grid=(2, per_half),
            in_specs=[full, mine_spec, theirs_spec, full, full], out_specs=[full] * 4),
        out_shape=[out] * 4,
        compiler_params=_params(("arbitrary", "arbitrary")),
    )(core, w, mine, theirs, m, v)


BIG = ("w_in", "w_conv_out", "w_pool", "w_kv", "w_xattn_out", "w_out", "w_gate", "w_up", "w_down")


def kernel(x, mem, norm_mix, w_in, conv_w, w_conv_out, w_pool, pool_scale, norm_mem, w_kv, w_xattn_out, w_out, norm_ffn, w_gate, w_up, w_down, norm_final, loss_target, m_norm_mix, m_w_in, m_conv_w, m_w_conv_out, m_w_pool, m_pool_scale, m_norm_mem, m_w_kv, m_w_xattn_out, m_w_out, m_norm_ffn, m_w_gate, m_w_up, m_w_down, m_norm_final, v_norm_mix, v_w_in, v_conv_w, v_w_conv_out, v_w_pool, v_pool_scale, v_norm_mem, v_w_kv, v_w_xattn_out, v_w_out, v_norm_ffn, v_w_gate, v_w_up, v_w_down, v_norm_final):
    t_len = x.shape[1]
    xi, yi, ci = lax.axis_index("x"), lax.axis_index("y"), lax.axis_index("c")
    chip = 2 * xi + yi
    chip_arr = jnp.reshape(chip, (1,)).astype(jnp.int32)
    core_arr = jnp.reshape(ci, (1,)).astype(jnp.int32)

    conv_pad = jnp.concatenate([conv_w, jnp.zeros((1, 13, 256), F32)], axis=1)
    def t2(w):
        return jnp.swapaxes(w, 1, 2)

    (g_in,) = gather_weights([w_in.astype(BF16)])
    w_in4 = g_in.reshape(N_CHIPS, D, D_IN // N_CHIPS)

    x2d = x[0]
    tgt = loss_target[0]
    (proj, h), (g_kv, g_conv_w, g_co, g_xo, g_out, g_pool, g_gate) = proj_fwd(
        x2d, norm_mix, w_in4,
        carry=(_Gather, [w_kv.astype(BF16), conv_pad,
                         w_conv_out.astype(BF16), w_xattn_out.astype(BF16), w_out.astype(BF16),
                         w_pool[0].astype(BF16),
                         t2(w_gate).astype(BF16)]))
    w_kv4 = g_kv.reshape(N_CHIPS, D, D // 2)
    conv_full = jnp.transpose(g_conv_w[:, 0, 0:8, :], (1, 0, 2)).reshape(8, D)
    w_co_f, w_xo_f, w_out_f = g_co.reshape(D, D), g_xo.reshape(D, D), g_out.reshape(D, D)
    w_pool_f = jnp.transpose(g_pool, (1, 0, 2, 3)).reshape(GROUPS, GROUP_DIM, GROUP_DIM)
    memn, k, v = kv_fwd(mem[0], norm_mem, w_kv4)
    (a, pooled, ya, pp, yx, o, probs, x1), (g_up, g_down) = mixer_fwd(
        proj, x2d, conv_full, w_co_f, w_pool_f, pool_scale, k, v, w_xo_f, w_out_f,
        carry=(_Gather, [t2(w_up).astype(BF16), w_down.astype(BF16)]))
    w_gate_f, w_up_f, w_down_f = g_gate.reshape(D_FF, D), g_up.reshape(D_FF, D), g_down.reshape(D_FF, D)
    gate, up, dx2, stat_f = ffn_fwd(x1, tgt, norm_ffn, w_gate_f, w_up_f, w_down_f, norm_final.reshape(1, D))

    def by_chip(pair):
        return tuple(gw.reshape(N_CHIPS, gw.shape[0] // N_CHIPS, gw.shape[1]) for gw in pair)

    def chip_partials(tag, names, grads):
        got = exchange_sibling_halves("exchange_sibling_halves_" + tag, [g16 for _, g16 in grads])
        return [add_sibling_half("add_sibling_" + n, g32, o_, core_arr) for n, (g32, _), o_ in zip(names, grads, got)]

    def chip_sums(names, partials, got):
        return [add_chip_blocks("add_chips_" + n, p32, g2, chip_arr) for n, (p32, _), g2 in zip(names, partials, got)]

    dx1, dgate, dup, act, h2, stat_b1 = ffn_bwd(dx2, x1, gate, up, norm_ffn, w_gate_f, w_up_f, w_down_f)
    gw_gate = matmul_tn("grad_w_gate", dgate, h2, 512)
    gw_up = matmul_tn("grad_w_up", dup, h2, 512)
    gw_down = matmul_tn("grad_w_down", act, dx2, 512)
    names_ffn = ("w_gate", "w_up", "w_down")
    part_ffn = chip_partials("ffn", names_ffn, [by_chip(gw_gate), by_chip(gw_up), by_chip(gw_down)])

    (dproj, merged, dya, dpp, dyx, dk, dv, stat_b2), got_ffn = mixer_bwd(
        dx1, proj, ya, pp, yx, probs, conv_full, w_co_f, w_pool_f, pool_scale, k, v, w_xo_f, w_out_f,
        carry=(_ChipExchange, [p16 for _, p16 in part_ffn]))
    gw_kv32, gw_kv16, stat_kv = kv_bwd(dk, dv, memn, mem[0], norm_mem, w_kv4)
    gw_co = matmul_tn("grad_w_conv_out", a, dya, 1024)
    gw_xo = matmul_tn("grad_w_xattn_out", o, dyx, 1024)
    gw_out = matmul_tn("grad_w_out", merged, dx1, 1024)
    gw_pool_g = [matmul_tn("grad_w_pool_%d" % g, pooled, dpp, GROUP_DIM, k_dim=GROUP_DIM, n_dim=GROUP_DIM,
                           a_col=g, b_col=g) for g in range(GROUPS)]
    gw_pool = tuple(jnp.transpose(jnp.stack([pair[k] for pair in gw_pool_g]).reshape(GROUPS, N_CHIPS, 64, GROUP_DIM),
                                  (1, 0, 2, 3)).reshape(N_CHIPS, 256, GROUP_DIM) for k in range(2))
    names_mix = ("w_conv_out", "w_pool", "w_kv", "w_xattn_out", "w_out")
    part_mix = chip_partials("mixer", names_mix,
                             [by_chip(gw_co), gw_pool, (gw_kv32, gw_kv16), by_chip(gw_xo), by_chip(gw_out)])

    gw_in, got_mix = matmul_tn("grad_w_in", h, dproj, 2048, col_blocks=N_CHIPS,
                               carry=(_ChipExchange, [p16 for _, p16 in part_mix]))
    part_in = chip_partials("in", ("w_in",), [gw_in])
    mine_early = chip_sums(names_ffn + names_mix, part_ffn + part_mix, list(got_ffn) + list(got_mix))
    (grad_x, stat_b3), (got_in, theirs_early) = in_bwd(
        dproj, w_in4, x2d, dx1, norm_mix,
        carry=[(_ChipExchange, [p16 for _, p16 in part_in]), (_SiblingSwap, mine_early)])
    mine_in = chip_sums(("w_in",), part_in, got_in)
    theirs_in = swap_sibling_halves(mine_in)
    reduced = dict(zip(names_ffn + names_mix + ("w_in",),
                       zip(mine_early + mine_in, list(theirs_early) + list(theirs_in))))
    mine = [reduced[n][0] for n in BIG]
    theirs = [reduced[n][1] for n in BIG]

    pack = jnp.concatenate([stat_b3[0:1], stat_b2[1:2], stat_kv[0:1], stat_b1[0:1], stat_f[0:1], stat_b2[5:8],
                            stat_f[1:2], jnp.zeros((7, D), F32)], axis=0)
    total = all_reduce_small(pack)
    loss = jnp.sum(total[8])
    g_conv_full = total[5:8]
    g_conv = lax.dynamic_slice_in_dim(g_conv_full, chip * 256, 256, axis=1)

    given = dict(w_in=(w_in, m_w_in, v_w_in), w_conv_out=(w_conv_out, m_w_conv_out, v_w_conv_out),
                 w_pool=(w_pool, m_w_pool, v_w_pool), w_kv=(w_kv, m_w_kv, v_w_kv),
                 w_xattn_out=(w_xattn_out, m_w_xattn_out, v_w_xattn_out), w_out=(w_out, m_w_out, v_w_out),
                 w_gate=(w_gate, m_w_gate, v_w_gate), w_up=(w_up, m_w_up, v_w_up), w_down=(w_down, m_w_down, v_w_down))
    out_g, out_d, out_m, out_v = {}, {}, {}, {}
    for n, mine_n, theirs_n in zip(BIG, mine, theirs):
        transposed = n in ("w_gate", "w_up")
        rows2d = (2 * mine_n.shape[0], mine_n.shape[1])
        w_, m_, v_ = ((t2(t) if transposed else t).reshape(rows2d) for t in given[n])
        res = adamw_halves("adamw_" + n, w_, mine_n, theirs_n, m_, v_, core_arr)
        if transposed:
            res = [t2(t.reshape(1, D_FF // N_CHIPS, D)) for t in res]
        out_g[n], out_d[n], out_m[n], out_v[n] = (t.reshape(given[n][0].shape) for t in res)

    def small_pack(vals, conv_part):
        conv_rows = jnp.concatenate([conv_part.reshape(3, 256), jnp.zeros((3, D - 256), F32)], axis=1)
        return jnp.concatenate([val.reshape(1, D) for val in vals] + [conv_rows], axis=0)

    sw = small_pack([norm_mix, pool_scale, norm_mem, norm_ffn, norm_final], conv_w)
    sm = small_pack([m_norm_mix, m_pool_scale, m_norm_mem, m_norm_ffn, m_norm_final], m_conv_w)
    sv = small_pack([v_norm_mix, v_pool_scale, v_norm_mem, v_norm_ffn, v_norm_final], v_conv_w)
    sg = small_pack([total[r] for r in range(5)], g_conv)
    sd, snm, snv = adamw("adamw_small", sw, sg, sm, sv)
    small_names = ("norm_mix", "pool_scale", "norm_mem", "norm_ffn", "norm_final")
    small_shapes = dict(norm_mix=(1, D), pool_scale=(1, D), norm_mem=(1, D), norm_ffn=(1, D), norm_final=(D,))
    for r, n in enumerate(small_names):
        out_g[n], out_d[n], out_m[n], out_v[n] = (t[r].reshape(small_shapes[n]) for t in (sg, sd, snm, snv))
    out_g["conv_w"], out_d["conv_w"], out_m["conv_w"], out_v["conv_w"] = (
        t[5:8, 0:256].reshape(1, 3, 256) for t in (sg, sd, snm, snv))

    order = ("norm_mix", "w_in", "conv_w", "w_conv_out", "w_pool", "pool_scale", "norm_mem", "w_kv", "w_xattn_out",
             "w_out", "norm_ffn", "w_gate", "w_up", "w_down", "norm_final")
    return (loss, grad_x.reshape(1, t_len, D), *[out_g[n] for n in order], *[out_d[n] for n in order],
            *[out_m[n] for n in order], *[out_v[n] for n in order])
```

```python
import functools

import jax
import jax.numpy as jnp
from jax import lax
from jax.experimental import pallas as pl
from jax.experimental.pallas import tpu as pltpu

F32 = jnp.float32
BF16 = jnp.bfloat16
MESH = pl.DeviceIdType.MESH

D = 1024
N_MEM = 256
HEADS = 4
HEAD_DIM = 256
GROUPS = 4
GROUP_DIM = 256
POOL_WINDOWS = (2, 4, 8, 16)
D_FF = 2816
D_IN = 8192
N_CHIPS = 4
EPS = 1e-6
HALO = 16
POOL_PAD = 128
ATT_SCALE = HEAD_DIM ** -0.5

ADAM_LR = 0.001
ADAM_B1 = 0.9
ADAM_B2 = 0.999
ADAM_EPS = 1e-08
ADAM_WD = 0.01
ADAM_STEP = 10

VMEM_LIMIT = 56 * 1024 * 1024

O_BA, O_CA, O_UA, O_UP, O_QX, O_GA, O_GP, O_GX = (k * D for k in range(8))

NT_DIMS = (((1,), (1,)), ((), ()))
TN_DIMS = (((0,), (0,)), ((), ()))


def _dot(a, b):
    return jnp.dot(a, b, preferred_element_type=F32)


def _dot_nt(a, b):
    return lax.dot_general(a, b, NT_DIMS, preferred_element_type=F32)


def _dot_tn(a, b):
    return lax.dot_general(a, b, TN_DIMS, preferred_element_type=F32)


def _sigmoid(z):
    return pl.reciprocal(1.0 + jnp.exp(-z), approx=True)


def _params(semantics=None):
    return pltpu.CompilerParams(dimension_semantics=semantics, vmem_limit_bytes=VMEM_LIMIT)


def _resident(shape):
    zeros = (0,) * len(shape)
    return pl.BlockSpec(shape, lambda *_: zeros, pipeline_mode=pl.Buffered(1))


def _const(shape):
    zeros = (0,) * len(shape)
    return pl.BlockSpec(shape, lambda *_: zeros)


def _rows(tm, width):
    return pl.BlockSpec((tm, width), lambda i: (i, 0))


def _inv_count(tile, tm, window):
    t = tile * tm + lax.broadcasted_iota(jnp.int32, (tm, 1), 0)
    return 1.0 / jnp.minimum(t + 1, window).astype(F32)


def _carried_call(body, name, grid, in_specs, out_specs, out_shape, scratch_shapes, semantics, args, carry):
    if carry is None:
        res = pl.pallas_call(body, name=name, grid=grid, in_specs=in_specs, out_specs=out_specs, out_shape=out_shape,
                             scratch_shapes=scratch_shapes, compiler_params=_params(semantics))(*args)
        return res, []
    carries = [carry] if isinstance(carry, tuple) else list(carry)
    comm_args = [arr for _, arrs in carries for arr in arrs]
    n, n_in, n_out, n_scratch = len(comm_args), len(in_specs), len(out_specs), len(scratch_shapes)
    comm_shapes = [s for cls, arrs in carries for s in cls.out_shapes(arrs)]
    comm_sems = [s for cls, arrs in carries for s in cls.sems(len(arrs))]

    def carrying(*refs):
        ins, comm_ins = refs[:n_in], refs[n_in:n_in + n]
        outs, comm_outs = refs[n_in + n:n_in + n + n_out], refs[n_in + n + n_out:n_in + 2 * n + n_out]
        scratch, sems = refs[n_in + 2 * n + n_out:n_in + 2 * n + n_out + n_scratch], refs[n_in + 2 * n + n_out + n_scratch:]
        steps = [pl.program_id(d) for d in range(len(grid))]
        first = functools.reduce(jnp.logical_and, [s == 0 for s in steps])
        last = functools.reduce(jnp.logical_and, [s == g - 1 for s, g in zip(steps, grid)])

        def exchanges():
            at = 0
            for k, (cls, arrs) in enumerate(carries):
                yield cls(comm_ins[at:at + len(arrs)], comm_outs[at:at + len(arrs)], sems[2 * k], sems[2 * k + 1])
                at += len(arrs)

        @pl.when(first)
        def _():
            for exchange in exchanges():
                exchange.start()

        if any(hasattr(cls, "middle") for cls, _ in carries):
            linear, total = 0, 1
            for s, g in zip(steps, grid):
                linear, total = linear * g + s, total * g

            @pl.when(linear == (5 * total) // 8)
            def _():
                for exchange in exchanges():
                    if hasattr(exchange, "middle"):
                        exchange.middle()

        body(*ins, *outs, *scratch)

        @pl.when(last)
        def _():
            for exchange in exchanges():
                exchange.finish()

    res = pl.pallas_call(
        carrying, name=name, grid=grid, in_specs=list(in_specs) + _any_specs(n), out_specs=list(out_specs) + _any_specs(n),
        out_shape=list(out_shape) + comm_shapes, scratch_shapes=list(scratch_shapes) + comm_sems,
        compiler_params=_params(semantics))(*args, *comm_args)
    comm_res, at = [], n_out
    for _, arrs in carries:
        comm_res.append(res[at:at + len(arrs)])
        at += len(arrs)
    return res[:n_out], (comm_res[0] if isinstance(carry, tuple) else comm_res)


def proj_fwd(x, g_mix, w_in4, carry=None):
    t_len = x.shape[0]
    tm = min(1024, t_len)
    tn = D_IN // N_CHIPS

    def body(x_ref, g_ref, w_ref, proj_ref, h_ref):
        @pl.when(pl.program_id(1) == 0)
        def _():
            xv = x_ref[...]
            r = lax.rsqrt(jnp.mean(xv * xv, axis=-1, keepdims=True) + EPS)
            h_ref[...] = (xv * r * g_ref[...]).astype(BF16)

        proj_ref[...] = _dot(h_ref[...], w_ref[...]).astype(BF16)

    return _carried_call(
        body, "proj_fwd", (t_len // tm, N_CHIPS),
        in_specs=[pl.BlockSpec((tm, D), lambda i, j: (i, 0)),
                  pl.BlockSpec((1, D), lambda i, j: (0, 0)),
                  pl.BlockSpec((None, D, tn), lambda i, j: (j, 0, 0))],
        out_specs=[pl.BlockSpec((tm, tn), lambda i, j: (i, j)),
                   pl.BlockSpec((tm, D), lambda i, j: (i, 0))],
        out_shape=[jax.ShapeDtypeStruct((t_len, D_IN), BF16), jax.ShapeDtypeStruct((t_len, D), BF16)],
        scratch_shapes=[], semantics=("arbitrary", "arbitrary"), args=(x, g_mix, w_in4), carry=carry)


def kv_fwd(mem, g_mem, w_kv4):
    half = D // 2

    def body(mem_ref, g_ref, w_ref, memn_ref, k_ref, v_ref):
        mv = mem_ref[...]
        r = lax.rsqrt(jnp.mean(mv * mv, axis=-1, keepdims=True) + EPS)
        mn = (mv * r * g_ref[...]).astype(BF16)
        memn_ref[...] = mn
        k_ref[:, 0:half] = _dot(mn, w_ref[0]).astype(BF16)
        k_ref[:, half:D] = _dot(mn, w_ref[1]).astype(BF16)
        v_ref[:, 0:half] = _dot(mn, w_ref[2]).astype(BF16)
        v_ref[:, half:D] = _dot(mn, w_ref[3]).astype(BF16)

    out = jax.ShapeDtypeStruct((N_MEM, D), BF16)
    return pl.pallas_call(body, name="kv_fwd", out_shape=[out, out, out], compiler_params=_params())(mem, g_mem, w_kv4)


def _softmax_rows(s):
    m = jnp.max(s, axis=-1, keepdims=True)
    e = jnp.exp(s - m)
    return e * pl.reciprocal(jnp.sum(e, axis=-1, keepdims=True), approx=True)


def _window_bands(tm, causal):
    t = lax.broadcasted_iota(jnp.int32, (tm, tm + POOL_PAD), 0)
    s = lax.broadcasted_iota(jnp.int32, (tm, tm + POOL_PAD), 1)
    d = (t + POOL_PAD - s) if causal else (s - t)
    return jnp.stack([((d >= 0) & (d < w)).astype(BF16) for w in POOL_WINDOWS])


def mixer_fwd(proj, x, conv_w8, w_co, w_pool, pool_scale, k, v, w_xo, w_out, carry=None):
    t_len = x.shape[0]
    tm = min(256, t_len)

    def body(proj_ref, x_ref, cw_ref, wco_ref, wpool_ref, ps_ref, k_ref, v_ref, wxo_ref, wout_ref,
             a_ref, pooled_ref, ya_ref, pp_ref, yx_ref, o_ref, p_ref, x1_ref, cu_ext, up_ext):
        i = pl.program_id(0)

        @pl.when(i == 0)
        def _():
            cu_ext[0:HALO, :] = jnp.zeros((HALO, D), F32)
            up_ext[0:HALO, :] = jnp.zeros((HALO, D), F32)

        cu = proj_ref[:, O_CA:O_CA + D].astype(F32) * proj_ref[:, O_UA:O_UA + D].astype(F32)
        cu_ext[HALO:HALO + tm, :] = cu
        conv = (cw_ref[2:3, :] * cu + cw_ref[1:2, :] * cu_ext[HALO - 1:HALO - 1 + tm, :]
                + cw_ref[0:1, :] * cu_ext[HALO - 2:HALO - 2 + tm, :])
        a = (proj_ref[:, O_BA:O_BA + D].astype(F32) * conv).astype(BF16)
        a_ref[...] = a
        ya = _dot(a, wco_ref[...])
        ya_ref[...] = ya.astype(BF16)

        up_ext[HALO:HALO + tm, :] = proj_ref[:, O_UP:O_UP + D].astype(F32)
        for g, window in enumerate(POOL_WINDOWS):
            cols = slice(g * GROUP_DIM, (g + 1) * GROUP_DIM)
            tok = up_ext[HALO:HALO + tm, cols]
            acc = tok
            for j in range(1, window):
                acc = acc + up_ext[HALO - j:HALO - j + tm, cols]
            pooled = (acc * _inv_count(i, tm, window) - tok).astype(BF16)
            pooled_ref[:, cols] = pooled
            pp_ref[:, cols] = _dot(pooled, wpool_ref[g]).astype(BF16)

        for hd in range(HEADS):
            cols = slice(hd * HEAD_DIM, (hd + 1) * HEAD_DIM)
            q = proj_ref[:, O_QX + hd * HEAD_DIM:O_QX + (hd + 1) * HEAD_DIM]
            p = _softmax_rows(_dot_nt(q, k_ref[:, cols]) * ATT_SCALE).astype(BF16)
            p_ref[:, hd * N_MEM:(hd + 1) * N_MEM] = p
            o_ref[:, cols] = _dot(p, v_ref[:, cols]).astype(BF16)
        yx = _dot(o_ref[...], wxo_ref[...])
        yx_ref[...] = yx.astype(BF16)

        merged = (_sigmoid(proj_ref[:, O_GA:O_GA + D].astype(F32)) * ya
                  + _sigmoid(proj_ref[:, O_GP:O_GP + D].astype(F32)) * (pp_ref[...].astype(F32) * ps_ref[...])
                  + _sigmoid(proj_ref[:, O_GX:O_GX + D].astype(F32)) * yx)
        x1_ref[...] = x_ref[...] + _dot(merged.astype(BF16), wout_ref[...])

        cu_ext[0:HALO, :] = cu_ext[tm:tm + HALO, :]
        up_ext[0:HALO, :] = up_ext[tm:tm + HALO, :]

    act = jax.ShapeDtypeStruct((t_len, D), BF16)
    return _carried_call(
        body, "mixer_fwd", (t_len // tm,),
        in_specs=[_rows(tm, D_IN), _rows(tm, D), _resident((8, D)), _resident((D, D)),
                  _resident((GROUPS, GROUP_DIM, GROUP_DIM)), _resident((1, D)),
                  _resident((N_MEM, D)), _resident((N_MEM, D)), _resident((D, D)), _resident((D, D))],
        out_specs=[_rows(tm, D)] * 7 + [_rows(tm, D)],
        out_shape=[act] * 6 + [jax.ShapeDtypeStruct((t_len, HEADS * N_MEM), BF16), jax.ShapeDtypeStruct((t_len, D), F32)],
        scratch_shapes=[pltpu.VMEM((tm + HALO, D), F32), pltpu.VMEM((tm + HALO, D), F32)],
        semantics=("arbitrary",), args=(proj, x, conv_w8, w_co, w_pool, pool_scale, k, v, w_xo, w_out), carry=carry)


def ffn_fwd(x1, target, g_ffn, w_gate, w_up, w_down, g_final):
    t_len = x1.shape[0]
    tm = min(512, t_len)

    def body(x1_ref, tgt_ref, g_ref, wg_ref, wu_ref, wd_ref, gf_ref, gate_ref, up_ref, dx2_ref, stat_ref):
        @pl.when(pl.program_id(0) == 0)
        def _():
            stat_ref[...] = jnp.zeros((8, D), F32)

        x1v = x1_ref[...]
        r2 = lax.rsqrt(jnp.mean(x1v * x1v, axis=-1, keepdims=True) + EPS)
        h2 = (x1v * r2 * g_ref[...]).astype(BF16)
        gate = _dot_nt(h2, wg_ref[...])
        up = _dot_nt(h2, wu_ref[...])
        gate_ref[...] = gate.astype(BF16)
        up_ref[...] = up.astype(BF16)
        act = (gate * _sigmoid(gate) * up).astype(BF16)
        x2 = x1v + _dot(act, wd_ref[...])
        r3 = lax.rsqrt(jnp.mean(x2 * x2, axis=-1, keepdims=True) + EPS)
        xh = x2 * r3
        diff = xh * gf_ref[...] - tgt_ref[...]
        dy = diff * (1.0 / D)
        stat_ref[0:1, :] += jnp.sum(dy * xh, axis=0, keepdims=True)
        stat_ref[1:2, :] += (0.5 / D) * jnp.sum(diff * diff, axis=0, keepdims=True)
        dxh = dy * gf_ref[...]
        dx2_ref[...] = r3 * (dxh - xh * jnp.mean(dxh * xh, axis=-1, keepdims=True))

    return pl.pallas_call(
        body, name="ffn_fwd",
        grid=(t_len // tm,),
        in_specs=[_rows(tm, D), _rows(tm, D), _resident((1, D)), _resident((D_FF, D)), _resident((D_FF, D)),
                  _resident((D_FF, D)), _resident((1, D))],
        out_specs=[_rows(tm, D_FF), _rows(tm, D_FF), _rows(tm, D), _const((8, D))],
        out_shape=[jax.ShapeDtypeStruct((t_len, D_FF), BF16), jax.ShapeDtypeStruct((t_len, D_FF), BF16),
                   jax.ShapeDtypeStruct((t_len, D), F32), jax.ShapeDtypeStruct((8, D), F32)],
        compiler_params=_params(("arbitrary",)),
    )(x1, target, g_ffn, w_gate, w_up, w_down, g_final)


def ffn_bwd(dx2, x1, gate, up, g_ffn, w_gate, w_up, w_down):
    t_len = x1.shape[0]
    tm = min(256, t_len)

    def body(dx2_ref, x1_ref, gate_ref, up_ref, g_ref, wg_ref, wu_ref, wd_ref,
             dx1_ref, dgate_ref, dup_ref, act_ref, h2_ref, stat_ref):
        @pl.when(pl.program_id(0) == 0)
        def _():
            stat_ref[...] = jnp.zeros((8, D), F32)

        dx2v = dx2_ref[...]
        gate = gate_ref[...].astype(F32)
        upv = up_ref[...].astype(F32)
        sg = _sigmoid(gate)
        silu = gate * sg
        act_ref[...] = (silu * upv).astype(BF16)
        dact = _dot_nt(dx2v.astype(BF16), wd_ref[...])
        dup = (dact * silu).astype(BF16)
        dgate = (dact * upv * (sg * (1.0 + gate * (1.0 - sg)))).astype(BF16)
        dup_ref[...] = dup
        dgate_ref[...] = dgate
        dh2 = _dot(dgate, wg_ref[...]) + _dot(dup, wu_ref[...])
        x1v = x1_ref[...]
        r2 = lax.rsqrt(jnp.mean(x1v * x1v, axis=-1, keepdims=True) + EPS)
        xh = x1v * r2
        h2_ref[...] = (xh * g_ref[...]).astype(BF16)
        stat_ref[0:1, :] += jnp.sum(dh2 * xh, axis=0, keepdims=True)
        dxh = dh2 * g_ref[...]
        dx1_ref[...] = dx2v + r2 * (dxh - xh * jnp.mean(dxh * xh, axis=-1, keepdims=True))

    ff = jax.ShapeDtypeStruct((t_len, D_FF), BF16)
    return pl.pallas_call(
        body, name="ffn_bwd",
        grid=(t_len // tm,),
        in_specs=[_rows(tm, D), _rows(tm, D), _rows(tm, D_FF), _rows(tm, D_FF), _resident((1, D)),
                  _resident((D_FF, D)), _resident((D_FF, D)), _resident((D_FF, D))],
        out_specs=[_rows(tm, D), _rows(tm, D_FF), _rows(tm, D_FF), _rows(tm, D_FF), _rows(tm, D), _const((8, D))],
        out_shape=[jax.ShapeDtypeStruct((t_len, D), F32), ff, ff, ff, jax.ShapeDtypeStruct((t_len, D), BF16),
                   jax.ShapeDtypeStruct((8, D), F32)],
        compiler_params=_params(("arbitrary",)),
    )(dx2, x1, gate, up, g_ffn, w_gate, w_up, w_down)


def mixer_bwd(dx1, proj, ya, pp, yx, probs, conv_w8, w_co, w_pool, pool_scale, k, v, w_xo, w_out, carry=None):
    t_len = dx1.shape[0]
    tm = min(256, t_len)
    n_tiles = t_len // tm
    halo_blocks = tm // HALO

    def body(dx1_ref, proj_ref, halo_ref, ya_ref, pp_ref, yx_ref, p_ref,
             cw_ref, wco_ref, wpool_ref, ps_ref, k_ref, v_ref, wxo_ref, wout_ref, band_ref,
             dproj_ref, merged_ref, dya_ref, dpp_ref, dyx_ref, dk_ref, dv_ref, stat_ref,
             cu_ext, dconv_ext, dpn_ext):
        step = pl.program_id(0)
        tile = n_tiles - 1 - step

        @pl.when(step == 0)
        def _():
            dk_ref[...] = jnp.zeros((N_MEM, D), F32)
            dv_ref[...] = jnp.zeros((N_MEM, D), F32)
            stat_ref[...] = jnp.zeros((8, D), F32)
            dconv_ext[tm:tm + HALO, :] = jnp.zeros((HALO, D), F32)
            dpn_ext[tm:tm + POOL_PAD, :] = jnp.zeros((POOL_PAD, D), BF16)

        dmerged = _dot_nt(dx1_ref[...].astype(BF16), wout_ref[...])
        sa = _sigmoid(proj_ref[:, O_GA:O_GA + D].astype(F32))
        sp = _sigmoid(proj_ref[:, O_GP:O_GP + D].astype(F32))
        sx = _sigmoid(proj_ref[:, O_GX:O_GX + D].astype(F32))
        ya = ya_ref[...].astype(F32)
        ppv = pp_ref[...].astype(F32)
        yp = ppv * ps_ref[...]
        yx = yx_ref[...].astype(F32)
        merged_ref[...] = (sa * ya + sp * yp + sx * yx).astype(BF16)
        dproj_ref[:, O_GA:O_GA + D] = (dmerged * ya * (sa * (1.0 - sa))).astype(BF16)
        dproj_ref[:, O_GP:O_GP + D] = (dmerged * yp * (sp * (1.0 - sp))).astype(BF16)
        dproj_ref[:, O_GX:O_GX + D] = (dmerged * yx * (sx * (1.0 - sx))).astype(BF16)
        dya = (dmerged * sa).astype(BF16)
        dyp = dmerged * sp
        dyx = (dmerged * sx).astype(BF16)
        dya_ref[...] = dya
        dyx_ref[...] = dyx
        stat_ref[1:2, :] += jnp.sum(dyp * ppv, axis=0, keepdims=True)
        dpp = (dyp * ps_ref[...]).astype(BF16)
        dpp_ref[...] = dpp

        da = _dot_nt(dya, wco_ref[...])
        c_a = proj_ref[:, O_CA:O_CA + D].astype(F32)
        u_a = proj_ref[:, O_UA:O_UA + D].astype(F32)
        cu = c_a * u_a
        halo_cu = halo_ref[:, O_CA:O_CA + D].astype(F32) * halo_ref[:, O_UA:O_UA + D].astype(F32)
        cu_ext[0:HALO, :] = jnp.where(tile > 0, halo_cu, 0.0)
        cu_ext[HALO:HALO + tm, :] = cu
        cu1 = cu_ext[HALO - 1:HALO - 1 + tm, :]
        cu2 = cu_ext[HALO - 2:HALO - 2 + tm, :]
        conv = cw_ref[2:3, :] * cu + cw_ref[1:2, :] * cu1 + cw_ref[0:1, :] * cu2
        dproj_ref[:, O_BA:O_BA + D] = (da * conv).astype(BF16)
        dconv = da * proj_ref[:, O_BA:O_BA + D].astype(F32)
        stat_ref[5:6, :] += jnp.sum(dconv * cu2, axis=0, keepdims=True)
        stat_ref[6:7, :] += jnp.sum(dconv * cu1, axis=0, keepdims=True)
        stat_ref[7:8, :] += jnp.sum(dconv * cu, axis=0, keepdims=True)
        dconv_ext[0:tm, :] = dconv
        dcu = (cw_ref[2:3, :] * dconv + cw_ref[1:2, :] * dconv_ext[1:1 + tm, :]
               + cw_ref[0:1, :] * dconv_ext[2:2 + tm, :])
        dproj_ref[:, O_CA:O_CA + D] = (dcu * u_a).astype(BF16)
        dproj_ref[:, O_UA:O_UA + D] = (dcu * c_a).astype(BF16)

        for g, window in enumerate(POOL_WINDOWS):
            cols = slice(g * GROUP_DIM, (g + 1) * GROUP_DIM)
            dpooled = _dot_nt(dpp[:, cols], wpool_ref[g])
            dpn_ext[0:tm, cols] = (dpooled * _inv_count(tile, tm, window)).astype(BF16)
            acc = _dot(band_ref[g], dpn_ext[:, cols])
            dproj_ref[:, O_UP + g * GROUP_DIM:O_UP + (g + 1) * GROUP_DIM] = (acc - dpooled).astype(BF16)

        do = _dot_nt(dyx, wxo_ref[...])
        for hd in range(HEADS):
            cols = slice(hd * HEAD_DIM, (hd + 1) * HEAD_DIM)
            q = proj_ref[:, O_QX + hd * HEAD_DIM:O_QX + (hd + 1) * HEAD_DIM]
            kh = k_ref[:, cols]
            p16 = p_ref[:, hd * N_MEM:(hd + 1) * N_MEM]
            p = p16.astype(F32)
            doh = do[:, cols].astype(BF16)
            dp = _dot_nt(doh, v_ref[:, cols])
            dv_ref[:, cols] += _dot_tn(p16, doh)
            ds = (p * (dp - jnp.sum(dp * p, axis=-1, keepdims=True)) * ATT_SCALE).astype(BF16)
            dproj_ref[:, O_QX + hd * HEAD_DIM:O_QX + (hd + 1) * HEAD_DIM] = _dot(ds, kh).astype(BF16)
            dk_ref[:, cols] += _dot_tn(ds, q)

        dconv_ext[tm:tm + HALO, :] = dconv_ext[0:HALO, :]
        dpn_ext[tm:tm + HALO, :] = dpn_ext[0:HALO, :]

    def rev(width):
        return pl.BlockSpec((tm, width), lambda s: (n_tiles - 1 - s, 0))

    halo_spec = pl.BlockSpec((HALO, D_IN), lambda s: (jnp.maximum((n_tiles - 1 - s) * halo_blocks - 1, 0), 0))
    act = jax.ShapeDtypeStruct((t_len, D), BF16)
    kv_grad = jax.ShapeDtypeStruct((N_MEM, D), F32)
    return _carried_call(
        body, "mixer_bwd", (n_tiles,),
        in_specs=[rev(D), rev(D_IN), halo_spec, rev(D), rev(D), rev(D), rev(D),
                  _resident((8, D)), _resident((D, D)), _resident((GROUPS, GROUP_DIM, GROUP_DIM)), _resident((1, D)),
                  _resident((N_MEM, D)), _resident((N_MEM, D)), _resident((D, D)), _resident((D, D)),
                  _resident((GROUPS, tm, tm + POOL_PAD))],
        out_specs=[rev(D_IN), rev(D), rev(D), rev(D), rev(D),
                   _const((N_MEM, D)), _const((N_MEM, D)), _const((8, D))],
        out_shape=[jax.ShapeDtypeStruct((t_len, D_IN), BF16), act, act, act, act, kv_grad, kv_grad,
                   jax.ShapeDtypeStruct((8, D), F32)],
        scratch_shapes=[pltpu.VMEM((tm + HALO, D), F32)] * 2 + [pltpu.VMEM((tm + POOL_PAD, D), BF16)],
        semantics=("arbitrary",),
        args=(dx1, proj, proj, ya, pp, yx, probs, conv_w8, w_co, w_pool, pool_scale, k, v, w_xo, w_out,
              _window_bands(tm, False)), carry=carry)


def in_bwd(dproj, w_in4, x, dx1, g_mix, carry=None):
    t_len = x.shape[0]
    tm = min(1024, t_len)
    tk = D_IN // N_CHIPS

    def body(dproj_ref, w_ref, x_ref, dx1_ref, g_ref, gx_ref, stat_ref, acc_ref):
        i, j = pl.program_id(0), pl.program_id(1)

        @pl.when((i == 0) & (j == 0))
        def _():
            stat_ref[...] = jnp.zeros((8, D), F32)

        @pl.when(j == 0)
        def _():
            acc_ref[...] = jnp.zeros((tm, D), F32)

        acc_ref[...] += _dot_nt(dproj_ref[...], w_ref[...])

        @pl.when(j == N_CHIPS - 1)
        def _():
            dh = acc_ref[...]
            xv = x_ref[...]
            r = lax.rsqrt(jnp.mean(xv * xv, axis=-1, keepdims=True) + EPS)
            xh = xv * r
            stat_ref[0:1, :] += jnp.sum(dh * xh, axis=0, keepdims=True)
            dxh = dh * g_ref[...]
            gx_ref[...] = dx1_ref[...] + r * (dxh - xh * jnp.mean(dxh * xh, axis=-1, keepdims=True))

    return _carried_call(
        body, "in_bwd", (t_len // tm, N_CHIPS),
        in_specs=[pl.BlockSpec((tm, tk), lambda i, j: (i, j)),
                  pl.BlockSpec((None, D, tk), lambda i, j: (j, 0, 0)),
                  pl.BlockSpec((tm, D), lambda i, j: (i, 0)),
                  pl.BlockSpec((tm, D), lambda i, j: (i, 0)),
                  pl.BlockSpec((1, D), lambda i, j: (0, 0))],
        out_specs=[pl.BlockSpec((tm, D), lambda i, j: (i, 0)), pl.BlockSpec((8, D), lambda i, j: (0, 0))],
        out_shape=[jax.ShapeDtypeStruct((t_len, D), F32), jax.ShapeDtypeStruct((8, D), F32)],
        scratch_shapes=[pltpu.VMEM((tm, D), F32)], semantics=("arbitrary", "arbitrary"),
        args=(dproj, w_in4, x, dx1, g_mix), carry=carry)


def kv_bwd(dk, dv, memn, mem, g_mem, w_kv4):
    half = D // 2

    def body(dk_ref, dv_ref, memn_ref, mem_ref, g_ref, w_ref, gw_ref, gw16_ref, stat_ref):
        mn = memn_ref[...]
        parts = (dk_ref[:, 0:half], dk_ref[:, half:D], dv_ref[:, 0:half], dv_ref[:, half:D])
        dmemn = jnp.zeros((N_MEM, D), F32)
        for j, part in enumerate(parts):
            part = part.astype(BF16)
            gw = _dot_tn(mn, part)
            gw_ref[j] = gw
            gw16_ref[j] = gw.astype(BF16)
            dmemn = dmemn + _dot_nt(part, w_ref[j])
        mv = mem_ref[...]
        r = lax.rsqrt(jnp.mean(mv * mv, axis=-1, keepdims=True) + EPS)
        stat_ref[...] = jnp.zeros((8, D), F32)
        stat_ref[0:1, :] = jnp.sum(dmemn * (mv * r), axis=0, keepdims=True)

    return pl.pallas_call(
        body, name="kv_bwd",
        out_shape=[jax.ShapeDtypeStruct((N_CHIPS, D, half), F32), jax.ShapeDtypeStruct((N_CHIPS, D, half), BF16),
                   jax.ShapeDtypeStruct((8, D), F32)],
        compiler_params=_params(),
    )(dk, dv, memn, mem, g_mem, w_kv4)


def matmul_tn(name, a, b, tn, col_blocks=1, k_dim=None, n_dim=None, a_col=0, b_col=0, carry=None):
    t_len = a.shape[0]
    k_dim = a.shape[1] if k_dim is None else k_dim
    n_dim = b.shape[1] if n_dim is None else n_dim
    b_off = b_col * (n_dim // tn)
    tt = min(1024, t_len)
    per_block = n_dim // col_blocks // tn

    def body(a_ref, b_ref, out_ref, out16_ref):
        @pl.when(pl.program_id(1) == 0)
        def _():
            out_ref[...] = jnp.zeros((k_dim, tn), F32)

        out_ref[...] += _dot_tn(a_ref[...].astype(BF16), b_ref[...].astype(BF16))

        @pl.when(pl.program_id(1) == t_len // tt - 1)
        def _():
            out16_ref[...] = out_ref[...].astype(BF16)

    if col_blocks == 1:
        out_spec = pl.BlockSpec((k_dim, tn), lambda n, t: (0, n))
        shape = (k_dim, n_dim)
    else:
        out_spec = pl.BlockSpec((None, k_dim, tn), lambda n, t: (n // per_block, 0, n % per_block))
        shape = (col_blocks, k_dim, n_dim // col_blocks)
    outs, carried = _carried_call(
        body, name, (n_dim // tn, t_len // tt),
        in_specs=[pl.BlockSpec((tt, k_dim), lambda n, t: (t, a_col)), pl.BlockSpec((tt, tn), lambda n, t: (t, b_off + n))],
        out_specs=[out_spec, out_spec], out_shape=[jax.ShapeDtypeStruct(shape, F32), jax.ShapeDtypeStruct(shape, BF16)],
        scratch_shapes=[], semantics=("arbitrary", "arbitrary"), args=(a, b), carry=carry)
    return (tuple(outs), carried) if carry is not None else tuple(outs)


def _place():
    x, y, c = lax.axis_index("x"), lax.axis_index("y"), lax.axis_index("c")
    return x, y, c


def _other_chips(x, y):
    return [(1 - x, y), (x, 1 - y), (1 - x, 1 - y)]


def _any_specs(n):
    return [pl.BlockSpec(memory_space=pl.ANY)] * n


def gather_weights(shards):
    n = len(shards)

    def body(*refs):
        gather = _Gather(refs[:n], refs[n:2 * n], *refs[2 * n:])
        gather.start()
        gather.middle()
        gather.finish()

    return pl.pallas_call(
        body, name="gather_weights",
        in_specs=_any_specs(n), out_specs=_any_specs(n),
        out_shape=_Gather.out_shapes(shards), scratch_shapes=_Gather.sems(n),
    )(*shards)


class _Gather:
    SLOTS = 8
    ROW_ALIGN = 16

    @staticmethod
    def out_shapes(shards):
        return [jax.ShapeDtypeStruct((N_CHIPS,) + s.shape, s.dtype) for s in shards]

    @staticmethod
    def sems(n):
        return [pltpu.SemaphoreType.DMA((n, _Gather.SLOTS)), pltpu.SemaphoreType.DMA((n, _Gather.SLOTS))]

    def __init__(self, ins, outs, send_sems, recv_sems):
        self.ins, self.outs, self.send_sems, self.recv_sems = ins, outs, send_sems, recv_sems
        x, y, c = _place()
        self.c, self.me, self.sibling = c, 2 * x + y, (x, y, 1 - c)
        self.across = [(1 - x, y), (x, 1 - y), (1 - x, 1 - y)]

    def _rows(self, a, which, part=None):
        half = self.ins[a].shape[1] // 2
        first = (half // 2) // self.ROW_ALIGN * self.ROW_ALIGN
        if part is None:
            return pl.ds(which * half, half)
        return pl.ds(which * half, first) if part == 0 else pl.ds(which * half + first, half - first)

    def _has_part(self, a, part):
        half = self.ins[a].shape[1] // 2
        return part == 1 or (half // 2) // self.ROW_ALIGN > 0

    def _block(self, a, chip, rows):
        px, py = chip
        return self.outs[a].at[2 * px + py, :, rows, :]

    def _remote(self, src, dst, a, slot, to):
        return pltpu.make_async_remote_copy(src_ref=src, dst_ref=dst, send_sem=self.send_sems.at[a, slot],
                                            recv_sem=self.recv_sems.at[a, slot], device_id=to, device_id_type=MESH)

    def _own(self, a):
        return self._remote(self.ins[a], self.outs[a].at[self.me], a, 6, self.sibling)

    def _sent(self, a, axis):
        rows = self._rows(a, self.c)
        return self._remote(self.ins[a].at[:, rows, :], self.outs[a].at[self.me, :, rows, :], a, axis,
                            (*self.across[axis], self.c))

    def _landed(self, a, axis):
        block = self._block(a, self.across[axis], self._rows(a, self.c))
        return self._remote(block, block, a, axis, (*self.across[axis], self.c))

    def _relayed(self, a, part, incoming):
        source = self.across[2] if incoming else self.across[part]
        block = self._block(a, source, self._rows(a, self.c, part))
        return self._remote(block, block, a, (2, 7)[part], (*self.across[1 - part], self.c))

    def _passed_on(self, a, source, which):
        block = self._block(a, self.across[source], self._rows(a, which))
        return self._remote(block, block, a, 3 + source, self.sibling)

    def start(self):
        for a in range(len(self.ins)):
            self._own(a).start()
        for a in range(len(self.ins)):
            for axis in range(2):
                self._sent(a, axis).start()

    def middle(self):
        for a in range(len(self.ins)):
            for axis in range(2):
                self._landed(a, axis).wait_recv()
                if self._has_part(a, axis):
                    self._relayed(a, axis, incoming=False).start()
                self._passed_on(a, axis, self.c).start()

    def finish(self):
        n = len(self.ins)
        for a in range(n):
            for part in range(2):
                if self._has_part(a, part):
                    self._relayed(a, part, incoming=True).wait_recv()
            self._passed_on(a, 2, self.c).start()
        for a in range(n):
            for source in range(3):
                self._passed_on(a, source, 1 - self.c).wait_recv()
            self._own(a).wait_recv()
        for a in range(n):
            self._own(a).wait_send()
            for axis in range(2):
                self._sent(a, axis).wait_send()
                if self._has_part(a, axis):
                    self._relayed(a, axis, incoming=False).wait_send()
            for source in range(3):
                self._passed_on(a, source, self.c).wait_send()


def exchange_sibling_halves(name, grads):
    n = len(grads)

    def body(*refs):
        ins, outs = refs[:n], refs[n:2 * n]
        send_sems, recv_sems = refs[2 * n:]
        x, y, c = _place()
        copies = []
        for a in range(n):
            h = ins[a].shape[1] // 2
            cp = pltpu.make_async_remote_copy(
                src_ref=ins[a].at[:, pl.ds((1 - c) * h, h), :], dst_ref=outs[a],
                send_sem=send_sems.at[a], recv_sem=recv_sems.at[a], device_id=(x, y, 1 - c), device_id_type=MESH)
            cp.start()
            copies.append(cp)
        for cp in copies:
            cp.wait()

    return pl.pallas_call(
        body, name=name,
        in_specs=_any_specs(n), out_specs=_any_specs(n),
        out_shape=[jax.ShapeDtypeStruct((N_CHIPS, g.shape[1] // 2, g.shape[2]), g.dtype) for g in grads],
        scratch_shapes=[pltpu.SemaphoreType.DMA((n,)), pltpu.SemaphoreType.DMA((n,))],
    )(*grads)


def _row_tile(rows, cols, budget=1 << 20):
    best = 16
    for tr in range(16, rows + 1, 16):
        if rows % tr == 0 and tr * cols * 4 <= budget:
            best = tr
    return best


def add_sibling_half(name, grad, got, core):
    _, rows, cols = grad.shape
    h = rows // 2
    tr = _row_tile(h, cols)
    per_half = h // tr

    def body(core_ref, g_ref, o_ref, out_ref, out16_ref):
        total = g_ref[...] + o_ref[...].astype(F32)
        out_ref[...] = total
        out16_ref[...] = total.astype(BF16)

    out_spec = pl.BlockSpec((None, tr, cols), lambda j, r, cr: (j, r, 0))
    return pl.pallas_call(
        body, name=name,
        grid_spec=pltpu.PrefetchScalarGridSpec(
            num_scalar_prefetch=1, grid=(N_CHIPS, per_half),
            in_specs=[pl.BlockSpec((None, tr, cols), lambda j, r, cr: (j, cr[0] * per_half + r, 0)),
                      pl.BlockSpec((None, tr, cols), lambda j, r, cr: (j, r, 0))],
            out_specs=[out_spec, out_spec]),
        out_shape=[jax.ShapeDtypeStruct((N_CHIPS, h, cols), F32), jax.ShapeDtypeStruct((N_CHIPS, h, cols), BF16)],
        compiler_params=_params(("arbitrary", "arbitrary")),
    )(core, grad, got)


class _ChipExchange:
    @staticmethod
    def out_shapes(partials):
        return [jax.ShapeDtypeStruct((3,) + p.shape[1:], p.dtype) for p in partials]

    @staticmethod
    def sems(n):
        return [pltpu.SemaphoreType.DMA((n, 3)), pltpu.SemaphoreType.DMA((n, 3))]

    def __init__(self, ins, outs, send_sems, recv_sems):
        self.ins, self.outs, self.send_sems, self.recv_sems = ins, outs, send_sems, recv_sems

    def _copies(self):
        x, y, c = _place()
        for a in range(len(self.ins)):
            for slot, (px, py) in enumerate(_other_chips(x, y)):
                yield pltpu.make_async_remote_copy(
                    src_ref=self.ins[a].at[2 * px + py], dst_ref=self.outs[a].at[slot],
                    send_sem=self.send_sems.at[a, slot], recv_sem=self.recv_sems.at[a, slot],
                    device_id=(px, py, c), device_id_type=MESH)

    def start(self):
        for cp in self._copies():
            cp.start()

    def finish(self):
        for cp in self._copies():
            cp.wait()


def add_chip_blocks(name, partial, got, chip):
    _, h, cols = partial.shape
    tr = _row_tile(h, cols)

    def body(chip_ref, p_ref, g0_ref, g1_ref, g2_ref, out_ref):
        out_ref[...] = ((p_ref[...] + g0_ref[...].astype(F32)) + g1_ref[...].astype(F32)) + g2_ref[...].astype(F32)

    def got_spec(slot):
        return pl.BlockSpec((None, tr, cols), lambda r, ch: (slot, r, 0))

    return pl.pallas_call(
        body, name=name,
        grid_spec=pltpu.PrefetchScalarGridSpec(
            num_scalar_prefetch=1, grid=(h // tr,),
            in_specs=[pl.BlockSpec((None, tr, cols), lambda r, ch: (ch[0], r, 0)), got_spec(0), got_spec(1), got_spec(2)],
            out_specs=pl.BlockSpec((tr, cols), lambda r, ch: (r, 0))),
        out_shape=jax.ShapeDtypeStruct((h, cols), F32),
        compiler_params=_params(("arbitrary",)),
    )(chip, partial, got, got, got)


class _SiblingSwap:
    @staticmethod
    def out_shapes(halves):
        return [jax.ShapeDtypeStruct(v.shape, v.dtype) for v in halves]

    @staticmethod
    def sems(n):
        return [pltpu.SemaphoreType.DMA((n,)), pltpu.SemaphoreType.DMA((n,))]

    def __init__(self, ins, outs, send_sems, recv_sems):
        self.ins, self.outs, self.send_sems, self.recv_sems = ins, outs, send_sems, recv_sems

    def _copies(self):
        x, y, c = _place()
        for a in range(len(self.ins)):
            yield pltpu.make_async_remote_copy(
                src_ref=self.ins[a], dst_ref=self.outs[a], send_sem=self.send_sems.at[a], recv_sem=self.recv_sems.at[a],
                device_id=(x, y, 1 - c), device_id_type=MESH)

    def start(self):
        for cp in self._copies():
            cp.start()

    def finish(self):
        for cp in self._copies():
            cp.wait()


def swap_sibling_halves(halves):
    n = len(halves)

    def body(*refs):
        swap = _SiblingSwap(refs[:n], refs[n:2 * n], *refs[2 * n:])
        swap.start()
        swap.finish()

    return pl.pallas_call(
        body, name="swap_sibling_halves",
        in_specs=_any_specs(n), out_specs=_any_specs(n),
        out_shape=_SiblingSwap.out_shapes(halves), scratch_shapes=_SiblingSwap.sems(n),
    )(*halves)


def all_reduce_small(pack):
    rows = pack.shape[0]

    def body(pack_ref, out_ref, gathered, send_sems, recv_sems):
        x, y, c = _place()
        me = 4 * x + 2 * y + c
        gathered[me] = pack_ref[...]
        copies = []
        for rel in range(1, 8):
            fx, fy, fc = (rel >> 2) & 1, (rel >> 1) & 1, rel & 1
            peer = (x ^ fx, y ^ fy, c ^ fc)
            cp = pltpu.make_async_remote_copy(
                src_ref=pack_ref, dst_ref=gathered.at[me], send_sem=send_sems.at[rel - 1], recv_sem=recv_sems.at[rel - 1],
                device_id=peer, device_id_type=MESH)
            cp.start()
            copies.append(cp)
        for rel in range(1, 8):
            fx, fy, fc = (rel >> 2) & 1, (rel >> 1) & 1, rel & 1
            src = 4 * (x ^ fx) + 2 * (y ^ fy) + (c ^ fc)
            pltpu.make_async_remote_copy(
                src_ref=pack_ref, dst_ref=gathered.at[src], send_sem=send_sems.at[rel - 1], recv_sem=recv_sems.at[rel - 1],
                device_id=(x, y, c), device_id_type=MESH).wait_recv()
        for cp in copies:
            cp.wait_send()
        total = gathered[0]
        for dev in range(1, 8):
            total = total + gathered[dev]
        out_ref[...] = total

    return pl.pallas_call(
        body, name="all_reduce_small",
        in_specs=[pl.BlockSpec(memory_space=pltpu.VMEM)], out_specs=pl.BlockSpec(memory_space=pltpu.VMEM),
        out_shape=jax.ShapeDtypeStruct((rows, D), F32),
        scratch_shapes=[pltpu.VMEM((8, rows, D), F32), pltpu.SemaphoreType.DMA((7,)), pltpu.SemaphoreType.DMA((7,))],
    )(pack)


def _adamw_update(w, g, m, v):
    nm = ADAM_B1 * m + (1.0 - ADAM_B1) * g
    nv = ADAM_B2 * v + (1.0 - ADAM_B2) * (g * g)
    m_hat = nm / (1.0 - ADAM_B1 ** ADAM_STEP)
    v_hat = nv / (1.0 - ADAM_B2 ** ADAM_STEP)
    delta = -ADAM_LR * (m_hat / (jnp.sqrt(v_hat) + ADAM_EPS) + ADAM_WD * w)
    return delta, nm, nv


def adamw(name, w, g, m, v):
    def body(w_ref, g_ref, m_ref, v_ref, d_ref, nm_ref, nv_ref):
        d_ref[...], nm_ref[...], nv_ref[...] = _adamw_update(w_ref[...], g_ref[...], m_ref[...], v_ref[...])

    out = jax.ShapeDtypeStruct(w.shape, F32)
    return pl.pallas_call(body, name=name, out_shape=[out] * 3, compiler_params=_params())(w, g, m, v)


def adamw_halves(name, w, mine, theirs, m, v, core):
    h, cols = mine.shape
    tr = _row_tile(h, cols)
    per_half = h // tr

    def body(core_ref, w_ref, mine_ref, theirs_ref, m_ref, v_ref, g_ref, d_ref, nm_ref, nv_ref):
        g = jnp.where(pl.program_id(0) == core_ref[0], mine_ref[...], theirs_ref[...])
        g_ref[...] = g
        d_ref[...], nm_ref[...], nv_ref[...] = _adamw_update(w_ref[...], g, m_ref[...], v_ref[...])

    full = pl.BlockSpec((tr, cols), lambda hh, r, cr: (hh * per_half + r, 0))
    mine_spec = pl.BlockSpec((tr, cols), lambda hh, r, cr: (jnp.where(hh == cr[0], r, 0), 0))
    theirs_spec = pl.BlockSpec((tr, cols), lambda hh, r, cr: (jnp.where(hh == cr[0], 0, r), 0))
    out = jax.ShapeDtypeStruct((2 * h, cols), F32)
    return pl.pallas_call(
        body, name=name,
        grid_spec=pltpu.PrefetchScalarGridSpec(
            num_scalar_prefetch=1, grid=(2, per_half),
            in_specs=[full, mine_spec, theirs_spec, full, full], out_specs=[full] * 4),
        out_shape=[out] * 4,
        compiler_params=_params(("arbitrary", "arbitrary")),
    )(core, w, mine, theirs, m, v)


BIG = ("w_in", "w_conv_out", "w_pool", "w_kv", "w_xattn_out", "w_out", "w_gate", "w_up", "w_down")


def kernel(x, mem, norm_mix, w_in, conv_w, w_conv_out, w_pool, pool_scale, norm_mem, w_kv, w_xattn_out, w_out, norm_ffn, w_gate, w_up, w_down, norm_final, loss_target, m_norm_mix, m_w_in, m_conv_w, m_w_conv_out, m_w_pool, m_pool_scale, m_norm_mem, m_w_kv, m_w_xattn_out, m_w_out, m_norm_ffn, m_w_gate, m_w_up, m_w_down, m_norm_final, v_norm_mix, v_w_in, v_conv_w, v_w_conv_out, v_w_pool, v_pool_scale, v_norm_mem, v_w_kv, v_w_xattn_out, v_w_out, v_norm_ffn, v_w_gate, v_w_up, v_w_down, v_norm_final):
    t_len = x.shape[1]
    xi, yi, ci = lax.axis_index("x"), lax.axis_index("y"), lax.axis_index("c")
    chip = 2 * xi + yi
    chip_arr = jnp.reshape(chip, (1,)).astype(jnp.int32)
    core_arr = jnp.reshape(ci, (1,)).astype(jnp.int32)

    conv_pad = jnp.concatenate([conv_w, jnp.zeros((1, 13, 256), F32)], axis=1)
    def t2(w):
        return jnp.swapaxes(w, 1, 2)

    (g_in,) = gather_weights([w_in.astype(BF16)])
    w_in4 = g_in.reshape(N_CHIPS, D, D_IN // N_CHIPS)

    x2d = x[0]
    tgt = loss_target[0]
    (proj, h), (g_kv, g_conv_w, g_co, g_xo, g_out, g_pool, g_gate) = proj_fwd(
        x2d, norm_mix, w_in4,
        carry=(_Gather, [w_kv.astype(BF16), conv_pad,
                         w_conv_out.astype(BF16), w_xattn_out.astype(BF16), w_out.astype(BF16),
                         w_pool[0].astype(BF16),
                         t2(w_gate).astype(BF16)]))
    w_kv4 = g_kv.reshape(N_CHIPS, D, D // 2)
    conv_full = jnp.transpose(g_conv_w[:, 0, 0:8, :], (1, 0, 2)).reshape(8, D)
    w_co_f, w_xo_f, w_out_f = g_co.reshape(D, D), g_xo.reshape(D, D), g_out.reshape(D, D)
    w_pool_f = jnp.transpose(g_pool, (1, 0, 2, 3)).reshape(GROUPS, GROUP_DIM, GROUP_DIM)
    memn, k, v = kv_fwd(mem[0], norm_mem, w_kv4)
    (a, pooled, ya, pp, yx, o, probs, x1), (g_up, g_down) = mixer_fwd(
        proj, x2d, conv_full, w_co_f, w_pool_f, pool_scale, k, v, w_xo_f, w_out_f,
        carry=(_Gather, [t2(w_up).astype(BF16), w_down.astype(BF16)]))
    w_gate_f, w_up_f, w_down_f = g_gate.reshape(D_FF, D), g_up.reshape(D_FF, D), g_down.reshape(D_FF, D)
    gate, up, dx2, stat_f = ffn_fwd(x1, tgt, norm_ffn, w_gate_f, w_up_f, w_down_f, norm_final.reshape(1, D))

    def by_chip(pair):
        return tuple(gw.reshape(N_CHIPS, gw.shape[0] // N_CHIPS, gw.shape[1]) for gw in pair)

    def chip_partials(tag, names, grads):
        got = exchange_sibling_halves("exchange_sibling_halves_" + tag, [g16 for _, g16 in grads])
        return [add_sibling_half("add_sibling_" + n, g32, o_, core_arr) for n, (g32, _), o_ in zip(names, grads, got)]

    def chip_sums(names, partials, got):
        return [add_chip_blocks("add_chips_" + n, p32, g2, chip_arr) for n, (p32, _), g2 in zip(names, partials, got)]

    dx1, dgate, dup, act, h2, stat_b1 = ffn_bwd(dx2, x1, gate, up, norm_ffn, w_gate_f, w_up_f, w_down_f)
    gw_gate = matmul_tn("grad_w_gate", dgate, h2, 512)
    gw_up = matmul_tn("grad_w_up", dup, h2, 512)
    gw_down = matmul_tn("grad_w_down", act, dx2, 512)
    names_ffn = ("w_gate", "w_up", "w_down")
    part_ffn = chip_partials("ffn", names_ffn, [by_chip(gw_gate), by_chip(gw_up), by_chip(gw_down)])

    (dproj, merged, dya, dpp, dyx, dk, dv, stat_b2), got_ffn = mixer_bwd(
        dx1, proj, ya, pp, yx, probs, conv_full, w_co_f, w_pool_f, pool_scale, k, v, w_xo_f, w_out_f,
        carry=(_ChipExchange, [p16 for _, p16 in part_ffn]))
    gw_kv32, gw_kv16, stat_kv = kv_bwd(dk, dv, memn, mem[0], norm_mem, w_kv4)
    gw_co = matmul_tn("grad_w_conv_out", a, dya, 1024)
    gw_xo = matmul_tn("grad_w_xattn_out", o, dyx, 1024)
    gw_out = matmul_tn("grad_w_out", merged, dx1, 1024)
    gw_pool_g = [matmul_tn("grad_w_pool_%d" % g, pooled, dpp, GROUP_DIM, k_dim=GROUP_DIM, n_dim=GROUP_DIM,
                           a_col=g, b_col=g) for g in range(GROUPS)]
    gw_pool = tuple(jnp.transpose(jnp.stack([pair[k] for pair in gw_pool_g]).reshape(GROUPS, N_CHIPS, 64, GROUP_DIM),
                                  (1, 0, 2, 3)).reshape(N_CHIPS, 256, GROUP_DIM) for k in range(2))
    names_mix = ("w_conv_out", "w_pool", "w_kv", "w_xattn_out", "w_out")
    part_mix = chip_partials("mixer", names_mix,
                             [by_chip(gw_co), gw_pool, (gw_kv32, gw_kv16), by_chip(gw_xo), by_chip(gw_out)])

    gw_in, got_mix = matmul_tn("grad_w_in", h, dproj, 2048, col_blocks=N_CHIPS,
                               carry=(_ChipExchange, [p16 for _, p16 in part_mix]))
    part_in = chip_partials("in", ("w_in",), [gw_in])
    mine_early = chip_sums(names_ffn + names_mix, part_ffn + part_mix, list(got_ffn) + list(got_mix))
    (grad_x, stat_b3), (got_in, theirs_early) = in_bwd(
        dproj, w_in4, x2d, dx1, norm_mix,
        carry=[(_ChipExchange, [p16 for _, p16 in part_in]), (_SiblingSwap, mine_early)])
    mine_in = chip_sums(("w_in",), part_in, got_in)
    theirs_in = swap_sibling_halves(mine_in)
    reduced = dict(zip(names_ffn + names_mix + ("w_in",),
                       zip(mine_early + mine_in, list(theirs_early) + list(theirs_in))))
    mine = [reduced[n][0] for n in BIG]
    theirs = [reduced[n][1] for n in BIG]

    pack = jnp.concatenate([stat_b3[0:1], stat_b2[1:2], stat_kv[0:1], stat_b1[0:1], stat_f[0:1], stat_b2[5:8],
                            stat_f[1:2], jnp.zeros((7, D), F32)], axis=0)
    total = all_reduce_small(pack)
    loss = jnp.sum(total[8])
    g_conv_full = total[5:8]
    g_conv = lax.dynamic_slice_in_dim(g_conv_full, chip * 256, 256, axis=1)

    given = dict(w_in=(w_in, m_w_in, v_w_in), w_conv_out=(w_conv_out, m_w_conv_out, v_w_conv_out),
                 w_pool=(w_pool, m_w_pool, v_w_pool), w_kv=(w_kv, m_w_kv, v_w_kv),
                 w_xattn_out=(w_xattn_out, m_w_xattn_out, v_w_xattn_out), w_out=(w_out, m_w_out, v_w_out),
                 w_gate=(w_gate, m_w_gate, v_w_gate), w_up=(w_up, m_w_up, v_w_up), w_down=(w_down, m_w_down, v_w_down))
    out_g, out_d, out_m, out_v = {}, {}, {}, {}
    for n, mine_n, theirs_n in zip(BIG, mine, theirs):
        transposed = n in ("w_gate", "w_up")
        rows2d = (2 * mine_n.shape[0], mine_n.shape[1])
        w_, m_, v_ = ((t2(t) if transposed else t).reshape(rows2d) for t in given[n])
        res = adamw_halves("adamw_" + n, w_, mine_n, theirs_n, m_, v_, core_arr)
        if transposed:
            res = [t2(t.reshape(1, D_FF // N_CHIPS, D)) for t in res]
        out_g[n], out_d[n], out_m[n], out_v[n] = (t.reshape(given[n][0].shape) for t in res)

    def small_pack(vals, conv_part):
        conv_rows = jnp.concatenate([conv_part.reshape(3, 256), jnp.zeros((3, D - 256), F32)], axis=1)
        return jnp.concatenate([val.reshape(1, D) for val in vals] + [conv_rows], axis=0)

    sw = small_pack([norm_mix, pool_scale, norm_mem, norm_ffn, norm_final], conv_w)
    sm = small_pack([m_norm_mix, m_pool_scale, m_norm_mem, m_norm_ffn, m_norm_final], m_conv_w)
    sv = small_pack([v_norm_mix, v_pool_scale, v_norm_mem, v_norm_ffn, v_norm_final], v_conv_w)
    sg = small_pack([total[r] for r in range(5)], g_conv)
    sd, snm, snv = adamw("adamw_small", sw, sg, sm, sv)
    small_names = ("norm_mix", "pool_scale", "norm_mem", "norm_ffn", "norm_final")
    small_shapes = dict(norm_mix=(1, D), pool_scale=(1, D), norm_mem=(1, D), norm_ffn=(1, D), norm_final=(D,))
    for r, n in enumerate(small_names):
        out_g[n], out_d[n], out_m[n], out_v[n] = (t[r].reshape(small_shapes[n]) for t in (sg, sd, snm, snv))
    out_g["conv_w"], out_d["conv_w"], out_m["conv_w"], out_v["conv_w"] = (
        t[5:8, 0:256].reshape(1, 3, 256) for t in (sg, sd, snm, snv))

    order = ("norm_mix", "w_in", "conv_w", "w_conv_out", "w_pool", "pool_scale", "norm_mem", "w_kv", "w_xattn_out",
             "w_out", "norm_ffn", "w_gate", "w_up", "w_down", "norm_final")
    return (loss, grad_x.reshape(1, t_len, D), *[out_g[n] for n in order], *[out_d[n] for n in order],
            *[out_m[n] for n in order], *[out_v[n] for n in order])
```

```python
import functools

import jax
import jax.numpy as jnp
from jax import lax
from jax.experimental import pallas as pl
from jax.experimental.pallas import tpu as pltpu

F32 = jnp.float32
BF16 = jnp.bfloat16
MESH = pl.DeviceIdType.MESH

D = 1024
N_MEM = 256
HEADS = 4
HEAD_DIM = 256
GROUPS = 4
GROUP_DIM = 256
POOL_WINDOWS = (2, 4, 8, 16)
D_FF = 2816
D_IN = 8192
N_CHIPS = 4
EPS = 1e-6
HALO = 16
POOL_PAD = 128
ATT_SCALE = HEAD_DIM ** -0.5

ADAM_LR = 0.001
ADAM_B1 = 0.9
ADAM_B2 = 0.999
ADAM_EPS = 1e-08
ADAM_WD = 0.01
ADAM_STEP = 10

VMEM_LIMIT = 56 * 1024 * 1024

O_BA, O_CA, O_UA, O_UP, O_QX, O_GA, O_GP, O_GX = (k * D for k in range(8))

NT_DIMS = (((1,), (1,)), ((), ()))
TN_DIMS = (((0,), (0,)), ((), ()))


def _dot(a, b):
    return jnp.dot(a, b, preferred_element_type=F32)


def _dot_nt(a, b):
    return lax.dot_general(a, b, NT_DIMS, preferred_element_type=F32)


def _dot_tn(a, b):
    return lax.dot_general(a, b, TN_DIMS, preferred_element_type=F32)


def _sigmoid(z):
    return pl.reciprocal(1.0 + jnp.exp(-z), approx=True)


def _params(semantics=None):
    return pltpu.CompilerParams(dimension_semantics=semantics, vmem_limit_bytes=VMEM_LIMIT)


def _resident(shape):
    zeros = (0,) * len(shape)
    return pl.BlockSpec(shape, lambda *_: zeros, pipeline_mode=pl.Buffered(1))


def _const(shape):
    zeros = (0,) * len(shape)
    return pl.BlockSpec(shape, lambda *_: zeros)


def _rows(tm, width):
    return pl.BlockSpec((tm, width), lambda i: (i, 0))


def _inv_count(tile, tm, window):
    t = tile * tm + lax.broadcasted_iota(jnp.int32, (tm, 1), 0)
    return 1.0 / jnp.minimum(t + 1, window).astype(F32)


def _carried_call(body, name, grid, in_specs, out_specs, out_shape, scratch_shapes, semantics, args, carry):
    if carry is None:
        res = pl.pallas_call(body, name=name, grid=grid, in_specs=in_specs, out_specs=out_specs, out_shape=out_shape,
                             scratch_shapes=scratch_shapes, compiler_params=_params(semantics))(*args)
        return res, []
    carries = [carry] if isinstance(carry, tuple) else list(carry)
    comm_args = [arr for _, arrs in carries for arr in arrs]
    n, n_in, n_out, n_scratch = len(comm_args), len(in_specs), len(out_specs), len(scratch_shapes)
    comm_shapes = [s for cls, arrs in carries for s in cls.out_shapes(arrs)]
    comm_sems = [s for cls, arrs in carries for s in cls.sems(len(arrs))]

    def carrying(*refs):
        ins, comm_ins = refs[:n_in], refs[n_in:n_in + n]
        outs, comm_outs = refs[n_in + n:n_in + n + n_out], refs[n_in + n + n_out:n_in + 2 * n + n_out]
        scratch, sems = refs[n_in + 2 * n + n_out:n_in + 2 * n + n_out + n_scratch], refs[n_in + 2 * n + n_out + n_scratch:]
        steps = [pl.program_id(d) for d in range(len(grid))]
        first = functools.reduce(jnp.logical_and, [s == 0 for s in steps])
        last = functools.reduce(jnp.logical_and, [s == g - 1 for s, g in zip(steps, grid)])

        def exchanges():
            at = 0
            for k, (cls, arrs) in enumerate(carries):
                yield cls(comm_ins[at:at + len(arrs)], comm_outs[at:at + len(arrs)], sems[2 * k], sems[2 * k + 1])
                at += len(arrs)

        @pl.when(first)
        def _():
            for exchange in exchanges():
                exchange.start()

        if any(hasattr(cls, "middle") for cls, _ in carries):
            linear, total = 0, 1
            for s, g in zip(steps, grid):
                linear, total = linear * g + s, total * g

            @pl.when(linear == (5 * total) // 8)
            def _():
                for exchange in exchanges():
                    if hasattr(exchange, "middle"):
                        exchange.middle()

        body(*ins, *outs, *scratch)

        @pl.when(last)
        def _():
            for exchange in exchanges():
                exchange.finish()

    res = pl.pallas_call(
        carrying, name=name, grid=grid, in_specs=list(in_specs) + _any_specs(n), out_specs=list(out_specs) + _any_specs(n),
        out_shape=list(out_shape) + comm_shapes, scratch_shapes=list(scratch_shapes) + comm_sems,
        compiler_params=_params(semantics))(*args, *comm_args)
    comm_res, at = [], n_out
    for _, arrs in carries:
        comm_res.append(res[at:at + len(arrs)])
        at += len(arrs)
    return res[:n_out], (comm_res[0] if isinstance(carry, tuple) else comm_res)


def proj_fwd(x, g_mix, w_in, carry=None):
    t_len = x.shape[0]
    tm = min(1024, t_len)
    tn = D_IN // N_CHIPS

    def body(x_ref, g_ref, w_ref, proj_ref, h_ref):
        @pl.when(pl.program_id(1) == 0)
        def _():
            xv = x_ref[...]
            r = lax.rsqrt(jnp.mean(xv * xv, axis=-1, keepdims=True) + EPS)
            h_ref[...] = (xv * r * g_ref[...]).astype(BF16)

        proj_ref[...] = _dot(h_ref[...], w_ref[...]).astype(BF16)

    return _carried_call(
        body, "proj_fwd", (t_len // tm, N_CHIPS),
        in_specs=[pl.BlockSpec((tm, D), lambda i, j: (i, 0)),
                  pl.BlockSpec((1, D), lambda i, j: (0, 0)),
                  pl.BlockSpec((D, tn), lambda i, j: (0, j))],
        out_specs=[pl.BlockSpec((tm, tn), lambda i, j: (i, j)),
                   pl.BlockSpec((tm, D), lambda i, j: (i, 0))],
        out_shape=[jax.ShapeDtypeStruct((t_len, D_IN), BF16), jax.ShapeDtypeStruct((t_len, D), BF16)],
        scratch_shapes=[], semantics=("arbitrary", "arbitrary"), args=(x, g_mix, w_in), carry=carry)


def kv_fwd(mem, g_mem, w_kv4):
    half = D // 2

    def body(mem_ref, g_ref, w_ref, memn_ref, k_ref, v_ref):
        mv = mem_ref[...]
        r = lax.rsqrt(jnp.mean(mv * mv, axis=-1, keepdims=True) + EPS)
        mn = (mv * r * g_ref[...]).astype(BF16)
        memn_ref[...] = mn
        k_ref[:, 0:half] = _dot(mn, w_ref[0]).astype(BF16)
        k_ref[:, half:D] = _dot(mn, w_ref[1]).astype(BF16)
        v_ref[:, 0:half] = _dot(mn, w_ref[2]).astype(BF16)
        v_ref[:, half:D] = _dot(mn, w_ref[3]).astype(BF16)

    out = jax.ShapeDtypeStruct((N_MEM, D), BF16)
    return pl.pallas_call(body, name="kv_fwd", out_shape=[out, out, out], compiler_params=_params())(mem, g_mem, w_kv4)


def _softmax_rows(s):
    m = jnp.max(s, axis=-1, keepdims=True)
    e = jnp.exp(s - m)
    return e * pl.reciprocal(jnp.sum(e, axis=-1, keepdims=True), approx=True)


def _window_bands(tm, causal):
    t = lax.broadcasted_iota(jnp.int32, (tm, tm + POOL_PAD), 0)
    s = lax.broadcasted_iota(jnp.int32, (tm, tm + POOL_PAD), 1)
    d = (t + POOL_PAD - s) if causal else (s - t)
    return jnp.stack([((d >= 0) & (d < w)).astype(BF16) for w in POOL_WINDOWS])


def mixer_fwd(proj, x, conv_w8, w_co, w_pool, pool_scale, k, v, w_xo, w_out, carry=None):
    t_len = x.shape[0]
    tm = min(256, t_len)

    def body(proj_ref, x_ref, cw_ref, wco_ref, wpool_ref, ps_ref, k_ref, v_ref, wxo_ref, wout_ref,
             a_ref, pooled_ref, ya_ref, pp_ref, yx_ref, o_ref, p_ref, x1_ref, cu_ext, up_ext):
        i = pl.program_id(0)

        @pl.when(i == 0)
        def _():
            cu_ext[0:HALO, :] = jnp.zeros((HALO, D), F32)
            up_ext[0:HALO, :] = jnp.zeros((HALO, D), F32)

        cu = proj_ref[:, O_CA:O_CA + D].astype(F32) * proj_ref[:, O_UA:O_UA + D].astype(F32)
        cu_ext[HALO:HALO + tm, :] = cu
        conv = (cw_ref[2:3, :] * cu + cw_ref[1:2, :] * cu_ext[HALO - 1:HALO - 1 + tm, :]
                + cw_ref[0:1, :] * cu_ext[HALO - 2:HALO - 2 + tm, :])
        a = (proj_ref[:, O_BA:O_BA + D].astype(F32) * conv).astype(BF16)
        a_ref[...] = a
        ya = _dot(a, wco_ref[...])
        ya_ref[...] = ya.astype(BF16)

        up_ext[HALO:HALO + tm, :] = proj_ref[:, O_UP:O_UP + D].astype(F32)
        for g, window in enumerate(POOL_WINDOWS):
            cols = slice(g * GROUP_DIM, (g + 1) * GROUP_DIM)
            tok = up_ext[HALO:HALO + tm, cols]
            acc = tok
            for j in range(1, window):
                acc = acc + up_ext[HALO - j:HALO - j + tm, cols]
            pooled = (acc * _inv_count(i, tm, window) - tok).astype(BF16)
            pooled_ref[:, cols] = pooled
            pp_ref[:, cols] = _dot(pooled, wpool_ref[g]).astype(BF16)

        for hd in range(HEADS):
            cols = slice(hd * HEAD_DIM, (hd + 1) * HEAD_DIM)
            q = proj_ref[:, O_QX + hd * HEAD_DIM:O_QX + (hd + 1) * HEAD_DIM]
            p = _softmax_rows(_dot_nt(q, k_ref[:, cols]) * ATT_SCALE).astype(BF16)
            p_ref[:, hd * N_MEM:(hd + 1) * N_MEM] = p
            o_ref[:, cols] = _dot(p, v_ref[:, cols]).astype(BF16)
        yx = _dot(o_ref[...], wxo_ref[...])
        yx_ref[...] = yx.astype(BF16)

        merged = (_sigmoid(proj_ref[:, O_GA:O_GA + D].astype(F32)) * ya
                  + _sigmoid(proj_ref[:, O_GP:O_GP + D].astype(F32)) * (pp_ref[...].astype(F32) * ps_ref[...])
                  + _sigmoid(proj_ref[:, O_GX:O_GX + D].astype(F32)) * yx)
        x1_ref[...] = x_ref[...] + _dot(merged.astype(BF16), wout_ref[...])

        cu_ext[0:HALO, :] = cu_ext[tm:tm + HALO, :]
        up_ext[0:HALO, :] = up_ext[tm:tm + HALO, :]

    act = jax.ShapeDtypeStruct((t_len, D), BF16)
    return _carried_call(
        body, "mixer_fwd", (t_len // tm,),
        in_specs=[_rows(tm, D_IN), _rows(tm, D), _resident((8, D)), _resident((D, D)),
                  _resident((GROUPS, GROUP_DIM, GROUP_DIM)), _resident((1, D)),
                  _resident((N_MEM, D)), _resident((N_MEM, D)), _resident((D, D)), _resident((D, D))],
        out_specs=[_rows(tm, D)] * 7 + [_rows(tm, D)],
        out_shape=[act] * 6 + [jax.ShapeDtypeStruct((t_len, HEADS * N_MEM), BF16), jax.ShapeDtypeStruct((t_len, D), F32)],
        scratch_shapes=[pltpu.VMEM((tm + HALO, D), F32), pltpu.VMEM((tm + HALO, D), F32)],
        semantics=("arbitrary",), args=(proj, x, conv_w8, w_co, w_pool, pool_scale, k, v, w_xo, w_out), carry=carry)


def ffn_fwd(x1, target, g_ffn, w_gate, w_up, w_down, g_final):
    t_len = x1.shape[0]
    tm = min(512, t_len)

    def body(x1_ref, tgt_ref, g_ref, wg_ref, wu_ref, wd_ref, gf_ref, gate_ref, up_ref, dx2_ref, stat_ref):
        @pl.when(pl.program_id(0) == 0)
        def _():
            stat_ref[...] = jnp.zeros((8, D), F32)

        x1v = x1_ref[...]
        r2 = lax.rsqrt(jnp.mean(x1v * x1v, axis=-1, keepdims=True) + EPS)
        h2 = (x1v * r2 * g_ref[...]).astype(BF16)
        gate = _dot_nt(h2, wg_ref[...])
        up = _dot_nt(h2, wu_ref[...])
        gate_ref[...] = gate.astype(BF16)
        up_ref[...] = up.astype(BF16)
        act = (gate * _sigmoid(gate) * up).astype(BF16)
        x2 = x1v + _dot(act, wd_ref[...])
        r3 = lax.rsqrt(jnp.mean(x2 * x2, axis=-1, keepdims=True) + EPS)
        xh = x2 * r3
        diff = xh * gf_ref[...] - tgt_ref[...]
        dy = diff * (1.0 / D)
        stat_ref[0:1, :] += jnp.sum(dy * xh, axis=0, keepdims=True)
        stat_ref[1:2, :] += (0.5 / D) * jnp.sum(diff * diff, axis=0, keepdims=True)
        dxh = dy * gf_ref[...]
        dx2_ref[...] = r3 * (dxh - xh * jnp.mean(dxh * xh, axis=-1, keepdims=True))

    return pl.pallas_call(
        body, name="ffn_fwd",
        grid=(t_len // tm,),
        in_specs=[_rows(tm, D), _rows(tm, D), _resident((1, D)), _resident((D_FF, D)), _resident((D_FF, D)),
                  _resident((D_FF, D)), _resident((1, D))],
        out_specs=[_rows(tm, D_FF), _rows(tm, D_FF), _rows(tm, D), _const((8, D))],
        out_shape=[jax.ShapeDtypeStruct((t_len, D_FF), BF16), jax.ShapeDtypeStruct((t_len, D_FF), BF16),
                   jax.ShapeDtypeStruct((t_len, D), F32), jax.ShapeDtypeStruct((8, D), F32)],
        compiler_params=_params(("arbitrary",)),
    )(x1, target, g_ffn, w_gate, w_up, w_down, g_final)


def ffn_bwd(dx2, x1, gate, up, g_ffn, w_gate, w_up, w_down):
    t_len = x1.shape[0]
    tm = min(256, t_len)

    def body(dx2_ref, x1_ref, gate_ref, up_ref, g_ref, wg_ref, wu_ref, wd_ref,
             dx1_ref, dgate_ref, dup_ref, act_ref, h2_ref, stat_ref):
        @pl.when(pl.program_id(0) == 0)
        def _():
            stat_ref[...] = jnp.zeros((8, D), F32)

        dx2v = dx2_ref[...]
        gate = gate_ref[...].astype(F32)
        upv = up_ref[...].astype(F32)
        sg = _sigmoid(gate)
        silu = gate * sg
        act_ref[...] = (silu * upv).astype(BF16)
        dact = _dot_nt(dx2v.astype(BF16), wd_ref[...])
        dup = (dact * silu).astype(BF16)
        dgate = (dact * upv * (sg * (1.0 + gate * (1.0 - sg)))).astype(BF16)
        dup_ref[...] = dup
        dgate_ref[...] = dgate
        dh2 = _dot(dgate, wg_ref[...]) + _dot(dup, wu_ref[...])
        x1v = x1_ref[...]
        r2 = lax.rsqrt(jnp.mean(x1v * x1v, axis=-1, keepdims=True) + EPS)
        xh = x1v * r2
        h2_ref[...] = (xh * g_ref[...]).astype(BF16)
        stat_ref[0:1, :] += jnp.sum(dh2 * xh, axis=0, keepdims=True)
        dxh = dh2 * g_ref[...]
        dx1_ref[...] = dx2v + r2 * (dxh - xh * jnp.mean(dxh * xh, axis=-1, keepdims=True))

    ff = jax.ShapeDtypeStruct((t_len, D_FF), BF16)
    return pl.pallas_call(
        body, name="ffn_bwd",
        grid=(t_len // tm,),
        in_specs=[_rows(tm, D), _rows(tm, D), _rows(tm, D_FF), _rows(tm, D_FF), _resident((1, D)),
                  _resident((D_FF, D)), _resident((D_FF, D)), _resident((D_FF, D))],
        out_specs=[_rows(tm, D), _rows(tm, D_FF), _rows(tm, D_FF), _rows(tm, D_FF), _rows(tm, D), _const((8, D))],
        out_shape=[jax.ShapeDtypeStruct((t_len, D), F32), ff, ff, ff, jax.ShapeDtypeStruct((t_len, D), BF16),
                   jax.ShapeDtypeStruct((8, D), F32)],
        compiler_params=_params(("arbitrary",)),
    )(dx2, x1, gate, up, g_ffn, w_gate, w_up, w_down)


def mixer_bwd(dx1, proj, ya, pp, yx, probs, conv_w8, w_co, w_pool, pool_scale, k, v, w_xo, w_out, carry=None):
    t_len = dx1.shape[0]
    tm = min(256, t_len)
    n_tiles = t_len // tm
    halo_blocks = tm // HALO

    def body(dx1_ref, proj_ref, halo_ref, ya_ref, pp_ref, yx_ref, p_ref,
             cw_ref, wco_ref, wpool_ref, ps_ref, k_ref, v_ref, wxo_ref, wout_ref, band_ref,
             dproj_ref, merged_ref, dya_ref, dpp_ref, dyx_ref, dk_ref, dv_ref, stat_ref,
             cu_ext, dconv_ext, dpn_ext):
        step = pl.program_id(0)
        tile = n_tiles - 1 - step

        @pl.when(step == 0)
        def _():
            dk_ref[...] = jnp.zeros((N_MEM, D), F32)
            dv_ref[...] = jnp.zeros((N_MEM, D), F32)
            stat_ref[...] = jnp.zeros((8, D), F32)
            dconv_ext[tm:tm + HALO, :] = jnp.zeros((HALO, D), F32)
            dpn_ext[tm:tm + POOL_PAD, :] = jnp.zeros((POOL_PAD, D), BF16)

        dmerged = _dot_nt(dx1_ref[...].astype(BF16), wout_ref[...])
        sa = _sigmoid(proj_ref[:, O_GA:O_GA + D].astype(F32))
        sp = _sigmoid(proj_ref[:, O_GP:O_GP + D].astype(F32))
        sx = _sigmoid(proj_ref[:, O_GX:O_GX + D].astype(F32))
        ya = ya_ref[...].astype(F32)
        ppv = pp_ref[...].astype(F32)
        yp = ppv * ps_ref[...]
        yx = yx_ref[...].astype(F32)
        merged_ref[...] = (sa * ya + sp * yp + sx * yx).astype(BF16)
        dproj_ref[:, O_GA:O_GA + D] = (dmerged * ya * (sa * (1.0 - sa))).astype(BF16)
        dproj_ref[:, O_GP:O_GP + D] = (dmerged * yp * (sp * (1.0 - sp))).astype(BF16)
        dproj_ref[:, O_GX:O_GX + D] = (dmerged * yx * (sx * (1.0 - sx))).astype(BF16)
        dya = (dmerged * sa).astype(BF16)
        dyp = dmerged * sp
        dyx = (dmerged * sx).astype(BF16)
        dya_ref[...] = dya
        dyx_ref[...] = dyx
        stat_ref[1:2, :] += jnp.sum(dyp * ppv, axis=0, keepdims=True)
        dpp = (dyp * ps_ref[...]).astype(BF16)
        dpp_ref[...] = dpp

        da = _dot_nt(dya, wco_ref[...])
        c_a = proj_ref[:, O_CA:O_CA + D].astype(F32)
        u_a = proj_ref[:, O_UA:O_UA + D].astype(F32)
        cu = c_a * u_a
        halo_cu = halo_ref[:, O_CA:O_CA + D].astype(F32) * halo_ref[:, O_UA:O_UA + D].astype(F32)
        cu_ext[0:HALO, :] = jnp.where(tile > 0, halo_cu, 0.0)
        cu_ext[HALO:HALO + tm, :] = cu
        cu1 = cu_ext[HALO - 1:HALO - 1 + tm, :]
        cu2 = cu_ext[HALO - 2:HALO - 2 + tm, :]
        conv = cw_ref[2:3, :] * cu + cw_ref[1:2, :] * cu1 + cw_ref[0:1, :] * cu2
        dproj_ref[:, O_BA:O_BA + D] = (da * conv).astype(BF16)
        dconv = da * proj_ref[:, O_BA:O_BA + D].astype(F32)
        stat_ref[5:6, :] += jnp.sum(dconv * cu2, axis=0, keepdims=True)
        stat_ref[6:7, :] += jnp.sum(dconv * cu1, axis=0, keepdims=True)
        stat_ref[7:8, :] += jnp.sum(dconv * cu, axis=0, keepdims=True)
        dconv_ext[0:tm, :] = dconv
        dcu = (cw_ref[2:3, :] * dconv + cw_ref[1:2, :] * dconv_ext[1:1 + tm, :]
               + cw_ref[0:1, :] * dconv_ext[2:2 + tm, :])
        dproj_ref[:, O_CA:O_CA + D] = (dcu * u_a).astype(BF16)
        dproj_ref[:, O_UA:O_UA + D] = (dcu * c_a).astype(BF16)

        for g, window in enumerate(POOL_WINDOWS):
            cols = slice(g * GROUP_DIM, (g + 1) * GROUP_DIM)
            dpooled = _dot_nt(dpp[:, cols], wpool_ref[g])
            dpn_ext[0:tm, cols] = (dpooled * _inv_count(tile, tm, window)).astype(BF16)
            acc = _dot(band_ref[g], dpn_ext[:, cols])
            dproj_ref[:, O_UP + g * GROUP_DIM:O_UP + (g + 1) * GROUP_DIM] = (acc - dpooled).astype(BF16)

        do = _dot_nt(dyx, wxo_ref[...])
        for hd in range(HEADS):
            cols = slice(hd * HEAD_DIM, (hd + 1) * HEAD_DIM)
            q = proj_ref[:, O_QX + hd * HEAD_DIM:O_QX + (hd + 1) * HEAD_DIM]
            kh = k_ref[:, cols]
            p16 = p_ref[:, hd * N_MEM:(hd + 1) * N_MEM]
            p = p16.astype(F32)
            doh = do[:, cols].astype(BF16)
            dp = _dot_nt(doh, v_ref[:, cols])
            dv_ref[:, cols] += _dot_tn(p16, doh)
            ds = (p * (dp - jnp.sum(dp * p, axis=-1, keepdims=True)) * ATT_SCALE).astype(BF16)
            dproj_ref[:, O_QX + hd * HEAD_DIM:O_QX + (hd + 1) * HEAD_DIM] = _dot(ds, kh).astype(BF16)
            dk_ref[:, cols] += _dot_tn(ds, q)

        dconv_ext[tm:tm + HALO, :] = dconv_ext[0:HALO, :]
        dpn_ext[tm:tm + HALO, :] = dpn_ext[0:HALO, :]

    def rev(width):
        return pl.BlockSpec((tm, width), lambda s: (n_tiles - 1 - s, 0))

    halo_spec = pl.BlockSpec((HALO, D_IN), lambda s: (jnp.maximum((n_tiles - 1 - s) * halo_blocks - 1, 0), 0))
    act = jax.ShapeDtypeStruct((t_len, D), BF16)
    kv_grad = jax.ShapeDtypeStruct((N_MEM, D), F32)
    return _carried_call(
        body, "mixer_bwd", (n_tiles,),
        in_specs=[rev(D), rev(D_IN), halo_spec, rev(D), rev(D), rev(D), rev(D),
                  _resident((8, D)), _resident((D, D)), _resident((GROUPS, GROUP_DIM, GROUP_DIM)), _resident((1, D)),
                  _resident((N_MEM, D)), _resident((N_MEM, D)), _resident((D, D)), _resident((D, D)),
                  _resident((GROUPS, tm, tm + POOL_PAD))],
        out_specs=[rev(D_IN), rev(D), rev(D), rev(D), rev(D),
                   _const((N_MEM, D)), _const((N_MEM, D)), _const((8, D))],
        out_shape=[jax.ShapeDtypeStruct((t_len, D_IN), BF16), act, act, act, act, kv_grad, kv_grad,
                   jax.ShapeDtypeStruct((8, D), F32)],
        scratch_shapes=[pltpu.VMEM((tm + HALO, D), F32)] * 2 + [pltpu.VMEM((tm + POOL_PAD, D), BF16)],
        semantics=("arbitrary",),
        args=(dx1, proj, proj, ya, pp, yx, probs, conv_w8, w_co, w_pool, pool_scale, k, v, w_xo, w_out,
              _window_bands(tm, False)), carry=carry)


def in_bwd(dproj, w_in, x, dx1, g_mix, carry=None):
    t_len = x.shape[0]
    tm = min(512, t_len)

    def body(dproj_ref, w_ref, x_ref, dx1_ref, g_ref, gx_ref, stat_ref):
        @pl.when(pl.program_id(0) == 0)
        def _():
            stat_ref[...] = jnp.zeros((8, D), F32)

        dh = _dot_nt(dproj_ref[...], w_ref[...])
        xv = x_ref[...]
        r = lax.rsqrt(jnp.mean(xv * xv, axis=-1, keepdims=True) + EPS)
        xh = xv * r
        stat_ref[0:1, :] += jnp.sum(dh * xh, axis=0, keepdims=True)
        dxh = dh * g_ref[...]
        gx_ref[...] = dx1_ref[...] + r * (dxh - xh * jnp.mean(dxh * xh, axis=-1, keepdims=True))

    return _carried_call(
        body, "in_bwd", (t_len // tm,),
        in_specs=[_rows(tm, D_IN), _resident((D, D_IN)), _rows(tm, D), _rows(tm, D), _resident((1, D))],
        out_specs=[_rows(tm, D), _const((8, D))],
        out_shape=[jax.ShapeDtypeStruct((t_len, D), F32), jax.ShapeDtypeStruct((8, D), F32)],
        scratch_shapes=[], semantics=("arbitrary",), args=(dproj, w_in, x, dx1, g_mix), carry=carry)


def kv_bwd(dk, dv, memn, mem, g_mem, w_kv4):
    half = D // 2

    def body(dk_ref, dv_ref, memn_ref, mem_ref, g_ref, w_ref, gw_ref, gw16_ref, stat_ref):
        mn = memn_ref[...]
        parts = (dk_ref[:, 0:half], dk_ref[:, half:D], dv_ref[:, 0:half], dv_ref[:, half:D])
        dmemn = jnp.zeros((N_MEM, D), F32)
        for j, part in enumerate(parts):
            part = part.astype(BF16)
            gw = _dot_tn(mn, part)
            gw_ref[j] = gw
            gw16_ref[j] = gw.astype(BF16)
            dmemn = dmemn + _dot_nt(part, w_ref[j])
        mv = mem_ref[...]
        r = lax.rsqrt(jnp.mean(mv * mv, axis=-1, keepdims=True) + EPS)
        stat_ref[...] = jnp.zeros((8, D), F32)
        stat_ref[0:1, :] = jnp.sum(dmemn * (mv * r), axis=0, keepdims=True)

    return pl.pallas_call(
        body, name="kv_bwd",
        out_shape=[jax.ShapeDtypeStruct((N_CHIPS, D, half), F32), jax.ShapeDtypeStruct((N_CHIPS, D, half), BF16),
                   jax.ShapeDtypeStruct((8, D), F32)],
        compiler_params=_params(),
    )(dk, dv, memn, mem, g_mem, w_kv4)


def matmul_tn(name, a, b, tn, col_blocks=1, carry=None):
    t_len, k_dim = a.shape
    n_dim = b.shape[1]
    tt = min(1024, t_len)
    per_block = n_dim // col_blocks // tn

    def body(a_ref, b_ref, out_ref, out16_ref):
        @pl.when(pl.program_id(1) == 0)
        def _():
            out_ref[...] = jnp.zeros((k_dim, tn), F32)

        out_ref[...] += _dot_tn(a_ref[...].astype(BF16), b_ref[...].astype(BF16))

        @pl.when(pl.program_id(1) == t_len // tt - 1)
        def _():
            out16_ref[...] = out_ref[...].astype(BF16)

    if col_blocks == 1:
        out_spec = pl.BlockSpec((k_dim, tn), lambda n, t: (0, n))
        shape = (k_dim, n_dim)
    else:
        out_spec = pl.BlockSpec((None, k_dim, tn), lambda n, t: (n // per_block, 0, n % per_block))
        shape = (col_blocks, k_dim, n_dim // col_blocks)
    outs, carried = _carried_call(
        body, name, (n_dim // tn, t_len // tt),
        in_specs=[pl.BlockSpec((tt, k_dim), lambda n, t: (t, 0)), pl.BlockSpec((tt, tn), lambda n, t: (t, n))],
        out_specs=[out_spec, out_spec], out_shape=[jax.ShapeDtypeStruct(shape, F32), jax.ShapeDtypeStruct(shape, BF16)],
        scratch_shapes=[], semantics=("arbitrary", "arbitrary"), args=(a, b), carry=carry)
    return (tuple(outs), carried) if carry is not None else tuple(outs)


def grad_w_pool(pooled, dpp):
    t_len = pooled.shape[0]
    tt = min(1024, t_len)
    steps = t_len // tt

    def body(a_ref, b_ref, out_ref, out16_ref):
        @pl.when(pl.program_id(0) == 0)
        def _():
            out_ref[...] = jnp.zeros((GROUPS, GROUP_DIM, GROUP_DIM), F32)

        for g in range(GROUPS):
            cols = slice(g * GROUP_DIM, (g + 1) * GROUP_DIM)
            out_ref[g] += _dot_tn(a_ref[:, cols], b_ref[:, cols])

        @pl.when(pl.program_id(0) == steps - 1)
        def _():
            out16_ref[...] = out_ref[...].astype(BF16)

    shape = (GROUPS, GROUP_DIM, GROUP_DIM)
    return pl.pallas_call(
        body, name="grad_w_pool", grid=(steps,),
        in_specs=[_rows(tt, D), _rows(tt, D)], out_specs=[_const(shape), _const(shape)],
        out_shape=[jax.ShapeDtypeStruct(shape, F32), jax.ShapeDtypeStruct(shape, BF16)],
        compiler_params=_params(("arbitrary",)),
    )(pooled, dpp)


def _place():
    x, y, c = lax.axis_index("x"), lax.axis_index("y"), lax.axis_index("c")
    return x, y, c


def _other_chips(x, y):
    return [(1 - x, y), (x, 1 - y), (1 - x, 1 - y)]


def _any_specs(n):
    return [pl.BlockSpec(memory_space=pl.ANY)] * n


def gather_weights(shards, gather_cls):
    n = len(shards)

    def body(*refs):
        gather = gather_cls(refs[:n], refs[n:2 * n], *refs[2 * n:])
        gather.start()
        gather.middle()
        gather.finish()

    return pl.pallas_call(
        body, name="gather_weights",
        in_specs=_any_specs(n), out_specs=_any_specs(n),
        out_shape=gather_cls.out_shapes(shards), scratch_shapes=gather_cls.sems(n),
    )(*shards)


class _Gather:
    SLOTS = 8
    ROW_ALIGN = 16

    @staticmethod
    def out_shapes(shards):
        return [jax.ShapeDtypeStruct((N_CHIPS,) + s.shape, s.dtype) for s in shards]

    @staticmethod
    def sems(n):
        return [pltpu.SemaphoreType.DMA((n, _Gather.SLOTS)), pltpu.SemaphoreType.DMA((n, _Gather.SLOTS))]

    def __init__(self, ins, outs, send_sems, recv_sems):
        self.ins, self.outs, self.send_sems, self.recv_sems = ins, outs, send_sems, recv_sems
        x, y, c = _place()
        self.c, self.me, self.sibling = c, 2 * x + y, (x, y, 1 - c)
        self.across = [(1 - x, y), (x, 1 - y), (1 - x, 1 - y)]

    def _rows(self, a, which, part=None):
        half = self.ins[a].shape[1] // 2
        first = (half // 2) // self.ROW_ALIGN * self.ROW_ALIGN
        if part is None:
            return pl.ds(which * half, half)
        return pl.ds(which * half, first) if part == 0 else pl.ds(which * half + first, half - first)

    def _has_part(self, a, part):
        half = self.ins[a].shape[1] // 2
        return part == 1 or (half // 2) // self.ROW_ALIGN > 0

    def _block(self, a, chip, rows=slice(None)):
        index = chip if not isinstance(chip, tuple) else 2 * chip[0] + chip[1]
        if len(self.outs[a].shape) == len(self.ins[a].shape):
            cols = self.ins[a].shape[2]
            return self.outs[a].at[:, rows, pl.ds(pl.multiple_of(index * cols, cols), cols)]
        return self.outs[a].at[index, :, rows, :]

    def _remote(self, src, dst, a, slot, to):
        return pltpu.make_async_remote_copy(src_ref=src, dst_ref=dst, send_sem=self.send_sems.at[a, slot],
                                            recv_sem=self.recv_sems.at[a, slot], device_id=to, device_id_type=MESH)

    def _own(self, a):
        return self._remote(self.ins[a], self._block(a, self.me), a, 6, self.sibling)

    def _sent(self, a, axis):
        rows = self._rows(a, self.c)
        return self._remote(self.ins[a].at[:, rows, :], self._block(a, self.me, rows), a, axis,
                            (*self.across[axis], self.c))

    def _landed(self, a, axis):
        block = self._block(a, self.across[axis], self._rows(a, self.c))
        return self._remote(block, block, a, axis, (*self.across[axis], self.c))

    def _relayed(self, a, part, incoming):
        source = self.across[2] if incoming else self.across[part]
        block = self._block(a, source, self._rows(a, self.c, part))
        return self._remote(block, block, a, (2, 7)[part], (*self.across[1 - part], self.c))

    def _passed_on(self, a, source, which):
        block = self._block(a, self.across[source], self._rows(a, which))
        return self._remote(block, block, a, 3 + source, self.sibling)

    def start(self):
        for a in range(len(self.ins)):
            self._own(a).start()
        for a in range(len(self.ins)):
            for axis in range(2):
                self._sent(a, axis).start()

    def middle(self):
        for a in range(len(self.ins)):
            for axis in range(2):
                self._landed(a, axis).wait_recv()
                if self._has_part(a, axis):
                    self._relayed(a, axis, incoming=False).start()
                self._passed_on(a, axis, self.c).start()

    def finish(self):
        n = len(self.ins)
        for a in range(n):
            for part in range(2):
                if self._has_part(a, part):
                    self._relayed(a, part, incoming=True).wait_recv()
            self._passed_on(a, 2, self.c).start()
        for a in range(n):
            for source in range(3):
                self._passed_on(a, source, 1 - self.c).wait_recv()
            self._own(a).wait_recv()
        for a in range(n):
            self._own(a).wait_send()
            for axis in range(2):
                self._sent(a, axis).wait_send()
                if self._has_part(a, axis):
                    self._relayed(a, axis, incoming=False).wait_send()
            for source in range(3):
                self._passed_on(a, source, self.c).wait_send()


class _HalfExchange:
    @staticmethod
    def out_shapes(grads):
        return [jax.ShapeDtypeStruct((N_CHIPS, g.shape[1] // 2, g.shape[2]), g.dtype) for g in grads]

    @staticmethod
    def sems(n):
        return [pltpu.SemaphoreType.DMA((n,)), pltpu.SemaphoreType.DMA((n,))]

    def __init__(self, ins, outs, send_sems, recv_sems):
        self.ins, self.outs, self.send_sems, self.recv_sems = ins, outs, send_sems, recv_sems

    def _copies(self):
        x, y, c = _place()
        for a in range(len(self.ins)):
            h = self.ins[a].shape[1] // 2
            yield pltpu.make_async_remote_copy(
                src_ref=self.ins[a].at[:, pl.ds((1 - c) * h, h), :], dst_ref=self.outs[a],
                send_sem=self.send_sems.at[a], recv_sem=self.recv_sems.at[a], device_id=(x, y, 1 - c), device_id_type=MESH)

    def start(self):
        for cp in self._copies():
            cp.start()

    def finish(self):
        for cp in self._copies():
            cp.wait()


class _GatherColumns(_Gather):
    @staticmethod
    def out_shapes(shards):
        return [jax.ShapeDtypeStruct(s.shape[:2] + (N_CHIPS * s.shape[2],), s.dtype) for s in shards]


def exchange_sibling_halves(name, grads):
    n = len(grads)

    def body(*refs):
        exchange = _HalfExchange(refs[:n], refs[n:2 * n], *refs[2 * n:])
        exchange.start()
        exchange.finish()

    return pl.pallas_call(
        body, name=name,
        in_specs=_any_specs(n), out_specs=_any_specs(n),
        out_shape=_HalfExchange.out_shapes(grads), scratch_shapes=_HalfExchange.sems(n),
    )(*grads)


def _row_tile(rows, cols, budget=1 << 20):
    best = 16
    for tr in range(16, rows + 1, 16):
        if rows % tr == 0 and tr * cols * 4 <= budget:
            best = tr
    return best


def add_sibling_half(name, grad, got, core):
    _, rows, cols = grad.shape
    h = rows // 2
    tr = _row_tile(h, cols)
    per_half = h // tr

    def body(core_ref, g_ref, o_ref, out_ref, out16_ref):
        total = g_ref[...] + o_ref[...].astype(F32)
        out_ref[...] = total
        out16_ref[...] = total.astype(BF16)

    out_spec = pl.BlockSpec((None, tr, cols), lambda j, r, cr: (j, r, 0))
    return pl.pallas_call(
        body, name=name,
        grid_spec=pltpu.PrefetchScalarGridSpec(
            num_scalar_prefetch=1, grid=(N_CHIPS, per_half),
            in_specs=[pl.BlockSpec((None, tr, cols), lambda j, r, cr: (j, cr[0] * per_half + r, 0)),
                      pl.BlockSpec((None, tr, cols), lambda j, r, cr: (j, r, 0))],
            out_specs=[out_spec, out_spec]),
        out_shape=[jax.ShapeDtypeStruct((N_CHIPS, h, cols), F32), jax.ShapeDtypeStruct((N_CHIPS, h, cols), BF16)],
        compiler_params=_params(("arbitrary", "arbitrary")),
    )(core, grad, got)


class _ChipExchange:
    @staticmethod
    def out_shapes(partials):
        return [jax.ShapeDtypeStruct((3,) + p.shape[1:], p.dtype) for p in partials]

    @staticmethod
    def sems(n):
        return [pltpu.SemaphoreType.DMA((n, 3)), pltpu.SemaphoreType.DMA((n, 3))]

    def __init__(self, ins, outs, send_sems, recv_sems):
        self.ins, self.outs, self.send_sems, self.recv_sems = ins, outs, send_sems, recv_sems

    def _copies(self):
        x, y, c = _place()
        for a in range(len(self.ins)):
            for slot, (px, py) in enumerate(_other_chips(x, y)):
                yield pltpu.make_async_remote_copy(
                    src_ref=self.ins[a].at[2 * px + py], dst_ref=self.outs[a].at[slot],
                    send_sem=self.send_sems.at[a, slot], recv_sem=self.recv_sems.at[a, slot],
                    device_id=(px, py, c), device_id_type=MESH)

    def start(self):
        for cp in self._copies():
            cp.start()

    def finish(self):
        for cp in self._copies():
            cp.wait()


def add_chip_blocks(name, partial, got, chip):
    _, h, cols = partial.shape
    tr = _row_tile(h, cols)

    def body(chip_ref, p_ref, g0_ref, g1_ref, g2_ref, out_ref):
        out_ref[...] = ((p_ref[...] + g0_ref[...].astype(F32)) + g1_ref[...].astype(F32)) + g2_ref[...].astype(F32)

    def got_spec(slot):
        return pl.BlockSpec((None, tr, cols), lambda r, ch: (slot, r, 0))

    return pl.pallas_call(
        body, name=name,
        grid_spec=pltpu.PrefetchScalarGridSpec(
            num_scalar_prefetch=1, grid=(h // tr,),
            in_specs=[pl.BlockSpec((None, tr, cols), lambda r, ch: (ch[0], r, 0)), got_spec(0), got_spec(1), got_spec(2)],
            out_specs=pl.BlockSpec((tr, cols), lambda r, ch: (r, 0))),
        out_shape=jax.ShapeDtypeStruct((h, cols), F32),
        compiler_params=_params(("arbitrary",)),
    )(chip, partial, got, got, got)


class _SiblingSwap:
    @staticmethod
    def out_shapes(halves):
        return [jax.ShapeDtypeStruct(v.shape, v.dtype) for v in halves]

    @staticmethod
    def sems(n):
        return [pltpu.SemaphoreType.DMA((n,)), pltpu.SemaphoreType.DMA((n,))]

    def __init__(self, ins, outs, send_sems, recv_sems):
        self.ins, self.outs, self.send_sems, self.recv_sems = ins, outs, send_sems, recv_sems

    def _copies(self):
        x, y, c = _place()
        for a in range(len(self.ins)):
            yield pltpu.make_async_remote_copy(
                src_ref=self.ins[a], dst_ref=self.outs[a], send_sem=self.send_sems.at[a], recv_sem=self.recv_sems.at[a],
                device_id=(x, y, 1 - c), device_id_type=MESH)

    def start(self):
        for cp in self._copies():
            cp.start()

    def finish(self):
        for cp in self._copies():
            cp.wait()


def swap_sibling_halves(halves):
    n = len(halves)

    def body(*refs):
        swap = _SiblingSwap(refs[:n], refs[n:2 * n], *refs[2 * n:])
        swap.start()
        swap.finish()

    return pl.pallas_call(
        body, name="swap_sibling_halves",
        in_specs=_any_specs(n), out_specs=_any_specs(n),
        out_shape=_SiblingSwap.out_shapes(halves), scratch_shapes=_SiblingSwap.sems(n),
    )(*halves)


def all_reduce_small(pack):
    rows = pack.shape[0]

    def body(pack_ref, out_ref, gathered, send_sems, recv_sems):
        x, y, c = _place()
        me = 4 * x + 2 * y + c
        gathered[me] = pack_ref[...]
        copies = []
        for rel in range(1, 8):
            fx, fy, fc = (rel >> 2) & 1, (rel >> 1) & 1, rel & 1
            peer = (x ^ fx, y ^ fy, c ^ fc)
            cp = pltpu.make_async_remote_copy(
                src_ref=pack_ref, dst_ref=gathered.at[me], send_sem=send_sems.at[rel - 1], recv_sem=recv_sems.at[rel - 1],
                device_id=peer, device_id_type=MESH)
            cp.start()
            copies.append(cp)
        for rel in range(1, 8):
            fx, fy, fc = (rel >> 2) & 1, (rel >> 1) & 1, rel & 1
            src = 4 * (x ^ fx) + 2 * (y ^ fy) + (c ^ fc)
            pltpu.make_async_remote_copy(
                src_ref=pack_ref, dst_ref=gathered.at[src], send_sem=send_sems.at[rel - 1], recv_sem=recv_sems.at[rel - 1],
                device_id=(x, y, c), device_id_type=MESH).wait_recv()
        for cp in copies:
            cp.wait_send()
        total = gathered[0]
        for dev in range(1, 8):
            total = total + gathered[dev]
        out_ref[...] = total

    return pl.pallas_call(
        body, name="all_reduce_small",
        in_specs=[pl.BlockSpec(memory_space=pltpu.VMEM)], out_specs=pl.BlockSpec(memory_space=pltpu.VMEM),
        out_shape=jax.ShapeDtypeStruct((rows, D), F32),
        scratch_shapes=[pltpu.VMEM((8, rows, D), F32), pltpu.SemaphoreType.DMA((7,)), pltpu.SemaphoreType.DMA((7,))],
    )(pack)


def _adamw_update(w, g, m, v):
    nm = ADAM_B1 * m + (1.0 - ADAM_B1) * g
    nv = ADAM_B2 * v + (1.0 - ADAM_B2) * (g * g)
    m_hat = nm / (1.0 - ADAM_B1 ** ADAM_STEP)
    v_hat = nv / (1.0 - ADAM_B2 ** ADAM_STEP)
    delta = -ADAM_LR * (m_hat / (jnp.sqrt(v_hat) + ADAM_EPS) + ADAM_WD * w)
    return delta, nm, nv


def adamw(name, w, g, m, v):
    def body(w_ref, g_ref, m_ref, v_ref, d_ref, nm_ref, nv_ref):
        d_ref[...], nm_ref[...], nv_ref[...] = _adamw_update(w_ref[...], g_ref[...], m_ref[...], v_ref[...])

    out = jax.ShapeDtypeStruct(w.shape, F32)
    return pl.pallas_call(body, name=name, out_shape=[out] * 3, compiler_params=_params())(w, g, m, v)


def adamw_halves(name, w, mine, theirs, m, v, core):
    h, cols = mine.shape
    tr = _row_tile(h, cols)
    per_half = h // tr

    def body(core_ref, w_ref, mine_ref, theirs_ref, m_ref, v_ref, g_ref, d_ref, nm_ref, nv_ref):
        g = jnp.where(pl.program_id(0) == core_ref[0], mine_ref[...], theirs_ref[...])
        g_ref[...] = g
        d_ref[...], nm_ref[...], nv_ref[...] = _adamw_update(w_ref[...], g, m_ref[...], v_ref[...])

    full = pl.BlockSpec((tr, cols), lambda hh, r, cr: (hh * per_half + r, 0))
    mine_spec = pl.BlockSpec((tr, cols), lambda hh, r, cr: (jnp.where(hh == cr[0], r, 0), 0))
    theirs_spec = pl.BlockSpec((tr, cols), lambda hh, r, cr: (jnp.where(hh == cr[0], 0, r), 0))
    out = jax.ShapeDtypeStruct((2 * h, cols), F32)
    return pl.pallas_call(
        body, name=name,
        grid_spec=pltpu.PrefetchScalarGridSpec(
            num_scalar_prefetch=1, grid=(2, per_half),
            in_specs=[full, mine_spec, theirs_spec, full, full], out_specs=[full] * 4),
        out_shape=[out] * 4,
        compiler_params=_params(("arbitrary", "arbitrary")),
    )(core, w, mine, theirs, m, v)


BIG = ("w_in", "w_conv_out", "w_pool", "w_kv", "w_xattn_out", "w_out", "w_gate", "w_up", "w_down")


def kernel(x, mem, norm_mix, w_in, conv_w, w_conv_out, w_pool, pool_scale, norm_mem, w_kv, w_xattn_out, w_out, norm_ffn, w_gate, w_up, w_down, norm_final, loss_target, m_norm_mix, m_w_in, m_conv_w, m_w_conv_out, m_w_pool, m_pool_scale, m_norm_mem, m_w_kv, m_w_xattn_out, m_w_out, m_norm_ffn, m_w_gate, m_w_up, m_w_down, m_norm_final, v_norm_mix, v_w_in, v_conv_w, v_w_conv_out, v_w_pool, v_pool_scale, v_norm_mem, v_w_kv, v_w_xattn_out, v_w_out, v_norm_ffn, v_w_gate, v_w_up, v_w_down, v_norm_final):
    t_len = x.shape[1]
    xi, yi, ci = lax.axis_index("x"), lax.axis_index("y"), lax.axis_index("c")
    chip = 2 * xi + yi
    chip_arr = jnp.reshape(chip, (1,)).astype(jnp.int32)
    core_arr = jnp.reshape(ci, (1,)).astype(jnp.int32)

    conv_pad = jnp.concatenate([conv_w, jnp.zeros((1, 13, 256), F32)], axis=1)
    def t2(w):
        return jnp.swapaxes(w, 1, 2)

    (g_in,) = gather_weights([w_in.astype(BF16)], _GatherColumns)
    w_in_f = g_in[0]

    x2d = x[0]
    tgt = loss_target[0]
    (proj, h), (g_kv, g_conv_w, g_co, g_xo, g_out, g_pool, g_gate) = proj_fwd(
        x2d, norm_mix, w_in_f,
        carry=(_Gather, [w_kv.astype(BF16), conv_pad,
                         w_conv_out.astype(BF16), w_xattn_out.astype(BF16), w_out.astype(BF16),
                         w_pool[0].astype(BF16),
                         t2(w_gate).astype(BF16)]))
    w_kv4 = g_kv.reshape(N_CHIPS, D, D // 2)
    conv_full = jnp.transpose(g_conv_w[:, 0, 0:8, :], (1, 0, 2)).reshape(8, D)
    w_co_f, w_xo_f, w_out_f = g_co.reshape(D, D), g_xo.reshape(D, D), g_out.reshape(D, D)
    w_pool_f = jnp.transpose(g_pool, (1, 0, 2, 3)).reshape(GROUPS, GROUP_DIM, GROUP_DIM)
    memn, k, v = kv_fwd(mem[0], norm_mem, w_kv4)
    (a, pooled, ya, pp, yx, o, probs, x1), (g_up, g_down) = mixer_fwd(
        proj, x2d, conv_full, w_co_f, w_pool_f, pool_scale, k, v, w_xo_f, w_out_f,
        carry=(_Gather, [t2(w_up).astype(BF16), w_down.astype(BF16)]))
    w_gate_f, w_up_f, w_down_f = g_gate.reshape(D_FF, D), g_up.reshape(D_FF, D), g_down.reshape(D_FF, D)
    gate, up, dx2, stat_f = ffn_fwd(x1, tgt, norm_ffn, w_gate_f, w_up_f, w_down_f, norm_final.reshape(1, D))

    def by_chip(pair):
        return tuple(gw.reshape(N_CHIPS, gw.shape[0] // N_CHIPS, gw.shape[1]) for gw in pair)

    def chip_partials(names, grads, got):
        return [add_sibling_half("add_sibling_" + n, g32, o_, core_arr) for n, (g32, _), o_ in zip(names, grads, got)]

    def chip_sums(names, partials, got):
        return [add_chip_blocks("add_chips_" + n, p32, g2, chip_arr) for n, (p32, _), g2 in zip(names, partials, got)]

    dx1, dgate, dup, act, h2, stat_b1 = ffn_bwd(dx2, x1, gate, up, norm_ffn, w_gate_f, w_up_f, w_down_f)
    gw_gate = by_chip(matmul_tn("grad_w_gate", dgate, h2, 512))
    gw_up, got_gate = matmul_tn("grad_w_up", dup, h2, 512, carry=(_HalfExchange, [gw_gate[1]]))
    gw_up = by_chip(gw_up)
    gw_down, got_up = matmul_tn("grad_w_down", act, dx2, 512, carry=(_HalfExchange, [gw_up[1]]))
    gw_down = by_chip(gw_down)
    got_down = exchange_sibling_halves("exchange_sibling_halves_ffn", [gw_down[1]])
    names_ffn = ("w_gate", "w_up", "w_down")
    part_ffn = chip_partials(names_ffn, [gw_gate, gw_up, gw_down], list(got_gate) + list(got_up) + list(got_down))

    (dproj, merged, dya, dpp, dyx, dk, dv, stat_b2), got_ffn = mixer_bwd(
        dx1, proj, ya, pp, yx, probs, conv_full, w_co_f, w_pool_f, pool_scale, k, v, w_xo_f, w_out_f,
        carry=(_ChipExchange, [p16 for _, p16 in part_ffn]))
    gw_kv32, gw_kv16, stat_kv = kv_bwd(dk, dv, memn, mem[0], norm_mem, w_kv4)
    gw_pool = tuple(jnp.transpose(gw.reshape(GROUPS, N_CHIPS, 64, GROUP_DIM), (1, 0, 2, 3)).reshape(N_CHIPS, 256, GROUP_DIM)
                    for gw in grad_w_pool(pooled, dpp))
    gw_co, got_kv_pool = matmul_tn("grad_w_conv_out", a, dya, 1024, carry=(_HalfExchange, [gw_kv16, gw_pool[1]]))
    gw_co = by_chip(gw_co)
    gw_xo, got_co = matmul_tn("grad_w_xattn_out", o, dyx, 1024, carry=(_HalfExchange, [gw_co[1]]))
    gw_xo = by_chip(gw_xo)
    gw_out, got_xo = matmul_tn("grad_w_out", merged, dx1, 1024, carry=(_HalfExchange, [gw_xo[1]]))
    gw_out = by_chip(gw_out)
    got_out = exchange_sibling_halves("exchange_sibling_halves_mixer", [gw_out[1]])
    names_mix = ("w_kv", "w_pool", "w_conv_out", "w_xattn_out", "w_out")
    part_mix = chip_partials(names_mix, [(gw_kv32, gw_kv16), gw_pool, gw_co, gw_xo, gw_out],
                             list(got_kv_pool) + list(got_co) + list(got_xo) + list(got_out))

    gw_in, got_mix = matmul_tn("grad_w_in", h, dproj, 2048, col_blocks=N_CHIPS,
                               carry=(_ChipExchange, [p16 for _, p16 in part_mix]))
    part_in = chip_partials(("w_in",), [gw_in], exchange_sibling_halves("exchange_sibling_halves_in", [gw_in[1]]))
    mine_early = chip_sums(names_ffn + names_mix, part_ffn + part_mix, list(got_ffn) + list(got_mix))
    (grad_x, stat_b3), (got_in, theirs_early) = in_bwd(
        dproj, w_in_f, x2d, dx1, norm_mix,
        carry=[(_ChipExchange, [p16 for _, p16 in part_in]), (_SiblingSwap, mine_early)])
    mine_in = chip_sums(("w_in",), part_in, got_in)
    theirs_in = swap_sibling_halves(mine_in)
    reduced = dict(zip(names_ffn + names_mix + ("w_in",),
                       zip(mine_early + mine_in, list(theirs_early) + list(theirs_in))))
    mine = [reduced[n][0] for n in BIG]
    theirs = [reduced[n][1] for n in BIG]

    pack = jnp.concatenate([stat_b3[0:1], stat_b2[1:2], stat_kv[0:1], stat_b1[0:1], stat_f[0:1], stat_b2[5:8],
                            stat_f[1:2], jnp.zeros((7, D), F32)], axis=0)
    total = all_reduce_small(pack)
    loss = jnp.sum(total[8])
    g_conv_full = total[5:8]
    g_conv = lax.dynamic_slice_in_dim(g_conv_full, chip * 256, 256, axis=1)

    given = dict(w_in=(w_in, m_w_in, v_w_in), w_conv_out=(w_conv_out, m_w_conv_out, v_w_conv_out),
                 w_pool=(w_pool, m_w_pool, v_w_pool), w_kv=(w_kv, m_w_kv, v_w_kv),
                 w_xattn_out=(w_xattn_out, m_w_xattn_out, v_w_xattn_out), w_out=(w_out, m_w_out, v_w_out),
                 w_gate=(w_gate, m_w_gate, v_w_gate), w_up=(w_up, m_w_up, v_w_up), w_down=(w_down, m_w_down, v_w_down))
    out_g, out_d, out_m, out_v = {}, {}, {}, {}
    for n, mine_n, theirs_n in zip(BIG, mine, theirs):
        transposed = n in ("w_gate", "w_up")
        rows2d = (2 * mine_n.shape[0], mine_n.shape[1])
        w_, m_, v_ = ((t2(t) if transposed else t).reshape(rows2d) for t in given[n])
        res = adamw_halves("adamw_" + n, w_, mine_n, theirs_n, m_, v_, core_arr)
        if transposed:
            res = [t2(t.reshape(1, D_FF // N_CHIPS, D)) for t in res]
        out_g[n], out_d[n], out_m[n], out_v[n] = (t.reshape(given[n][0].shape) for t in res)

    def small_pack(vals, conv_part):
        conv_rows = jnp.concatenate([conv_part.reshape(3, 256), jnp.zeros((3, D - 256), F32)], axis=1)
        return jnp.concatenate([val.reshape(1, D) for val in vals] + [conv_rows], axis=0)

    sw = small_pack([norm_mix, pool_scale, norm_mem, norm_ffn, norm_final], conv_w)
    sm = small_pack([m_norm_mix, m_pool_scale, m_norm_mem, m_norm_ffn, m_norm_final], m_conv_w)
    sv = small_pack([v_norm_mix, v_pool_scale, v_norm_mem, v_norm_ffn, v_norm_final], v_conv_w)
    sg = small_pack([total[r] for r in range(5)], g_conv)
    sd, snm, snv = adamw("adamw_small", sw, sg, sm, sv)
    small_names = ("norm_mix", "pool_scale", "norm_mem", "norm_ffn", "norm_final")
    small_shapes = dict(norm_mix=(1, D), pool_scale=(1, D), norm_mem=(1, D), norm_ffn=(1, D), norm_final=(D,))
    for r, n in enumerate(small_names):
        out_g[n], out_d[n], out_m[n], out_v[n] = (t[r].reshape(small_shapes[n]) for t in (sg, sd, snm, snv))
    out_g["conv_w"], out_d["conv_w"], out_m["conv_w"], out_v["conv_w"] = (
        t[5:8, 0:256].reshape(1, 3, 256) for t in (sg, sd, snm, snv))

    order = ("norm_mix", "w_in", "conv_w", "w_conv_out", "w_pool", "pool_scale", "norm_mem", "w_kv", "w_xattn_out",
             "w_out", "norm_ffn", "w_gate", "w_up", "w_down", "norm_final")
    return (loss, grad_x.reshape(1, t_len, D), *[out_g[n] for n in order], *[out_d[n] for n in order],
            *[out_m[n] for n in order], *[out_v[n] for n in order])
```

```python
import functools

import jax
import jax.numpy as jnp
from jax import lax
from jax.experimental import pallas as pl
from jax.experimental.pallas import tpu as pltpu

F32 = jnp.float32
BF16 = jnp.bfloat16
MESH = pl.DeviceIdType.MESH

D = 1024
N_MEM = 256
HEADS = 4
HEAD_DIM = 256
GROUPS = 4
GROUP_DIM = 256
POOL_WINDOWS = (2, 4, 8, 16)
D_FF = 2816
D_IN = 8192
N_CHIPS = 4
EPS = 1e-6
HALO = 16
POOL_PAD = 128
ATT_SCALE = HEAD_DIM ** -0.5

ADAM_LR = 0.001
ADAM_B1 = 0.9
ADAM_B2 = 0.999
ADAM_EPS = 1e-08
ADAM_WD = 0.01
ADAM_STEP = 10

VMEM_LIMIT = 56 * 1024 * 1024

O_BA, O_CA, O_UA, O_UP, O_QX, O_GA, O_GP, O_GX = (k * D for k in range(8))

NT_DIMS = (((1,), (1,)), ((), ()))
TN_DIMS = (((0,), (0,)), ((), ()))


def _dot(a, b):
    return jnp.dot(a, b, preferred_element_type=F32)


def _dot_nt(a, b):
    return lax.dot_general(a, b, NT_DIMS, preferred_element_type=F32)


def _dot_tn(a, b):
    return lax.dot_general(a, b, TN_DIMS, preferred_element_type=F32)


def _sigmoid(z):
    return pl.reciprocal(1.0 + jnp.exp(-z), approx=True)


def _params(semantics=None):
    return pltpu.CompilerParams(dimension_semantics=semantics, vmem_limit_bytes=VMEM_LIMIT)


def _resident(shape):
    zeros = (0,) * len(shape)
    return pl.BlockSpec(shape, lambda *_: zeros, pipeline_mode=pl.Buffered(1))


def _const(shape):
    zeros = (0,) * len(shape)
    return pl.BlockSpec(shape, lambda *_: zeros)


def _rows(tm, width):
    return pl.BlockSpec((tm, width), lambda i: (i, 0))


def _inv_count(tile, tm, window):
    t = tile * tm + lax.broadcasted_iota(jnp.int32, (tm, 1), 0)
    return 1.0 / jnp.minimum(t + 1, window).astype(F32)


def _carried_call(body, name, grid, in_specs, out_specs, out_shape, scratch_shapes, semantics, args, carry):
    if carry is None:
        res = pl.pallas_call(body, name=name, grid=grid, in_specs=in_specs, out_specs=out_specs, out_shape=out_shape,
                             scratch_shapes=scratch_shapes, compiler_params=_params(semantics))(*args)
        return res, []
    carries = [carry] if isinstance(carry, tuple) else list(carry)
    comm_args = [arr for _, arrs in carries for arr in arrs]
    n, n_in, n_out, n_scratch = len(comm_args), len(in_specs), len(out_specs), len(scratch_shapes)
    shapes_of = [cls.out_shapes(arrs) for cls, arrs in carries]
    comm_shapes = [s for shapes in shapes_of for s in shapes]
    m = len(comm_shapes)
    comm_sems = [s for cls, arrs in carries for s in cls.sems(len(arrs))]

    def carrying(*refs):
        ins, comm_ins = refs[:n_in], refs[n_in:n_in + n]
        outs, comm_outs = refs[n_in + n:n_in + n + n_out], refs[n_in + n + n_out:n_in + n + n_out + m]
        scratch, sems = refs[n_in + n + n_out + m:n_in + n + n_out + m + n_scratch], refs[n_in + n + n_out + m + n_scratch:]
        steps = [pl.program_id(d) for d in range(len(grid))]
        first = functools.reduce(jnp.logical_and, [s == 0 for s in steps])
        last = functools.reduce(jnp.logical_and, [s == g - 1 for s, g in zip(steps, grid)])

        def exchanges():
            at_in = at_out = 0
            for k, (cls, arrs) in enumerate(carries):
                yield cls(comm_ins[at_in:at_in + len(arrs)], comm_outs[at_out:at_out + len(shapes_of[k])],
                          sems[2 * k], sems[2 * k + 1])
                at_in, at_out = at_in + len(arrs), at_out + len(shapes_of[k])

        @pl.when(first)
        def _():
            for exchange in exchanges():
                exchange.start()

        if any(hasattr(cls, "middle") for cls, _ in carries):
            linear, total = 0, 1
            for s, g in zip(steps, grid):
                linear, total = linear * g + s, total * g

            @pl.when(linear == (5 * total) // 8)
            def _():
                for exchange in exchanges():
                    if hasattr(exchange, "middle"):
                        exchange.middle()

        body(*ins, *outs, *scratch)

        @pl.when(last)
        def _():
            for exchange in exchanges():
                exchange.finish()

    res = pl.pallas_call(
        carrying, name=name, grid=grid, in_specs=list(in_specs) + _any_specs(n), out_specs=list(out_specs) + _any_specs(m),
        out_shape=list(out_shape) + comm_shapes, scratch_shapes=list(scratch_shapes) + comm_sems,
        compiler_params=_params(semantics))(*args, *comm_args)
    comm_res, at = [], n_out
    for (_, arrs), shapes in zip(carries, shapes_of):
        comm_res.append(res[at:at + len(arrs)])
        at += len(shapes)
    return res[:n_out], (comm_res[0] if isinstance(carry, tuple) else comm_res)


def proj_fwd(x, g_mix, w_in, carry=None):
    t_len = x.shape[0]
    tm = min(1024, t_len)
    tn = D_IN // N_CHIPS

    def body(x_ref, g_ref, w_ref, proj_ref, h_ref):
        @pl.when(pl.program_id(1) == 0)
        def _():
            xv = x_ref[...]
            r = lax.rsqrt(jnp.mean(xv * xv, axis=-1, keepdims=True) + EPS)
            h_ref[...] = (xv * r * g_ref[...]).astype(BF16)

        proj_ref[...] = _dot(h_ref[...], w_ref[...]).astype(BF16)

    return _carried_call(
        body, "proj_fwd", (t_len // tm, N_CHIPS),
        in_specs=[pl.BlockSpec((tm, D), lambda i, j: (i, 0)),
                  pl.BlockSpec((1, D), lambda i, j: (0, 0)),
                  pl.BlockSpec((D, tn), lambda i, j: (0, j))],
        out_specs=[pl.BlockSpec((tm, tn), lambda i, j: (i, j)),
                   pl.BlockSpec((tm, D), lambda i, j: (i, 0))],
        out_shape=[jax.ShapeDtypeStruct((t_len, D_IN), BF16), jax.ShapeDtypeStruct((t_len, D), BF16)],
        scratch_shapes=[], semantics=("arbitrary", "arbitrary"), args=(x, g_mix, w_in), carry=carry)


def kv_fwd(mem, g_mem, w_kv4):
    half = D // 2

    def body(mem_ref, g_ref, w_ref, memn_ref, k_ref, v_ref):
        mv = mem_ref[...]
        r = lax.rsqrt(jnp.mean(mv * mv, axis=-1, keepdims=True) + EPS)
        mn = (mv * r * g_ref[...]).astype(BF16)
        memn_ref[...] = mn
        k_ref[:, 0:half] = _dot(mn, w_ref[0]).astype(BF16)
        k_ref[:, half:D] = _dot(mn, w_ref[1]).astype(BF16)
        v_ref[:, 0:half] = _dot(mn, w_ref[2]).astype(BF16)
        v_ref[:, half:D] = _dot(mn, w_ref[3]).astype(BF16)

    out = jax.ShapeDtypeStruct((N_MEM, D), BF16)
    return pl.pallas_call(body, name="kv_fwd", out_shape=[out, out, out], compiler_params=_params())(mem, g_mem, w_kv4)


def _softmax_rows(s):
    m = jnp.max(s, axis=-1, keepdims=True)
    e = jnp.exp(s - m)
    return e * pl.reciprocal(jnp.sum(e, axis=-1, keepdims=True), approx=True)


def _window_bands(tm, causal):
    t = lax.broadcasted_iota(jnp.int32, (tm, tm + POOL_PAD), 0)
    s = lax.broadcasted_iota(jnp.int32, (tm, tm + POOL_PAD), 1)
    d = (t + POOL_PAD - s) if causal else (s - t)
    return jnp.stack([((d >= 0) & (d < w)).astype(BF16) for w in POOL_WINDOWS])


def mixer_fwd(proj, x, conv_w8, w_co, w_pool, pool_scale, k, v, w_xo, w_out, carry=None):
    t_len = x.shape[0]
    tm = min(256, t_len)

    def body(proj_ref, x_ref, cw_ref, wco_ref, wpool_ref, ps_ref, k_ref, v_ref, wxo_ref, wout_ref,
             a_ref, pooled_ref, ya_ref, pp_ref, yx_ref, o_ref, p_ref, x1_ref, cu_ext, up_ext):
        i = pl.program_id(0)

        @pl.when(i == 0)
        def _():
            cu_ext[0:HALO, :] = jnp.zeros((HALO, D), F32)
            up_ext[0:HALO, :] = jnp.zeros((HALO, D), F32)

        cu = proj_ref[:, O_CA:O_CA + D].astype(F32) * proj_ref[:, O_UA:O_UA + D].astype(F32)
        cu_ext[HALO:HALO + tm, :] = cu
        conv = (cw_ref[2:3, :] * cu + cw_ref[1:2, :] * cu_ext[HALO - 1:HALO - 1 + tm, :]
                + cw_ref[0:1, :] * cu_ext[HALO - 2:HALO - 2 + tm, :])
        a = (proj_ref[:, O_BA:O_BA + D].astype(F32) * conv).astype(BF16)
        a_ref[...] = a
        ya = _dot(a, wco_ref[...])
        ya_ref[...] = ya.astype(BF16)

        up_ext[HALO:HALO + tm, :] = proj_ref[:, O_UP:O_UP + D].astype(F32)
        for g, window in enumerate(POOL_WINDOWS):
            cols = slice(g * GROUP_DIM, (g + 1) * GROUP_DIM)
            tok = up_ext[HALO:HALO + tm, cols]
            acc = tok
            for j in range(1, window):
                acc = acc + up_ext[HALO - j:HALO - j + tm, cols]
            pooled = (acc * _inv_count(i, tm, window) - tok).astype(BF16)
            pooled_ref[:, cols] = pooled
            pp_ref[:, cols] = _dot(pooled, wpool_ref[g]).astype(BF16)

        for hd in range(HEADS):
            cols = slice(hd * HEAD_DIM, (hd + 1) * HEAD_DIM)
            q = proj_ref[:, O_QX + hd * HEAD_DIM:O_QX + (hd + 1) * HEAD_DIM]
            p = _softmax_rows(_dot_nt(q, k_ref[:, cols]) * ATT_SCALE).astype(BF16)
            p_ref[:, hd * N_MEM:(hd + 1) * N_MEM] = p
            o_ref[:, cols] = _dot(p, v_ref[:, cols]).astype(BF16)
        yx = _dot(o_ref[...], wxo_ref[...])
        yx_ref[...] = yx.astype(BF16)

        merged = (_sigmoid(proj_ref[:, O_GA:O_GA + D].astype(F32)) * ya
                  + _sigmoid(proj_ref[:, O_GP:O_GP + D].astype(F32)) * (pp_ref[...].astype(F32) * ps_ref[...])
                  + _sigmoid(proj_ref[:, O_GX:O_GX + D].astype(F32)) * yx)
        x1_ref[...] = x_ref[...] + _dot(merged.astype(BF16), wout_ref[...])

        cu_ext[0:HALO, :] = cu_ext[tm:tm + HALO, :]
        up_ext[0:HALO, :] = up_ext[tm:tm + HALO, :]

    act = jax.ShapeDtypeStruct((t_len, D), BF16)
    return _carried_call(
        body, "mixer_fwd", (t_len // tm,),
        in_specs=[_rows(tm, D_IN), _rows(tm, D), _resident((8, D)), _resident((D, D)),
                  _resident((GROUPS, GROUP_DIM, GROUP_DIM)), _resident((1, D)),
                  _resident((N_MEM, D)), _resident((N_MEM, D)), _resident((D, D)), _resident((D, D))],
        out_specs=[_rows(tm, D)] * 7 + [_rows(tm, D)],
        out_shape=[act] * 6 + [jax.ShapeDtypeStruct((t_len, HEADS * N_MEM), BF16), jax.ShapeDtypeStruct((t_len, D), F32)],
        scratch_shapes=[pltpu.VMEM((tm + HALO, D), F32), pltpu.VMEM((tm + HALO, D), F32)],
        semantics=("arbitrary",), args=(proj, x, conv_w8, w_co, w_pool, pool_scale, k, v, w_xo, w_out), carry=carry)


def ffn_fwd(x1, target, g_ffn, w_gate, w_up, w_down, g_final):
    t_len = x1.shape[0]
    tm = min(512, t_len)

    def body(x1_ref, tgt_ref, g_ref, wg_ref, wu_ref, wd_ref, gf_ref, gate_ref, up_ref, dx2_ref, stat_ref):
        @pl.when(pl.program_id(0) == 0)
        def _():
            stat_ref[...] = jnp.zeros((8, D), F32)

        x1v = x1_ref[...]
        r2 = lax.rsqrt(jnp.mean(x1v * x1v, axis=-1, keepdims=True) + EPS)
        h2 = (x1v * r2 * g_ref[...]).astype(BF16)
        gate = _dot_nt(h2, wg_ref[...])
        up = _dot_nt(h2, wu_ref[...])
        gate_ref[...] = gate.astype(BF16)
        up_ref[...] = up.astype(BF16)
        act = (gate * _sigmoid(gate) * up).astype(BF16)
        x2 = x1v + _dot(act, wd_ref[...])
        r3 = lax.rsqrt(jnp.mean(x2 * x2, axis=-1, keepdims=True) + EPS)
        xh = x2 * r3
        diff = xh * gf_ref[...] - tgt_ref[...]
        dy = diff * (1.0 / D)
        stat_ref[0:1, :] += jnp.sum(dy * xh, axis=0, keepdims=True)
        stat_ref[1:2, :] += (0.5 / D) * jnp.sum(diff * diff, axis=0, keepdims=True)
        dxh = dy * gf_ref[...]
        dx2_ref[...] = r3 * (dxh - xh * jnp.mean(dxh * xh, axis=-1, keepdims=True))

    return pl.pallas_call(
        body, name="ffn_fwd",
        grid=(t_len // tm,),
        in_specs=[_rows(tm, D), _rows(tm, D), _resident((1, D)), _resident((D_FF, D)), _resident((D_FF, D)),
                  _resident((D_FF, D)), _resident((1, D))],
        out_specs=[_rows(tm, D_FF), _rows(tm, D_FF), _rows(tm, D), _const((8, D))],
        out_shape=[jax.ShapeDtypeStruct((t_len, D_FF), BF16), jax.ShapeDtypeStruct((t_len, D_FF), BF16),
                   jax.ShapeDtypeStruct((t_len, D), F32), jax.ShapeDtypeStruct((8, D), F32)],
        compiler_params=_params(("arbitrary",)),
    )(x1, target, g_ffn, w_gate, w_up, w_down, g_final)


def ffn_bwd(dx2, x1, gate, up, g_ffn, w_gate, w_up, w_down):
    t_len = x1.shape[0]
    tm = min(256, t_len)

    def body(dx2_ref, x1_ref, gate_ref, up_ref, g_ref, wg_ref, wu_ref, wd_ref,
             dx1_ref, dgate_ref, dup_ref, act_ref, h2_ref, stat_ref):
        @pl.when(pl.program_id(0) == 0)
        def _():
            stat_ref[...] = jnp.zeros((8, D), F32)

        dx2v = dx2_ref[...]
        gate = gate_ref[...].astype(F32)
        upv = up_ref[...].astype(F32)
        sg = _sigmoid(gate)
        silu = gate * sg
        act_ref[...] = (silu * upv).astype(BF16)
        dact = _dot_nt(dx2v.astype(BF16), wd_ref[...])
        dup = (dact * silu).astype(BF16)
        dgate = (dact * upv * (sg * (1.0 + gate * (1.0 - sg)))).astype(BF16)
        dup_ref[...] = dup
        dgate_ref[...] = dgate
        dh2 = _dot(dgate, wg_ref[...]) + _dot(dup, wu_ref[...])
        x1v = x1_ref[...]
        r2 = lax.rsqrt(jnp.mean(x1v * x1v, axis=-1, keepdims=True) + EPS)
        xh = x1v * r2
        h2_ref[...] = (xh * g_ref[...]).astype(BF16)
        stat_ref[0:1, :] += jnp.sum(dh2 * xh, axis=0, keepdims=True)
        dxh = dh2 * g_ref[...]
        dx1_ref[...] = dx2v + r2 * (dxh - xh * jnp.mean(dxh * xh, axis=-1, keepdims=True))

    ff = jax.ShapeDtypeStruct((t_len, D_FF), BF16)
    return pl.pallas_call(
        body, name="ffn_bwd",
        grid=(t_len // tm,),
        in_specs=[_rows(tm, D), _rows(tm, D), _rows(tm, D_FF), _rows(tm, D_FF), _resident((1, D)),
                  _resident((D_FF, D)), _resident((D_FF, D)), _resident((D_FF, D))],
        out_specs=[_rows(tm, D), _rows(tm, D_FF), _rows(tm, D_FF), _rows(tm, D_FF), _rows(tm, D), _const((8, D))],
        out_shape=[jax.ShapeDtypeStruct((t_len, D), F32), ff, ff, ff, jax.ShapeDtypeStruct((t_len, D), BF16),
                   jax.ShapeDtypeStruct((8, D), F32)],
        compiler_params=_params(("arbitrary",)),
    )(dx2, x1, gate, up, g_ffn, w_gate, w_up, w_down)


def mixer_bwd(dx1, proj, ya, pp, yx, probs, conv_w8, w_co, w_pool, pool_scale, k, v, w_xo, w_out, carry=None):
    t_len = dx1.shape[0]
    tm = min(256, t_len)
    n_tiles = t_len // tm
    halo_blocks = tm // HALO

    def body(dx1_ref, proj_ref, halo_ref, ya_ref, pp_ref, yx_ref, p_ref,
             cw_ref, wco_ref, wpool_ref, ps_ref, k_ref, v_ref, wxo_ref, wout_ref, band_ref,
             dproj_ref, merged_ref, dya_ref, dpp_ref, dyx_ref, dk_ref, dv_ref, stat_ref,
             cu_ext, dconv_ext, dpn_ext):
        step = pl.program_id(0)
        tile = n_tiles - 1 - step

        @pl.when(step == 0)
        def _():
            dk_ref[...] = jnp.zeros((N_MEM, D), F32)
            dv_ref[...] = jnp.zeros((N_MEM, D), F32)
            stat_ref[...] = jnp.zeros((8, D), F32)
            dconv_ext[tm:tm + HALO, :] = jnp.zeros((HALO, D), F32)
            dpn_ext[tm:tm + POOL_PAD, :] = jnp.zeros((POOL_PAD, D), BF16)

        dmerged = _dot_nt(dx1_ref[...].astype(BF16), wout_ref[...])
        sa = _sigmoid(proj_ref[:, O_GA:O_GA + D].astype(F32))
        sp = _sigmoid(proj_ref[:, O_GP:O_GP + D].astype(F32))
        sx = _sigmoid(proj_ref[:, O_GX:O_GX + D].astype(F32))
        ya = ya_ref[...].astype(F32)
        ppv = pp_ref[...].astype(F32)
        yp = ppv * ps_ref[...]
        yx = yx_ref[...].astype(F32)
        merged_ref[...] = (sa * ya + sp * yp + sx * yx).astype(BF16)
        dproj_ref[:, O_GA:O_GA + D] = (dmerged * ya * (sa * (1.0 - sa))).astype(BF16)
        dproj_ref[:, O_GP:O_GP + D] = (dmerged * yp * (sp * (1.0 - sp))).astype(BF16)
        dproj_ref[:, O_GX:O_GX + D] = (dmerged * yx * (sx * (1.0 - sx))).astype(BF16)
        dya = (dmerged * sa).astype(BF16)
        dyp = dmerged * sp
        dyx = (dmerged * sx).astype(BF16)
        dya_ref[...] = dya
        dyx_ref[...] = dyx
        stat_ref[1:2, :] += jnp.sum(dyp * ppv, axis=0, keepdims=True)
        dpp = (dyp * ps_ref[...]).astype(BF16)
        dpp_ref[...] = dpp

        da = _dot_nt(dya, wco_ref[...])
        c_a = proj_ref[:, O_CA:O_CA + D].astype(F32)
        u_a = proj_ref[:, O_UA:O_UA + D].astype(F32)
        cu = c_a * u_a
        halo_cu = halo_ref[:, O_CA:O_CA + D].astype(F32) * halo_ref[:, O_UA:O_UA + D].astype(F32)
        cu_ext[0:HALO, :] = jnp.where(tile > 0, halo_cu, 0.0)
        cu_ext[HALO:HALO + tm, :] = cu
        cu1 = cu_ext[HALO - 1:HALO - 1 + tm, :]
        cu2 = cu_ext[HALO - 2:HALO - 2 + tm, :]
        conv = cw_ref[2:3, :] * cu + cw_ref[1:2, :] * cu1 + cw_ref[0:1, :] * cu2
        dproj_ref[:, O_BA:O_BA + D] = (da * conv).astype(BF16)
        dconv = da * proj_ref[:, O_BA:O_BA + D].astype(F32)
        stat_ref[5:6, :] += jnp.sum(dconv * cu2, axis=0, keepdims=True)
        stat_ref[6:7, :] += jnp.sum(dconv * cu1, axis=0, keepdims=True)
        stat_ref[7:8, :] += jnp.sum(dconv * cu, axis=0, keepdims=True)
        dconv_ext[0:tm, :] = dconv
        dcu = (cw_ref[2:3, :] * dconv + cw_ref[1:2, :] * dconv_ext[1:1 + tm, :]
               + cw_ref[0:1, :] * dconv_ext[2:2 + tm, :])
        dproj_ref[:, O_CA:O_CA + D] = (dcu * u_a).astype(BF16)
        dproj_ref[:, O_UA:O_UA + D] = (dcu * c_a).astype(BF16)

        for g, window in enumerate(POOL_WINDOWS):
            cols = slice(g * GROUP_DIM, (g + 1) * GROUP_DIM)
            dpooled = _dot_nt(dpp[:, cols], wpool_ref[g])
            dpn_ext[0:tm, cols] = (dpooled * _inv_count(tile, tm, window)).astype(BF16)
            acc = _dot(band_ref[g], dpn_ext[:, cols])
            dproj_ref[:, O_UP + g * GROUP_DIM:O_UP + (g + 1) * GROUP_DIM] = (acc - dpooled).astype(BF16)

        do = _dot_nt(dyx, wxo_ref[...])
        for hd in range(HEADS):
            cols = slice(hd * HEAD_DIM, (hd + 1) * HEAD_DIM)
            q = proj_ref[:, O_QX + hd * HEAD_DIM:O_QX + (hd + 1) * HEAD_DIM]
            kh = k_ref[:, cols]
            p16 = p_ref[:, hd * N_MEM:(hd + 1) * N_MEM]
            p = p16.astype(F32)
            doh = do[:, cols].astype(BF16)
            dp = _dot_nt(doh, v_ref[:, cols])
            dv_ref[:, cols] += _dot_tn(p16, doh)
            ds = (p * (dp - jnp.sum(dp * p, axis=-1, keepdims=True)) * ATT_SCALE).astype(BF16)
            dproj_ref[:, O_QX + hd * HEAD_DIM:O_QX + (hd + 1) * HEAD_DIM] = _dot(ds, kh).astype(BF16)
            dk_ref[:, cols] += _dot_tn(ds, q)

        dconv_ext[tm:tm + HALO, :] = dconv_ext[0:HALO, :]
        dpn_ext[tm:tm + HALO, :] = dpn_ext[0:HALO, :]

    def rev(width):
        return pl.BlockSpec((tm, width), lambda s: (n_tiles - 1 - s, 0))

    halo_spec = pl.BlockSpec((HALO, D_IN), lambda s: (jnp.maximum((n_tiles - 1 - s) * halo_blocks - 1, 0), 0))
    act = jax.ShapeDtypeStruct((t_len, D), BF16)
    kv_grad = jax.ShapeDtypeStruct((N_MEM, D), F32)
    return _carried_call(
        body, "mixer_bwd", (n_tiles,),
        in_specs=[rev(D), rev(D_IN), halo_spec, rev(D), rev(D), rev(D), rev(D),
                  _resident((8, D)), _resident((D, D)), _resident((GROUPS, GROUP_DIM, GROUP_DIM)), _resident((1, D)),
                  _resident((N_MEM, D)), _resident((N_MEM, D)), _resident((D, D)), _resident((D, D)),
                  _resident((GROUPS, tm, tm + POOL_PAD))],
        out_specs=[rev(D_IN), rev(D), rev(D), rev(D), rev(D),
                   _const((N_MEM, D)), _const((N_MEM, D)), _const((8, D))],
        out_shape=[jax.ShapeDtypeStruct((t_len, D_IN), BF16), act, act, act, act, kv_grad, kv_grad,
                   jax.ShapeDtypeStruct((8, D), F32)],
        scratch_shapes=[pltpu.VMEM((tm + HALO, D), F32)] * 2 + [pltpu.VMEM((tm + POOL_PAD, D), BF16)],
        semantics=("arbitrary",),
        args=(dx1, proj, proj, ya, pp, yx, probs, conv_w8, w_co, w_pool, pool_scale, k, v, w_xo, w_out,
              _window_bands(tm, False)), carry=carry)


def in_bwd(dproj, w_in, x, dx1, g_mix, carry=None):
    t_len = x.shape[0]
    tm = min(512, t_len)

    def body(dproj_ref, w_ref, x_ref, dx1_ref, g_ref, gx_ref, stat_ref):
        @pl.when(pl.program_id(0) == 0)
        def _():
            stat_ref[...] = jnp.zeros((8, D), F32)

        dh = _dot_nt(dproj_ref[...], w_ref[...])
        xv = x_ref[...]
        r = lax.rsqrt(jnp.mean(xv * xv, axis=-1, keepdims=True) + EPS)
        xh = xv * r
        stat_ref[0:1, :] += jnp.sum(dh * xh, axis=0, keepdims=True)
        dxh = dh * g_ref[...]
        gx_ref[...] = dx1_ref[...] + r * (dxh - xh * jnp.mean(dxh * xh, axis=-1, keepdims=True))

    return _carried_call(
        body, "in_bwd", (t_len // tm,),
        in_specs=[_rows(tm, D_IN), _resident((D, D_IN)), _rows(tm, D), _rows(tm, D), _resident((1, D))],
        out_specs=[_rows(tm, D), _const((8, D))],
        out_shape=[jax.ShapeDtypeStruct((t_len, D), F32), jax.ShapeDtypeStruct((8, D), F32)],
        scratch_shapes=[], semantics=("arbitrary",), args=(dproj, w_in, x, dx1, g_mix), carry=carry)


def kv_bwd(dk, dv, memn, mem, g_mem, w_kv4):
    half = D // 2

    def body(dk_ref, dv_ref, memn_ref, mem_ref, g_ref, w_ref, gw_ref, gw16_ref, stat_ref):
        mn = memn_ref[...]
        parts = (dk_ref[:, 0:half], dk_ref[:, half:D], dv_ref[:, 0:half], dv_ref[:, half:D])
        dmemn = jnp.zeros((N_MEM, D), F32)
        for j, part in enumerate(parts):
            part = part.astype(BF16)
            gw = _dot_tn(mn, part)
            gw_ref[j] = gw
            gw16_ref[j] = gw.astype(BF16)
            dmemn = dmemn + _dot_nt(part, w_ref[j])
        mv = mem_ref[...]
        r = lax.rsqrt(jnp.mean(mv * mv, axis=-1, keepdims=True) + EPS)
        stat_ref[...] = jnp.zeros((8, D), F32)
        stat_ref[0:1, :] = jnp.sum(dmemn * (mv * r), axis=0, keepdims=True)

    return pl.pallas_call(
        body, name="kv_bwd",
        out_shape=[jax.ShapeDtypeStruct((N_CHIPS, D, half), F32), jax.ShapeDtypeStruct((N_CHIPS, D, half), BF16),
                   jax.ShapeDtypeStruct((8, D), F32)],
        compiler_params=_params(),
    )(dk, dv, memn, mem, g_mem, w_kv4)


def matmul_tn(name, a, b, tn, col_blocks=1, carry=None):
    t_len, k_dim = a.shape
    n_dim = b.shape[1]
    tt = min(1024, t_len)
    per_block = n_dim // col_blocks // tn

    def body(a_ref, b_ref, out_ref, out16_ref):
        @pl.when(pl.program_id(1) == 0)
        def _():
            out_ref[...] = jnp.zeros((k_dim, tn), F32)

        out_ref[...] += _dot_tn(a_ref[...].astype(BF16), b_ref[...].astype(BF16))

        @pl.when(pl.program_id(1) == t_len // tt - 1)
        def _():
            out16_ref[...] = out_ref[...].astype(BF16)

    if col_blocks == 1:
        out_spec = pl.BlockSpec((k_dim, tn), lambda n, t: (0, n))
        shape = (k_dim, n_dim)
    else:
        out_spec = pl.BlockSpec((None, k_dim, tn), lambda n, t: (n // per_block, 0, n % per_block))
        shape = (col_blocks, k_dim, n_dim // col_blocks)
    outs, carried = _carried_call(
        body, name, (n_dim // tn, t_len // tt),
        in_specs=[pl.BlockSpec((tt, k_dim), lambda n, t: (t, 0)), pl.BlockSpec((tt, tn), lambda n, t: (t, n))],
        out_specs=[out_spec, out_spec], out_shape=[jax.ShapeDtypeStruct(shape, F32), jax.ShapeDtypeStruct(shape, BF16)],
        scratch_shapes=[], semantics=("arbitrary", "arbitrary"), args=(a, b), carry=carry)
    return (tuple(outs), carried) if carry is not None else tuple(outs)


def grad_w_pool(pooled, dpp):
    t_len = pooled.shape[0]
    tt = min(1024, t_len)
    steps = t_len // tt

    def body(a_ref, b_ref, out_ref, out16_ref):
        @pl.when(pl.program_id(0) == 0)
        def _():
            out_ref[...] = jnp.zeros((GROUPS, GROUP_DIM, GROUP_DIM), F32)

        for g in range(GROUPS):
            cols = slice(g * GROUP_DIM, (g + 1) * GROUP_DIM)
            out_ref[g] += _dot_tn(a_ref[:, cols], b_ref[:, cols])

        @pl.when(pl.program_id(0) == steps - 1)
        def _():
            out16_ref[...] = out_ref[...].astype(BF16)

    shape = (GROUPS, GROUP_DIM, GROUP_DIM)
    return pl.pallas_call(
        body, name="grad_w_pool", grid=(steps,),
        in_specs=[_rows(tt, D), _rows(tt, D)], out_specs=[_const(shape), _const(shape)],
        out_shape=[jax.ShapeDtypeStruct(shape, F32), jax.ShapeDtypeStruct(shape, BF16)],
        compiler_params=_params(("arbitrary",)),
    )(pooled, dpp)


def _place():
    x, y, c = lax.axis_index("x"), lax.axis_index("y"), lax.axis_index("c")
    return x, y, c


def _other_chips(x, y):
    return [(1 - x, y), (x, 1 - y), (1 - x, 1 - y)]


def _any_specs(n):
    return [pl.BlockSpec(memory_space=pl.ANY)] * n


def gather_weights(shards, gather_cls):
    n = len(shards)

    def body(*refs):
        gather = gather_cls(refs[:n], refs[n:2 * n], *refs[2 * n:])
        gather.start()
        gather.middle()
        gather.finish()

    return pl.pallas_call(
        body, name="gather_weights",
        in_specs=_any_specs(n), out_specs=_any_specs(n),
        out_shape=gather_cls.out_shapes(shards), scratch_shapes=gather_cls.sems(n),
    )(*shards)


class _Gather:
    SLOTS = 8
    ROW_ALIGN = 16

    @staticmethod
    def out_shapes(shards):
        return [jax.ShapeDtypeStruct((N_CHIPS,) + s.shape, s.dtype) for s in shards]

    @staticmethod
    def sems(n):
        return [pltpu.SemaphoreType.DMA((n, _Gather.SLOTS)), pltpu.SemaphoreType.DMA((n, _Gather.SLOTS))]

    def __init__(self, ins, outs, send_sems, recv_sems):
        self.ins, self.outs, self.send_sems, self.recv_sems = ins, outs, send_sems, recv_sems
        x, y, c = _place()
        self.c, self.me, self.sibling = c, 2 * x + y, (x, y, 1 - c)
        self.across = [(1 - x, y), (x, 1 - y), (1 - x, 1 - y)]

    def _rows(self, a, which, part=None):
        half = self.ins[a].shape[1] // 2
        first = (half // 2) // self.ROW_ALIGN * self.ROW_ALIGN
        if part is None:
            return pl.ds(which * half, half)
        return pl.ds(which * half, first) if part == 0 else pl.ds(which * half + first, half - first)

    def _has_part(self, a, part):
        half = self.ins[a].shape[1] // 2
        return part == 1 or (half // 2) // self.ROW_ALIGN > 0

    def _block(self, a, chip, rows=slice(None)):
        index = chip if not isinstance(chip, tuple) else 2 * chip[0] + chip[1]
        if len(self.outs[a].shape) == len(self.ins[a].shape):
            cols = self.ins[a].shape[2]
            return self.outs[a].at[:, rows, pl.ds(pl.multiple_of(index * cols, cols), cols)]
        return self.outs[a].at[index, :, rows, :]

    def _remote(self, src, dst, a, slot, to):
        return pltpu.make_async_remote_copy(src_ref=src, dst_ref=dst, send_sem=self.send_sems.at[a, slot],
                                            recv_sem=self.recv_sems.at[a, slot], device_id=to, device_id_type=MESH)

    def _own(self, a):
        return self._remote(self.ins[a], self._block(a, self.me), a, 6, self.sibling)

    def _sent(self, a, axis):
        rows = self._rows(a, self.c)
        return self._remote(self.ins[a].at[:, rows, :], self._block(a, self.me, rows), a, axis,
                            (*self.across[axis], self.c))

    def _landed(self, a, axis):
        block = self._block(a, self.across[axis], self._rows(a, self.c))
        return self._remote(block, block, a, axis, (*self.across[axis], self.c))

    def _relayed(self, a, part, incoming):
        source = self.across[2] if incoming else self.across[part]
        block = self._block(a, source, self._rows(a, self.c, part))
        return self._remote(block, block, a, (2, 7)[part], (*self.across[1 - part], self.c))

    def _passed_on(self, a, source, which):
        block = self._block(a, self.across[source], self._rows(a, which))
        return self._remote(block, block, a, 3 + source, self.sibling)

    def start(self):
        for a in range(len(self.ins)):
            self._own(a).start()
        for a in range(len(self.ins)):
            for axis in range(2):
                self._sent(a, axis).start()

    def middle(self):
        for a in range(len(self.ins)):
            for axis in range(2):
                self._landed(a, axis).wait_recv()
                if self._has_part(a, axis):
                    self._relayed(a, axis, incoming=False).start()
                self._passed_on(a, axis, self.c).start()

    def finish(self):
        n = len(self.ins)
        for a in range(n):
            for part in range(2):
                if self._has_part(a, part):
                    self._relayed(a, part, incoming=True).wait_recv()
            self._passed_on(a, 2, self.c).start()
        for a in range(n):
            for source in range(3):
                self._passed_on(a, source, 1 - self.c).wait_recv()
            self._own(a).wait_recv()
        for a in range(n):
            self._own(a).wait_send()
            for axis in range(2):
                self._sent(a, axis).wait_send()
                if self._has_part(a, axis):
                    self._relayed(a, axis, incoming=False).wait_send()
            for source in range(3):
                self._passed_on(a, source, self.c).wait_send()


class _HalfExchange:
    @staticmethod
    def out_shapes(grads):
        return [jax.ShapeDtypeStruct((N_CHIPS, g.shape[1] // 2, g.shape[2]), g.dtype) for g in grads]

    @staticmethod
    def sems(n):
        return [pltpu.SemaphoreType.DMA((n,)), pltpu.SemaphoreType.DMA((n,))]

    def __init__(self, ins, outs, send_sems, recv_sems):
        self.ins, self.outs, self.send_sems, self.recv_sems = ins, outs, send_sems, recv_sems

    def _copies(self):
        x, y, c = _place()
        for a in range(len(self.ins)):
            h = self.ins[a].shape[1] // 2
            yield pltpu.make_async_remote_copy(
                src_ref=self.ins[a].at[:, pl.ds((1 - c) * h, h), :], dst_ref=self.outs[a],
                send_sem=self.send_sems.at[a], recv_sem=self.recv_sems.at[a], device_id=(x, y, 1 - c), device_id_type=MESH)

    def start(self):
        for cp in self._copies():
            cp.start()

    def finish(self):
        for cp in self._copies():
            cp.wait()


class _GatherColumns(_Gather):
    @staticmethod
    def out_shapes(shards):
        return [jax.ShapeDtypeStruct(s.shape[:2] + (N_CHIPS * s.shape[2],), s.dtype) for s in shards]


def exchange_sibling_halves(name, grads):
    n = len(grads)

    def body(*refs):
        exchange = _HalfExchange(refs[:n], refs[n:2 * n], *refs[2 * n:])
        exchange.start()
        exchange.finish()

    return pl.pallas_call(
        body, name=name,
        in_specs=_any_specs(n), out_specs=_any_specs(n),
        out_shape=_HalfExchange.out_shapes(grads), scratch_shapes=_HalfExchange.sems(n),
    )(*grads)


def _row_tile(rows, cols, budget=1 << 20):
    best = 16
    for tr in range(16, rows + 1, 16):
        if rows % tr == 0 and tr * cols * 4 <= budget:
            best = tr
    return best


def add_sibling_half(name, grad, got, core):
    _, rows, cols = grad.shape
    h = rows // 2
    tr = _row_tile(h, cols)
    per_half = h // tr

    def body(core_ref, g_ref, o_ref, out_ref, out16_ref):
        total = g_ref[...] + o_ref[...].astype(F32)
        out_ref[...] = total
        out16_ref[...] = total.astype(BF16)

    out_spec = pl.BlockSpec((None, tr, cols), lambda j, r, cr: (j, r, 0))
    return pl.pallas_call(
        body, name=name,
        grid_spec=pltpu.PrefetchScalarGridSpec(
            num_scalar_prefetch=1, grid=(N_CHIPS, per_half),
            in_specs=[pl.BlockSpec((None, tr, cols), lambda j, r, cr: (j, cr[0] * per_half + r, 0)),
                      pl.BlockSpec((None, tr, cols), lambda j, r, cr: (j, r, 0))],
            out_specs=[out_spec, out_spec]),
        out_shape=[jax.ShapeDtypeStruct((N_CHIPS, h, cols), F32), jax.ShapeDtypeStruct((N_CHIPS, h, cols), BF16)],
        compiler_params=_params(("arbitrary", "arbitrary")),
    )(core, grad, got)


class _ChipExchange:
    SLOTS = 6
    ROW_ALIGN = 16

    @staticmethod
    def out_shapes(partials):
        return ([jax.ShapeDtypeStruct((3,) + p.shape[1:], p.dtype) for p in partials]
                + [jax.ShapeDtypeStruct(p.shape[1:], p.dtype) for p in partials])

    @staticmethod
    def sems(n):
        return [pltpu.SemaphoreType.DMA((n, _ChipExchange.SLOTS)), pltpu.SemaphoreType.DMA((n, _ChipExchange.SLOTS))]

    def __init__(self, ins, outs, send_sems, recv_sems):
        n = len(ins)
        self.ins, self.outs, self.relays, self.send_sems, self.recv_sems = ins, outs[:n], outs[n:], send_sems, recv_sems

    def _copy(self, a, slot):
        x, y, c = _place()
        across = _other_chips(x, y)
        rows = self.ins[a].shape[1]
        first = (rows // 2) // self.ROW_ALIGN * self.ROW_ALIGN
        part = (pl.ds(0, first), pl.ds(first, rows - first))
        if slot < 2:
            px, py = across[slot]
            src, dst, to = self.ins[a].at[2 * px + py], self.outs[a].at[slot], across[slot]
        elif slot < 4:
            px, py = across[2]
            src, dst, to = self.ins[a].at[2 * px + py, part[slot - 2], :], self.relays[a].at[part[slot - 2], :], across[slot - 2]
        else:
            src, dst, to = self.relays[a].at[part[slot - 4], :], self.outs[a].at[2, part[slot - 4], :], across[5 - slot]
        return pltpu.make_async_remote_copy(src_ref=src, dst_ref=dst, send_sem=self.send_sems.at[a, slot],
                                            recv_sem=self.recv_sems.at[a, slot], device_id=(*to, c), device_id_type=MESH)

    def start(self):
        for a in range(len(self.ins)):
            for slot in range(4):
                self._copy(a, slot).start()

    def middle(self):
        for a in range(len(self.ins)):
            for part in range(2):
                self._copy(a, 2 + part).wait_recv()
                self._copy(a, 4 + part).start()

    def finish(self):
        for a in range(len(self.ins)):
            for slot in (0, 1, 4, 5):
                self._copy(a, slot).wait_recv()
        for a in range(len(self.ins)):
            for slot in range(self.SLOTS):
                self._copy(a, slot).wait_send()


def add_chip_blocks(name, partial, got, chip):
    _, h, cols = partial.shape
    tr = _row_tile(h, cols)

    def body(chip_ref, p_ref, g0_ref, g1_ref, g2_ref, out_ref):
        out_ref[...] = ((p_ref[...] + g0_ref[...].astype(F32)) + g1_ref[...].astype(F32)) + g2_ref[...].astype(F32)

    def got_spec(slot):
        return pl.BlockSpec((None, tr, cols), lambda r, ch: (slot, r, 0))

    return pl.pallas_call(
        body, name=name,
        grid_spec=pltpu.PrefetchScalarGridSpec(
            num_scalar_prefetch=1, grid=(h // tr,),
            in_specs=[pl.BlockSpec((None, tr, cols), lambda r, ch: (ch[0], r, 0)), got_spec(0), got_spec(1), got_spec(2)],
            out_specs=pl.BlockSpec((tr, cols), lambda r, ch: (r, 0))),
        out_shape=jax.ShapeDtypeStruct((h, cols), F32),
        compiler_params=_params(("arbitrary",)),
    )(chip, partial, got, got, got)


class _SiblingSwap:
    @staticmethod
    def out_shapes(halves):
        return [jax.ShapeDtypeStruct(v.shape, v.dtype) for v in halves]

    @staticmethod
    def sems(n):
        return [pltpu.SemaphoreType.DMA((n,)), pltpu.SemaphoreType.DMA((n,))]

    def __init__(self, ins, outs, send_sems, recv_sems):
        self.ins, self.outs, self.send_sems, self.recv_sems = ins, outs, send_sems, recv_sems

    def _copies(self):
        x, y, c = _place()
        for a in range(len(self.ins)):
            yield pltpu.make_async_remote_copy(
                src_ref=self.ins[a], dst_ref=self.outs[a], send_sem=self.send_sems.at[a], recv_sem=self.recv_sems.at[a],
                device_id=(x, y, 1 - c), device_id_type=MESH)

    def start(self):
        for cp in self._copies():
            cp.start()

    def finish(self):
        for cp in self._copies():
            cp.wait()


def swap_sibling_halves(halves):
    n = len(halves)

    def body(*refs):
        swap = _SiblingSwap(refs[:n], refs[n:2 * n], *refs[2 * n:])
        swap.start()
        swap.finish()

    return pl.pallas_call(
        body, name="swap_sibling_halves",
        in_specs=_any_specs(n), out_specs=_any_specs(n),
        out_shape=_SiblingSwap.out_shapes(halves), scratch_shapes=_SiblingSwap.sems(n),
    )(*halves)


def all_reduce_small(pack):
    rows = pack.shape[0]

    def body(pack_ref, out_ref, gathered, send_sems, recv_sems):
        x, y, c = _place()
        me = 4 * x + 2 * y + c
        gathered[me] = pack_ref[...]
        copies = []
        for rel in range(1, 8):
            fx, fy, fc = (rel >> 2) & 1, (rel >> 1) & 1, rel & 1
            peer = (x ^ fx, y ^ fy, c ^ fc)
            cp = pltpu.make_async_remote_copy(
                src_ref=pack_ref, dst_ref=gathered.at[me], send_sem=send_sems.at[rel - 1], recv_sem=recv_sems.at[rel - 1],
                device_id=peer, device_id_type=MESH)
            cp.start()
            copies.append(cp)
        for rel in range(1, 8):
            fx, fy, fc = (rel >> 2) & 1, (rel >> 1) & 1, rel & 1
            src = 4 * (x ^ fx) + 2 * (y ^ fy) + (c ^ fc)
            pltpu.make_async_remote_copy(
                src_ref=pack_ref, dst_ref=gathered.at[src], send_sem=send_sems.at[rel - 1], recv_sem=recv_sems.at[rel - 1],
                device_id=(x, y, c), device_id_type=MESH).wait_recv()
        for cp in copies:
            cp.wait_send()
        total = gathered[0]
        for dev in range(1, 8):
            total = total + gathered[dev]
        out_ref[...] = total

    return pl.pallas_call(
        body, name="all_reduce_small",
        in_specs=[pl.BlockSpec(memory_space=pltpu.VMEM)], out_specs=pl.BlockSpec(memory_space=pltpu.VMEM),
        out_shape=jax.ShapeDtypeStruct((rows, D), F32),
        scratch_shapes=[pltpu.VMEM((8, rows, D), F32), pltpu.SemaphoreType.DMA((7,)), pltpu.SemaphoreType.DMA((7,))],
    )(pack)


def _adamw_update(w, g, m, v):
    nm = ADAM_B1 * m + (1.0 - ADAM_B1) * g
    nv = ADAM_B2 * v + (1.0 - ADAM_B2) * (g * g)
    m_hat = nm / (1.0 - ADAM_B1 ** ADAM_STEP)
    v_hat = nv / (1.0 - ADAM_B2 ** ADAM_STEP)
    delta = -ADAM_LR * (m_hat / (jnp.sqrt(v_hat) + ADAM_EPS) + ADAM_WD * w)
    return delta, nm, nv


def adamw(name, w, g, m, v):
    def body(w_ref, g_ref, m_ref, v_ref, d_ref, nm_ref, nv_ref):
        d_ref[...], nm_ref[...], nv_ref[...] = _adamw_update(w_ref[...], g_ref[...], m_ref[...], v_ref[...])

    out = jax.ShapeDtypeStruct(w.shape, F32)
    return pl.pallas_call(body, name=name, out_shape=[out] * 3, compiler_params=_params())(w, g, m, v)


def adamw_halves(name, w, mine, theirs, m, v, core):
    h, cols = mine.shape
    tr = _row_tile(h, cols)
    per_half = h // tr

    def body(core_ref, w_ref, mine_ref, theirs_ref, m_ref, v_ref, g_ref, d_ref, nm_ref, nv_ref):
        g = jnp.where(pl.program_id(0) == core_ref[0], mine_ref[...], theirs_ref[...])
        g_ref[...] = g
        d_ref[...], nm_ref[...], nv_ref[...] = _adamw_update(w_ref[...], g, m_ref[...], v_ref[...])

    full = pl.BlockSpec((tr, cols), lambda hh, r, cr: (hh * per_half + r, 0))
    mine_spec = pl.BlockSpec((tr, cols), lambda hh, r, cr: (jnp.where(hh == cr[0], r, 0), 0))
    theirs_spec = pl.BlockSpec((tr, cols), lambda hh, r, cr: (jnp.where(hh == cr[0], 0, r), 0))
    out = jax.ShapeDtypeStruct((2 * h, cols), F32)
    return pl.pallas_call(
        body, name=name,
        grid_spec=pltpu.PrefetchScalarGridSpec(
            num_scalar_prefetch=1, grid=(2, per_half),
            in_specs=[full, mine_spec, theirs_spec, full, full], out_specs=[full] * 4),
        out_shape=[out] * 4,
        compiler_params=_params(("arbitrary", "arbitrary")),
    )(core, w, mine, theirs, m, v)


BIG = ("w_in", "w_conv_out", "w_pool", "w_kv", "w_xattn_out", "w_out", "w_gate", "w_up", "w_down")


def kernel(x, mem, norm_mix, w_in, conv_w, w_conv_out, w_pool, pool_scale, norm_mem, w_kv, w_xattn_out, w_out, norm_ffn, w_gate, w_up, w_down, norm_final, loss_target, m_norm_mix, m_w_in, m_conv_w, m_w_conv_out, m_w_pool, m_pool_scale, m_norm_mem, m_w_kv, m_w_xattn_out, m_w_out, m_norm_ffn, m_w_gate, m_w_up, m_w_down, m_norm_final, v_norm_mix, v_w_in, v_conv_w, v_w_conv_out, v_w_pool, v_pool_scale, v_norm_mem, v_w_kv, v_w_xattn_out, v_w_out, v_norm_ffn, v_w_gate, v_w_up, v_w_down, v_norm_final):
    t_len = x.shape[1]
    xi, yi, ci = lax.axis_index("x"), lax.axis_index("y"), lax.axis_index("c")
    chip = 2 * xi + yi
    chip_arr = jnp.reshape(chip, (1,)).astype(jnp.int32)
    core_arr = jnp.reshape(ci, (1,)).astype(jnp.int32)

    conv_pad = jnp.concatenate([conv_w, jnp.zeros((1, 13, 256), F32)], axis=1)
    def t2(w):
        return jnp.swapaxes(w, 1, 2)

    (g_in,) = gather_weights([w_in.astype(BF16)], _GatherColumns)
    w_in_f = g_in[0]

    x2d = x[0]
    tgt = loss_target[0]
    (proj, h), (g_kv, g_conv_w, g_co, g_xo, g_out, g_pool, g_gate) = proj_fwd(
        x2d, norm_mix, w_in_f,
        carry=(_Gather, [w_kv.astype(BF16), conv_pad,
                         w_conv_out.astype(BF16), w_xattn_out.astype(BF16), w_out.astype(BF16),
                         w_pool[0].astype(BF16),
                         t2(w_gate).astype(BF16)]))
    w_kv4 = g_kv.reshape(N_CHIPS, D, D // 2)
    conv_full = jnp.transpose(g_conv_w[:, 0, 0:8, :], (1, 0, 2)).reshape(8, D)
    w_co_f, w_xo_f, w_out_f = g_co.reshape(D, D), g_xo.reshape(D, D), g_out.reshape(D, D)
    w_pool_f = jnp.transpose(g_pool, (1, 0, 2, 3)).reshape(GROUPS, GROUP_DIM, GROUP_DIM)
    memn, k, v = kv_fwd(mem[0], norm_mem, w_kv4)
    (a, pooled, ya, pp, yx, o, probs, x1), (g_up, g_down) = mixer_fwd(
        proj, x2d, conv_full, w_co_f, w_pool_f, pool_scale, k, v, w_xo_f, w_out_f,
        carry=(_Gather, [t2(w_up).astype(BF16), w_down.astype(BF16)]))
    w_gate_f, w_up_f, w_down_f = g_gate.reshape(D_FF, D), g_up.reshape(D_FF, D), g_down.reshape(D_FF, D)
    gate, up, dx2, stat_f = ffn_fwd(x1, tgt, norm_ffn, w_gate_f, w_up_f, w_down_f, norm_final.reshape(1, D))

    def by_chip(pair):
        return tuple(gw.reshape(N_CHIPS, gw.shape[0] // N_CHIPS, gw.shape[1]) for gw in pair)

    def chip_partials(names, grads, got):
        return [add_sibling_half("add_sibling_" + n, g32, o_, core_arr) for n, (g32, _), o_ in zip(names, grads, got)]

    def chip_sums(names, partials, got):
        return [add_chip_blocks("add_chips_" + n, p32, g2, chip_arr) for n, (p32, _), g2 in zip(names, partials, got)]

    dx1, dgate, dup, act, h2, stat_b1 = ffn_bwd(dx2, x1, gate, up, norm_ffn, w_gate_f, w_up_f, w_down_f)
    gw_gate = by_chip(matmul_tn("grad_w_gate", dgate, h2, 512))
    gw_up, got_gate = matmul_tn("grad_w_up", dup, h2, 512, carry=(_HalfExchange, [gw_gate[1]]))
    gw_up = by_chip(gw_up)
    gw_down, got_up = matmul_tn("grad_w_down", act, dx2, 512, carry=(_HalfExchange, [gw_up[1]]))
    gw_down = by_chip(gw_down)
    got_down = exchange_sibling_halves("exchange_sibling_halves_ffn", [gw_down[1]])
    names_ffn = ("w_gate", "w_up", "w_down")
    part_ffn = chip_partials(names_ffn, [gw_gate, gw_up, gw_down], list(got_gate) + list(got_up) + list(got_down))

    (dproj, merged, dya, dpp, dyx, dk, dv, stat_b2), got_ffn = mixer_bwd(
        dx1, proj, ya, pp, yx, probs, conv_full, w_co_f, w_pool_f, pool_scale, k, v, w_xo_f, w_out_f,
        carry=(_ChipExchange, [p16 for _, p16 in part_ffn]))
    gw_kv32, gw_kv16, stat_kv = kv_bwd(dk, dv, memn, mem[0], norm_mem, w_kv4)
    gw_pool = tuple(jnp.transpose(gw.reshape(GROUPS, N_CHIPS, 64, GROUP_DIM), (1, 0, 2, 3)).reshape(N_CHIPS, 256, GROUP_DIM)
                    for gw in grad_w_pool(pooled, dpp))
    gw_co, got_kv_pool = matmul_tn("grad_w_conv_out", a, dya, 1024, carry=(_HalfExchange, [gw_kv16, gw_pool[1]]))
    gw_co = by_chip(gw_co)
    gw_xo, got_co = matmul_tn("grad_w_xattn_out", o, dyx, 1024, carry=(_HalfExchange, [gw_co[1]]))
    gw_xo = by_chip(gw_xo)
    gw_out, got_xo = matmul_tn("grad_w_out", merged, dx1, 1024, carry=(_HalfExchange, [gw_xo[1]]))
    gw_out = by_chip(gw_out)
    got_out = exchange_sibling_halves("exchange_sibling_halves_mixer", [gw_out[1]])
    names_mix = ("w_kv", "w_pool", "w_conv_out", "w_xattn_out", "w_out")
    part_mix = chip_partials(names_mix, [(gw_kv32, gw_kv16), gw_pool, gw_co, gw_xo, gw_out],
                             list(got_kv_pool) + list(got_co) + list(got_xo) + list(got_out))

    gw_in, got_mix = matmul_tn("grad_w_in", h, dproj, 2048, col_blocks=N_CHIPS,
                               carry=(_ChipExchange, [p16 for _, p16 in part_mix]))
    part_in = chip_partials(("w_in",), [gw_in], exchange_sibling_halves("exchange_sibling_halves_in", [gw_in[1]]))
    mine_early = chip_sums(names_ffn + names_mix, part_ffn + part_mix, list(got_ffn) + list(got_mix))
    (grad_x, stat_b3), (got_in, theirs_early) = in_bwd(
        dproj, w_in_f, x2d, dx1, norm_mix,
        carry=[(_ChipExchange, [p16 for _, p16 in part_in]), (_SiblingSwap, mine_early)])
    mine_in = chip_sums(("w_in",), part_in, got_in)
    theirs_in = swap_sibling_halves(mine_in)
    reduced = dict(zip(names_ffn + names_mix + ("w_in",),
                       zip(mine_early + mine_in, list(theirs_early) + list(theirs_in))))
    mine = [reduced[n][0] for n in BIG]
    theirs = [reduced[n][1] for n in BIG]

    pack = jnp.concatenate([stat_b3[0:1], stat_b2[1:2], stat_kv[0:1], stat_b1[0:1], stat_f[0:1], stat_b2[5:8],
                            stat_f[1:2], jnp.zeros((7, D), F32)], axis=0)
    total = all_reduce_small(pack)
    loss = jnp.sum(total[8])
    g_conv_full = total[5:8]
    g_conv = lax.dynamic_slice_in_dim(g_conv_full, chip * 256, 256, axis=1)

    given = dict(w_in=(w_in, m_w_in, v_w_in), w_conv_out=(w_conv_out, m_w_conv_out, v_w_conv_out),
                 w_pool=(w_pool, m_w_pool, v_w_pool), w_kv=(w_kv, m_w_kv, v_w_kv),
                 w_xattn_out=(w_xattn_out, m_w_xattn_out, v_w_xattn_out), w_out=(w_out, m_w_out, v_w_out),
                 w_gate=(w_gate, m_w_gate, v_w_gate), w_up=(w_up, m_w_up, v_w_up), w_down=(w_down, m_w_down, v_w_down))
    out_g, out_d, out_m, out_v = {}, {}, {}, {}
    for n, mine_n, theirs_n in zip(BIG, mine, theirs):
        transposed = n in ("w_gate", "w_up")
        rows2d = (2 * mine_n.shape[0], mine_n.shape[1])
        w_, m_, v_ = ((t2(t) if transposed else t).reshape(rows2d) for t in given[n])
        res = adamw_halves("adamw_" + n, w_, mine_n, theirs_n, m_, v_, core_arr)
        if transposed:
            res = [t2(t.reshape(1, D_FF // N_CHIPS, D)) for t in res]
        out_g[n], out_d[n], out_m[n], out_v[n] = (t.reshape(given[n][0].shape) for t in res)

    def small_pack(vals, conv_part):
        conv_rows = jnp.concatenate([conv_part.reshape(3, 256), jnp.zeros((3, D - 256), F32)], axis=1)
        return jnp.concatenate([val.reshape(1, D) for val in vals] + [conv_rows], axis=0)

    sw = small_pack([norm_mix, pool_scale, norm_mem, norm_ffn, norm_final], conv_w)
    sm = small_pack([m_norm_mix, m_pool_scale, m_norm_mem, m_norm_ffn, m_norm_final], m_conv_w)
    sv = small_pack([v_norm_mix, v_pool_scale, v_norm_mem, v_norm_ffn, v_norm_final], v_conv_w)
    sg = small_pack([total[r] for r in range(5)], g_conv)
    sd, snm, snv = adamw("adamw_small", sw, sg, sm, sv)
    small_names = ("norm_mix", "pool_scale", "norm_mem", "norm_ffn", "norm_final")
    small_shapes = dict(norm_mix=(1, D), pool_scale=(1, D), norm_mem=(1, D), norm_ffn=(1, D), norm_final=(D,))
    for r, n in enumerate(small_names):
        out_g[n], out_d[n], out_m[n], out_v[n] = (t[r].reshape(small_shapes[n]) for t in (sg, sd, snm, snv))
    out_g["conv_w"], out_d["conv_w"], out_m["conv_w"], out_v["conv_w"] = (
        t[5:8, 0:256].reshape(1, 3, 256) for t in (sg, sd, snm, snv))

    order = ("norm_mix", "w_in", "conv_w", "w_conv_out", "w_pool", "pool_scale", "norm_mem", "w_kv", "w_xattn_out",
             "w_out", "norm_ffn", "w_gate", "w_up", "w_down", "norm_final")
    return (loss, grad_x.reshape(1, t_len, D), *[out_g[n] for n in order], *[out_d[n] for n in order],
            *[out_m[n] for n in order], *[out_v[n] for n in order])
```

```python
import functools

import jax
import jax.numpy as jnp
from jax import lax
from jax.experimental import pallas as pl
from jax.experimental.pallas import tpu as pltpu

F32 = jnp.float32
BF16 = jnp.bfloat16
MESH = pl.DeviceIdType.MESH

D = 1024
N_MEM = 256
HEADS = 4
HEAD_DIM = 256
GROUPS = 4
GROUP_DIM = 256
POOL_WINDOWS = (2, 4, 8, 16)
D_FF = 2816
D_IN = 8192
N_CHIPS = 4
EPS = 1e-6
HALO = 16
POOL_PAD = 128
ATT_SCALE = HEAD_DIM ** -0.5

ADAM_LR = 0.001
ADAM_B1 = 0.9
ADAM_B2 = 0.999
ADAM_EPS = 1e-08
ADAM_WD = 0.01
ADAM_STEP = 10

VMEM_LIMIT = 56 * 1024 * 1024

O_BA, O_CA, O_UA, O_UP, O_QX, O_GA, O_GP, O_GX = (k * D for k in range(8))

NT_DIMS = (((1,), (1,)), ((), ()))
TN_DIMS = (((0,), (0,)), ((), ()))


def _dot(a, b):
    return jnp.dot(a, b, preferred_element_type=F32)


def _dot_nt(a, b):
    return lax.dot_general(a, b, NT_DIMS, preferred_element_type=F32)


def _dot_tn(a, b):
    return lax.dot_general(a, b, TN_DIMS, preferred_element_type=F32)


def _sigmoid(z):
    return pl.reciprocal(1.0 + jnp.exp(-z), approx=True)


def _params(semantics=None):
    return pltpu.CompilerParams(dimension_semantics=semantics, vmem_limit_bytes=VMEM_LIMIT)


def _resident(shape):
    zeros = (0,) * len(shape)
    return pl.BlockSpec(shape, lambda *_: zeros, pipeline_mode=pl.Buffered(1))


def _const(shape):
    zeros = (0,) * len(shape)
    return pl.BlockSpec(shape, lambda *_: zeros)


def _rows(tm, width):
    return pl.BlockSpec((tm, width), lambda i: (i, 0))


def _inv_count(tile, tm, window):
    t = tile * tm + lax.broadcasted_iota(jnp.int32, (tm, 1), 0)
    return 1.0 / jnp.minimum(t + 1, window).astype(F32)


def _carried_call(body, name, grid, in_specs, out_specs, out_shape, scratch_shapes, semantics, args, carry):
    if carry is None:
        res = pl.pallas_call(body, name=name, grid=grid, in_specs=in_specs, out_specs=out_specs, out_shape=out_shape,
                             scratch_shapes=scratch_shapes, compiler_params=_params(semantics))(*args)
        return res, []
    carries = [carry] if isinstance(carry, tuple) else list(carry)
    comm_args = [arr for _, arrs in carries for arr in arrs]
    n, n_in, n_out, n_scratch = len(comm_args), len(in_specs), len(out_specs), len(scratch_shapes)
    shapes_of = [cls.out_shapes(arrs) for cls, arrs in carries]
    comm_shapes = [s for shapes in shapes_of for s in shapes]
    m = len(comm_shapes)
    comm_sems = [s for cls, arrs in carries for s in cls.sems(len(arrs))]

    def carrying(*refs):
        ins, comm_ins = refs[:n_in], refs[n_in:n_in + n]
        outs, comm_outs = refs[n_in + n:n_in + n + n_out], refs[n_in + n + n_out:n_in + n + n_out + m]
        scratch, sems = refs[n_in + n + n_out + m:n_in + n + n_out + m + n_scratch], refs[n_in + n + n_out + m + n_scratch:]
        steps = [pl.program_id(d) for d in range(len(grid))]
        first = functools.reduce(jnp.logical_and, [s == 0 for s in steps])
        last = functools.reduce(jnp.logical_and, [s == g - 1 for s, g in zip(steps, grid)])

        def exchanges():
            at_in = at_out = 0
            for k, (cls, arrs) in enumerate(carries):
                yield cls(comm_ins[at_in:at_in + len(arrs)], comm_outs[at_out:at_out + len(shapes_of[k])],
                          sems[2 * k], sems[2 * k + 1])
                at_in, at_out = at_in + len(arrs), at_out + len(shapes_of[k])

        @pl.when(first)
        def _():
            for exchange in exchanges():
                exchange.start()

        linear, total = 0, 1
        for s, g in zip(steps, grid):
            linear, total = linear * g + s, total * g
        for k, (cls, _) in enumerate(carries):
            if hasattr(cls, "middle"):
                @pl.when(linear == (cls.MIDDLE_AT[0] * total) // cls.MIDDLE_AT[1])
                def _(k=k):
                    list(exchanges())[k].middle()

        body(*ins, *outs, *scratch)

        @pl.when(last)
        def _():
            for exchange in exchanges():
                exchange.finish()

    res = pl.pallas_call(
        carrying, name=name, grid=grid, in_specs=list(in_specs) + _any_specs(n), out_specs=list(out_specs) + _any_specs(m),
        out_shape=list(out_shape) + comm_shapes, scratch_shapes=list(scratch_shapes) + comm_sems,
        compiler_params=_params(semantics))(*args, *comm_args)
    comm_res, at = [], n_out
    for (_, arrs), shapes in zip(carries, shapes_of):
        comm_res.append(res[at:at + len(arrs)])
        at += len(shapes)
    return res[:n_out], (comm_res[0] if isinstance(carry, tuple) else comm_res)


def proj_fwd(x, g_mix, w_in, carry=None):
    t_len = x.shape[0]
    tm = min(1024, t_len)
    tn = D_IN // N_CHIPS

    def body(x_ref, g_ref, w_ref, proj_ref, h_ref):
        @pl.when(pl.program_id(1) == 0)
        def _():
            xv = x_ref[...]
            r = lax.rsqrt(jnp.mean(xv * xv, axis=-1, keepdims=True) + EPS)
            h_ref[...] = (xv * r * g_ref[...]).astype(BF16)

        proj_ref[...] = _dot(h_ref[...], w_ref[...]).astype(BF16)

    return _carried_call(
        body, "proj_fwd", (t_len // tm, N_CHIPS),
        in_specs=[pl.BlockSpec((tm, D), lambda i, j: (i, 0)),
                  pl.BlockSpec((1, D), lambda i, j: (0, 0)),
                  pl.BlockSpec((D, tn), lambda i, j: (0, j))],
        out_specs=[pl.BlockSpec((tm, tn), lambda i, j: (i, j)),
                   pl.BlockSpec((tm, D), lambda i, j: (i, 0))],
        out_shape=[jax.ShapeDtypeStruct((t_len, D_IN), BF16), jax.ShapeDtypeStruct((t_len, D), BF16)],
        scratch_shapes=[], semantics=("arbitrary", "arbitrary"), args=(x, g_mix, w_in), carry=carry)


def kv_fwd(mem, g_mem, w_kv4):
    half = D // 2

    def body(mem_ref, g_ref, w_ref, memn_ref, k_ref, v_ref):
        mv = mem_ref[...]
        r = lax.rsqrt(jnp.mean(mv * mv, axis=-1, keepdims=True) + EPS)
        mn = (mv * r * g_ref[...]).astype(BF16)
        memn_ref[...] = mn
        k_ref[:, 0:half] = _dot(mn, w_ref[0]).astype(BF16)
        k_ref[:, half:D] = _dot(mn, w_ref[1]).astype(BF16)
        v_ref[:, 0:half] = _dot(mn, w_ref[2]).astype(BF16)
        v_ref[:, half:D] = _dot(mn, w_ref[3]).astype(BF16)

    out = jax.ShapeDtypeStruct((N_MEM, D), BF16)
    return pl.pallas_call(body, name="kv_fwd", out_shape=[out, out, out], compiler_params=_params())(mem, g_mem, w_kv4)


def _softmax_rows(s):
    m = jnp.max(s, axis=-1, keepdims=True)
    e = jnp.exp(s - m)
    return e * pl.reciprocal(jnp.sum(e, axis=-1, keepdims=True), approx=True)


def _window_bands(tm, causal):
    t = lax.broadcasted_iota(jnp.int32, (tm, tm + POOL_PAD), 0)
    s = lax.broadcasted_iota(jnp.int32, (tm, tm + POOL_PAD), 1)
    d = (t + POOL_PAD - s) if causal else (s - t)
    return jnp.stack([((d >= 0) & (d < w)).astype(BF16) for w in POOL_WINDOWS])


def mixer_fwd(proj, x, conv_w8, w_co, w_pool, pool_scale, k, v, w_xo, w_out, carry=None):
    t_len = x.shape[0]
    tm = min(256, t_len)

    def body(proj_ref, x_ref, cw_ref, wco_ref, wpool_ref, ps_ref, k_ref, v_ref, wxo_ref, wout_ref,
             a_ref, pooled_ref, ya_ref, pp_ref, yx_ref, o_ref, p_ref, x1_ref, cu_ext, up_ext):
        i = pl.program_id(0)

        @pl.when(i == 0)
        def _():
            cu_ext[0:HALO, :] = jnp.zeros((HALO, D), F32)
            up_ext[0:HALO, :] = jnp.zeros((HALO, D), F32)

        cu = proj_ref[:, O_CA:O_CA + D].astype(F32) * proj_ref[:, O_UA:O_UA + D].astype(F32)
        cu_ext[HALO:HALO + tm, :] = cu
        conv = (cw_ref[2:3, :] * cu + cw_ref[1:2, :] * cu_ext[HALO - 1:HALO - 1 + tm, :]
                + cw_ref[0:1, :] * cu_ext[HALO - 2:HALO - 2 + tm, :])
        a = (proj_ref[:, O_BA:O_BA + D].astype(F32) * conv).astype(BF16)
        a_ref[...] = a
        ya = _dot(a, wco_ref[...])
        ya_ref[...] = ya.astype(BF16)

        up_ext[HALO:HALO + tm, :] = proj_ref[:, O_UP:O_UP + D].astype(F32)
        for g, window in enumerate(POOL_WINDOWS):
            cols = slice(g * GROUP_DIM, (g + 1) * GROUP_DIM)
            tok = up_ext[HALO:HALO + tm, cols]
            acc = tok
            for j in range(1, window):
                acc = acc + up_ext[HALO - j:HALO - j + tm, cols]
            pooled = (acc * _inv_count(i, tm, window) - tok).astype(BF16)
            pooled_ref[:, cols] = pooled
            pp_ref[:, cols] = _dot(pooled, wpool_ref[g]).astype(BF16)

        for hd in range(HEADS):
            cols = slice(hd * HEAD_DIM, (hd + 1) * HEAD_DIM)
            q = proj_ref[:, O_QX + hd * HEAD_DIM:O_QX + (hd + 1) * HEAD_DIM]
            p = _softmax_rows(_dot_nt(q, k_ref[:, cols]) * ATT_SCALE).astype(BF16)
            p_ref[:, hd * N_MEM:(hd + 1) * N_MEM] = p
            o_ref[:, cols] = _dot(p, v_ref[:, cols]).astype(BF16)
        yx = _dot(o_ref[...], wxo_ref[...])
        yx_ref[...] = yx.astype(BF16)

        merged = (_sigmoid(proj_ref[:, O_GA:O_GA + D].astype(F32)) * ya
                  + _sigmoid(proj_ref[:, O_GP:O_GP + D].astype(F32)) * (pp_ref[...].astype(F32) * ps_ref[...])
                  + _sigmoid(proj_ref[:, O_GX:O_GX + D].astype(F32)) * yx)
        x1_ref[...] = x_ref[...] + _dot(merged.astype(BF16), wout_ref[...])

        cu_ext[0:HALO, :] = cu_ext[tm:tm + HALO, :]
        up_ext[0:HALO, :] = up_ext[tm:tm + HALO, :]

    act = jax.ShapeDtypeStruct((t_len, D), BF16)
    return _carried_call(
        body, "mixer_fwd", (t_len // tm,),
        in_specs=[_rows(tm, D_IN), _rows(tm, D), _resident((8, D)), _resident((D, D)),
                  _resident((GROUPS, GROUP_DIM, GROUP_DIM)), _resident((1, D)),
                  _resident((N_MEM, D)), _resident((N_MEM, D)), _resident((D, D)), _resident((D, D))],
        out_specs=[_rows(tm, D)] * 7 + [_rows(tm, D)],
        out_shape=[act] * 6 + [jax.ShapeDtypeStruct((t_len, HEADS * N_MEM), BF16), jax.ShapeDtypeStruct((t_len, D), F32)],
        scratch_shapes=[pltpu.VMEM((tm + HALO, D), F32), pltpu.VMEM((tm + HALO, D), F32)],
        semantics=("arbitrary",), args=(proj, x, conv_w8, w_co, w_pool, pool_scale, k, v, w_xo, w_out), carry=carry)


def ffn_fwd(x1, target, g_ffn, w_gate, w_up, w_down, g_final):
    t_len = x1.shape[0]
    tm = min(512, t_len)

    def body(x1_ref, tgt_ref, g_ref, wg_ref, wu_ref, wd_ref, gf_ref, gate_ref, up_ref, dx2_ref, stat_ref):
        @pl.when(pl.program_id(0) == 0)
        def _():
            stat_ref[...] = jnp.zeros((8, D), F32)

        x1v = x1_ref[...]
        r2 = lax.rsqrt(jnp.mean(x1v * x1v, axis=-1, keepdims=True) + EPS)
        h2 = (x1v * r2 * g_ref[...]).astype(BF16)
        gate = _dot_nt(h2, wg_ref[...])
        up = _dot_nt(h2, wu_ref[...])
        gate_ref[...] = gate.astype(BF16)
        up_ref[...] = up.astype(BF16)
        act = (gate * _sigmoid(gate) * up).astype(BF16)
        x2 = x1v + _dot(act, wd_ref[...])
        r3 = lax.rsqrt(jnp.mean(x2 * x2, axis=-1, keepdims=True) + EPS)
        xh = x2 * r3
        diff = xh * gf_ref[...] - tgt_ref[...]
        dy = diff * (1.0 / D)
        stat_ref[0:1, :] += jnp.sum(dy * xh, axis=0, keepdims=True)
        stat_ref[1:2, :] += (0.5 / D) * jnp.sum(diff * diff, axis=0, keepdims=True)
        dxh = dy * gf_ref[...]
        dx2_ref[...] = r3 * (dxh - xh * jnp.mean(dxh * xh, axis=-1, keepdims=True))

    return pl.pallas_call(
        body, name="ffn_fwd",
        grid=(t_len // tm,),
        in_specs=[_rows(tm, D), _rows(tm, D), _resident((1, D)), _resident((D_FF, D)), _resident((D_FF, D)),
                  _resident((D_FF, D)), _resident((1, D))],
        out_specs=[_rows(tm, D_FF), _rows(tm, D_FF), _rows(tm, D), _const((8, D))],
        out_shape=[jax.ShapeDtypeStruct((t_len, D_FF), BF16), jax.ShapeDtypeStruct((t_len, D_FF), BF16),
                   jax.ShapeDtypeStruct((t_len, D), F32), jax.ShapeDtypeStruct((8, D), F32)],
        compiler_params=_params(("arbitrary",)),
    )(x1, target, g_ffn, w_gate, w_up, w_down, g_final)


def ffn_bwd(dx2, x1, gate, up, g_ffn, w_gate, w_up, w_down):
    t_len = x1.shape[0]
    tm = min(256, t_len)

    def body(dx2_ref, x1_ref, gate_ref, up_ref, g_ref, wg_ref, wu_ref, wd_ref,
             dx1_ref, dgate_ref, dup_ref, act_ref, h2_ref, stat_ref):
        @pl.when(pl.program_id(0) == 0)
        def _():
            stat_ref[...] = jnp.zeros((8, D), F32)

        dx2v = dx2_ref[...]
        gate = gate_ref[...].astype(F32)
        upv = up_ref[...].astype(F32)
        sg = _sigmoid(gate)
        silu = gate * sg
        act_ref[...] = (silu * upv).astype(BF16)
        dact = _dot_nt(dx2v.astype(BF16), wd_ref[...])
        dup = (dact * silu).astype(BF16)
        dgate = (dact * upv * (sg * (1.0 + gate * (1.0 - sg)))).astype(BF16)
        dup_ref[...] = dup
        dgate_ref[...] = dgate
        dh2 = _dot(dgate, wg_ref[...]) + _dot(dup, wu_ref[...])
        x1v = x1_ref[...]
        r2 = lax.rsqrt(jnp.mean(x1v * x1v, axis=-1, keepdims=True) + EPS)
        xh = x1v * r2
        h2_ref[...] = (xh * g_ref[...]).astype(BF16)
        stat_ref[0:1, :] += jnp.sum(dh2 * xh, axis=0, keepdims=True)
        dxh = dh2 * g_ref[...]
        dx1_ref[...] = dx2v + r2 * (dxh - xh * jnp.mean(dxh * xh, axis=-1, keepdims=True))

    ff = jax.ShapeDtypeStruct((t_len, D_FF), BF16)
    return pl.pallas_call(
        body, name="ffn_bwd",
        grid=(t_len // tm,),
        in_specs=[_rows(tm, D), _rows(tm, D), _rows(tm, D_FF), _rows(tm, D_FF), _resident((1, D)),
                  _resident((D_FF, D)), _resident((D_FF, D)), _resident((D_FF, D))],
        out_specs=[_rows(tm, D), _rows(tm, D_FF), _rows(tm, D_FF), _rows(tm, D_FF), _rows(tm, D), _const((8, D))],
        out_shape=[jax.ShapeDtypeStruct((t_len, D), F32), ff, ff, ff, jax.ShapeDtypeStruct((t_len, D), BF16),
                   jax.ShapeDtypeStruct((8, D), F32)],
        compiler_params=_params(("arbitrary",)),
    )(dx2, x1, gate, up, g_ffn, w_gate, w_up, w_down)


def mixer_bwd(dx1, proj, ya, pp, yx, probs, conv_w8, w_co, w_pool, pool_scale, k, v, w_xo, w_out, carry=None):
    t_len = dx1.shape[0]
    tm = min(256, t_len)
    n_tiles = t_len // tm
    halo_blocks = tm // HALO

    def body(dx1_ref, proj_ref, halo_ref, ya_ref, pp_ref, yx_ref, p_ref,
             cw_ref, wco_ref, wpool_ref, ps_ref, k_ref, v_ref, wxo_ref, wout_ref, band_ref,
             dproj_ref, merged_ref, dya_ref, dpp_ref, dyx_ref, dk_ref, dv_ref, stat_ref,
             cu_ext, dconv_ext, dpn_ext):
        step = pl.program_id(0)
        tile = n_tiles - 1 - step

        @pl.when(step == 0)
        def _():
            dk_ref[...] = jnp.zeros((N_MEM, D), F32)
            dv_ref[...] = jnp.zeros((N_MEM, D), F32)
            stat_ref[...] = jnp.zeros((8, D), F32)
            dconv_ext[tm:tm + HALO, :] = jnp.zeros((HALO, D), F32)
            dpn_ext[tm:tm + POOL_PAD, :] = jnp.zeros((POOL_PAD, D), BF16)

        dmerged = _dot_nt(dx1_ref[...].astype(BF16), wout_ref[...])
        sa = _sigmoid(proj_ref[:, O_GA:O_GA + D].astype(F32))
        sp = _sigmoid(proj_ref[:, O_GP:O_GP + D].astype(F32))
        sx = _sigmoid(proj_ref[:, O_GX:O_GX + D].astype(F32))
        ya = ya_ref[...].astype(F32)
        ppv = pp_ref[...].astype(F32)
        yp = ppv * ps_ref[...]
        yx = yx_ref[...].astype(F32)
        merged_ref[...] = (sa * ya + sp * yp + sx * yx).astype(BF16)
        dproj_ref[:, O_GA:O_GA + D] = (dmerged * ya * (sa * (1.0 - sa))).astype(BF16)
        dproj_ref[:, O_GP:O_GP + D] = (dmerged * yp * (sp * (1.0 - sp))).astype(BF16)
        dproj_ref[:, O_GX:O_GX + D] = (dmerged * yx * (sx * (1.0 - sx))).astype(BF16)
        dya = (dmerged * sa).astype(BF16)
        dyp = dmerged * sp
        dyx = (dmerged * sx).astype(BF16)
        dya_ref[...] = dya
        dyx_ref[...] = dyx
        stat_ref[1:2, :] += jnp.sum(dyp * ppv, axis=0, keepdims=True)
        dpp = (dyp * ps_ref[...]).astype(BF16)
        dpp_ref[...] = dpp

        da = _dot_nt(dya, wco_ref[...])
        c_a = proj_ref[:, O_CA:O_CA + D].astype(F32)
        u_a = proj_ref[:, O_UA:O_UA + D].astype(F32)
        cu = c_a * u_a
        halo_cu = halo_ref[:, O_CA:O_CA + D].astype(F32) * halo_ref[:, O_UA:O_UA + D].astype(F32)
        cu_ext[0:HALO, :] = jnp.where(tile > 0, halo_cu, 0.0)
        cu_ext[HALO:HALO + tm, :] = cu
        cu1 = cu_ext[HALO - 1:HALO - 1 + tm, :]
        cu2 = cu_ext[HALO - 2:HALO - 2 + tm, :]
        conv = cw_ref[2:3, :] * cu + cw_ref[1:2, :] * cu1 + cw_ref[0:1, :] * cu2
        dproj_ref[:, O_BA:O_BA + D] = (da * conv).astype(BF16)
        dconv = da * proj_ref[:, O_BA:O_BA + D].astype(F32)
        stat_ref[5:6, :] += jnp.sum(dconv * cu2, axis=0, keepdims=True)
        stat_ref[6:7, :] += jnp.sum(dconv * cu1, axis=0, keepdims=True)
        stat_ref[7:8, :] += jnp.sum(dconv * cu, axis=0, keepdims=True)
        dconv_ext[0:tm, :] = dconv
        dcu = (cw_ref[2:3, :] * dconv + cw_ref[1:2, :] * dconv_ext[1:1 + tm, :]
               + cw_ref[0:1, :] * dconv_ext[2:2 + tm, :])
        dproj_ref[:, O_CA:O_CA + D] = (dcu * u_a).astype(BF16)
        dproj_ref[:, O_UA:O_UA + D] = (dcu * c_a).astype(BF16)

        for g, window in enumerate(POOL_WINDOWS):
            cols = slice(g * GROUP_DIM, (g + 1) * GROUP_DIM)
            dpooled = _dot_nt(dpp[:, cols], wpool_ref[g])
            dpn_ext[0:tm, cols] = (dpooled * _inv_count(tile, tm, window)).astype(BF16)
            acc = _dot(band_ref[g], dpn_ext[:, cols])
            dproj_ref[:, O_UP + g * GROUP_DIM:O_UP + (g + 1) * GROUP_DIM] = (acc - dpooled).astype(BF16)

        do = _dot_nt(dyx, wxo_ref[...])
        for hd in range(HEADS):
            cols = slice(hd * HEAD_DIM, (hd + 1) * HEAD_DIM)
            q = proj_ref[:, O_QX + hd * HEAD_DIM:O_QX + (hd + 1) * HEAD_DIM]
            kh = k_ref[:, cols]
            p16 = p_ref[:, hd * N_MEM:(hd + 1) * N_MEM]
            p = p16.astype(F32)
            doh = do[:, cols].astype(BF16)
            dp = _dot_nt(doh, v_ref[:, cols])
            dv_ref[:, cols] += _dot_tn(p16, doh)
            ds = (p * (dp - jnp.sum(dp * p, axis=-1, keepdims=True)) * ATT_SCALE).astype(BF16)
            dproj_ref[:, O_QX + hd * HEAD_DIM:O_QX + (hd + 1) * HEAD_DIM] = _dot(ds, kh).astype(BF16)
            dk_ref[:, cols] += _dot_tn(ds, q)

        dconv_ext[tm:tm + HALO, :] = dconv_ext[0:HALO, :]
        dpn_ext[tm:tm + HALO, :] = dpn_ext[0:HALO, :]

    def rev(width):
        return pl.BlockSpec((tm, width), lambda s: (n_tiles - 1 - s, 0))

    halo_spec = pl.BlockSpec((HALO, D_IN), lambda s: (jnp.maximum((n_tiles - 1 - s) * halo_blocks - 1, 0), 0))
    act = jax.ShapeDtypeStruct((t_len, D), BF16)
    kv_grad = jax.ShapeDtypeStruct((N_MEM, D), F32)
    return _carried_call(
        body, "mixer_bwd", (n_tiles,),
        in_specs=[rev(D), rev(D_IN), halo_spec, rev(D), rev(D), rev(D), rev(D),
                  _resident((8, D)), _resident((D, D)), _resident((GROUPS, GROUP_DIM, GROUP_DIM)), _resident((1, D)),
                  _resident((N_MEM, D)), _resident((N_MEM, D)), _resident((D, D)), _resident((D, D)),
                  _resident((GROUPS, tm, tm + POOL_PAD))],
        out_specs=[rev(D_IN), rev(D), rev(D), rev(D), rev(D),
                   _const((N_MEM, D)), _const((N_MEM, D)), _const((8, D))],
        out_shape=[jax.ShapeDtypeStruct((t_len, D_IN), BF16), act, act, act, act, kv_grad, kv_grad,
                   jax.ShapeDtypeStruct((8, D), F32)],
        scratch_shapes=[pltpu.VMEM((tm + HALO, D), F32)] * 2 + [pltpu.VMEM((tm + POOL_PAD, D), BF16)],
        semantics=("arbitrary",),
        args=(dx1, proj, proj, ya, pp, yx, probs, conv_w8, w_co, w_pool, pool_scale, k, v, w_xo, w_out,
              _window_bands(tm, False)), carry=carry)


def in_bwd(dproj, w_in, x, dx1, g_mix, carry=None):
    t_len = x.shape[0]
    tm = min(512, t_len)

    def body(dproj_ref, w_ref, x_ref, dx1_ref, g_ref, gx_ref, stat_ref):
        @pl.when(pl.program_id(0) == 0)
        def _():
            stat_ref[...] = jnp.zeros((8, D), F32)

        dh = _dot_nt(dproj_ref[...], w_ref[...])
        xv = x_ref[...]
        r = lax.rsqrt(jnp.mean(xv * xv, axis=-1, keepdims=True) + EPS)
        xh = xv * r
        stat_ref[0:1, :] += jnp.sum(dh * xh, axis=0, keepdims=True)
        dxh = dh * g_ref[...]
        gx_ref[...] = dx1_ref[...] + r * (dxh - xh * jnp.mean(dxh * xh, axis=-1, keepdims=True))

    return _carried_call(
        body, "in_bwd", (t_len // tm,),
        in_specs=[_rows(tm, D_IN), _resident((D, D_IN)), _rows(tm, D), _rows(tm, D), _resident((1, D))],
        out_specs=[_rows(tm, D), _const((8, D))],
        out_shape=[jax.ShapeDtypeStruct((t_len, D), F32), jax.ShapeDtypeStruct((8, D), F32)],
        scratch_shapes=[], semantics=("arbitrary",), args=(dproj, w_in, x, dx1, g_mix), carry=carry)


def kv_bwd(dk, dv, memn, mem, g_mem, w_kv4):
    half = D // 2

    def body(dk_ref, dv_ref, memn_ref, mem_ref, g_ref, w_ref, gw_ref, gw16_ref, stat_ref):
        mn = memn_ref[...]
        parts = (dk_ref[:, 0:half], dk_ref[:, half:D], dv_ref[:, 0:half], dv_ref[:, half:D])
        dmemn = jnp.zeros((N_MEM, D), F32)
        for j, part in enumerate(parts):
            part = part.astype(BF16)
            gw = _dot_tn(mn, part)
            gw_ref[j] = gw
            gw16_ref[j] = gw.astype(BF16)
            dmemn = dmemn + _dot_nt(part, w_ref[j])
        mv = mem_ref[...]
        r = lax.rsqrt(jnp.mean(mv * mv, axis=-1, keepdims=True) + EPS)
        stat_ref[...] = jnp.zeros((8, D), F32)
        stat_ref[0:1, :] = jnp.sum(dmemn * (mv * r), axis=0, keepdims=True)

    return pl.pallas_call(
        body, name="kv_bwd",
        out_shape=[jax.ShapeDtypeStruct((N_CHIPS, D, half), F32), jax.ShapeDtypeStruct((N_CHIPS, D, half), BF16),
                   jax.ShapeDtypeStruct((8, D), F32)],
        compiler_params=_params(),
    )(dk, dv, memn, mem, g_mem, w_kv4)


def matmul_tn(name, a, b, tn, col_blocks=1, carry=None):
    t_len, k_dim = a.shape
    n_dim = b.shape[1]
    tt = min(1024, t_len)
    per_block = n_dim // col_blocks // tn

    def body(a_ref, b_ref, out_ref, out16_ref):
        @pl.when(pl.program_id(1) == 0)
        def _():
            out_ref[...] = jnp.zeros((k_dim, tn), F32)

        out_ref[...] += _dot_tn(a_ref[...].astype(BF16), b_ref[...].astype(BF16))

        @pl.when(pl.program_id(1) == t_len // tt - 1)
        def _():
            out16_ref[...] = out_ref[...].astype(BF16)

    if col_blocks == 1:
        out_spec = pl.BlockSpec((k_dim, tn), lambda n, t: (0, n))
        shape = (k_dim, n_dim)
    else:
        out_spec = pl.BlockSpec((None, k_dim, tn), lambda n, t: (n // per_block, 0, n % per_block))
        shape = (col_blocks, k_dim, n_dim // col_blocks)
    outs, carried = _carried_call(
        body, name, (n_dim // tn, t_len // tt),
        in_specs=[pl.BlockSpec((tt, k_dim), lambda n, t: (t, 0)), pl.BlockSpec((tt, tn), lambda n, t: (t, n))],
        out_specs=[out_spec, out_spec], out_shape=[jax.ShapeDtypeStruct(shape, F32), jax.ShapeDtypeStruct(shape, BF16)],
        scratch_shapes=[], semantics=("arbitrary", "arbitrary"), args=(a, b), carry=carry)
    return (tuple(outs), carried) if carry is not None else tuple(outs)


def grad_w_pool(pooled, dpp):
    t_len = pooled.shape[0]
    tt = min(1024, t_len)
    steps = t_len // tt

    def body(a_ref, b_ref, out_ref, out16_ref):
        @pl.when(pl.program_id(0) == 0)
        def _():
            out_ref[...] = jnp.zeros((GROUPS, GROUP_DIM, GROUP_DIM), F32)

        for g in range(GROUPS):
            cols = slice(g * GROUP_DIM, (g + 1) * GROUP_DIM)
            out_ref[g] += _dot_tn(a_ref[:, cols], b_ref[:, cols])

        @pl.when(pl.program_id(0) == steps - 1)
        def _():
            out16_ref[...] = out_ref[...].astype(BF16)

    shape = (GROUPS, GROUP_DIM, GROUP_DIM)
    return pl.pallas_call(
        body, name="grad_w_pool", grid=(steps,),
        in_specs=[_rows(tt, D), _rows(tt, D)], out_specs=[_const(shape), _const(shape)],
        out_shape=[jax.ShapeDtypeStruct(shape, F32), jax.ShapeDtypeStruct(shape, BF16)],
        compiler_params=_params(("arbitrary",)),
    )(pooled, dpp)


def _place():
    x, y, c = lax.axis_index("x"), lax.axis_index("y"), lax.axis_index("c")
    return x, y, c


def _other_chips(x, y):
    return [(1 - x, y), (x, 1 - y), (1 - x, 1 - y)]


def _any_specs(n):
    return [pl.BlockSpec(memory_space=pl.ANY)] * n


def gather_weights(shards, gather_cls):
    n = len(shards)

    def body(*refs):
        gather = gather_cls(refs[:n], refs[n:2 * n], *refs[2 * n:])
        gather.start()
        gather.middle()
        gather.finish()

    return pl.pallas_call(
        body, name="gather_weights",
        in_specs=_any_specs(n), out_specs=_any_specs(n),
        out_shape=gather_cls.out_shapes(shards), scratch_shapes=gather_cls.sems(n),
    )(*shards)


class _Gather:
    SLOTS = 8
    ROW_ALIGN = 16
    MIDDLE_AT = (5, 8)

    @staticmethod
    def out_shapes(shards):
        return [jax.ShapeDtypeStruct((N_CHIPS,) + s.shape, s.dtype) for s in shards]

    @staticmethod
    def sems(n):
        return [pltpu.SemaphoreType.DMA((n, _Gather.SLOTS)), pltpu.SemaphoreType.DMA((n, _Gather.SLOTS))]

    def __init__(self, ins, outs, send_sems, recv_sems):
        self.ins, self.outs, self.send_sems, self.recv_sems = ins, outs, send_sems, recv_sems
        x, y, c = _place()
        self.c, self.me, self.sibling = c, 2 * x + y, (x, y, 1 - c)
        self.across = [(1 - x, y), (x, 1 - y), (1 - x, 1 - y)]

    def _rows(self, a, which, part=None):
        half = self.ins[a].shape[1] // 2
        first = (half // 2) // self.ROW_ALIGN * self.ROW_ALIGN
        if part is None:
            return pl.ds(which * half, half)
        return pl.ds(which * half, first) if part == 0 else pl.ds(which * half + first, half - first)

    def _has_part(self, a, part):
        half = self.ins[a].shape[1] // 2
        return part == 1 or (half // 2) // self.ROW_ALIGN > 0

    def _block(self, a, chip, rows=slice(None)):
        index = chip if not isinstance(chip, tuple) else 2 * chip[0] + chip[1]
        if len(self.outs[a].shape) == len(self.ins[a].shape):
            cols = self.ins[a].shape[2]
            return self.outs[a].at[:, rows, pl.ds(pl.multiple_of(index * cols, cols), cols)]
        return self.outs[a].at[index, :, rows, :]

    def _remote(self, src, dst, a, slot, to):
        return pltpu.make_async_remote_copy(src_ref=src, dst_ref=dst, send_sem=self.send_sems.at[a, slot],
                                            recv_sem=self.recv_sems.at[a, slot], device_id=to, device_id_type=MESH)

    def _own(self, a):
        return self._remote(self.ins[a], self._block(a, self.me), a, 6, self.sibling)

    def _sent(self, a, axis):
        rows = self._rows(a, self.c)
        return self._remote(self.ins[a].at[:, rows, :], self._block(a, self.me, rows), a, axis,
                            (*self.across[axis], self.c))

    def _landed(self, a, axis):
        block = self._block(a, self.across[axis], self._rows(a, self.c))
        return self._remote(block, block, a, axis, (*self.across[axis], self.c))

    def _relayed(self, a, part, incoming):
        source = self.across[2] if incoming else self.across[part]
        block = self._block(a, source, self._rows(a, self.c, part))
        return self._remote(block, block, a, (2, 7)[part], (*self.across[1 - part], self.c))

    def _passed_on(self, a, source, which):
        block = self._block(a, self.across[source], self._rows(a, which))
        return self._remote(block, block, a, 3 + source, self.sibling)

    def start(self):
        for a in range(len(self.ins)):
            self._own(a).start()
        for a in range(len(self.ins)):
            for axis in range(2):
                self._sent(a, axis).start()

    def middle(self):
        for a in range(len(self.ins)):
            for axis in range(2):
                self._landed(a, axis).wait_recv()
                if self._has_part(a, axis):
                    self._relayed(a, axis, incoming=False).start()
                self._passed_on(a, axis, self.c).start()

    def finish(self):
        n = len(self.ins)
        for a in range(n):
            for part in range(2):
                if self._has_part(a, part):
                    self._relayed(a, part, incoming=True).wait_recv()
            self._passed_on(a, 2, self.c).start()
        for a in range(n):
            for source in range(3):
                self._passed_on(a, source, 1 - self.c).wait_recv()
            self._own(a).wait_recv()
        for a in range(n):
            self._own(a).wait_send()
            for axis in range(2):
                self._sent(a, axis).wait_send()
                if self._has_part(a, axis):
                    self._relayed(a, axis, incoming=False).wait_send()
            for source in range(3):
                self._passed_on(a, source, self.c).wait_send()


class _HalfExchange:
    @staticmethod
    def out_shapes(grads):
        return [jax.ShapeDtypeStruct((N_CHIPS, g.shape[1] // 2, g.shape[2]), g.dtype) for g in grads]

    @staticmethod
    def sems(n):
        return [pltpu.SemaphoreType.DMA((n,)), pltpu.SemaphoreType.DMA((n,))]

    def __init__(self, ins, outs, send_sems, recv_sems):
        self.ins, self.outs, self.send_sems, self.recv_sems = ins, outs, send_sems, recv_sems

    def _copies(self):
        x, y, c = _place()
        for a in range(len(self.ins)):
            h = self.ins[a].shape[1] // 2
            yield pltpu.make_async_remote_copy(
                src_ref=self.ins[a].at[:, pl.ds((1 - c) * h, h), :], dst_ref=self.outs[a],
                send_sem=self.send_sems.at[a], recv_sem=self.recv_sems.at[a], device_id=(x, y, 1 - c), device_id_type=MESH)

    def start(self):
        for cp in self._copies():
            cp.start()

    def finish(self):
        for cp in self._copies():
            cp.wait()


class _GatherColumns(_Gather):
    @staticmethod
    def out_shapes(shards):
        return [jax.ShapeDtypeStruct(s.shape[:2] + (N_CHIPS * s.shape[2],), s.dtype) for s in shards]


def exchange_sibling_halves(name, grads):
    n = len(grads)

    def body(*refs):
        exchange = _HalfExchange(refs[:n], refs[n:2 * n], *refs[2 * n:])
        exchange.start()
        exchange.finish()

    return pl.pallas_call(
        body, name=name,
        in_specs=_any_specs(n), out_specs=_any_specs(n),
        out_shape=_HalfExchange.out_shapes(grads), scratch_shapes=_HalfExchange.sems(n),
    )(*grads)


def _row_tile(rows, cols, budget=2 << 20):
    best = 16
    for tr in range(16, rows + 1, 16):
        if rows % tr == 0 and tr * cols * 4 <= budget:
            best = tr
    return best


def add_sibling_half(name, grad, got, core):
    _, rows, cols = grad.shape
    h = rows // 2
    tr = _row_tile(h, cols)
    per_half = h // tr

    def body(core_ref, g_ref, o_ref, out_ref, out16_ref):
        total = g_ref[...] + o_ref[...].astype(F32)
        out_ref[...] = total
        out16_ref[...] = total.astype(BF16)

    out_spec = pl.BlockSpec((None, tr, cols), lambda j, r, cr: (j, r, 0))
    return pl.pallas_call(
        body, name=name,
        grid_spec=pltpu.PrefetchScalarGridSpec(
            num_scalar_prefetch=1, grid=(N_CHIPS, per_half),
            in_specs=[pl.BlockSpec((None, tr, cols), lambda j, r, cr: (j, cr[0] * per_half + r, 0)),
                      pl.BlockSpec((None, tr, cols), lambda j, r, cr: (j, r, 0))],
            out_specs=[out_spec, out_spec]),
        out_shape=[jax.ShapeDtypeStruct((N_CHIPS, h, cols), F32), jax.ShapeDtypeStruct((N_CHIPS, h, cols), BF16)],
        compiler_params=_params(("arbitrary", "arbitrary")),
    )(core, grad, got)


class _ChipExchange:
    SLOTS = 6
    ROW_ALIGN = 16
    MIDDLE_AT = (3, 8)

    @staticmethod
    def out_shapes(partials):
        return ([jax.ShapeDtypeStruct((3,) + p.shape[1:], p.dtype) for p in partials]
                + [jax.ShapeDtypeStruct(p.shape[1:], p.dtype) for p in partials])

    @staticmethod
    def sems(n):
        return [pltpu.SemaphoreType.DMA((n, _ChipExchange.SLOTS)), pltpu.SemaphoreType.DMA((n, _ChipExchange.SLOTS))]

    def __init__(self, ins, outs, send_sems, recv_sems):
        n = len(ins)
        self.ins, self.outs, self.relays, self.send_sems, self.recv_sems = ins, outs[:n], outs[n:], send_sems, recv_sems

    def _copy(self, a, slot):
        x, y, c = _place()
        across = _other_chips(x, y)
        rows = self.ins[a].shape[1]
        first = (rows // 2) // self.ROW_ALIGN * self.ROW_ALIGN
        part = (pl.ds(0, first), pl.ds(first, rows - first))
        if slot < 2:
            px, py = across[slot]
            src, dst, to = self.ins[a].at[2 * px + py], self.outs[a].at[slot], across[slot]
        elif slot < 4:
            px, py = across[2]
            src, dst, to = self.ins[a].at[2 * px + py, part[slot - 2], :], self.relays[a].at[part[slot - 2], :], across[slot - 2]
        else:
            src, dst, to = self.relays[a].at[part[slot - 4], :], self.outs[a].at[2, part[slot - 4], :], across[5 - slot]
        return pltpu.make_async_remote_copy(src_ref=src, dst_ref=dst, send_sem=self.send_sems.at[a, slot],
                                            recv_sem=self.recv_sems.at[a, slot], device_id=(*to, c), device_id_type=MESH)

    def start(self):
        for slot in (2, 3, 0, 1):
            for a in range(len(self.ins)):
                self._copy(a, slot).start()

    def middle(self):
        for part in range(2):
            for a in range(len(self.ins)):
                self._copy(a, 2 + part).wait_recv()
                self._copy(a, 4 + part).start()

    def finish(self):
        for a in range(len(self.ins)):
            for slot in (0, 1, 4, 5):
                self._copy(a, slot).wait_recv()
        for a in range(len(self.ins)):
            for slot in range(self.SLOTS):
                self._copy(a, slot).wait_send()


def add_chip_blocks(name, partial, got, chip):
    _, h, cols = partial.shape
    tr = _row_tile(h, cols)

    def body(chip_ref, p_ref, g0_ref, g1_ref, g2_ref, out_ref):
        out_ref[...] = ((p_ref[...] + g0_ref[...].astype(F32)) + g1_ref[...].astype(F32)) + g2_ref[...].astype(F32)

    def got_spec(slot):
        return pl.BlockSpec((None, tr, cols), lambda r, ch: (slot, r, 0))

    return pl.pallas_call(
        body, name=name,
        grid_spec=pltpu.PrefetchScalarGridSpec(
            num_scalar_prefetch=1, grid=(h // tr,),
            in_specs=[pl.BlockSpec((None, tr, cols), lambda r, ch: (ch[0], r, 0)), got_spec(0), got_spec(1), got_spec(2)],
            out_specs=pl.BlockSpec((tr, cols), lambda r, ch: (r, 0))),
        out_shape=jax.ShapeDtypeStruct((h, cols), F32),
        compiler_params=_params(("arbitrary",)),
    )(chip, partial, got, got, got)


class _SiblingSwap:
    @staticmethod
    def out_shapes(halves):
        return [jax.ShapeDtypeStruct(v.shape, v.dtype) for v in halves]

    @staticmethod
    def sems(n):
        return [pltpu.SemaphoreType.DMA((n,)), pltpu.SemaphoreType.DMA((n,))]

    def __init__(self, ins, outs, send_sems, recv_sems):
        self.ins, self.outs, self.send_sems, self.recv_sems = ins, outs, send_sems, recv_sems

    def _copies(self):
        x, y, c = _place()
        for a in range(len(self.ins)):
            yield pltpu.make_async_remote_copy(
                src_ref=self.ins[a], dst_ref=self.outs[a], send_sem=self.send_sems.at[a], recv_sem=self.recv_sems.at[a],
                device_id=(x, y, 1 - c), device_id_type=MESH)

    def start(self):
        for cp in self._copies():
            cp.start()

    def finish(self):
        for cp in self._copies():
            cp.wait()


def swap_sibling_halves(halves):
    n = len(halves)

    def body(*refs):
        swap = _SiblingSwap(refs[:n], refs[n:2 * n], *refs[2 * n:])
        swap.start()
        swap.finish()

    return pl.pallas_call(
        body, name="swap_sibling_halves",
        in_specs=_any_specs(n), out_specs=_any_specs(n),
        out_shape=_SiblingSwap.out_shapes(halves), scratch_shapes=_SiblingSwap.sems(n),
    )(*halves)


def all_reduce_small(pack):
    rows = pack.shape[0]

    def body(pack_ref, out_ref, gathered, send_sems, recv_sems):
        x, y, c = _place()
        me = 4 * x + 2 * y + c
        gathered[me] = pack_ref[...]
        copies = []
        for rel in range(1, 8):
            fx, fy, fc = (rel >> 2) & 1, (rel >> 1) & 1, rel & 1
            peer = (x ^ fx, y ^ fy, c ^ fc)
            cp = pltpu.make_async_remote_copy(
                src_ref=pack_ref, dst_ref=gathered.at[me], send_sem=send_sems.at[rel - 1], recv_sem=recv_sems.at[rel - 1],
                device_id=peer, device_id_type=MESH)
            cp.start()
            copies.append(cp)
        for rel in range(1, 8):
            fx, fy, fc = (rel >> 2) & 1, (rel >> 1) & 1, rel & 1
            src = 4 * (x ^ fx) + 2 * (y ^ fy) + (c ^ fc)
            pltpu.make_async_remote_copy(
                src_ref=pack_ref, dst_ref=gathered.at[src], send_sem=send_sems.at[rel - 1], recv_sem=recv_sems.at[rel - 1],
                device_id=(x, y, c), device_id_type=MESH).wait_recv()
        for cp in copies:
            cp.wait_send()
        total = gathered[0]
        for dev in range(1, 8):
            total = total + gathered[dev]
        out_ref[...] = total

    return pl.pallas_call(
        body, name="all_reduce_small",
        in_specs=[pl.BlockSpec(memory_space=pltpu.VMEM)], out_specs=pl.BlockSpec(memory_space=pltpu.VMEM),
        out_shape=jax.ShapeDtypeStruct((rows, D), F32),
        scratch_shapes=[pltpu.VMEM((8, rows, D), F32), pltpu.SemaphoreType.DMA((7,)), pltpu.SemaphoreType.DMA((7,))],
    )(pack)


def hand_over(name, value):
    def body(in_ref, out_ref):
        del in_ref, out_ref

    return pl.pallas_call(
        body, name=name, in_specs=_any_specs(1), out_specs=_any_specs(1)[0],
        out_shape=jax.ShapeDtypeStruct(value.shape, value.dtype), input_output_aliases={0: 0},
    )(value)


def _adamw_update(w, g, m, v):
    nm = ADAM_B1 * m + (1.0 - ADAM_B1) * g
    nv = ADAM_B2 * v + (1.0 - ADAM_B2) * (g * g)
    m_hat = nm / (1.0 - ADAM_B1 ** ADAM_STEP)
    v_hat = nv / (1.0 - ADAM_B2 ** ADAM_STEP)
    delta = -ADAM_LR * (m_hat / (jnp.sqrt(v_hat) + ADAM_EPS) + ADAM_WD * w)
    return delta, nm, nv


def adamw(name, w, g, m, v):
    def body(w_ref, g_ref, m_ref, v_ref, d_ref, nm_ref, nv_ref):
        d_ref[...], nm_ref[...], nv_ref[...] = _adamw_update(w_ref[...], g_ref[...], m_ref[...], v_ref[...])

    out = jax.ShapeDtypeStruct(w.shape, F32)
    return pl.pallas_call(body, name=name, out_shape=[out] * 3, compiler_params=_params())(w, g, m, v)


def adamw_halves(name, w, mine, theirs, m, v, core):
    h, cols = mine.shape
    tr = _row_tile(h, cols)
    per_half = h // tr

    def body(core_ref, w_ref, mine_ref, theirs_ref, m_ref, v_ref, g_ref, d_ref, nm_ref, nv_ref):
        g = jnp.where(pl.program_id(0) == core_ref[0], mine_ref[...], theirs_ref[...])
        g_ref[...] = g
        d_ref[...], nm_ref[...], nv_ref[...] = _adamw_update(w_ref[...], g, m_ref[...], v_ref[...])

    full = pl.BlockSpec((tr, cols), lambda hh, r, cr: (hh * per_half + r, 0))
    mine_spec = pl.BlockSpec((tr, cols), lambda hh, r, cr: (jnp.where(hh == cr[0], r, 0), 0))
    theirs_spec = pl.BlockSpec((tr, cols), lambda hh, r, cr: (jnp.where(hh == cr[0], 0, r), 0))
    out = jax.ShapeDtypeStruct((2 * h, cols), F32)
    return pl.pallas_call(
        body, name=name,
        grid_spec=pltpu.PrefetchScalarGridSpec(
            num_scalar_prefetch=1, grid=(2, per_half),
            in_specs=[full, mine_spec, theirs_spec, full, full], out_specs=[full] * 4),
        out_shape=[out] * 4,
        compiler_params=_params(("arbitrary", "arbitrary")),
    )(core, w, mine, theirs, m, v)


BIG = ("w_in", "w_conv_out", "w_pool", "w_kv", "w_xattn_out", "w_out", "w_gate", "w_up", "w_down")


def kernel(x, mem, norm_mix, w_in, conv_w, w_conv_out, w_pool, pool_scale, norm_mem, w_kv, w_xattn_out, w_out, norm_ffn, w_gate, w_up, w_down, norm_final, loss_target, m_norm_mix, m_w_in, m_conv_w, m_w_conv_out, m_w_pool, m_pool_scale, m_norm_mem, m_w_kv, m_w_xattn_out, m_w_out, m_norm_ffn, m_w_gate, m_w_up, m_w_down, m_norm_final, v_norm_mix, v_w_in, v_conv_w, v_w_conv_out, v_w_pool, v_pool_scale, v_norm_mem, v_w_kv, v_w_xattn_out, v_w_out, v_norm_ffn, v_w_gate, v_w_up, v_w_down, v_norm_final):
    t_len = x.shape[1]
    xi, yi, ci = lax.axis_index("x"), lax.axis_index("y"), lax.axis_index("c")
    chip = 2 * xi + yi
    chip_arr = jnp.reshape(chip, (1,)).astype(jnp.int32)
    core_arr = jnp.reshape(ci, (1,)).astype(jnp.int32)

    conv_pad = jnp.concatenate([conv_w, jnp.zeros((1, 13, 256), F32)], axis=1)
    def t2(w):
        return jnp.swapaxes(w, 1, 2)

    (g_in,) = gather_weights([w_in.astype(BF16)], _GatherColumns)
    w_in_f = g_in[0]

    x2d = x[0]
    tgt = loss_target[0]
    (proj, h), (g_kv, g_conv_w, g_co, g_xo, g_out, g_pool, g_gate) = proj_fwd(
        x2d, norm_mix, w_in_f,
        carry=(_Gather, [w_kv.astype(BF16), conv_pad,
                         w_conv_out.astype(BF16), w_xattn_out.astype(BF16), w_out.astype(BF16),
                         w_pool[0].astype(BF16),
                         t2(w_gate).astype(BF16)]))
    w_kv4 = g_kv.reshape(N_CHIPS, D, D // 2)
    conv_full = jnp.transpose(g_conv_w[:, 0, 0:8, :], (1, 0, 2)).reshape(8, D)
    w_co_f, w_xo_f, w_out_f = g_co.reshape(D, D), g_xo.reshape(D, D), g_out.reshape(D, D)
    w_pool_f = jnp.transpose(g_pool, (1, 0, 2, 3)).reshape(GROUPS, GROUP_DIM, GROUP_DIM)
    memn, k, v = kv_fwd(mem[0], norm_mem, w_kv4)
    (a, pooled, ya, pp, yx, o, probs, x1), (g_up, g_down) = mixer_fwd(
        proj, x2d, conv_full, w_co_f, w_pool_f, pool_scale, k, v, w_xo_f, w_out_f,
        carry=(_Gather, [t2(w_up).astype(BF16), w_down.astype(BF16)]))
    w_gate_f, w_up_f, w_down_f = g_gate.reshape(D_FF, D), g_up.reshape(D_FF, D), g_down.reshape(D_FF, D)
    gate, up, dx2, stat_f = ffn_fwd(x1, tgt, norm_ffn, w_gate_f, w_up_f, w_down_f, norm_final.reshape(1, D))

    def by_chip(pair):
        return tuple(gw.reshape(N_CHIPS, gw.shape[0] // N_CHIPS, gw.shape[1]) for gw in pair)

    def chip_partials(names, grads, got):
        return [add_sibling_half("add_sibling_" + n, g32, o_, core_arr) for n, (g32, _), o_ in zip(names, grads, got)]

    def chip_sums(names, partials, got):
        return [add_chip_blocks("add_chips_" + n, p32, g2, chip_arr) for n, (p32, _), g2 in zip(names, partials, got)]

    dx1, dgate, dup, act, h2, stat_b1 = ffn_bwd(dx2, x1, gate, up, norm_ffn, w_gate_f, w_up_f, w_down_f)
    gw_gate = by_chip(matmul_tn("grad_w_gate", dgate, h2, 512))
    gw_up, got_gate = matmul_tn("grad_w_up", dup, h2, 512, carry=(_HalfExchange, [gw_gate[1]]))
    gw_up = by_chip(gw_up)
    gw_down, got_up = matmul_tn("grad_w_down", act, dx2, 512, carry=(_HalfExchange, [gw_up[1]]))
    gw_down = by_chip(gw_down)
    got_down = exchange_sibling_halves("exchange_sibling_halves_ffn", [gw_down[1]])
    names_ffn = ("w_gate", "w_up", "w_down")
    part_ffn = chip_partials(names_ffn, [gw_gate, gw_up, gw_down], list(got_gate) + list(got_up) + list(got_down))

    (dproj, merged, dya, dpp, dyx, dk, dv, stat_b2), got_ffn = mixer_bwd(
        dx1, proj, ya, pp, yx, probs, conv_full, w_co_f, w_pool_f, pool_scale, k, v, w_xo_f, w_out_f,
        carry=(_ChipExchange, [p16 for _, p16 in part_ffn]))
    gw_kv32, gw_kv16, stat_kv = kv_bwd(dk, dv, memn, mem[0], norm_mem, w_kv4)
    gw_pool = tuple(jnp.transpose(gw.reshape(GROUPS, N_CHIPS, 64, GROUP_DIM), (1, 0, 2, 3)).reshape(N_CHIPS, 256, GROUP_DIM)
                    for gw in grad_w_pool(pooled, dpp))
    gw_co, got_kv_pool = matmul_tn("grad_w_conv_out", a, dya, 1024, carry=(_HalfExchange, [gw_kv16, gw_pool[1]]))
    gw_co = by_chip(gw_co)
    gw_xo, got_co = matmul_tn("grad_w_xattn_out", o, dyx, 1024, carry=(_HalfExchange, [gw_co[1]]))
    gw_xo = by_chip(gw_xo)
    gw_out, got_xo = matmul_tn("grad_w_out", merged, dx1, 1024, carry=(_HalfExchange, [gw_xo[1]]))
    gw_out = by_chip(gw_out)
    got_out = exchange_sibling_halves("exchange_sibling_halves_mixer", [gw_out[1]])
    names_mix = ("w_kv", "w_pool", "w_conv_out", "w_xattn_out", "w_out")
    part_mix = chip_partials(names_mix, [(gw_kv32, gw_kv16), gw_pool, gw_co, gw_xo, gw_out],
                             list(got_kv_pool) + list(got_co) + list(got_xo) + list(got_out))

    gw_in, got_mix = matmul_tn("grad_w_in", h, dproj, 2048, col_blocks=N_CHIPS,
                               carry=(_ChipExchange, [p16 for _, p16 in part_mix]))
    part_in = chip_partials(("w_in",), [gw_in], exchange_sibling_halves("exchange_sibling_halves_in", [gw_in[1]]))
    mine_early = chip_sums(names_ffn + names_mix, part_ffn + part_mix, list(got_ffn) + list(got_mix))
    (grad_x, stat_b3), (got_in, theirs_early) = in_bwd(
        dproj, w_in_f, x2d, dx1, norm_mix,
        carry=[(_ChipExchange, [p16 for _, p16 in part_in]), (_SiblingSwap, mine_early)])
    mine_in = chip_sums(("w_in",), part_in, got_in)
    theirs_in = swap_sibling_halves(mine_in)
    reduced = dict(zip(names_ffn + names_mix + ("w_in",),
                       zip(mine_early + mine_in, list(theirs_early) + list(theirs_in))))
    mine = [reduced[n][0] for n in BIG]
    theirs = [reduced[n][1] for n in BIG]

    pack = jnp.concatenate([stat_b3[0:1], stat_b2[1:2], stat_kv[0:1], stat_b1[0:1], stat_f[0:1], stat_b2[5:8],
                            stat_f[1:2], jnp.zeros((7, D), F32)], axis=0)
    total = all_reduce_small(pack)
    loss = jnp.sum(total[8])
    g_conv_full = total[5:8]
    g_conv = lax.dynamic_slice_in_dim(g_conv_full, chip * 256, 256, axis=1)

    given = dict(w_in=(w_in, m_w_in, v_w_in), w_conv_out=(w_conv_out, m_w_conv_out, v_w_conv_out),
                 w_pool=(w_pool, m_w_pool, v_w_pool), w_kv=(w_kv, m_w_kv, v_w_kv),
                 w_xattn_out=(w_xattn_out, m_w_xattn_out, v_w_xattn_out), w_out=(w_out, m_w_out, v_w_out),
                 w_gate=(w_gate, m_w_gate, v_w_gate), w_up=(w_up, m_w_up, v_w_up), w_down=(w_down, m_w_down, v_w_down))
    out_g, out_d, out_m, out_v = {}, {}, {}, {}
    for n, mine_n, theirs_n in zip(BIG, mine, theirs):
        transposed = n in ("w_gate", "w_up")
        rows2d = (2 * mine_n.shape[0], mine_n.shape[1])
        w_, m_, v_ = ((t2(t) if transposed else t).reshape(rows2d) for t in given[n])
        res = adamw_halves("adamw_" + n, w_, mine_n, theirs_n, m_, v_, core_arr)
        if transposed:
            res = [t2(t.reshape(1, D_FF // N_CHIPS, D)) for t in res]
        out_g[n], out_d[n], out_m[n], out_v[n] = (t.reshape(given[n][0].shape) for t in res)

    def small_pack(vals, conv_part):
        conv_rows = jnp.concatenate([conv_part.reshape(3, 256), jnp.zeros((3, D - 256), F32)], axis=1)
        return jnp.concatenate([val.reshape(1, D) for val in vals] + [conv_rows], axis=0)

    sw = small_pack([norm_mix, pool_scale, norm_mem, norm_ffn, norm_final], conv_w)
    sm = small_pack([m_norm_mix, m_pool_scale, m_norm_mem, m_norm_ffn, m_norm_final], m_conv_w)
    sv = small_pack([v_norm_mix, v_pool_scale, v_norm_mem, v_norm_ffn, v_norm_final], v_conv_w)
    sg = small_pack([total[r] for r in range(5)], g_conv)
    sd, snm, snv = adamw("adamw_small", sw, sg, sm, sv)
    small_names = ("norm_mix", "pool_scale", "norm_mem", "norm_ffn", "norm_final")
    small_shapes = dict(norm_mix=(1, D), pool_scale=(1, D), norm_mem=(1, D), norm_ffn=(1, D), norm_final=(D,))
    for r, n in enumerate(small_names):
        out_g[n], out_d[n], out_m[n], out_v[n] = (t[r].reshape(small_shapes[n]) for t in (sg, sd, snm, snv))
    out_g["conv_w"], out_d["conv_w"], out_m["conv_w"], out_v["conv_w"] = (
        t[5:8, 0:256].reshape(1, 3, 256) for t in (sg, sd, snm, snv))

    order = ("norm_mix", "w_in", "conv_w", "w_conv_out", "w_pool", "pool_scale", "norm_mem", "w_kv", "w_xattn_out",
             "w_out", "norm_ffn", "w_gate", "w_up", "w_down", "norm_final")
    grad_x = hand_over("grad_x", grad_x)
    return (loss, grad_x.reshape(1, t_len, D), *[out_g[n] for n in order], *[out_d[n] for n in order],
            *[out_m[n] for n in order], *[out_v[n] for n in order])
```

```python
import functools

import jax
import jax.numpy as jnp
from jax import lax
from jax.experimental import pallas as pl
from jax.experimental.pallas import tpu as pltpu

F32 = jnp.float32
BF16 = jnp.bfloat16
MESH = pl.DeviceIdType.MESH

D = 1024
N_MEM = 256
HEADS = 4
HEAD_DIM = 256
GROUPS = 4
GROUP_DIM = 256
POOL_WINDOWS = (2, 4, 8, 16)
D_FF = 2816
D_IN = 8192
N_CHIPS = 4
EPS = 1e-6
HALO = 16
POOL_PAD = 128
ATT_SCALE = HEAD_DIM ** -0.5

ADAM_LR = 0.001
ADAM_B1 = 0.9
ADAM_B2 = 0.999
ADAM_EPS = 1e-08
ADAM_WD = 0.01
ADAM_STEP = 10

VMEM_LIMIT = 56 * 1024 * 1024

O_BA, O_CA, O_UA, O_UP, O_QX, O_GA, O_GP, O_GX = (k * D for k in range(8))

NT_DIMS = (((1,), (1,)), ((), ()))
TN_DIMS = (((0,), (0,)), ((), ()))


def _dot(a, b):
    return jnp.dot(a, b, preferred_element_type=F32)


def _dot_nt(a, b):
    return lax.dot_general(a, b, NT_DIMS, preferred_element_type=F32)


def _dot_tn(a, b):
    return lax.dot_general(a, b, TN_DIMS, preferred_element_type=F32)


def _sigmoid(z):
    return pl.reciprocal(1.0 + jnp.exp(-z), approx=True)


def _params(semantics=None):
    return pltpu.CompilerParams(dimension_semantics=semantics, vmem_limit_bytes=VMEM_LIMIT)


def _resident(shape):
    zeros = (0,) * len(shape)
    return pl.BlockSpec(shape, lambda *_: zeros, pipeline_mode=pl.Buffered(1))


def _const(shape):
    zeros = (0,) * len(shape)
    return pl.BlockSpec(shape, lambda *_: zeros)


def _rows(tm, width):
    return pl.BlockSpec((tm, width), lambda i: (i, 0))


def _inv_count(tile, tm, window):
    t = tile * tm + lax.broadcasted_iota(jnp.int32, (tm, 1), 0)
    return 1.0 / jnp.minimum(t + 1, window).astype(F32)


def _carried_call(body, name, grid, in_specs, out_specs, out_shape, scratch_shapes, semantics, args, carry):
    if carry is None:
        res = pl.pallas_call(body, name=name, grid=grid, in_specs=in_specs, out_specs=out_specs, out_shape=out_shape,
                             scratch_shapes=scratch_shapes, compiler_params=_params(semantics))(*args)
        return res, []
    carries = [carry] if isinstance(carry, tuple) else list(carry)
    comm_args = [arr for _, arrs in carries for arr in arrs]
    n, n_in, n_out, n_scratch = len(comm_args), len(in_specs), len(out_specs), len(scratch_shapes)
    shapes_of = [cls.out_shapes(arrs) for cls, arrs in carries]
    comm_shapes = [s for shapes in shapes_of for s in shapes]
    m = len(comm_shapes)
    comm_sems = [s for cls, arrs in carries for s in cls.sems(len(arrs))]

    def carrying(*refs):
        ins, comm_ins = refs[:n_in], refs[n_in:n_in + n]
        outs, comm_outs = refs[n_in + n:n_in + n + n_out], refs[n_in + n + n_out:n_in + n + n_out + m]
        scratch, sems = refs[n_in + n + n_out + m:n_in + n + n_out + m + n_scratch], refs[n_in + n + n_out + m + n_scratch:]
        steps = [pl.program_id(d) for d in range(len(grid))]
        first = functools.reduce(jnp.logical_and, [s == 0 for s in steps])
        last = functools.reduce(jnp.logical_and, [s == g - 1 for s, g in zip(steps, grid)])

        def exchanges():
            at_in = at_out = 0
            for k, (cls, arrs) in enumerate(carries):
                yield cls(comm_ins[at_in:at_in + len(arrs)], comm_outs[at_out:at_out + len(shapes_of[k])],
                          sems[2 * k], sems[2 * k + 1])
                at_in, at_out = at_in + len(arrs), at_out + len(shapes_of[k])

        @pl.when(first)
        def _():
            for exchange in exchanges():
                exchange.start()

        linear, total = 0, 1
        for s, g in zip(steps, grid):
            linear, total = linear * g + s, total * g
        for k, (cls, _) in enumerate(carries):
            if hasattr(cls, "middle"):
                @pl.when(linear == (cls.MIDDLE_AT[0] * total) // cls.MIDDLE_AT[1])
                def _(k=k):
                    list(exchanges())[k].middle()

        body(*ins, *outs, *scratch)

        @pl.when(last)
        def _():
            for exchange in exchanges():
                exchange.finish()

    res = pl.pallas_call(
        carrying, name=name, grid=grid, in_specs=list(in_specs) + _any_specs(n), out_specs=list(out_specs) + _any_specs(m),
        out_shape=list(out_shape) + comm_shapes, scratch_shapes=list(scratch_shapes) + comm_sems,
        compiler_params=_params(semantics))(*args, *comm_args)
    comm_res, at = [], n_out
    for (_, arrs), shapes in zip(carries, shapes_of):
        comm_res.append(res[at:at + len(arrs)])
        at += len(shapes)
    return res[:n_out], (comm_res[0] if isinstance(carry, tuple) else comm_res)


def proj_fwd(x, g_mix, w_in, carry=None):
    t_len = x.shape[0]
    tm = min(1024, t_len)
    tn = D_IN // N_CHIPS

    def body(x_ref, g_ref, w_ref, proj_ref, h_ref):
        @pl.when(pl.program_id(1) == 0)
        def _():
            xv = x_ref[...]
            r = lax.rsqrt(jnp.mean(xv * xv, axis=-1, keepdims=True) + EPS)
            h_ref[...] = (xv * r * g_ref[...]).astype(BF16)

        proj_ref[...] = _dot(h_ref[...], w_ref[...]).astype(BF16)

    return _carried_call(
        body, "proj_fwd", (t_len // tm, N_CHIPS),
        in_specs=[pl.BlockSpec((tm, D), lambda i, j: (i, 0)),
                  pl.BlockSpec((1, D), lambda i, j: (0, 0)),
                  pl.BlockSpec((D, tn), lambda i, j: (0, j))],
        out_specs=[pl.BlockSpec((tm, tn), lambda i, j: (i, j)),
                   pl.BlockSpec((tm, D), lambda i, j: (i, 0))],
        out_shape=[jax.ShapeDtypeStruct((t_len, D_IN), BF16), jax.ShapeDtypeStruct((t_len, D), BF16)],
        scratch_shapes=[], semantics=("arbitrary", "arbitrary"), args=(x, g_mix, w_in), carry=carry)


def kv_fwd(mem, g_mem, w_kv4):
    half = D // 2

    def body(mem_ref, g_ref, w_ref, memn_ref, k_ref, v_ref):
        mv = mem_ref[...]
        r = lax.rsqrt(jnp.mean(mv * mv, axis=-1, keepdims=True) + EPS)
        mn = (mv * r * g_ref[...]).astype(BF16)
        memn_ref[...] = mn
        k_ref[:, 0:half] = _dot(mn, w_ref[0]).astype(BF16)
        k_ref[:, half:D] = _dot(mn, w_ref[1]).astype(BF16)
        v_ref[:, 0:half] = _dot(mn, w_ref[2]).astype(BF16)
        v_ref[:, half:D] = _dot(mn, w_ref[3]).astype(BF16)

    out = jax.ShapeDtypeStruct((N_MEM, D), BF16)
    return pl.pallas_call(body, name="kv_fwd", out_shape=[out, out, out], compiler_params=_params())(mem, g_mem, w_kv4)


def _softmax_rows(s):
    m = jnp.max(s, axis=-1, keepdims=True)
    e = jnp.exp(s - m)
    return e * pl.reciprocal(jnp.sum(e, axis=-1, keepdims=True), approx=True)


def _window_bands(tm, causal):
    t = lax.broadcasted_iota(jnp.int32, (tm, tm + POOL_PAD), 0)
    s = lax.broadcasted_iota(jnp.int32, (tm, tm + POOL_PAD), 1)
    d = (t + POOL_PAD - s) if causal else (s - t)
    return jnp.stack([((d >= 0) & (d < w)).astype(BF16) for w in POOL_WINDOWS])


def mixer_fwd(proj, x, conv_w8, w_co, w_pool, pool_scale, k, v, w_xo, w_out, carry=None):
    t_len = x.shape[0]
    tm = min(256, t_len)

    def body(proj_ref, x_ref, cw_ref, wco_ref, wpool_ref, ps_ref, k_ref, v_ref, wxo_ref, wout_ref,
             a_ref, pooled_ref, ya_ref, pp_ref, yx_ref, o_ref, p_ref, x1_ref, cu_ext, up_ext):
        i = pl.program_id(0)

        @pl.when(i == 0)
        def _():
            cu_ext[0:HALO, :] = jnp.zeros((HALO, D), F32)
            up_ext[0:HALO, :] = jnp.zeros((HALO, D), F32)

        cu = proj_ref[:, O_CA:O_CA + D].astype(F32) * proj_ref[:, O_UA:O_UA + D].astype(F32)
        cu_ext[HALO:HALO + tm, :] = cu
        conv = (cw_ref[2:3, :] * cu + cw_ref[1:2, :] * cu_ext[HALO - 1:HALO - 1 + tm, :]
                + cw_ref[0:1, :] * cu_ext[HALO - 2:HALO - 2 + tm, :])
        a = (proj_ref[:, O_BA:O_BA + D].astype(F32) * conv).astype(BF16)
        a_ref[...] = a
        ya = _dot(a, wco_ref[...])
        ya_ref[...] = ya.astype(BF16)

        up_ext[HALO:HALO + tm, :] = proj_ref[:, O_UP:O_UP + D].astype(F32)
        for g, window in enumerate(POOL_WINDOWS):
            cols = slice(g * GROUP_DIM, (g + 1) * GROUP_DIM)
            tok = up_ext[HALO:HALO + tm, cols]
            acc = tok
            for j in range(1, window):
                acc = acc + up_ext[HALO - j:HALO - j + tm, cols]
            pooled = (acc * _inv_count(i, tm, window) - tok).astype(BF16)
            pooled_ref[:, cols] = pooled
            pp_ref[:, cols] = _dot(pooled, wpool_ref[g]).astype(BF16)

        for hd in range(HEADS):
            cols = slice(hd * HEAD_DIM, (hd + 1) * HEAD_DIM)
            q = proj_ref[:, O_QX + hd * HEAD_DIM:O_QX + (hd + 1) * HEAD_DIM]
            p = _softmax_rows(_dot_nt(q, k_ref[:, cols]) * ATT_SCALE).astype(BF16)
            p_ref[:, hd * N_MEM:(hd + 1) * N_MEM] = p
            o_ref[:, cols] = _dot(p, v_ref[:, cols]).astype(BF16)
        yx = _dot(o_ref[...], wxo_ref[...])
        yx_ref[...] = yx.astype(BF16)

        merged = (_sigmoid(proj_ref[:, O_GA:O_GA + D].astype(F32)) * ya
                  + _sigmoid(proj_ref[:, O_GP:O_GP + D].astype(F32)) * (pp_ref[...].astype(F32) * ps_ref[...])
                  + _sigmoid(proj_ref[:, O_GX:O_GX + D].astype(F32)) * yx)
        x1_ref[...] = x_ref[...] + _dot(merged.astype(BF16), wout_ref[...])

        cu_ext[0:HALO, :] = cu_ext[tm:tm + HALO, :]
        up_ext[0:HALO, :] = up_ext[tm:tm + HALO, :]

    act = jax.ShapeDtypeStruct((t_len, D), BF16)
    return _carried_call(
        body, "mixer_fwd", (t_len // tm,),
        in_specs=[_rows(tm, D_IN), _rows(tm, D), _resident((8, D)), _resident((D, D)),
                  _resident((GROUPS, GROUP_DIM, GROUP_DIM)), _resident((1, D)),
                  _resident((N_MEM, D)), _resident((N_MEM, D)), _resident((D, D)), _resident((D, D))],
        out_specs=[_rows(tm, D)] * 7 + [_rows(tm, D)],
        out_shape=[act] * 6 + [jax.ShapeDtypeStruct((t_len, HEADS * N_MEM), BF16), jax.ShapeDtypeStruct((t_len, D), F32)],
        scratch_shapes=[pltpu.VMEM((tm + HALO, D), F32), pltpu.VMEM((tm + HALO, D), F32)],
        semantics=("arbitrary",), args=(proj, x, conv_w8, w_co, w_pool, pool_scale, k, v, w_xo, w_out), carry=carry)


def ffn_fwd(x1, target, g_ffn, w_gate, w_up, w_down, g_final):
    t_len = x1.shape[0]
    tm = min(512, t_len)

    def body(x1_ref, tgt_ref, g_ref, wg_ref, wu_ref, wd_ref, gf_ref, gate_ref, up_ref, dx2_ref, stat_ref):
        @pl.when(pl.program_id(0) == 0)
        def _():
            stat_ref[...] = jnp.zeros((8, D), F32)

        x1v = x1_ref[...]
        r2 = lax.rsqrt(jnp.mean(x1v * x1v, axis=-1, keepdims=True) + EPS)
        h2 = (x1v * r2 * g_ref[...]).astype(BF16)
        gate = _dot_nt(h2, wg_ref[...])
        up = _dot_nt(h2, wu_ref[...])
        gate_ref[...] = gate.astype(BF16)
        up_ref[...] = up.astype(BF16)
        act = (gate * _sigmoid(gate) * up).astype(BF16)
        x2 = x1v + _dot(act, wd_ref[...])
        r3 = lax.rsqrt(jnp.mean(x2 * x2, axis=-1, keepdims=True) + EPS)
        xh = x2 * r3
        diff = xh * gf_ref[...] - tgt_ref[...]
        dy = diff * (1.0 / D)
        stat_ref[0:1, :] += jnp.sum(dy * xh, axis=0, keepdims=True)
        stat_ref[1:2, :] += (0.5 / D) * jnp.sum(diff * diff, axis=0, keepdims=True)
        dxh = dy * gf_ref[...]
        dx2_ref[...] = r3 * (dxh - xh * jnp.mean(dxh * xh, axis=-1, keepdims=True))

    return pl.pallas_call(
        body, name="ffn_fwd",
        grid=(t_len // tm,),
        in_specs=[_rows(tm, D), _rows(tm, D), _resident((1, D)), _resident((D_FF, D)), _resident((D_FF, D)),
                  _resident((D_FF, D)), _resident((1, D))],
        out_specs=[_rows(tm, D_FF), _rows(tm, D_FF), _rows(tm, D), _const((8, D))],
        out_shape=[jax.ShapeDtypeStruct((t_len, D_FF), BF16), jax.ShapeDtypeStruct((t_len, D_FF), BF16),
                   jax.ShapeDtypeStruct((t_len, D), F32), jax.ShapeDtypeStruct((8, D), F32)],
        compiler_params=_params(("arbitrary",)),
    )(x1, target, g_ffn, w_gate, w_up, w_down, g_final)


def ffn_bwd(dx2, x1, gate, up, g_ffn, w_gate, w_up, w_down):
    t_len = x1.shape[0]
    tm = min(256, t_len)

    def body(dx2_ref, x1_ref, gate_ref, up_ref, g_ref, wg_ref, wu_ref, wd_ref,
             dx1_ref, dgate_ref, dup_ref, act_ref, h2_ref, stat_ref):
        @pl.when(pl.program_id(0) == 0)
        def _():
            stat_ref[...] = jnp.zeros((8, D), F32)

        dx2v = dx2_ref[...]
        gate = gate_ref[...].astype(F32)
        upv = up_ref[...].astype(F32)
        sg = _sigmoid(gate)
        silu = gate * sg
        act_ref[...] = (silu * upv).astype(BF16)
        dact = _dot_nt(dx2v.astype(BF16), wd_ref[...])
        dup = (dact * silu).astype(BF16)
        dgate = (dact * upv * (sg * (1.0 + gate * (1.0 - sg)))).astype(BF16)
        dup_ref[...] = dup
        dgate_ref[...] = dgate
        dh2 = _dot(dgate, wg_ref[...]) + _dot(dup, wu_ref[...])
        x1v = x1_ref[...]
        r2 = lax.rsqrt(jnp.mean(x1v * x1v, axis=-1, keepdims=True) + EPS)
        xh = x1v * r2
        h2_ref[...] = (xh * g_ref[...]).astype(BF16)
        stat_ref[0:1, :] += jnp.sum(dh2 * xh, axis=0, keepdims=True)
        dxh = dh2 * g_ref[...]
        dx1_ref[...] = dx2v + r2 * (dxh - xh * jnp.mean(dxh * xh, axis=-1, keepdims=True))

    ff = jax.ShapeDtypeStruct((t_len, D_FF), BF16)
    return pl.pallas_call(
        body, name="ffn_bwd",
        grid=(t_len // tm,),
        in_specs=[_rows(tm, D), _rows(tm, D), _rows(tm, D_FF), _rows(tm, D_FF), _resident((1, D)),
                  _resident((D_FF, D)), _resident((D_FF, D)), _resident((D_FF, D))],
        out_specs=[_rows(tm, D), _rows(tm, D_FF), _rows(tm, D_FF), _rows(tm, D_FF), _rows(tm, D), _const((8, D))],
        out_shape=[jax.ShapeDtypeStruct((t_len, D), F32), ff, ff, ff, jax.ShapeDtypeStruct((t_len, D), BF16),
                   jax.ShapeDtypeStruct((8, D), F32)],
        compiler_params=_params(("arbitrary",)),
    )(dx2, x1, gate, up, g_ffn, w_gate, w_up, w_down)


def mixer_bwd(dx1, proj, ya, pp, yx, probs, conv_w8, w_co, w_pool, pool_scale, k, v, w_xo, w_out, carry=None):
    t_len = dx1.shape[0]
    tm = min(256, t_len)
    n_tiles = t_len // tm
    halo_blocks = tm // HALO

    def body(dx1_ref, proj_ref, halo_ref, ya_ref, pp_ref, yx_ref, p_ref,
             cw_ref, wco_ref, wpool_ref, ps_ref, k_ref, v_ref, wxo_ref, wout_ref, band_ref,
             dproj_ref, merged_ref, dya_ref, dpp_ref, dyx_ref, dk_ref, dv_ref, stat_ref,
             cu_ext, dconv_ext, dpn_ext):
        step = pl.program_id(0)
        tile = n_tiles - 1 - step

        @pl.when(step == 0)
        def _():
            dk_ref[...] = jnp.zeros((N_MEM, D), F32)
            dv_ref[...] = jnp.zeros((N_MEM, D), F32)
            stat_ref[...] = jnp.zeros((8, D), F32)
            dconv_ext[tm:tm + HALO, :] = jnp.zeros((HALO, D), F32)
            dpn_ext[tm:tm + POOL_PAD, :] = jnp.zeros((POOL_PAD, D), BF16)

        dmerged = _dot_nt(dx1_ref[...].astype(BF16), wout_ref[...]).astype(BF16)
        sa = _sigmoid(proj_ref[:, O_GA:O_GA + D].astype(F32)).astype(BF16)
        sp = _sigmoid(proj_ref[:, O_GP:O_GP + D].astype(F32)).astype(BF16)
        sx = _sigmoid(proj_ref[:, O_GX:O_GX + D].astype(F32)).astype(BF16)
        ya = ya_ref[...]
        ppv = pp_ref[...]
        scale = ps_ref[...].astype(BF16)
        yp = ppv * scale
        yx = yx_ref[...]
        merged_ref[...] = sa * ya + sp * yp + sx * yx
        dproj_ref[:, O_GA:O_GA + D] = dmerged * ya * (sa * (1.0 - sa))
        dproj_ref[:, O_GP:O_GP + D] = dmerged * yp * (sp * (1.0 - sp))
        dproj_ref[:, O_GX:O_GX + D] = dmerged * yx * (sx * (1.0 - sx))
        dya = dmerged * sa
        dyp = dmerged * sp
        dyx = dmerged * sx
        dya_ref[...] = dya
        dyx_ref[...] = dyx
        stat_ref[1:2, :] += jnp.sum(dyp.astype(F32) * ppv.astype(F32), axis=0, keepdims=True)
        dpp = dyp * scale
        dpp_ref[...] = dpp

        da = _dot_nt(dya, wco_ref[...])
        c_a = proj_ref[:, O_CA:O_CA + D].astype(F32)
        u_a = proj_ref[:, O_UA:O_UA + D].astype(F32)
        cu = c_a * u_a
        halo_cu = halo_ref[:, O_CA:O_CA + D].astype(F32) * halo_ref[:, O_UA:O_UA + D].astype(F32)
        cu_ext[0:HALO, :] = jnp.where(tile > 0, halo_cu, 0.0)
        cu_ext[HALO:HALO + tm, :] = cu
        cu1 = cu_ext[HALO - 1:HALO - 1 + tm, :]
        cu2 = cu_ext[HALO - 2:HALO - 2 + tm, :]
        conv = cw_ref[2:3, :] * cu + cw_ref[1:2, :] * cu1 + cw_ref[0:1, :] * cu2
        dproj_ref[:, O_BA:O_BA + D] = (da * conv).astype(BF16)
        dconv = da * proj_ref[:, O_BA:O_BA + D].astype(F32)
        stat_ref[5:6, :] += jnp.sum(dconv * cu2, axis=0, keepdims=True)
        stat_ref[6:7, :] += jnp.sum(dconv * cu1, axis=0, keepdims=True)
        stat_ref[7:8, :] += jnp.sum(dconv * cu, axis=0, keepdims=True)
        dconv_ext[0:tm, :] = dconv
        dcu = (cw_ref[2:3, :] * dconv + cw_ref[1:2, :] * dconv_ext[1:1 + tm, :]
               + cw_ref[0:1, :] * dconv_ext[2:2 + tm, :])
        dproj_ref[:, O_CA:O_CA + D] = (dcu * u_a).astype(BF16)
        dproj_ref[:, O_UA:O_UA + D] = (dcu * c_a).astype(BF16)

        for g, window in enumerate(POOL_WINDOWS):
            cols = slice(g * GROUP_DIM, (g + 1) * GROUP_DIM)
            dpooled = _dot_nt(dpp[:, cols], wpool_ref[g])
            dpn_ext[0:tm, cols] = (dpooled * _inv_count(tile, tm, window)).astype(BF16)
            acc = _dot(band_ref[g], dpn_ext[:, cols])
            dproj_ref[:, O_UP + g * GROUP_DIM:O_UP + (g + 1) * GROUP_DIM] = (acc - dpooled).astype(BF16)

        do = _dot_nt(dyx, wxo_ref[...])
        for hd in range(HEADS):
            cols = slice(hd * HEAD_DIM, (hd + 1) * HEAD_DIM)
            q = proj_ref[:, O_QX + hd * HEAD_DIM:O_QX + (hd + 1) * HEAD_DIM]
            kh = k_ref[:, cols]
            p16 = p_ref[:, hd * N_MEM:(hd + 1) * N_MEM]
            p = p16.astype(F32)
            doh = do[:, cols].astype(BF16)
            dp = _dot_nt(doh, v_ref[:, cols])
            dv_ref[:, cols] += _dot_tn(p16, doh)
            ds = (p * (dp - jnp.sum(dp * p, axis=-1, keepdims=True)) * ATT_SCALE).astype(BF16)
            dproj_ref[:, O_QX + hd * HEAD_DIM:O_QX + (hd + 1) * HEAD_DIM] = _dot(ds, kh).astype(BF16)
            dk_ref[:, cols] += _dot_tn(ds, q)

        dconv_ext[tm:tm + HALO, :] = dconv_ext[0:HALO, :]
        dpn_ext[tm:tm + HALO, :] = dpn_ext[0:HALO, :]

    def rev(width):
        return pl.BlockSpec((tm, width), lambda s: (n_tiles - 1 - s, 0))

    halo_spec = pl.BlockSpec((HALO, D_IN), lambda s: (jnp.maximum((n_tiles - 1 - s) * halo_blocks - 1, 0), 0))
    act = jax.ShapeDtypeStruct((t_len, D), BF16)
    kv_grad = jax.ShapeDtypeStruct((N_MEM, D), F32)
    return _carried_call(
        body, "mixer_bwd", (n_tiles,),
        in_specs=[rev(D), rev(D_IN), halo_spec, rev(D), rev(D), rev(D), rev(D),
                  _resident((8, D)), _resident((D, D)), _resident((GROUPS, GROUP_DIM, GROUP_DIM)), _resident((1, D)),
                  _resident((N_MEM, D)), _resident((N_MEM, D)), _resident((D, D)), _resident((D, D)),
                  _resident((GROUPS, tm, tm + POOL_PAD))],
        out_specs=[rev(D_IN), rev(D), rev(D), rev(D), rev(D),
                   _const((N_MEM, D)), _const((N_MEM, D)), _const((8, D))],
        out_shape=[jax.ShapeDtypeStruct((t_len, D_IN), BF16), act, act, act, act, kv_grad, kv_grad,
                   jax.ShapeDtypeStruct((8, D), F32)],
        scratch_shapes=[pltpu.VMEM((tm + HALO, D), F32)] * 2 + [pltpu.VMEM((tm + POOL_PAD, D), BF16)],
        semantics=("arbitrary",),
        args=(dx1, proj, proj, ya, pp, yx, probs, conv_w8, w_co, w_pool, pool_scale, k, v, w_xo, w_out,
              _window_bands(tm, False)), carry=carry)


def in_bwd(dproj, w_in, x, dx1, g_mix, carry=None):
    t_len = x.shape[0]
    tm = min(512, t_len)

    def body(dproj_ref, w_ref, x_ref, dx1_ref, g_ref, gx_ref, stat_ref):
        @pl.when(pl.program_id(0) == 0)
        def _():
            stat_ref[...] = jnp.zeros((8, D), F32)

        dh = _dot_nt(dproj_ref[...], w_ref[...])
        xv = x_ref[...]
        r = lax.rsqrt(jnp.mean(xv * xv, axis=-1, keepdims=True) + EPS)
        xh = xv * r
        stat_ref[0:1, :] += jnp.sum(dh * xh, axis=0, keepdims=True)
        dxh = dh * g_ref[...]
        gx_ref[...] = dx1_ref[...] + r * (dxh - xh * jnp.mean(dxh * xh, axis=-1, keepdims=True))

    return _carried_call(
        body, "in_bwd", (t_len // tm,),
        in_specs=[_rows(tm, D_IN), _resident((D, D_IN)), _rows(tm, D), _rows(tm, D), _resident((1, D))],
        out_specs=[_rows(tm, D), _const((8, D))],
        out_shape=[jax.ShapeDtypeStruct((t_len, D), F32), jax.ShapeDtypeStruct((8, D), F32)],
        scratch_shapes=[], semantics=("arbitrary",), args=(dproj, w_in, x, dx1, g_mix), carry=carry)


def kv_bwd(dk, dv, memn, mem, g_mem, w_kv4):
    half = D // 2

    def body(dk_ref, dv_ref, memn_ref, mem_ref, g_ref, w_ref, gw_ref, gw16_ref, stat_ref):
        mn = memn_ref[...]
        parts = (dk_ref[:, 0:half], dk_ref[:, half:D], dv_ref[:, 0:half], dv_ref[:, half:D])
        dmemn = jnp.zeros((N_MEM, D), F32)
        for j, part in enumerate(parts):
            part = part.astype(BF16)
            gw = _dot_tn(mn, part)
            gw_ref[j] = gw
            gw16_ref[j] = gw.astype(BF16)
            dmemn = dmemn + _dot_nt(part, w_ref[j])
        mv = mem_ref[...]
        r = lax.rsqrt(jnp.mean(mv * mv, axis=-1, keepdims=True) + EPS)
        stat_ref[...] = jnp.zeros((8, D), F32)
        stat_ref[0:1, :] = jnp.sum(dmemn * (mv * r), axis=0, keepdims=True)

    return pl.pallas_call(
        body, name="kv_bwd",
        out_shape=[jax.ShapeDtypeStruct((N_CHIPS, D, half), F32), jax.ShapeDtypeStruct((N_CHIPS, D, half), BF16),
                   jax.ShapeDtypeStruct((8, D), F32)],
        compiler_params=_params(),
    )(dk, dv, memn, mem, g_mem, w_kv4)


def matmul_tn(name, a, b, tn, col_blocks=1, carry=None):
    t_len, k_dim = a.shape
    n_dim = b.shape[1]
    tt = min(1024, t_len)
    per_block = n_dim // col_blocks // tn

    def body(a_ref, b_ref, out_ref, out16_ref):
        @pl.when(pl.program_id(1) == 0)
        def _():
            out_ref[...] = jnp.zeros((k_dim, tn), F32)

        out_ref[...] += _dot_tn(a_ref[...].astype(BF16), b_ref[...].astype(BF16))

        @pl.when(pl.program_id(1) == t_len // tt - 1)
        def _():
            out16_ref[...] = out_ref[...].astype(BF16)

    if col_blocks == 1:
        out_spec = pl.BlockSpec((k_dim, tn), lambda n, t: (0, n))
        shape = (k_dim, n_dim)
    else:
        out_spec = pl.BlockSpec((None, k_dim, tn), lambda n, t: (n // per_block, 0, n % per_block))
        shape = (col_blocks, k_dim, n_dim // col_blocks)
    outs, carried = _carried_call(
        body, name, (n_dim // tn, t_len // tt),
        in_specs=[pl.BlockSpec((tt, k_dim), lambda n, t: (t, 0)), pl.BlockSpec((tt, tn), lambda n, t: (t, n))],
        out_specs=[out_spec, out_spec], out_shape=[jax.ShapeDtypeStruct(shape, F32), jax.ShapeDtypeStruct(shape, BF16)],
        scratch_shapes=[], semantics=("arbitrary", "arbitrary"), args=(a, b), carry=carry)
    return (tuple(outs), carried) if carry is not None else tuple(outs)


def grad_w_pool(pooled, dpp):
    t_len = pooled.shape[0]
    tt = min(1024, t_len)
    steps = t_len // tt

    def body(a_ref, b_ref, out_ref, out16_ref):
        @pl.when(pl.program_id(0) == 0)
        def _():
            out_ref[...] = jnp.zeros((GROUPS, GROUP_DIM, GROUP_DIM), F32)

        for g in range(GROUPS):
            cols = slice(g * GROUP_DIM, (g + 1) * GROUP_DIM)
            out_ref[g] += _dot_tn(a_ref[:, cols], b_ref[:, cols])

        @pl.when(pl.program_id(0) == steps - 1)
        def _():
            out16_ref[...] = out_ref[...].astype(BF16)

    shape = (GROUPS, GROUP_DIM, GROUP_DIM)
    return pl.pallas_call(
        body, name="grad_w_pool", grid=(steps,),
        in_specs=[_rows(tt, D), _rows(tt, D)], out_specs=[_const(shape), _const(shape)],
        out_shape=[jax.ShapeDtypeStruct(shape, F32), jax.ShapeDtypeStruct(shape, BF16)],
        compiler_params=_params(("arbitrary",)),
    )(pooled, dpp)


def _place():
    x, y, c = lax.axis_index("x"), lax.axis_index("y"), lax.axis_index("c")
    return x, y, c


def _other_chips(x, y):
    return [(1 - x, y), (x, 1 - y), (1 - x, 1 - y)]


def _any_specs(n):
    return [pl.BlockSpec(memory_space=pl.ANY)] * n


def exchange_call(name, exchange_cls, arrays):
    n = len(arrays)
    shapes = exchange_cls.out_shapes(arrays)

    def body(*refs):
        exchange = exchange_cls(refs[:n], refs[n:n + len(shapes)], *refs[n + len(shapes):])
        exchange.start()
        if hasattr(exchange, "middle"):
            exchange.middle()
        exchange.finish()

    res = pl.pallas_call(
        body, name=name, in_specs=_any_specs(n), out_specs=_any_specs(len(shapes)),
        out_shape=shapes, scratch_shapes=exchange_cls.sems(n),
    )(*arrays)
    return res[:n]


class _Gather:
    SLOTS = 8
    ROW_ALIGN = 16
    MIDDLE_AT = (5, 8)

    @staticmethod
    def out_shapes(shards):
        return [jax.ShapeDtypeStruct((N_CHIPS,) + s.shape, s.dtype) for s in shards]

    @staticmethod
    def sems(n):
        return [pltpu.SemaphoreType.DMA((n, _Gather.SLOTS)), pltpu.SemaphoreType.DMA((n, _Gather.SLOTS))]

    def __init__(self, ins, outs, send_sems, recv_sems):
        self.ins, self.outs, self.send_sems, self.recv_sems = ins, outs, send_sems, recv_sems
        x, y, c = _place()
        self.c, self.me, self.sibling = c, 2 * x + y, (x, y, 1 - c)
        self.across = [(1 - x, y), (x, 1 - y), (1 - x, 1 - y)]

    def _rows(self, a, which, part=None):
        half = self.ins[a].shape[1] // 2
        first = (half // 2) // self.ROW_ALIGN * self.ROW_ALIGN
        if part is None:
            return pl.ds(which * half, half)
        return pl.ds(which * half, first) if part == 0 else pl.ds(which * half + first, half - first)

    def _has_part(self, a, part):
        half = self.ins[a].shape[1] // 2
        return part == 1 or (half // 2) // self.ROW_ALIGN > 0

    def _block(self, a, chip, rows=slice(None)):
        index = chip if not isinstance(chip, tuple) else 2 * chip[0] + chip[1]
        if len(self.outs[a].shape) == len(self.ins[a].shape):
            cols = self.ins[a].shape[2]
            return self.outs[a].at[:, rows, pl.ds(pl.multiple_of(index * cols, cols), cols)]
        return self.outs[a].at[index, :, rows, :]

    def _remote(self, src, dst, a, slot, to):
        return pltpu.make_async_remote_copy(src_ref=src, dst_ref=dst, send_sem=self.send_sems.at[a, slot],
                                            recv_sem=self.recv_sems.at[a, slot], device_id=to, device_id_type=MESH)

    def _own(self, a):
        return self._remote(self.ins[a], self._block(a, self.me), a, 6, self.sibling)

    def _sent(self, a, axis):
        rows = self._rows(a, self.c)
        return self._remote(self.ins[a].at[:, rows, :], self._block(a, self.me, rows), a, axis,
                            (*self.across[axis], self.c))

    def _landed(self, a, axis):
        block = self._block(a, self.across[axis], self._rows(a, self.c))
        return self._remote(block, block, a, axis, (*self.across[axis], self.c))

    def _relayed(self, a, part, incoming):
        source = self.across[2] if incoming else self.across[part]
        block = self._block(a, source, self._rows(a, self.c, part))
        return self._remote(block, block, a, (2, 7)[part], (*self.across[1 - part], self.c))

    def _passed_on(self, a, source, which):
        block = self._block(a, self.across[source], self._rows(a, which))
        return self._remote(block, block, a, 3 + source, self.sibling)

    def start(self):
        for a in range(len(self.ins)):
            self._own(a).start()
        for a in range(len(self.ins)):
            for axis in range(2):
                self._sent(a, axis).start()

    def middle(self):
        for a in range(len(self.ins)):
            for axis in range(2):
                self._landed(a, axis).wait_recv()
                if self._has_part(a, axis):
                    self._relayed(a, axis, incoming=False).start()
                self._passed_on(a, axis, self.c).start()

    def finish(self):
        n = len(self.ins)
        for a in range(n):
            for part in range(2):
                if self._has_part(a, part):
                    self._relayed(a, part, incoming=True).wait_recv()
            self._passed_on(a, 2, self.c).start()
        for a in range(n):
            for source in range(3):
                self._passed_on(a, source, 1 - self.c).wait_recv()
            self._own(a).wait_recv()
        for a in range(n):
            self._own(a).wait_send()
            for axis in range(2):
                self._sent(a, axis).wait_send()
                if self._has_part(a, axis):
                    self._relayed(a, axis, incoming=False).wait_send()
            for source in range(3):
                self._passed_on(a, source, self.c).wait_send()


class _HalfExchange:
    @staticmethod
    def out_shapes(grads):
        return [jax.ShapeDtypeStruct((N_CHIPS, g.shape[1] // 2, g.shape[2]), g.dtype) for g in grads]

    @staticmethod
    def sems(n):
        return [pltpu.SemaphoreType.DMA((n,)), pltpu.SemaphoreType.DMA((n,))]

    def __init__(self, ins, outs, send_sems, recv_sems):
        self.ins, self.outs, self.send_sems, self.recv_sems = ins, outs, send_sems, recv_sems

    def _copies(self):
        x, y, c = _place()
        for a in range(len(self.ins)):
            h = self.ins[a].shape[1] // 2
            yield pltpu.make_async_remote_copy(
                src_ref=self.ins[a].at[:, pl.ds((1 - c) * h, h), :], dst_ref=self.outs[a],
                send_sem=self.send_sems.at[a], recv_sem=self.recv_sems.at[a], device_id=(x, y, 1 - c), device_id_type=MESH)

    def start(self):
        for cp in self._copies():
            cp.start()

    def finish(self):
        for cp in self._copies():
            cp.wait()


class _GatherColumns(_Gather):
    @staticmethod
    def out_shapes(shards):
        return [jax.ShapeDtypeStruct(s.shape[:2] + (N_CHIPS * s.shape[2],), s.dtype) for s in shards]


def _row_tile(rows, cols, budget=2 << 20):
    best = 16
    for tr in range(16, rows + 1, 16):
        if rows % tr == 0 and tr * cols * 4 <= budget:
            best = tr
    return best


def add_sibling_half(name, grad, got, core):
    _, rows, cols = grad.shape
    h = rows // 2
    tr = _row_tile(h, cols)
    per_half = h // tr

    def body(core_ref, g_ref, o_ref, out_ref, out16_ref):
        total = g_ref[...] + o_ref[...].astype(F32)
        out_ref[...] = total
        out16_ref[...] = total.astype(BF16)

    out_spec = pl.BlockSpec((None, tr, cols), lambda j, r, cr: (j, r, 0))
    return pl.pallas_call(
        body, name=name,
        grid_spec=pltpu.PrefetchScalarGridSpec(
            num_scalar_prefetch=1, grid=(N_CHIPS, per_half),
            in_specs=[pl.BlockSpec((None, tr, cols), lambda j, r, cr: (j, cr[0] * per_half + r, 0)),
                      pl.BlockSpec((None, tr, cols), lambda j, r, cr: (j, r, 0))],
            out_specs=[out_spec, out_spec]),
        out_shape=[jax.ShapeDtypeStruct((N_CHIPS, h, cols), F32), jax.ShapeDtypeStruct((N_CHIPS, h, cols), BF16)],
        compiler_params=_params(("arbitrary", "arbitrary")),
    )(core, grad, got)


class _ChipExchange:
    SLOTS = 6
    ROW_ALIGN = 16
    MIDDLE_AT = (3, 8)

    @staticmethod
    def out_shapes(partials):
        return ([jax.ShapeDtypeStruct((3,) + p.shape[1:], p.dtype) for p in partials]
                + [jax.ShapeDtypeStruct(p.shape[1:], p.dtype) for p in partials])

    @staticmethod
    def sems(n):
        return [pltpu.SemaphoreType.DMA((n, _ChipExchange.SLOTS)), pltpu.SemaphoreType.DMA((n, _ChipExchange.SLOTS))]

    def __init__(self, ins, outs, send_sems, recv_sems):
        n = len(ins)
        self.ins, self.outs, self.relays, self.send_sems, self.recv_sems = ins, outs[:n], outs[n:], send_sems, recv_sems

    def _copy(self, a, slot):
        x, y, c = _place()
        across = _other_chips(x, y)
        rows = self.ins[a].shape[1]
        first = (rows // 2) // self.ROW_ALIGN * self.ROW_ALIGN
        part = (pl.ds(0, first), pl.ds(first, rows - first))
        if slot < 2:
            px, py = across[slot]
            src, dst, to = self.ins[a].at[2 * px + py], self.outs[a].at[slot], across[slot]
        elif slot < 4:
            px, py = across[2]
            src, dst, to = self.ins[a].at[2 * px + py, part[slot - 2], :], self.relays[a].at[part[slot - 2], :], across[slot - 2]
        else:
            src, dst, to = self.relays[a].at[part[slot - 4], :], self.outs[a].at[2, part[slot - 4], :], across[5 - slot]
        return pltpu.make_async_remote_copy(src_ref=src, dst_ref=dst, send_sem=self.send_sems.at[a, slot],
                                            recv_sem=self.recv_sems.at[a, slot], device_id=(*to, c), device_id_type=MESH)

    def start(self):
        for slot in (2, 3, 0, 1):
            for a in range(len(self.ins)):
                self._copy(a, slot).start()

    def middle(self):
        for part in range(2):
            for a in range(len(self.ins)):
                self._copy(a, 2 + part).wait_recv()
                self._copy(a, 4 + part).start()

    def finish(self):
        for a in range(len(self.ins)):
            for slot in (0, 1, 4, 5):
                self._copy(a, slot).wait_recv()
        for a in range(len(self.ins)):
            for slot in range(self.SLOTS):
                self._copy(a, slot).wait_send()


def add_chip_blocks(name, partial, got, chip):
    _, h, cols = partial.shape
    tr = _row_tile(h, cols)

    def body(chip_ref, p_ref, g0_ref, g1_ref, g2_ref, out_ref):
        out_ref[...] = ((p_ref[...] + g0_ref[...].astype(F32)) + g1_ref[...].astype(F32)) + g2_ref[...].astype(F32)

    def got_spec(slot):
        return pl.BlockSpec((None, tr, cols), lambda r, ch: (slot, r, 0))

    return pl.pallas_call(
        body, name=name,
        grid_spec=pltpu.PrefetchScalarGridSpec(
            num_scalar_prefetch=1, grid=(h // tr,),
            in_specs=[pl.BlockSpec((None, tr, cols), lambda r, ch: (ch[0], r, 0)), got_spec(0), got_spec(1), got_spec(2)],
            out_specs=pl.BlockSpec((tr, cols), lambda r, ch: (r, 0))),
        out_shape=jax.ShapeDtypeStruct((h, cols), F32),
        compiler_params=_params(("arbitrary",)),
    )(chip, partial, got, got, got)


class _SiblingSwap:
    @staticmethod
    def out_shapes(halves):
        return [jax.ShapeDtypeStruct(v.shape, v.dtype) for v in halves]

    @staticmethod
    def sems(n):
        return [pltpu.SemaphoreType.DMA((n,)), pltpu.SemaphoreType.DMA((n,))]

    def __init__(self, ins, outs, send_sems, recv_sems):
        self.ins, self.outs, self.send_sems, self.recv_sems = ins, outs, send_sems, recv_sems

    def _copies(self):
        x, y, c = _place()
        for a in range(len(self.ins)):
            yield pltpu.make_async_remote_copy(
                src_ref=self.ins[a], dst_ref=self.outs[a], send_sem=self.send_sems.at[a], recv_sem=self.recv_sems.at[a],
                device_id=(x, y, 1 - c), device_id_type=MESH)

    def start(self):
        for cp in self._copies():
            cp.start()

    def finish(self):
        for cp in self._copies():
            cp.wait()


def all_reduce_small(pack):
    rows = pack.shape[0]

    def body(pack_ref, out_ref, gathered, send_sems, recv_sems):
        x, y, c = _place()
        me = 4 * x + 2 * y + c
        gathered[me] = pack_ref[...]
        copies = []
        for rel in range(1, 8):
            fx, fy, fc = (rel >> 2) & 1, (rel >> 1) & 1, rel & 1
            peer = (x ^ fx, y ^ fy, c ^ fc)
            cp = pltpu.make_async_remote_copy(
                src_ref=pack_ref, dst_ref=gathered.at[me], send_sem=send_sems.at[rel - 1], recv_sem=recv_sems.at[rel - 1],
                device_id=peer, device_id_type=MESH)
            cp.start()
            copies.append(cp)
        for rel in range(1, 8):
            fx, fy, fc = (rel >> 2) & 1, (rel >> 1) & 1, rel & 1
            src = 4 * (x ^ fx) + 2 * (y ^ fy) + (c ^ fc)
            pltpu.make_async_remote_copy(
                src_ref=pack_ref, dst_ref=gathered.at[src], send_sem=send_sems.at[rel - 1], recv_sem=recv_sems.at[rel - 1],
                device_id=(x, y, c), device_id_type=MESH).wait_recv()
        for cp in copies:
            cp.wait_send()
        total = gathered[0]
        for dev in range(1, 8):
            total = total + gathered[dev]
        out_ref[...] = total

    return pl.pallas_call(
        body, name="all_reduce_small",
        in_specs=[pl.BlockSpec(memory_space=pltpu.VMEM)], out_specs=pl.BlockSpec(memory_space=pltpu.VMEM),
        out_shape=jax.ShapeDtypeStruct((rows, D), F32),
        scratch_shapes=[pltpu.VMEM((8, rows, D), F32), pltpu.SemaphoreType.DMA((7,)), pltpu.SemaphoreType.DMA((7,))],
    )(pack)


def _adamw_update(w, g, m, v):
    nm = ADAM_B1 * m + (1.0 - ADAM_B1) * g
    nv = ADAM_B2 * v + (1.0 - ADAM_B2) * (g * g)
    m_hat = nm / (1.0 - ADAM_B1 ** ADAM_STEP)
    v_hat = nv / (1.0 - ADAM_B2 ** ADAM_STEP)
    delta = -ADAM_LR * (m_hat / (jnp.sqrt(v_hat) + ADAM_EPS) + ADAM_WD * w)
    return delta, nm, nv


def adamw(name, w, g, m, v):
    def body(w_ref, g_ref, m_ref, v_ref, d_ref, nm_ref, nv_ref):
        d_ref[...], nm_ref[...], nv_ref[...] = _adamw_update(w_ref[...], g_ref[...], m_ref[...], v_ref[...])

    out = jax.ShapeDtypeStruct(w.shape, F32)
    return pl.pallas_call(body, name=name, out_shape=[out] * 3, compiler_params=_params())(w, g, m, v)


def adamw_halves(name, w, mine, theirs, m, v, core):
    h, cols = mine.shape
    tr = _row_tile(h, cols)
    per_half = h // tr

    def body(core_ref, w_ref, mine_ref, theirs_ref, m_ref, v_ref, g_ref, d_ref, nm_ref, nv_ref):
        g = jnp.where(pl.program_id(0) == core_ref[0], mine_ref[...], theirs_ref[...])
        g_ref[...] = g
        d_ref[...], nm_ref[...], nv_ref[...] = _adamw_update(w_ref[...], g, m_ref[...], v_ref[...])

    full = pl.BlockSpec((tr, cols), lambda hh, r, cr: (hh * per_half + r, 0))
    mine_spec = pl.BlockSpec((tr, cols), lambda hh, r, cr: (jnp.where(hh == cr[0], r, 0), 0))
    theirs_spec = pl.BlockSpec((tr, cols), lambda hh, r, cr: (jnp.where(hh == cr[0], 0, r), 0))
    out = jax.ShapeDtypeStruct((2 * h, cols), F32)
    return pl.pallas_call(
        body, name=name,
        grid_spec=pltpu.PrefetchScalarGridSpec(
            num_scalar_prefetch=1, grid=(2, per_half),
            in_specs=[full, mine_spec, theirs_spec, full, full], out_specs=[full] * 4),
        out_shape=[out] * 4,
        compiler_params=_params(("arbitrary", "arbitrary")),
    )(core, w, mine, theirs, m, v)


BIG = ("w_in", "w_conv_out", "w_pool", "w_kv", "w_xattn_out", "w_out", "w_gate", "w_up", "w_down")


def kernel(x, mem, norm_mix, w_in, conv_w, w_conv_out, w_pool, pool_scale, norm_mem, w_kv, w_xattn_out, w_out, norm_ffn, w_gate, w_up, w_down, norm_final, loss_target, m_norm_mix, m_w_in, m_conv_w, m_w_conv_out, m_w_pool, m_pool_scale, m_norm_mem, m_w_kv, m_w_xattn_out, m_w_out, m_norm_ffn, m_w_gate, m_w_up, m_w_down, m_norm_final, v_norm_mix, v_w_in, v_conv_w, v_w_conv_out, v_w_pool, v_pool_scale, v_norm_mem, v_w_kv, v_w_xattn_out, v_w_out, v_norm_ffn, v_w_gate, v_w_up, v_w_down, v_norm_final):
    t_len = x.shape[1]
    xi, yi, ci = lax.axis_index("x"), lax.axis_index("y"), lax.axis_index("c")
    chip = 2 * xi + yi
    chip_arr = jnp.reshape(chip, (1,)).astype(jnp.int32)
    core_arr = jnp.reshape(ci, (1,)).astype(jnp.int32)

    conv_pad = jnp.concatenate([conv_w, jnp.zeros((1, 13, 256), F32)], axis=1)
    def t2(w):
        return jnp.swapaxes(w, 1, 2)

    (g_in,) = exchange_call("gather_w_in", _GatherColumns, [w_in.astype(BF16)])
    w_in_f = g_in[0]

    x2d = x[0]
    tgt = loss_target[0]
    (proj, h), (g_kv, g_conv_w, g_co, g_xo, g_out, g_pool, g_gate) = proj_fwd(
        x2d, norm_mix, w_in_f,
        carry=(_Gather, [w_kv.astype(BF16), conv_pad,
                         w_conv_out.astype(BF16), w_xattn_out.astype(BF16), w_out.astype(BF16),
                         w_pool[0].astype(BF16),
                         t2(w_gate).astype(BF16)]))
    w_kv4 = g_kv.reshape(N_CHIPS, D, D // 2)
    conv_full = jnp.transpose(g_conv_w[:, 0, 0:8, :], (1, 0, 2)).reshape(8, D)
    w_co_f, w_xo_f, w_out_f = g_co.reshape(D, D), g_xo.reshape(D, D), g_out.reshape(D, D)
    w_pool_f = jnp.transpose(g_pool, (1, 0, 2, 3)).reshape(GROUPS, GROUP_DIM, GROUP_DIM)
    memn, k, v = kv_fwd(mem[0], norm_mem, w_kv4)
    (a, pooled, ya, pp, yx, o, probs, x1), (g_up, g_down) = mixer_fwd(
        proj, x2d, conv_full, w_co_f, w_pool_f, pool_scale, k, v, w_xo_f, w_out_f,
        carry=(_Gather, [t2(w_up).astype(BF16), w_down.astype(BF16)]))
    w_gate_f, w_up_f, w_down_f = g_gate.reshape(D_FF, D), g_up.reshape(D_FF, D), g_down.reshape(D_FF, D)
    gate, up, dx2, stat_f = ffn_fwd(x1, tgt, norm_ffn, w_gate_f, w_up_f, w_down_f, norm_final.reshape(1, D))

    def by_chip(pair):
        return tuple(gw.reshape(N_CHIPS, gw.shape[0] // N_CHIPS, gw.shape[1]) for gw in pair)

    def chip_partials(names, grads, got):
        return [add_sibling_half("add_sibling_" + n, g32, o_, core_arr) for n, (g32, _), o_ in zip(names, grads, got)]

    def chip_sums(names, partials, got):
        return [add_chip_blocks("add_chips_" + n, p32, g2, chip_arr) for n, (p32, _), g2 in zip(names, partials, got)]

    dx1, dgate, dup, act, h2, stat_b1 = ffn_bwd(dx2, x1, gate, up, norm_ffn, w_gate_f, w_up_f, w_down_f)
    gw_gate = by_chip(matmul_tn("grad_w_gate", dgate, h2, 512))
    gw_up, got_gate = matmul_tn("grad_w_up", dup, h2, 512, carry=(_HalfExchange, [gw_gate[1]]))
    gw_up = by_chip(gw_up)
    gw_down, got_up = matmul_tn("grad_w_down", act, dx2, 512, carry=(_HalfExchange, [gw_up[1]]))
    gw_down = by_chip(gw_down)
    got_down = exchange_call("exchange_halves_w_down", _HalfExchange, [gw_down[1]])
    names_ffn = ("w_gate", "w_up", "w_down")
    part_ffn = chip_partials(names_ffn, [gw_gate, gw_up, gw_down], list(got_gate) + list(got_up) + list(got_down))

    (dproj, merged, dya, dpp, dyx, dk, dv, stat_b2), got_ffn = mixer_bwd(
        dx1, proj, ya, pp, yx, probs, conv_full, w_co_f, w_pool_f, pool_scale, k, v, w_xo_f, w_out_f,
        carry=(_ChipExchange, [p16 for _, p16 in part_ffn]))
    gw_kv32, gw_kv16, stat_kv = kv_bwd(dk, dv, memn, mem[0], norm_mem, w_kv4)
    gw_pool = tuple(jnp.transpose(gw.reshape(GROUPS, N_CHIPS, 64, GROUP_DIM), (1, 0, 2, 3)).reshape(N_CHIPS, 256, GROUP_DIM)
                    for gw in grad_w_pool(pooled, dpp))
    gw_co, got_kv_pool = matmul_tn("grad_w_conv_out", a, dya, 1024, carry=(_HalfExchange, [gw_kv16, gw_pool[1]]))
    gw_co = by_chip(gw_co)
    gw_xo, got_co = matmul_tn("grad_w_xattn_out", o, dyx, 1024, carry=(_HalfExchange, [gw_co[1]]))
    gw_xo = by_chip(gw_xo)
    gw_out, got_xo = matmul_tn("grad_w_out", merged, dx1, 1024, carry=(_HalfExchange, [gw_xo[1]]))
    gw_out = by_chip(gw_out)
    got_out = exchange_call("exchange_halves_w_out", _HalfExchange, [gw_out[1]])
    names_mix = ("w_kv", "w_pool", "w_conv_out", "w_xattn_out", "w_out")
    part_mix = chip_partials(names_mix, [(gw_kv32, gw_kv16), gw_pool, gw_co, gw_xo, gw_out],
                             list(got_kv_pool) + list(got_co) + list(got_xo) + list(got_out))

    gw_in, got_mix = matmul_tn("grad_w_in", h, dproj, 2048, col_blocks=N_CHIPS,
                               carry=(_ChipExchange, [p16 for _, p16 in part_mix]))
    part_in = chip_partials(("w_in",), [gw_in], exchange_call("exchange_halves_w_in", _HalfExchange, [gw_in[1]]))
    mine_early = chip_sums(names_ffn + names_mix, part_ffn + part_mix, list(got_ffn) + list(got_mix))
    (grad_x, stat_b3), (got_in, theirs_early) = in_bwd(
        dproj, w_in_f, x2d, dx1, norm_mix,
        carry=[(_ChipExchange, [p16 for _, p16 in part_in]), (_SiblingSwap, mine_early)])
    mine_in = chip_sums(("w_in",), part_in, got_in)
    theirs_in = exchange_call("swap_halves_w_in", _SiblingSwap, mine_in)
    reduced = dict(zip(names_ffn + names_mix + ("w_in",),
                       zip(mine_early + mine_in, list(theirs_early) + list(theirs_in))))
    mine = [reduced[n][0] for n in BIG]
    theirs = [reduced[n][1] for n in BIG]

    pack = jnp.concatenate([stat_b3[0:1], stat_b2[1:2], stat_kv[0:1], stat_b1[0:1], stat_f[0:1], stat_b2[5:8],
                            stat_f[1:2], jnp.zeros((7, D), F32)], axis=0)
    total = all_reduce_small(pack)
    loss = jnp.sum(total[8])
    g_conv_full = total[5:8]
    g_conv = lax.dynamic_slice_in_dim(g_conv_full, chip * 256, 256, axis=1)

    given = dict(w_in=(w_in, m_w_in, v_w_in), w_conv_out=(w_conv_out, m_w_conv_out, v_w_conv_out),
                 w_pool=(w_pool, m_w_pool, v_w_pool), w_kv=(w_kv, m_w_kv, v_w_kv),
                 w_xattn_out=(w_xattn_out, m_w_xattn_out, v_w_xattn_out), w_out=(w_out, m_w_out, v_w_out),
                 w_gate=(w_gate, m_w_gate, v_w_gate), w_up=(w_up, m_w_up, v_w_up), w_down=(w_down, m_w_down, v_w_down))
    out_g, out_d, out_m, out_v = {}, {}, {}, {}
    for n, mine_n, theirs_n in zip(BIG, mine, theirs):
        transposed = n in ("w_gate", "w_up")
        rows2d = (2 * mine_n.shape[0], mine_n.shape[1])
        w_, m_, v_ = ((t2(t) if transposed else t).reshape(rows2d) for t in given[n])
        res = adamw_halves("adamw_" + n, w_, mine_n, theirs_n, m_, v_, core_arr)
        if transposed:
            res = [t2(t.reshape(1, D_FF // N_CHIPS, D)) for t in res]
        out_g[n], out_d[n], out_m[n], out_v[n] = (t.reshape(given[n][0].shape) for t in res)

    def small_pack(vals, conv_part):
        conv_rows = jnp.concatenate([conv_part.reshape(3, 256), jnp.zeros((3, D - 256), F32)], axis=1)
        return jnp.concatenate([val.reshape(1, D) for val in vals] + [conv_rows], axis=0)

    sw = small_pack([norm_mix, pool_scale, norm_mem, norm_ffn, norm_final], conv_w)
    sm = small_pack([m_norm_mix, m_pool_scale, m_norm_mem, m_norm_ffn, m_norm_final], m_conv_w)
    sv = small_pack([v_norm_mix, v_pool_scale, v_norm_mem, v_norm_ffn, v_norm_final], v_conv_w)
    sg = small_pack([total[r] for r in range(5)], g_conv)
    sd, snm, snv = adamw("adamw_small", sw, sg, sm, sv)
    small_names = ("norm_mix", "pool_scale", "norm_mem", "norm_ffn", "norm_final")
    small_shapes = dict(norm_mix=(1, D), pool_scale=(1, D), norm_mem=(1, D), norm_ffn=(1, D), norm_final=(D,))
    for r, n in enumerate(small_names):
        out_g[n], out_d[n], out_m[n], out_v[n] = (t[r].reshape(small_shapes[n]) for t in (sg, sd, snm, snv))
    out_g["conv_w"], out_d["conv_w"], out_m["conv_w"], out_v["conv_w"] = (
        t[5:8, 0:256].reshape(1, 3, 256) for t in (sg, sd, snm, snv))

    order = ("norm_mix", "w_in", "conv_w", "w_conv_out", "w_pool", "pool_scale", "norm_mem", "w_kv", "w_xattn_out",
             "w_out", "norm_ffn", "w_gate", "w_up", "w_down", "norm_final")
    return (loss, grad_x.reshape(1, t_len, D), *[out_g[n] for n in order], *[out_d[n] for n in order],
            *[out_m[n] for n in order], *[out_v[n] for n in order])
```

```python
import functools

import jax
import jax.numpy as jnp
from jax import lax
from jax.experimental import pallas as pl
from jax.experimental.pallas import tpu as pltpu

F32 = jnp.float32
BF16 = jnp.bfloat16
MESH = pl.DeviceIdType.MESH

D = 1024
N_MEM = 256
HEADS = 4
HEAD_DIM = 256
GROUPS = 4
GROUP_DIM = 256
POOL_WINDOWS = (2, 4, 8, 16)
D_FF = 2816
D_IN = 8192
N_CHIPS = 4
EPS = 1e-6
HALO = 16
POOL_PAD = 128
ATT_SCALE = HEAD_DIM ** -0.5

ADAM_LR = 0.001
ADAM_B1 = 0.9
ADAM_B2 = 0.999
ADAM_EPS = 1e-08
ADAM_WD = 0.01
ADAM_STEP = 10

VMEM_LIMIT = 56 * 1024 * 1024

O_BA, O_CA, O_UA, O_UP, O_QX, O_GA, O_GP, O_GX = (k * D for k in range(8))

NT_DIMS = (((1,), (1,)), ((), ()))
TN_DIMS = (((0,), (0,)), ((), ()))


def _dot(a, b):
    return jnp.dot(a, b, preferred_element_type=F32)


def _dot_nt(a, b):
    return lax.dot_general(a, b, NT_DIMS, preferred_element_type=F32)


def _dot_tn(a, b):
    return lax.dot_general(a, b, TN_DIMS, preferred_element_type=F32)


def _sigmoid(z):
    return pl.reciprocal(1.0 + jnp.exp(-z), approx=True)


def _params(semantics=None):
    return pltpu.CompilerParams(dimension_semantics=semantics, vmem_limit_bytes=VMEM_LIMIT)


def _resident(shape):
    zeros = (0,) * len(shape)
    return pl.BlockSpec(shape, lambda *_: zeros, pipeline_mode=pl.Buffered(1))


def _const(shape):
    zeros = (0,) * len(shape)
    return pl.BlockSpec(shape, lambda *_: zeros)


def _rows(tm, width):
    return pl.BlockSpec((tm, width), lambda i: (i, 0))


def _inv_count(tile, tm, window):
    t = tile * tm + lax.broadcasted_iota(jnp.int32, (tm, 1), 0)
    return 1.0 / jnp.minimum(t + 1, window).astype(F32)


def _carried_call(body, name, grid, in_specs, out_specs, out_shape, scratch_shapes, semantics, args, carry):
    if carry is None:
        res = pl.pallas_call(body, name=name, grid=grid, in_specs=in_specs, out_specs=out_specs, out_shape=out_shape,
                             scratch_shapes=scratch_shapes, compiler_params=_params(semantics))(*args)
        return res, []
    carries = [carry] if isinstance(carry, tuple) else list(carry)
    comm_args = [arr for _, arrs in carries for arr in arrs]
    n, n_in, n_out, n_scratch = len(comm_args), len(in_specs), len(out_specs), len(scratch_shapes)
    shapes_of = [cls.out_shapes(arrs) for cls, arrs in carries]
    comm_shapes = [s for shapes in shapes_of for s in shapes]
    m = len(comm_shapes)
    comm_sems = [s for cls, arrs in carries for s in cls.sems(len(arrs))]

    def carrying(*refs):
        ins, comm_ins = refs[:n_in], refs[n_in:n_in + n]
        outs, comm_outs = refs[n_in + n:n_in + n + n_out], refs[n_in + n + n_out:n_in + n + n_out + m]
        scratch, sems = refs[n_in + n + n_out + m:n_in + n + n_out + m + n_scratch], refs[n_in + n + n_out + m + n_scratch:]
        steps = [pl.program_id(d) for d in range(len(grid))]
        first = functools.reduce(jnp.logical_and, [s == 0 for s in steps])
        last = functools.reduce(jnp.logical_and, [s == g - 1 for s, g in zip(steps, grid)])

        def exchanges():
            at_in = at_out = 0
            for k, (cls, arrs) in enumerate(carries):
                yield cls(comm_ins[at_in:at_in + len(arrs)], comm_outs[at_out:at_out + len(shapes_of[k])],
                          sems[2 * k], sems[2 * k + 1])
                at_in, at_out = at_in + len(arrs), at_out + len(shapes_of[k])

        @pl.when(first)
        def _():
            for exchange in exchanges():
                exchange.start()

        linear, total = 0, 1
        for s, g in zip(steps, grid):
            linear, total = linear * g + s, total * g
        for k, (cls, _) in enumerate(carries):
            if hasattr(cls, "middle"):
                @pl.when(linear == (cls.MIDDLE_AT[0] * total) // cls.MIDDLE_AT[1])
                def _(k=k):
                    list(exchanges())[k].middle()

        body(*ins, *outs, *scratch)

        @pl.when(last)
        def _():
            for exchange in exchanges():
                exchange.finish()

    res = pl.pallas_call(
        carrying, name=name, grid=grid, in_specs=list(in_specs) + _any_specs(n), out_specs=list(out_specs) + _any_specs(m),
        out_shape=list(out_shape) + comm_shapes, scratch_shapes=list(scratch_shapes) + comm_sems,
        compiler_params=_params(semantics))(*args, *comm_args)
    comm_res, at = [], n_out
    for (_, arrs), shapes in zip(carries, shapes_of):
        comm_res.append(res[at:at + len(arrs)])
        at += len(shapes)
    return res[:n_out], (comm_res[0] if isinstance(carry, tuple) else comm_res)


def proj_fwd(x, g_mix, w_in, carry=None):
    t_len = x.shape[0]
    tm = min(1024, t_len)
    tn = D_IN // N_CHIPS

    def body(x_ref, g_ref, w_ref, proj_ref, h_ref):
        @pl.when(pl.program_id(1) == 0)
        def _():
            xv = x_ref[...]
            r = lax.rsqrt(jnp.mean(xv * xv, axis=-1, keepdims=True) + EPS)
            h_ref[...] = (xv * r * g_ref[...]).astype(BF16)

        proj_ref[...] = _dot(h_ref[...], w_ref[...]).astype(BF16)

    return _carried_call(
        body, "proj_fwd", (t_len // tm, N_CHIPS),
        in_specs=[pl.BlockSpec((tm, D), lambda i, j: (i, 0)),
                  pl.BlockSpec((1, D), lambda i, j: (0, 0)),
                  pl.BlockSpec((D, tn), lambda i, j: (0, j))],
        out_specs=[pl.BlockSpec((tm, tn), lambda i, j: (i, j)),
                   pl.BlockSpec((tm, D), lambda i, j: (i, 0))],
        out_shape=[jax.ShapeDtypeStruct((t_len, D_IN), BF16), jax.ShapeDtypeStruct((t_len, D), BF16)],
        scratch_shapes=[], semantics=("arbitrary", "arbitrary"), args=(x, g_mix, w_in), carry=carry)


def kv_fwd(mem, g_mem, w_kv4):
    half = D // 2

    def body(mem_ref, g_ref, w_ref, memn_ref, k_ref, v_ref):
        mv = mem_ref[...]
        r = lax.rsqrt(jnp.mean(mv * mv, axis=-1, keepdims=True) + EPS)
        mn = (mv * r * g_ref[...]).astype(BF16)
        memn_ref[...] = mn
        k_ref[:, 0:half] = _dot(mn, w_ref[0]).astype(BF16)
        k_ref[:, half:D] = _dot(mn, w_ref[1]).astype(BF16)
        v_ref[:, 0:half] = _dot(mn, w_ref[2]).astype(BF16)
        v_ref[:, half:D] = _dot(mn, w_ref[3]).astype(BF16)

    out = jax.ShapeDtypeStruct((N_MEM, D), BF16)
    return pl.pallas_call(body, name="kv_fwd", out_shape=[out, out, out], compiler_params=_params())(mem, g_mem, w_kv4)


def _softmax_rows(s):
    m = jnp.max(s, axis=-1, keepdims=True)
    e = jnp.exp(s - m)
    return e * pl.reciprocal(jnp.sum(e, axis=-1, keepdims=True), approx=True)


def _window_bands(tm, causal):
    t = lax.broadcasted_iota(jnp.int32, (tm, tm + POOL_PAD), 0)
    s = lax.broadcasted_iota(jnp.int32, (tm, tm + POOL_PAD), 1)
    d = (t + POOL_PAD - s) if causal else (s - t)
    return jnp.stack([((d >= 0) & (d < w)).astype(BF16) for w in POOL_WINDOWS])


def mixer_fwd(proj, x, conv_w8, w_co, w_pool, pool_scale, k, v, w_xo, w_out, carry=None):
    t_len = x.shape[0]
    tm = min(256, t_len)

    def body(proj_ref, x_ref, cw_ref, wco_ref, wpool_ref, ps_ref, k_ref, v_ref, wxo_ref, wout_ref,
             a_ref, pooled_ref, ya_ref, pp_ref, yx_ref, o_ref, p_ref, x1_ref, cu_ext, up_ext):
        i = pl.program_id(0)

        @pl.when(i == 0)
        def _():
            cu_ext[0:HALO, :] = jnp.zeros((HALO, D), F32)
            up_ext[0:HALO, :] = jnp.zeros((HALO, D), F32)

        cu = proj_ref[:, O_CA:O_CA + D].astype(F32) * proj_ref[:, O_UA:O_UA + D].astype(F32)
        cu_ext[HALO:HALO + tm, :] = cu
        conv = (cw_ref[2:3, :] * cu + cw_ref[1:2, :] * cu_ext[HALO - 1:HALO - 1 + tm, :]
                + cw_ref[0:1, :] * cu_ext[HALO - 2:HALO - 2 + tm, :])
        a = (proj_ref[:, O_BA:O_BA + D].astype(F32) * conv).astype(BF16)
        a_ref[...] = a
        ya = _dot(a, wco_ref[...])
        ya_ref[...] = ya.astype(BF16)

        up_ext[HALO:HALO + tm, :] = proj_ref[:, O_UP:O_UP + D].astype(F32)
        for g, window in enumerate(POOL_WINDOWS):
            cols = slice(g * GROUP_DIM, (g + 1) * GROUP_DIM)
            tok = up_ext[HALO:HALO + tm, cols]
            acc = tok
            for j in range(1, window):
                acc = acc + up_ext[HALO - j:HALO - j + tm, cols]
            pooled = (acc * _inv_count(i, tm, window) - tok).astype(BF16)
            pooled_ref[:, cols] = pooled
            pp_ref[:, cols] = _dot(pooled, wpool_ref[g]).astype(BF16)

        for hd in range(HEADS):
            cols = slice(hd * HEAD_DIM, (hd + 1) * HEAD_DIM)
            q = proj_ref[:, O_QX + hd * HEAD_DIM:O_QX + (hd + 1) * HEAD_DIM]
            p = _softmax_rows(_dot_nt(q, k_ref[:, cols]) * ATT_SCALE).astype(BF16)
            p_ref[:, hd * N_MEM:(hd + 1) * N_MEM] = p
            o_ref[:, cols] = _dot(p, v_ref[:, cols]).astype(BF16)
        yx = _dot(o_ref[...], wxo_ref[...])
        yx_ref[...] = yx.astype(BF16)

        merged = (_sigmoid(proj_ref[:, O_GA:O_GA + D].astype(F32)) * ya
                  + _sigmoid(proj_ref[:, O_GP:O_GP + D].astype(F32)) * (pp_ref[...].astype(F32) * ps_ref[...])
                  + _sigmoid(proj_ref[:, O_GX:O_GX + D].astype(F32)) * yx)
        x1_ref[...] = x_ref[...] + _dot(merged.astype(BF16), wout_ref[...])

        cu_ext[0:HALO, :] = cu_ext[tm:tm + HALO, :]
        up_ext[0:HALO, :] = up_ext[tm:tm + HALO, :]

    act = jax.ShapeDtypeStruct((t_len, D), BF16)
    return _carried_call(
        body, "mixer_fwd", (t_len // tm,),
        in_specs=[_rows(tm, D_IN), _rows(tm, D), _resident((8, D)), _resident((D, D)),
                  _resident((GROUPS, GROUP_DIM, GROUP_DIM)), _resident((1, D)),
                  _resident((N_MEM, D)), _resident((N_MEM, D)), _resident((D, D)), _resident((D, D))],
        out_specs=[_rows(tm, D)] * 7 + [_rows(tm, D)],
        out_shape=[act] * 6 + [jax.ShapeDtypeStruct((t_len, HEADS * N_MEM), BF16), jax.ShapeDtypeStruct((t_len, D), F32)],
        scratch_shapes=[pltpu.VMEM((tm + HALO, D), F32), pltpu.VMEM((tm + HALO, D), F32)],
        semantics=("arbitrary",), args=(proj, x, conv_w8, w_co, w_pool, pool_scale, k, v, w_xo, w_out), carry=carry)


def ffn_fwd(x1, target, g_ffn, w_gate, w_up, w_down, g_final):
    t_len = x1.shape[0]
    tm = min(512, t_len)

    def body(x1_ref, tgt_ref, g_ref, wg_ref, wu_ref, wd_ref, gf_ref, gate_ref, up_ref, dx2_ref, stat_ref):
        @pl.when(pl.program_id(0) == 0)
        def _():
            stat_ref[...] = jnp.zeros((8, D), F32)

        x1v = x1_ref[...]
        r2 = lax.rsqrt(jnp.mean(x1v * x1v, axis=-1, keepdims=True) + EPS)
        h2 = (x1v * r2 * g_ref[...]).astype(BF16)
        gate = _dot_nt(h2, wg_ref[...])
        up = _dot_nt(h2, wu_ref[...])
        gate_ref[...] = gate.astype(BF16)
        up_ref[...] = up.astype(BF16)
        act = (gate * _sigmoid(gate) * up).astype(BF16)
        x2 = x1v + _dot(act, wd_ref[...])
        r3 = lax.rsqrt(jnp.mean(x2 * x2, axis=-1, keepdims=True) + EPS)
        xh = x2 * r3
        diff = xh * gf_ref[...] - tgt_ref[...]
        dy = diff * (1.0 / D)
        stat_ref[0:1, :] += jnp.sum(dy * xh, axis=0, keepdims=True)
        stat_ref[1:2, :] += (0.5 / D) * jnp.sum(diff * diff, axis=0, keepdims=True)
        dxh = dy * gf_ref[...]
        dx2_ref[...] = r3 * (dxh - xh * jnp.mean(dxh * xh, axis=-1, keepdims=True))

    return pl.pallas_call(
        body, name="ffn_fwd",
        grid=(t_len // tm,),
        in_specs=[_rows(tm, D), _rows(tm, D), _resident((1, D)), _resident((D_FF, D)), _resident((D_FF, D)),
                  _resident((D_FF, D)), _resident((1, D))],
        out_specs=[_rows(tm, D_FF), _rows(tm, D_FF), _rows(tm, D), _const((8, D))],
        out_shape=[jax.ShapeDtypeStruct((t_len, D_FF), BF16), jax.ShapeDtypeStruct((t_len, D_FF), BF16),
                   jax.ShapeDtypeStruct((t_len, D), F32), jax.ShapeDtypeStruct((8, D), F32)],
        compiler_params=_params(("arbitrary",)),
    )(x1, target, g_ffn, w_gate, w_up, w_down, g_final)


def ffn_bwd(dx2, x1, gate, up, g_ffn, w_gate, w_up, w_down):
    t_len = x1.shape[0]
    tm = min(256, t_len)

    def body(dx2_ref, x1_ref, gate_ref, up_ref, g_ref, wg_ref, wu_ref, wd_ref,
             dx1_ref, dgate_ref, dup_ref, act_ref, h2_ref, stat_ref):
        @pl.when(pl.program_id(0) == 0)
        def _():
            stat_ref[...] = jnp.zeros((8, D), F32)

        dx2v = dx2_ref[...]
        gate = gate_ref[...]
        upv = up_ref[...]
        sg = _sigmoid(gate.astype(F32)).astype(BF16)
        silu = gate * sg
        act_ref[...] = silu * upv
        dact = _dot_nt(dx2v.astype(BF16), wd_ref[...]).astype(BF16)
        dup = dact * silu
        dgate = dact * upv * (sg * (1.0 + gate * (1.0 - sg)))
        dup_ref[...] = dup
        dgate_ref[...] = dgate
        dh2 = _dot(dgate, wg_ref[...]) + _dot(dup, wu_ref[...])
        x1v = x1_ref[...]
        r2 = lax.rsqrt(jnp.mean(x1v * x1v, axis=-1, keepdims=True) + EPS)
        xh = x1v * r2
        h2_ref[...] = (xh * g_ref[...]).astype(BF16)
        stat_ref[0:1, :] += jnp.sum(dh2 * xh, axis=0, keepdims=True)
        dxh = dh2 * g_ref[...]
        dx1_ref[...] = dx2v + r2 * (dxh - xh * jnp.mean(dxh * xh, axis=-1, keepdims=True))

    ff = jax.ShapeDtypeStruct((t_len, D_FF), BF16)
    return pl.pallas_call(
        body, name="ffn_bwd",
        grid=(t_len // tm,),
        in_specs=[_rows(tm, D), _rows(tm, D), _rows(tm, D_FF), _rows(tm, D_FF), _resident((1, D)),
                  _resident((D_FF, D)), _resident((D_FF, D)), _resident((D_FF, D))],
        out_specs=[_rows(tm, D), _rows(tm, D_FF), _rows(tm, D_FF), _rows(tm, D_FF), _rows(tm, D), _const((8, D))],
        out_shape=[jax.ShapeDtypeStruct((t_len, D), F32), ff, ff, ff, jax.ShapeDtypeStruct((t_len, D), BF16),
                   jax.ShapeDtypeStruct((8, D), F32)],
        compiler_params=_params(("arbitrary",)),
    )(dx2, x1, gate, up, g_ffn, w_gate, w_up, w_down)


def mixer_bwd(dx1, proj, ya, pp, yx, probs, conv_w8, w_co, w_pool, pool_scale, k, v, w_xo, w_out, carry=None):
    t_len = dx1.shape[0]
    tm = min(256, t_len)
    n_tiles = t_len // tm
    halo_blocks = tm // HALO

    def body(dx1_ref, proj_ref, halo_ref, ya_ref, pp_ref, yx_ref, p_ref,
             cw_ref, wco_ref, wpool_ref, ps_ref, k_ref, v_ref, wxo_ref, wout_ref, band_ref,
             dproj_ref, merged_ref, dya_ref, dpp_ref, dyx_ref, dk_ref, dv_ref, stat_ref,
             cu_ext, dconv_ext, dpn_ext):
        step = pl.program_id(0)
        tile = n_tiles - 1 - step

        @pl.when(step == 0)
        def _():
            dk_ref[...] = jnp.zeros((N_MEM, D), F32)
            dv_ref[...] = jnp.zeros((N_MEM, D), F32)
            stat_ref[...] = jnp.zeros((8, D), F32)
            dconv_ext[tm:tm + HALO, :] = jnp.zeros((HALO, D), F32)
            dpn_ext[tm:tm + POOL_PAD, :] = jnp.zeros((POOL_PAD, D), BF16)

        dmerged = _dot_nt(dx1_ref[...].astype(BF16), wout_ref[...]).astype(BF16)
        sa = _sigmoid(proj_ref[:, O_GA:O_GA + D].astype(F32)).astype(BF16)
        sp = _sigmoid(proj_ref[:, O_GP:O_GP + D].astype(F32)).astype(BF16)
        sx = _sigmoid(proj_ref[:, O_GX:O_GX + D].astype(F32)).astype(BF16)
        ya = ya_ref[...]
        ppv = pp_ref[...]
        scale = ps_ref[...].astype(BF16)
        yp = ppv * scale
        yx = yx_ref[...]
        merged_ref[...] = sa * ya + sp * yp + sx * yx
        dproj_ref[:, O_GA:O_GA + D] = dmerged * ya * (sa * (1.0 - sa))
        dproj_ref[:, O_GP:O_GP + D] = dmerged * yp * (sp * (1.0 - sp))
        dproj_ref[:, O_GX:O_GX + D] = dmerged * yx * (sx * (1.0 - sx))
        dya = dmerged * sa
        dyp = dmerged * sp
        dyx = dmerged * sx
        dya_ref[...] = dya
        dyx_ref[...] = dyx
        stat_ref[1:2, :] += jnp.sum(dyp.astype(F32) * ppv.astype(F32), axis=0, keepdims=True)
        dpp = dyp * scale
        dpp_ref[...] = dpp

        da = _dot_nt(dya, wco_ref[...])
        c_a = proj_ref[:, O_CA:O_CA + D].astype(F32)
        u_a = proj_ref[:, O_UA:O_UA + D].astype(F32)
        cu = c_a * u_a
        halo_cu = halo_ref[:, O_CA:O_CA + D].astype(F32) * halo_ref[:, O_UA:O_UA + D].astype(F32)
        cu_ext[0:HALO, :] = jnp.where(tile > 0, halo_cu, 0.0)
        cu_ext[HALO:HALO + tm, :] = cu
        cu1 = cu_ext[HALO - 1:HALO - 1 + tm, :]
        cu2 = cu_ext[HALO - 2:HALO - 2 + tm, :]
        conv = cw_ref[2:3, :] * cu + cw_ref[1:2, :] * cu1 + cw_ref[0:1, :] * cu2
        dproj_ref[:, O_BA:O_BA + D] = (da * conv).astype(BF16)
        dconv = da * proj_ref[:, O_BA:O_BA + D].astype(F32)
        stat_ref[5:6, :] += jnp.sum(dconv * cu2, axis=0, keepdims=True)
        stat_ref[6:7, :] += jnp.sum(dconv * cu1, axis=0, keepdims=True)
        stat_ref[7:8, :] += jnp.sum(dconv * cu, axis=0, keepdims=True)
        dconv_ext[0:tm, :] = dconv
        dcu = (cw_ref[2:3, :] * dconv + cw_ref[1:2, :] * dconv_ext[1:1 + tm, :]
               + cw_ref[0:1, :] * dconv_ext[2:2 + tm, :])
        dproj_ref[:, O_CA:O_CA + D] = (dcu * u_a).astype(BF16)
        dproj_ref[:, O_UA:O_UA + D] = (dcu * c_a).astype(BF16)

        for g, window in enumerate(POOL_WINDOWS):
            cols = slice(g * GROUP_DIM, (g + 1) * GROUP_DIM)
            dpooled = _dot_nt(dpp[:, cols], wpool_ref[g])
            dpn_ext[0:tm, cols] = (dpooled * _inv_count(tile, tm, window)).astype(BF16)
            acc = _dot(band_ref[g], dpn_ext[:, cols])
            dproj_ref[:, O_UP + g * GROUP_DIM:O_UP + (g + 1) * GROUP_DIM] = (acc - dpooled).astype(BF16)

        do = _dot_nt(dyx, wxo_ref[...])
        for hd in range(HEADS):
            cols = slice(hd * HEAD_DIM, (hd + 1) * HEAD_DIM)
            q = proj_ref[:, O_QX + hd * HEAD_DIM:O_QX + (hd + 1) * HEAD_DIM]
            kh = k_ref[:, cols]
            p16 = p_ref[:, hd * N_MEM:(hd + 1) * N_MEM]
            p = p16.astype(F32)
            doh = do[:, cols].astype(BF16)
            dp = _dot_nt(doh, v_ref[:, cols])
            dv_ref[:, cols] += _dot_tn(p16, doh)
            ds = (p * (dp - jnp.sum(dp * p, axis=-1, keepdims=True)) * ATT_SCALE).astype(BF16)
            dproj_ref[:, O_QX + hd * HEAD_DIM:O_QX + (hd + 1) * HEAD_DIM] = _dot(ds, kh).astype(BF16)
            dk_ref[:, cols] += _dot_tn(ds, q)

        dconv_ext[tm:tm + HALO, :] = dconv_ext[0:HALO, :]
        dpn_ext[tm:tm + HALO, :] = dpn_ext[0:HALO, :]

    def rev(width):
        return pl.BlockSpec((tm, width), lambda s: (n_tiles - 1 - s, 0))

    halo_spec = pl.BlockSpec((HALO, D_IN), lambda s: (jnp.maximum((n_tiles - 1 - s) * halo_blocks - 1, 0), 0))
    act = jax.ShapeDtypeStruct((t_len, D), BF16)
    kv_grad = jax.ShapeDtypeStruct((N_MEM, D), F32)
    return _carried_call(
        body, "mixer_bwd", (n_tiles,),
        in_specs=[rev(D), rev(D_IN), halo_spec, rev(D), rev(D), rev(D), rev(D),
                  _resident((8, D)), _resident((D, D)), _resident((GROUPS, GROUP_DIM, GROUP_DIM)), _resident((1, D)),
                  _resident((N_MEM, D)), _resident((N_MEM, D)), _resident((D, D)), _resident((D, D)),
                  _resident((GROUPS, tm, tm + POOL_PAD))],
        out_specs=[rev(D_IN), rev(D), rev(D), rev(D), rev(D),
                   _const((N_MEM, D)), _const((N_MEM, D)), _const((8, D))],
        out_shape=[jax.ShapeDtypeStruct((t_len, D_IN), BF16), act, act, act, act, kv_grad, kv_grad,
                   jax.ShapeDtypeStruct((8, D), F32)],
        scratch_shapes=[pltpu.VMEM((tm + HALO, D), F32)] * 2 + [pltpu.VMEM((tm + POOL_PAD, D), BF16)],
        semantics=("arbitrary",),
        args=(dx1, proj, proj, ya, pp, yx, probs, conv_w8, w_co, w_pool, pool_scale, k, v, w_xo, w_out,
              _window_bands(tm, False)), carry=carry)


def in_bwd(dproj, w_in, x, dx1, g_mix, carry=None):
    t_len = x.shape[0]
    tm = min(512, t_len)

    def body(dproj_ref, w_ref, x_ref, dx1_ref, g_ref, gx_ref, stat_ref):
        @pl.when(pl.program_id(0) == 0)
        def _():
            stat_ref[...] = jnp.zeros((8, D), F32)

        dh = _dot_nt(dproj_ref[...], w_ref[...])
        xv = x_ref[...]
        r = lax.rsqrt(jnp.mean(xv * xv, axis=-1, keepdims=True) + EPS)
        xh = xv * r
        stat_ref[0:1, :] += jnp.sum(dh * xh, axis=0, keepdims=True)
        dxh = dh * g_ref[...]
        gx_ref[...] = dx1_ref[...] + r * (dxh - xh * jnp.mean(dxh * xh, axis=-1, keepdims=True))

    return _carried_call(
        body, "in_bwd", (t_len // tm,),
        in_specs=[_rows(tm, D_IN), _resident((D, D_IN)), _rows(tm, D), _rows(tm, D), _resident((1, D))],
        out_specs=[_rows(tm, D), _const((8, D))],
        out_shape=[jax.ShapeDtypeStruct((t_len, D), F32), jax.ShapeDtypeStruct((8, D), F32)],
        scratch_shapes=[], semantics=("arbitrary",), args=(dproj, w_in, x, dx1, g_mix), carry=carry)


def kv_bwd(dk, dv, memn, mem, g_mem, w_kv4):
    half = D // 2

    def body(dk_ref, dv_ref, memn_ref, mem_ref, g_ref, w_ref, gw_ref, gw16_ref, stat_ref):
        mn = memn_ref[...]
        parts = (dk_ref[:, 0:half], dk_ref[:, half:D], dv_ref[:, 0:half], dv_ref[:, half:D])
        dmemn = jnp.zeros((N_MEM, D), F32)
        for j, part in enumerate(parts):
            part = part.astype(BF16)
            gw = _dot_tn(mn, part)
            gw_ref[j] = gw
            gw16_ref[j] = gw.astype(BF16)
            dmemn = dmemn + _dot_nt(part, w_ref[j])
        mv = mem_ref[...]
        r = lax.rsqrt(jnp.mean(mv * mv, axis=-1, keepdims=True) + EPS)
        stat_ref[...] = jnp.zeros((8, D), F32)
        stat_ref[0:1, :] = jnp.sum(dmemn * (mv * r), axis=0, keepdims=True)

    return pl.pallas_call(
        body, name="kv_bwd",
        out_shape=[jax.ShapeDtypeStruct((N_CHIPS, D, half), F32), jax.ShapeDtypeStruct((N_CHIPS, D, half), BF16),
                   jax.ShapeDtypeStruct((8, D), F32)],
        compiler_params=_params(),
    )(dk, dv, memn, mem, g_mem, w_kv4)


def matmul_tn(name, a, b, tn, col_blocks=1, carry=None):
    t_len, k_dim = a.shape
    n_dim = b.shape[1]
    tt = min(1024, t_len)
    per_block = n_dim // col_blocks // tn

    def body(a_ref, b_ref, out_ref, out16_ref):
        @pl.when(pl.program_id(1) == 0)
        def _():
            out_ref[...] = jnp.zeros((k_dim, tn), F32)

        out_ref[...] += _dot_tn(a_ref[...].astype(BF16), b_ref[...].astype(BF16))

        @pl.when(pl.program_id(1) == t_len // tt - 1)
        def _():
            out16_ref[...] = out_ref[...].astype(BF16)

    if col_blocks == 1:
        out_spec = pl.BlockSpec((k_dim, tn), lambda n, t: (0, n))
        shape = (k_dim, n_dim)
    else:
        out_spec = pl.BlockSpec((None, k_dim, tn), lambda n, t: (n // per_block, 0, n % per_block))
        shape = (col_blocks, k_dim, n_dim // col_blocks)
    outs, carried = _carried_call(
        body, name, (n_dim // tn, t_len // tt),
        in_specs=[pl.BlockSpec((tt, k_dim), lambda n, t: (t, 0)), pl.BlockSpec((tt, tn), lambda n, t: (t, n))],
        out_specs=[out_spec, out_spec], out_shape=[jax.ShapeDtypeStruct(shape, F32), jax.ShapeDtypeStruct(shape, BF16)],
        scratch_shapes=[], semantics=("arbitrary", "arbitrary"), args=(a, b), carry=carry)
    return (tuple(outs), carried) if carry is not None else tuple(outs)


def grad_w_pool(pooled, dpp):
    t_len = pooled.shape[0]
    tt = min(1024, t_len)
    steps = t_len // tt

    def body(a_ref, b_ref, out_ref, out16_ref):
        @pl.when(pl.program_id(0) == 0)
        def _():
            out_ref[...] = jnp.zeros((GROUPS, GROUP_DIM, GROUP_DIM), F32)

        for g in range(GROUPS):
            cols = slice(g * GROUP_DIM, (g + 1) * GROUP_DIM)
            out_ref[g] += _dot_tn(a_ref[:, cols], b_ref[:, cols])

        @pl.when(pl.program_id(0) == steps - 1)
        def _():
            out16_ref[...] = out_ref[...].astype(BF16)

    shape = (GROUPS, GROUP_DIM, GROUP_DIM)
    return pl.pallas_call(
        body, name="grad_w_pool", grid=(steps,),
        in_specs=[_rows(tt, D), _rows(tt, D)], out_specs=[_const(shape), _const(shape)],
        out_shape=[jax.ShapeDtypeStruct(shape, F32), jax.ShapeDtypeStruct(shape, BF16)],
        compiler_params=_params(("arbitrary",)),
    )(pooled, dpp)


def _place():
    x, y, c = lax.axis_index("x"), lax.axis_index("y"), lax.axis_index("c")
    return x, y, c


def _other_chips(x, y):
    return [(1 - x, y), (x, 1 - y), (1 - x, 1 - y)]


def _any_specs(n):
    return [pl.BlockSpec(memory_space=pl.ANY)] * n


def exchange_call(name, exchange_cls, arrays):
    n = len(arrays)
    shapes = exchange_cls.out_shapes(arrays)

    def body(*refs):
        exchange = exchange_cls(refs[:n], refs[n:n + len(shapes)], *refs[n + len(shapes):])
        exchange.start()
        if hasattr(exchange, "middle"):
            exchange.middle()
        exchange.finish()

    res = pl.pallas_call(
        body, name=name, in_specs=_any_specs(n), out_specs=_any_specs(len(shapes)),
        out_shape=shapes, scratch_shapes=exchange_cls.sems(n),
    )(*arrays)
    return res[:n]


class _Gather:
    SLOTS = 8
    ROW_ALIGN = 16
    MIDDLE_AT = (5, 8)

    @staticmethod
    def out_shapes(shards):
        return [jax.ShapeDtypeStruct((N_CHIPS,) + s.shape, s.dtype) for s in shards]

    @staticmethod
    def sems(n):
        return [pltpu.SemaphoreType.DMA((n, _Gather.SLOTS)), pltpu.SemaphoreType.DMA((n, _Gather.SLOTS))]

    def __init__(self, ins, outs, send_sems, recv_sems):
        self.ins, self.outs, self.send_sems, self.recv_sems = ins, outs, send_sems, recv_sems
        x, y, c = _place()
        self.c, self.me, self.sibling = c, 2 * x + y, (x, y, 1 - c)
        self.across = [(1 - x, y), (x, 1 - y), (1 - x, 1 - y)]

    def _rows(self, a, which, part=None):
        half = self.ins[a].shape[1] // 2
        first = (half // 2) // self.ROW_ALIGN * self.ROW_ALIGN
        if part is None:
            return pl.ds(which * half, half)
        return pl.ds(which * half, first) if part == 0 else pl.ds(which * half + first, half - first)

    def _has_part(self, a, part):
        half = self.ins[a].shape[1] // 2
        return part == 1 or (half // 2) // self.ROW_ALIGN > 0

    def _block(self, a, chip, rows=slice(None)):
        index = chip if not isinstance(chip, tuple) else 2 * chip[0] + chip[1]
        if len(self.outs[a].shape) == len(self.ins[a].shape):
            cols = self.ins[a].shape[2]
            return self.outs[a].at[:, rows, pl.ds(pl.multiple_of(index * cols, cols), cols)]
        return self.outs[a].at[index, :, rows, :]

    def _remote(self, src, dst, a, slot, to):
        return pltpu.make_async_remote_copy(src_ref=src, dst_ref=dst, send_sem=self.send_sems.at[a, slot],
                                            recv_sem=self.recv_sems.at[a, slot], device_id=to, device_id_type=MESH)

    def _own(self, a):
        return self._remote(self.ins[a], self._block(a, self.me), a, 6, self.sibling)

    def _sent(self, a, axis):
        rows = self._rows(a, self.c)
        return self._remote(self.ins[a].at[:, rows, :], self._block(a, self.me, rows), a, axis,
                            (*self.across[axis], self.c))

    def _landed(self, a, axis):
        block = self._block(a, self.across[axis], self._rows(a, self.c))
        return self._remote(block, block, a, axis, (*self.across[axis], self.c))

    def _relayed(self, a, part, incoming):
        source = self.across[2] if incoming else self.across[part]
        block = self._block(a, source, self._rows(a, self.c, part))
        return self._remote(block, block, a, (2, 7)[part], (*self.across[1 - part], self.c))

    def _passed_on(self, a, source, which):
        block = self._block(a, self.across[source], self._rows(a, which))
        return self._remote(block, block, a, 3 + source, self.sibling)

    def start(self):
        for a in range(len(self.ins)):
            self._own(a).start()
        for a in range(len(self.ins)):
            for axis in range(2):
                self._sent(a, axis).start()

    def middle(self):
        for a in range(len(self.ins)):
            for axis in range(2):
                self._landed(a, axis).wait_recv()
                if self._has_part(a, axis):
                    self._relayed(a, axis, incoming=False).start()
                self._passed_on(a, axis, self.c).start()

    def finish(self):
        n = len(self.ins)
        for a in range(n):
            for part in range(2):
                if self._has_part(a, part):
                    self._relayed(a, part, incoming=True).wait_recv()
            self._passed_on(a, 2, self.c).start()
        for a in range(n):
            for source in range(3):
                self._passed_on(a, source, 1 - self.c).wait_recv()
            self._own(a).wait_recv()
        for a in range(n):
            self._own(a).wait_send()
            for axis in range(2):
                self._sent(a, axis).wait_send()
                if self._has_part(a, axis):
                    self._relayed(a, axis, incoming=False).wait_send()
            for source in range(3):
                self._passed_on(a, source, self.c).wait_send()


class _HalfExchange:
    @staticmethod
    def out_shapes(grads):
        return [jax.ShapeDtypeStruct((N_CHIPS, g.shape[1] // 2, g.shape[2]), g.dtype) for g in grads]

    @staticmethod
    def sems(n):
        return [pltpu.SemaphoreType.DMA((n,)), pltpu.SemaphoreType.DMA((n,))]

    def __init__(self, ins, outs, send_sems, recv_sems):
        self.ins, self.outs, self.send_sems, self.recv_sems = ins, outs, send_sems, recv_sems

    def _copies(self):
        x, y, c = _place()
        for a in range(len(self.ins)):
            h = self.ins[a].shape[1] // 2
            yield pltpu.make_async_remote_copy(
                src_ref=self.ins[a].at[:, pl.ds((1 - c) * h, h), :], dst_ref=self.outs[a],
                send_sem=self.send_sems.at[a], recv_sem=self.recv_sems.at[a], device_id=(x, y, 1 - c), device_id_type=MESH)

    def start(self):
        for cp in self._copies():
            cp.start()

    def finish(self):
        for cp in self._copies():
            cp.wait()


class _GatherColumns(_Gather):
    @staticmethod
    def out_shapes(shards):
        return [jax.ShapeDtypeStruct(s.shape[:2] + (N_CHIPS * s.shape[2],), s.dtype) for s in shards]


def _row_tile(rows, cols, budget=2 << 20):
    best = 16
    for tr in range(16, rows + 1, 16):
        if rows % tr == 0 and tr * cols * 4 <= budget:
            best = tr
    return best


def add_sibling_half(name, grad, got, place):
    _, rows, cols = grad.shape
    h = rows // 2
    tr = _row_tile(h, cols)
    per_half = h // tr

    def body(place_ref, g_ref, o_ref, own_ref, out16_ref):
        total = g_ref[...] + o_ref[...].astype(F32)
        out16_ref[...] = total.astype(BF16)

        @pl.when(pl.program_id(1) == place_ref[1])
        def _():
            own_ref[...] = total

    return pl.pallas_call(
        body, name=name,
        grid_spec=pltpu.PrefetchScalarGridSpec(
            num_scalar_prefetch=1, grid=(per_half, N_CHIPS),
            in_specs=[pl.BlockSpec((None, tr, cols), lambda r, j, pr: (j, pr[0] * per_half + r, 0)),
                      pl.BlockSpec((None, tr, cols), lambda r, j, pr: (j, r, 0))],
            out_specs=[pl.BlockSpec((tr, cols), lambda r, j, pr: (r, 0)),
                       pl.BlockSpec((None, tr, cols), lambda r, j, pr: (j, r, 0))]),
        out_shape=[jax.ShapeDtypeStruct((h, cols), F32), jax.ShapeDtypeStruct((N_CHIPS, h, cols), BF16)],
        compiler_params=_params(("arbitrary", "arbitrary")),
    )(place, grad, got)


class _ChipExchange:
    SLOTS = 6
    ROW_ALIGN = 16
    MIDDLE_AT = (3, 8)

    @staticmethod
    def out_shapes(partials):
        return ([jax.ShapeDtypeStruct((3,) + p.shape[1:], p.dtype) for p in partials]
                + [jax.ShapeDtypeStruct(p.shape[1:], p.dtype) for p in partials])

    @staticmethod
    def sems(n):
        return [pltpu.SemaphoreType.DMA((n, _ChipExchange.SLOTS)), pltpu.SemaphoreType.DMA((n, _ChipExchange.SLOTS))]

    def __init__(self, ins, outs, send_sems, recv_sems):
        n = len(ins)
        self.ins, self.outs, self.relays, self.send_sems, self.recv_sems = ins, outs[:n], outs[n:], send_sems, recv_sems

    def _copy(self, a, slot):
        x, y, c = _place()
        across = _other_chips(x, y)
        rows = self.ins[a].shape[1]
        first = (rows // 2) // self.ROW_ALIGN * self.ROW_ALIGN
        part = (pl.ds(0, first), pl.ds(first, rows - first))
        if slot < 2:
            px, py = across[slot]
            src, dst, to = self.ins[a].at[2 * px + py], self.outs[a].at[slot], across[slot]
        elif slot < 4:
            px, py = across[2]
            src, dst, to = self.ins[a].at[2 * px + py, part[slot - 2], :], self.relays[a].at[part[slot - 2], :], across[slot - 2]
        else:
            src, dst, to = self.relays[a].at[part[slot - 4], :], self.outs[a].at[2, part[slot - 4], :], across[5 - slot]
        return pltpu.make_async_remote_copy(src_ref=src, dst_ref=dst, send_sem=self.send_sems.at[a, slot],
                                            recv_sem=self.recv_sems.at[a, slot], device_id=(*to, c), device_id_type=MESH)

    def start(self):
        for slot in (2, 3, 0, 1):
            for a in range(len(self.ins)):
                self._copy(a, slot).start()

    def middle(self):
        for part in range(2):
            for a in range(len(self.ins)):
                self._copy(a, 2 + part).wait_recv()
                self._copy(a, 4 + part).start()

    def finish(self):
        for a in range(len(self.ins)):
            for slot in (0, 1, 4, 5):
                self._copy(a, slot).wait_recv()
        for a in range(len(self.ins)):
            for slot in range(self.SLOTS):
                self._copy(a, slot).wait_send()


def add_chip_blocks(name, own, got):
    h, cols = own.shape
    tr = _row_tile(h, cols)

    def body(p_ref, g0_ref, g1_ref, g2_ref, out_ref):
        out_ref[...] = ((p_ref[...] + g0_ref[...].astype(F32)) + g1_ref[...].astype(F32)) + g2_ref[...].astype(F32)

    def got_spec(slot):
        return pl.BlockSpec((None, tr, cols), lambda r: (slot, r, 0))

    return pl.pallas_call(
        body, name=name, grid=(h // tr,),
        in_specs=[_rows(tr, cols), got_spec(0), got_spec(1), got_spec(2)], out_specs=_rows(tr, cols),
        out_shape=jax.ShapeDtypeStruct((h, cols), F32),
        compiler_params=_params(("arbitrary",)),
    )(own, got, got, got)


class _SiblingSwap:
    @staticmethod
    def out_shapes(halves):
        return [jax.ShapeDtypeStruct(v.shape, v.dtype) for v in halves]

    @staticmethod
    def sems(n):
        return [pltpu.SemaphoreType.DMA((n,)), pltpu.SemaphoreType.DMA((n,))]

    def __init__(self, ins, outs, send_sems, recv_sems):
        self.ins, self.outs, self.send_sems, self.recv_sems = ins, outs, send_sems, recv_sems

    def _copies(self):
        x, y, c = _place()
        for a in range(len(self.ins)):
            yield pltpu.make_async_remote_copy(
                src_ref=self.ins[a], dst_ref=self.outs[a], send_sem=self.send_sems.at[a], recv_sem=self.recv_sems.at[a],
                device_id=(x, y, 1 - c), device_id_type=MESH)

    def start(self):
        for cp in self._copies():
            cp.start()

    def finish(self):
        for cp in self._copies():
            cp.wait()


def all_reduce_small(pack):
    rows = pack.shape[0]

    def body(pack_ref, out_ref, gathered, send_sems, recv_sems):
        x, y, c = _place()
        me = 4 * x + 2 * y + c
        gathered[me] = pack_ref[...]
        copies = []
        for rel in range(1, 8):
            fx, fy, fc = (rel >> 2) & 1, (rel >> 1) & 1, rel & 1
            peer = (x ^ fx, y ^ fy, c ^ fc)
            cp = pltpu.make_async_remote_copy(
                src_ref=pack_ref, dst_ref=gathered.at[me], send_sem=send_sems.at[rel - 1], recv_sem=recv_sems.at[rel - 1],
                device_id=peer, device_id_type=MESH)
            cp.start()
            copies.append(cp)
        for rel in range(1, 8):
            fx, fy, fc = (rel >> 2) & 1, (rel >> 1) & 1, rel & 1
            src = 4 * (x ^ fx) + 2 * (y ^ fy) + (c ^ fc)
            pltpu.make_async_remote_copy(
                src_ref=pack_ref, dst_ref=gathered.at[src], send_sem=send_sems.at[rel - 1], recv_sem=recv_sems.at[rel - 1],
                device_id=(x, y, c), device_id_type=MESH).wait_recv()
        for cp in copies:
            cp.wait_send()
        total = gathered[0]
        for dev in range(1, 8):
            total = total + gathered[dev]
        out_ref[...] = total

    return pl.pallas_call(
        body, name="all_reduce_small",
        in_specs=[pl.BlockSpec(memory_space=pltpu.VMEM)], out_specs=pl.BlockSpec(memory_space=pltpu.VMEM),
        out_shape=jax.ShapeDtypeStruct((rows, D), F32),
        scratch_shapes=[pltpu.VMEM((8, rows, D), F32), pltpu.SemaphoreType.DMA((7,)), pltpu.SemaphoreType.DMA((7,))],
    )(pack)


def _adamw_update(w, g, m, v):
    nm = ADAM_B1 * m + (1.0 - ADAM_B1) * g
    nv = ADAM_B2 * v + (1.0 - ADAM_B2) * (g * g)
    m_hat = nm / (1.0 - ADAM_B1 ** ADAM_STEP)
    v_hat = nv / (1.0 - ADAM_B2 ** ADAM_STEP)
    delta = -ADAM_LR * (m_hat / (jnp.sqrt(v_hat) + ADAM_EPS) + ADAM_WD * w)
    return delta, nm, nv


def adamw(name, w, g, m, v):
    def body(w_ref, g_ref, m_ref, v_ref, d_ref, nm_ref, nv_ref):
        d_ref[...], nm_ref[...], nv_ref[...] = _adamw_update(w_ref[...], g_ref[...], m_ref[...], v_ref[...])

    out = jax.ShapeDtypeStruct(w.shape, F32)
    return pl.pallas_call(body, name=name, out_shape=[out] * 3, compiler_params=_params())(w, g, m, v)


def adamw_halves(name, w, mine, theirs, m, v, core):
    h, cols = mine.shape
    tr = _row_tile(h, cols)
    per_half = h // tr

    def body(core_ref, w_ref, mine_ref, theirs_ref, m_ref, v_ref, g_ref, d_ref, nm_ref, nv_ref):
        g = jnp.where(pl.program_id(0) == core_ref[0], mine_ref[...], theirs_ref[...])
        g_ref[...] = g
        d_ref[...], nm_ref[...], nv_ref[...] = _adamw_update(w_ref[...], g, m_ref[...], v_ref[...])

    full = pl.BlockSpec((tr, cols), lambda hh, r, cr: (hh * per_half + r, 0))
    mine_spec = pl.BlockSpec((tr, cols), lambda hh, r, cr: (jnp.where(hh == cr[0], r, 0), 0))
    theirs_spec = pl.BlockSpec((tr, cols), lambda hh, r, cr: (jnp.where(hh == cr[0], 0, r), 0))
    out = jax.ShapeDtypeStruct((2 * h, cols), F32)
    return pl.pallas_call(
        body, name=name,
        grid_spec=pltpu.PrefetchScalarGridSpec(
            num_scalar_prefetch=1, grid=(2, per_half),
            in_specs=[full, mine_spec, theirs_spec, full, full], out_specs=[full] * 4),
        out_shape=[out] * 4,
        compiler_params=_params(("arbitrary", "arbitrary")),
    )(core, w, mine, theirs, m, v)


BIG = ("w_in", "w_conv_out", "w_pool", "w_kv", "w_xattn_out", "w_out", "w_gate", "w_up", "w_down")


def kernel(x, mem, norm_mix, w_in, conv_w, w_conv_out, w_pool, pool_scale, norm_mem, w_kv, w_xattn_out, w_out, norm_ffn, w_gate, w_up, w_down, norm_final, loss_target, m_norm_mix, m_w_in, m_conv_w, m_w_conv_out, m_w_pool, m_pool_scale, m_norm_mem, m_w_kv, m_w_xattn_out, m_w_out, m_norm_ffn, m_w_gate, m_w_up, m_w_down, m_norm_final, v_norm_mix, v_w_in, v_conv_w, v_w_conv_out, v_w_pool, v_pool_scale, v_norm_mem, v_w_kv, v_w_xattn_out, v_w_out, v_norm_ffn, v_w_gate, v_w_up, v_w_down, v_norm_final):
    t_len = x.shape[1]
    xi, yi, ci = lax.axis_index("x"), lax.axis_index("y"), lax.axis_index("c")
    chip = 2 * xi + yi
    core_arr = jnp.reshape(ci, (1,)).astype(jnp.int32)
    place_arr = jnp.stack([ci, chip]).astype(jnp.int32)

    conv_pad = jnp.concatenate([conv_w, jnp.zeros((1, 13, 256), F32)], axis=1)
    def t2(w):
        return jnp.swapaxes(w, 1, 2)

    (g_in,) = exchange_call("gather_w_in", _GatherColumns, [w_in.astype(BF16)])
    w_in_f = g_in[0]

    x2d = x[0]
    tgt = loss_target[0]
    (proj, h), (g_kv, g_conv_w, g_co, g_xo, g_out, g_pool, g_gate) = proj_fwd(
        x2d, norm_mix, w_in_f,
        carry=(_Gather, [w_kv.astype(BF16), conv_pad,
                         w_conv_out.astype(BF16), w_xattn_out.astype(BF16), w_out.astype(BF16),
                         w_pool[0].astype(BF16),
                         t2(w_gate).astype(BF16)]))
    w_kv4 = g_kv.reshape(N_CHIPS, D, D // 2)
    conv_full = jnp.transpose(g_conv_w[:, 0, 0:8, :], (1, 0, 2)).reshape(8, D)
    w_co_f, w_xo_f, w_out_f = g_co.reshape(D, D), g_xo.reshape(D, D), g_out.reshape(D, D)
    w_pool_f = jnp.transpose(g_pool, (1, 0, 2, 3)).reshape(GROUPS, GROUP_DIM, GROUP_DIM)
    memn, k, v = kv_fwd(mem[0], norm_mem, w_kv4)
    (a, pooled, ya, pp, yx, o, probs, x1), (g_up, g_down) = mixer_fwd(
        proj, x2d, conv_full, w_co_f, w_pool_f, pool_scale, k, v, w_xo_f, w_out_f,
        carry=(_Gather, [t2(w_up).astype(BF16), w_down.astype(BF16)]))
    w_gate_f, w_up_f, w_down_f = g_gate.reshape(D_FF, D), g_up.reshape(D_FF, D), g_down.reshape(D_FF, D)
    gate, up, dx2, stat_f = ffn_fwd(x1, tgt, norm_ffn, w_gate_f, w_up_f, w_down_f, norm_final.reshape(1, D))

    def by_chip(pair):
        return tuple(gw.reshape(N_CHIPS, gw.shape[0] // N_CHIPS, gw.shape[1]) for gw in pair)

    def chip_partials(names, grads, got):
        return [add_sibling_half("add_sibling_" + n, g32, o_, place_arr) for n, (g32, _), o_ in zip(names, grads, got)]

    def chip_sums(names, partials, got):
        return [add_chip_blocks("add_chips_" + n, own, g2) for n, (own, _), g2 in zip(names, partials, got)]

    dx1, dgate, dup, act, h2, stat_b1 = ffn_bwd(dx2, x1, gate, up, norm_ffn, w_gate_f, w_up_f, w_down_f)
    gw_gate = by_chip(matmul_tn("grad_w_gate", dgate, h2, 512))
    gw_up, got_gate = matmul_tn("grad_w_up", dup, h2, 512, carry=(_HalfExchange, [gw_gate[1]]))
    gw_up = by_chip(gw_up)
    gw_down, got_up = matmul_tn("grad_w_down", act, dx2, 512, carry=(_HalfExchange, [gw_up[1]]))
    gw_down = by_chip(gw_down)
    got_down = exchange_call("exchange_halves_w_down", _HalfExchange, [gw_down[1]])
    names_ffn = ("w_gate", "w_up", "w_down")
    part_ffn = chip_partials(names_ffn, [gw_gate, gw_up, gw_down], list(got_gate) + list(got_up) + list(got_down))

    (dproj, merged, dya, dpp, dyx, dk, dv, stat_b2), got_ffn = mixer_bwd(
        dx1, proj, ya, pp, yx, probs, conv_full, w_co_f, w_pool_f, pool_scale, k, v, w_xo_f, w_out_f,
        carry=(_ChipExchange, [p16 for _, p16 in part_ffn]))
    gw_kv32, gw_kv16, stat_kv = kv_bwd(dk, dv, memn, mem[0], norm_mem, w_kv4)
    gw_pool = tuple(jnp.transpose(gw.reshape(GROUPS, N_CHIPS, 64, GROUP_DIM), (1, 0, 2, 3)).reshape(N_CHIPS, 256, GROUP_DIM)
                    for gw in grad_w_pool(pooled, dpp))
    gw_co, got_kv_pool = matmul_tn("grad_w_conv_out", a, dya, 1024, carry=(_HalfExchange, [gw_kv16, gw_pool[1]]))
    gw_co = by_chip(gw_co)
    gw_xo, got_co = matmul_tn("grad_w_xattn_out", o, dyx, 1024, carry=(_HalfExchange, [gw_co[1]]))
    gw_xo = by_chip(gw_xo)
    gw_out, got_xo = matmul_tn("grad_w_out", merged, dx1, 1024, carry=(_HalfExchange, [gw_xo[1]]))
    gw_out = by_chip(gw_out)
    got_out = exchange_call("exchange_halves_w_out", _HalfExchange, [gw_out[1]])
    names_mix = ("w_kv", "w_pool", "w_conv_out", "w_xattn_out", "w_out")
    part_mix = chip_partials(names_mix, [(gw_kv32, gw_kv16), gw_pool, gw_co, gw_xo, gw_out],
                             list(got_kv_pool) + list(got_co) + list(got_xo) + list(got_out))

    gw_in, got_mix = matmul_tn("grad_w_in", h, dproj, 2048, col_blocks=N_CHIPS,
                               carry=(_ChipExchange, [p16 for _, p16 in part_mix]))
    part_in = chip_partials(("w_in",), [gw_in], exchange_call("exchange_halves_w_in", _HalfExchange, [gw_in[1]]))
    mine_early = chip_sums(names_ffn + names_mix, part_ffn + part_mix, list(got_ffn) + list(got_mix))
    (grad_x, stat_b3), (got_in, theirs_early) = in_bwd(
        dproj, w_in_f, x2d, dx1, norm_mix,
        carry=[(_ChipExchange, [p16 for _, p16 in part_in]), (_SiblingSwap, mine_early)])
    mine_in = chip_sums(("w_in",), part_in, got_in)
    theirs_in = exchange_call("swap_halves_w_in", _SiblingSwap, mine_in)
    reduced = dict(zip(names_ffn + names_mix + ("w_in",),
                       zip(mine_early + mine_in, list(theirs_early) + list(theirs_in))))
    mine = [reduced[n][0] for n in BIG]
    theirs = [reduced[n][1] for n in BIG]

    pack = jnp.concatenate([stat_b3[0:1], stat_b2[1:2], stat_kv[0:1], stat_b1[0:1], stat_f[0:1], stat_b2[5:8],
                            stat_f[1:2], jnp.zeros((7, D), F32)], axis=0)
    total = all_reduce_small(pack)
    loss = jnp.sum(total[8])
    g_conv_full = total[5:8]
    g_conv = lax.dynamic_slice_in_dim(g_conv_full, chip * 256, 256, axis=1)

    given = dict(w_in=(w_in, m_w_in, v_w_in), w_conv_out=(w_conv_out, m_w_conv_out, v_w_conv_out),
                 w_pool=(w_pool, m_w_pool, v_w_pool), w_kv=(w_kv, m_w_kv, v_w_kv),
                 w_xattn_out=(w_xattn_out, m_w_xattn_out, v_w_xattn_out), w_out=(w_out, m_w_out, v_w_out),
                 w_gate=(w_gate, m_w_gate, v_w_gate), w_up=(w_up, m_w_up, v_w_up), w_down=(w_down, m_w_down, v_w_down))
    out_g, out_d, out_m, out_v = {}, {}, {}, {}
    for n, mine_n, theirs_n in zip(BIG, mine, theirs):
        transposed = n in ("w_gate", "w_up")
        rows2d = (2 * mine_n.shape[0], mine_n.shape[1])
        w_, m_, v_ = ((t2(t) if transposed else t).reshape(rows2d) for t in given[n])
        res = adamw_halves("adamw_" + n, w_, mine_n, theirs_n, m_, v_, core_arr)
        if transposed:
            res = [t2(t.reshape(1, D_FF // N_CHIPS, D)) for t in res]
        out_g[n], out_d[n], out_m[n], out_v[n] = (t.reshape(given[n][0].shape) for t in res)

    def small_pack(vals, conv_part):
        conv_rows = jnp.concatenate([conv_part.reshape(3, 256), jnp.zeros((3, D - 256), F32)], axis=1)
        return jnp.concatenate([val.reshape(1, D) for val in vals] + [conv_rows], axis=0)

    sw = small_pack([norm_mix, pool_scale, norm_mem, norm_ffn, norm_final], conv_w)
    sm = small_pack([m_norm_mix, m_pool_scale, m_norm_mem, m_norm_ffn, m_norm_final], m_conv_w)
    sv = small_pack([v_norm_mix, v_pool_scale, v_norm_mem, v_norm_ffn, v_norm_final], v_conv_w)
    sg = small_pack([total[r] for r in range(5)], g_conv)
    sd, snm, snv = adamw("adamw_small", sw, sg, sm, sv)
    small_names = ("norm_mix", "pool_scale", "norm_mem", "norm_ffn", "norm_final")
    small_shapes = dict(norm_mix=(1, D), pool_scale=(1, D), norm_mem=(1, D), norm_ffn=(1, D), norm_final=(D,))
    for r, n in enumerate(small_names):
        out_g[n], out_d[n], out_m[n], out_v[n] = (t[r].reshape(small_shapes[n]) for t in (sg, sd, snm, snv))
    out_g["conv_w"], out_d["conv_w"], out_m["conv_w"], out_v["conv_w"] = (
        t[5:8, 0:256].reshape(1, 3, 256) for t in (sg, sd, snm, snv))

    order = ("norm_mix", "w_in", "conv_w", "w_conv_out", "w_pool", "pool_scale", "norm_mem", "w_kv", "w_xattn_out",
             "w_out", "norm_ffn", "w_gate", "w_up", "w_down", "norm_final")
    return (loss, grad_x.reshape(1, t_len, D), *[out_g[n] for n in order], *[out_d[n] for n in order],
            *[out_m[n] for n in order], *[out_v[n] for n in order])
```

```python
import functools

import jax
import jax.numpy as jnp
from jax import lax
from jax.experimental import pallas as pl
from jax.experimental.pallas import tpu as pltpu

F32 = jnp.float32
BF16 = jnp.bfloat16
MESH = pl.DeviceIdType.MESH

D = 1024
N_MEM = 256
HEADS = 4
HEAD_DIM = 256
GROUPS = 4
GROUP_DIM = 256
POOL_WINDOWS = (2, 4, 8, 16)
D_FF = 2816
D_IN = 8192
N_CHIPS = 4
EPS = 1e-6
HALO = 16
POOL_PAD = 128
ATT_SCALE = HEAD_DIM ** -0.5

ADAM_LR = 0.001
ADAM_B1 = 0.9
ADAM_B2 = 0.999
ADAM_EPS = 1e-08
ADAM_WD = 0.01
ADAM_STEP = 10

VMEM_LIMIT = 56 * 1024 * 1024

O_BA, O_CA, O_UA, O_UP, O_QX, O_GA, O_GP, O_GX = (k * D for k in range(8))

NT_DIMS = (((1,), (1,)), ((), ()))
TN_DIMS = (((0,), (0,)), ((), ()))


def _dot(a, b):
    return jnp.dot(a, b, preferred_element_type=F32)


def _dot_nt(a, b):
    return lax.dot_general(a, b, NT_DIMS, preferred_element_type=F32)


def _dot_tn(a, b):
    return lax.dot_general(a, b, TN_DIMS, preferred_element_type=F32)


def _sigmoid(z):
    return pl.reciprocal(1.0 + jnp.exp(-z), approx=True)


def _params(semantics=None):
    return pltpu.CompilerParams(dimension_semantics=semantics, vmem_limit_bytes=VMEM_LIMIT)


def _resident(shape):
    zeros = (0,) * len(shape)
    return pl.BlockSpec(shape, lambda *_: zeros, pipeline_mode=pl.Buffered(1))


def _const(shape):
    zeros = (0,) * len(shape)
    return pl.BlockSpec(shape, lambda *_: zeros)


def _rows(tm, width):
    return pl.BlockSpec((tm, width), lambda i: (i, 0))


def _inv_count(tile, tm, window):
    t = tile * tm + lax.broadcasted_iota(jnp.int32, (tm, 1), 0)
    return 1.0 / jnp.minimum(t + 1, window).astype(F32)


def _carried_call(body, name, grid, in_specs, out_specs, out_shape, scratch_shapes, semantics, args, carry):
    if carry is None:
        res = pl.pallas_call(body, name=name, grid=grid, in_specs=in_specs, out_specs=out_specs, out_shape=out_shape,
                             scratch_shapes=scratch_shapes, compiler_params=_params(semantics))(*args)
        return res, []
    carries = [carry] if isinstance(carry, tuple) else list(carry)
    comm_args = [arr for _, arrs in carries for arr in arrs]
    n, n_in, n_out, n_scratch = len(comm_args), len(in_specs), len(out_specs), len(scratch_shapes)
    shapes_of = [cls.out_shapes(arrs) for cls, arrs in carries]
    comm_shapes = [s for shapes in shapes_of for s in shapes]
    m = len(comm_shapes)
    comm_sems = [s for cls, arrs in carries for s in cls.sems(len(arrs))]

    def carrying(*refs):
        ins, comm_ins = refs[:n_in], refs[n_in:n_in + n]
        outs, comm_outs = refs[n_in + n:n_in + n + n_out], refs[n_in + n + n_out:n_in + n + n_out + m]
        scratch, sems = refs[n_in + n + n_out + m:n_in + n + n_out + m + n_scratch], refs[n_in + n + n_out + m + n_scratch:]
        steps = [pl.program_id(d) for d in range(len(grid))]
        first = functools.reduce(jnp.logical_and, [s == 0 for s in steps])
        last = functools.reduce(jnp.logical_and, [s == g - 1 for s, g in zip(steps, grid)])

        def exchanges():
            at_in = at_out = 0
            for k, (cls, arrs) in enumerate(carries):
                yield cls(comm_ins[at_in:at_in + len(arrs)], comm_outs[at_out:at_out + len(shapes_of[k])],
                          sems[2 * k], sems[2 * k + 1])
                at_in, at_out = at_in + len(arrs), at_out + len(shapes_of[k])

        @pl.when(first)
        def _():
            for exchange in exchanges():
                exchange.start()

        linear, total = 0, 1
        for s, g in zip(steps, grid):
            linear, total = linear * g + s, total * g
        for k, (cls, _) in enumerate(carries):
            if hasattr(cls, "middle"):
                @pl.when(linear == (cls.MIDDLE_AT[0] * total) // cls.MIDDLE_AT[1])
                def _(k=k):
                    list(exchanges())[k].middle()

        body(*ins, *outs, *scratch)

        @pl.when(last)
        def _():
            for exchange in exchanges():
                exchange.finish()

    res = pl.pallas_call(
        carrying, name=name, grid=grid, in_specs=list(in_specs) + _any_specs(n), out_specs=list(out_specs) + _any_specs(m),
        out_shape=list(out_shape) + comm_shapes, scratch_shapes=list(scratch_shapes) + comm_sems,
        compiler_params=_params(semantics))(*args, *comm_args)
    comm_res, at = [], n_out
    for (_, arrs), shapes in zip(carries, shapes_of):
        comm_res.append(res[at:at + len(arrs)])
        at += len(shapes)
    return res[:n_out], (comm_res[0] if isinstance(carry, tuple) else comm_res)


def proj_fwd(x, g_mix, w_in, carry=None):
    t_len = x.shape[0]
    tm = min(1024, t_len)
    tn = D_IN // N_CHIPS

    def body(x_ref, g_ref, w_ref, proj_ref, h_ref):
        @pl.when(pl.program_id(1) == 0)
        def _():
            xv = x_ref[...]
            r = lax.rsqrt(jnp.mean(xv * xv, axis=-1, keepdims=True) + EPS)
            h_ref[...] = (xv * r * g_ref[...]).astype(BF16)

        proj_ref[...] = _dot(h_ref[...], w_ref[...]).astype(BF16)

    return _carried_call(
        body, "proj_fwd", (t_len // tm, N_CHIPS),
        in_specs=[pl.BlockSpec((tm, D), lambda i, j: (i, 0)),
                  pl.BlockSpec((1, D), lambda i, j: (0, 0)),
                  pl.BlockSpec((D, tn), lambda i, j: (0, j))],
        out_specs=[pl.BlockSpec((tm, tn), lambda i, j: (i, j)),
                   pl.BlockSpec((tm, D), lambda i, j: (i, 0))],
        out_shape=[jax.ShapeDtypeStruct((t_len, D_IN), BF16), jax.ShapeDtypeStruct((t_len, D), BF16)],
        scratch_shapes=[], semantics=("arbitrary", "arbitrary"), args=(x, g_mix, w_in), carry=carry)


def kv_fwd(mem, g_mem, w_kv4):
    half = D // 2

    def body(mem_ref, g_ref, w_ref, memn_ref, k_ref, v_ref):
        mv = mem_ref[...]
        r = lax.rsqrt(jnp.mean(mv * mv, axis=-1, keepdims=True) + EPS)
        mn = (mv * r * g_ref[...]).astype(BF16)
        memn_ref[...] = mn
        k_ref[:, 0:half] = _dot(mn, w_ref[0]).astype(BF16)
        k_ref[:, half:D] = _dot(mn, w_ref[1]).astype(BF16)
        v_ref[:, 0:half] = _dot(mn, w_ref[2]).astype(BF16)
        v_ref[:, half:D] = _dot(mn, w_ref[3]).astype(BF16)

    out = jax.ShapeDtypeStruct((N_MEM, D), BF16)
    return pl.pallas_call(body, name="kv_fwd", out_shape=[out, out, out], compiler_params=_params())(mem, g_mem, w_kv4)


def _softmax_rows(s):
    m = jnp.max(s, axis=-1, keepdims=True)
    e = jnp.exp(s - m)
    return e * pl.reciprocal(jnp.sum(e, axis=-1, keepdims=True), approx=True)


def _window_bands(tm, causal):
    t = lax.broadcasted_iota(jnp.int32, (tm, tm + POOL_PAD), 0)
    s = lax.broadcasted_iota(jnp.int32, (tm, tm + POOL_PAD), 1)
    d = (t + POOL_PAD - s) if causal else (s - t)
    return jnp.stack([((d >= 0) & (d < w)).astype(BF16) for w in POOL_WINDOWS])


def mixer_fwd(proj, x, conv_w8, w_co, w_pool, pool_scale, k, v, w_xo, w_out, carry=None):
    t_len = x.shape[0]
    tm = min(256, t_len)

    def body(proj_ref, x_ref, cw_ref, wco_ref, wpool_ref, ps_ref, k_ref, v_ref, wxo_ref, wout_ref,
             a_ref, pooled_ref, ya_ref, pp_ref, yx_ref, o_ref, p_ref, x1_ref, cu_ext, up_ext):
        i = pl.program_id(0)

        @pl.when(i == 0)
        def _():
            cu_ext[0:HALO, :] = jnp.zeros((HALO, D), F32)
            up_ext[0:HALO, :] = jnp.zeros((HALO, D), F32)

        cu = proj_ref[:, O_CA:O_CA + D].astype(F32) * proj_ref[:, O_UA:O_UA + D].astype(F32)
        cu_ext[HALO:HALO + tm, :] = cu
        conv = (cw_ref[2:3, :] * cu + cw_ref[1:2, :] * cu_ext[HALO - 1:HALO - 1 + tm, :]
                + cw_ref[0:1, :] * cu_ext[HALO - 2:HALO - 2 + tm, :])
        a = (proj_ref[:, O_BA:O_BA + D].astype(F32) * conv).astype(BF16)
        a_ref[...] = a
        ya = _dot(a, wco_ref[...])
        ya_ref[...] = ya.astype(BF16)

        up_ext[HALO:HALO + tm, :] = proj_ref[:, O_UP:O_UP + D].astype(F32)
        for g, window in enumerate(POOL_WINDOWS):
            cols = slice(g * GROUP_DIM, (g + 1) * GROUP_DIM)
            tok = up_ext[HALO:HALO + tm, cols]
            acc = tok
            for j in range(1, window):
                acc = acc + up_ext[HALO - j:HALO - j + tm, cols]
            pooled = (acc * _inv_count(i, tm, window) - tok).astype(BF16)
            pooled_ref[:, cols] = pooled
            pp_ref[:, cols] = _dot(pooled, wpool_ref[g]).astype(BF16)

        for hd in range(HEADS):
            cols = slice(hd * HEAD_DIM, (hd + 1) * HEAD_DIM)
            q = proj_ref[:, O_QX + hd * HEAD_DIM:O_QX + (hd + 1) * HEAD_DIM]
            p = _softmax_rows(_dot_nt(q, k_ref[:, cols]) * ATT_SCALE).astype(BF16)
            p_ref[:, hd * N_MEM:(hd + 1) * N_MEM] = p
            o_ref[:, cols] = _dot(p, v_ref[:, cols]).astype(BF16)
        yx = _dot(o_ref[...], wxo_ref[...])
        yx_ref[...] = yx.astype(BF16)

        merged = (_sigmoid(proj_ref[:, O_GA:O_GA + D].astype(F32)) * ya
                  + _sigmoid(proj_ref[:, O_GP:O_GP + D].astype(F32)) * (pp_ref[...].astype(F32) * ps_ref[...])
                  + _sigmoid(proj_ref[:, O_GX:O_GX + D].astype(F32)) * yx)
        x1_ref[...] = x_ref[...] + _dot(merged.astype(BF16), wout_ref[...])

        cu_ext[0:HALO, :] = cu_ext[tm:tm + HALO, :]
        up_ext[0:HALO, :] = up_ext[tm:tm + HALO, :]

    act = jax.ShapeDtypeStruct((t_len, D), BF16)
    return _carried_call(
        body, "mixer_fwd", (t_len // tm,),
        in_specs=[_rows(tm, D_IN), _rows(tm, D), _resident((8, D)), _resident((D, D)),
                  _resident((GROUPS, GROUP_DIM, GROUP_DIM)), _resident((1, D)),
                  _resident((N_MEM, D)), _resident((N_MEM, D)), _resident((D, D)), _resident((D, D))],
        out_specs=[_rows(tm, D)] * 7 + [_rows(tm, D)],
        out_shape=[act] * 6 + [jax.ShapeDtypeStruct((t_len, HEADS * N_MEM), BF16), jax.ShapeDtypeStruct((t_len, D), F32)],
        scratch_shapes=[pltpu.VMEM((tm + HALO, D), F32), pltpu.VMEM((tm + HALO, D), F32)],
        semantics=("arbitrary",), args=(proj, x, conv_w8, w_co, w_pool, pool_scale, k, v, w_xo, w_out), carry=carry)


def ffn_fwd(x1, target, g_ffn, w_gate, w_up, w_down, g_final):
    t_len = x1.shape[0]
    tm = min(512, t_len)

    def body(x1_ref, tgt_ref, g_ref, wg_ref, wu_ref, wd_ref, gf_ref, gate_ref, up_ref, dx2_ref, stat_ref):
        @pl.when(pl.program_id(0) == 0)
        def _():
            stat_ref[...] = jnp.zeros((8, D), F32)

        x1v = x1_ref[...]
        r2 = lax.rsqrt(jnp.mean(x1v * x1v, axis=-1, keepdims=True) + EPS)
        h2 = (x1v * r2 * g_ref[...]).astype(BF16)
        gate = _dot_nt(h2, wg_ref[...])
        up = _dot_nt(h2, wu_ref[...])
        gate_ref[...] = gate.astype(BF16)
        up_ref[...] = up.astype(BF16)
        act = (gate * _sigmoid(gate) * up).astype(BF16)
        x2 = x1v + _dot(act, wd_ref[...])
        r3 = lax.rsqrt(jnp.mean(x2 * x2, axis=-1, keepdims=True) + EPS)
        xh = x2 * r3
        diff = xh * gf_ref[...] - tgt_ref[...]
        dy = diff * (1.0 / D)
        stat_ref[0:1, :] += jnp.sum(dy * xh, axis=0, keepdims=True)
        stat_ref[1:2, :] += (0.5 / D) * jnp.sum(diff * diff, axis=0, keepdims=True)
        dxh = dy * gf_ref[...]
        dx2_ref[...] = r3 * (dxh - xh * jnp.mean(dxh * xh, axis=-1, keepdims=True))

    return pl.pallas_call(
        body, name="ffn_fwd",
        grid=(t_len // tm,),
        in_specs=[_rows(tm, D), _rows(tm, D), _resident((1, D)), _resident((D_FF, D)), _resident((D_FF, D)),
                  _resident((D_FF, D)), _resident((1, D))],
        out_specs=[_rows(tm, D_FF), _rows(tm, D_FF), _rows(tm, D), _const((8, D))],
        out_shape=[jax.ShapeDtypeStruct((t_len, D_FF), BF16), jax.ShapeDtypeStruct((t_len, D_FF), BF16),
                   jax.ShapeDtypeStruct((t_len, D), F32), jax.ShapeDtypeStruct((8, D), F32)],
        compiler_params=_params(("arbitrary",)),
    )(x1, target, g_ffn, w_gate, w_up, w_down, g_final)


def ffn_bwd(dx2, x1, gate, up, g_ffn, w_gate, w_up, w_down):
    t_len = x1.shape[0]
    tm = min(256, t_len)

    def body(dx2_ref, x1_ref, gate_ref, up_ref, g_ref, wg_ref, wu_ref, wd_ref,
             dx1_ref, dgate_ref, dup_ref, act_ref, h2_ref, stat_ref):
        @pl.when(pl.program_id(0) == 0)
        def _():
            stat_ref[...] = jnp.zeros((8, D), F32)

        dx2v = dx2_ref[...]
        gate = gate_ref[...]
        upv = up_ref[...]
        sg = _sigmoid(gate.astype(F32)).astype(BF16)
        silu = gate * sg
        act_ref[...] = silu * upv
        dact = _dot_nt(dx2v.astype(BF16), wd_ref[...]).astype(BF16)
        dup = dact * silu
        dgate = dact * upv * (sg * (1.0 + gate * (1.0 - sg)))
        dup_ref[...] = dup
        dgate_ref[...] = dgate
        dh2 = _dot(dgate, wg_ref[...]) + _dot(dup, wu_ref[...])
        x1v = x1_ref[...]
        r2 = lax.rsqrt(jnp.mean(x1v * x1v, axis=-1, keepdims=True) + EPS)
        xh = x1v * r2
        h2_ref[...] = (xh * g_ref[...]).astype(BF16)
        stat_ref[0:1, :] += jnp.sum(dh2 * xh, axis=0, keepdims=True)
        dxh = dh2 * g_ref[...]
        dx1_ref[...] = dx2v + r2 * (dxh - xh * jnp.mean(dxh * xh, axis=-1, keepdims=True))

    ff = jax.ShapeDtypeStruct((t_len, D_FF), BF16)
    return pl.pallas_call(
        body, name="ffn_bwd",
        grid=(t_len // tm,),
        in_specs=[_rows(tm, D), _rows(tm, D), _rows(tm, D_FF), _rows(tm, D_FF), _resident((1, D)),
                  _resident((D_FF, D)), _resident((D_FF, D)), _resident((D_FF, D))],
        out_specs=[_rows(tm, D), _rows(tm, D_FF), _rows(tm, D_FF), _rows(tm, D_FF), _rows(tm, D), _const((8, D))],
        out_shape=[jax.ShapeDtypeStruct((t_len, D), F32), ff, ff, ff, jax.ShapeDtypeStruct((t_len, D), BF16),
                   jax.ShapeDtypeStruct((8, D), F32)],
        compiler_params=_params(("arbitrary",)),
    )(dx2, x1, gate, up, g_ffn, w_gate, w_up, w_down)


def mixer_bwd(dx1, proj, ya, pp, yx, probs, conv_w8, w_co, w_pool, pool_scale, k, v, w_xo, w_out, carry=None):
    t_len = dx1.shape[0]
    tm = min(256, t_len)
    n_tiles = t_len // tm
    halo_blocks = tm // HALO

    def body(dx1_ref, proj_ref, halo_ref, ya_ref, pp_ref, yx_ref, p_ref,
             cw_ref, wco_ref, wpool_ref, ps_ref, k_ref, v_ref, wxo_ref, wout_ref, band_ref,
             dproj_ref, merged_ref, dya_ref, dpp_ref, dyx_ref, dk_ref, dv_ref, stat_ref,
             cu_ext, dconv_ext, dpn_ext):
        step = pl.program_id(0)
        tile = n_tiles - 1 - step

        @pl.when(step == 0)
        def _():
            dk_ref[...] = jnp.zeros((N_MEM, D), F32)
            dv_ref[...] = jnp.zeros((N_MEM, D), F32)
            stat_ref[...] = jnp.zeros((8, D), F32)
            dconv_ext[tm:tm + HALO, :] = jnp.zeros((HALO, D), F32)
            dpn_ext[tm:tm + POOL_PAD, :] = jnp.zeros((POOL_PAD, D), BF16)

        dmerged = _dot_nt(dx1_ref[...].astype(BF16), wout_ref[...]).astype(BF16)
        sa = _sigmoid(proj_ref[:, O_GA:O_GA + D].astype(F32)).astype(BF16)
        sp = _sigmoid(proj_ref[:, O_GP:O_GP + D].astype(F32)).astype(BF16)
        sx = _sigmoid(proj_ref[:, O_GX:O_GX + D].astype(F32)).astype(BF16)
        ya = ya_ref[...]
        ppv = pp_ref[...]
        scale = ps_ref[...].astype(BF16)
        yp = ppv * scale
        yx = yx_ref[...]
        merged_ref[...] = sa * ya + sp * yp + sx * yx
        dproj_ref[:, O_GA:O_GA + D] = dmerged * ya * (sa * (1.0 - sa))
        dproj_ref[:, O_GP:O_GP + D] = dmerged * yp * (sp * (1.0 - sp))
        dproj_ref[:, O_GX:O_GX + D] = dmerged * yx * (sx * (1.0 - sx))
        dya = dmerged * sa
        dyp = dmerged * sp
        dyx = dmerged * sx
        dya_ref[...] = dya
        dyx_ref[...] = dyx
        stat_ref[1:2, :] += jnp.sum(dyp.astype(F32) * ppv.astype(F32), axis=0, keepdims=True)
        dpp = dyp * scale
        dpp_ref[...] = dpp

        da = _dot_nt(dya, wco_ref[...])
        c_a = proj_ref[:, O_CA:O_CA + D].astype(F32)
        u_a = proj_ref[:, O_UA:O_UA + D].astype(F32)
        cu = c_a * u_a
        halo_cu = halo_ref[:, O_CA:O_CA + D].astype(F32) * halo_ref[:, O_UA:O_UA + D].astype(F32)
        cu_ext[0:HALO, :] = jnp.where(tile > 0, halo_cu, 0.0)
        cu_ext[HALO:HALO + tm, :] = cu
        cu1 = cu_ext[HALO - 1:HALO - 1 + tm, :]
        cu2 = cu_ext[HALO - 2:HALO - 2 + tm, :]
        conv = cw_ref[2:3, :] * cu + cw_ref[1:2, :] * cu1 + cw_ref[0:1, :] * cu2
        dproj_ref[:, O_BA:O_BA + D] = (da * conv).astype(BF16)
        dconv = da * proj_ref[:, O_BA:O_BA + D].astype(F32)
        stat_ref[5:6, :] += jnp.sum(dconv * cu2, axis=0, keepdims=True)
        stat_ref[6:7, :] += jnp.sum(dconv * cu1, axis=0, keepdims=True)
        stat_ref[7:8, :] += jnp.sum(dconv * cu, axis=0, keepdims=True)
        dconv_ext[0:tm, :] = dconv
        dcu = (cw_ref[2:3, :] * dconv + cw_ref[1:2, :] * dconv_ext[1:1 + tm, :]
               + cw_ref[0:1, :] * dconv_ext[2:2 + tm, :])
        dproj_ref[:, O_CA:O_CA + D] = (dcu * u_a).astype(BF16)
        dproj_ref[:, O_UA:O_UA + D] = (dcu * c_a).astype(BF16)

        for g, window in enumerate(POOL_WINDOWS):
            cols = slice(g * GROUP_DIM, (g + 1) * GROUP_DIM)
            dpooled = _dot_nt(dpp[:, cols], wpool_ref[g])
            dpn_ext[0:tm, cols] = (dpooled * _inv_count(tile, tm, window)).astype(BF16)
            acc = _dot(band_ref[g], dpn_ext[:, cols])
            dproj_ref[:, O_UP + g * GROUP_DIM:O_UP + (g + 1) * GROUP_DIM] = (acc - dpooled).astype(BF16)

        do = _dot_nt(dyx, wxo_ref[...])
        for hd in range(HEADS):
            cols = slice(hd * HEAD_DIM, (hd + 1) * HEAD_DIM)
            q = proj_ref[:, O_QX + hd * HEAD_DIM:O_QX + (hd + 1) * HEAD_DIM]
            kh = k_ref[:, cols]
            p16 = p_ref[:, hd * N_MEM:(hd + 1) * N_MEM]
            p = p16.astype(F32)
            doh = do[:, cols].astype(BF16)
            dp = _dot_nt(doh, v_ref[:, cols])
            dv_ref[:, cols] += _dot_tn(p16, doh)
            ds = (p * (dp - jnp.sum(dp * p, axis=-1, keepdims=True)) * ATT_SCALE).astype(BF16)
            dproj_ref[:, O_QX + hd * HEAD_DIM:O_QX + (hd + 1) * HEAD_DIM] = _dot(ds, kh).astype(BF16)
            dk_ref[:, cols] += _dot_tn(ds, q)

        dconv_ext[tm:tm + HALO, :] = dconv_ext[0:HALO, :]
        dpn_ext[tm:tm + HALO, :] = dpn_ext[0:HALO, :]

    def rev(width):
        return pl.BlockSpec((tm, width), lambda s: (n_tiles - 1 - s, 0))

    halo_spec = pl.BlockSpec((HALO, D_IN), lambda s: (jnp.maximum((n_tiles - 1 - s) * halo_blocks - 1, 0), 0))
    act = jax.ShapeDtypeStruct((t_len, D), BF16)
    kv_grad = jax.ShapeDtypeStruct((N_MEM, D), F32)
    return _carried_call(
        body, "mixer_bwd", (n_tiles,),
        in_specs=[rev(D), rev(D_IN), halo_spec, rev(D), rev(D), rev(D), rev(D),
                  _resident((8, D)), _resident((D, D)), _resident((GROUPS, GROUP_DIM, GROUP_DIM)), _resident((1, D)),
                  _resident((N_MEM, D)), _resident((N_MEM, D)), _resident((D, D)), _resident((D, D)),
                  _resident((GROUPS, tm, tm + POOL_PAD))],
        out_specs=[rev(D_IN), rev(D), rev(D), rev(D), rev(D),
                   _const((N_MEM, D)), _const((N_MEM, D)), _const((8, D))],
        out_shape=[jax.ShapeDtypeStruct((t_len, D_IN), BF16), act, act, act, act, kv_grad, kv_grad,
                   jax.ShapeDtypeStruct((8, D), F32)],
        scratch_shapes=[pltpu.VMEM((tm + HALO, D), F32)] * 2 + [pltpu.VMEM((tm + POOL_PAD, D), BF16)],
        semantics=("arbitrary",),
        args=(dx1, proj, proj, ya, pp, yx, probs, conv_w8, w_co, w_pool, pool_scale, k, v, w_xo, w_out,
              _window_bands(tm, False)), carry=carry)


def in_bwd(dproj, w_in, x, dx1, g_mix, carry=None):
    t_len = x.shape[0]
    tm = min(512, t_len)

    def body(dproj_ref, w_ref, x_ref, dx1_ref, g_ref, gx_ref, stat_ref):
        @pl.when(pl.program_id(0) == 0)
        def _():
            stat_ref[...] = jnp.zeros((8, D), F32)

        dh = _dot_nt(dproj_ref[...], w_ref[...])
        xv = x_ref[...]
        r = lax.rsqrt(jnp.mean(xv * xv, axis=-1, keepdims=True) + EPS)
        xh = xv * r
        stat_ref[0:1, :] += jnp.sum(dh * xh, axis=0, keepdims=True)
        dxh = dh * g_ref[...]
        gx_ref[...] = dx1_ref[...] + r * (dxh - xh * jnp.mean(dxh * xh, axis=-1, keepdims=True))

    return _carried_call(
        body, "in_bwd", (t_len // tm,),
        in_specs=[_rows(tm, D_IN), _resident((D, D_IN)), _rows(tm, D), _rows(tm, D), _resident((1, D))],
        out_specs=[_rows(tm, D), _const((8, D))],
        out_shape=[jax.ShapeDtypeStruct((t_len, D), F32), jax.ShapeDtypeStruct((8, D), F32)],
        scratch_shapes=[], semantics=("arbitrary",), args=(dproj, w_in, x, dx1, g_mix), carry=carry)


def kv_bwd(dk, dv, memn, mem, g_mem, w_kv4):
    half = D // 2

    def body(dk_ref, dv_ref, memn_ref, mem_ref, g_ref, w_ref, gw_ref, gw16_ref, stat_ref):
        mn = memn_ref[...]
        parts = (dk_ref[:, 0:half], dk_ref[:, half:D], dv_ref[:, 0:half], dv_ref[:, half:D])
        dmemn = jnp.zeros((N_MEM, D), F32)
        for j, part in enumerate(parts):
            part = part.astype(BF16)
            gw = _dot_tn(mn, part)
            gw_ref[j] = gw
            gw16_ref[j] = gw.astype(BF16)
            dmemn = dmemn + _dot_nt(part, w_ref[j])
        mv = mem_ref[...]
        r = lax.rsqrt(jnp.mean(mv * mv, axis=-1, keepdims=True) + EPS)
        stat_ref[...] = jnp.zeros((8, D), F32)
        stat_ref[0:1, :] = jnp.sum(dmemn * (mv * r), axis=0, keepdims=True)

    return pl.pallas_call(
        body, name="kv_bwd",
        out_shape=[jax.ShapeDtypeStruct((N_CHIPS, D, half), F32), jax.ShapeDtypeStruct((N_CHIPS, D, half), BF16),
                   jax.ShapeDtypeStruct((8, D), F32)],
        compiler_params=_params(),
    )(dk, dv, memn, mem, g_mem, w_kv4)


def matmul_tn(name, a, b, tn, col_blocks=1, carry=None):
    t_len, k_dim = a.shape
    n_dim = b.shape[1]
    tt = min(1024, t_len)
    per_block = n_dim // col_blocks // tn

    def body(a_ref, b_ref, out_ref, out16_ref):
        @pl.when(pl.program_id(1) == 0)
        def _():
            out_ref[...] = jnp.zeros((k_dim, tn), F32)

        out_ref[...] += _dot_tn(a_ref[...].astype(BF16), b_ref[...].astype(BF16))

        @pl.when(pl.program_id(1) == t_len // tt - 1)
        def _():
            out16_ref[...] = out_ref[...].astype(BF16)

    if col_blocks == 1:
        out_spec = pl.BlockSpec((k_dim, tn), lambda n, t: (0, n))
        shape = (k_dim, n_dim)
    else:
        out_spec = pl.BlockSpec((None, k_dim, tn), lambda n, t: (n // per_block, 0, n % per_block))
        shape = (col_blocks, k_dim, n_dim // col_blocks)
    outs, carried = _carried_call(
        body, name, (n_dim // tn, t_len // tt),
        in_specs=[pl.BlockSpec((tt, k_dim), lambda n, t: (t, 0)), pl.BlockSpec((tt, tn), lambda n, t: (t, n))],
        out_specs=[out_spec, out_spec], out_shape=[jax.ShapeDtypeStruct(shape, F32), jax.ShapeDtypeStruct(shape, BF16)],
        scratch_shapes=[], semantics=("arbitrary", "arbitrary"), args=(a, b), carry=carry)
    return (tuple(outs), carried) if carry is not None else tuple(outs)


def grad_w_pool(pooled, dpp):
    t_len = pooled.shape[0]
    tt = min(1024, t_len)
    steps = t_len // tt

    def body(a_ref, b_ref, out_ref, out16_ref):
        @pl.when(pl.program_id(0) == 0)
        def _():
            out_ref[...] = jnp.zeros((GROUPS, GROUP_DIM, GROUP_DIM), F32)

        for g in range(GROUPS):
            cols = slice(g * GROUP_DIM, (g + 1) * GROUP_DIM)
            out_ref[g] += _dot_tn(a_ref[:, cols], b_ref[:, cols])

        @pl.when(pl.program_id(0) == steps - 1)
        def _():
            out16_ref[...] = out_ref[...].astype(BF16)

    shape = (GROUPS, GROUP_DIM, GROUP_DIM)
    return pl.pallas_call(
        body, name="grad_w_pool", grid=(steps,),
        in_specs=[_rows(tt, D), _rows(tt, D)], out_specs=[_const(shape), _const(shape)],
        out_shape=[jax.ShapeDtypeStruct(shape, F32), jax.ShapeDtypeStruct(shape, BF16)],
        compiler_params=_params(("arbitrary",)),
    )(pooled, dpp)


def _place():
    x, y, c = lax.axis_index("x"), lax.axis_index("y"), lax.axis_index("c")
    return x, y, c


def _other_chips(x, y):
    return [(1 - x, y), (x, 1 - y), (1 - x, 1 - y)]


def _any_specs(n):
    return [pl.BlockSpec(memory_space=pl.ANY)] * n


def exchange_call(name, exchange_cls, arrays):
    n = len(arrays)
    shapes = exchange_cls.out_shapes(arrays)

    def body(*refs):
        exchange = exchange_cls(refs[:n], refs[n:n + len(shapes)], *refs[n + len(shapes):])
        exchange.start()
        if hasattr(exchange, "middle"):
            exchange.middle()
        exchange.finish()

    res = pl.pallas_call(
        body, name=name, in_specs=_any_specs(n), out_specs=_any_specs(len(shapes)),
        out_shape=shapes, scratch_shapes=exchange_cls.sems(n),
    )(*arrays)
    return res[:n]


class _Gather:
    SLOTS = 8
    ROW_ALIGN = 16
    MIDDLE_AT = (5, 8)

    @staticmethod
    def out_shapes(shards):
        return [jax.ShapeDtypeStruct((N_CHIPS,) + s.shape, s.dtype) for s in shards]

    @staticmethod
    def sems(n):
        return [pltpu.SemaphoreType.DMA((n, _Gather.SLOTS)), pltpu.SemaphoreType.DMA((n, _Gather.SLOTS))]

    def __init__(self, ins, outs, send_sems, recv_sems):
        self.ins, self.outs, self.send_sems, self.recv_sems = ins, outs, send_sems, recv_sems
        x, y, c = _place()
        self.c, self.me, self.sibling = c, 2 * x + y, (x, y, 1 - c)
        self.across = [(1 - x, y), (x, 1 - y), (1 - x, 1 - y)]

    def _rows(self, a, which, part=None):
        half = self.ins[a].shape[1] // 2
        first = (half // 2) // self.ROW_ALIGN * self.ROW_ALIGN
        if part is None:
            return pl.ds(which * half, half)
        return pl.ds(which * half, first) if part == 0 else pl.ds(which * half + first, half - first)

    def _has_part(self, a, part):
        half = self.ins[a].shape[1] // 2
        return part == 1 or (half // 2) // self.ROW_ALIGN > 0

    def _block(self, a, chip, rows=slice(None)):
        index = chip if not isinstance(chip, tuple) else 2 * chip[0] + chip[1]
        if len(self.outs[a].shape) == len(self.ins[a].shape):
            cols = self.ins[a].shape[2]
            return self.outs[a].at[:, rows, pl.ds(pl.multiple_of(index * cols, cols), cols)]
        return self.outs[a].at[index, :, rows, :]

    def _remote(self, src, dst, a, slot, to):
        return pltpu.make_async_remote_copy(src_ref=src, dst_ref=dst, send_sem=self.send_sems.at[a, slot],
                                            recv_sem=self.recv_sems.at[a, slot], device_id=to, device_id_type=MESH)

    def _own(self, a):
        return self._remote(self.ins[a], self._block(a, self.me), a, 6, self.sibling)

    def _sent(self, a, axis):
        rows = self._rows(a, self.c)
        return self._remote(self.ins[a].at[:, rows, :], self._block(a, self.me, rows), a, axis,
                            (*self.across[axis], self.c))

    def _landed(self, a, axis):
        block = self._block(a, self.across[axis], self._rows(a, self.c))
        return self._remote(block, block, a, axis, (*self.across[axis], self.c))

    def _relayed(self, a, part, incoming):
        source = self.across[2] if incoming else self.across[part]
        block = self._block(a, source, self._rows(a, self.c, part))
        return self._remote(block, block, a, (2, 7)[part], (*self.across[1 - part], self.c))

    def _passed_on(self, a, source, which):
        block = self._block(a, self.across[source], self._rows(a, which))
        return self._remote(block, block, a, 3 + source, self.sibling)

    def start(self):
        for a in range(len(self.ins)):
            self._own(a).start()
        for a in range(len(self.ins)):
            for axis in range(2):
                self._sent(a, axis).start()

    def middle(self):
        for a in range(len(self.ins)):
            for axis in range(2):
                self._landed(a, axis).wait_recv()
                if self._has_part(a, axis):
                    self._relayed(a, axis, incoming=False).start()
                self._passed_on(a, axis, self.c).start()

    def finish(self):
        n = len(self.ins)
        for a in range(n):
            for part in range(2):
                if self._has_part(a, part):
                    self._relayed(a, part, incoming=True).wait_recv()
            self._passed_on(a, 2, self.c).start()
        for a in range(n):
            for source in range(3):
                self._passed_on(a, source, 1 - self.c).wait_recv()
            self._own(a).wait_recv()
        for a in range(n):
            self._own(a).wait_send()
            for axis in range(2):
                self._sent(a, axis).wait_send()
                if self._has_part(a, axis):
                    self._relayed(a, axis, incoming=False).wait_send()
            for source in range(3):
                self._passed_on(a, source, self.c).wait_send()


class _HalfExchange:
    @staticmethod
    def out_shapes(grads):
        return [jax.ShapeDtypeStruct((N_CHIPS, g.shape[1] // 2, g.shape[2]), g.dtype) for g in grads]

    @staticmethod
    def sems(n):
        return [pltpu.SemaphoreType.DMA((n,)), pltpu.SemaphoreType.DMA((n,))]

    def __init__(self, ins, outs, send_sems, recv_sems):
        self.ins, self.outs, self.send_sems, self.recv_sems = ins, outs, send_sems, recv_sems

    def _copies(self):
        x, y, c = _place()
        for a in range(len(self.ins)):
            h = self.ins[a].shape[1] // 2
            yield pltpu.make_async_remote_copy(
                src_ref=self.ins[a].at[:, pl.ds((1 - c) * h, h), :], dst_ref=self.outs[a],
                send_sem=self.send_sems.at[a], recv_sem=self.recv_sems.at[a], device_id=(x, y, 1 - c), device_id_type=MESH)

    def start(self):
        for cp in self._copies():
            cp.start()

    def finish(self):
        for cp in self._copies():
            cp.wait()


class _GatherColumns(_Gather):
    @staticmethod
    def out_shapes(shards):
        return [jax.ShapeDtypeStruct(s.shape[:2] + (N_CHIPS * s.shape[2],), s.dtype) for s in shards]


def _row_tile(rows, cols, budget=2 << 20):
    best = 16
    for tr in range(16, rows + 1, 16):
        if rows % tr == 0 and tr * cols * 4 <= budget:
            best = tr
    return best


def add_sibling_half(name, pairs, place):
    k = len(pairs)
    _, rows, cols = pairs[0][0].shape
    h = rows // 2
    tr = _row_tile(h, cols)
    per_half = h // tr

    def body(place_ref, *refs):
        for a in range(k):
            @pl.when(pl.program_id(0) == a)
            def _(a=a):
                g_ref, o_ref = refs[2 * a:2 * a + 2]
                own_ref, out16_ref = refs[2 * k + 2 * a:2 * k + 2 * a + 2]
                total = g_ref[...] + o_ref[...].astype(F32)
                out16_ref[...] = total.astype(BF16)

                @pl.when(pl.program_id(2) == place_ref[1])
                def _():
                    own_ref[...] = total

    def spec(a, shape, index_of):
        return pl.BlockSpec(shape, _one_at_a_time(
            a, index_of, lambda r, j, pr: index_of(0, 0, pr), lambda r, j, pr: index_of(per_half - 1, N_CHIPS - 1, pr)))

    in_specs, out_specs = [], []
    for a in range(k):
        in_specs += [spec(a, (None, tr, cols), lambda r, j, pr: (j, pr[0] * per_half + r, 0)),
                     spec(a, (None, tr, cols), lambda r, j, pr: (j, r, 0))]
        out_specs += [spec(a, (tr, cols), lambda r, j, pr: (r, 0)),
                      spec(a, (None, tr, cols), lambda r, j, pr: (j, r, 0))]
    res = pl.pallas_call(
        body, name=name,
        grid_spec=pltpu.PrefetchScalarGridSpec(
            num_scalar_prefetch=1, grid=(k, per_half, N_CHIPS), in_specs=in_specs, out_specs=out_specs),
        out_shape=[jax.ShapeDtypeStruct((h, cols), F32), jax.ShapeDtypeStruct((N_CHIPS, h, cols), BF16)] * k,
        compiler_params=_params(("arbitrary", "arbitrary", "arbitrary")),
    )(place, *[t for p in pairs for t in p])
    return [tuple(res[2 * a:2 * a + 2]) for a in range(k)]


class _ChipExchange:
    SLOTS = 6
    ROW_ALIGN = 16
    MIDDLE_AT = (3, 8)

    @staticmethod
    def out_shapes(partials):
        return ([jax.ShapeDtypeStruct((3,) + p.shape[1:], p.dtype) for p in partials]
                + [jax.ShapeDtypeStruct(p.shape[1:], p.dtype) for p in partials])

    @staticmethod
    def sems(n):
        return [pltpu.SemaphoreType.DMA((n, _ChipExchange.SLOTS)), pltpu.SemaphoreType.DMA((n, _ChipExchange.SLOTS))]

    def __init__(self, ins, outs, send_sems, recv_sems):
        n = len(ins)
        self.ins, self.outs, self.relays, self.send_sems, self.recv_sems = ins, outs[:n], outs[n:], send_sems, recv_sems

    def _copy(self, a, slot):
        x, y, c = _place()
        across = _other_chips(x, y)
        rows = self.ins[a].shape[1]
        first = (rows // 2) // self.ROW_ALIGN * self.ROW_ALIGN
        part = (pl.ds(0, first), pl.ds(first, rows - first))
        if slot < 2:
            px, py = across[slot]
            src, dst, to = self.ins[a].at[2 * px + py], self.outs[a].at[slot], across[slot]
        elif slot < 4:
            px, py = across[2]
            src, dst, to = self.ins[a].at[2 * px + py, part[slot - 2], :], self.relays[a].at[part[slot - 2], :], across[slot - 2]
        else:
            src, dst, to = self.relays[a].at[part[slot - 4], :], self.outs[a].at[2, part[slot - 4], :], across[5 - slot]
        return pltpu.make_async_remote_copy(src_ref=src, dst_ref=dst, send_sem=self.send_sems.at[a, slot],
                                            recv_sem=self.recv_sems.at[a, slot], device_id=(*to, c), device_id_type=MESH)

    def start(self):
        for slot in (2, 3, 0, 1):
            for a in range(len(self.ins)):
                self._copy(a, slot).start()

    def middle(self):
        for part in range(2):
            for a in range(len(self.ins)):
                self._copy(a, 2 + part).wait_recv()
                self._copy(a, 4 + part).start()

    def finish(self):
        for a in range(len(self.ins)):
            for slot in (0, 1, 4, 5):
                self._copy(a, slot).wait_recv()
        for a in range(len(self.ins)):
            for slot in range(self.SLOTS):
                self._copy(a, slot).wait_send()


def add_chip_blocks(name, own, got):
    h, cols = own.shape
    tr = _row_tile(h, cols)

    def body(p_ref, g0_ref, g1_ref, g2_ref, out_ref):
        out_ref[...] = ((p_ref[...] + g0_ref[...].astype(F32)) + g1_ref[...].astype(F32)) + g2_ref[...].astype(F32)

    def got_spec(slot):
        return pl.BlockSpec((None, tr, cols), lambda r: (slot, r, 0))

    return pl.pallas_call(
        body, name=name, grid=(h // tr,),
        in_specs=[_rows(tr, cols), got_spec(0), got_spec(1), got_spec(2)], out_specs=_rows(tr, cols),
        out_shape=jax.ShapeDtypeStruct((h, cols), F32),
        compiler_params=_params(("arbitrary",)),
    )(own, got, got, got)


class _SiblingSwap:
    @staticmethod
    def out_shapes(halves):
        return [jax.ShapeDtypeStruct(v.shape, v.dtype) for v in halves]

    @staticmethod
    def sems(n):
        return [pltpu.SemaphoreType.DMA((n,)), pltpu.SemaphoreType.DMA((n,))]

    def __init__(self, ins, outs, send_sems, recv_sems):
        self.ins, self.outs, self.send_sems, self.recv_sems = ins, outs, send_sems, recv_sems

    def _copies(self):
        x, y, c = _place()
        for a in range(len(self.ins)):
            yield pltpu.make_async_remote_copy(
                src_ref=self.ins[a], dst_ref=self.outs[a], send_sem=self.send_sems.at[a], recv_sem=self.recv_sems.at[a],
                device_id=(x, y, 1 - c), device_id_type=MESH)

    def start(self):
        for cp in self._copies():
            cp.start()

    def finish(self):
        for cp in self._copies():
            cp.wait()


def all_reduce_small(pack):
    rows = pack.shape[0]

    def body(pack_ref, out_ref, gathered, send_sems, recv_sems):
        x, y, c = _place()
        me = 4 * x + 2 * y + c
        gathered[me] = pack_ref[...]
        copies = []
        for rel in range(1, 8):
            fx, fy, fc = (rel >> 2) & 1, (rel >> 1) & 1, rel & 1
            peer = (x ^ fx, y ^ fy, c ^ fc)
            cp = pltpu.make_async_remote_copy(
                src_ref=pack_ref, dst_ref=gathered.at[me], send_sem=send_sems.at[rel - 1], recv_sem=recv_sems.at[rel - 1],
                device_id=peer, device_id_type=MESH)
            cp.start()
            copies.append(cp)
        for rel in range(1, 8):
            fx, fy, fc = (rel >> 2) & 1, (rel >> 1) & 1, rel & 1
            src = 4 * (x ^ fx) + 2 * (y ^ fy) + (c ^ fc)
            pltpu.make_async_remote_copy(
                src_ref=pack_ref, dst_ref=gathered.at[src], send_sem=send_sems.at[rel - 1], recv_sem=recv_sems.at[rel - 1],
                device_id=(x, y, c), device_id_type=MESH).wait_recv()
        for cp in copies:
            cp.wait_send()
        total = gathered[0]
        for dev in range(1, 8):
            total = total + gathered[dev]
        out_ref[...] = total

    return pl.pallas_call(
        body, name="all_reduce_small",
        in_specs=[pl.BlockSpec(memory_space=pltpu.VMEM)], out_specs=pl.BlockSpec(memory_space=pltpu.VMEM),
        out_shape=jax.ShapeDtypeStruct((rows, D), F32),
        scratch_shapes=[pltpu.VMEM((8, rows, D), F32), pltpu.SemaphoreType.DMA((7,)), pltpu.SemaphoreType.DMA((7,))],
    )(pack)


def _adamw_update(w, g, m, v):
    nm = ADAM_B1 * m + (1.0 - ADAM_B1) * g
    nv = ADAM_B2 * v + (1.0 - ADAM_B2) * (g * g)
    m_hat = nm / (1.0 - ADAM_B1 ** ADAM_STEP)
    v_hat = nv / (1.0 - ADAM_B2 ** ADAM_STEP)
    delta = -ADAM_LR * (m_hat / (jnp.sqrt(v_hat) + ADAM_EPS) + ADAM_WD * w)
    return delta, nm, nv


def adamw(name, w, g, m, v):
    def body(w_ref, g_ref, m_ref, v_ref, d_ref, nm_ref, nv_ref):
        d_ref[...], nm_ref[...], nv_ref[...] = _adamw_update(w_ref[...], g_ref[...], m_ref[...], v_ref[...])

    out = jax.ShapeDtypeStruct(w.shape, F32)
    return pl.pallas_call(body, name=name, out_shape=[out] * 3, compiler_params=_params())(w, g, m, v)


def _one_at_a_time(which, index_of, first, final):
    def index_map(turn, *rest):
        return tuple(jnp.where(turn == which, i, jnp.where(turn < which, f, e))
                     for i, f, e in zip(index_of(*rest), first(*rest), final(*rest)))
    return index_map


def adamw_halves(name, params, core):
    k = len(params)
    h, cols = params[0][1].shape
    tr = _row_tile(h, cols, budget=(2 << 20) if k == 1 else (3 << 18))
    per_half = h // tr

    def body(core_ref, *refs):
        for a in range(k):
            @pl.when(pl.program_id(0) == a)
            def _(a=a):
                w_ref, mine_ref, theirs_ref, m_ref, v_ref = refs[5 * a:5 * a + 5]
                g_ref, d_ref, nm_ref, nv_ref = refs[5 * k + 4 * a:5 * k + 4 * a + 4]
                g = jnp.where(pl.program_id(1) == core_ref[0], mine_ref[...], theirs_ref[...])
                g_ref[...] = g
                d_ref[...], nm_ref[...], nv_ref[...] = _adamw_update(w_ref[...], g, m_ref[...], v_ref[...])

    def spec(a, index_of):
        return pl.BlockSpec((tr, cols), _one_at_a_time(
            a, index_of, lambda hh, r, cr: index_of(0, 0, cr), lambda hh, r, cr: index_of(1, per_half - 1, cr)))

    def full(hh, r, cr):
        return (hh * per_half + r, 0)

    def mine(hh, r, cr):
        return (jnp.where(hh == cr[0], r, 0), 0)

    def theirs(hh, r, cr):
        return (jnp.where(hh == cr[0], 0, r), 0)

    in_specs, out_specs = [], []
    for a in range(k):
        in_specs += [spec(a, full), spec(a, mine), spec(a, theirs), spec(a, full), spec(a, full)]
        out_specs += [spec(a, full)] * 4
    out = jax.ShapeDtypeStruct((2 * h, cols), F32)
    res = pl.pallas_call(
        body, name=name,
        grid_spec=pltpu.PrefetchScalarGridSpec(
            num_scalar_prefetch=1, grid=(k, 2, per_half), in_specs=in_specs, out_specs=out_specs),
        out_shape=[out] * (4 * k),
        compiler_params=_params(("arbitrary", "arbitrary", "arbitrary")),
    )(core, *[t for p in params for t in p])
    return [tuple(res[4 * a:4 * a + 4]) for a in range(k)]


BIG = ("w_in", "w_conv_out", "w_pool", "w_kv", "w_xattn_out", "w_out", "w_gate", "w_up", "w_down")


def kernel(x, mem, norm_mix, w_in, conv_w, w_conv_out, w_pool, pool_scale, norm_mem, w_kv, w_xattn_out, w_out, norm_ffn, w_gate, w_up, w_down, norm_final, loss_target, m_norm_mix, m_w_in, m_conv_w, m_w_conv_out, m_w_pool, m_pool_scale, m_norm_mem, m_w_kv, m_w_xattn_out, m_w_out, m_norm_ffn, m_w_gate, m_w_up, m_w_down, m_norm_final, v_norm_mix, v_w_in, v_conv_w, v_w_conv_out, v_w_pool, v_pool_scale, v_norm_mem, v_w_kv, v_w_xattn_out, v_w_out, v_norm_ffn, v_w_gate, v_w_up, v_w_down, v_norm_final):
    t_len = x.shape[1]
    xi, yi, ci = lax.axis_index("x"), lax.axis_index("y"), lax.axis_index("c")
    chip = 2 * xi + yi
    core_arr = jnp.reshape(ci, (1,)).astype(jnp.int32)
    place_arr = jnp.stack([ci, chip]).astype(jnp.int32)

    conv_pad = jnp.concatenate([conv_w, jnp.zeros((1, 13, 256), F32)], axis=1)
    def t2(w):
        return jnp.swapaxes(w, 1, 2)

    (g_in,) = exchange_call("gather_w_in", _GatherColumns, [w_in.astype(BF16)])
    w_in_f = g_in[0]

    x2d = x[0]
    tgt = loss_target[0]
    (proj, h), (g_kv, g_conv_w, g_co, g_xo, g_out, g_pool, g_gate) = proj_fwd(
        x2d, norm_mix, w_in_f,
        carry=(_Gather, [w_kv.astype(BF16), conv_pad,
                         w_conv_out.astype(BF16), w_xattn_out.astype(BF16), w_out.astype(BF16),
                         w_pool[0].astype(BF16),
                         t2(w_gate).astype(BF16)]))
    w_kv4 = g_kv.reshape(N_CHIPS, D, D // 2)
    conv_full = jnp.transpose(g_conv_w[:, 0, 0:8, :], (1, 0, 2)).reshape(8, D)
    w_co_f, w_xo_f, w_out_f = g_co.reshape(D, D), g_xo.reshape(D, D), g_out.reshape(D, D)
    w_pool_f = jnp.transpose(g_pool, (1, 0, 2, 3)).reshape(GROUPS, GROUP_DIM, GROUP_DIM)
    memn, k, v = kv_fwd(mem[0], norm_mem, w_kv4)
    (a, pooled, ya, pp, yx, o, probs, x1), (g_up, g_down) = mixer_fwd(
        proj, x2d, conv_full, w_co_f, w_pool_f, pool_scale, k, v, w_xo_f, w_out_f,
        carry=(_Gather, [t2(w_up).astype(BF16), w_down.astype(BF16)]))
    w_gate_f, w_up_f, w_down_f = g_gate.reshape(D_FF, D), g_up.reshape(D_FF, D), g_down.reshape(D_FF, D)
    gate, up, dx2, stat_f = ffn_fwd(x1, tgt, norm_ffn, w_gate_f, w_up_f, w_down_f, norm_final.reshape(1, D))

    def by_chip(pair):
        return tuple(gw.reshape(N_CHIPS, gw.shape[0] // N_CHIPS, gw.shape[1]) for gw in pair)

    def chip_partials(names, grads, got):
        sums, at = [], 0
        while at < len(names):
            end = at + 1
            while end < len(names) and grads[end][0].shape == grads[at][0].shape:
                end += 1
            sums += add_sibling_half("add_sibling_" + "_".join(names[at:end]),
                                     [(g32, o_) for (g32, _), o_ in zip(grads[at:end], got[at:end])], place_arr)
            at = end
        return sums

    def chip_sums(names, partials, got):
        return [add_chip_blocks("add_chips_" + n, own, g2) for n, (own, _), g2 in zip(names, partials, got)]

    dx1, dgate, dup, act, h2, stat_b1 = ffn_bwd(dx2, x1, gate, up, norm_ffn, w_gate_f, w_up_f, w_down_f)
    gw_gate = by_chip(matmul_tn("grad_w_gate", dgate, h2, 512))
    gw_up, got_gate = matmul_tn("grad_w_up", dup, h2, 512, carry=(_HalfExchange, [gw_gate[1]]))
    gw_up = by_chip(gw_up)
    gw_down, got_up = matmul_tn("grad_w_down", act, dx2, 512, carry=(_HalfExchange, [gw_up[1]]))
    gw_down = by_chip(gw_down)
    got_down = exchange_call("exchange_halves_w_down", _HalfExchange, [gw_down[1]])
    names_ffn = ("w_gate", "w_up", "w_down")
    part_ffn = chip_partials(names_ffn, [gw_gate, gw_up, gw_down], list(got_gate) + list(got_up) + list(got_down))

    (dproj, merged, dya, dpp, dyx, dk, dv, stat_b2), got_ffn = mixer_bwd(
        dx1, proj, ya, pp, yx, probs, conv_full, w_co_f, w_pool_f, pool_scale, k, v, w_xo_f, w_out_f,
        carry=(_ChipExchange, [p16 for _, p16 in part_ffn]))
    gw_kv32, gw_kv16, stat_kv = kv_bwd(dk, dv, memn, mem[0], norm_mem, w_kv4)
    gw_pool = tuple(jnp.transpose(gw.reshape(GROUPS, N_CHIPS, 64, GROUP_DIM), (1, 0, 2, 3)).reshape(N_CHIPS, 256, GROUP_DIM)
                    for gw in grad_w_pool(pooled, dpp))
    gw_co, got_kv_pool = matmul_tn("grad_w_conv_out", a, dya, 1024, carry=(_HalfExchange, [gw_kv16, gw_pool[1]]))
    gw_co = by_chip(gw_co)
    gw_xo, got_co = matmul_tn("grad_w_xattn_out", o, dyx, 1024, carry=(_HalfExchange, [gw_co[1]]))
    gw_xo = by_chip(gw_xo)
    gw_out, got_xo = matmul_tn("grad_w_out", merged, dx1, 1024, carry=(_HalfExchange, [gw_xo[1]]))
    gw_out = by_chip(gw_out)
    got_out = exchange_call("exchange_halves_w_out", _HalfExchange, [gw_out[1]])
    names_mix = ("w_kv", "w_pool", "w_conv_out", "w_xattn_out", "w_out")
    part_mix = chip_partials(names_mix, [(gw_kv32, gw_kv16), gw_pool, gw_co, gw_xo, gw_out],
                             list(got_kv_pool) + list(got_co) + list(got_xo) + list(got_out))

    gw_in, got_mix = matmul_tn("grad_w_in", h, dproj, 2048, col_blocks=N_CHIPS,
                               carry=(_ChipExchange, [p16 for _, p16 in part_mix]))
    part_in = chip_partials(("w_in",), [gw_in], exchange_call("exchange_halves_w_in", _HalfExchange, [gw_in[1]]))
    mine_early = chip_sums(names_ffn + names_mix, part_ffn + part_mix, list(got_ffn) + list(got_mix))
    (grad_x, stat_b3), (got_in, theirs_early) = in_bwd(
        dproj, w_in_f, x2d, dx1, norm_mix,
        carry=[(_ChipExchange, [p16 for _, p16 in part_in]), (_SiblingSwap, mine_early)])
    mine_in = chip_sums(("w_in",), part_in, got_in)
    theirs_in = exchange_call("swap_halves_w_in", _SiblingSwap, mine_in)
    reduced = dict(zip(names_ffn + names_mix + ("w_in",),
                       zip(mine_early + mine_in, list(theirs_early) + list(theirs_in))))
    mine = [reduced[n][0] for n in BIG]
    theirs = [reduced[n][1] for n in BIG]

    pack = jnp.concatenate([stat_b3[0:1], stat_b2[1:2], stat_kv[0:1], stat_b1[0:1], stat_f[0:1], stat_b2[5:8],
                            stat_f[1:2], jnp.zeros((7, D), F32)], axis=0)
    total = all_reduce_small(pack)
    loss = jnp.sum(total[8])
    g_conv_full = total[5:8]
    g_conv = lax.dynamic_slice_in_dim(g_conv_full, chip * 256, 256, axis=1)

    given = dict(w_in=(w_in, m_w_in, v_w_in), w_conv_out=(w_conv_out, m_w_conv_out, v_w_conv_out),
                 w_pool=(w_pool, m_w_pool, v_w_pool), w_kv=(w_kv, m_w_kv, v_w_kv),
                 w_xattn_out=(w_xattn_out, m_w_xattn_out, v_w_xattn_out), w_out=(w_out, m_w_out, v_w_out),
                 w_gate=(w_gate, m_w_gate, v_w_gate), w_up=(w_up, m_w_up, v_w_up), w_down=(w_down, m_w_down, v_w_down))
    out_g, out_d, out_m, out_v = {}, {}, {}, {}
    halves = dict(zip(BIG, zip(mine, theirs)))
    for group in (("w_in",), ("w_kv",), ("w_pool",), ("w_conv_out", "w_xattn_out", "w_out"), ("w_gate", "w_up", "w_down")):
        params = []
        for n in group:
            mine_n, theirs_n = halves[n]
            rows2d = (2 * mine_n.shape[0], mine_n.shape[1])
            w_, m_, v_ = ((t2(t) if n in ("w_gate", "w_up") else t).reshape(rows2d) for t in given[n])
            params.append((w_, mine_n, theirs_n, m_, v_))
        for n, res in zip(group, adamw_halves("adamw_" + "_".join(group), params, core_arr)):
            if n in ("w_gate", "w_up"):
                res = [t2(t.reshape(1, D_FF // N_CHIPS, D)) for t in res]
            out_g[n], out_d[n], out_m[n], out_v[n] = (t.reshape(given[n][0].shape) for t in res)

    def small_pack(vals, conv_part):
        conv_rows = jnp.concatenate([conv_part.reshape(3, 256), jnp.zeros((3, D - 256), F32)], axis=1)
        return jnp.concatenate([val.reshape(1, D) for val in vals] + [conv_rows], axis=0)

    sw = small_pack([norm_mix, pool_scale, norm_mem, norm_ffn, norm_final], conv_w)
    sm = small_pack([m_norm_mix, m_pool_scale, m_norm_mem, m_norm_ffn, m_norm_final], m_conv_w)
    sv = small_pack([v_norm_mix, v_pool_scale, v_norm_mem, v_norm_ffn, v_norm_final], v_conv_w)
    sg = small_pack([total[r] for r in range(5)], g_conv)
    sd, snm, snv = adamw("adamw_small", sw, sg, sm, sv)
    small_names = ("norm_mix", "pool_scale", "norm_mem", "norm_ffn", "norm_final")
    small_shapes = dict(norm_mix=(1, D), pool_scale=(1, D), norm_mem=(1, D), norm_ffn=(1, D), norm_final=(D,))
    for r, n in enumerate(small_names):
        out_g[n], out_d[n], out_m[n], out_v[n] = (t[r].reshape(small_shapes[n]) for t in (sg, sd, snm, snv))
    out_g["conv_w"], out_d["conv_w"], out_m["conv_w"], out_v["conv_w"] = (
        t[5:8, 0:256].reshape(1, 3, 256) for t in (sg, sd, snm, snv))

    order = ("norm_mix", "w_in", "conv_w", "w_conv_out", "w_pool", "pool_scale", "norm_mem", "w_kv", "w_xattn_out",
             "w_out", "norm_ffn", "w_gate", "w_up", "w_down", "norm_final")
    return (loss, grad_x.reshape(1, t_len, D), *[out_g[n] for n in order], *[out_d[n] for n in order],
            *[out_m[n] for n in order], *[out_v[n] for n in order])
```

```python
import functools

import jax
import jax.numpy as jnp
from jax import lax
from jax.experimental import pallas as pl
from jax.experimental.pallas import tpu as pltpu

F32 = jnp.float32
BF16 = jnp.bfloat16
MESH = pl.DeviceIdType.MESH

D = 1024
N_MEM = 256
HEADS = 4
HEAD_DIM = 256
GROUPS = 4
GROUP_DIM = 256
POOL_WINDOWS = (2, 4, 8, 16)
D_FF = 2816
D_IN = 8192
N_CHIPS = 4
EPS = 1e-6
HALO = 16
POOL_PAD = 128
ATT_SCALE = HEAD_DIM ** -0.5

ADAM_LR = 0.001
ADAM_B1 = 0.9
ADAM_B2 = 0.999
ADAM_EPS = 1e-08
ADAM_WD = 0.01
ADAM_STEP = 10

VMEM_LIMIT = 56 * 1024 * 1024

O_BA, O_CA, O_UA, O_UP, O_QX, O_GA, O_GP, O_GX = (k * D for k in range(8))

NT_DIMS = (((1,), (1,)), ((), ()))
TN_DIMS = (((0,), (0,)), ((), ()))


def _dot(a, b):
    return jnp.dot(a, b, preferred_element_type=F32)


def _dot_nt(a, b):
    return lax.dot_general(a, b, NT_DIMS, preferred_element_type=F32)


def _dot_tn(a, b):
    return lax.dot_general(a, b, TN_DIMS, preferred_element_type=F32)


def _sigmoid(z):
    return pl.reciprocal(1.0 + jnp.exp(-z), approx=True)


def _params(semantics=None):
    return pltpu.CompilerParams(dimension_semantics=semantics, vmem_limit_bytes=VMEM_LIMIT)


def _resident(shape):
    zeros = (0,) * len(shape)
    return pl.BlockSpec(shape, lambda *_: zeros, pipeline_mode=pl.Buffered(1))


def _const(shape):
    zeros = (0,) * len(shape)
    return pl.BlockSpec(shape, lambda *_: zeros)


def _rows(tm, width):
    return pl.BlockSpec((tm, width), lambda i: (i, 0))


def _inv_count(tile, tm, window):
    t = tile * tm + lax.broadcasted_iota(jnp.int32, (tm, 1), 0)
    return 1.0 / jnp.minimum(t + 1, window).astype(F32)


def _carried_call(body, name, grid, in_specs, out_specs, out_shape, scratch_shapes, semantics, args, carry):
    if carry is None:
        res = pl.pallas_call(body, name=name, grid=grid, in_specs=in_specs, out_specs=out_specs, out_shape=out_shape,
                             scratch_shapes=scratch_shapes, compiler_params=_params(semantics))(*args)
        return res, []
    carries = [carry] if isinstance(carry, tuple) else list(carry)
    comm_args = [arr for _, arrs in carries for arr in arrs]
    n, n_in, n_out, n_scratch = len(comm_args), len(in_specs), len(out_specs), len(scratch_shapes)
    shapes_of = [cls.out_shapes(arrs) for cls, arrs in carries]
    comm_shapes = [s for shapes in shapes_of for s in shapes]
    m = len(comm_shapes)
    comm_sems = [s for cls, arrs in carries for s in cls.sems(len(arrs))]

    def carrying(*refs):
        ins, comm_ins = refs[:n_in], refs[n_in:n_in + n]
        outs, comm_outs = refs[n_in + n:n_in + n + n_out], refs[n_in + n + n_out:n_in + n + n_out + m]
        scratch, sems = refs[n_in + n + n_out + m:n_in + n + n_out + m + n_scratch], refs[n_in + n + n_out + m + n_scratch:]
        steps = [pl.program_id(d) for d in range(len(grid))]
        first = functools.reduce(jnp.logical_and, [s == 0 for s in steps])
        last = functools.reduce(jnp.logical_and, [s == g - 1 for s, g in zip(steps, grid)])

        def exchanges():
            at_in = at_out = 0
            for k, (cls, arrs) in enumerate(carries):
                yield cls(comm_ins[at_in:at_in + len(arrs)], comm_outs[at_out:at_out + len(shapes_of[k])],
                          sems[2 * k], sems[2 * k + 1])
                at_in, at_out = at_in + len(arrs), at_out + len(shapes_of[k])

        @pl.when(first)
        def _():
            for exchange in exchanges():
                exchange.start()

        linear, total = 0, 1
        for s, g in zip(steps, grid):
            linear, total = linear * g + s, total * g
        for k, (cls, _) in enumerate(carries):
            for phase, (num, den) in getattr(cls, "PHASES_AT", {}).items():
                @pl.when(linear == (num * total) // den)
                def _(k=k, phase=phase):
                    getattr(list(exchanges())[k], phase)()

        body(*ins, *outs, *scratch)

        @pl.when(last)
        def _():
            for exchange in exchanges():
                exchange.finish()

    res = pl.pallas_call(
        carrying, name=name, grid=grid, in_specs=list(in_specs) + _any_specs(n), out_specs=list(out_specs) + _any_specs(m),
        out_shape=list(out_shape) + comm_shapes, scratch_shapes=list(scratch_shapes) + comm_sems,
        compiler_params=_params(semantics))(*args, *comm_args)
    comm_res, at = [], n_out
    for (_, arrs), shapes in zip(carries, shapes_of):
        comm_res.append(res[at:at + len(arrs)])
        at += len(shapes)
    return res[:n_out], (comm_res[0] if isinstance(carry, tuple) else comm_res)


def proj_fwd(x, g_mix, w_in, carry=None):
    t_len = x.shape[0]
    tm = min(1024, t_len)
    tn = D_IN // N_CHIPS

    def body(x_ref, g_ref, w_ref, proj_ref, h_ref):
        @pl.when(pl.program_id(1) == 0)
        def _():
            xv = x_ref[...]
            r = lax.rsqrt(jnp.mean(xv * xv, axis=-1, keepdims=True) + EPS)
            h_ref[...] = (xv * r * g_ref[...]).astype(BF16)

        proj_ref[...] = _dot(h_ref[...], w_ref[...]).astype(BF16)

    return _carried_call(
        body, "proj_fwd", (t_len // tm, N_CHIPS),
        in_specs=[pl.BlockSpec((tm, D), lambda i, j: (i, 0)),
                  pl.BlockSpec((1, D), lambda i, j: (0, 0)),
                  pl.BlockSpec((D, tn), lambda i, j: (0, j))],
        out_specs=[pl.BlockSpec((tm, tn), lambda i, j: (i, j)),
                   pl.BlockSpec((tm, D), lambda i, j: (i, 0))],
        out_shape=[jax.ShapeDtypeStruct((t_len, D_IN), BF16), jax.ShapeDtypeStruct((t_len, D), BF16)],
        scratch_shapes=[], semantics=("arbitrary", "arbitrary"), args=(x, g_mix, w_in), carry=carry)


def kv_fwd(mem, g_mem, w_kv4):
    half = D // 2

    def body(mem_ref, g_ref, w_ref, memn_ref, k_ref, v_ref):
        mv = mem_ref[...]
        r = lax.rsqrt(jnp.mean(mv * mv, axis=-1, keepdims=True) + EPS)
        mn = (mv * r * g_ref[...]).astype(BF16)
        memn_ref[...] = mn
        k_ref[:, 0:half] = _dot(mn, w_ref[0]).astype(BF16)
        k_ref[:, half:D] = _dot(mn, w_ref[1]).astype(BF16)
        v_ref[:, 0:half] = _dot(mn, w_ref[2]).astype(BF16)
        v_ref[:, half:D] = _dot(mn, w_ref[3]).astype(BF16)

    out = jax.ShapeDtypeStruct((N_MEM, D), BF16)
    return pl.pallas_call(body, name="kv_fwd", out_shape=[out, out, out], compiler_params=_params())(mem, g_mem, w_kv4)


def _softmax_rows(s):
    m = jnp.max(s, axis=-1, keepdims=True)
    e = jnp.exp(s - m)
    return e * pl.reciprocal(jnp.sum(e, axis=-1, keepdims=True), approx=True)


def _window_bands(tm, causal):
    t = lax.broadcasted_iota(jnp.int32, (tm, tm + POOL_PAD), 0)
    s = lax.broadcasted_iota(jnp.int32, (tm, tm + POOL_PAD), 1)
    d = (t + POOL_PAD - s) if causal else (s - t)
    return jnp.stack([((d >= 0) & (d < w)).astype(BF16) for w in POOL_WINDOWS])


def mixer_fwd(proj, x, conv_w8, w_co, w_pool, pool_scale, k, v, w_xo, w_out, carry=None):
    t_len = x.shape[0]
    tm = min(256, t_len)

    def body(proj_ref, x_ref, cw_ref, wco_ref, wpool_ref, ps_ref, k_ref, v_ref, wxo_ref, wout_ref,
             a_ref, pooled_ref, ya_ref, pp_ref, yx_ref, o_ref, p_ref, x1_ref, cu_ext, up_ext):
        i = pl.program_id(0)

        @pl.when(i == 0)
        def _():
            cu_ext[0:HALO, :] = jnp.zeros((HALO, D), F32)
            up_ext[0:HALO, :] = jnp.zeros((HALO, D), F32)

        cu = proj_ref[:, O_CA:O_CA + D].astype(F32) * proj_ref[:, O_UA:O_UA + D].astype(F32)
        cu_ext[HALO:HALO + tm, :] = cu
        conv = (cw_ref[2:3, :] * cu + cw_ref[1:2, :] * cu_ext[HALO - 1:HALO - 1 + tm, :]
                + cw_ref[0:1, :] * cu_ext[HALO - 2:HALO - 2 + tm, :])
        a = (proj_ref[:, O_BA:O_BA + D].astype(F32) * conv).astype(BF16)
        a_ref[...] = a
        ya = _dot(a, wco_ref[...])
        ya_ref[...] = ya.astype(BF16)

        up_ext[HALO:HALO + tm, :] = proj_ref[:, O_UP:O_UP + D].astype(F32)
        for g, window in enumerate(POOL_WINDOWS):
            cols = slice(g * GROUP_DIM, (g + 1) * GROUP_DIM)
            tok = up_ext[HALO:HALO + tm, cols]
            acc = tok
            for j in range(1, window):
                acc = acc + up_ext[HALO - j:HALO - j + tm, cols]
            pooled = (acc * _inv_count(i, tm, window) - tok).astype(BF16)
            pooled_ref[:, cols] = pooled
            pp_ref[:, cols] = _dot(pooled, wpool_ref[g]).astype(BF16)

        for hd in range(HEADS):
            cols = slice(hd * HEAD_DIM, (hd + 1) * HEAD_DIM)
            q = proj_ref[:, O_QX + hd * HEAD_DIM:O_QX + (hd + 1) * HEAD_DIM]
            p = _softmax_rows(_dot_nt(q, k_ref[:, cols]) * ATT_SCALE).astype(BF16)
            p_ref[:, hd * N_MEM:(hd + 1) * N_MEM] = p
            o_ref[:, cols] = _dot(p, v_ref[:, cols]).astype(BF16)
        yx = _dot(o_ref[...], wxo_ref[...])
        yx_ref[...] = yx.astype(BF16)

        merged = (_sigmoid(proj_ref[:, O_GA:O_GA + D].astype(F32)) * ya
                  + _sigmoid(proj_ref[:, O_GP:O_GP + D].astype(F32)) * (pp_ref[...].astype(F32) * ps_ref[...])
                  + _sigmoid(proj_ref[:, O_GX:O_GX + D].astype(F32)) * yx)
        x1_ref[...] = x_ref[...] + _dot(merged.astype(BF16), wout_ref[...])

        cu_ext[0:HALO, :] = cu_ext[tm:tm + HALO, :]
        up_ext[0:HALO, :] = up_ext[tm:tm + HALO, :]

    act = jax.ShapeDtypeStruct((t_len, D), BF16)
    return _carried_call(
        body, "mixer_fwd", (t_len // tm,),
        in_specs=[_rows(tm, D_IN), _rows(tm, D), _resident((8, D)), _resident((D, D)),
                  _resident((GROUPS, GROUP_DIM, GROUP_DIM)), _resident((1, D)),
                  _resident((N_MEM, D)), _resident((N_MEM, D)), _resident((D, D)), _resident((D, D))],
        out_specs=[_rows(tm, D)] * 7 + [_rows(tm, D)],
        out_shape=[act] * 6 + [jax.ShapeDtypeStruct((t_len, HEADS * N_MEM), BF16), jax.ShapeDtypeStruct((t_len, D), F32)],
        scratch_shapes=[pltpu.VMEM((tm + HALO, D), F32), pltpu.VMEM((tm + HALO, D), F32)],
        semantics=("arbitrary",), args=(proj, x, conv_w8, w_co, w_pool, pool_scale, k, v, w_xo, w_out), carry=carry)


def ffn_fwd(x1, target, g_ffn, w_gate, w_up, w_down, g_final):
    t_len = x1.shape[0]
    tm = min(512, t_len)

    def body(x1_ref, tgt_ref, g_ref, wg_ref, wu_ref, wd_ref, gf_ref, gate_ref, up_ref, dx2_ref, stat_ref):
        @pl.when(pl.program_id(0) == 0)
        def _():
            stat_ref[...] = jnp.zeros((8, D), F32)

        x1v = x1_ref[...]
        r2 = lax.rsqrt(jnp.mean(x1v * x1v, axis=-1, keepdims=True) + EPS)
        h2 = (x1v * r2 * g_ref[...]).astype(BF16)
        gate = _dot_nt(h2, wg_ref[...])
        up = _dot_nt(h2, wu_ref[...])
        gate_ref[...] = gate.astype(BF16)
        up_ref[...] = up.astype(BF16)
        act = (gate * _sigmoid(gate) * up).astype(BF16)
        x2 = x1v + _dot(act, wd_ref[...])
        r3 = lax.rsqrt(jnp.mean(x2 * x2, axis=-1, keepdims=True) + EPS)
        xh = x2 * r3
        diff = xh * gf_ref[...] - tgt_ref[...]
        dy = diff * (1.0 / D)
        stat_ref[0:1, :] += jnp.sum(dy * xh, axis=0, keepdims=True)
        stat_ref[1:2, :] += (0.5 / D) * jnp.sum(diff * diff, axis=0, keepdims=True)
        dxh = dy * gf_ref[...]
        dx2_ref[...] = r3 * (dxh - xh * jnp.mean(dxh * xh, axis=-1, keepdims=True))

    return pl.pallas_call(
        body, name="ffn_fwd",
        grid=(t_len // tm,),
        in_specs=[_rows(tm, D), _rows(tm, D), _resident((1, D)), _resident((D_FF, D)), _resident((D_FF, D)),
                  _resident((D_FF, D)), _resident((1, D))],
        out_specs=[_rows(tm, D_FF), _rows(tm, D_FF), _rows(tm, D), _const((8, D))],
        out_shape=[jax.ShapeDtypeStruct((t_len, D_FF), BF16), jax.ShapeDtypeStruct((t_len, D_FF), BF16),
                   jax.ShapeDtypeStruct((t_len, D), F32), jax.ShapeDtypeStruct((8, D), F32)],
        compiler_params=_params(("arbitrary",)),
    )(x1, target, g_ffn, w_gate, w_up, w_down, g_final)


def ffn_bwd(dx2, x1, gate, up, g_ffn, w_gate, w_up, w_down):
    t_len = x1.shape[0]
    tm = min(256, t_len)

    def body(dx2_ref, x1_ref, gate_ref, up_ref, g_ref, wg_ref, wu_ref, wd_ref,
             dx1_ref, dgate_ref, dup_ref, act_ref, h2_ref, stat_ref):
        @pl.when(pl.program_id(0) == 0)
        def _():
            stat_ref[...] = jnp.zeros((8, D), F32)

        dx2v = dx2_ref[...]
        gate = gate_ref[...]
        upv = up_ref[...]
        sg = _sigmoid(gate.astype(F32)).astype(BF16)
        silu = gate * sg
        act_ref[...] = silu * upv
        dact = _dot_nt(dx2v.astype(BF16), wd_ref[...]).astype(BF16)
        dup = dact * silu
        dgate = dact * upv * (sg * (1.0 + gate * (1.0 - sg)))
        dup_ref[...] = dup
        dgate_ref[...] = dgate
        dh2 = _dot(dgate, wg_ref[...]) + _dot(dup, wu_ref[...])
        x1v = x1_ref[...]
        r2 = lax.rsqrt(jnp.mean(x1v * x1v, axis=-1, keepdims=True) + EPS)
        xh = x1v * r2
        h2_ref[...] = (xh * g_ref[...]).astype(BF16)
        stat_ref[0:1, :] += jnp.sum(dh2 * xh, axis=0, keepdims=True)
        dxh = dh2 * g_ref[...]
        dx1_ref[...] = dx2v + r2 * (dxh - xh * jnp.mean(dxh * xh, axis=-1, keepdims=True))

    ff = jax.ShapeDtypeStruct((t_len, D_FF), BF16)
    return pl.pallas_call(
        body, name="ffn_bwd",
        grid=(t_len // tm,),
        in_specs=[_rows(tm, D), _rows(tm, D), _rows(tm, D_FF), _rows(tm, D_FF), _resident((1, D)),
                  _resident((D_FF, D)), _resident((D_FF, D)), _resident((D_FF, D))],
        out_specs=[_rows(tm, D), _rows(tm, D_FF), _rows(tm, D_FF), _rows(tm, D_FF), _rows(tm, D), _const((8, D))],
        out_shape=[jax.ShapeDtypeStruct((t_len, D), F32), ff, ff, ff, jax.ShapeDtypeStruct((t_len, D), BF16),
                   jax.ShapeDtypeStruct((8, D), F32)],
        compiler_params=_params(("arbitrary",)),
    )(dx2, x1, gate, up, g_ffn, w_gate, w_up, w_down)


def mixer_bwd(dx1, proj, ya, pp, yx, probs, conv_w8, w_co, w_pool, pool_scale, k, v, w_xo, w_out, carry=None):
    t_len = dx1.shape[0]
    tm = min(256, t_len)
    n_tiles = t_len // tm
    halo_blocks = tm // HALO

    def body(dx1_ref, proj_ref, halo_ref, ya_ref, pp_ref, yx_ref, p_ref,
             cw_ref, wco_ref, wpool_ref, ps_ref, k_ref, v_ref, wxo_ref, wout_ref, band_ref,
             dproj_ref, merged_ref, dya_ref, dpp_ref, dyx_ref, dk_ref, dv_ref, stat_ref,
             cu_ext, dconv_ext, dpn_ext):
        step = pl.program_id(0)
        tile = n_tiles - 1 - step

        @pl.when(step == 0)
        def _():
            dk_ref[...] = jnp.zeros((N_MEM, D), F32)
            dv_ref[...] = jnp.zeros((N_MEM, D), F32)
            stat_ref[...] = jnp.zeros((8, D), F32)
            dconv_ext[tm:tm + HALO, :] = jnp.zeros((HALO, D), F32)
            dpn_ext[tm:tm + POOL_PAD, :] = jnp.zeros((POOL_PAD, D), BF16)

        dmerged = _dot_nt(dx1_ref[...].astype(BF16), wout_ref[...]).astype(BF16)
        sa = _sigmoid(proj_ref[:, O_GA:O_GA + D].astype(F32)).astype(BF16)
        sp = _sigmoid(proj_ref[:, O_GP:O_GP + D].astype(F32)).astype(BF16)
        sx = _sigmoid(proj_ref[:, O_GX:O_GX + D].astype(F32)).astype(BF16)
        ya = ya_ref[...]
        ppv = pp_ref[...]
        scale = ps_ref[...].astype(BF16)
        yp = ppv * scale
        yx = yx_ref[...]
        merged_ref[...] = sa * ya + sp * yp + sx * yx
        dproj_ref[:, O_GA:O_GA + D] = dmerged * ya * (sa * (1.0 - sa))
        dproj_ref[:, O_GP:O_GP + D] = dmerged * yp * (sp * (1.0 - sp))
        dproj_ref[:, O_GX:O_GX + D] = dmerged * yx * (sx * (1.0 - sx))
        dya = dmerged * sa
        dyp = dmerged * sp
        dyx = dmerged * sx
        dya_ref[...] = dya
        dyx_ref[...] = dyx
        stat_ref[1:2, :] += jnp.sum(dyp.astype(F32) * ppv.astype(F32), axis=0, keepdims=True)
        dpp = dyp * scale
        dpp_ref[...] = dpp

        da = _dot_nt(dya, wco_ref[...])
        c_a = proj_ref[:, O_CA:O_CA + D].astype(F32)
        u_a = proj_ref[:, O_UA:O_UA + D].astype(F32)
        cu = c_a * u_a
        halo_cu = halo_ref[:, O_CA:O_CA + D].astype(F32) * halo_ref[:, O_UA:O_UA + D].astype(F32)
        cu_ext[0:HALO, :] = jnp.where(tile > 0, halo_cu, 0.0)
        cu_ext[HALO:HALO + tm, :] = cu
        cu1 = cu_ext[HALO - 1:HALO - 1 + tm, :]
        cu2 = cu_ext[HALO - 2:HALO - 2 + tm, :]
        conv = cw_ref[2:3, :] * cu + cw_ref[1:2, :] * cu1 + cw_ref[0:1, :] * cu2
        dproj_ref[:, O_BA:O_BA + D] = (da * conv).astype(BF16)
        dconv = da * proj_ref[:, O_BA:O_BA + D].astype(F32)
        stat_ref[5:6, :] += jnp.sum(dconv * cu2, axis=0, keepdims=True)
        stat_ref[6:7, :] += jnp.sum(dconv * cu1, axis=0, keepdims=True)
        stat_ref[7:8, :] += jnp.sum(dconv * cu, axis=0, keepdims=True)
        dconv_ext[0:tm, :] = dconv
        dcu = (cw_ref[2:3, :] * dconv + cw_ref[1:2, :] * dconv_ext[1:1 + tm, :]
               + cw_ref[0:1, :] * dconv_ext[2:2 + tm, :])
        dproj_ref[:, O_CA:O_CA + D] = (dcu * u_a).astype(BF16)
        dproj_ref[:, O_UA:O_UA + D] = (dcu * c_a).astype(BF16)

        for g, window in enumerate(POOL_WINDOWS):
            cols = slice(g * GROUP_DIM, (g + 1) * GROUP_DIM)
            dpooled = _dot_nt(dpp[:, cols], wpool_ref[g])
            dpn_ext[0:tm, cols] = (dpooled * _inv_count(tile, tm, window)).astype(BF16)
            acc = _dot(band_ref[g], dpn_ext[:, cols])
            dproj_ref[:, O_UP + g * GROUP_DIM:O_UP + (g + 1) * GROUP_DIM] = (acc - dpooled).astype(BF16)

        do = _dot_nt(dyx, wxo_ref[...])
        for hd in range(HEADS):
            cols = slice(hd * HEAD_DIM, (hd + 1) * HEAD_DIM)
            q = proj_ref[:, O_QX + hd * HEAD_DIM:O_QX + (hd + 1) * HEAD_DIM]
            kh = k_ref[:, cols]
            p16 = p_ref[:, hd * N_MEM:(hd + 1) * N_MEM]
            p = p16.astype(F32)
            doh = do[:, cols].astype(BF16)
            dp = _dot_nt(doh, v_ref[:, cols])
            dv_ref[:, cols] += _dot_tn(p16, doh)
            ds = (p * (dp - jnp.sum(dp * p, axis=-1, keepdims=True)) * ATT_SCALE).astype(BF16)
            dproj_ref[:, O_QX + hd * HEAD_DIM:O_QX + (hd + 1) * HEAD_DIM] = _dot(ds, kh).astype(BF16)
            dk_ref[:, cols] += _dot_tn(ds, q)

        dconv_ext[tm:tm + HALO, :] = dconv_ext[0:HALO, :]
        dpn_ext[tm:tm + HALO, :] = dpn_ext[0:HALO, :]

    def rev(width):
        return pl.BlockSpec((tm, width), lambda s: (n_tiles - 1 - s, 0))

    halo_spec = pl.BlockSpec((HALO, D_IN), lambda s: (jnp.maximum((n_tiles - 1 - s) * halo_blocks - 1, 0), 0))
    act = jax.ShapeDtypeStruct((t_len, D), BF16)
    kv_grad = jax.ShapeDtypeStruct((N_MEM, D), F32)
    return _carried_call(
        body, "mixer_bwd", (n_tiles,),
        in_specs=[rev(D), rev(D_IN), halo_spec, rev(D), rev(D), rev(D), rev(D),
                  _resident((8, D)), _resident((D, D)), _resident((GROUPS, GROUP_DIM, GROUP_DIM)), _resident((1, D)),
                  _resident((N_MEM, D)), _resident((N_MEM, D)), _resident((D, D)), _resident((D, D)),
                  _resident((GROUPS, tm, tm + POOL_PAD))],
        out_specs=[rev(D_IN), rev(D), rev(D), rev(D), rev(D),
                   _const((N_MEM, D)), _const((N_MEM, D)), _const((8, D))],
        out_shape=[jax.ShapeDtypeStruct((t_len, D_IN), BF16), act, act, act, act, kv_grad, kv_grad,
                   jax.ShapeDtypeStruct((8, D), F32)],
        scratch_shapes=[pltpu.VMEM((tm + HALO, D), F32)] * 2 + [pltpu.VMEM((tm + POOL_PAD, D), BF16)],
        semantics=("arbitrary",),
        args=(dx1, proj, proj, ya, pp, yx, probs, conv_w8, w_co, w_pool, pool_scale, k, v, w_xo, w_out,
              _window_bands(tm, False)), carry=carry)


def in_bwd(dproj, w_in, x, dx1, g_mix, carry=None):
    t_len = x.shape[0]
    tm = min(512, t_len)

    def body(dproj_ref, w_ref, x_ref, dx1_ref, g_ref, gx_ref, stat_ref):
        @pl.when(pl.program_id(0) == 0)
        def _():
            stat_ref[...] = jnp.zeros((8, D), F32)

        dh = _dot_nt(dproj_ref[...], w_ref[...])
        xv = x_ref[...]
        r = lax.rsqrt(jnp.mean(xv * xv, axis=-1, keepdims=True) + EPS)
        xh = xv * r
        stat_ref[0:1, :] += jnp.sum(dh * xh, axis=0, keepdims=True)
        dxh = dh * g_ref[...]
        gx_ref[...] = dx1_ref[...] + r * (dxh - xh * jnp.mean(dxh * xh, axis=-1, keepdims=True))

    return _carried_call(
        body, "in_bwd", (t_len // tm,),
        in_specs=[_rows(tm, D_IN), _resident((D, D_IN)), _rows(tm, D), _rows(tm, D), _resident((1, D))],
        out_specs=[_rows(tm, D), _const((8, D))],
        out_shape=[jax.ShapeDtypeStruct((t_len, D), F32), jax.ShapeDtypeStruct((8, D), F32)],
        scratch_shapes=[], semantics=("arbitrary",), args=(dproj, w_in, x, dx1, g_mix), carry=carry)


def kv_bwd(dk, dv, memn, mem, g_mem, w_kv4):
    half = D // 2

    def body(dk_ref, dv_ref, memn_ref, mem_ref, g_ref, w_ref, gw_ref, gw16_ref, stat_ref):
        mn = memn_ref[...]
        parts = (dk_ref[:, 0:half], dk_ref[:, half:D], dv_ref[:, 0:half], dv_ref[:, half:D])
        dmemn = jnp.zeros((N_MEM, D), F32)
        for j, part in enumerate(parts):
            part = part.astype(BF16)
            gw = _dot_tn(mn, part)
            gw_ref[j] = gw
            gw16_ref[j] = gw.astype(BF16)
            dmemn = dmemn + _dot_nt(part, w_ref[j])
        mv = mem_ref[...]
        r = lax.rsqrt(jnp.mean(mv * mv, axis=-1, keepdims=True) + EPS)
        stat_ref[...] = jnp.zeros((8, D), F32)
        stat_ref[0:1, :] = jnp.sum(dmemn * (mv * r), axis=0, keepdims=True)

    return pl.pallas_call(
        body, name="kv_bwd",
        out_shape=[jax.ShapeDtypeStruct((N_CHIPS, D, half), F32), jax.ShapeDtypeStruct((N_CHIPS, D, half), BF16),
                   jax.ShapeDtypeStruct((8, D), F32)],
        compiler_params=_params(),
    )(dk, dv, memn, mem, g_mem, w_kv4)


def matmul_tn(name, a, b, tn, col_blocks=1, carry=None):
    t_len, k_dim = a.shape
    n_dim = b.shape[1]
    tt = min(1024, t_len)
    per_block = n_dim // col_blocks // tn

    def body(a_ref, b_ref, out_ref, out16_ref):
        @pl.when(pl.program_id(1) == 0)
        def _():
            out_ref[...] = jnp.zeros((k_dim, tn), F32)

        out_ref[...] += _dot_tn(a_ref[...].astype(BF16), b_ref[...].astype(BF16))

        @pl.when(pl.program_id(1) == t_len // tt - 1)
        def _():
            out16_ref[...] = out_ref[...].astype(BF16)

    if col_blocks == 1:
        out_spec = pl.BlockSpec((k_dim, tn), lambda n, t: (0, n))
        shape = (k_dim, n_dim)
    else:
        out_spec = pl.BlockSpec((None, k_dim, tn), lambda n, t: (n // per_block, 0, n % per_block))
        shape = (col_blocks, k_dim, n_dim // col_blocks)
    outs, carried = _carried_call(
        body, name, (n_dim // tn, t_len // tt),
        in_specs=[pl.BlockSpec((tt, k_dim), lambda n, t: (t, 0)), pl.BlockSpec((tt, tn), lambda n, t: (t, n))],
        out_specs=[out_spec, out_spec], out_shape=[jax.ShapeDtypeStruct(shape, F32), jax.ShapeDtypeStruct(shape, BF16)],
        scratch_shapes=[], semantics=("arbitrary", "arbitrary"), args=(a, b), carry=carry)
    return (tuple(outs), carried) if carry is not None else tuple(outs)


def grad_w_pool(pooled, dpp):
    t_len = pooled.shape[0]
    tt = min(1024, t_len)
    steps = t_len // tt

    def body(a_ref, b_ref, out_ref, out16_ref):
        @pl.when(pl.program_id(0) == 0)
        def _():
            out_ref[...] = jnp.zeros((GROUPS, GROUP_DIM, GROUP_DIM), F32)

        for g in range(GROUPS):
            cols = slice(g * GROUP_DIM, (g + 1) * GROUP_DIM)
            out_ref[g] += _dot_tn(a_ref[:, cols], b_ref[:, cols])

        @pl.when(pl.program_id(0) == steps - 1)
        def _():
            out16_ref[...] = out_ref[...].astype(BF16)

    shape = (GROUPS, GROUP_DIM, GROUP_DIM)
    return pl.pallas_call(
        body, name="grad_w_pool", grid=(steps,),
        in_specs=[_rows(tt, D), _rows(tt, D)], out_specs=[_const(shape), _const(shape)],
        out_shape=[jax.ShapeDtypeStruct(shape, F32), jax.ShapeDtypeStruct(shape, BF16)],
        compiler_params=_params(("arbitrary",)),
    )(pooled, dpp)


def _place():
    x, y, c = lax.axis_index("x"), lax.axis_index("y"), lax.axis_index("c")
    return x, y, c


def _other_chips(x, y):
    return [(1 - x, y), (x, 1 - y), (1 - x, 1 - y)]


def _any_specs(n):
    return [pl.BlockSpec(memory_space=pl.ANY)] * n


def exchange_call(name, exchange_cls, arrays):
    n = len(arrays)
    shapes = exchange_cls.out_shapes(arrays)

    def body(*refs):
        exchange = exchange_cls(refs[:n], refs[n:n + len(shapes)], *refs[n + len(shapes):])
        exchange.start()
        for phase in getattr(exchange_cls, "PHASES_AT", {}):
            getattr(exchange, phase)()
        exchange.finish()

    res = pl.pallas_call(
        body, name=name, in_specs=_any_specs(n), out_specs=_any_specs(len(shapes)),
        out_shape=shapes, scratch_shapes=exchange_cls.sems(n),
    )(*arrays)
    return res[:n]


class _Gather:
    SLOTS = 8
    ROW_ALIGN = 16
    PHASES_AT = {"middle": (4, 8), "late": (7, 8)}

    @staticmethod
    def out_shapes(shards):
        return [jax.ShapeDtypeStruct((N_CHIPS,) + s.shape, s.dtype) for s in shards]

    @staticmethod
    def sems(n):
        return [pltpu.SemaphoreType.DMA((n, _Gather.SLOTS)), pltpu.SemaphoreType.DMA((n, _Gather.SLOTS))]

    def __init__(self, ins, outs, send_sems, recv_sems):
        self.ins, self.outs, self.send_sems, self.recv_sems = ins, outs, send_sems, recv_sems
        x, y, c = _place()
        self.c, self.me, self.sibling = c, 2 * x + y, (x, y, 1 - c)
        self.across = [(1 - x, y), (x, 1 - y), (1 - x, 1 - y)]

    def _rows(self, a, which, part=None):
        half = self.ins[a].shape[1] // 2
        first = (half // 2) // self.ROW_ALIGN * self.ROW_ALIGN
        if part is None:
            return pl.ds(which * half, half)
        return pl.ds(which * half, first) if part == 0 else pl.ds(which * half + first, half - first)

    def _has_part(self, a, part):
        half = self.ins[a].shape[1] // 2
        return part == 1 or (half // 2) // self.ROW_ALIGN > 0

    def _block(self, a, chip, rows=slice(None)):
        index = chip if not isinstance(chip, tuple) else 2 * chip[0] + chip[1]
        if len(self.outs[a].shape) == len(self.ins[a].shape):
            cols = self.ins[a].shape[2]
            return self.outs[a].at[:, rows, pl.ds(pl.multiple_of(index * cols, cols), cols)]
        return self.outs[a].at[index, :, rows, :]

    def _remote(self, src, dst, a, slot, to):
        return pltpu.make_async_remote_copy(src_ref=src, dst_ref=dst, send_sem=self.send_sems.at[a, slot],
                                            recv_sem=self.recv_sems.at[a, slot], device_id=to, device_id_type=MESH)

    def _own(self, a):
        return self._remote(self.ins[a], self._block(a, self.me), a, 6, self.sibling)

    def _sent(self, a, axis):
        rows = self._rows(a, self.c)
        return self._remote(self.ins[a].at[:, rows, :], self._block(a, self.me, rows), a, axis,
                            (*self.across[axis], self.c))

    def _landed(self, a, axis):
        block = self._block(a, self.across[axis], self._rows(a, self.c))
        return self._remote(block, block, a, axis, (*self.across[axis], self.c))

    def _relayed(self, a, part, incoming):
        source = self.across[2] if incoming else self.across[part]
        block = self._block(a, source, self._rows(a, self.c, part))
        return self._remote(block, block, a, (2, 7)[part], (*self.across[1 - part], self.c))

    def _passed_on(self, a, source, which):
        block = self._block(a, self.across[source], self._rows(a, which))
        return self._remote(block, block, a, 3 + source, self.sibling)

    def start(self):
        for a in range(len(self.ins)):
            self._own(a).start()
        for a in range(len(self.ins)):
            for axis in range(2):
                self._sent(a, axis).start()

    def middle(self):
        for a in range(len(self.ins)):
            for axis in range(2):
                self._landed(a, axis).wait_recv()
                if self._has_part(a, axis):
                    self._relayed(a, axis, incoming=False).start()
                self._passed_on(a, axis, self.c).start()

    def late(self):
        for a in range(len(self.ins)):
            for part in range(2):
                if self._has_part(a, part):
                    self._relayed(a, part, incoming=True).wait_recv()
            self._passed_on(a, 2, self.c).start()

    def finish(self):
        n = len(self.ins)
        for a in range(n):
            for source in range(3):
                self._passed_on(a, source, 1 - self.c).wait_recv()
            self._own(a).wait_recv()
        for a in range(n):
            self._own(a).wait_send()
            for axis in range(2):
                self._sent(a, axis).wait_send()
                if self._has_part(a, axis):
                    self._relayed(a, axis, incoming=False).wait_send()
            for source in range(3):
                self._passed_on(a, source, self.c).wait_send()


class _HalfExchange:
    @staticmethod
    def out_shapes(grads):
        return [jax.ShapeDtypeStruct((N_CHIPS, g.shape[1] // 2, g.shape[2]), g.dtype) for g in grads]

    @staticmethod
    def sems(n):
        return [pltpu.SemaphoreType.DMA((n,)), pltpu.SemaphoreType.DMA((n,))]

    def __init__(self, ins, outs, send_sems, recv_sems):
        self.ins, self.outs, self.send_sems, self.recv_sems = ins, outs, send_sems, recv_sems

    def _copies(self):
        x, y, c = _place()
        for a in range(len(self.ins)):
            h = self.ins[a].shape[1] // 2
            yield pltpu.make_async_remote_copy(
                src_ref=self.ins[a].at[:, pl.ds((1 - c) * h, h), :], dst_ref=self.outs[a],
                send_sem=self.send_sems.at[a], recv_sem=self.recv_sems.at[a], device_id=(x, y, 1 - c), device_id_type=MESH)

    def start(self):
        for cp in self._copies():
            cp.start()

    def finish(self):
        for cp in self._copies():
            cp.wait()


class _GatherColumns(_Gather):
    @staticmethod
    def out_shapes(shards):
        return [jax.ShapeDtypeStruct(s.shape[:2] + (N_CHIPS * s.shape[2],), s.dtype) for s in shards]


def _row_tile(rows, cols, budget=2 << 20):
    best = 16
    for tr in range(16, rows + 1, 16):
        if rows % tr == 0 and tr * cols * 4 <= budget:
            best = tr
    return best


def add_sibling_half(name, grad, got, place):
    _, rows, cols = grad.shape
    h = rows // 2
    tr = _row_tile(h, cols)
    per_half = h // tr

    def body(place_ref, g_ref, o_ref, own_ref, out16_ref):
        total = g_ref[...] + o_ref[...].astype(F32)
        out16_ref[...] = total.astype(BF16)

        @pl.when(pl.program_id(1) == place_ref[1])
        def _():
            own_ref[...] = total

    return pl.pallas_call(
        body, name=name,
        grid_spec=pltpu.PrefetchScalarGridSpec(
            num_scalar_prefetch=1, grid=(per_half, N_CHIPS),
            in_specs=[pl.BlockSpec((None, tr, cols), lambda r, j, pr: (j, pr[0] * per_half + r, 0)),
                      pl.BlockSpec((None, tr, cols), lambda r, j, pr: (j, r, 0))],
            out_specs=[pl.BlockSpec((tr, cols), lambda r, j, pr: (r, 0)),
                       pl.BlockSpec((None, tr, cols), lambda r, j, pr: (j, r, 0))]),
        out_shape=[jax.ShapeDtypeStruct((h, cols), F32), jax.ShapeDtypeStruct((N_CHIPS, h, cols), BF16)],
        compiler_params=_params(("arbitrary", "arbitrary")),
    )(place, grad, got)


class _ChipExchange:
    SLOTS = 6
    ROW_ALIGN = 16
    PHASES_AT = {"middle": (3, 8)}

    @staticmethod
    def out_shapes(partials):
        return ([jax.ShapeDtypeStruct((3,) + p.shape[1:], p.dtype) for p in partials]
                + [jax.ShapeDtypeStruct(p.shape[1:], p.dtype) for p in partials])

    @staticmethod
    def sems(n):
        return [pltpu.SemaphoreType.DMA((n, _ChipExchange.SLOTS)), pltpu.SemaphoreType.DMA((n, _ChipExchange.SLOTS))]

    def __init__(self, ins, outs, send_sems, recv_sems):
        n = len(ins)
        self.ins, self.outs, self.relays, self.send_sems, self.recv_sems = ins, outs[:n], outs[n:], send_sems, recv_sems

    def _copy(self, a, slot):
        x, y, c = _place()
        across = _other_chips(x, y)
        rows = self.ins[a].shape[1]
        first = (rows // 2) // self.ROW_ALIGN * self.ROW_ALIGN
        part = (pl.ds(0, first), pl.ds(first, rows - first))
        if slot < 2:
            px, py = across[slot]
            src, dst, to = self.ins[a].at[2 * px + py], self.outs[a].at[slot], across[slot]
        elif slot < 4:
            px, py = across[2]
            src, dst, to = self.ins[a].at[2 * px + py, part[slot - 2], :], self.relays[a].at[part[slot - 2], :], across[slot - 2]
        else:
            src, dst, to = self.relays[a].at[part[slot - 4], :], self.outs[a].at[2, part[slot - 4], :], across[5 - slot]
        return pltpu.make_async_remote_copy(src_ref=src, dst_ref=dst, send_sem=self.send_sems.at[a, slot],
                                            recv_sem=self.recv_sems.at[a, slot], device_id=(*to, c), device_id_type=MESH)

    def start(self):
        for slot in (2, 3, 0, 1):
            for a in range(len(self.ins)):
                self._copy(a, slot).start()

    def middle(self):
        for part in range(2):
            for a in range(len(self.ins)):
                self._copy(a, 2 + part).wait_recv()
                self._copy(a, 4 + part).start()

    def finish(self):
        for a in range(len(self.ins)):
            for slot in (0, 1, 4, 5):
                self._copy(a, slot).wait_recv()
        for a in range(len(self.ins)):
            for slot in range(self.SLOTS):
                self._copy(a, slot).wait_send()


def add_chip_blocks(name, own, got):
    h, cols = own.shape
    tr = _row_tile(h, cols)

    def body(p_ref, g0_ref, g1_ref, g2_ref, out_ref):
        out_ref[...] = ((p_ref[...] + g0_ref[...].astype(F32)) + g1_ref[...].astype(F32)) + g2_ref[...].astype(F32)

    def got_spec(slot):
        return pl.BlockSpec((None, tr, cols), lambda r: (slot, r, 0))

    return pl.pallas_call(
        body, name=name, grid=(h // tr,),
        in_specs=[_rows(tr, cols), got_spec(0), got_spec(1), got_spec(2)], out_specs=_rows(tr, cols),
        out_shape=jax.ShapeDtypeStruct((h, cols), F32),
        compiler_params=_params(("arbitrary",)),
    )(own, got, got, got)


class _SiblingSwap:
    @staticmethod
    def out_shapes(halves):
        return [jax.ShapeDtypeStruct(v.shape, v.dtype) for v in halves]

    @staticmethod
    def sems(n):
        return [pltpu.SemaphoreType.DMA((n,)), pltpu.SemaphoreType.DMA((n,))]

    def __init__(self, ins, outs, send_sems, recv_sems):
        self.ins, self.outs, self.send_sems, self.recv_sems = ins, outs, send_sems, recv_sems

    def _copies(self):
        x, y, c = _place()
        for a in range(len(self.ins)):
            yield pltpu.make_async_remote_copy(
                src_ref=self.ins[a], dst_ref=self.outs[a], send_sem=self.send_sems.at[a], recv_sem=self.recv_sems.at[a],
                device_id=(x, y, 1 - c), device_id_type=MESH)

    def start(self):
        for cp in self._copies():
            cp.start()

    def finish(self):
        for cp in self._copies():
            cp.wait()


def all_reduce_small(pack):
    rows = pack.shape[0]

    def body(pack_ref, out_ref, gathered, send_sems, recv_sems):
        x, y, c = _place()
        me = 4 * x + 2 * y + c
        gathered[me] = pack_ref[...]
        copies = []
        for rel in range(1, 8):
            fx, fy, fc = (rel >> 2) & 1, (rel >> 1) & 1, rel & 1
            peer = (x ^ fx, y ^ fy, c ^ fc)
            cp = pltpu.make_async_remote_copy(
                src_ref=pack_ref, dst_ref=gathered.at[me], send_sem=send_sems.at[rel - 1], recv_sem=recv_sems.at[rel - 1],
                device_id=peer, device_id_type=MESH)
            cp.start()
            copies.append(cp)
        for rel in range(1, 8):
            fx, fy, fc = (rel >> 2) & 1, (rel >> 1) & 1, rel & 1
            src = 4 * (x ^ fx) + 2 * (y ^ fy) + (c ^ fc)
            pltpu.make_async_remote_copy(
                src_ref=pack_ref, dst_ref=gathered.at[src], send_sem=send_sems.at[rel - 1], recv_sem=recv_sems.at[rel - 1],
                device_id=(x, y, c), device_id_type=MESH).wait_recv()
        for cp in copies:
            cp.wait_send()
        total = gathered[0]
        for dev in range(1, 8):
            total = total + gathered[dev]
        out_ref[...] = total

    return pl.pallas_call(
        body, name="all_reduce_small",
        in_specs=[pl.BlockSpec(memory_space=pltpu.VMEM)], out_specs=pl.BlockSpec(memory_space=pltpu.VMEM),
        out_shape=jax.ShapeDtypeStruct((rows, D), F32),
        scratch_shapes=[pltpu.VMEM((8, rows, D), F32), pltpu.SemaphoreType.DMA((7,)), pltpu.SemaphoreType.DMA((7,))],
    )(pack)


def _adamw_update(w, g, m, v):
    nm = ADAM_B1 * m + (1.0 - ADAM_B1) * g
    nv = ADAM_B2 * v + (1.0 - ADAM_B2) * (g * g)
    m_hat = nm / (1.0 - ADAM_B1 ** ADAM_STEP)
    v_hat = nv / (1.0 - ADAM_B2 ** ADAM_STEP)
    delta = -ADAM_LR * (m_hat / (jnp.sqrt(v_hat) + ADAM_EPS) + ADAM_WD * w)
    return delta, nm, nv


def adamw(name, w, g, m, v):
    def body(w_ref, g_ref, m_ref, v_ref, d_ref, nm_ref, nv_ref):
        d_ref[...], nm_ref[...], nv_ref[...] = _adamw_update(w_ref[...], g_ref[...], m_ref[...], v_ref[...])

    out = jax.ShapeDtypeStruct(w.shape, F32)
    return pl.pallas_call(body, name=name, out_shape=[out] * 3, compiler_params=_params())(w, g, m, v)


def adamw_halves(name, w, mine, theirs, m, v, core):
    h, cols = mine.shape
    tr = _row_tile(h, cols)
    per_half = h // tr

    def body(core_ref, w_ref, mine_ref, theirs_ref, m_ref, v_ref, g_ref, d_ref, nm_ref, nv_ref):
        g = jnp.where(pl.program_id(0) == core_ref[0], mine_ref[...], theirs_ref[...])
        g_ref[...] = g
        d_ref[...], nm_ref[...], nv_ref[...] = _adamw_update(w_ref[...], g, m_ref[...], v_ref[...])

    full = pl.BlockSpec((tr, cols), lambda hh, r, cr: (hh * per_half + r, 0))
    mine_spec = pl.BlockSpec((tr, cols), lambda hh, r, cr: (jnp.where(hh == cr[0], r, 0), 0))
    theirs_spec = pl.BlockSpec((tr, cols), lambda hh, r, cr: (jnp.where(hh == cr[0], 0, r), 0))
    out = jax.ShapeDtypeStruct((2 * h, cols), F32)
    return pl.pallas_call(
        body, name=name,
        grid_spec=pltpu.PrefetchScalarGridSpec(
            num_scalar_prefetch=1, grid=(2, per_half),
            in_specs=[full, mine_spec, theirs_spec, full, full], out_specs=[full] * 4),
        out_shape=[out] * 4,
        compiler_params=_params(("arbitrary", "arbitrary")),
    )(core, w, mine, theirs, m, v)


BIG = ("w_in", "w_conv_out", "w_pool", "w_kv", "w_xattn_out", "w_out", "w_gate", "w_up", "w_down")


def kernel(x, mem, norm_mix, w_in, conv_w, w_conv_out, w_pool, pool_scale, norm_mem, w_kv, w_xattn_out, w_out, norm_ffn, w_gate, w_up, w_down, norm_final, loss_target, m_norm_mix, m_w_in, m_conv_w, m_w_conv_out, m_w_pool, m_pool_scale, m_norm_mem, m_w_kv, m_w_xattn_out, m_w_out, m_norm_ffn, m_w_gate, m_w_up, m_w_down, m_norm_final, v_norm_mix, v_w_in, v_conv_w, v_w_conv_out, v_w_pool, v_pool_scale, v_norm_mem, v_w_kv, v_w_xattn_out, v_w_out, v_norm_ffn, v_w_gate, v_w_up, v_w_down, v_norm_final):
    t_len = x.shape[1]
    xi, yi, ci = lax.axis_index("x"), lax.axis_index("y"), lax.axis_index("c")
    chip = 2 * xi + yi
    core_arr = jnp.reshape(ci, (1,)).astype(jnp.int32)
    place_arr = jnp.stack([ci, chip]).astype(jnp.int32)

    conv_pad = jnp.concatenate([conv_w, jnp.zeros((1, 13, 256), F32)], axis=1)
    def t2(w):
        return jnp.swapaxes(w, 1, 2)

    (g_in,) = exchange_call("gather_w_in", _GatherColumns, [w_in.astype(BF16)])
    w_in_f = g_in[0]

    x2d = x[0]
    tgt = loss_target[0]
    (proj, h), (g_kv, g_conv_w, g_co, g_xo, g_out, g_pool, g_gate) = proj_fwd(
        x2d, norm_mix, w_in_f,
        carry=(_Gather, [w_kv.astype(BF16), conv_pad,
                         w_conv_out.astype(BF16), w_xattn_out.astype(BF16), w_out.astype(BF16),
                         w_pool[0].astype(BF16),
                         t2(w_gate).astype(BF16)]))
    w_kv4 = g_kv.reshape(N_CHIPS, D, D // 2)
    conv_full = jnp.transpose(g_conv_w[:, 0, 0:8, :], (1, 0, 2)).reshape(8, D)
    w_co_f, w_xo_f, w_out_f = g_co.reshape(D, D), g_xo.reshape(D, D), g_out.reshape(D, D)
    w_pool_f = jnp.transpose(g_pool, (1, 0, 2, 3)).reshape(GROUPS, GROUP_DIM, GROUP_DIM)
    memn, k, v = kv_fwd(mem[0], norm_mem, w_kv4)
    (a, pooled, ya, pp, yx, o, probs, x1), (g_up, g_down) = mixer_fwd(
        proj, x2d, conv_full, w_co_f, w_pool_f, pool_scale, k, v, w_xo_f, w_out_f,
        carry=(_Gather, [t2(w_up).astype(BF16), w_down.astype(BF16)]))
    w_gate_f, w_up_f, w_down_f = g_gate.reshape(D_FF, D), g_up.reshape(D_FF, D), g_down.reshape(D_FF, D)
    gate, up, dx2, stat_f = ffn_fwd(x1, tgt, norm_ffn, w_gate_f, w_up_f, w_down_f, norm_final.reshape(1, D))

    def by_chip(pair):
        return tuple(gw.reshape(N_CHIPS, gw.shape[0] // N_CHIPS, gw.shape[1]) for gw in pair)

    def chip_partials(names, grads, got):
        return [add_sibling_half("add_sibling_" + n, g32, o_, place_arr) for n, (g32, _), o_ in zip(names, grads, got)]

    def chip_sums(names, partials, got):
        return [add_chip_blocks("add_chips_" + n, own, g2) for n, (own, _), g2 in zip(names, partials, got)]

    dx1, dgate, dup, act, h2, stat_b1 = ffn_bwd(dx2, x1, gate, up, norm_ffn, w_gate_f, w_up_f, w_down_f)
    gw_gate = by_chip(matmul_tn("grad_w_gate", dgate, h2, 512))
    gw_up, got_gate = matmul_tn("grad_w_up", dup, h2, 512, carry=(_HalfExchange, [gw_gate[1]]))
    gw_up = by_chip(gw_up)
    gw_down, got_up = matmul_tn("grad_w_down", act, dx2, 512, carry=(_HalfExchange, [gw_up[1]]))
    gw_down = by_chip(gw_down)
    got_down = exchange_call("exchange_halves_w_down", _HalfExchange, [gw_down[1]])
    names_ffn = ("w_gate", "w_up", "w_down")
    part_ffn = chip_partials(names_ffn, [gw_gate, gw_up, gw_down], list(got_gate) + list(got_up) + list(got_down))

    (dproj, merged, dya, dpp, dyx, dk, dv, stat_b2), got_ffn = mixer_bwd(
        dx1, proj, ya, pp, yx, probs, conv_full, w_co_f, w_pool_f, pool_scale, k, v, w_xo_f, w_out_f,
        carry=(_ChipExchange, [p16 for _, p16 in part_ffn]))
    gw_kv32, gw_kv16, stat_kv = kv_bwd(dk, dv, memn, mem[0], norm_mem, w_kv4)
    gw_pool = tuple(jnp.transpose(gw.reshape(GROUPS, N_CHIPS, 64, GROUP_DIM), (1, 0, 2, 3)).reshape(N_CHIPS, 256, GROUP_DIM)
                    for gw in grad_w_pool(pooled, dpp))
    gw_co, got_kv_pool = matmul_tn("grad_w_conv_out", a, dya, 1024, carry=(_HalfExchange, [gw_kv16, gw_pool[1]]))
    gw_co = by_chip(gw_co)
    gw_xo, got_co = matmul_tn("grad_w_xattn_out", o, dyx, 1024, carry=(_HalfExchange, [gw_co[1]]))
    gw_xo = by_chip(gw_xo)
    gw_out, got_xo = matmul_tn("grad_w_out", merged, dx1, 1024, carry=(_HalfExchange, [gw_xo[1]]))
    gw_out = by_chip(gw_out)
    got_out = exchange_call("exchange_halves_w_out", _HalfExchange, [gw_out[1]])
    names_mix = ("w_kv", "w_pool", "w_conv_out", "w_xattn_out", "w_out")
    part_mix = chip_partials(names_mix, [(gw_kv32, gw_kv16), gw_pool, gw_co, gw_xo, gw_out],
                             list(got_kv_pool) + list(got_co) + list(got_xo) + list(got_out))

    gw_in, got_mix = matmul_tn("grad_w_in", h, dproj, 2048, col_blocks=N_CHIPS,
                               carry=(_ChipExchange, [p16 for _, p16 in part_mix]))
    part_in = chip_partials(("w_in",), [gw_in], exchange_call("exchange_halves_w_in", _HalfExchange, [gw_in[1]]))
    mine_early = chip_sums(names_ffn + names_mix, part_ffn + part_mix, list(got_ffn) + list(got_mix))
    (grad_x, stat_b3), (got_in, theirs_early) = in_bwd(
        dproj, w_in_f, x2d, dx1, norm_mix,
        carry=[(_ChipExchange, [p16 for _, p16 in part_in]), (_SiblingSwap, mine_early)])
    mine_in = chip_sums(("w_in",), part_in, got_in)
    theirs_in = exchange_call("swap_halves_w_in", _SiblingSwap, mine_in)
    reduced = dict(zip(names_ffn + names_mix + ("w_in",),
                       zip(mine_early + mine_in, list(theirs_early) + list(theirs_in))))
    mine = [reduced[n][0] for n in BIG]
    theirs = [reduced[n][1] for n in BIG]

    pack = jnp.concatenate([stat_b3[0:1], stat_b2[1:2], stat_kv[0:1], stat_b1[0:1], stat_f[0:1], stat_b2[5:8],
                            stat_f[1:2], jnp.zeros((7, D), F32)], axis=0)
    total = all_reduce_small(pack)
    loss = jnp.sum(total[8])
    g_conv_full = total[5:8]
    g_conv = lax.dynamic_slice_in_dim(g_conv_full, chip * 256, 256, axis=1)

    given = dict(w_in=(w_in, m_w_in, v_w_in), w_conv_out=(w_conv_out, m_w_conv_out, v_w_conv_out),
                 w_pool=(w_pool, m_w_pool, v_w_pool), w_kv=(w_kv, m_w_kv, v_w_kv),
                 w_xattn_out=(w_xattn_out, m_w_xattn_out, v_w_xattn_out), w_out=(w_out, m_w_out, v_w_out),
                 w_gate=(w_gate, m_w_gate, v_w_gate), w_up=(w_up, m_w_up, v_w_up), w_down=(w_down, m_w_down, v_w_down))
    out_g, out_d, out_m, out_v = {}, {}, {}, {}
    for n, mine_n, theirs_n in zip(BIG, mine, theirs):
        transposed = n in ("w_gate", "w_up")
        rows2d = (2 * mine_n.shape[0], mine_n.shape[1])
        w_, m_, v_ = ((t2(t) if transposed else t).reshape(rows2d) for t in given[n])
        res = adamw_halves("adamw_" + n, w_, mine_n, theirs_n, m_, v_, core_arr)
        if transposed:
            res = [t2(t.reshape(1, D_FF // N_CHIPS, D)) for t in res]
        out_g[n], out_d[n], out_m[n], out_v[n] = (t.reshape(given[n][0].shape) for t in res)

    def small_pack(vals, conv_part):
        conv_rows = jnp.concatenate([conv_part.reshape(3, 256), jnp.zeros((3, D - 256), F32)], axis=1)
        return jnp.concatenate([val.reshape(1, D) for val in vals] + [conv_rows], axis=0)

    sw = small_pack([norm_mix, pool_scale, norm_mem, norm_ffn, norm_final], conv_w)
    sm = small_pack([m_norm_mix, m_pool_scale, m_norm_mem, m_norm_ffn, m_norm_final], m_conv_w)
    sv = small_pack([v_norm_mix, v_pool_scale, v_norm_mem, v_norm_ffn, v_norm_final], v_conv_w)
    sg = small_pack([total[r] for r in range(5)], g_conv)
    sd, snm, snv = adamw("adamw_small", sw, sg, sm, sv)
    small_names = ("norm_mix", "pool_scale", "norm_mem", "norm_ffn", "norm_final")
    small_shapes = dict(norm_mix=(1, D), pool_scale=(1, D), norm_mem=(1, D), norm_ffn=(1, D), norm_final=(D,))
    for r, n in enumerate(small_names):
        out_g[n], out_d[n], out_m[n], out_v[n] = (t[r].reshape(small_shapes[n]) for t in (sg, sd, snm, snv))
    out_g["conv_w"], out_d["conv_w"], out_m["conv_w"], out_v["conv_w"] = (
        t[5:8, 0:256].reshape(1, 3, 256) for t in (sg, sd, snm, snv))

    order = ("norm_mix", "w_in", "conv_w", "w_conv_out", "w_pool", "pool_scale", "norm_mem", "w_kv", "w_xattn_out",
             "w_out", "norm_ffn", "w_gate", "w_up", "w_down", "norm_final")
    return (loss, grad_x.reshape(1, t_len, D), *[out_g[n] for n in order], *[out_d[n] for n in order],
            *[out_m[n] for n in order], *[out_v[n] for n in order])
```

```python
import functools

import jax
import jax.numpy as jnp
from jax import lax
from jax.experimental import pallas as pl
from jax.experimental.pallas import tpu as pltpu

F32 = jnp.float32
BF16 = jnp.bfloat16
MESH = pl.DeviceIdType.MESH

D = 1024
N_MEM = 256
HEADS = 4
HEAD_DIM = 256
GROUPS = 4
GROUP_DIM = 256
POOL_WINDOWS = (2, 4, 8, 16)
D_FF = 2816
D_IN = 8192
N_CHIPS = 4
EPS = 1e-6
HALO = 16
POOL_PAD = 128
ATT_SCALE = HEAD_DIM ** -0.5

ADAM_LR = 0.001
ADAM_B1 = 0.9
ADAM_B2 = 0.999
ADAM_EPS = 1e-08
ADAM_WD = 0.01
ADAM_STEP = 10

VMEM_LIMIT = 56 * 1024 * 1024

O_BA, O_CA, O_UA, O_UP, O_QX, O_GA, O_GP, O_GX = (k * D for k in range(8))

NT_DIMS = (((1,), (1,)), ((), ()))
TN_DIMS = (((0,), (0,)), ((), ()))


def _dot(a, b):
    return jnp.dot(a, b, preferred_element_type=F32)


def _dot_nt(a, b):
    return lax.dot_general(a, b, NT_DIMS, preferred_element_type=F32)


def _dot_tn(a, b):
    return lax.dot_general(a, b, TN_DIMS, preferred_element_type=F32)


def _sigmoid(z):
    return pl.reciprocal(1.0 + jnp.exp(-z), approx=True)


def _params(semantics=None):
    return pltpu.CompilerParams(dimension_semantics=semantics, vmem_limit_bytes=VMEM_LIMIT)


def _resident(shape):
    zeros = (0,) * len(shape)
    return pl.BlockSpec(shape, lambda *_: zeros, pipeline_mode=pl.Buffered(1))


def _const(shape):
    zeros = (0,) * len(shape)
    return pl.BlockSpec(shape, lambda *_: zeros)


def _rows(tm, width):
    return pl.BlockSpec((tm, width), lambda i: (i, 0))


def _inv_count(tile, tm, window):
    t = tile * tm + lax.broadcasted_iota(jnp.int32, (tm, 1), 0)
    return 1.0 / jnp.minimum(t + 1, window).astype(F32)


def _carried_call(body, name, grid, in_specs, out_specs, out_shape, scratch_shapes, semantics, args, carry):
    if carry is None:
        res = pl.pallas_call(body, name=name, grid=grid, in_specs=in_specs, out_specs=out_specs, out_shape=out_shape,
                             scratch_shapes=scratch_shapes, compiler_params=_params(semantics))(*args)
        return res, []
    carries = [carry] if isinstance(carry, tuple) else list(carry)
    comm_args = [arr for _, arrs in carries for arr in arrs]
    n, n_in, n_out, n_scratch = len(comm_args), len(in_specs), len(out_specs), len(scratch_shapes)
    shapes_of = [cls.out_shapes(arrs) for cls, arrs in carries]
    comm_shapes = [s for shapes in shapes_of for s in shapes]
    m = len(comm_shapes)
    comm_sems = [s for cls, arrs in carries for s in cls.sems(len(arrs))]

    def carrying(*refs):
        ins, comm_ins = refs[:n_in], refs[n_in:n_in + n]
        outs, comm_outs = refs[n_in + n:n_in + n + n_out], refs[n_in + n + n_out:n_in + n + n_out + m]
        scratch, sems = refs[n_in + n + n_out + m:n_in + n + n_out + m + n_scratch], refs[n_in + n + n_out + m + n_scratch:]
        steps = [pl.program_id(d) for d in range(len(grid))]
        first = functools.reduce(jnp.logical_and, [s == 0 for s in steps])
        last = functools.reduce(jnp.logical_and, [s == g - 1 for s, g in zip(steps, grid)])

        def exchanges():
            at_in = at_out = 0
            for k, (cls, arrs) in enumerate(carries):
                yield cls(comm_ins[at_in:at_in + len(arrs)], comm_outs[at_out:at_out + len(shapes_of[k])],
                          sems[2 * k], sems[2 * k + 1])
                at_in, at_out = at_in + len(arrs), at_out + len(shapes_of[k])

        @pl.when(first)
        def _():
            for exchange in exchanges():
                exchange.start()

        linear, total = 0, 1
        for s, g in zip(steps, grid):
            linear, total = linear * g + s, total * g
        for k, (cls, _) in enumerate(carries):
            for phase, (num, den) in getattr(cls, "PHASES_AT", {}).items():
                @pl.when(linear == (num * total) // den)
                def _(k=k, phase=phase):
                    getattr(list(exchanges())[k], phase)()

        body(*ins, *outs, *scratch)

        @pl.when(last)
        def _():
            for exchange in exchanges():
                exchange.finish()

    res = pl.pallas_call(
        carrying, name=name, grid=grid, in_specs=list(in_specs) + _any_specs(n), out_specs=list(out_specs) + _any_specs(m),
        out_shape=list(out_shape) + comm_shapes, scratch_shapes=list(scratch_shapes) + comm_sems,
        compiler_params=_params(semantics))(*args, *comm_args)
    comm_res, at = [], n_out
    for (_, arrs), shapes in zip(carries, shapes_of):
        comm_res.append(res[at:at + len(arrs)])
        at += len(shapes)
    return res[:n_out], (comm_res[0] if isinstance(carry, tuple) else comm_res)


def proj_fwd(x, g_mix, w_in, carry=None):
    t_len = x.shape[0]
    tm = min(1024, t_len)
    tn = D_IN // N_CHIPS

    def body(x_ref, g_ref, w_ref, proj_ref, h_ref):
        @pl.when(pl.program_id(1) == 0)
        def _():
            xv = x_ref[...]
            r = lax.rsqrt(jnp.mean(xv * xv, axis=-1, keepdims=True) + EPS)
            h_ref[...] = (xv * r * g_ref[...]).astype(BF16)

        proj_ref[...] = _dot(h_ref[...], w_ref[...]).astype(BF16)

    return _carried_call(
        body, "proj_fwd", (t_len // tm, N_CHIPS),
        in_specs=[pl.BlockSpec((tm, D), lambda i, j: (i, 0)),
                  pl.BlockSpec((1, D), lambda i, j: (0, 0)),
                  pl.BlockSpec((D, tn), lambda i, j: (0, j))],
        out_specs=[pl.BlockSpec((tm, tn), lambda i, j: (i, j)),
                   pl.BlockSpec((tm, D), lambda i, j: (i, 0))],
        out_shape=[jax.ShapeDtypeStruct((t_len, D_IN), BF16), jax.ShapeDtypeStruct((t_len, D), BF16)],
        scratch_shapes=[], semantics=("arbitrary", "arbitrary"), args=(x, g_mix, w_in), carry=carry)


def kv_fwd(mem, g_mem, w_kv4):
    half = D // 2

    def body(mem_ref, g_ref, w_ref, memn_ref, k_ref, v_ref):
        mv = mem_ref[...]
        r = lax.rsqrt(jnp.mean(mv * mv, axis=-1, keepdims=True) + EPS)
        mn = (mv * r * g_ref[...]).astype(BF16)
        memn_ref[...] = mn
        k_ref[:, 0:half] = _dot(mn, w_ref[0]).astype(BF16)
        k_ref[:, half:D] = _dot(mn, w_ref[1]).astype(BF16)
        v_ref[:, 0:half] = _dot(mn, w_ref[2]).astype(BF16)
        v_ref[:, half:D] = _dot(mn, w_ref[3]).astype(BF16)

    out = jax.ShapeDtypeStruct((N_MEM, D), BF16)
    return pl.pallas_call(body, name="kv_fwd", out_shape=[out, out, out], compiler_params=_params())(mem, g_mem, w_kv4)


def _softmax_rows(s):
    m = jnp.max(s, axis=-1, keepdims=True)
    e = jnp.exp(s - m)
    return e * pl.reciprocal(jnp.sum(e, axis=-1, keepdims=True), approx=True)


def _window_bands(tm, causal):
    t = lax.broadcasted_iota(jnp.int32, (tm, tm + POOL_PAD), 0)
    s = lax.broadcasted_iota(jnp.int32, (tm, tm + POOL_PAD), 1)
    d = (t + POOL_PAD - s) if causal else (s - t)
    return jnp.stack([((d >= 0) & (d < w)).astype(BF16) for w in POOL_WINDOWS])


def mixer_fwd(proj, x, conv_w8, w_co, w_pool, pool_scale, k, v, w_xo, w_out, carry=None):
    t_len = x.shape[0]
    tm = min(256, t_len)

    def body(proj_ref, x_ref, cw_ref, wco_ref, wpool_ref, ps_ref, k_ref, v_ref, wxo_ref, wout_ref,
             a_ref, pooled_ref, ya_ref, pp_ref, yx_ref, o_ref, p_ref, x1_ref, cu_ext, up_ext):
        i = pl.program_id(0)

        @pl.when(i == 0)
        def _():
            cu_ext[0:HALO, :] = jnp.zeros((HALO, D), F32)
            up_ext[0:HALO, :] = jnp.zeros((HALO, D), F32)

        cu = proj_ref[:, O_CA:O_CA + D].astype(F32) * proj_ref[:, O_UA:O_UA + D].astype(F32)
        cu_ext[HALO:HALO + tm, :] = cu
        conv = (cw_ref[2:3, :] * cu + cw_ref[1:2, :] * cu_ext[HALO - 1:HALO - 1 + tm, :]
                + cw_ref[0:1, :] * cu_ext[HALO - 2:HALO - 2 + tm, :])
        a = (proj_ref[:, O_BA:O_BA + D].astype(F32) * conv).astype(BF16)
        a_ref[...] = a
        ya = _dot(a, wco_ref[...])
        ya_ref[...] = ya.astype(BF16)

        up_ext[HALO:HALO + tm, :] = proj_ref[:, O_UP:O_UP + D].astype(F32)
        for g, window in enumerate(POOL_WINDOWS):
            cols = slice(g * GROUP_DIM, (g + 1) * GROUP_DIM)
            tok = up_ext[HALO:HALO + tm, cols]
            acc = tok
            for j in range(1, window):
                acc = acc + up_ext[HALO - j:HALO - j + tm, cols]
            pooled = (acc * _inv_count(i, tm, window) - tok).astype(BF16)
            pooled_ref[:, cols] = pooled
            pp_ref[:, cols] = _dot(pooled, wpool_ref[g]).astype(BF16)

        for hd in range(HEADS):
            cols = slice(hd * HEAD_DIM, (hd + 1) * HEAD_DIM)
            q = proj_ref[:, O_QX + hd * HEAD_DIM:O_QX + (hd + 1) * HEAD_DIM]
            p = _softmax_rows(_dot_nt(q, k_ref[:, cols]) * ATT_SCALE).astype(BF16)
            p_ref[:, hd * N_MEM:(hd + 1) * N_MEM] = p
            o_ref[:, cols] = _dot(p, v_ref[:, cols]).astype(BF16)
        yx = _dot(o_ref[...], wxo_ref[...])
        yx_ref[...] = yx.astype(BF16)

        merged = (_sigmoid(proj_ref[:, O_GA:O_GA + D].astype(F32)) * ya
                  + _sigmoid(proj_ref[:, O_GP:O_GP + D].astype(F32)) * (pp_ref[...].astype(F32) * ps_ref[...])
                  + _sigmoid(proj_ref[:, O_GX:O_GX + D].astype(F32)) * yx)
        x1_ref[...] = x_ref[...] + _dot(merged.astype(BF16), wout_ref[...])

        cu_ext[0:HALO, :] = cu_ext[tm:tm + HALO, :]
        up_ext[0:HALO, :] = up_ext[tm:tm + HALO, :]

    act = jax.ShapeDtypeStruct((t_len, D), BF16)
    return _carried_call(
        body, "mixer_fwd", (t_len // tm,),
        in_specs=[_rows(tm, D_IN), _rows(tm, D), _resident((8, D)), _resident((D, D)),
                  _resident((GROUPS, GROUP_DIM, GROUP_DIM)), _resident((1, D)),
                  _resident((N_MEM, D)), _resident((N_MEM, D)), _resident((D, D)), _resident((D, D))],
        out_specs=[_rows(tm, D)] * 7 + [_rows(tm, D)],
        out_shape=[act] * 6 + [jax.ShapeDtypeStruct((t_len, HEADS * N_MEM), BF16), jax.ShapeDtypeStruct((t_len, D), F32)],
        scratch_shapes=[pltpu.VMEM((tm + HALO, D), F32), pltpu.VMEM((tm + HALO, D), F32)],
        semantics=("arbitrary",), args=(proj, x, conv_w8, w_co, w_pool, pool_scale, k, v, w_xo, w_out), carry=carry)


def ffn_fwd(x1, target, g_ffn, w_gate, w_up, w_down, g_final):
    t_len = x1.shape[0]
    tm = min(512, t_len)

    def body(x1_ref, tgt_ref, g_ref, wg_ref, wu_ref, wd_ref, gf_ref, gate_ref, up_ref, dx2_ref, stat_ref):
        @pl.when(pl.program_id(0) == 0)
        def _():
            stat_ref[...] = jnp.zeros((8, D), F32)

        x1v = x1_ref[...]
        r2 = lax.rsqrt(jnp.mean(x1v * x1v, axis=-1, keepdims=True) + EPS)
        h2 = (x1v * r2 * g_ref[...]).astype(BF16)
        gate = _dot_nt(h2, wg_ref[...])
        up = _dot_nt(h2, wu_ref[...])
        gate_ref[...] = gate.astype(BF16)
        up_ref[...] = up.astype(BF16)
        act = (gate * _sigmoid(gate) * up).astype(BF16)
        x2 = x1v + _dot(act, wd_ref[...])
        r3 = lax.rsqrt(jnp.mean(x2 * x2, axis=-1, keepdims=True) + EPS)
        xh = x2 * r3
        diff = xh * gf_ref[...] - tgt_ref[...]
        dy = diff * (1.0 / D)
        stat_ref[0:1, :] += jnp.sum(dy * xh, axis=0, keepdims=True)
        stat_ref[1:2, :] += (0.5 / D) * jnp.sum(diff * diff, axis=0, keepdims=True)
        dxh = dy * gf_ref[...]
        dx2_ref[...] = r3 * (dxh - xh * jnp.mean(dxh * xh, axis=-1, keepdims=True))

    return pl.pallas_call(
        body, name="ffn_fwd",
        grid=(t_len // tm,),
        in_specs=[_rows(tm, D), _rows(tm, D), _resident((1, D)), _resident((D_FF, D)), _resident((D_FF, D)),
                  _resident((D_FF, D)), _resident((1, D))],
        out_specs=[_rows(tm, D_FF), _rows(tm, D_FF), _rows(tm, D), _const((8, D))],
        out_shape=[jax.ShapeDtypeStruct((t_len, D_FF), BF16), jax.ShapeDtypeStruct((t_len, D_FF), BF16),
                   jax.ShapeDtypeStruct((t_len, D), F32), jax.ShapeDtypeStruct((8, D), F32)],
        compiler_params=_params(("arbitrary",)),
    )(x1, target, g_ffn, w_gate, w_up, w_down, g_final)


def ffn_bwd(dx2, x1, gate, up, g_ffn, w_gate, w_up, w_down):
    t_len = x1.shape[0]
    tm = min(256, t_len)

    def body(dx2_ref, x1_ref, gate_ref, up_ref, g_ref, wg_ref, wu_ref, wd_ref,
             dx1_ref, dgate_ref, dup_ref, act_ref, h2_ref, stat_ref):
        @pl.when(pl.program_id(0) == 0)
        def _():
            stat_ref[...] = jnp.zeros((8, D), F32)

        dx2v = dx2_ref[...]
        gate = gate_ref[...]
        upv = up_ref[...]
        sg = _sigmoid(gate.astype(F32)).astype(BF16)
        silu = gate * sg
        act_ref[...] = silu * upv
        dact = _dot_nt(dx2v.astype(BF16), wd_ref[...]).astype(BF16)
        dup = dact * silu
        dgate = dact * upv * (sg * (1.0 + gate * (1.0 - sg)))
        dup_ref[...] = dup
        dgate_ref[...] = dgate
        dh2 = _dot(dgate, wg_ref[...]) + _dot(dup, wu_ref[...])
        x1v = x1_ref[...]
        r2 = lax.rsqrt(jnp.mean(x1v * x1v, axis=-1, keepdims=True) + EPS)
        xh = x1v * r2
        h2_ref[...] = (xh * g_ref[...]).astype(BF16)
        stat_ref[0:1, :] += jnp.sum(dh2 * xh, axis=0, keepdims=True)
        dxh = dh2 * g_ref[...]
        dx1_ref[...] = dx2v + r2 * (dxh - xh * jnp.mean(dxh * xh, axis=-1, keepdims=True))

    ff = jax.ShapeDtypeStruct((t_len, D_FF), BF16)
    return pl.pallas_call(
        body, name="ffn_bwd",
        grid=(t_len // tm,),
        in_specs=[_rows(tm, D), _rows(tm, D), _rows(tm, D_FF), _rows(tm, D_FF), _resident((1, D)),
                  _resident((D_FF, D)), _resident((D_FF, D)), _resident((D_FF, D))],
        out_specs=[_rows(tm, D), _rows(tm, D_FF), _rows(tm, D_FF), _rows(tm, D_FF), _rows(tm, D), _const((8, D))],
        out_shape=[jax.ShapeDtypeStruct((t_len, D), F32), ff, ff, ff, jax.ShapeDtypeStruct((t_len, D), BF16),
                   jax.ShapeDtypeStruct((8, D), F32)],
        compiler_params=_params(("arbitrary",)),
    )(dx2, x1, gate, up, g_ffn, w_gate, w_up, w_down)


def mixer_bwd(dx1, proj, ya, pp, yx, probs, conv_w8, w_co, w_pool, pool_scale, k, v, w_xo, w_out, carry=None):
    t_len = dx1.shape[0]
    tm = min(256, t_len)
    n_tiles = t_len // tm
    halo_blocks = tm // HALO

    def body(dx1_ref, proj_ref, halo_ref, ya_ref, pp_ref, yx_ref, p_ref,
             cw_ref, wco_ref, wpool_ref, ps_ref, k_ref, v_ref, wxo_ref, wout_ref, band_ref,
             dproj_ref, merged_ref, dya_ref, dpp_ref, dyx_ref, dk_ref, dv_ref, stat_ref,
             cu_ext, dconv_ext, dpn_ext):
        step = pl.program_id(0)
        tile = n_tiles - 1 - step

        @pl.when(step == 0)
        def _():
            dk_ref[...] = jnp.zeros((N_MEM, D), F32)
            dv_ref[...] = jnp.zeros((N_MEM, D), F32)
            stat_ref[...] = jnp.zeros((8, D), F32)
            dconv_ext[tm:tm + HALO, :] = jnp.zeros((HALO, D), F32)
            dpn_ext[tm:tm + POOL_PAD, :] = jnp.zeros((POOL_PAD, D), BF16)

        dmerged = _dot_nt(dx1_ref[...].astype(BF16), wout_ref[...]).astype(BF16)
        sa = _sigmoid(proj_ref[:, O_GA:O_GA + D].astype(F32)).astype(BF16)
        sp = _sigmoid(proj_ref[:, O_GP:O_GP + D].astype(F32)).astype(BF16)
        sx = _sigmoid(proj_ref[:, O_GX:O_GX + D].astype(F32)).astype(BF16)
        ya = ya_ref[...]
        ppv = pp_ref[...]
        scale = ps_ref[...].astype(BF16)
        yp = ppv * scale
        yx = yx_ref[...]
        merged_ref[...] = sa * ya + sp * yp + sx * yx
        dproj_ref[:, O_GA:O_GA + D] = dmerged * ya * (sa * (1.0 - sa))
        dproj_ref[:, O_GP:O_GP + D] = dmerged * yp * (sp * (1.0 - sp))
        dproj_ref[:, O_GX:O_GX + D] = dmerged * yx * (sx * (1.0 - sx))
        dya = dmerged * sa
        dyp = dmerged * sp
        dyx = dmerged * sx
        dya_ref[...] = dya
        dyx_ref[...] = dyx
        stat_ref[1:2, :] += jnp.sum(dyp.astype(F32) * ppv.astype(F32), axis=0, keepdims=True)
        dpp = dyp * scale
        dpp_ref[...] = dpp

        da = _dot_nt(dya, wco_ref[...])
        c_a = proj_ref[:, O_CA:O_CA + D].astype(F32)
        u_a = proj_ref[:, O_UA:O_UA + D].astype(F32)
        cu = c_a * u_a
        halo_cu = halo_ref[:, O_CA:O_CA + D].astype(F32) * halo_ref[:, O_UA:O_UA + D].astype(F32)
        cu_ext[0:HALO, :] = jnp.where(tile > 0, halo_cu, 0.0)
        cu_ext[HALO:HALO + tm, :] = cu
        cu1 = cu_ext[HALO - 1:HALO - 1 + tm, :]
        cu2 = cu_ext[HALO - 2:HALO - 2 + tm, :]
        conv = cw_ref[2:3, :] * cu + cw_ref[1:2, :] * cu1 + cw_ref[0:1, :] * cu2
        dproj_ref[:, O_BA:O_BA + D] = (da * conv).astype(BF16)
        dconv = da * proj_ref[:, O_BA:O_BA + D].astype(F32)
        stat_ref[5:6, :] += jnp.sum(dconv * cu2, axis=0, keepdims=True)
        stat_ref[6:7, :] += jnp.sum(dconv * cu1, axis=0, keepdims=True)
        stat_ref[7:8, :] += jnp.sum(dconv * cu, axis=0, keepdims=True)
        dconv_ext[0:tm, :] = dconv
        dcu = (cw_ref[2:3, :] * dconv + cw_ref[1:2, :] * dconv_ext[1:1 + tm, :]
               + cw_ref[0:1, :] * dconv_ext[2:2 + tm, :])
        dproj_ref[:, O_CA:O_CA + D] = (dcu * u_a).astype(BF16)
        dproj_ref[:, O_UA:O_UA + D] = (dcu * c_a).astype(BF16)

        for g, window in enumerate(POOL_WINDOWS):
            cols = slice(g * GROUP_DIM, (g + 1) * GROUP_DIM)
            dpooled = _dot_nt(dpp[:, cols], wpool_ref[g])
            dpn_ext[0:tm, cols] = (dpooled * _inv_count(tile, tm, window)).astype(BF16)
            acc = _dot(band_ref[g], dpn_ext[:, cols])
            dproj_ref[:, O_UP + g * GROUP_DIM:O_UP + (g + 1) * GROUP_DIM] = (acc - dpooled).astype(BF16)

        do = _dot_nt(dyx, wxo_ref[...])
        for hd in range(HEADS):
            cols = slice(hd * HEAD_DIM, (hd + 1) * HEAD_DIM)
            q = proj_ref[:, O_QX + hd * HEAD_DIM:O_QX + (hd + 1) * HEAD_DIM]
            kh = k_ref[:, cols]
            p16 = p_ref[:, hd * N_MEM:(hd + 1) * N_MEM]
            p = p16.astype(F32)
            doh = do[:, cols].astype(BF16)
            dp = _dot_nt(doh, v_ref[:, cols])
            dv_ref[:, cols] += _dot_tn(p16, doh)
            ds = (p * (dp - jnp.sum(dp * p, axis=-1, keepdims=True)) * ATT_SCALE).astype(BF16)
            dproj_ref[:, O_QX + hd * HEAD_DIM:O_QX + (hd + 1) * HEAD_DIM] = _dot(ds, kh).astype(BF16)
            dk_ref[:, cols] += _dot_tn(ds, q)

        dconv_ext[tm:tm + HALO, :] = dconv_ext[0:HALO, :]
        dpn_ext[tm:tm + HALO, :] = dpn_ext[0:HALO, :]

    def rev(width):
        return pl.BlockSpec((tm, width), lambda s: (n_tiles - 1 - s, 0))

    halo_spec = pl.BlockSpec((HALO, D_IN), lambda s: (jnp.maximum((n_tiles - 1 - s) * halo_blocks - 1, 0), 0))
    act = jax.ShapeDtypeStruct((t_len, D), BF16)
    kv_grad = jax.ShapeDtypeStruct((N_MEM, D), F32)
    return _carried_call(
        body, "mixer_bwd", (n_tiles,),
        in_specs=[rev(D), rev(D_IN), halo_spec, rev(D), rev(D), rev(D), rev(D),
                  _resident((8, D)), _resident((D, D)), _resident((GROUPS, GROUP_DIM, GROUP_DIM)), _resident((1, D)),
                  _resident((N_MEM, D)), _resident((N_MEM, D)), _resident((D, D)), _resident((D, D)),
                  _resident((GROUPS, tm, tm + POOL_PAD))],
        out_specs=[rev(D_IN), rev(D), rev(D), rev(D), rev(D),
                   _const((N_MEM, D)), _const((N_MEM, D)), _const((8, D))],
        out_shape=[jax.ShapeDtypeStruct((t_len, D_IN), BF16), act, act, act, act, kv_grad, kv_grad,
                   jax.ShapeDtypeStruct((8, D), F32)],
        scratch_shapes=[pltpu.VMEM((tm + HALO, D), F32)] * 2 + [pltpu.VMEM((tm + POOL_PAD, D), BF16)],
        semantics=("arbitrary",),
        args=(dx1, proj, proj, ya, pp, yx, probs, conv_w8, w_co, w_pool, pool_scale, k, v, w_xo, w_out,
              _window_bands(tm, False)), carry=carry)


def in_bwd(dproj, w_in, x, dx1, g_mix, carry=None):
    t_len = x.shape[0]
    tm = min(512, t_len)

    def body(dproj_ref, w_ref, x_ref, dx1_ref, g_ref, gx_ref, stat_ref):
        @pl.when(pl.program_id(0) == 0)
        def _():
            stat_ref[...] = jnp.zeros((8, D), F32)

        dh = _dot_nt(dproj_ref[...], w_ref[...])
        xv = x_ref[...]
        r = lax.rsqrt(jnp.mean(xv * xv, axis=-1, keepdims=True) + EPS)
        xh = xv * r
        stat_ref[0:1, :] += jnp.sum(dh * xh, axis=0, keepdims=True)
        dxh = dh * g_ref[...]
        gx_ref[...] = dx1_ref[...] + r * (dxh - xh * jnp.mean(dxh * xh, axis=-1, keepdims=True))

    return _carried_call(
        body, "in_bwd", (t_len // tm,),
        in_specs=[_rows(tm, D_IN), _resident((D, D_IN)), _rows(tm, D), _rows(tm, D), _resident((1, D))],
        out_specs=[_rows(tm, D), _const((8, D))],
        out_shape=[jax.ShapeDtypeStruct((t_len, D), F32), jax.ShapeDtypeStruct((8, D), F32)],
        scratch_shapes=[], semantics=("arbitrary",), args=(dproj, w_in, x, dx1, g_mix), carry=carry)


def kv_bwd(dk, dv, memn, mem, g_mem, w_kv4):
    half = D // 2

    def body(dk_ref, dv_ref, memn_ref, mem_ref, g_ref, w_ref, gw_ref, gw16_ref, stat_ref):
        mn = memn_ref[...]
        parts = (dk_ref[:, 0:half], dk_ref[:, half:D], dv_ref[:, 0:half], dv_ref[:, half:D])
        dmemn = jnp.zeros((N_MEM, D), F32)
        for j, part in enumerate(parts):
            part = part.astype(BF16)
            gw = _dot_tn(mn, part)
            gw_ref[j] = gw
            gw16_ref[j] = gw.astype(BF16)
            dmemn = dmemn + _dot_nt(part, w_ref[j])
        mv = mem_ref[...]
        r = lax.rsqrt(jnp.mean(mv * mv, axis=-1, keepdims=True) + EPS)
        stat_ref[...] = jnp.zeros((8, D), F32)
        stat_ref[0:1, :] = jnp.sum(dmemn * (mv * r), axis=0, keepdims=True)

    return pl.pallas_call(
        body, name="kv_bwd",
        out_shape=[jax.ShapeDtypeStruct((N_CHIPS, D, half), F32), jax.ShapeDtypeStruct((N_CHIPS, D, half), BF16),
                   jax.ShapeDtypeStruct((8, D), F32)],
        compiler_params=_params(),
    )(dk, dv, memn, mem, g_mem, w_kv4)


def matmul_tn(name, a, b, tn, col_blocks=1, carry=None):
    t_len, k_dim = a.shape
    n_dim = b.shape[1]
    tt = min(1024, t_len)
    per_block = n_dim // col_blocks // tn

    def body(a_ref, b_ref, out_ref, out16_ref):
        @pl.when(pl.program_id(1) == 0)
        def _():
            out_ref[...] = jnp.zeros((k_dim, tn), F32)

        out_ref[...] += _dot_tn(a_ref[...].astype(BF16), b_ref[...].astype(BF16))

        @pl.when(pl.program_id(1) == t_len // tt - 1)
        def _():
            out16_ref[...] = out_ref[...].astype(BF16)

    if col_blocks == 1:
        out_spec = pl.BlockSpec((k_dim, tn), lambda n, t: (0, n))
        shape = (k_dim, n_dim)
    else:
        out_spec = pl.BlockSpec((None, k_dim, tn), lambda n, t: (n // per_block, 0, n % per_block))
        shape = (col_blocks, k_dim, n_dim // col_blocks)
    outs, carried = _carried_call(
        body, name, (n_dim // tn, t_len // tt),
        in_specs=[pl.BlockSpec((tt, k_dim), lambda n, t: (t, 0)), pl.BlockSpec((tt, tn), lambda n, t: (t, n))],
        out_specs=[out_spec, out_spec], out_shape=[jax.ShapeDtypeStruct(shape, F32), jax.ShapeDtypeStruct(shape, BF16)],
        scratch_shapes=[], semantics=("arbitrary", "arbitrary"), args=(a, b), carry=carry)
    return (tuple(outs), carried) if carry is not None else tuple(outs)


def grad_w_pool(pooled, dpp):
    t_len = pooled.shape[0]
    tt = min(1024, t_len)
    steps = t_len // tt

    def body(a_ref, b_ref, out_ref, out16_ref):
        @pl.when(pl.program_id(0) == 0)
        def _():
            out_ref[...] = jnp.zeros((GROUPS, GROUP_DIM, GROUP_DIM), F32)

        for g in range(GROUPS):
            cols = slice(g * GROUP_DIM, (g + 1) * GROUP_DIM)
            out_ref[g] += _dot_tn(a_ref[:, cols], b_ref[:, cols])

        @pl.when(pl.program_id(0) == steps - 1)
        def _():
            out16_ref[...] = out_ref[...].astype(BF16)

    shape = (GROUPS, GROUP_DIM, GROUP_DIM)
    return pl.pallas_call(
        body, name="grad_w_pool", grid=(steps,),
        in_specs=[_rows(tt, D), _rows(tt, D)], out_specs=[_const(shape), _const(shape)],
        out_shape=[jax.ShapeDtypeStruct(shape, F32), jax.ShapeDtypeStruct(shape, BF16)],
        compiler_params=_params(("arbitrary",)),
    )(pooled, dpp)


def _place():
    x, y, c = lax.axis_index("x"), lax.axis_index("y"), lax.axis_index("c")
    return x, y, c


def _other_chips(x, y):
    return [(1 - x, y), (x, 1 - y), (1 - x, 1 - y)]


def _any_specs(n):
    return [pl.BlockSpec(memory_space=pl.ANY)] * n


def exchange_call(name, exchange_cls, arrays):
    n = len(arrays)
    shapes = exchange_cls.out_shapes(arrays)

    def body(*refs):
        exchange = exchange_cls(refs[:n], refs[n:n + len(shapes)], *refs[n + len(shapes):])
        exchange.start()
        for phase in getattr(exchange_cls, "PHASES_AT", {}):
            getattr(exchange, phase)()
        exchange.finish()

    res = pl.pallas_call(
        body, name=name, in_specs=_any_specs(n), out_specs=_any_specs(len(shapes)),
        out_shape=shapes, scratch_shapes=exchange_cls.sems(n),
    )(*arrays)
    return res[:n]


def gather_w_in_and_cast(w_in16, shards):
    n = len(shards)

    def body(*refs):
        w_ref, ins, g_ref, outs = refs[0], refs[1:1 + n], refs[1 + n], refs[2 + n:2 + 2 * n]
        wide, narrow = refs[2 + 2 * n:2 + 3 * n], refs[2 + 3 * n:2 + 4 * n]
        send_sems, recv_sems, load_sems, store_sems = refs[2 + 4 * n:]
        gather = _GatherColumns([w_ref], [g_ref], send_sems, recv_sems)
        gather.start()
        loads = [pltpu.make_async_copy(ins[a], wide[a], load_sems.at[a]) for a in range(n)]
        stores = [pltpu.make_async_copy(narrow[a], outs[a], store_sems.at[a]) for a in range(n)]
        for cp in loads:
            cp.start()
        for a in range(n):
            loads[a].wait()
            narrow[a][...] = wide[a][...].astype(BF16)
            stores[a].start()
        for cp in stores:
            cp.wait()
        gather.middle()
        gather.late()
        gather.finish()

    res = pl.pallas_call(
        body, name="gather_w_in_and_cast",
        in_specs=_any_specs(1 + n), out_specs=_any_specs(1 + n),
        out_shape=_GatherColumns.out_shapes([w_in16]) + [jax.ShapeDtypeStruct(s.shape, BF16) for s in shards],
        scratch_shapes=([pltpu.VMEM(s.shape, F32) for s in shards] + [pltpu.VMEM(s.shape, BF16) for s in shards]
                        + _GatherColumns.sems(1) + [pltpu.SemaphoreType.DMA((n,)), pltpu.SemaphoreType.DMA((n,))]),
        compiler_params=_params(),
    )(w_in16, *shards)
    return res[0], res[1:]


class _Gather:
    SLOTS = 8
    ROW_ALIGN = 16
    PHASES_AT = {"middle": (4, 8), "late": (7, 8)}

    @staticmethod
    def out_shapes(shards):
        return [jax.ShapeDtypeStruct((N_CHIPS,) + s.shape, s.dtype) for s in shards]

    @staticmethod
    def sems(n):
        return [pltpu.SemaphoreType.DMA((n, _Gather.SLOTS)), pltpu.SemaphoreType.DMA((n, _Gather.SLOTS))]

    def __init__(self, ins, outs, send_sems, recv_sems):
        self.ins, self.outs, self.send_sems, self.recv_sems = ins, outs, send_sems, recv_sems
        x, y, c = _place()
        self.c, self.me, self.sibling = c, 2 * x + y, (x, y, 1 - c)
        self.across = [(1 - x, y), (x, 1 - y), (1 - x, 1 - y)]

    def _rows(self, a, which, part=None):
        half = self.ins[a].shape[1] // 2
        first = (half // 2) // self.ROW_ALIGN * self.ROW_ALIGN
        if part is None:
            return pl.ds(which * half, half)
        return pl.ds(which * half, first) if part == 0 else pl.ds(which * half + first, half - first)

    def _has_part(self, a, part):
        half = self.ins[a].shape[1] // 2
        return part == 1 or (half // 2) // self.ROW_ALIGN > 0

    def _block(self, a, chip, rows=slice(None)):
        index = chip if not isinstance(chip, tuple) else 2 * chip[0] + chip[1]
        if len(self.outs[a].shape) == len(self.ins[a].shape):
            cols = self.ins[a].shape[2]
            return self.outs[a].at[:, rows, pl.ds(pl.multiple_of(index * cols, cols), cols)]
        return self.outs[a].at[index, :, rows, :]

    def _remote(self, src, dst, a, slot, to):
        return pltpu.make_async_remote_copy(src_ref=src, dst_ref=dst, send_sem=self.send_sems.at[a, slot],
                                            recv_sem=self.recv_sems.at[a, slot], device_id=to, device_id_type=MESH)

    def _own(self, a):
        return self._remote(self.ins[a], self._block(a, self.me), a, 6, self.sibling)

    def _sent(self, a, axis):
        rows = self._rows(a, self.c)
        return self._remote(self.ins[a].at[:, rows, :], self._block(a, self.me, rows), a, axis,
                            (*self.across[axis], self.c))

    def _landed(self, a, axis):
        block = self._block(a, self.across[axis], self._rows(a, self.c))
        return self._remote(block, block, a, axis, (*self.across[axis], self.c))

    def _relayed(self, a, part, incoming):
        source = self.across[2] if incoming else self.across[part]
        block = self._block(a, source, self._rows(a, self.c, part))
        return self._remote(block, block, a, (2, 7)[part], (*self.across[1 - part], self.c))

    def _passed_on(self, a, source, which):
        block = self._block(a, self.across[source], self._rows(a, which))
        return self._remote(block, block, a, 3 + source, self.sibling)

    def start(self):
        for a in range(len(self.ins)):
            self._own(a).start()
        for a in range(len(self.ins)):
            for axis in range(2):
                self._sent(a, axis).start()

    def middle(self):
        for a in range(len(self.ins)):
            for axis in range(2):
                self._landed(a, axis).wait_recv()
                if self._has_part(a, axis):
                    self._relayed(a, axis, incoming=False).start()
                self._passed_on(a, axis, self.c).start()

    def late(self):
        for a in range(len(self.ins)):
            for part in range(2):
                if self._has_part(a, part):
                    self._relayed(a, part, incoming=True).wait_recv()
            self._passed_on(a, 2, self.c).start()

    def finish(self):
        n = len(self.ins)
        for a in range(n):
            for source in range(3):
                self._passed_on(a, source, 1 - self.c).wait_recv()
            self._own(a).wait_recv()
        for a in range(n):
            self._own(a).wait_send()
            for axis in range(2):
                self._sent(a, axis).wait_send()
                if self._has_part(a, axis):
                    self._relayed(a, axis, incoming=False).wait_send()
            for source in range(3):
                self._passed_on(a, source, self.c).wait_send()


class _HalfExchange:
    @staticmethod
    def out_shapes(grads):
        return [jax.ShapeDtypeStruct((N_CHIPS, g.shape[1] // 2, g.shape[2]), g.dtype) for g in grads]

    @staticmethod
    def sems(n):
        return [pltpu.SemaphoreType.DMA((n,)), pltpu.SemaphoreType.DMA((n,))]

    def __init__(self, ins, outs, send_sems, recv_sems):
        self.ins, self.outs, self.send_sems, self.recv_sems = ins, outs, send_sems, recv_sems

    def _copies(self):
        x, y, c = _place()
        for a in range(len(self.ins)):
            h = self.ins[a].shape[1] // 2
            yield pltpu.make_async_remote_copy(
                src_ref=self.ins[a].at[:, pl.ds((1 - c) * h, h), :], dst_ref=self.outs[a],
                send_sem=self.send_sems.at[a], recv_sem=self.recv_sems.at[a], device_id=(x, y, 1 - c), device_id_type=MESH)

    def start(self):
        for cp in self._copies():
            cp.start()

    def finish(self):
        for cp in self._copies():
            cp.wait()


class _GatherColumns(_Gather):
    @staticmethod
    def out_shapes(shards):
        return [jax.ShapeDtypeStruct(s.shape[:2] + (N_CHIPS * s.shape[2],), s.dtype) for s in shards]


def _row_tile(rows, cols, budget=2 << 20):
    best = 16
    for tr in range(16, rows + 1, 16):
        if rows % tr == 0 and tr * cols * 4 <= budget:
            best = tr
    return best


def add_sibling_half(name, grad, got, place):
    _, rows, cols = grad.shape
    h = rows // 2
    tr = _row_tile(h, cols)
    per_half = h // tr

    def body(place_ref, g_ref, o_ref, own_ref, out16_ref):
        total = g_ref[...] + o_ref[...].astype(F32)
        out16_ref[...] = total.astype(BF16)

        @pl.when(pl.program_id(1) == place_ref[1])
        def _():
            own_ref[...] = total

    return pl.pallas_call(
        body, name=name,
        grid_spec=pltpu.PrefetchScalarGridSpec(
            num_scalar_prefetch=1, grid=(per_half, N_CHIPS),
            in_specs=[pl.BlockSpec((None, tr, cols), lambda r, j, pr: (j, pr[0] * per_half + r, 0)),
                      pl.BlockSpec((None, tr, cols), lambda r, j, pr: (j, r, 0))],
            out_specs=[pl.BlockSpec((tr, cols), lambda r, j, pr: (r, 0)),
                       pl.BlockSpec((None, tr, cols), lambda r, j, pr: (j, r, 0))]),
        out_shape=[jax.ShapeDtypeStruct((h, cols), F32), jax.ShapeDtypeStruct((N_CHIPS, h, cols), BF16)],
        compiler_params=_params(("arbitrary", "arbitrary")),
    )(place, grad, got)


class _ChipExchange:
    SLOTS = 6
    ROW_ALIGN = 16
    PHASES_AT = {"middle": (3, 8)}

    @staticmethod
    def out_shapes(partials):
        return ([jax.ShapeDtypeStruct((3,) + p.shape[1:], p.dtype) for p in partials]
                + [jax.ShapeDtypeStruct(p.shape[1:], p.dtype) for p in partials])

    @staticmethod
    def sems(n):
        return [pltpu.SemaphoreType.DMA((n, _ChipExchange.SLOTS)), pltpu.SemaphoreType.DMA((n, _ChipExchange.SLOTS))]

    def __init__(self, ins, outs, send_sems, recv_sems):
        n = len(ins)
        self.ins, self.outs, self.relays, self.send_sems, self.recv_sems = ins, outs[:n], outs[n:], send_sems, recv_sems

    def _copy(self, a, slot):
        x, y, c = _place()
        across = _other_chips(x, y)
        rows = self.ins[a].shape[1]
        first = (rows // 2) // self.ROW_ALIGN * self.ROW_ALIGN
        part = (pl.ds(0, first), pl.ds(first, rows - first))
        if slot < 2:
            px, py = across[slot]
            src, dst, to = self.ins[a].at[2 * px + py], self.outs[a].at[slot], across[slot]
        elif slot < 4:
            px, py = across[2]
            src, dst, to = self.ins[a].at[2 * px + py, part[slot - 2], :], self.relays[a].at[part[slot - 2], :], across[slot - 2]
        else:
            src, dst, to = self.relays[a].at[part[slot - 4], :], self.outs[a].at[2, part[slot - 4], :], across[5 - slot]
        return pltpu.make_async_remote_copy(src_ref=src, dst_ref=dst, send_sem=self.send_sems.at[a, slot],
                                            recv_sem=self.recv_sems.at[a, slot], device_id=(*to, c), device_id_type=MESH)

    def start(self):
        for slot in (2, 3, 0, 1):
            for a in range(len(self.ins)):
                self._copy(a, slot).start()

    def middle(self):
        for part in range(2):
            for a in range(len(self.ins)):
                self._copy(a, 2 + part).wait_recv()
                self._copy(a, 4 + part).start()

    def finish(self):
        for a in range(len(self.ins)):
            for slot in (0, 1, 4, 5):
                self._copy(a, slot).wait_recv()
        for a in range(len(self.ins)):
            for slot in range(self.SLOTS):
                self._copy(a, slot).wait_send()


def add_chip_blocks(name, own, got):
    h, cols = own.shape
    tr = _row_tile(h, cols)

    def body(p_ref, g0_ref, g1_ref, g2_ref, out_ref):
        out_ref[...] = ((p_ref[...] + g0_ref[...].astype(F32)) + g1_ref[...].astype(F32)) + g2_ref[...].astype(F32)

    def got_spec(slot):
        return pl.BlockSpec((None, tr, cols), lambda r: (slot, r, 0))

    return pl.pallas_call(
        body, name=name, grid=(h // tr,),
        in_specs=[_rows(tr, cols), got_spec(0), got_spec(1), got_spec(2)], out_specs=_rows(tr, cols),
        out_shape=jax.ShapeDtypeStruct((h, cols), F32),
        compiler_params=_params(("arbitrary",)),
    )(own, got, got, got)


class _SiblingSwap:
    @staticmethod
    def out_shapes(halves):
        return [jax.ShapeDtypeStruct(v.shape, v.dtype) for v in halves]

    @staticmethod
    def sems(n):
        return [pltpu.SemaphoreType.DMA((n,)), pltpu.SemaphoreType.DMA((n,))]

    def __init__(self, ins, outs, send_sems, recv_sems):
        self.ins, self.outs, self.send_sems, self.recv_sems = ins, outs, send_sems, recv_sems

    def _copies(self):
        x, y, c = _place()
        for a in range(len(self.ins)):
            yield pltpu.make_async_remote_copy(
                src_ref=self.ins[a], dst_ref=self.outs[a], send_sem=self.send_sems.at[a], recv_sem=self.recv_sems.at[a],
                device_id=(x, y, 1 - c), device_id_type=MESH)

    def start(self):
        for cp in self._copies():
            cp.start()

    def finish(self):
        for cp in self._copies():
            cp.wait()


def all_reduce_small(pack):
    rows = pack.shape[0]

    def body(pack_ref, out_ref, gathered, send_sems, recv_sems):
        x, y, c = _place()
        me = 4 * x + 2 * y + c
        gathered[me] = pack_ref[...]
        copies = []
        for rel in range(1, 8):
            fx, fy, fc = (rel >> 2) & 1, (rel >> 1) & 1, rel & 1
            peer = (x ^ fx, y ^ fy, c ^ fc)
            cp = pltpu.make_async_remote_copy(
                src_ref=pack_ref, dst_ref=gathered.at[me], send_sem=send_sems.at[rel - 1], recv_sem=recv_sems.at[rel - 1],
                device_id=peer, device_id_type=MESH)
            cp.start()
            copies.append(cp)
        for rel in range(1, 8):
            fx, fy, fc = (rel >> 2) & 1, (rel >> 1) & 1, rel & 1
            src = 4 * (x ^ fx) + 2 * (y ^ fy) + (c ^ fc)
            pltpu.make_async_remote_copy(
                src_ref=pack_ref, dst_ref=gathered.at[src], send_sem=send_sems.at[rel - 1], recv_sem=recv_sems.at[rel - 1],
                device_id=(x, y, c), device_id_type=MESH).wait_recv()
        for cp in copies:
            cp.wait_send()
        total = gathered[0]
        for dev in range(1, 8):
            total = total + gathered[dev]
        out_ref[...] = total

    return pl.pallas_call(
        body, name="all_reduce_small",
        in_specs=[pl.BlockSpec(memory_space=pltpu.VMEM)], out_specs=pl.BlockSpec(memory_space=pltpu.VMEM),
        out_shape=jax.ShapeDtypeStruct((rows, D), F32),
        scratch_shapes=[pltpu.VMEM((8, rows, D), F32), pltpu.SemaphoreType.DMA((7,)), pltpu.SemaphoreType.DMA((7,))],
    )(pack)


def _adamw_update(w, g, m, v):
    nm = ADAM_B1 * m + (1.0 - ADAM_B1) * g
    nv = ADAM_B2 * v + (1.0 - ADAM_B2) * (g * g)
    m_hat = nm / (1.0 - ADAM_B1 ** ADAM_STEP)
    v_hat = nv / (1.0 - ADAM_B2 ** ADAM_STEP)
    delta = -ADAM_LR * (m_hat / (jnp.sqrt(v_hat) + ADAM_EPS) + ADAM_WD * w)
    return delta, nm, nv


def adamw(name, w, g, m, v):
    def body(w_ref, g_ref, m_ref, v_ref, d_ref, nm_ref, nv_ref):
        d_ref[...], nm_ref[...], nv_ref[...] = _adamw_update(w_ref[...], g_ref[...], m_ref[...], v_ref[...])

    out = jax.ShapeDtypeStruct(w.shape, F32)
    return pl.pallas_call(body, name=name, out_shape=[out] * 3, compiler_params=_params())(w, g, m, v)


def adamw_halves(name, w, mine, theirs, m, v, core):
    h, cols = mine.shape
    tr = _row_tile(h, cols)
    per_half = h // tr

    def body(core_ref, w_ref, mine_ref, theirs_ref, m_ref, v_ref, g_ref, d_ref, nm_ref, nv_ref):
        g = jnp.where(pl.program_id(0) == core_ref[0], mine_ref[...], theirs_ref[...])
        g_ref[...] = g
        d_ref[...], nm_ref[...], nv_ref[...] = _adamw_update(w_ref[...], g, m_ref[...], v_ref[...])

    full = pl.BlockSpec((tr, cols), lambda hh, r, cr: (hh * per_half + r, 0))
    mine_spec = pl.BlockSpec((tr, cols), lambda hh, r, cr: (jnp.where(hh == cr[0], r, 0), 0))
    theirs_spec = pl.BlockSpec((tr, cols), lambda hh, r, cr: (jnp.where(hh == cr[0], 0, r), 0))
    out = jax.ShapeDtypeStruct((2 * h, cols), F32)
    return pl.pallas_call(
        body, name=name,
        grid_spec=pltpu.PrefetchScalarGridSpec(
            num_scalar_prefetch=1, grid=(2, per_half),
            in_specs=[full, mine_spec, theirs_spec, full, full], out_specs=[full] * 4),
        out_shape=[out] * 4,
        compiler_params=_params(("arbitrary", "arbitrary")),
    )(core, w, mine, theirs, m, v)


BIG = ("w_in", "w_conv_out", "w_pool", "w_kv", "w_xattn_out", "w_out", "w_gate", "w_up", "w_down")


def kernel(x, mem, norm_mix, w_in, conv_w, w_conv_out, w_pool, pool_scale, norm_mem, w_kv, w_xattn_out, w_out, norm_ffn, w_gate, w_up, w_down, norm_final, loss_target, m_norm_mix, m_w_in, m_conv_w, m_w_conv_out, m_w_pool, m_pool_scale, m_norm_mem, m_w_kv, m_w_xattn_out, m_w_out, m_norm_ffn, m_w_gate, m_w_up, m_w_down, m_norm_final, v_norm_mix, v_w_in, v_conv_w, v_w_conv_out, v_w_pool, v_pool_scale, v_norm_mem, v_w_kv, v_w_xattn_out, v_w_out, v_norm_ffn, v_w_gate, v_w_up, v_w_down, v_norm_final):
    t_len = x.shape[1]
    xi, yi, ci = lax.axis_index("x"), lax.axis_index("y"), lax.axis_index("c")
    chip = 2 * xi + yi
    core_arr = jnp.reshape(ci, (1,)).astype(jnp.int32)
    place_arr = jnp.stack([ci, chip]).astype(jnp.int32)

    conv_pad = jnp.concatenate([conv_w, jnp.zeros((1, 13, 256), F32)], axis=1)
    def t2(w):
        return jnp.swapaxes(w, 1, 2)

    g_in, (kv16, co16, xo16, out16, pool16, gate16, up16, down16) = gather_w_in_and_cast(
        w_in.astype(BF16), [w_kv, w_conv_out, w_xattn_out, w_out, w_pool[0], t2(w_gate), t2(w_up), w_down])
    w_in_f = g_in[0]

    x2d = x[0]
    tgt = loss_target[0]
    (proj, h), (g_kv, g_conv_w, g_co, g_xo, g_out, g_pool, g_gate) = proj_fwd(
        x2d, norm_mix, w_in_f,
        carry=(_Gather, [kv16, conv_pad,
                         co16, xo16, out16,
                         pool16,
                         gate16]))
    w_kv4 = g_kv.reshape(N_CHIPS, D, D // 2)
    conv_full = jnp.transpose(g_conv_w[:, 0, 0:8, :], (1, 0, 2)).reshape(8, D)
    w_co_f, w_xo_f, w_out_f = g_co.reshape(D, D), g_xo.reshape(D, D), g_out.reshape(D, D)
    w_pool_f = jnp.transpose(g_pool, (1, 0, 2, 3)).reshape(GROUPS, GROUP_DIM, GROUP_DIM)
    memn, k, v = kv_fwd(mem[0], norm_mem, w_kv4)
    (a, pooled, ya, pp, yx, o, probs, x1), (g_up, g_down) = mixer_fwd(
        proj, x2d, conv_full, w_co_f, w_pool_f, pool_scale, k, v, w_xo_f, w_out_f,
        carry=(_Gather, [up16, down16]))
    w_gate_f, w_up_f, w_down_f = g_gate.reshape(D_FF, D), g_up.reshape(D_FF, D), g_down.reshape(D_FF, D)
    gate, up, dx2, stat_f = ffn_fwd(x1, tgt, norm_ffn, w_gate_f, w_up_f, w_down_f, norm_final.reshape(1, D))

    def by_chip(pair):
        return tuple(gw.reshape(N_CHIPS, gw.shape[0] // N_CHIPS, gw.shape[1]) for gw in pair)

    def chip_partials(names, grads, got):
        return [add_sibling_half("add_sibling_" + n, g32, o_, place_arr) for n, (g32, _), o_ in zip(names, grads, got)]

    def chip_sums(names, partials, got):
        return [add_chip_blocks("add_chips_" + n, own, g2) for n, (own, _), g2 in zip(names, partials, got)]

    dx1, dgate, dup, act, h2, stat_b1 = ffn_bwd(dx2, x1, gate, up, norm_ffn, w_gate_f, w_up_f, w_down_f)
    gw_gate = by_chip(matmul_tn("grad_w_gate", dgate, h2, 512))
    gw_up, got_gate = matmul_tn("grad_w_up", dup, h2, 512, carry=(_HalfExchange, [gw_gate[1]]))
    gw_up = by_chip(gw_up)
    gw_down, got_up = matmul_tn("grad_w_down", act, dx2, 512, carry=(_HalfExchange, [gw_up[1]]))
    gw_down = by_chip(gw_down)
    got_down = exchange_call("exchange_halves_w_down", _HalfExchange, [gw_down[1]])
    names_ffn = ("w_gate", "w_up", "w_down")
    part_ffn = chip_partials(names_ffn, [gw_gate, gw_up, gw_down], list(got_gate) + list(got_up) + list(got_down))

    (dproj, merged, dya, dpp, dyx, dk, dv, stat_b2), got_ffn = mixer_bwd(
        dx1, proj, ya, pp, yx, probs, conv_full, w_co_f, w_pool_f, pool_scale, k, v, w_xo_f, w_out_f,
        carry=(_ChipExchange, [p16 for _, p16 in part_ffn]))
    gw_kv32, gw_kv16, stat_kv = kv_bwd(dk, dv, memn, mem[0], norm_mem, w_kv4)
    gw_pool = tuple(jnp.transpose(gw.reshape(GROUPS, N_CHIPS, 64, GROUP_DIM), (1, 0, 2, 3)).reshape(N_CHIPS, 256, GROUP_DIM)
                    for gw in grad_w_pool(pooled, dpp))
    gw_co, got_kv_pool = matmul_tn("grad_w_conv_out", a, dya, 1024, carry=(_HalfExchange, [gw_kv16, gw_pool[1]]))
    gw_co = by_chip(gw_co)
    gw_xo, got_co = matmul_tn("grad_w_xattn_out", o, dyx, 1024, carry=(_HalfExchange, [gw_co[1]]))
    gw_xo = by_chip(gw_xo)
    gw_out, got_xo = matmul_tn("grad_w_out", merged, dx1, 1024, carry=(_HalfExchange, [gw_xo[1]]))
    gw_out = by_chip(gw_out)
    got_out = exchange_call("exchange_halves_w_out", _HalfExchange, [gw_out[1]])
    names_mix = ("w_kv", "w_pool", "w_conv_out", "w_xattn_out", "w_out")
    part_mix = chip_partials(names_mix, [(gw_kv32, gw_kv16), gw_pool, gw_co, gw_xo, gw_out],
                             list(got_kv_pool) + list(got_co) + list(got_xo) + list(got_out))

    gw_in, got_mix = matmul_tn("grad_w_in", h, dproj, 2048, col_blocks=N_CHIPS,
                               carry=(_ChipExchange, [p16 for _, p16 in part_mix]))
    part_in = chip_partials(("w_in",), [gw_in], exchange_call("exchange_halves_w_in", _HalfExchange, [gw_in[1]]))
    mine_early = chip_sums(names_ffn + names_mix, part_ffn + part_mix, list(got_ffn) + list(got_mix))
    (grad_x, stat_b3), (got_in, theirs_early) = in_bwd(
        dproj, w_in_f, x2d, dx1, norm_mix,
        carry=[(_ChipExchange, [p16 for _, p16 in part_in]), (_SiblingSwap, mine_early)])
    mine_in = chip_sums(("w_in",), part_in, got_in)
    theirs_in = exchange_call("swap_halves_w_in", _SiblingSwap, mine_in)
    reduced = dict(zip(names_ffn + names_mix + ("w_in",),
                       zip(mine_early + mine_in, list(theirs_early) + list(theirs_in))))
    mine = [reduced[n][0] for n in BIG]
    theirs = [reduced[n][1] for n in BIG]

    pack = jnp.concatenate([stat_b3[0:1], stat_b2[1:2], stat_kv[0:1], stat_b1[0:1], stat_f[0:1], stat_b2[5:8],
                            stat_f[1:2], jnp.zeros((7, D), F32)], axis=0)
    total = all_reduce_small(pack)
    loss = jnp.sum(total[8])
    g_conv_full = total[5:8]
    g_conv = lax.dynamic_slice_in_dim(g_conv_full, chip * 256, 256, axis=1)

    given = dict(w_in=(w_in, m_w_in, v_w_in), w_conv_out=(w_conv_out, m_w_conv_out, v_w_conv_out),
                 w_pool=(w_pool, m_w_pool, v_w_pool), w_kv=(w_kv, m_w_kv, v_w_kv),
                 w_xattn_out=(w_xattn_out, m_w_xattn_out, v_w_xattn_out), w_out=(w_out, m_w_out, v_w_out),
                 w_gate=(w_gate, m_w_gate, v_w_gate), w_up=(w_up, m_w_up, v_w_up), w_down=(w_down, m_w_down, v_w_down))
    out_g, out_d, out_m, out_v = {}, {}, {}, {}
    for n, mine_n, theirs_n in zip(BIG, mine, theirs):
        transposed = n in ("w_gate", "w_up")
        rows2d = (2 * mine_n.shape[0], mine_n.shape[1])
        w_, m_, v_ = ((t2(t) if transposed else t).reshape(rows2d) for t in given[n])
        res = adamw_halves("adamw_" + n, w_, mine_n, theirs_n, m_, v_, core_arr)
        if transposed:
            res = [t2(t.reshape(1, D_FF // N_CHIPS, D)) for t in res]
        out_g[n], out_d[n], out_m[n], out_v[n] = (t.reshape(given[n][0].shape) for t in res)

    def small_pack(vals, conv_part):
        conv_rows = jnp.concatenate([conv_part.reshape(3, 256), jnp.zeros((3, D - 256), F32)], axis=1)
        return jnp.concatenate([val.reshape(1, D) for val in vals] + [conv_rows], axis=0)

    sw = small_pack([norm_mix, pool_scale, norm_mem, norm_ffn, norm_final], conv_w)
    sm = small_pack([m_norm_mix, m_pool_scale, m_norm_mem, m_norm_ffn, m_norm_final], m_conv_w)
    sv = small_pack([v_norm_mix, v_pool_scale, v_norm_mem, v_norm_ffn, v_norm_final], v_conv_w)
    sg = small_pack([total[r] for r in range(5)], g_conv)
    sd, snm, snv = adamw("adamw_small", sw, sg, sm, sv)
    small_names = ("norm_mix", "pool_scale", "norm_mem", "norm_ffn", "norm_final")
    small_shapes = dict(norm_mix=(1, D), pool_scale=(1, D), norm_mem=(1, D), norm_ffn=(1, D), norm_final=(D,))
    for r, n in enumerate(small_names):
        out_g[n], out_d[n], out_m[n], out_v[n] = (t[r].reshape(small_shapes[n]) for t in (sg, sd, snm, snv))
    out_g["conv_w"], out_d["conv_w"], out_m["conv_w"], out_v["conv_w"] = (
        t[5:8, 0:256].reshape(1, 3, 256) for t in (sg, sd, snm, snv))

    order = ("norm_mix", "w_in", "conv_w", "w_conv_out", "w_pool", "pool_scale", "norm_mem", "w_kv", "w_xattn_out",
             "w_out", "norm_ffn", "w_gate", "w_up", "w_down", "norm_final")
    return (loss, grad_x.reshape(1, t_len, D), *[out_g[n] for n in order], *[out_d[n] for n in order],
            *[out_m[n] for n in order], *[out_v[n] for n in order])
```

```python
import functools

import jax
import jax.numpy as jnp
from jax import lax
from jax.experimental import pallas as pl
from jax.experimental.pallas import tpu as pltpu

F32 = jnp.float32
BF16 = jnp.bfloat16
MESH = pl.DeviceIdType.MESH

D = 1024
N_MEM = 256
HEADS = 4
HEAD_DIM = 256
GROUPS = 4
GROUP_DIM = 256
POOL_WINDOWS = (2, 4, 8, 16)
D_FF = 2816
D_IN = 8192
N_CHIPS = 4
EPS = 1e-6
HALO = 16
POOL_PAD = 128
ATT_SCALE = HEAD_DIM ** -0.5

ADAM_LR = 0.001
ADAM_B1 = 0.9
ADAM_B2 = 0.999
ADAM_EPS = 1e-08
ADAM_WD = 0.01
ADAM_STEP = 10

VMEM_LIMIT = 56 * 1024 * 1024

O_BA, O_CA, O_UA, O_UP, O_QX, O_GA, O_GP, O_GX = (k * D for k in range(8))

NT_DIMS = (((1,), (1,)), ((), ()))
TN_DIMS = (((0,), (0,)), ((), ()))


def _dot(a, b):
    return jnp.dot(a, b, preferred_element_type=F32)


def _dot_nt(a, b):
    return lax.dot_general(a, b, NT_DIMS, preferred_element_type=F32)


def _dot_tn(a, b):
    return lax.dot_general(a, b, TN_DIMS, preferred_element_type=F32)


def _sigmoid(z):
    return pl.reciprocal(1.0 + jnp.exp(-z), approx=True)


def _params(semantics=None):
    return pltpu.CompilerParams(dimension_semantics=semantics, vmem_limit_bytes=VMEM_LIMIT)


def _resident(shape):
    zeros = (0,) * len(shape)
    return pl.BlockSpec(shape, lambda *_: zeros, pipeline_mode=pl.Buffered(1))


def _const(shape):
    zeros = (0,) * len(shape)
    return pl.BlockSpec(shape, lambda *_: zeros)


def _rows(tm, width):
    return pl.BlockSpec((tm, width), lambda i: (i, 0))


def _inv_count(tile, tm, window):
    t = tile * tm + lax.broadcasted_iota(jnp.int32, (tm, 1), 0)
    return 1.0 / jnp.minimum(t + 1, window).astype(F32)


def _carried_call(body, name, grid, in_specs, out_specs, out_shape, scratch_shapes, semantics, args, carry):
    if carry is None:
        res = pl.pallas_call(body, name=name, grid=grid, in_specs=in_specs, out_specs=out_specs, out_shape=out_shape,
                             scratch_shapes=scratch_shapes, compiler_params=_params(semantics))(*args)
        return res, []
    carries = [carry] if isinstance(carry, tuple) else list(carry)
    comm_args = [arr for _, arrs in carries for arr in arrs]
    n, n_in, n_out, n_scratch = len(comm_args), len(in_specs), len(out_specs), len(scratch_shapes)
    shapes_of = [cls.out_shapes(arrs) for cls, arrs in carries]
    comm_shapes = [s for shapes in shapes_of for s in shapes]
    m = len(comm_shapes)
    comm_sems = [s for cls, arrs in carries for s in cls.sems(len(arrs))]

    def carrying(*refs):
        ins, comm_ins = refs[:n_in], refs[n_in:n_in + n]
        outs, comm_outs = refs[n_in + n:n_in + n + n_out], refs[n_in + n + n_out:n_in + n + n_out + m]
        scratch, sems = refs[n_in + n + n_out + m:n_in + n + n_out + m + n_scratch], refs[n_in + n + n_out + m + n_scratch:]
        steps = [pl.program_id(d) for d in range(len(grid))]
        first = functools.reduce(jnp.logical_and, [s == 0 for s in steps])
        last = functools.reduce(jnp.logical_and, [s == g - 1 for s, g in zip(steps, grid)])

        def exchanges():
            at_in = at_out = 0
            for k, (cls, arrs) in enumerate(carries):
                yield cls(comm_ins[at_in:at_in + len(arrs)], comm_outs[at_out:at_out + len(shapes_of[k])],
                          sems[2 * k], sems[2 * k + 1])
                at_in, at_out = at_in + len(arrs), at_out + len(shapes_of[k])

        @pl.when(first)
        def _():
            for exchange in exchanges():
                exchange.start()

        linear, total = 0, 1
        for s, g in zip(steps, grid):
            linear, total = linear * g + s, total * g
        for k, (cls, _) in enumerate(carries):
            for phase, (num, den) in getattr(cls, "PHASES_AT", {}).items():
                @pl.when(linear == (num * total) // den)
                def _(k=k, phase=phase):
                    getattr(list(exchanges())[k], phase)()

        body(*ins, *outs, *scratch)

        @pl.when(last)
        def _():
            for exchange in exchanges():
                exchange.finish()

    res = pl.pallas_call(
        carrying, name=name, grid=grid, in_specs=list(in_specs) + _any_specs(n), out_specs=list(out_specs) + _any_specs(m),
        out_shape=list(out_shape) + comm_shapes, scratch_shapes=list(scratch_shapes) + comm_sems,
        compiler_params=_params(semantics))(*args, *comm_args)
    comm_res, at = [], n_out
    for (_, arrs), shapes in zip(carries, shapes_of):
        comm_res.append(res[at:at + len(arrs)])
        at += len(shapes)
    return res[:n_out], (comm_res[0] if isinstance(carry, tuple) else comm_res)


def proj_fwd(x, g_mix, w_in, carry=None):
    t_len = x.shape[0]
    tm = min(1024, t_len)
    tn = D_IN // N_CHIPS

    def body(x_ref, g_ref, w_ref, proj_ref, h_ref):
        @pl.when(pl.program_id(1) == 0)
        def _():
            xv = x_ref[...]
            r = lax.rsqrt(jnp.mean(xv * xv, axis=-1, keepdims=True) + EPS)
            h_ref[...] = (xv * r * g_ref[...]).astype(BF16)

        proj_ref[...] = _dot(h_ref[...], w_ref[...]).astype(BF16)

    return _carried_call(
        body, "proj_fwd", (t_len // tm, N_CHIPS),
        in_specs=[pl.BlockSpec((tm, D), lambda i, j: (i, 0)),
                  pl.BlockSpec((1, D), lambda i, j: (0, 0)),
                  pl.BlockSpec((D, tn), lambda i, j: (0, j))],
        out_specs=[pl.BlockSpec((tm, tn), lambda i, j: (i, j)),
                   pl.BlockSpec((tm, D), lambda i, j: (i, 0))],
        out_shape=[jax.ShapeDtypeStruct((t_len, D_IN), BF16), jax.ShapeDtypeStruct((t_len, D), BF16)],
        scratch_shapes=[], semantics=("arbitrary", "arbitrary"), args=(x, g_mix, w_in), carry=carry)


def kv_fwd(mem, g_mem, w_kv4):
    half = D // 2

    def body(mem_ref, g_ref, w_ref, memn_ref, k_ref, v_ref):
        mv = mem_ref[...]
        r = lax.rsqrt(jnp.mean(mv * mv, axis=-1, keepdims=True) + EPS)
        mn = (mv * r * g_ref[...]).astype(BF16)
        memn_ref[...] = mn
        k_ref[:, 0:half] = _dot(mn, w_ref[0]).astype(BF16)
        k_ref[:, half:D] = _dot(mn, w_ref[1]).astype(BF16)
        v_ref[:, 0:half] = _dot(mn, w_ref[2]).astype(BF16)
        v_ref[:, half:D] = _dot(mn, w_ref[3]).astype(BF16)

    out = jax.ShapeDtypeStruct((N_MEM, D), BF16)
    return pl.pallas_call(body, name="kv_fwd", out_shape=[out, out, out], compiler_params=_params())(mem, g_mem, w_kv4)


def _softmax_rows(s):
    m = jnp.max(s, axis=-1, keepdims=True)
    e = jnp.exp(s - m)
    return e * pl.reciprocal(jnp.sum(e, axis=-1, keepdims=True), approx=True)


def _window_bands(tm, causal):
    t = lax.broadcasted_iota(jnp.int32, (tm, tm + POOL_PAD), 0)
    s = lax.broadcasted_iota(jnp.int32, (tm, tm + POOL_PAD), 1)
    d = (t + POOL_PAD - s) if causal else (s - t)
    return jnp.stack([((d >= 0) & (d < w)).astype(BF16) for w in POOL_WINDOWS])


def mixer_fwd(proj, x, conv_w8, w_co, w_pool, pool_scale, k, v, w_xo, w_out, carry=None):
    t_len = x.shape[0]
    tm = min(256, t_len)

    def body(proj_ref, x_ref, cw_ref, wco_ref, wpool_ref, ps_ref, k_ref, v_ref, wxo_ref, wout_ref,
             a_ref, pooled_ref, ya_ref, pp_ref, yx_ref, o_ref, p_ref, x1_ref, cu_ext, up_ext):
        i = pl.program_id(0)

        @pl.when(i == 0)
        def _():
            cu_ext[0:HALO, :] = jnp.zeros((HALO, D), F32)
            up_ext[0:HALO, :] = jnp.zeros((HALO, D), F32)

        cu = proj_ref[:, O_CA:O_CA + D].astype(F32) * proj_ref[:, O_UA:O_UA + D].astype(F32)
        cu_ext[HALO:HALO + tm, :] = cu
        conv = (cw_ref[2:3, :] * cu + cw_ref[1:2, :] * cu_ext[HALO - 1:HALO - 1 + tm, :]
                + cw_ref[0:1, :] * cu_ext[HALO - 2:HALO - 2 + tm, :])
        a = (proj_ref[:, O_BA:O_BA + D].astype(F32) * conv).astype(BF16)
        a_ref[...] = a
        ya = _dot(a, wco_ref[...])
        ya_ref[...] = ya.astype(BF16)

        up_ext[HALO:HALO + tm, :] = proj_ref[:, O_UP:O_UP + D].astype(F32)
        for g, window in enumerate(POOL_WINDOWS):
            cols = slice(g * GROUP_DIM, (g + 1) * GROUP_DIM)
            tok = up_ext[HALO:HALO + tm, cols]
            acc = tok
            for j in range(1, window):
                acc = acc + up_ext[HALO - j:HALO - j + tm, cols]
            pooled = (acc * _inv_count(i, tm, window) - tok).astype(BF16)
            pooled_ref[:, cols] = pooled
            pp_ref[:, cols] = _dot(pooled, wpool_ref[g]).astype(BF16)

        for hd in range(HEADS):
            cols = slice(hd * HEAD_DIM, (hd + 1) * HEAD_DIM)
            q = proj_ref[:, O_QX + hd * HEAD_DIM:O_QX + (hd + 1) * HEAD_DIM]
            p = _softmax_rows(_dot_nt(q, k_ref[:, cols]) * ATT_SCALE).astype(BF16)
            p_ref[:, hd * N_MEM:(hd + 1) * N_MEM] = p
            o_ref[:, cols] = _dot(p, v_ref[:, cols]).astype(BF16)
        yx = _dot(o_ref[...], wxo_ref[...])
        yx_ref[...] = yx.astype(BF16)

        merged = (_sigmoid(proj_ref[:, O_GA:O_GA + D].astype(F32)) * ya
                  + _sigmoid(proj_ref[:, O_GP:O_GP + D].astype(F32)) * (pp_ref[...].astype(F32) * ps_ref[...])
                  + _sigmoid(proj_ref[:, O_GX:O_GX + D].astype(F32)) * yx)
        x1_ref[...] = x_ref[...] + _dot(merged.astype(BF16), wout_ref[...])

        cu_ext[0:HALO, :] = cu_ext[tm:tm + HALO, :]
        up_ext[0:HALO, :] = up_ext[tm:tm + HALO, :]

    act = jax.ShapeDtypeStruct((t_len, D), BF16)
    return _carried_call(
        body, "mixer_fwd", (t_len // tm,),
        in_specs=[_rows(tm, D_IN), _rows(tm, D), _resident((8, D)), _resident((D, D)),
                  _resident((GROUPS, GROUP_DIM, GROUP_DIM)), _resident((1, D)),
                  _resident((N_MEM, D)), _resident((N_MEM, D)), _resident((D, D)), _resident((D, D))],
        out_specs=[_rows(tm, D)] * 7 + [_rows(tm, D)],
        out_shape=[act] * 6 + [jax.ShapeDtypeStruct((t_len, HEADS * N_MEM), BF16), jax.ShapeDtypeStruct((t_len, D), F32)],
        scratch_shapes=[pltpu.VMEM((tm + HALO, D), F32), pltpu.VMEM((tm + HALO, D), F32)],
        semantics=("arbitrary",), args=(proj, x, conv_w8, w_co, w_pool, pool_scale, k, v, w_xo, w_out), carry=carry)


def ffn_fwd(x1, target, g_ffn, w_gate, w_up, w_down, g_final):
    t_len = x1.shape[0]
    tm = min(512, t_len)

    def body(x1_ref, tgt_ref, g_ref, wg_ref, wu_ref, wd_ref, gf_ref, gate_ref, up_ref, dx2_ref, stat_ref):
        @pl.when(pl.program_id(0) == 0)
        def _():
            stat_ref[...] = jnp.zeros((8, D), F32)

        x1v = x1_ref[...]
        r2 = lax.rsqrt(jnp.mean(x1v * x1v, axis=-1, keepdims=True) + EPS)
        h2 = (x1v * r2 * g_ref[...]).astype(BF16)
        gate = _dot_nt(h2, wg_ref[...])
        up = _dot_nt(h2, wu_ref[...])
        gate_ref[...] = gate.astype(BF16)
        up_ref[...] = up.astype(BF16)
        act = (gate * _sigmoid(gate) * up).astype(BF16)
        x2 = x1v + _dot(act, wd_ref[...])
        r3 = lax.rsqrt(jnp.mean(x2 * x2, axis=-1, keepdims=True) + EPS)
        xh = x2 * r3
        diff = xh * gf_ref[...] - tgt_ref[...]
        dy = diff * (1.0 / D)
        stat_ref[0:1, :] += jnp.sum(dy * xh, axis=0, keepdims=True)
        stat_ref[1:2, :] += (0.5 / D) * jnp.sum(diff * diff, axis=0, keepdims=True)
        dxh = dy * gf_ref[...]
        dx2_ref[...] = r3 * (dxh - xh * jnp.mean(dxh * xh, axis=-1, keepdims=True))

    return pl.pallas_call(
        body, name="ffn_fwd",
        grid=(t_len // tm,),
        in_specs=[_rows(tm, D), _rows(tm, D), _resident((1, D)), _resident((D_FF, D)), _resident((D_FF, D)),
                  _resident((D_FF, D)), _resident((1, D))],
        out_specs=[_rows(tm, D_FF), _rows(tm, D_FF), _rows(tm, D), _const((8, D))],
        out_shape=[jax.ShapeDtypeStruct((t_len, D_FF), BF16), jax.ShapeDtypeStruct((t_len, D_FF), BF16),
                   jax.ShapeDtypeStruct((t_len, D), F32), jax.ShapeDtypeStruct((8, D), F32)],
        compiler_params=_params(("arbitrary",)),
    )(x1, target, g_ffn, w_gate, w_up, w_down, g_final)


def ffn_bwd(dx2, x1, gate, up, g_ffn, w_gate, w_up, w_down):
    t_len = x1.shape[0]
    tm = min(256, t_len)

    def body(dx2_ref, x1_ref, gate_ref, up_ref, g_ref, wg_ref, wu_ref, wd_ref,
             dx1_ref, dgate_ref, dup_ref, act_ref, h2_ref, stat_ref):
        @pl.when(pl.program_id(0) == 0)
        def _():
            stat_ref[...] = jnp.zeros((8, D), F32)

        dx2v = dx2_ref[...]
        gate = gate_ref[...]
        upv = up_ref[...]
        sg = _sigmoid(gate.astype(F32)).astype(BF16)
        silu = gate * sg
        act_ref[...] = silu * upv
        dact = _dot_nt(dx2v.astype(BF16), wd_ref[...]).astype(BF16)
        dup = dact * silu
        dgate = dact * upv * (sg * (1.0 + gate * (1.0 - sg)))
        dup_ref[...] = dup
        dgate_ref[...] = dgate
        dh2 = _dot(dgate, wg_ref[...]) + _dot(dup, wu_ref[...])
        x1v = x1_ref[...]
        r2 = lax.rsqrt(jnp.mean(x1v * x1v, axis=-1, keepdims=True) + EPS)
        xh = x1v * r2
        h2_ref[...] = (xh * g_ref[...]).astype(BF16)
        stat_ref[0:1, :] += jnp.sum(dh2 * xh, axis=0, keepdims=True)
        dxh = dh2 * g_ref[...]
        dx1_ref[...] = dx2v + r2 * (dxh - xh * jnp.mean(dxh * xh, axis=-1, keepdims=True))

    ff = jax.ShapeDtypeStruct((t_len, D_FF), BF16)
    return pl.pallas_call(
        body, name="ffn_bwd",
        grid=(t_len // tm,),
        in_specs=[_rows(tm, D), _rows(tm, D), _rows(tm, D_FF), _rows(tm, D_FF), _resident((1, D)),
                  _resident((D_FF, D)), _resident((D_FF, D)), _resident((D_FF, D))],
        out_specs=[_rows(tm, D), _rows(tm, D_FF), _rows(tm, D_FF), _rows(tm, D_FF), _rows(tm, D), _const((8, D))],
        out_shape=[jax.ShapeDtypeStruct((t_len, D), F32), ff, ff, ff, jax.ShapeDtypeStruct((t_len, D), BF16),
                   jax.ShapeDtypeStruct((8, D), F32)],
        compiler_params=_params(("arbitrary",)),
    )(dx2, x1, gate, up, g_ffn, w_gate, w_up, w_down)


def mixer_bwd(dx1, proj, ya, pp, yx, probs, conv_w8, w_co, w_pool, pool_scale, k, v, w_xo, w_out, carry=None):
    t_len = dx1.shape[0]
    tm = min(256, t_len)
    n_tiles = t_len // tm
    halo_blocks = tm // HALO

    def body(dx1_ref, proj_ref, halo_ref, ya_ref, pp_ref, yx_ref, p_ref,
             cw_ref, wco_ref, wpool_ref, ps_ref, k_ref, v_ref, wxo_ref, wout_ref, band_ref,
             dproj_ref, merged_ref, dya_ref, dpp_ref, dyx_ref, dk_ref, dv_ref, stat_ref,
             cu_ext, dconv_ext, dpn_ext):
        step = pl.program_id(0)
        tile = n_tiles - 1 - step

        @pl.when(step == 0)
        def _():
            dk_ref[...] = jnp.zeros((N_MEM, D), F32)
            dv_ref[...] = jnp.zeros((N_MEM, D), F32)
            stat_ref[...] = jnp.zeros((8, D), F32)
            dconv_ext[tm:tm + HALO, :] = jnp.zeros((HALO, D), F32)
            dpn_ext[tm:tm + POOL_PAD, :] = jnp.zeros((POOL_PAD, D), BF16)

        dmerged = _dot_nt(dx1_ref[...].astype(BF16), wout_ref[...]).astype(BF16)
        sa = _sigmoid(proj_ref[:, O_GA:O_GA + D].astype(F32)).astype(BF16)
        sp = _sigmoid(proj_ref[:, O_GP:O_GP + D].astype(F32)).astype(BF16)
        sx = _sigmoid(proj_ref[:, O_GX:O_GX + D].astype(F32)).astype(BF16)
        ya = ya_ref[...]
        ppv = pp_ref[...]
        scale = ps_ref[...].astype(BF16)
        yp = ppv * scale
        yx = yx_ref[...]
        merged_ref[...] = sa * ya + sp * yp + sx * yx
        dproj_ref[:, O_GA:O_GA + D] = dmerged * ya * (sa * (1.0 - sa))
        dproj_ref[:, O_GP:O_GP + D] = dmerged * yp * (sp * (1.0 - sp))
        dproj_ref[:, O_GX:O_GX + D] = dmerged * yx * (sx * (1.0 - sx))
        dya = dmerged * sa
        dyp = dmerged * sp
        dyx = dmerged * sx
        dya_ref[...] = dya
        dyx_ref[...] = dyx
        stat_ref[1:2, :] += jnp.sum(dyp.astype(F32) * ppv.astype(F32), axis=0, keepdims=True)
        dpp = dyp * scale
        dpp_ref[...] = dpp

        da = _dot_nt(dya, wco_ref[...])
        c_a = proj_ref[:, O_CA:O_CA + D].astype(F32)
        u_a = proj_ref[:, O_UA:O_UA + D].astype(F32)
        cu = c_a * u_a
        halo_cu = halo_ref[:, O_CA:O_CA + D].astype(F32) * halo_ref[:, O_UA:O_UA + D].astype(F32)
        cu_ext[0:HALO, :] = jnp.where(tile > 0, halo_cu, 0.0)
        cu_ext[HALO:HALO + tm, :] = cu
        cu1 = cu_ext[HALO - 1:HALO - 1 + tm, :]
        cu2 = cu_ext[HALO - 2:HALO - 2 + tm, :]
        conv = cw_ref[2:3, :] * cu + cw_ref[1:2, :] * cu1 + cw_ref[0:1, :] * cu2
        dproj_ref[:, O_BA:O_BA + D] = (da * conv).astype(BF16)
        dconv = da * proj_ref[:, O_BA:O_BA + D].astype(F32)
        stat_ref[5:6, :] += jnp.sum(dconv * cu2, axis=0, keepdims=True)
        stat_ref[6:7, :] += jnp.sum(dconv * cu1, axis=0, keepdims=True)
        stat_ref[7:8, :] += jnp.sum(dconv * cu, axis=0, keepdims=True)
        dconv_ext[0:tm, :] = dconv
        dcu = (cw_ref[2:3, :] * dconv + cw_ref[1:2, :] * dconv_ext[1:1 + tm, :]
               + cw_ref[0:1, :] * dconv_ext[2:2 + tm, :])
        dproj_ref[:, O_CA:O_CA + D] = (dcu * u_a).astype(BF16)
        dproj_ref[:, O_UA:O_UA + D] = (dcu * c_a).astype(BF16)

        for g, window in enumerate(POOL_WINDOWS):
            cols = slice(g * GROUP_DIM, (g + 1) * GROUP_DIM)
            dpooled = _dot_nt(dpp[:, cols], wpool_ref[g])
            dpn_ext[0:tm, cols] = (dpooled * _inv_count(tile, tm, window)).astype(BF16)
            acc = _dot(band_ref[g], dpn_ext[:, cols])
            dproj_ref[:, O_UP + g * GROUP_DIM:O_UP + (g + 1) * GROUP_DIM] = (acc - dpooled).astype(BF16)

        do = _dot_nt(dyx, wxo_ref[...])
        for hd in range(HEADS):
            cols = slice(hd * HEAD_DIM, (hd + 1) * HEAD_DIM)
            q = proj_ref[:, O_QX + hd * HEAD_DIM:O_QX + (hd + 1) * HEAD_DIM]
            kh = k_ref[:, cols]
            p16 = p_ref[:, hd * N_MEM:(hd + 1) * N_MEM]
            p = p16.astype(F32)
            doh = do[:, cols].astype(BF16)
            dp = _dot_nt(doh, v_ref[:, cols])
            dv_ref[:, cols] += _dot_tn(p16, doh)
            ds = (p * (dp - jnp.sum(dp * p, axis=-1, keepdims=True)) * ATT_SCALE).astype(BF16)
            dproj_ref[:, O_QX + hd * HEAD_DIM:O_QX + (hd + 1) * HEAD_DIM] = _dot(ds, kh).astype(BF16)
            dk_ref[:, cols] += _dot_tn(ds, q)

        dconv_ext[tm:tm + HALO, :] = dconv_ext[0:HALO, :]
        dpn_ext[tm:tm + HALO, :] = dpn_ext[0:HALO, :]

    def rev(width):
        return pl.BlockSpec((tm, width), lambda s: (n_tiles - 1 - s, 0))

    halo_spec = pl.BlockSpec((HALO, D_IN), lambda s: (jnp.maximum((n_tiles - 1 - s) * halo_blocks - 1, 0), 0))
    act = jax.ShapeDtypeStruct((t_len, D), BF16)
    kv_grad = jax.ShapeDtypeStruct((N_MEM, D), F32)
    return _carried_call(
        body, "mixer_bwd", (n_tiles,),
        in_specs=[rev(D), rev(D_IN), halo_spec, rev(D), rev(D), rev(D), rev(D),
                  _resident((8, D)), _resident((D, D)), _resident((GROUPS, GROUP_DIM, GROUP_DIM)), _resident((1, D)),
                  _resident((N_MEM, D)), _resident((N_MEM, D)), _resident((D, D)), _resident((D, D)),
                  _resident((GROUPS, tm, tm + POOL_PAD))],
        out_specs=[rev(D_IN), rev(D), rev(D), rev(D), rev(D),
                   _const((N_MEM, D)), _const((N_MEM, D)), _const((8, D))],
        out_shape=[jax.ShapeDtypeStruct((t_len, D_IN), BF16), act, act, act, act, kv_grad, kv_grad,
                   jax.ShapeDtypeStruct((8, D), F32)],
        scratch_shapes=[pltpu.VMEM((tm + HALO, D), F32)] * 2 + [pltpu.VMEM((tm + POOL_PAD, D), BF16)],
        semantics=("arbitrary",),
        args=(dx1, proj, proj, ya, pp, yx, probs, conv_w8, w_co, w_pool, pool_scale, k, v, w_xo, w_out,
              _window_bands(tm, False)), carry=carry)


def in_bwd(dproj, w_in, x, dx1, g_mix, carry=None):
    t_len = x.shape[0]
    tm = min(512, t_len)

    def body(dproj_ref, w_ref, x_ref, dx1_ref, g_ref, gx_ref, stat_ref):
        @pl.when(pl.program_id(0) == 0)
        def _():
            stat_ref[...] = jnp.zeros((8, D), F32)

        dh = _dot_nt(dproj_ref[...], w_ref[...])
        xv = x_ref[...]
        r = lax.rsqrt(jnp.mean(xv * xv, axis=-1, keepdims=True) + EPS)
        xh = xv * r
        stat_ref[0:1, :] += jnp.sum(dh * xh, axis=0, keepdims=True)
        dxh = dh * g_ref[...]
        gx_ref[...] = dx1_ref[...] + r * (dxh - xh * jnp.mean(dxh * xh, axis=-1, keepdims=True))

    return _carried_call(
        body, "in_bwd", (t_len // tm,),
        in_specs=[_rows(tm, D_IN), _resident((D, D_IN)), _rows(tm, D), _rows(tm, D), _resident((1, D))],
        out_specs=[_rows(tm, D), _const((8, D))],
        out_shape=[jax.ShapeDtypeStruct((t_len, D), F32), jax.ShapeDtypeStruct((8, D), F32)],
        scratch_shapes=[], semantics=("arbitrary",), args=(dproj, w_in, x, dx1, g_mix), carry=carry)


def kv_bwd(dk, dv, memn, mem, g_mem, w_kv4):
    half = D // 2

    def body(dk_ref, dv_ref, memn_ref, mem_ref, g_ref, w_ref, gw_ref, gw16_ref, stat_ref):
        mn = memn_ref[...]
        parts = (dk_ref[:, 0:half], dk_ref[:, half:D], dv_ref[:, 0:half], dv_ref[:, half:D])
        dmemn = jnp.zeros((N_MEM, D), F32)
        for j, part in enumerate(parts):
            part = part.astype(BF16)
            gw = _dot_tn(mn, part)
            gw_ref[j] = gw
            gw16_ref[j] = gw.astype(BF16)
            dmemn = dmemn + _dot_nt(part, w_ref[j])
        mv = mem_ref[...]
        r = lax.rsqrt(jnp.mean(mv * mv, axis=-1, keepdims=True) + EPS)
        stat_ref[...] = jnp.zeros((8, D), F32)
        stat_ref[0:1, :] = jnp.sum(dmemn * (mv * r), axis=0, keepdims=True)

    return pl.pallas_call(
        body, name="kv_bwd",
        out_shape=[jax.ShapeDtypeStruct((N_CHIPS, D, half), F32), jax.ShapeDtypeStruct((N_CHIPS, D, half), BF16),
                   jax.ShapeDtypeStruct((8, D), F32)],
        compiler_params=_params(),
    )(dk, dv, memn, mem, g_mem, w_kv4)


def matmul_tn(name, a, b, tn, col_blocks=1, carry=None):
    t_len, k_dim = a.shape
    n_dim = b.shape[1]
    tt = min(1024, t_len)
    per_block = n_dim // col_blocks // tn

    def body(a_ref, b_ref, out_ref, out16_ref):
        @pl.when(pl.program_id(1) == 0)
        def _():
            out_ref[...] = jnp.zeros((k_dim, tn), F32)

        out_ref[...] += _dot_tn(a_ref[...].astype(BF16), b_ref[...].astype(BF16))

        @pl.when(pl.program_id(1) == t_len // tt - 1)
        def _():
            out16_ref[...] = out_ref[...].astype(BF16)

    if col_blocks == 1:
        out_spec = pl.BlockSpec((k_dim, tn), lambda n, t: (0, n))
        shape = (k_dim, n_dim)
    else:
        out_spec = pl.BlockSpec((None, k_dim, tn), lambda n, t: (n // per_block, 0, n % per_block))
        shape = (col_blocks, k_dim, n_dim // col_blocks)
    outs, carried = _carried_call(
        body, name, (n_dim // tn, t_len // tt),
        in_specs=[pl.BlockSpec((tt, k_dim), lambda n, t: (t, 0)), pl.BlockSpec((tt, tn), lambda n, t: (t, n))],
        out_specs=[out_spec, out_spec], out_shape=[jax.ShapeDtypeStruct(shape, F32), jax.ShapeDtypeStruct(shape, BF16)],
        scratch_shapes=[], semantics=("arbitrary", "arbitrary"), args=(a, b), carry=carry)
    return (tuple(outs), carried) if carry is not None else tuple(outs)


def grad_w_pool(pooled, dpp):
    t_len = pooled.shape[0]
    tt = min(1024, t_len)
    steps = t_len // tt

    def body(a_ref, b_ref, out_ref, out16_ref):
        @pl.when(pl.program_id(0) == 0)
        def _():
            out_ref[...] = jnp.zeros((GROUPS, GROUP_DIM, GROUP_DIM), F32)

        for g in range(GROUPS):
            cols = slice(g * GROUP_DIM, (g + 1) * GROUP_DIM)
            out_ref[g] += _dot_tn(a_ref[:, cols], b_ref[:, cols])

        @pl.when(pl.program_id(0) == steps - 1)
        def _():
            out16_ref[...] = out_ref[...].astype(BF16)

    shape = (GROUPS, GROUP_DIM, GROUP_DIM)
    return pl.pallas_call(
        body, name="grad_w_pool", grid=(steps,),
        in_specs=[_rows(tt, D), _rows(tt, D)], out_specs=[_const(shape), _const(shape)],
        out_shape=[jax.ShapeDtypeStruct(shape, F32), jax.ShapeDtypeStruct(shape, BF16)],
        compiler_params=_params(("arbitrary",)),
    )(pooled, dpp)


def _place():
    x, y, c = lax.axis_index("x"), lax.axis_index("y"), lax.axis_index("c")
    return x, y, c


def _other_chips(x, y):
    return [(1 - x, y), (x, 1 - y), (1 - x, 1 - y)]


def _any_specs(n):
    return [pl.BlockSpec(memory_space=pl.ANY)] * n


def exchange_call(name, exchange_cls, arrays):
    n = len(arrays)
    shapes = exchange_cls.out_shapes(arrays)

    def body(*refs):
        exchange = exchange_cls(refs[:n], refs[n:n + len(shapes)], *refs[n + len(shapes):])
        exchange.start()
        for phase in getattr(exchange_cls, "PHASES_AT", {}):
            getattr(exchange, phase)()
        exchange.finish()

    res = pl.pallas_call(
        body, name=name, in_specs=_any_specs(n), out_specs=_any_specs(len(shapes)),
        out_shape=shapes, scratch_shapes=exchange_cls.sems(n),
    )(*arrays)
    return res[:n]


class _Gather:
    SLOTS = 8
    ROW_ALIGN = 16
    PHASES_AT = {"middle": (4, 8), "late": (7, 8)}

    @staticmethod
    def out_shapes(shards):
        return [jax.ShapeDtypeStruct((N_CHIPS,) + s.shape, s.dtype) for s in shards]

    @staticmethod
    def sems(n):
        return [pltpu.SemaphoreType.DMA((n, _Gather.SLOTS)), pltpu.SemaphoreType.DMA((n, _Gather.SLOTS))]

    def __init__(self, ins, outs, send_sems, recv_sems):
        self.ins, self.outs, self.send_sems, self.recv_sems = ins, outs, send_sems, recv_sems
        x, y, c = _place()
        self.c, self.me, self.sibling = c, 2 * x + y, (x, y, 1 - c)
        self.across = [(1 - x, y), (x, 1 - y), (1 - x, 1 - y)]

    def _rows(self, a, which, part=None):
        half = self.ins[a].shape[1] // 2
        first = (half // 2) // self.ROW_ALIGN * self.ROW_ALIGN
        if part is None:
            return pl.ds(which * half, half)
        return pl.ds(which * half, first) if part == 0 else pl.ds(which * half + first, half - first)

    def _has_part(self, a, part):
        half = self.ins[a].shape[1] // 2
        return part == 1 or (half // 2) // self.ROW_ALIGN > 0

    def _block(self, a, chip, rows=slice(None)):
        index = chip if not isinstance(chip, tuple) else 2 * chip[0] + chip[1]
        if len(self.outs[a].shape) == len(self.ins[a].shape):
            cols = self.ins[a].shape[2]
            return self.outs[a].at[:, rows, pl.ds(pl.multiple_of(index * cols, cols), cols)]
        return self.outs[a].at[index, :, rows, :]

    def _remote(self, src, dst, a, slot, to):
        return pltpu.make_async_remote_copy(src_ref=src, dst_ref=dst, send_sem=self.send_sems.at[a, slot],
                                            recv_sem=self.recv_sems.at[a, slot], device_id=to, device_id_type=MESH)

    def _own(self, a):
        return self._remote(self.ins[a], self._block(a, self.me), a, 6, self.sibling)

    def _sent(self, a, axis):
        rows = self._rows(a, self.c)
        return self._remote(self.ins[a].at[:, rows, :], self._block(a, self.me, rows), a, axis,
                            (*self.across[axis], self.c))

    def _landed(self, a, axis):
        block = self._block(a, self.across[axis], self._rows(a, self.c))
        return self._remote(block, block, a, axis, (*self.across[axis], self.c))

    def _relayed(self, a, part, incoming):
        source = self.across[2] if incoming else self.across[part]
        block = self._block(a, source, self._rows(a, self.c, part))
        return self._remote(block, block, a, (2, 7)[part], (*self.across[1 - part], self.c))

    def _passed_on(self, a, source, which):
        block = self._block(a, self.across[source], self._rows(a, which))
        return self._remote(block, block, a, 3 + source, self.sibling)

    def start(self):
        for a in range(len(self.ins)):
            self._own(a).start()
        for a in range(len(self.ins)):
            for axis in range(2):
                self._sent(a, axis).start()

    def middle(self):
        for a in range(len(self.ins)):
            for axis in range(2):
                self._landed(a, axis).wait_recv()
                if self._has_part(a, axis):
                    self._relayed(a, axis, incoming=False).start()
                self._passed_on(a, axis, self.c).start()

    def late(self):
        for a in range(len(self.ins)):
            for part in range(2):
                if self._has_part(a, part):
                    self._relayed(a, part, incoming=True).wait_recv()
            self._passed_on(a, 2, self.c).start()

    def finish(self):
        n = len(self.ins)
        for a in range(n):
            for source in range(3):
                self._passed_on(a, source, 1 - self.c).wait_recv()
            self._own(a).wait_recv()
        for a in range(n):
            self._own(a).wait_send()
            for axis in range(2):
                self._sent(a, axis).wait_send()
                if self._has_part(a, axis):
                    self._relayed(a, axis, incoming=False).wait_send()
            for source in range(3):
                self._passed_on(a, source, self.c).wait_send()


class _HalfExchange:
    @staticmethod
    def out_shapes(grads):
        return [jax.ShapeDtypeStruct((N_CHIPS, g.shape[1] // 2, g.shape[2]), g.dtype) for g in grads]

    @staticmethod
    def sems(n):
        return [pltpu.SemaphoreType.DMA((n,)), pltpu.SemaphoreType.DMA((n,))]

    def __init__(self, ins, outs, send_sems, recv_sems):
        self.ins, self.outs, self.send_sems, self.recv_sems = ins, outs, send_sems, recv_sems

    def _copies(self):
        x, y, c = _place()
        for a in range(len(self.ins)):
            h = self.ins[a].shape[1] // 2
            yield pltpu.make_async_remote_copy(
                src_ref=self.ins[a].at[:, pl.ds((1 - c) * h, h), :], dst_ref=self.outs[a],
                send_sem=self.send_sems.at[a], recv_sem=self.recv_sems.at[a], device_id=(x, y, 1 - c), device_id_type=MESH)

    def start(self):
        for cp in self._copies():
            cp.start()

    def finish(self):
        for cp in self._copies():
            cp.wait()


class _GatherColumns(_Gather):
    @staticmethod
    def out_shapes(shards):
        return [jax.ShapeDtypeStruct(s.shape[:2] + (N_CHIPS * s.shape[2],), s.dtype) for s in shards]


def _row_tile(rows, cols, budget=2 << 20):
    best = 16
    for tr in range(16, rows + 1, 16):
        if rows % tr == 0 and tr * cols * 4 <= budget:
            best = tr
    return best


def add_sibling_half(name, grad, got, place):
    _, rows, cols = grad.shape
    h = rows // 2
    tr = _row_tile(h, cols)
    per_half = h // tr

    def body(place_ref, g_ref, o_ref, own_ref, out16_ref):
        total = g_ref[...] + o_ref[...].astype(F32)
        out16_ref[...] = total.astype(BF16)

        @pl.when(pl.program_id(1) == place_ref[1])
        def _():
            own_ref[...] = total

    return pl.pallas_call(
        body, name=name,
        grid_spec=pltpu.PrefetchScalarGridSpec(
            num_scalar_prefetch=1, grid=(per_half, N_CHIPS),
            in_specs=[pl.BlockSpec((None, tr, cols), lambda r, j, pr: (j, pr[0] * per_half + r, 0)),
                      pl.BlockSpec((None, tr, cols), lambda r, j, pr: (j, r, 0))],
            out_specs=[pl.BlockSpec((tr, cols), lambda r, j, pr: (r, 0)),
                       pl.BlockSpec((None, tr, cols), lambda r, j, pr: (j, r, 0))]),
        out_shape=[jax.ShapeDtypeStruct((h, cols), F32), jax.ShapeDtypeStruct((N_CHIPS, h, cols), BF16)],
        compiler_params=_params(("arbitrary", "arbitrary")),
    )(place, grad, got)


class _ChipExchange:
    SLOTS = 6
    ROW_ALIGN = 16
    PHASES_AT = {"middle": (2, 8)}

    @staticmethod
    def out_shapes(partials):
        return ([jax.ShapeDtypeStruct((3,) + p.shape[1:], p.dtype) for p in partials]
                + [jax.ShapeDtypeStruct(p.shape[1:], p.dtype) for p in partials])

    @staticmethod
    def sems(n):
        return [pltpu.SemaphoreType.DMA((n, _ChipExchange.SLOTS)), pltpu.SemaphoreType.DMA((n, _ChipExchange.SLOTS))]

    def __init__(self, ins, outs, send_sems, recv_sems):
        n = len(ins)
        self.ins, self.outs, self.relays, self.send_sems, self.recv_sems = ins, outs[:n], outs[n:], send_sems, recv_sems

    def _copy(self, a, slot):
        x, y, c = _place()
        across = _other_chips(x, y)
        rows = self.ins[a].shape[1]
        first = (rows // 2) // self.ROW_ALIGN * self.ROW_ALIGN
        part = (pl.ds(0, first), pl.ds(first, rows - first))
        if slot < 2:
            px, py = across[slot]
            src, dst, to = self.ins[a].at[2 * px + py], self.outs[a].at[slot], across[slot]
        elif slot < 4:
            px, py = across[2]
            src, dst, to = self.ins[a].at[2 * px + py, part[slot - 2], :], self.relays[a].at[part[slot - 2], :], across[slot - 2]
        else:
            src, dst, to = self.relays[a].at[part[slot - 4], :], self.outs[a].at[2, part[slot - 4], :], across[5 - slot]
        return pltpu.make_async_remote_copy(src_ref=src, dst_ref=dst, send_sem=self.send_sems.at[a, slot],
                                            recv_sem=self.recv_sems.at[a, slot], device_id=(*to, c), device_id_type=MESH)

    def start(self):
        for slot in (2, 3, 0, 1):
            for a in range(len(self.ins)):
                self._copy(a, slot).start()

    def middle(self):
        for part in range(2):
            for a in range(len(self.ins)):
                self._copy(a, 2 + part).wait_recv()
                self._copy(a, 4 + part).start()

    def finish(self):
        for a in range(len(self.ins)):
            for slot in (0, 1, 4, 5):
                self._copy(a, slot).wait_recv()
        for a in range(len(self.ins)):
            for slot in range(self.SLOTS):
                self._copy(a, slot).wait_send()


def add_chip_blocks(name, own, got):
    h, cols = own.shape
    tr = _row_tile(h, cols)

    def body(p_ref, g0_ref, g1_ref, g2_ref, out_ref):
        out_ref[...] = ((p_ref[...] + g0_ref[...].astype(F32)) + g1_ref[...].astype(F32)) + g2_ref[...].astype(F32)

    def got_spec(slot):
        return pl.BlockSpec((None, tr, cols), lambda r: (slot, r, 0))

    return pl.pallas_call(
        body, name=name, grid=(h // tr,),
        in_specs=[_rows(tr, cols), got_spec(0), got_spec(1), got_spec(2)], out_specs=_rows(tr, cols),
        out_shape=jax.ShapeDtypeStruct((h, cols), F32),
        compiler_params=_params(("arbitrary",)),
    )(own, got, got, got)


class _SiblingSwap:
    @staticmethod
    def out_shapes(halves):
        return [jax.ShapeDtypeStruct(v.shape, v.dtype) for v in halves]

    @staticmethod
    def sems(n):
        return [pltpu.SemaphoreType.DMA((n,)), pltpu.SemaphoreType.DMA((n,))]

    def __init__(self, ins, outs, send_sems, recv_sems):
        self.ins, self.outs, self.send_sems, self.recv_sems = ins, outs, send_sems, recv_sems

    def _copies(self):
        x, y, c = _place()
        for a in range(len(self.ins)):
            yield pltpu.make_async_remote_copy(
                src_ref=self.ins[a], dst_ref=self.outs[a], send_sem=self.send_sems.at[a], recv_sem=self.recv_sems.at[a],
                device_id=(x, y, 1 - c), device_id_type=MESH)

    def start(self):
        for cp in self._copies():
            cp.start()

    def finish(self):
        for cp in self._copies():
            cp.wait()


def all_reduce_small(pack):
    rows = pack.shape[0]

    def body(pack_ref, out_ref, gathered, send_sems, recv_sems):
        x, y, c = _place()
        me = 4 * x + 2 * y + c
        gathered[me] = pack_ref[...]
        copies = []
        for rel in range(1, 8):
            fx, fy, fc = (rel >> 2) & 1, (rel >> 1) & 1, rel & 1
            peer = (x ^ fx, y ^ fy, c ^ fc)
            cp = pltpu.make_async_remote_copy(
                src_ref=pack_ref, dst_ref=gathered.at[me], send_sem=send_sems.at[rel - 1], recv_sem=recv_sems.at[rel - 1],
                device_id=peer, device_id_type=MESH)
            cp.start()
            copies.append(cp)
        for rel in range(1, 8):
            fx, fy, fc = (rel >> 2) & 1, (rel >> 1) & 1, rel & 1
            src = 4 * (x ^ fx) + 2 * (y ^ fy) + (c ^ fc)
            pltpu.make_async_remote_copy(
                src_ref=pack_ref, dst_ref=gathered.at[src], send_sem=send_sems.at[rel - 1], recv_sem=recv_sems.at[rel - 1],
                device_id=(x, y, c), device_id_type=MESH).wait_recv()
        for cp in copies:
            cp.wait_send()
        total = gathered[0]
        for dev in range(1, 8):
            total = total + gathered[dev]
        out_ref[...] = total

    return pl.pallas_call(
        body, name="all_reduce_small",
        in_specs=[pl.BlockSpec(memory_space=pltpu.VMEM)], out_specs=pl.BlockSpec(memory_space=pltpu.VMEM),
        out_shape=jax.ShapeDtypeStruct((rows, D), F32),
        scratch_shapes=[pltpu.VMEM((8, rows, D), F32), pltpu.SemaphoreType.DMA((7,)), pltpu.SemaphoreType.DMA((7,))],
    )(pack)


def _adamw_update(w, g, m, v):
    nm = ADAM_B1 * m + (1.0 - ADAM_B1) * g
    nv = ADAM_B2 * v + (1.0 - ADAM_B2) * (g * g)
    m_hat = nm / (1.0 - ADAM_B1 ** ADAM_STEP)
    v_hat = nv / (1.0 - ADAM_B2 ** ADAM_STEP)
    delta = -ADAM_LR * (m_hat / (jnp.sqrt(v_hat) + ADAM_EPS) + ADAM_WD * w)
    return delta, nm, nv


def adamw(name, w, g, m, v):
    def body(w_ref, g_ref, m_ref, v_ref, d_ref, nm_ref, nv_ref):
        d_ref[...], nm_ref[...], nv_ref[...] = _adamw_update(w_ref[...], g_ref[...], m_ref[...], v_ref[...])

    out = jax.ShapeDtypeStruct(w.shape, F32)
    return pl.pallas_call(body, name=name, out_shape=[out] * 3, compiler_params=_params())(w, g, m, v)


def adamw_halves(name, w, mine, theirs, m, v, core):
    h, cols = mine.shape
    tr = _row_tile(h, cols)
    per_half = h // tr

    def body(core_ref, w_ref, mine_ref, theirs_ref, m_ref, v_ref, g_ref, d_ref, nm_ref, nv_ref):
        g = jnp.where(pl.program_id(0) == core_ref[0], mine_ref[...], theirs_ref[...])
        g_ref[...] = g
        d_ref[...], nm_ref[...], nv_ref[...] = _adamw_update(w_ref[...], g, m_ref[...], v_ref[...])

    full = pl.BlockSpec((tr, cols), lambda hh, r, cr: (hh * per_half + r, 0))
    mine_spec = pl.BlockSpec((tr, cols), lambda hh, r, cr: (jnp.where(hh == cr[0], r, 0), 0))
    theirs_spec = pl.BlockSpec((tr, cols), lambda hh, r, cr: (jnp.where(hh == cr[0], 0, r), 0))
    out = jax.ShapeDtypeStruct((2 * h, cols), F32)
    return pl.pallas_call(
        body, name=name,
        grid_spec=pltpu.PrefetchScalarGridSpec(
            num_scalar_prefetch=1, grid=(2, per_half),
            in_specs=[full, mine_spec, theirs_spec, full, full], out_specs=[full] * 4),
        out_shape=[out] * 4,
        compiler_params=_params(("arbitrary", "arbitrary")),
    )(core, w, mine, theirs, m, v)


BIG = ("w_in", "w_conv_out", "w_pool", "w_kv", "w_xattn_out", "w_out", "w_gate", "w_up", "w_down")


def kernel(x, mem, norm_mix, w_in, conv_w, w_conv_out, w_pool, pool_scale, norm_mem, w_kv, w_xattn_out, w_out, norm_ffn, w_gate, w_up, w_down, norm_final, loss_target, m_norm_mix, m_w_in, m_conv_w, m_w_conv_out, m_w_pool, m_pool_scale, m_norm_mem, m_w_kv, m_w_xattn_out, m_w_out, m_norm_ffn, m_w_gate, m_w_up, m_w_down, m_norm_final, v_norm_mix, v_w_in, v_conv_w, v_w_conv_out, v_w_pool, v_pool_scale, v_norm_mem, v_w_kv, v_w_xattn_out, v_w_out, v_norm_ffn, v_w_gate, v_w_up, v_w_down, v_norm_final):
    t_len = x.shape[1]
    xi, yi, ci = lax.axis_index("x"), lax.axis_index("y"), lax.axis_index("c")
    chip = 2 * xi + yi
    core_arr = jnp.reshape(ci, (1,)).astype(jnp.int32)
    place_arr = jnp.stack([ci, chip]).astype(jnp.int32)

    conv_pad = jnp.concatenate([conv_w, jnp.zeros((1, 13, 256), F32)], axis=1)
    def t2(w):
        return jnp.swapaxes(w, 1, 2)

    (g_in,) = exchange_call("gather_w_in", _GatherColumns, [w_in.astype(BF16)])
    w_in_f = g_in[0]

    x2d = x[0]
    tgt = loss_target[0]
    (proj, h), (g_kv, g_conv_w, g_co, g_xo, g_out, g_pool, g_gate) = proj_fwd(
        x2d, norm_mix, w_in_f,
        carry=(_Gather, [w_kv.astype(BF16), conv_pad,
                         w_conv_out.astype(BF16), w_xattn_out.astype(BF16), w_out.astype(BF16),
                         w_pool[0].astype(BF16),
                         t2(w_gate).astype(BF16)]))
    w_kv4 = g_kv.reshape(N_CHIPS, D, D // 2)
    conv_full = jnp.transpose(g_conv_w[:, 0, 0:8, :], (1, 0, 2)).reshape(8, D)
    w_co_f, w_xo_f, w_out_f = g_co.reshape(D, D), g_xo.reshape(D, D), g_out.reshape(D, D)
    w_pool_f = jnp.transpose(g_pool, (1, 0, 2, 3)).reshape(GROUPS, GROUP_DIM, GROUP_DIM)
    memn, k, v = kv_fwd(mem[0], norm_mem, w_kv4)
    (a, pooled, ya, pp, yx, o, probs, x1), (g_up, g_down) = mixer_fwd(
        proj, x2d, conv_full, w_co_f, w_pool_f, pool_scale, k, v, w_xo_f, w_out_f,
        carry=(_Gather, [t2(w_up).astype(BF16), w_down.astype(BF16)]))
    w_gate_f, w_up_f, w_down_f = g_gate.reshape(D_FF, D), g_up.reshape(D_FF, D), g_down.reshape(D_FF, D)
    gate, up, dx2, stat_f = ffn_fwd(x1, tgt, norm_ffn, w_gate_f, w_up_f, w_down_f, norm_final.reshape(1, D))

    def by_chip(pair):
        return tuple(gw.reshape(N_CHIPS, gw.shape[0] // N_CHIPS, gw.shape[1]) for gw in pair)

    def chip_partials(names, grads, got):
        return [add_sibling_half("add_sibling_" + n, g32, o_, place_arr) for n, (g32, _), o_ in zip(names, grads, got)]

    def chip_sums(names, partials, got):
        return [add_chip_blocks("add_chips_" + n, own, g2) for n, (own, _), g2 in zip(names, partials, got)]

    dx1, dgate, dup, act, h2, stat_b1 = ffn_bwd(dx2, x1, gate, up, norm_ffn, w_gate_f, w_up_f, w_down_f)
    gw_gate = by_chip(matmul_tn("grad_w_gate", dgate, h2, 512))
    gw_up, got_gate = matmul_tn("grad_w_up", dup, h2, 512, carry=(_HalfExchange, [gw_gate[1]]))
    gw_up = by_chip(gw_up)
    gw_down, got_up = matmul_tn("grad_w_down", act, dx2, 512, carry=(_HalfExchange, [gw_up[1]]))
    gw_down = by_chip(gw_down)
    got_down = exchange_call("exchange_halves_w_down", _HalfExchange, [gw_down[1]])
    names_ffn = ("w_gate", "w_up", "w_down")
    part_ffn = chip_partials(names_ffn, [gw_gate, gw_up, gw_down], list(got_gate) + list(got_up) + list(got_down))

    (dproj, merged, dya, dpp, dyx, dk, dv, stat_b2), got_ffn = mixer_bwd(
        dx1, proj, ya, pp, yx, probs, conv_full, w_co_f, w_pool_f, pool_scale, k, v, w_xo_f, w_out_f,
        carry=(_ChipExchange, [p16 for _, p16 in part_ffn]))
    gw_kv32, gw_kv16, stat_kv = kv_bwd(dk, dv, memn, mem[0], norm_mem, w_kv4)
    gw_pool = tuple(jnp.transpose(gw.reshape(GROUPS, N_CHIPS, 64, GROUP_DIM), (1, 0, 2, 3)).reshape(N_CHIPS, 256, GROUP_DIM)
                    for gw in grad_w_pool(pooled, dpp))
    gw_co, got_kv_pool = matmul_tn("grad_w_conv_out", a, dya, 1024, carry=(_HalfExchange, [gw_kv16, gw_pool[1]]))
    gw_co = by_chip(gw_co)
    gw_xo, got_co = matmul_tn("grad_w_xattn_out", o, dyx, 1024, carry=(_HalfExchange, [gw_co[1]]))
    gw_xo = by_chip(gw_xo)
    gw_out, got_xo = matmul_tn("grad_w_out", merged, dx1, 1024, carry=(_HalfExchange, [gw_xo[1]]))
    gw_out = by_chip(gw_out)
    got_out = exchange_call("exchange_halves_w_out", _HalfExchange, [gw_out[1]])
    names_mix = ("w_kv", "w_pool", "w_conv_out", "w_xattn_out", "w_out")
    part_mix = chip_partials(names_mix, [(gw_kv32, gw_kv16), gw_pool, gw_co, gw_xo, gw_out],
                             list(got_kv_pool) + list(got_co) + list(got_xo) + list(got_out))

    gw_in, got_mix = matmul_tn("grad_w_in", h, dproj, 2048, col_blocks=N_CHIPS,
                               carry=(_ChipExchange, [p16 for _, p16 in part_mix]))
    part_in = chip_partials(("w_in",), [gw_in], exchange_call("exchange_halves_w_in", _HalfExchange, [gw_in[1]]))
    mine_early = chip_sums(names_ffn + names_mix, part_ffn + part_mix, list(got_ffn) + list(got_mix))
    (grad_x, stat_b3), (got_in, theirs_early) = in_bwd(
        dproj, w_in_f, x2d, dx1, norm_mix,
        carry=[(_ChipExchange, [p16 for _, p16 in part_in]), (_SiblingSwap, mine_early)])
    mine_in = chip_sums(("w_in",), part_in, got_in)
    theirs_in = exchange_call("swap_halves_w_in", _SiblingSwap, mine_in)
    reduced = dict(zip(names_ffn + names_mix + ("w_in",),
                       zip(mine_early + mine_in, list(theirs_early) + list(theirs_in))))
    mine = [reduced[n][0] for n in BIG]
    theirs = [reduced[n][1] for n in BIG]

    pack = jnp.concatenate([stat_b3[0:1], stat_b2[1:2], stat_kv[0:1], stat_b1[0:1], stat_f[0:1], stat_b2[5:8],
                            stat_f[1:2], jnp.zeros((7, D), F32)], axis=0)
    total = all_reduce_small(pack)
    loss = jnp.sum(total[8])
    g_conv_full = total[5:8]
    g_conv = lax.dynamic_slice_in_dim(g_conv_full, chip * 256, 256, axis=1)

    given = dict(w_in=(w_in, m_w_in, v_w_in), w_conv_out=(w_conv_out, m_w_conv_out, v_w_conv_out),
                 w_pool=(w_pool, m_w_pool, v_w_pool), w_kv=(w_kv, m_w_kv, v_w_kv),
                 w_xattn_out=(w_xattn_out, m_w_xattn_out, v_w_xattn_out), w_out=(w_out, m_w_out, v_w_out),
                 w_gate=(w_gate, m_w_gate, v_w_gate), w_up=(w_up, m_w_up, v_w_up), w_down=(w_down, m_w_down, v_w_down))
    out_g, out_d, out_m, out_v = {}, {}, {}, {}
    for n, mine_n, theirs_n in zip(BIG, mine, theirs):
        transposed = n in ("w_gate", "w_up")
        rows2d = (2 * mine_n.shape[0], mine_n.shape[1])
        w_, m_, v_ = ((t2(t) if transposed else t).reshape(rows2d) for t in given[n])
        res = adamw_halves("adamw_" + n, w_, mine_n, theirs_n, m_, v_, core_arr)
        if transposed:
            res = [t2(t.reshape(1, D_FF // N_CHIPS, D)) for t in res]
        out_g[n], out_d[n], out_m[n], out_v[n] = (t.reshape(given[n][0].shape) for t in res)

    def small_pack(vals, conv_part):
        conv_rows = jnp.concatenate([conv_part.reshape(3, 256), jnp.zeros((3, D - 256), F32)], axis=1)
        return jnp.concatenate([val.reshape(1, D) for val in vals] + [conv_rows], axis=0)

    sw = small_pack([norm_mix, pool_scale, norm_mem, norm_ffn, norm_final], conv_w)
    sm = small_pack([m_norm_mix, m_pool_scale, m_norm_mem, m_norm_ffn, m_norm_final], m_conv_w)
    sv = small_pack([v_norm_mix, v_pool_scale, v_norm_mem, v_norm_ffn, v_norm_final], v_conv_w)
    sg = small_pack([total[r] for r in range(5)], g_conv)
    sd, snm, snv = adamw("adamw_small", sw, sg, sm, sv)
    small_names = ("norm_mix", "pool_scale", "norm_mem", "norm_ffn", "norm_final")
    small_shapes = dict(norm_mix=(1, D), pool_scale=(1, D), norm_mem=(1, D), norm_ffn=(1, D), norm_final=(D,))
    for r, n in enumerate(small_names):
        out_g[n], out_d[n], out_m[n], out_v[n] = (t[r].reshape(small_shapes[n]) for t in (sg, sd, snm, snv))
    out_g["conv_w"], out_d["conv_w"], out_m["conv_w"], out_v["conv_w"] = (
        t[5:8, 0:256].reshape(1, 3, 256) for t in (sg, sd, snm, snv))

    order = ("norm_mix", "w_in", "conv_w", "w_conv_out", "w_pool", "pool_scale", "norm_mem", "w_kv", "w_xattn_out",
             "w_out", "norm_ffn", "w_gate", "w_up", "w_down", "norm_final")
    return (loss, grad_x.reshape(1, t_len, D), *[out_g[n] for n in order], *[out_d[n] for n in order],
            *[out_m[n] for n in order], *[out_v[n] for n in order])
```

```python
import functools

import jax
import jax.numpy as jnp
from jax import lax
from jax.experimental import pallas as pl
from jax.experimental.pallas import tpu as pltpu

F32 = jnp.float32
BF16 = jnp.bfloat16
MESH = pl.DeviceIdType.MESH

D = 1024
N_MEM = 256
HEADS = 4
HEAD_DIM = 256
GROUPS = 4
GROUP_DIM = 256
POOL_WINDOWS = (2, 4, 8, 16)
D_FF = 2816
D_IN = 8192
N_CHIPS = 4
EPS = 1e-6
HALO = 16
POOL_PAD = 128
ATT_SCALE = HEAD_DIM ** -0.5

ADAM_LR = 0.001
ADAM_B1 = 0.9
ADAM_B2 = 0.999
ADAM_EPS = 1e-08
ADAM_WD = 0.01
ADAM_STEP = 10

VMEM_LIMIT = 56 * 1024 * 1024

O_BA, O_CA, O_UA, O_UP, O_QX, O_GA, O_GP, O_GX = (k * D for k in range(8))

NT_DIMS = (((1,), (1,)), ((), ()))
TN_DIMS = (((0,), (0,)), ((), ()))


def _dot(a, b):
    return jnp.dot(a, b, preferred_element_type=F32)


def _dot_nt(a, b):
    return lax.dot_general(a, b, NT_DIMS, preferred_element_type=F32)


def _dot_tn(a, b):
    return lax.dot_general(a, b, TN_DIMS, preferred_element_type=F32)


def _sigmoid(z):
    return pl.reciprocal(1.0 + jnp.exp(-z), approx=True)


def _params(semantics=None):
    return pltpu.CompilerParams(dimension_semantics=semantics, vmem_limit_bytes=VMEM_LIMIT)


def _resident(shape):
    zeros = (0,) * len(shape)
    return pl.BlockSpec(shape, lambda *_: zeros, pipeline_mode=pl.Buffered(1))


def _const(shape):
    zeros = (0,) * len(shape)
    return pl.BlockSpec(shape, lambda *_: zeros)


def _rows(tm, width):
    return pl.BlockSpec((tm, width), lambda i: (i, 0))


def _inv_count(tile, tm, window):
    t = tile * tm + lax.broadcasted_iota(jnp.int32, (tm, 1), 0)
    return 1.0 / jnp.minimum(t + 1, window).astype(F32)


def _carried_call(body, name, grid, in_specs, out_specs, out_shape, scratch_shapes, semantics, args, carry):
    if carry is None:
        res = pl.pallas_call(body, name=name, grid=grid, in_specs=in_specs, out_specs=out_specs, out_shape=out_shape,
                             scratch_shapes=scratch_shapes, compiler_params=_params(semantics))(*args)
        return res, []
    carries = [carry] if isinstance(carry, tuple) else list(carry)
    comm_args = [arr for _, arrs in carries for arr in arrs]
    n, n_in, n_out, n_scratch = len(comm_args), len(in_specs), len(out_specs), len(scratch_shapes)
    shapes_of = [cls.out_shapes(arrs) for cls, arrs in carries]
    comm_shapes = [s for shapes in shapes_of for s in shapes]
    m = len(comm_shapes)
    comm_sems = [s for cls, arrs in carries for s in cls.sems(len(arrs))]

    def carrying(*refs):
        ins, comm_ins = refs[:n_in], refs[n_in:n_in + n]
        outs, comm_outs = refs[n_in + n:n_in + n + n_out], refs[n_in + n + n_out:n_in + n + n_out + m]
        scratch, sems = refs[n_in + n + n_out + m:n_in + n + n_out + m + n_scratch], refs[n_in + n + n_out + m + n_scratch:]
        steps = [pl.program_id(d) for d in range(len(grid))]
        first = functools.reduce(jnp.logical_and, [s == 0 for s in steps])
        last = functools.reduce(jnp.logical_and, [s == g - 1 for s, g in zip(steps, grid)])

        def exchanges():
            at_in = at_out = 0
            for k, (cls, arrs) in enumerate(carries):
                yield cls(comm_ins[at_in:at_in + len(arrs)], comm_outs[at_out:at_out + len(shapes_of[k])],
                          sems[2 * k], sems[2 * k + 1])
                at_in, at_out = at_in + len(arrs), at_out + len(shapes_of[k])

        @pl.when(first)
        def _():
            for exchange in exchanges():
                exchange.start()

        linear, total = 0, 1
        for s, g in zip(steps, grid):
            linear, total = linear * g + s, total * g
        for k, (cls, _) in enumerate(carries):
            for phase, (num, den) in getattr(cls, "PHASES_AT", {}).items():
                @pl.when(linear == (num * total) // den)
                def _(k=k, phase=phase):
                    getattr(list(exchanges())[k], phase)()

        body(*ins, *outs, *scratch)

        @pl.when(last)
        def _():
            for exchange in exchanges():
                exchange.finish()

    res = pl.pallas_call(
        carrying, name=name, grid=grid, in_specs=list(in_specs) + _any_specs(n), out_specs=list(out_specs) + _any_specs(m),
        out_shape=list(out_shape) + comm_shapes, scratch_shapes=list(scratch_shapes) + comm_sems,
        compiler_params=_params(semantics))(*args, *comm_args)
    comm_res, at = [], n_out
    for (_, arrs), shapes in zip(carries, shapes_of):
        comm_res.append(res[at:at + len(arrs)])
        at += len(shapes)
    return res[:n_out], (comm_res[0] if isinstance(carry, tuple) else comm_res)


def proj_fwd(x, g_mix, w_in, carry=None):
    t_len = x.shape[0]
    tm = min(1024, t_len)
    tn = D_IN // N_CHIPS

    def body(x_ref, g_ref, w_ref, proj_ref, h_ref):
        @pl.when(pl.program_id(1) == 0)
        def _():
            xv = x_ref[...]
            r = lax.rsqrt(jnp.mean(xv * xv, axis=-1, keepdims=True) + EPS)
            h_ref[...] = (xv * r * g_ref[...]).astype(BF16)

        proj_ref[...] = _dot(h_ref[...], w_ref[...]).astype(BF16)

    return _carried_call(
        body, "proj_fwd", (t_len // tm, N_CHIPS),
        in_specs=[pl.BlockSpec((tm, D), lambda i, j: (i, 0)),
                  pl.BlockSpec((1, D), lambda i, j: (0, 0)),
                  pl.BlockSpec((D, tn), lambda i, j: (0, j))],
        out_specs=[pl.BlockSpec((tm, tn), lambda i, j: (i, j)),
                   pl.BlockSpec((tm, D), lambda i, j: (i, 0))],
        out_shape=[jax.ShapeDtypeStruct((t_len, D_IN), BF16), jax.ShapeDtypeStruct((t_len, D), BF16)],
        scratch_shapes=[], semantics=("arbitrary", "arbitrary"), args=(x, g_mix, w_in), carry=carry)


def kv_fwd(mem, g_mem, w_kv4):
    half = D // 2

    def body(mem_ref, g_ref, w_ref, memn_ref, k_ref, v_ref):
        mv = mem_ref[...]
        r = lax.rsqrt(jnp.mean(mv * mv, axis=-1, keepdims=True) + EPS)
        mn = (mv * r * g_ref[...]).astype(BF16)
        memn_ref[...] = mn
        k_ref[:, 0:half] = _dot(mn, w_ref[0]).astype(BF16)
        k_ref[:, half:D] = _dot(mn, w_ref[1]).astype(BF16)
        v_ref[:, 0:half] = _dot(mn, w_ref[2]).astype(BF16)
        v_ref[:, half:D] = _dot(mn, w_ref[3]).astype(BF16)

    out = jax.ShapeDtypeStruct((N_MEM, D), BF16)
    return pl.pallas_call(body, name="kv_fwd", out_shape=[out, out, out], compiler_params=_params())(mem, g_mem, w_kv4)


def _softmax_rows(s):
    m = jnp.max(s, axis=-1, keepdims=True)
    e = jnp.exp(s - m)
    return e * pl.reciprocal(jnp.sum(e, axis=-1, keepdims=True), approx=True)


def _window_bands(tm, causal):
    t = lax.broadcasted_iota(jnp.int32, (tm, tm + POOL_PAD), 0)
    s = lax.broadcasted_iota(jnp.int32, (tm, tm + POOL_PAD), 1)
    d = (t + POOL_PAD - s) if causal else (s - t)
    return jnp.stack([((d >= 0) & (d < w)).astype(BF16) for w in POOL_WINDOWS])


def mixer_fwd(proj, x, conv_w8, w_co, w_pool, pool_scale, k, v, w_xo, w_out, carry=None):
    t_len = x.shape[0]
    tm = min(256, t_len)

    def body(proj_ref, x_ref, cw_ref, wco_ref, wpool_ref, ps_ref, k_ref, v_ref, wxo_ref, wout_ref,
             a_ref, pooled_ref, ya_ref, pp_ref, yx_ref, o_ref, p_ref, x1_ref, cu_ext, up_ext):
        i = pl.program_id(0)

        @pl.when(i == 0)
        def _():
            cu_ext[0:HALO, :] = jnp.zeros((HALO, D), F32)
            up_ext[0:HALO, :] = jnp.zeros((HALO, D), F32)

        cu = proj_ref[:, O_CA:O_CA + D].astype(F32) * proj_ref[:, O_UA:O_UA + D].astype(F32)
        cu_ext[HALO:HALO + tm, :] = cu
        conv = (cw_ref[2:3, :] * cu + cw_ref[1:2, :] * cu_ext[HALO - 1:HALO - 1 + tm, :]
                + cw_ref[0:1, :] * cu_ext[HALO - 2:HALO - 2 + tm, :])
        a = (proj_ref[:, O_BA:O_BA + D].astype(F32) * conv).astype(BF16)
        a_ref[...] = a
        ya = _dot(a, wco_ref[...])
        ya_ref[...] = ya.astype(BF16)

        up_ext[HALO:HALO + tm, :] = proj_ref[:, O_UP:O_UP + D].astype(F32)
        for g, window in enumerate(POOL_WINDOWS):
            cols = slice(g * GROUP_DIM, (g + 1) * GROUP_DIM)
            tok = up_ext[HALO:HALO + tm, cols]
            acc = tok
            for j in range(1, window):
                acc = acc + up_ext[HALO - j:HALO - j + tm, cols]
            pooled = (acc * _inv_count(i, tm, window) - tok).astype(BF16)
            pooled_ref[:, cols] = pooled
            pp_ref[:, cols] = _dot(pooled, wpool_ref[g]).astype(BF16)

        for hd in range(HEADS):
            cols = slice(hd * HEAD_DIM, (hd + 1) * HEAD_DIM)
            q = proj_ref[:, O_QX + hd * HEAD_DIM:O_QX + (hd + 1) * HEAD_DIM]
            p = _softmax_rows(_dot_nt(q, k_ref[:, cols]) * ATT_SCALE).astype(BF16)
            p_ref[:, hd * N_MEM:(hd + 1) * N_MEM] = p
            o_ref[:, cols] = _dot(p, v_ref[:, cols]).astype(BF16)
        yx = _dot(o_ref[...], wxo_ref[...])
        yx_ref[...] = yx.astype(BF16)

        merged = (_sigmoid(proj_ref[:, O_GA:O_GA + D].astype(F32)) * ya
                  + _sigmoid(proj_ref[:, O_GP:O_GP + D].astype(F32)) * (pp_ref[...].astype(F32) * ps_ref[...])
                  + _sigmoid(proj_ref[:, O_GX:O_GX + D].astype(F32)) * yx)
        x1_ref[...] = x_ref[...] + _dot(merged.astype(BF16), wout_ref[...])

        cu_ext[0:HALO, :] = cu_ext[tm:tm + HALO, :]
        up_ext[0:HALO, :] = up_ext[tm:tm + HALO, :]

    act = jax.ShapeDtypeStruct((t_len, D), BF16)
    return _carried_call(
        body, "mixer_fwd", (t_len // tm,),
        in_specs=[_rows(tm, D_IN), _rows(tm, D), _resident((8, D)), _resident((D, D)),
                  _resident((GROUPS, GROUP_DIM, GROUP_DIM)), _resident((1, D)),
                  _resident((N_MEM, D)), _resident((N_MEM, D)), _resident((D, D)), _resident((D, D))],
        out_specs=[_rows(tm, D)] * 7 + [_rows(tm, D)],
        out_shape=[act] * 6 + [jax.ShapeDtypeStruct((t_len, HEADS * N_MEM), BF16), jax.ShapeDtypeStruct((t_len, D), F32)],
        scratch_shapes=[pltpu.VMEM((tm + HALO, D), F32), pltpu.VMEM((tm + HALO, D), F32)],
        semantics=("arbitrary",), args=(proj, x, conv_w8, w_co, w_pool, pool_scale, k, v, w_xo, w_out), carry=carry)


def ffn_fwd(x1, target, g_ffn, w_gate, w_up, w_down, g_final):
    t_len = x1.shape[0]
    tm = min(512, t_len)

    def body(x1_ref, tgt_ref, g_ref, wg_ref, wu_ref, wd_ref, gf_ref, gate_ref, up_ref, dx2_ref, stat_ref):
        @pl.when(pl.program_id(0) == 0)
        def _():
            stat_ref[...] = jnp.zeros((8, D), F32)

        x1v = x1_ref[...]
        r2 = lax.rsqrt(jnp.mean(x1v * x1v, axis=-1, keepdims=True) + EPS)
        h2 = (x1v * r2 * g_ref[...]).astype(BF16)
        gate = _dot_nt(h2, wg_ref[...])
        up = _dot_nt(h2, wu_ref[...])
        gate_ref[...] = gate.astype(BF16)
        up_ref[...] = up.astype(BF16)
        act = (gate * _sigmoid(gate) * up).astype(BF16)
        x2 = x1v + _dot(act, wd_ref[...])
        r3 = lax.rsqrt(jnp.mean(x2 * x2, axis=-1, keepdims=True) + EPS)
        xh = x2 * r3
        diff = xh * gf_ref[...] - tgt_ref[...]
        dy = diff * (1.0 / D)
        stat_ref[0:1, :] += jnp.sum(dy * xh, axis=0, keepdims=True)
        stat_ref[1:2, :] += (0.5 / D) * jnp.sum(diff * diff, axis=0, keepdims=True)
        dxh = dy * gf_ref[...]
        dx2_ref[...] = r3 * (dxh - xh * jnp.mean(dxh * xh, axis=-1, keepdims=True))

    return pl.pallas_call(
        body, name="ffn_fwd",
        grid=(t_len // tm,),
        in_specs=[_rows(tm, D), _rows(tm, D), _resident((1, D)), _resident((D_FF, D)), _resident((D_FF, D)),
                  _resident((D_FF, D)), _resident((1, D))],
        out_specs=[_rows(tm, D_FF), _rows(tm, D_FF), _rows(tm, D), _const((8, D))],
        out_shape=[jax.ShapeDtypeStruct((t_len, D_FF), BF16), jax.ShapeDtypeStruct((t_len, D_FF), BF16),
                   jax.ShapeDtypeStruct((t_len, D), F32), jax.ShapeDtypeStruct((8, D), F32)],
        compiler_params=_params(("arbitrary",)),
    )(x1, target, g_ffn, w_gate, w_up, w_down, g_final)


def ffn_bwd(dx2, x1, gate, up, g_ffn, w_gate, w_up, w_down):
    t_len = x1.shape[0]
    tm = min(256, t_len)

    def body(dx2_ref, x1_ref, gate_ref, up_ref, g_ref, wg_ref, wu_ref, wd_ref,
             dx1_ref, dgate_ref, dup_ref, act_ref, h2_ref, stat_ref):
        @pl.when(pl.program_id(0) == 0)
        def _():
            stat_ref[...] = jnp.zeros((8, D), F32)

        dx2v = dx2_ref[...]
        gate = gate_ref[...]
        upv = up_ref[...]
        sg = _sigmoid(gate.astype(F32)).astype(BF16)
        silu = gate * sg
        act_ref[...] = silu * upv
        dact = _dot_nt(dx2v.astype(BF16), wd_ref[...]).astype(BF16)
        dup = dact * silu
        dgate = dact * upv * (sg * (1.0 + gate * (1.0 - sg)))
        dup_ref[...] = dup
        dgate_ref[...] = dgate
        dh2 = _dot(dgate, wg_ref[...]) + _dot(dup, wu_ref[...])
        x1v = x1_ref[...]
        r2 = lax.rsqrt(jnp.mean(x1v * x1v, axis=-1, keepdims=True) + EPS)
        xh = x1v * r2
        h2_ref[...] = (xh * g_ref[...]).astype(BF16)
        stat_ref[0:1, :] += jnp.sum(dh2 * xh, axis=0, keepdims=True)
        dxh = dh2 * g_ref[...]
        dx1_ref[...] = dx2v + r2 * (dxh - xh * jnp.mean(dxh * xh, axis=-1, keepdims=True))

    ff = jax.ShapeDtypeStruct((t_len, D_FF), BF16)
    return pl.pallas_call(
        body, name="ffn_bwd",
        grid=(t_len // tm,),
        in_specs=[_rows(tm, D), _rows(tm, D), _rows(tm, D_FF), _rows(tm, D_FF), _resident((1, D)),
                  _resident((D_FF, D)), _resident((D_FF, D)), _resident((D_FF, D))],
        out_specs=[_rows(tm, D), _rows(tm, D_FF), _rows(tm, D_FF), _rows(tm, D_FF), _rows(tm, D), _const((8, D))],
        out_shape=[jax.ShapeDtypeStruct((t_len, D), F32), ff, ff, ff, jax.ShapeDtypeStruct((t_len, D), BF16),
                   jax.ShapeDtypeStruct((8, D), F32)],
        compiler_params=_params(("arbitrary",)),
    )(dx2, x1, gate, up, g_ffn, w_gate, w_up, w_down)


def mixer_bwd(dx1, proj, ya, pp, yx, probs, conv_w8, w_co, w_pool, pool_scale, k, v, w_xo, w_out, carry=None):
    t_len = dx1.shape[0]
    tm = min(256, t_len)
    n_tiles = t_len // tm
    halo_blocks = tm // HALO

    def body(dx1_ref, proj_ref, halo_ref, ya_ref, pp_ref, yx_ref, p_ref,
             cw_ref, wco_ref, wpool_ref, ps_ref, k_ref, v_ref, wxo_ref, wout_ref, band_ref,
             dproj_ref, merged_ref, dya_ref, dpp_ref, dyx_ref, dk_ref, dv_ref, stat_ref,
             cu_ext, dconv_ext, dpn_ext):
        step = pl.program_id(0)
        tile = n_tiles - 1 - step

        @pl.when(step == 0)
        def _():
            dk_ref[...] = jnp.zeros((N_MEM, D), F32)
            dv_ref[...] = jnp.zeros((N_MEM, D), F32)
            stat_ref[...] = jnp.zeros((8, D), F32)
            dconv_ext[tm:tm + HALO, :] = jnp.zeros((HALO, D), F32)
            dpn_ext[tm:tm + POOL_PAD, :] = jnp.zeros((POOL_PAD, D), BF16)

        dmerged = _dot_nt(dx1_ref[...].astype(BF16), wout_ref[...]).astype(BF16)
        sa = _sigmoid(proj_ref[:, O_GA:O_GA + D].astype(F32)).astype(BF16)
        sp = _sigmoid(proj_ref[:, O_GP:O_GP + D].astype(F32)).astype(BF16)
        sx = _sigmoid(proj_ref[:, O_GX:O_GX + D].astype(F32)).astype(BF16)
        ya = ya_ref[...]
        ppv = pp_ref[...]
        scale = ps_ref[...].astype(BF16)
        yp = ppv * scale
        yx = yx_ref[...]
        merged_ref[...] = sa * ya + sp * yp + sx * yx
        dproj_ref[:, O_GA:O_GA + D] = dmerged * ya * (sa * (1.0 - sa))
        dproj_ref[:, O_GP:O_GP + D] = dmerged * yp * (sp * (1.0 - sp))
        dproj_ref[:, O_GX:O_GX + D] = dmerged * yx * (sx * (1.0 - sx))
        dya = dmerged * sa
        dyp = dmerged * sp
        dyx = dmerged * sx
        dya_ref[...] = dya
        dyx_ref[...] = dyx
        stat_ref[1:2, :] += jnp.sum(dyp.astype(F32) * ppv.astype(F32), axis=0, keepdims=True)
        dpp = dyp * scale
        dpp_ref[...] = dpp

        da = _dot_nt(dya, wco_ref[...])
        c_a = proj_ref[:, O_CA:O_CA + D].astype(F32)
        u_a = proj_ref[:, O_UA:O_UA + D].astype(F32)
        cu = c_a * u_a
        halo_cu = halo_ref[:, O_CA:O_CA + D].astype(F32) * halo_ref[:, O_UA:O_UA + D].astype(F32)
        cu_ext[0:HALO, :] = jnp.where(tile > 0, halo_cu, 0.0)
        cu_ext[HALO:HALO + tm, :] = cu
        cu1 = cu_ext[HALO - 1:HALO - 1 + tm, :]
        cu2 = cu_ext[HALO - 2:HALO - 2 + tm, :]
        conv = cw_ref[2:3, :] * cu + cw_ref[1:2, :] * cu1 + cw_ref[0:1, :] * cu2
        dproj_ref[:, O_BA:O_BA + D] = (da * conv).astype(BF16)
        dconv = da * proj_ref[:, O_BA:O_BA + D].astype(F32)
        stat_ref[5:6, :] += jnp.sum(dconv * cu2, axis=0, keepdims=True)
        stat_ref[6:7, :] += jnp.sum(dconv * cu1, axis=0, keepdims=True)
        stat_ref[7:8, :] += jnp.sum(dconv * cu, axis=0, keepdims=True)
        dconv_ext[0:tm, :] = dconv
        dcu = (cw_ref[2:3, :] * dconv + cw_ref[1:2, :] * dconv_ext[1:1 + tm, :]
               + cw_ref[0:1, :] * dconv_ext[2:2 + tm, :])
        dproj_ref[:, O_CA:O_CA + D] = (dcu * u_a).astype(BF16)
        dproj_ref[:, O_UA:O_UA + D] = (dcu * c_a).astype(BF16)

        for g, window in enumerate(POOL_WINDOWS):
            cols = slice(g * GROUP_DIM, (g + 1) * GROUP_DIM)
            dpooled = _dot_nt(dpp[:, cols], wpool_ref[g])
            dpn_ext[0:tm, cols] = (dpooled * _inv_count(tile, tm, window)).astype(BF16)
            acc = _dot(band_ref[g], dpn_ext[:, cols])
            dproj_ref[:, O_UP + g * GROUP_DIM:O_UP + (g + 1) * GROUP_DIM] = (acc - dpooled).astype(BF16)

        do = _dot_nt(dyx, wxo_ref[...])
        for hd in range(HEADS):
            cols = slice(hd * HEAD_DIM, (hd + 1) * HEAD_DIM)
            q = proj_ref[:, O_QX + hd * HEAD_DIM:O_QX + (hd + 1) * HEAD_DIM]
            kh = k_ref[:, cols]
            p16 = p_ref[:, hd * N_MEM:(hd + 1) * N_MEM]
            p = p16.astype(F32)
            doh = do[:, cols].astype(BF16)
            dp = _dot_nt(doh, v_ref[:, cols])
            dv_ref[:, cols] += _dot_tn(p16, doh)
            ds = (p * (dp - jnp.sum(dp * p, axis=-1, keepdims=True)) * ATT_SCALE).astype(BF16)
            dproj_ref[:, O_QX + hd * HEAD_DIM:O_QX + (hd + 1) * HEAD_DIM] = _dot(ds, kh).astype(BF16)
            dk_ref[:, cols] += _dot_tn(ds, q)

        dconv_ext[tm:tm + HALO, :] = dconv_ext[0:HALO, :]
        dpn_ext[tm:tm + HALO, :] = dpn_ext[0:HALO, :]

    def rev(width):
        return pl.BlockSpec((tm, width), lambda s: (n_tiles - 1 - s, 0))

    halo_spec = pl.BlockSpec((HALO, D_IN), lambda s: (jnp.maximum((n_tiles - 1 - s) * halo_blocks - 1, 0), 0))
    act = jax.ShapeDtypeStruct((t_len, D), BF16)
    kv_grad = jax.ShapeDtypeStruct((N_MEM, D), F32)
    return _carried_call(
        body, "mixer_bwd", (n_tiles,),
        in_specs=[rev(D), rev(D_IN), halo_spec, rev(D), rev(D), rev(D), rev(D),
                  _resident((8, D)), _resident((D, D)), _resident((GROUPS, GROUP_DIM, GROUP_DIM)), _resident((1, D)),
                  _resident((N_MEM, D)), _resident((N_MEM, D)), _resident((D, D)), _resident((D, D)),
                  _resident((GROUPS, tm, tm + POOL_PAD))],
        out_specs=[rev(D_IN), rev(D), rev(D), rev(D), rev(D),
                   _const((N_MEM, D)), _const((N_MEM, D)), _const((8, D))],
        out_shape=[jax.ShapeDtypeStruct((t_len, D_IN), BF16), act, act, act, act, kv_grad, kv_grad,
                   jax.ShapeDtypeStruct((8, D), F32)],
        scratch_shapes=[pltpu.VMEM((tm + HALO, D), F32)] * 2 + [pltpu.VMEM((tm + POOL_PAD, D), BF16)],
        semantics=("arbitrary",),
        args=(dx1, proj, proj, ya, pp, yx, probs, conv_w8, w_co, w_pool, pool_scale, k, v, w_xo, w_out,
              _window_bands(tm, False)), carry=carry)


def in_bwd(dproj, w_in, x, dx1, g_mix, carry=None):
    t_len = x.shape[0]
    tm = min(512, t_len)

    def body(dproj_ref, w_ref, x_ref, dx1_ref, g_ref, gx_ref, stat_ref):
        @pl.when(pl.program_id(0) == 0)
        def _():
            stat_ref[...] = jnp.zeros((8, D), F32)

        dh = _dot_nt(dproj_ref[...], w_ref[...])
        xv = x_ref[...]
        r = lax.rsqrt(jnp.mean(xv * xv, axis=-1, keepdims=True) + EPS)
        xh = xv * r
        stat_ref[0:1, :] += jnp.sum(dh * xh, axis=0, keepdims=True)
        dxh = dh * g_ref[...]
        gx_ref[...] = dx1_ref[...] + r * (dxh - xh * jnp.mean(dxh * xh, axis=-1, keepdims=True))

    return _carried_call(
        body, "in_bwd", (t_len // tm,),
        in_specs=[_rows(tm, D_IN), _resident((D, D_IN)), _rows(tm, D), _rows(tm, D), _resident((1, D))],
        out_specs=[_rows(tm, D), _const((8, D))],
        out_shape=[jax.ShapeDtypeStruct((t_len, D), F32), jax.ShapeDtypeStruct((8, D), F32)],
        scratch_shapes=[], semantics=("arbitrary",), args=(dproj, w_in, x, dx1, g_mix), carry=carry)


def kv_bwd(dk, dv, memn, mem, g_mem, w_kv4):
    half = D // 2

    def body(dk_ref, dv_ref, memn_ref, mem_ref, g_ref, w_ref, gw_ref, gw16_ref, stat_ref):
        mn = memn_ref[...]
        parts = (dk_ref[:, 0:half], dk_ref[:, half:D], dv_ref[:, 0:half], dv_ref[:, half:D])
        dmemn = jnp.zeros((N_MEM, D), F32)
        for j, part in enumerate(parts):
            part = part.astype(BF16)
            gw = _dot_tn(mn, part)
            gw_ref[j] = gw
            gw16_ref[j] = gw.astype(BF16)
            dmemn = dmemn + _dot_nt(part, w_ref[j])
        mv = mem_ref[...]
        r = lax.rsqrt(jnp.mean(mv * mv, axis=-1, keepdims=True) + EPS)
        stat_ref[...] = jnp.zeros((8, D), F32)
        stat_ref[0:1, :] = jnp.sum(dmemn * (mv * r), axis=0, keepdims=True)

    return pl.pallas_call(
        body, name="kv_bwd",
        out_shape=[jax.ShapeDtypeStruct((N_CHIPS, D, half), F32), jax.ShapeDtypeStruct((N_CHIPS, D, half), BF16),
                   jax.ShapeDtypeStruct((8, D), F32)],
        compiler_params=_params(),
    )(dk, dv, memn, mem, g_mem, w_kv4)


def matmul_tn(name, a, b, tn, col_blocks=1, carry=None):
    t_len, k_dim = a.shape
    n_dim = b.shape[1]
    tt = min(1024, t_len)
    per_block = n_dim // col_blocks // tn

    def body(a_ref, b_ref, out_ref, out16_ref):
        @pl.when(pl.program_id(1) == 0)
        def _():
            out_ref[...] = jnp.zeros((k_dim, tn), F32)

        out_ref[...] += _dot_tn(a_ref[...].astype(BF16), b_ref[...].astype(BF16))

        @pl.when(pl.program_id(1) == t_len // tt - 1)
        def _():
            out16_ref[...] = out_ref[...].astype(BF16)

    if col_blocks == 1:
        out_spec = pl.BlockSpec((k_dim, tn), lambda n, t: (0, n))
        shape = (k_dim, n_dim)
    else:
        out_spec = pl.BlockSpec((None, k_dim, tn), lambda n, t: (n // per_block, 0, n % per_block))
        shape = (col_blocks, k_dim, n_dim // col_blocks)
    outs, carried = _carried_call(
        body, name, (n_dim // tn, t_len // tt),
        in_specs=[pl.BlockSpec((tt, k_dim), lambda n, t: (t, 0)), pl.BlockSpec((tt, tn), lambda n, t: (t, n))],
        out_specs=[out_spec, out_spec], out_shape=[jax.ShapeDtypeStruct(shape, F32), jax.ShapeDtypeStruct(shape, BF16)],
        scratch_shapes=[], semantics=("arbitrary", "arbitrary"), args=(a, b), carry=carry)
    return (tuple(outs), carried) if carry is not None else tuple(outs)


def grad_w_pool(pooled, dpp):
    t_len = pooled.shape[0]
    tt = min(1024, t_len)
    steps = t_len // tt

    def body(a_ref, b_ref, out_ref, out16_ref):
        @pl.when(pl.program_id(0) == 0)
        def _():
            out_ref[...] = jnp.zeros((GROUPS, GROUP_DIM, GROUP_DIM), F32)

        for g in range(GROUPS):
            cols = slice(g * GROUP_DIM, (g + 1) * GROUP_DIM)
            out_ref[g] += _dot_tn(a_ref[:, cols], b_ref[:, cols])

        @pl.when(pl.program_id(0) == steps - 1)
        def _():
            out16_ref[...] = out_ref[...].astype(BF16)

    shape = (GROUPS, GROUP_DIM, GROUP_DIM)
    return pl.pallas_call(
        body, name="grad_w_pool", grid=(steps,),
        in_specs=[_rows(tt, D), _rows(tt, D)], out_specs=[_const(shape), _const(shape)],
        out_shape=[jax.ShapeDtypeStruct(shape, F32), jax.ShapeDtypeStruct(shape, BF16)],
        compiler_params=_params(("arbitrary",)),
    )(pooled, dpp)


def _place():
    x, y, c = lax.axis_index("x"), lax.axis_index("y"), lax.axis_index("c")
    return x, y, c


def _other_chips(x, y):
    return [(1 - x, y), (x, 1 - y), (1 - x, 1 - y)]


def _any_specs(n):
    return [pl.BlockSpec(memory_space=pl.ANY)] * n


def exchange_call(name, exchange_cls, arrays):
    n = len(arrays)
    shapes = exchange_cls.out_shapes(arrays)

    def body(*refs):
        exchange = exchange_cls(refs[:n], refs[n:n + len(shapes)], *refs[n + len(shapes):])
        exchange.start()
        for phase in getattr(exchange_cls, "PHASES_AT", {}):
            getattr(exchange, phase)()
        exchange.finish()

    res = pl.pallas_call(
        body, name=name, in_specs=_any_specs(n), out_specs=_any_specs(len(shapes)),
        out_shape=shapes, scratch_shapes=exchange_cls.sems(n),
    )(*arrays)
    return res[:n]


class _Gather:
    SLOTS = 8
    ROW_ALIGN = 16
    PHASES_AT = {"middle": (4, 8), "late": (6, 8)}

    @staticmethod
    def out_shapes(shards):
        return [jax.ShapeDtypeStruct((N_CHIPS,) + s.shape, s.dtype) for s in shards]

    @staticmethod
    def sems(n):
        return [pltpu.SemaphoreType.DMA((n, _Gather.SLOTS)), pltpu.SemaphoreType.DMA((n, _Gather.SLOTS))]

    def __init__(self, ins, outs, send_sems, recv_sems):
        self.ins, self.outs, self.send_sems, self.recv_sems = ins, outs, send_sems, recv_sems
        x, y, c = _place()
        self.c, self.me, self.sibling = c, 2 * x + y, (x, y, 1 - c)
        self.across = [(1 - x, y), (x, 1 - y), (1 - x, 1 - y)]

    def _rows(self, a, which, part=None):
        half = self.ins[a].shape[1] // 2
        first = (half // 2) // self.ROW_ALIGN * self.ROW_ALIGN
        if part is None:
            return pl.ds(which * half, half)
        return pl.ds(which * half, first) if part == 0 else pl.ds(which * half + first, half - first)

    def _has_part(self, a, part):
        half = self.ins[a].shape[1] // 2
        return part == 1 or (half // 2) // self.ROW_ALIGN > 0

    def _block(self, a, chip, rows=slice(None)):
        index = chip if not isinstance(chip, tuple) else 2 * chip[0] + chip[1]
        if len(self.outs[a].shape) == len(self.ins[a].shape):
            cols = self.ins[a].shape[2]
            return self.outs[a].at[:, rows, pl.ds(pl.multiple_of(index * cols, cols), cols)]
        return self.outs[a].at[index, :, rows, :]

    def _remote(self, src, dst, a, slot, to):
        return pltpu.make_async_remote_copy(src_ref=src, dst_ref=dst, send_sem=self.send_sems.at[a, slot],
                                            recv_sem=self.recv_sems.at[a, slot], device_id=to, device_id_type=MESH)

    def _own(self, a):
        return self._remote(self.ins[a], self._block(a, self.me), a, 6, self.sibling)

    def _sent(self, a, axis):
        rows = self._rows(a, self.c)
        return self._remote(self.ins[a].at[:, rows, :], self._block(a, self.me, rows), a, axis,
                            (*self.across[axis], self.c))

    def _landed(self, a, axis):
        block = self._block(a, self.across[axis], self._rows(a, self.c))
        return self._remote(block, block, a, axis, (*self.across[axis], self.c))

    def _relayed(self, a, part, incoming):
        source = self.across[2] if incoming else self.across[part]
        block = self._block(a, source, self._rows(a, self.c, part))
        return self._remote(block, block, a, (2, 7)[part], (*self.across[1 - part], self.c))

    def _passed_on(self, a, source, which):
        block = self._block(a, self.across[source], self._rows(a, which))
        return self._remote(block, block, a, 3 + source, self.sibling)

    def start(self):
        for a in range(len(self.ins)):
            self._own(a).start()
        for a in range(len(self.ins)):
            for axis in range(2):
                self._sent(a, axis).start()

    def middle(self):
        for a in range(len(self.ins)):
            for axis in range(2):
                self._landed(a, axis).wait_recv()
                if self._has_part(a, axis):
                    self._relayed(a, axis, incoming=False).start()
                self._passed_on(a, axis, self.c).start()

    def late(self):
        for a in range(len(self.ins)):
            for part in range(2):
                if self._has_part(a, part):
                    self._relayed(a, part, incoming=True).wait_recv()
            self._passed_on(a, 2, self.c).start()

    def finish(self):
        n = len(self.ins)
        for a in range(n):
            for source in range(3):
                self._passed_on(a, source, 1 - self.c).wait_recv()
            self._own(a).wait_recv()
        for a in range(n):
            self._own(a).wait_send()
            for axis in range(2):
                self._sent(a, axis).wait_send()
                if self._has_part(a, axis):
                    self._relayed(a, axis, incoming=False).wait_send()
            for source in range(3):
                self._passed_on(a, source, self.c).wait_send()


class _HalfExchange:
    @staticmethod
    def out_shapes(grads):
        return [jax.ShapeDtypeStruct((N_CHIPS, g.shape[1] // 2, g.shape[2]), g.dtype) for g in grads]

    @staticmethod
    def sems(n):
        return [pltpu.SemaphoreType.DMA((n,)), pltpu.SemaphoreType.DMA((n,))]

    def __init__(self, ins, outs, send_sems, recv_sems):
        self.ins, self.outs, self.send_sems, self.recv_sems = ins, outs, send_sems, recv_sems

    def _copies(self):
        x, y, c = _place()
        for a in range(len(self.ins)):
            h = self.ins[a].shape[1] // 2
            yield pltpu.make_async_remote_copy(
                src_ref=self.ins[a].at[:, pl.ds((1 - c) * h, h), :], dst_ref=self.outs[a],
                send_sem=self.send_sems.at[a], recv_sem=self.recv_sems.at[a], device_id=(x, y, 1 - c), device_id_type=MESH)

    def start(self):
        for cp in self._copies():
            cp.start()

    def finish(self):
        for cp in self._copies():
            cp.wait()


class _GatherColumns(_Gather):
    @staticmethod
    def out_shapes(shards):
        return [jax.ShapeDtypeStruct(s.shape[:2] + (N_CHIPS * s.shape[2],), s.dtype) for s in shards]


def _row_tile(rows, cols, budget=2 << 20):
    best = 16
    for tr in range(16, rows + 1, 16):
        if rows % tr == 0 and tr * cols * 4 <= budget:
            best = tr
    return best


def add_sibling_half(name, grad, got, place):
    _, rows, cols = grad.shape
    h = rows // 2
    tr = _row_tile(h, cols)
    per_half = h // tr

    def body(place_ref, g_ref, o_ref, own_ref, out16_ref):
        total = g_ref[...] + o_ref[...].astype(F32)
        out16_ref[...] = total.astype(BF16)

        @pl.when(pl.program_id(1) == place_ref[1])
        def _():
            own_ref[...] = total

    return pl.pallas_call(
        body, name=name,
        grid_spec=pltpu.PrefetchScalarGridSpec(
            num_scalar_prefetch=1, grid=(per_half, N_CHIPS),
            in_specs=[pl.BlockSpec((None, tr, cols), lambda r, j, pr: (j, pr[0] * per_half + r, 0)),
                      pl.BlockSpec((None, tr, cols), lambda r, j, pr: (j, r, 0))],
            out_specs=[pl.BlockSpec((tr, cols), lambda r, j, pr: (r, 0)),
                       pl.BlockSpec((None, tr, cols), lambda r, j, pr: (j, r, 0))]),
        out_shape=[jax.ShapeDtypeStruct((h, cols), F32), jax.ShapeDtypeStruct((N_CHIPS, h, cols), BF16)],
        compiler_params=_params(("arbitrary", "arbitrary")),
    )(place, grad, got)


class _ChipExchange:
    SLOTS = 6
    ROW_ALIGN = 16
    PHASES_AT = {"middle": (3, 8)}

    @staticmethod
    def out_shapes(partials):
        return ([jax.ShapeDtypeStruct((3,) + p.shape[1:], p.dtype) for p in partials]
                + [jax.ShapeDtypeStruct(p.shape[1:], p.dtype) for p in partials])

    @staticmethod
    def sems(n):
        return [pltpu.SemaphoreType.DMA((n, _ChipExchange.SLOTS)), pltpu.SemaphoreType.DMA((n, _ChipExchange.SLOTS))]

    def __init__(self, ins, outs, send_sems, recv_sems):
        n = len(ins)
        self.ins, self.outs, self.relays, self.send_sems, self.recv_sems = ins, outs[:n], outs[n:], send_sems, recv_sems

    def _copy(self, a, slot):
        x, y, c = _place()
        across = _other_chips(x, y)
        rows = self.ins[a].shape[1]
        first = (rows // 2) // self.ROW_ALIGN * self.ROW_ALIGN
        part = (pl.ds(0, first), pl.ds(first, rows - first))
        if slot < 2:
            px, py = across[slot]
            src, dst, to = self.ins[a].at[2 * px + py], self.outs[a].at[slot], across[slot]
        elif slot < 4:
            px, py = across[2]
            src, dst, to = self.ins[a].at[2 * px + py, part[slot - 2], :], self.relays[a].at[part[slot - 2], :], across[slot - 2]
        else:
            src, dst, to = self.relays[a].at[part[slot - 4], :], self.outs[a].at[2, part[slot - 4], :], across[5 - slot]
        return pltpu.make_async_remote_copy(src_ref=src, dst_ref=dst, send_sem=self.send_sems.at[a, slot],
                                            recv_sem=self.recv_sems.at[a, slot], device_id=(*to, c), device_id_type=MESH)

    def start(self):
        for slot in (2, 3, 0, 1):
            for a in range(len(self.ins)):
                self._copy(a, slot).start()

    def middle(self):
        for part in range(2):
            for a in range(len(self.ins)):
                self._copy(a, 2 + part).wait_recv()
                self._copy(a, 4 + part).start()

    def finish(self):
        for a in range(len(self.ins)):
            for slot in (0, 1, 4, 5):
                self._copy(a, slot).wait_recv()
        for a in range(len(self.ins)):
            for slot in range(self.SLOTS):
                self._copy(a, slot).wait_send()


def add_chip_blocks(name, own, got):
    h, cols = own.shape
    tr = _row_tile(h, cols)

    def body(p_ref, g0_ref, g1_ref, g2_ref, out_ref):
        out_ref[...] = ((p_ref[...] + g0_ref[...].astype(F32)) + g1_ref[...].astype(F32)) + g2_ref[...].astype(F32)

    def got_spec(slot):
        return pl.BlockSpec((None, tr, cols), lambda r: (slot, r, 0))

    return pl.pallas_call(
        body, name=name, grid=(h // tr,),
        in_specs=[_rows(tr, cols), got_spec(0), got_spec(1), got_spec(2)], out_specs=_rows(tr, cols),
        out_shape=jax.ShapeDtypeStruct((h, cols), F32),
        compiler_params=_params(("arbitrary",)),
    )(own, got, got, got)


class _SiblingSwap:
    @staticmethod
    def out_shapes(halves):
        return [jax.ShapeDtypeStruct(v.shape, v.dtype) for v in halves]

    @staticmethod
    def sems(n):
        return [pltpu.SemaphoreType.DMA((n,)), pltpu.SemaphoreType.DMA((n,))]

    def __init__(self, ins, outs, send_sems, recv_sems):
        self.ins, self.outs, self.send_sems, self.recv_sems = ins, outs, send_sems, recv_sems

    def _copies(self):
        x, y, c = _place()
        for a in range(len(self.ins)):
            yield pltpu.make_async_remote_copy(
                src_ref=self.ins[a], dst_ref=self.outs[a], send_sem=self.send_sems.at[a], recv_sem=self.recv_sems.at[a],
                device_id=(x, y, 1 - c), device_id_type=MESH)

    def start(self):
        for cp in self._copies():
            cp.start()

    def finish(self):
        for cp in self._copies():
            cp.wait()


def all_reduce_small(pack):
    rows = pack.shape[0]

    def body(pack_ref, out_ref, gathered, send_sems, recv_sems):
        x, y, c = _place()
        me = 4 * x + 2 * y + c
        gathered[me] = pack_ref[...]
        copies = []
        for rel in range(1, 8):
            fx, fy, fc = (rel >> 2) & 1, (rel >> 1) & 1, rel & 1
            peer = (x ^ fx, y ^ fy, c ^ fc)
            cp = pltpu.make_async_remote_copy(
                src_ref=pack_ref, dst_ref=gathered.at[me], send_sem=send_sems.at[rel - 1], recv_sem=recv_sems.at[rel - 1],
                device_id=peer, device_id_type=MESH)
            cp.start()
            copies.append(cp)
        for rel in range(1, 8):
            fx, fy, fc = (rel >> 2) & 1, (rel >> 1) & 1, rel & 1
            src = 4 * (x ^ fx) + 2 * (y ^ fy) + (c ^ fc)
            pltpu.make_async_remote_copy(
                src_ref=pack_ref, dst_ref=gathered.at[src], send_sem=send_sems.at[rel - 1], recv_sem=recv_sems.at[rel - 1],
                device_id=(x, y, c), device_id_type=MESH).wait_recv()
        for cp in copies:
            cp.wait_send()
        total = gathered[0]
        for dev in range(1, 8):
            total = total + gathered[dev]
        out_ref[...] = total

    return pl.pallas_call(
        body, name="all_reduce_small",
        in_specs=[pl.BlockSpec(memory_space=pltpu.VMEM)], out_specs=pl.BlockSpec(memory_space=pltpu.VMEM),
        out_shape=jax.ShapeDtypeStruct((rows, D), F32),
        scratch_shapes=[pltpu.VMEM((8, rows, D), F32), pltpu.SemaphoreType.DMA((7,)), pltpu.SemaphoreType.DMA((7,))],
    )(pack)


def _adamw_update(w, g, m, v):
    nm = ADAM_B1 * m + (1.0 - ADAM_B1) * g
    nv = ADAM_B2 * v + (1.0 - ADAM_B2) * (g * g)
    m_hat = nm / (1.0 - ADAM_B1 ** ADAM_STEP)
    v_hat = nv / (1.0 - ADAM_B2 ** ADAM_STEP)
    delta = -ADAM_LR * (m_hat / (jnp.sqrt(v_hat) + ADAM_EPS) + ADAM_WD * w)
    return delta, nm, nv


def adamw(name, w, g, m, v):
    def body(w_ref, g_ref, m_ref, v_ref, d_ref, nm_ref, nv_ref):
        d_ref[...], nm_ref[...], nv_ref[...] = _adamw_update(w_ref[...], g_ref[...], m_ref[...], v_ref[...])

    out = jax.ShapeDtypeStruct(w.shape, F32)
    return pl.pallas_call(body, name=name, out_shape=[out] * 3, compiler_params=_params())(w, g, m, v)


def adamw_halves(name, w, mine, theirs, m, v, core):
    h, cols = mine.shape
    tr = _row_tile(h, cols)
    per_half = h // tr

    def body(core_ref, w_ref, mine_ref, theirs_ref, m_ref, v_ref, g_ref, d_ref, nm_ref, nv_ref):
        g = jnp.where(pl.program_id(0) == core_ref[0], mine_ref[...], theirs_ref[...])
        g_ref[...] = g
        d_ref[...], nm_ref[...], nv_ref[...] = _adamw_update(w_ref[...], g, m_ref[...], v_ref[...])

    full = pl.BlockSpec((tr, cols), lambda hh, r, cr: (hh * per_half + r, 0))
    mine_spec = pl.BlockSpec((tr, cols), lambda hh, r, cr: (jnp.where(hh == cr[0], r, 0), 0))
    theirs_spec = pl.BlockSpec((tr, cols), lambda hh, r, cr: (jnp.where(hh == cr[0], 0, r), 0))
    out = jax.ShapeDtypeStruct((2 * h, cols), F32)
    return pl.pallas_call(
        body, name=name,
        grid_spec=pltpu.PrefetchScalarGridSpec(
            num_scalar_prefetch=1, grid=(2, per_half),
            in_specs=[full, mine_spec, theirs_spec, full, full], out_specs=[full] * 4),
        out_shape=[out] * 4,
        compiler_params=_params(("arbitrary", "arbitrary")),
    )(core, w, mine, theirs, m, v)


BIG = ("w_in", "w_conv_out", "w_pool", "w_kv", "w_xattn_out", "w_out", "w_gate", "w_up", "w_down")


def kernel(x, mem, norm_mix, w_in, conv_w, w_conv_out, w_pool, pool_scale, norm_mem, w_kv, w_xattn_out, w_out, norm_ffn, w_gate, w_up, w_down, norm_final, loss_target, m_norm_mix, m_w_in, m_conv_w, m_w_conv_out, m_w_pool, m_pool_scale, m_norm_mem, m_w_kv, m_w_xattn_out, m_w_out, m_norm_ffn, m_w_gate, m_w_up, m_w_down, m_norm_final, v_norm_mix, v_w_in, v_conv_w, v_w_conv_out, v_w_pool, v_pool_scale, v_norm_mem, v_w_kv, v_w_xattn_out, v_w_out, v_norm_ffn, v_w_gate, v_w_up, v_w_down, v_norm_final):
    t_len = x.shape[1]
    xi, yi, ci = lax.axis_index("x"), lax.axis_index("y"), lax.axis_index("c")
    chip = 2 * xi + yi
    core_arr = jnp.reshape(ci, (1,)).astype(jnp.int32)
    place_arr = jnp.stack([ci, chip]).astype(jnp.int32)

    conv_pad = jnp.concatenate([conv_w, jnp.zeros((1, 13, 256), F32)], axis=1)
    def t2(w):
        return jnp.swapaxes(w, 1, 2)

    (g_in,) = exchange_call("gather_w_in", _GatherColumns, [w_in.astype(BF16)])
    w_in_f = g_in[0]

    x2d = x[0]
    tgt = loss_target[0]
    (proj, h), (g_kv, g_conv_w, g_co, g_xo, g_out, g_pool, g_gate) = proj_fwd(
        x2d, norm_mix, w_in_f,
        carry=(_Gather, [w_kv.astype(BF16), conv_pad,
                         w_conv_out.astype(BF16), w_xattn_out.astype(BF16), w_out.astype(BF16),
                         w_pool[0].astype(BF16),
                         t2(w_gate).astype(BF16)]))
    w_kv4 = g_kv.reshape(N_CHIPS, D, D // 2)
    conv_full = jnp.transpose(g_conv_w[:, 0, 0:8, :], (1, 0, 2)).reshape(8, D)
    w_co_f, w_xo_f, w_out_f = g_co.reshape(D, D), g_xo.reshape(D, D), g_out.reshape(D, D)
    w_pool_f = jnp.transpose(g_pool, (1, 0, 2, 3)).reshape(GROUPS, GROUP_DIM, GROUP_DIM)
    memn, k, v = kv_fwd(mem[0], norm_mem, w_kv4)
    (a, pooled, ya, pp, yx, o, probs, x1), (g_up, g_down) = mixer_fwd(
        proj, x2d, conv_full, w_co_f, w_pool_f, pool_scale, k, v, w_xo_f, w_out_f,
        carry=(_Gather, [t2(w_up).astype(BF16), w_down.astype(BF16)]))
    w_gate_f, w_up_f, w_down_f = g_gate.reshape(D_FF, D), g_up.reshape(D_FF, D), g_down.reshape(D_FF, D)
    gate, up, dx2, stat_f = ffn_fwd(x1, tgt, norm_ffn, w_gate_f, w_up_f, w_down_f, norm_final.reshape(1, D))

    def by_chip(pair):
        return tuple(gw.reshape(N_CHIPS, gw.shape[0] // N_CHIPS, gw.shape[1]) for gw in pair)

    def chip_partials(names, grads, got):
        return [add_sibling_half("add_sibling_" + n, g32, o_, place_arr) for n, (g32, _), o_ in zip(names, grads, got)]

    def chip_sums(names, partials, got):
        return [add_chip_blocks("add_chips_" + n, own, g2) for n, (own, _), g2 in zip(names, partials, got)]

    dx1, dgate, dup, act, h2, stat_b1 = ffn_bwd(dx2, x1, gate, up, norm_ffn, w_gate_f, w_up_f, w_down_f)
    gw_gate = by_chip(matmul_tn("grad_w_gate", dgate, h2, 512))
    gw_up, got_gate = matmul_tn("grad_w_up", dup, h2, 512, carry=(_HalfExchange, [gw_gate[1]]))
    gw_up = by_chip(gw_up)
    gw_down, got_up = matmul_tn("grad_w_down", act, dx2, 512, carry=(_HalfExchange, [gw_up[1]]))
    gw_down = by_chip(gw_down)
    got_down = exchange_call("exchange_halves_w_down", _HalfExchange, [gw_down[1]])
    names_ffn = ("w_gate", "w_up", "w_down")
    part_ffn = chip_partials(names_ffn, [gw_gate, gw_up, gw_down], list(got_gate) + list(got_up) + list(got_down))

    (dproj, merged, dya, dpp, dyx, dk, dv, stat_b2), got_ffn = mixer_bwd(
        dx1, proj, ya, pp, yx, probs, conv_full, w_co_f, w_pool_f, pool_scale, k, v, w_xo_f, w_out_f,
        carry=(_ChipExchange, [p16 for _, p16 in part_ffn]))
    gw_kv32, gw_kv16, stat_kv = kv_bwd(dk, dv, memn, mem[0], norm_mem, w_kv4)
    gw_pool = tuple(jnp.transpose(gw.reshape(GROUPS, N_CHIPS, 64, GROUP_DIM), (1, 0, 2, 3)).reshape(N_CHIPS, 256, GROUP_DIM)
                    for gw in grad_w_pool(pooled, dpp))
    gw_co, got_kv_pool = matmul_tn("grad_w_conv_out", a, dya, 1024, carry=(_HalfExchange, [gw_kv16, gw_pool[1]]))
    gw_co = by_chip(gw_co)
    gw_xo, got_co = matmul_tn("grad_w_xattn_out", o, dyx, 1024, carry=(_HalfExchange, [gw_co[1]]))
    gw_xo = by_chip(gw_xo)
    gw_out, got_xo = matmul_tn("grad_w_out", merged, dx1, 1024, carry=(_HalfExchange, [gw_xo[1]]))
    gw_out = by_chip(gw_out)
    got_out = exchange_call("exchange_halves_w_out", _HalfExchange, [gw_out[1]])
    names_mix = ("w_kv", "w_pool", "w_conv_out", "w_xattn_out", "w_out")
    part_mix = chip_partials(names_mix, [(gw_kv32, gw_kv16), gw_pool, gw_co, gw_xo, gw_out],
                             list(got_kv_pool) + list(got_co) + list(got_xo) + list(got_out))

    gw_in, got_mix = matmul_tn("grad_w_in", h, dproj, 2048, col_blocks=N_CHIPS,
                               carry=(_ChipExchange, [p16 for _, p16 in part_mix]))
    part_in = chip_partials(("w_in",), [gw_in], exchange_call("exchange_halves_w_in", _HalfExchange, [gw_in[1]]))
    mine_early = chip_sums(names_ffn + names_mix, part_ffn + part_mix, list(got_ffn) + list(got_mix))
    (grad_x, stat_b3), (got_in, theirs_early) = in_bwd(
        dproj, w_in_f, x2d, dx1, norm_mix,
        carry=[(_ChipExchange, [p16 for _, p16 in part_in]), (_SiblingSwap, mine_early)])
    mine_in = chip_sums(("w_in",), part_in, got_in)
    theirs_in = exchange_call("swap_halves_w_in", _SiblingSwap, mine_in)
    reduced = dict(zip(names_ffn + names_mix + ("w_in",),
                       zip(mine_early + mine_in, list(theirs_early) + list(theirs_in))))
    mine = [reduced[n][0] for n in BIG]
    theirs = [reduced[n][1] for n in BIG]

    pack = jnp.concatenate([stat_b3[0:1], stat_b2[1:2], stat_kv[0:1], stat_b1[0:1], stat_f[0:1], stat_b2[5:8],
                            stat_f[1:2], jnp.zeros((7, D), F32)], axis=0)
    total = all_reduce_small(pack)
    loss = jnp.sum(total[8])
    g_conv_full = total[5:8]
    g_conv = lax.dynamic_slice_in_dim(g_conv_full, chip * 256, 256, axis=1)

    given = dict(w_in=(w_in, m_w_in, v_w_in), w_conv_out=(w_conv_out, m_w_conv_out, v_w_conv_out),
                 w_pool=(w_pool, m_w_pool, v_w_pool), w_kv=(w_kv, m_w_kv, v_w_kv),
                 w_xattn_out=(w_xattn_out, m_w_xattn_out, v_w_xattn_out), w_out=(w_out, m_w_out, v_w_out),
                 w_gate=(w_gate, m_w_gate, v_w_gate), w_up=(w_up, m_w_up, v_w_up), w_down=(w_down, m_w_down, v_w_down))
    out_g, out_d, out_m, out_v = {}, {}, {}, {}
    for n, mine_n, theirs_n in zip(BIG, mine, theirs):
        transposed = n in ("w_gate", "w_up")
        rows2d = (2 * mine_n.shape[0], mine_n.shape[1])
        w_, m_, v_ = ((t2(t) if transposed else t).reshape(rows2d) for t in given[n])
        res = adamw_halves("adamw_" + n, w_, mine_n, theirs_n, m_, v_, core_arr)
        if transposed:
            res = [t2(t.reshape(1, D_FF // N_CHIPS, D)) for t in res]
        out_g[n], out_d[n], out_m[n], out_v[n] = (t.reshape(given[n][0].shape) for t in res)

    def small_pack(vals, conv_part):
        conv_rows = jnp.concatenate([conv_part.reshape(3, 256), jnp.zeros((3, D - 256), F32)], axis=1)
        return jnp.concatenate([val.reshape(1, D) for val in vals] + [conv_rows], axis=0)

    sw = small_pack([norm_mix, pool_scale, norm_mem, norm_ffn, norm_final], conv_w)
    sm = small_pack([m_norm_mix, m_pool_scale, m_norm_mem, m_norm_ffn, m_norm_final], m_conv_w)
    sv = small_pack([v_norm_mix, v_pool_scale, v_norm_mem, v_norm_ffn, v_norm_final], v_conv_w)
    sg = small_pack([total[r] for r in range(5)], g_conv)
    sd, snm, snv = adamw("adamw_small", sw, sg, sm, sv)
    small_names = ("norm_mix", "pool_scale", "norm_mem", "norm_ffn", "norm_final")
    small_shapes = dict(norm_mix=(1, D), pool_scale=(1, D), norm_mem=(1, D), norm_ffn=(1, D), norm_final=(D,))
    for r, n in enumerate(small_names):
        out_g[n], out_d[n], out_m[n], out_v[n] = (t[r].reshape(small_shapes[n]) for t in (sg, sd, snm, snv))
    out_g["conv_w"], out_d["conv_w"], out_m["conv_w"], out_v["conv_w"] = (
        t[5:8, 0:256].reshape(1, 3, 256) for t in (sg, sd, snm, snv))

    order = ("norm_mix", "w_in", "conv_w", "w_conv_out", "w_pool", "pool_scale", "norm_mem", "w_kv", "w_xattn_out",
             "w_out", "norm_ffn", "w_gate", "w_up", "w_down", "norm_final")
    return (loss, grad_x.reshape(1, t_len, D), *[out_g[n] for n in order], *[out_d[n] for n in order],
            *[out_m[n] for n in order], *[out_v[n] for n in order])
```

```python
import functools

import jax
import jax.numpy as jnp
from jax import lax
from jax.experimental import pallas as pl
from jax.experimental.pallas import tpu as pltpu

F32 = jnp.float32
BF16 = jnp.bfloat16
MESH = pl.DeviceIdType.MESH

D = 1024
N_MEM = 256
HEADS = 4
HEAD_DIM = 256
GROUPS = 4
GROUP_DIM = 256
POOL_WINDOWS = (2, 4, 8, 16)
D_FF = 2816
D_IN = 8192
N_CHIPS = 4
EPS = 1e-6
HALO = 16
POOL_PAD = 128
ATT_SCALE = HEAD_DIM ** -0.5

ADAM_LR = 0.001
ADAM_B1 = 0.9
ADAM_B2 = 0.999
ADAM_EPS = 1e-08
ADAM_WD = 0.01
ADAM_STEP = 10

VMEM_LIMIT = 56 * 1024 * 1024

O_BA, O_CA, O_UA, O_UP, O_QX, O_GA, O_GP, O_GX = (k * D for k in range(8))

NT_DIMS = (((1,), (1,)), ((), ()))
TN_DIMS = (((0,), (0,)), ((), ()))


def _dot(a, b):
    return jnp.dot(a, b, preferred_element_type=F32)


def _dot_nt(a, b):
    return lax.dot_general(a, b, NT_DIMS, preferred_element_type=F32)


def _dot_tn(a, b):
    return lax.dot_general(a, b, TN_DIMS, preferred_element_type=F32)


def _sigmoid(z):
    return pl.reciprocal(1.0 + jnp.exp(-z), approx=True)


def _params(semantics=None):
    return pltpu.CompilerParams(dimension_semantics=semantics, vmem_limit_bytes=VMEM_LIMIT)


def _resident(shape):
    zeros = (0,) * len(shape)
    return pl.BlockSpec(shape, lambda *_: zeros, pipeline_mode=pl.Buffered(1))


def _const(shape):
    zeros = (0,) * len(shape)
    return pl.BlockSpec(shape, lambda *_: zeros)


def _rows(tm, width):
    return pl.BlockSpec((tm, width), lambda i: (i, 0))


def _inv_count(tile, tm, window):
    t = tile * tm + lax.broadcasted_iota(jnp.int32, (tm, 1), 0)
    return 1.0 / jnp.minimum(t + 1, window).astype(F32)


def _carried_call(body, name, grid, in_specs, out_specs, out_shape, scratch_shapes, semantics, args, carry):
    if carry is None:
        res = pl.pallas_call(body, name=name, grid=grid, in_specs=in_specs, out_specs=out_specs, out_shape=out_shape,
                             scratch_shapes=scratch_shapes, compiler_params=_params(semantics))(*args)
        return res, []
    carries = [carry] if isinstance(carry, tuple) else list(carry)
    comm_args = [arr for _, arrs in carries for arr in arrs]
    n, n_in, n_out, n_scratch = len(comm_args), len(in_specs), len(out_specs), len(scratch_shapes)
    shapes_of = [cls.out_shapes(arrs) for cls, arrs in carries]
    comm_shapes = [s for shapes in shapes_of for s in shapes]
    m = len(comm_shapes)
    comm_sems = [s for cls, arrs in carries for s in cls.sems(len(arrs))]

    def carrying(*refs):
        ins, comm_ins = refs[:n_in], refs[n_in:n_in + n]
        outs, comm_outs = refs[n_in + n:n_in + n + n_out], refs[n_in + n + n_out:n_in + n + n_out + m]
        scratch, sems = refs[n_in + n + n_out + m:n_in + n + n_out + m + n_scratch], refs[n_in + n + n_out + m + n_scratch:]
        steps = [pl.program_id(d) for d in range(len(grid))]
        first = functools.reduce(jnp.logical_and, [s == 0 for s in steps])
        last = functools.reduce(jnp.logical_and, [s == g - 1 for s, g in zip(steps, grid)])

        def exchanges():
            at_in = at_out = 0
            for k, (cls, arrs) in enumerate(carries):
                yield cls(comm_ins[at_in:at_in + len(arrs)], comm_outs[at_out:at_out + len(shapes_of[k])],
                          sems[2 * k], sems[2 * k + 1])
                at_in, at_out = at_in + len(arrs), at_out + len(shapes_of[k])

        @pl.when(first)
        def _():
            for exchange in exchanges():
                exchange.start()

        linear, total = 0, 1
        for s, g in zip(steps, grid):
            linear, total = linear * g + s, total * g
        for k, (cls, _) in enumerate(carries):
            for phase, (num, den) in getattr(cls, "PHASES_AT", {}).items():
                @pl.when(linear == (num * total) // den)
                def _(k=k, phase=phase):
                    getattr(list(exchanges())[k], phase)()

        body(*ins, *outs, *scratch)

        @pl.when(last)
        def _():
            for exchange in exchanges():
                exchange.finish()

    res = pl.pallas_call(
        carrying, name=name, grid=grid, in_specs=list(in_specs) + _any_specs(n), out_specs=list(out_specs) + _any_specs(m),
        out_shape=list(out_shape) + comm_shapes, scratch_shapes=list(scratch_shapes) + comm_sems,
        compiler_params=_params(semantics))(*args, *comm_args)
    comm_res, at = [], n_out
    for (_, arrs), shapes in zip(carries, shapes_of):
        comm_res.append(res[at:at + len(arrs)])
        at += len(shapes)
    return res[:n_out], (comm_res[0] if isinstance(carry, tuple) else comm_res)


def proj_fwd(x, g_mix, w_in, carry=None):
    t_len = x.shape[0]
    tm = min(1024, t_len)
    tn = D_IN // N_CHIPS

    def body(x_ref, g_ref, w_ref, proj_ref, h_ref):
        @pl.when(pl.program_id(1) == 0)
        def _():
            xv = x_ref[...]
            r = lax.rsqrt(jnp.mean(xv * xv, axis=-1, keepdims=True) + EPS)
            h_ref[...] = (xv * r * g_ref[...]).astype(BF16)

        proj_ref[...] = _dot(h_ref[...], w_ref[...]).astype(BF16)

    return _carried_call(
        body, "proj_fwd", (t_len // tm, N_CHIPS),
        in_specs=[pl.BlockSpec((tm, D), lambda i, j: (i, 0)),
                  pl.BlockSpec((1, D), lambda i, j: (0, 0)),
                  pl.BlockSpec((D, tn), lambda i, j: (0, j))],
        out_specs=[pl.BlockSpec((tm, tn), lambda i, j: (i, j)),
                   pl.BlockSpec((tm, D), lambda i, j: (i, 0))],
        out_shape=[jax.ShapeDtypeStruct((t_len, D_IN), BF16), jax.ShapeDtypeStruct((t_len, D), BF16)],
        scratch_shapes=[], semantics=("arbitrary", "arbitrary"), args=(x, g_mix, w_in), carry=carry)


def kv_fwd(mem, g_mem, w_kv4):
    half = D // 2

    def body(mem_ref, g_ref, w_ref, memn_ref, k_ref, v_ref):
        mv = mem_ref[...]
        r = lax.rsqrt(jnp.mean(mv * mv, axis=-1, keepdims=True) + EPS)
        mn = (mv * r * g_ref[...]).astype(BF16)
        memn_ref[...] = mn
        k_ref[:, 0:half] = _dot(mn, w_ref[0]).astype(BF16)
        k_ref[:, half:D] = _dot(mn, w_ref[1]).astype(BF16)
        v_ref[:, 0:half] = _dot(mn, w_ref[2]).astype(BF16)
        v_ref[:, half:D] = _dot(mn, w_ref[3]).astype(BF16)

    out = jax.ShapeDtypeStruct((N_MEM, D), BF16)
    return pl.pallas_call(body, name="kv_fwd", out_shape=[out, out, out], compiler_params=_params())(mem, g_mem, w_kv4)


def _softmax_rows(s):
    m = jnp.max(s, axis=-1, keepdims=True)
    e = jnp.exp(s - m)
    return e * pl.reciprocal(jnp.sum(e, axis=-1, keepdims=True), approx=True)


def _window_bands(tm, causal):
    t = lax.broadcasted_iota(jnp.int32, (tm, tm + POOL_PAD), 0)
    s = lax.broadcasted_iota(jnp.int32, (tm, tm + POOL_PAD), 1)
    d = (t + POOL_PAD - s) if causal else (s - t)
    return jnp.stack([((d >= 0) & (d < w)).astype(BF16) for w in POOL_WINDOWS])


def mixer_fwd(proj, x, conv_w8, w_co, w_pool, pool_scale, k, v, w_xo, w_out, carry=None):
    t_len = x.shape[0]
    tm = min(256, t_len)

    def body(proj_ref, x_ref, cw_ref, wco_ref, wpool_ref, ps_ref, k_ref, v_ref, wxo_ref, wout_ref,
             a_ref, pooled_ref, ya_ref, pp_ref, yx_ref, o_ref, p_ref, x1_ref, cu_ext, up_ext):
        i = pl.program_id(0)

        @pl.when(i == 0)
        def _():
            cu_ext[0:HALO, :] = jnp.zeros((HALO, D), F32)
            up_ext[0:HALO, :] = jnp.zeros((HALO, D), F32)

        cu = proj_ref[:, O_CA:O_CA + D].astype(F32) * proj_ref[:, O_UA:O_UA + D].astype(F32)
        cu_ext[HALO:HALO + tm, :] = cu
        conv = (cw_ref[2:3, :] * cu + cw_ref[1:2, :] * cu_ext[HALO - 1:HALO - 1 + tm, :]
                + cw_ref[0:1, :] * cu_ext[HALO - 2:HALO - 2 + tm, :])
        a = (proj_ref[:, O_BA:O_BA + D].astype(F32) * conv).astype(BF16)
        a_ref[...] = a
        ya = _dot(a, wco_ref[...])
        ya_ref[...] = ya.astype(BF16)

        up_ext[HALO:HALO + tm, :] = proj_ref[:, O_UP:O_UP + D].astype(F32)
        for g, window in enumerate(POOL_WINDOWS):
            cols = slice(g * GROUP_DIM, (g + 1) * GROUP_DIM)
            tok = up_ext[HALO:HALO + tm, cols]
            acc = tok
            for j in range(1, window):
                acc = acc + up_ext[HALO - j:HALO - j + tm, cols]
            pooled = (acc * _inv_count(i, tm, window) - tok).astype(BF16)
            pooled_ref[:, cols] = pooled
            pp_ref[:, cols] = _dot(pooled, wpool_ref[g]).astype(BF16)

        for hd in range(HEADS):
            cols = slice(hd * HEAD_DIM, (hd + 1) * HEAD_DIM)
            q = proj_ref[:, O_QX + hd * HEAD_DIM:O_QX + (hd + 1) * HEAD_DIM]
            p = _softmax_rows(_dot_nt(q, k_ref[:, cols]) * ATT_SCALE).astype(BF16)
            p_ref[:, hd * N_MEM:(hd + 1) * N_MEM] = p
            o_ref[:, cols] = _dot(p, v_ref[:, cols]).astype(BF16)
        yx = _dot(o_ref[...], wxo_ref[...])
        yx_ref[...] = yx.astype(BF16)

        merged = (_sigmoid(proj_ref[:, O_GA:O_GA + D].astype(F32)) * ya
                  + _sigmoid(proj_ref[:, O_GP:O_GP + D].astype(F32)) * (pp_ref[...].astype(F32) * ps_ref[...])
                  + _sigmoid(proj_ref[:, O_GX:O_GX + D].astype(F32)) * yx)
        x1_ref[...] = x_ref[...] + _dot(merged.astype(BF16), wout_ref[...])

        cu_ext[0:HALO, :] = cu_ext[tm:tm + HALO, :]
        up_ext[0:HALO, :] = up_ext[tm:tm + HALO, :]

    act = jax.ShapeDtypeStruct((t_len, D), BF16)
    return _carried_call(
        body, "mixer_fwd", (t_len // tm,),
        in_specs=[_rows(tm, D_IN), _rows(tm, D), _resident((8, D)), _resident((D, D)),
                  _resident((GROUPS, GROUP_DIM, GROUP_DIM)), _resident((1, D)),
                  _resident((N_MEM, D)), _resident((N_MEM, D)), _resident((D, D)), _resident((D, D))],
        out_specs=[_rows(tm, D)] * 7 + [_rows(tm, D)],
        out_shape=[act] * 6 + [jax.ShapeDtypeStruct((t_len, HEADS * N_MEM), BF16), jax.ShapeDtypeStruct((t_len, D), F32)],
        scratch_shapes=[pltpu.VMEM((tm + HALO, D), F32), pltpu.VMEM((tm + HALO, D), F32)],
        semantics=("arbitrary",), args=(proj, x, conv_w8, w_co, w_pool, pool_scale, k, v, w_xo, w_out), carry=carry)


def ffn_fwd(x1, target, g_ffn, w_gate, w_up, w_down, g_final):
    t_len = x1.shape[0]
    tm = min(512, t_len)

    def body(x1_ref, tgt_ref, g_ref, wg_ref, wu_ref, wd_ref, gf_ref, gate_ref, up_ref, dx2_ref, stat_ref):
        @pl.when(pl.program_id(0) == 0)
        def _():
            stat_ref[...] = jnp.zeros((8, D), F32)

        x1v = x1_ref[...]
        r2 = lax.rsqrt(jnp.mean(x1v * x1v, axis=-1, keepdims=True) + EPS)
        h2 = (x1v * r2 * g_ref[...]).astype(BF16)
        gate = _dot_nt(h2, wg_ref[...])
        up = _dot_nt(h2, wu_ref[...])
        gate_ref[...] = gate.astype(BF16)
        up_ref[...] = up.astype(BF16)
        act = (gate * _sigmoid(gate) * up).astype(BF16)
        x2 = x1v + _dot(act, wd_ref[...])
        r3 = lax.rsqrt(jnp.mean(x2 * x2, axis=-1, keepdims=True) + EPS)
        xh = x2 * r3
        diff = xh * gf_ref[...] - tgt_ref[...]
        dy = diff * (1.0 / D)
        stat_ref[0:1, :] += jnp.sum(dy * xh, axis=0, keepdims=True)
        stat_ref[1:2, :] += (0.5 / D) * jnp.sum(diff * diff, axis=0, keepdims=True)
        dxh = dy * gf_ref[...]
        dx2_ref[...] = r3 * (dxh - xh * jnp.mean(dxh * xh, axis=-1, keepdims=True))

    return pl.pallas_call(
        body, name="ffn_fwd",
        grid=(t_len // tm,),
        in_specs=[_rows(tm, D), _rows(tm, D), _resident((1, D)), _resident((D_FF, D)), _resident((D_FF, D)),
                  _resident((D_FF, D)), _resident((1, D))],
        out_specs=[_rows(tm, D_FF), _rows(tm, D_FF), _rows(tm, D), _const((8, D))],
        out_shape=[jax.ShapeDtypeStruct((t_len, D_FF), BF16), jax.ShapeDtypeStruct((t_len, D_FF), BF16),
                   jax.ShapeDtypeStruct((t_len, D), F32), jax.ShapeDtypeStruct((8, D), F32)],
        compiler_params=_params(("arbitrary",)),
    )(x1, target, g_ffn, w_gate, w_up, w_down, g_final)


def ffn_bwd(dx2, x1, gate, up, g_ffn, w_gate, w_up, w_down):
    t_len = x1.shape[0]
    tm = min(256, t_len)

    def body(dx2_ref, x1_ref, gate_ref, up_ref, g_ref, wg_ref, wu_ref, wd_ref,
             dx1_ref, dgate_ref, dup_ref, act_ref, h2_ref, stat_ref):
        @pl.when(pl.program_id(0) == 0)
        def _():
            stat_ref[...] = jnp.zeros((8, D), F32)

        dx2v = dx2_ref[...]
        gate = gate_ref[...]
        upv = up_ref[...]
        sg = _sigmoid(gate.astype(F32)).astype(BF16)
        silu = gate * sg
        act_ref[...] = silu * upv
        dact = _dot_nt(dx2v.astype(BF16), wd_ref[...]).astype(BF16)
        dup = dact * silu
        dgate = dact * upv * (sg * (1.0 + gate * (1.0 - sg)))
        dup_ref[...] = dup
        dgate_ref[...] = dgate
        dh2 = _dot(dgate, wg_ref[...]) + _dot(dup, wu_ref[...])
        x1v = x1_ref[...]
        r2 = lax.rsqrt(jnp.mean(x1v * x1v, axis=-1, keepdims=True) + EPS)
        xh = x1v * r2
        h2_ref[...] = (xh * g_ref[...]).astype(BF16)
        stat_ref[0:1, :] += jnp.sum(dh2 * xh, axis=0, keepdims=True)
        dxh = dh2 * g_ref[...]
        dx1_ref[...] = dx2v + r2 * (dxh - xh * jnp.mean(dxh * xh, axis=-1, keepdims=True))

    ff = jax.ShapeDtypeStruct((t_len, D_FF), BF16)
    return pl.pallas_call(
        body, name="ffn_bwd",
        grid=(t_len // tm,),
        in_specs=[_rows(tm, D), _rows(tm, D), _rows(tm, D_FF), _rows(tm, D_FF), _resident((1, D)),
                  _resident((D_FF, D)), _resident((D_FF, D)), _resident((D_FF, D))],
        out_specs=[_rows(tm, D), _rows(tm, D_FF), _rows(tm, D_FF), _rows(tm, D_FF), _rows(tm, D), _const((8, D))],
        out_shape=[jax.ShapeDtypeStruct((t_len, D), F32), ff, ff, ff, jax.ShapeDtypeStruct((t_len, D), BF16),
                   jax.ShapeDtypeStruct((8, D), F32)],
        compiler_params=_params(("arbitrary",)),
    )(dx2, x1, gate, up, g_ffn, w_gate, w_up, w_down)


def mixer_bwd(dx1, proj, ya, pp, yx, probs, conv_w8, w_co, w_pool, pool_scale, k, v, w_xo, w_out, carry=None):
    t_len = dx1.shape[0]
    tm = min(256, t_len)
    n_tiles = t_len // tm
    halo_blocks = tm // HALO

    def body(dx1_ref, proj_ref, halo_ref, ya_ref, pp_ref, yx_ref, p_ref,
             cw_ref, wco_ref, wpool_ref, ps_ref, k_ref, v_ref, wxo_ref, wout_ref, band_ref,
             dproj_ref, merged_ref, dya_ref, dpp_ref, dyx_ref, dk_ref, dv_ref, stat_ref,
             cu_ext, dconv_ext, dpn_ext):
        step = pl.program_id(0)
        tile = n_tiles - 1 - step

        @pl.when(step == 0)
        def _():
            dk_ref[...] = jnp.zeros((N_MEM, D), F32)
            dv_ref[...] = jnp.zeros((N_MEM, D), F32)
            stat_ref[...] = jnp.zeros((8, D), F32)
            dconv_ext[tm:tm + HALO, :] = jnp.zeros((HALO, D), F32)
            dpn_ext[tm:tm + POOL_PAD, :] = jnp.zeros((POOL_PAD, D), BF16)

        dmerged = _dot_nt(dx1_ref[...].astype(BF16), wout_ref[...]).astype(BF16)
        sa = _sigmoid(proj_ref[:, O_GA:O_GA + D].astype(F32)).astype(BF16)
        sp = _sigmoid(proj_ref[:, O_GP:O_GP + D].astype(F32)).astype(BF16)
        sx = _sigmoid(proj_ref[:, O_GX:O_GX + D].astype(F32)).astype(BF16)
        ya = ya_ref[...]
        ppv = pp_ref[...]
        scale = ps_ref[...].astype(BF16)
        yp = ppv * scale
        yx = yx_ref[...]
        merged_ref[...] = sa * ya + sp * yp + sx * yx
        dproj_ref[:, O_GA:O_GA + D] = dmerged * ya * (sa * (1.0 - sa))
        dproj_ref[:, O_GP:O_GP + D] = dmerged * yp * (sp * (1.0 - sp))
        dproj_ref[:, O_GX:O_GX + D] = dmerged * yx * (sx * (1.0 - sx))
        dya = dmerged * sa
        dyp = dmerged * sp
        dyx = dmerged * sx
        dya_ref[...] = dya
        dyx_ref[...] = dyx
        stat_ref[1:2, :] += jnp.sum(dyp.astype(F32) * ppv.astype(F32), axis=0, keepdims=True)
        dpp = dyp * scale
        dpp_ref[...] = dpp

        da = _dot_nt(dya, wco_ref[...])
        c_a = proj_ref[:, O_CA:O_CA + D].astype(F32)
        u_a = proj_ref[:, O_UA:O_UA + D].astype(F32)
        cu = c_a * u_a
        halo_cu = halo_ref[:, O_CA:O_CA + D].astype(F32) * halo_ref[:, O_UA:O_UA + D].astype(F32)
        cu_ext[0:HALO, :] = jnp.where(tile > 0, halo_cu, 0.0)
        cu_ext[HALO:HALO + tm, :] = cu
        cu1 = cu_ext[HALO - 1:HALO - 1 + tm, :]
        cu2 = cu_ext[HALO - 2:HALO - 2 + tm, :]
        conv = cw_ref[2:3, :] * cu + cw_ref[1:2, :] * cu1 + cw_ref[0:1, :] * cu2
        dproj_ref[:, O_BA:O_BA + D] = (da * conv).astype(BF16)
        dconv = da * proj_ref[:, O_BA:O_BA + D].astype(F32)
        stat_ref[5:6, :] += jnp.sum(dconv * cu2, axis=0, keepdims=True)
        stat_ref[6:7, :] += jnp.sum(dconv * cu1, axis=0, keepdims=True)
        stat_ref[7:8, :] += jnp.sum(dconv * cu, axis=0, keepdims=True)
        dconv_ext[0:tm, :] = dconv
        dcu = (cw_ref[2:3, :] * dconv + cw_ref[1:2, :] * dconv_ext[1:1 + tm, :]
               + cw_ref[0:1, :] * dconv_ext[2:2 + tm, :])
        dproj_ref[:, O_CA:O_CA + D] = (dcu * u_a).astype(BF16)
        dproj_ref[:, O_UA:O_UA + D] = (dcu * c_a).astype(BF16)

        for g, window in enumerate(POOL_WINDOWS):
            cols = slice(g * GROUP_DIM, (g + 1) * GROUP_DIM)
            dpooled = _dot_nt(dpp[:, cols], wpool_ref[g])
            dpn_ext[0:tm, cols] = (dpooled * _inv_count(tile, tm, window)).astype(BF16)
            acc = _dot(band_ref[g], dpn_ext[:, cols])
            dproj_ref[:, O_UP + g * GROUP_DIM:O_UP + (g + 1) * GROUP_DIM] = (acc - dpooled).astype(BF16)

        do = _dot_nt(dyx, wxo_ref[...])
        for hd in range(HEADS):
            cols = slice(hd * HEAD_DIM, (hd + 1) * HEAD_DIM)
            q = proj_ref[:, O_QX + hd * HEAD_DIM:O_QX + (hd + 1) * HEAD_DIM]
            kh = k_ref[:, cols]
            p16 = p_ref[:, hd * N_MEM:(hd + 1) * N_MEM]
            p = p16.astype(F32)
            doh = do[:, cols].astype(BF16)
            dp = _dot_nt(doh, v_ref[:, cols])
            dv_ref[:, cols] += _dot_tn(p16, doh)
            ds = (p * (dp - jnp.sum(dp * p, axis=-1, keepdims=True)) * ATT_SCALE).astype(BF16)
            dproj_ref[:, O_QX + hd * HEAD_DIM:O_QX + (hd + 1) * HEAD_DIM] = _dot(ds, kh).astype(BF16)
            dk_ref[:, cols] += _dot_tn(ds, q)

        dconv_ext[tm:tm + HALO, :] = dconv_ext[0:HALO, :]
        dpn_ext[tm:tm + HALO, :] = dpn_ext[0:HALO, :]

    def rev(width):
        return pl.BlockSpec((tm, width), lambda s: (n_tiles - 1 - s, 0))

    halo_spec = pl.BlockSpec((HALO, D_IN), lambda s: (jnp.maximum((n_tiles - 1 - s) * halo_blocks - 1, 0), 0))
    act = jax.ShapeDtypeStruct((t_len, D), BF16)
    kv_grad = jax.ShapeDtypeStruct((N_MEM, D), F32)
    return _carried_call(
        body, "mixer_bwd", (n_tiles,),
        in_specs=[rev(D), rev(D_IN), halo_spec, rev(D), rev(D), rev(D), rev(D),
                  _resident((8, D)), _resident((D, D)), _resident((GROUPS, GROUP_DIM, GROUP_DIM)), _resident((1, D)),
                  _resident((N_MEM, D)), _resident((N_MEM, D)), _resident((D, D)), _resident((D, D)),
                  _resident((GROUPS, tm, tm + POOL_PAD))],
        out_specs=[rev(D_IN), rev(D), rev(D), rev(D), rev(D),
                   _const((N_MEM, D)), _const((N_MEM, D)), _const((8, D))],
        out_shape=[jax.ShapeDtypeStruct((t_len, D_IN), BF16), act, act, act, act, kv_grad, kv_grad,
                   jax.ShapeDtypeStruct((8, D), F32)],
        scratch_shapes=[pltpu.VMEM((tm + HALO, D), F32)] * 2 + [pltpu.VMEM((tm + POOL_PAD, D), BF16)],
        semantics=("arbitrary",),
        args=(dx1, proj, proj, ya, pp, yx, probs, conv_w8, w_co, w_pool, pool_scale, k, v, w_xo, w_out,
              _window_bands(tm, False)), carry=carry)


def in_bwd(dproj, w_in, x, dx1, g_mix, carry=None):
    t_len = x.shape[0]
    tm = min(512, t_len)

    def body(dproj_ref, w_ref, x_ref, dx1_ref, g_ref, gx_ref, stat_ref):
        @pl.when(pl.program_id(0) == 0)
        def _():
            stat_ref[...] = jnp.zeros((8, D), F32)

        dh = _dot_nt(dproj_ref[...], w_ref[...])
        xv = x_ref[...]
        r = lax.rsqrt(jnp.mean(xv * xv, axis=-1, keepdims=True) + EPS)
        xh = xv * r
        stat_ref[0:1, :] += jnp.sum(dh * xh, axis=0, keepdims=True)
        dxh = dh * g_ref[...]
        gx_ref[...] = dx1_ref[...] + r * (dxh - xh * jnp.mean(dxh * xh, axis=-1, keepdims=True))

    return _carried_call(
        body, "in_bwd", (t_len // tm,),
        in_specs=[_rows(tm, D_IN), _resident((D, D_IN)), _rows(tm, D), _rows(tm, D), _resident((1, D))],
        out_specs=[_rows(tm, D), _const((8, D))],
        out_shape=[jax.ShapeDtypeStruct((t_len, D), F32), jax.ShapeDtypeStruct((8, D), F32)],
        scratch_shapes=[], semantics=("arbitrary",), args=(dproj, w_in, x, dx1, g_mix), carry=carry)


def kv_bwd(dk, dv, memn, mem, g_mem, w_kv4):
    half = D // 2

    def body(dk_ref, dv_ref, memn_ref, mem_ref, g_ref, w_ref, gw_ref, gw16_ref, stat_ref):
        mn = memn_ref[...]
        parts = (dk_ref[:, 0:half], dk_ref[:, half:D], dv_ref[:, 0:half], dv_ref[:, half:D])
        dmemn = jnp.zeros((N_MEM, D), F32)
        for j, part in enumerate(parts):
            part = part.astype(BF16)
            gw = _dot_tn(mn, part)
            gw_ref[j] = gw
            gw16_ref[j] = gw.astype(BF16)
            dmemn = dmemn + _dot_nt(part, w_ref[j])
        mv = mem_ref[...]
        r = lax.rsqrt(jnp.mean(mv * mv, axis=-1, keepdims=True) + EPS)
        stat_ref[...] = jnp.zeros((8, D), F32)
        stat_ref[0:1, :] = jnp.sum(dmemn * (mv * r), axis=0, keepdims=True)

    return pl.pallas_call(
        body, name="kv_bwd",
        out_shape=[jax.ShapeDtypeStruct((N_CHIPS, D, half), F32), jax.ShapeDtypeStruct((N_CHIPS, D, half), BF16),
                   jax.ShapeDtypeStruct((8, D), F32)],
        compiler_params=_params(),
    )(dk, dv, memn, mem, g_mem, w_kv4)


def matmul_tn(name, a, b, tn, col_blocks=1, carry=None):
    t_len, k_dim = a.shape
    n_dim = b.shape[1]
    tt = min(2048, t_len)
    per_block = n_dim // col_blocks // tn

    def body(a_ref, b_ref, out_ref, out16_ref):
        @pl.when(pl.program_id(1) == 0)
        def _():
            out_ref[...] = jnp.zeros((k_dim, tn), F32)

        out_ref[...] += _dot_tn(a_ref[...].astype(BF16), b_ref[...].astype(BF16))

        @pl.when(pl.program_id(1) == t_len // tt - 1)
        def _():
            out16_ref[...] = out_ref[...].astype(BF16)

    if col_blocks == 1:
        out_spec = pl.BlockSpec((k_dim, tn), lambda n, t: (0, n))
        shape = (k_dim, n_dim)
    else:
        out_spec = pl.BlockSpec((None, k_dim, tn), lambda n, t: (n // per_block, 0, n % per_block))
        shape = (col_blocks, k_dim, n_dim // col_blocks)
    outs, carried = _carried_call(
        body, name, (n_dim // tn, t_len // tt),
        in_specs=[pl.BlockSpec((tt, k_dim), lambda n, t: (t, 0)), pl.BlockSpec((tt, tn), lambda n, t: (t, n))],
        out_specs=[out_spec, out_spec], out_shape=[jax.ShapeDtypeStruct(shape, F32), jax.ShapeDtypeStruct(shape, BF16)],
        scratch_shapes=[], semantics=("arbitrary", "arbitrary"), args=(a, b), carry=carry)
    return (tuple(outs), carried) if carry is not None else tuple(outs)


def grad_w_pool(pooled, dpp):
    t_len = pooled.shape[0]
    tt = min(1024, t_len)
    steps = t_len // tt

    def body(a_ref, b_ref, out_ref, out16_ref):
        @pl.when(pl.program_id(0) == 0)
        def _():
            out_ref[...] = jnp.zeros((GROUPS, GROUP_DIM, GROUP_DIM), F32)

        for g in range(GROUPS):
            cols = slice(g * GROUP_DIM, (g + 1) * GROUP_DIM)
            out_ref[g] += _dot_tn(a_ref[:, cols], b_ref[:, cols])

        @pl.when(pl.program_id(0) == steps - 1)
        def _():
            out16_ref[...] = out_ref[...].astype(BF16)

    shape = (GROUPS, GROUP_DIM, GROUP_DIM)
    return pl.pallas_call(
        body, name="grad_w_pool", grid=(steps,),
        in_specs=[_rows(tt, D), _rows(tt, D)], out_specs=[_const(shape), _const(shape)],
        out_shape=[jax.ShapeDtypeStruct(shape, F32), jax.ShapeDtypeStruct(shape, BF16)],
        compiler_params=_params(("arbitrary",)),
    )(pooled, dpp)


def _place():
    x, y, c = lax.axis_index("x"), lax.axis_index("y"), lax.axis_index("c")
    return x, y, c


def _other_chips(x, y):
    return [(1 - x, y), (x, 1 - y), (1 - x, 1 - y)]


def _any_specs(n):
    return [pl.BlockSpec(memory_space=pl.ANY)] * n


def exchange_call(name, exchange_cls, arrays):
    n = len(arrays)
    shapes = exchange_cls.out_shapes(arrays)

    def body(*refs):
        exchange = exchange_cls(refs[:n], refs[n:n + len(shapes)], *refs[n + len(shapes):])
        exchange.start()
        for phase in getattr(exchange_cls, "PHASES_AT", {}):
            getattr(exchange, phase)()
        exchange.finish()

    res = pl.pallas_call(
        body, name=name, in_specs=_any_specs(n), out_specs=_any_specs(len(shapes)),
        out_shape=shapes, scratch_shapes=exchange_cls.sems(n),
    )(*arrays)
    return res[:n]


class _Gather:
    SLOTS = 8
    ROW_ALIGN = 16
    PHASES_AT = {"middle": (4, 8), "late": (7, 8)}

    @staticmethod
    def out_shapes(shards):
        return [jax.ShapeDtypeStruct((N_CHIPS,) + s.shape, s.dtype) for s in shards]

    @staticmethod
    def sems(n):
        return [pltpu.SemaphoreType.DMA((n, _Gather.SLOTS)), pltpu.SemaphoreType.DMA((n, _Gather.SLOTS))]

    def __init__(self, ins, outs, send_sems, recv_sems):
        self.ins, self.outs, self.send_sems, self.recv_sems = ins, outs, send_sems, recv_sems
        x, y, c = _place()
        self.c, self.me, self.sibling = c, 2 * x + y, (x, y, 1 - c)
        self.across = [(1 - x, y), (x, 1 - y), (1 - x, 1 - y)]

    def _rows(self, a, which, part=None):
        half = self.ins[a].shape[1] // 2
        first = (half // 2) // self.ROW_ALIGN * self.ROW_ALIGN
        if part is None:
            return pl.ds(which * half, half)
        return pl.ds(which * half, first) if part == 0 else pl.ds(which * half + first, half - first)

    def _has_part(self, a, part):
        half = self.ins[a].shape[1] // 2
        return part == 1 or (half // 2) // self.ROW_ALIGN > 0

    def _block(self, a, chip, rows=slice(None)):
        index = chip if not isinstance(chip, tuple) else 2 * chip[0] + chip[1]
        if len(self.outs[a].shape) == len(self.ins[a].shape):
            cols = self.ins[a].shape[2]
            return self.outs[a].at[:, rows, pl.ds(pl.multiple_of(index * cols, cols), cols)]
        return self.outs[a].at[index, :, rows, :]

    def _remote(self, src, dst, a, slot, to):
        return pltpu.make_async_remote_copy(src_ref=src, dst_ref=dst, send_sem=self.send_sems.at[a, slot],
                                            recv_sem=self.recv_sems.at[a, slot], device_id=to, device_id_type=MESH)

    def _own(self, a):
        return self._remote(self.ins[a], self._block(a, self.me), a, 6, self.sibling)

    def _sent(self, a, axis):
        rows = self._rows(a, self.c)
        return self._remote(self.ins[a].at[:, rows, :], self._block(a, self.me, rows), a, axis,
                            (*self.across[axis], self.c))

    def _landed(self, a, axis):
        block = self._block(a, self.across[axis], self._rows(a, self.c))
        return self._remote(block, block, a, axis, (*self.across[axis], self.c))

    def _relayed(self, a, part, incoming):
        source = self.across[2] if incoming else self.across[part]
        block = self._block(a, source, self._rows(a, self.c, part))
        return self._remote(block, block, a, (2, 7)[part], (*self.across[1 - part], self.c))

    def _passed_on(self, a, source, which):
        block = self._block(a, self.across[source], self._rows(a, which))
        return self._remote(block, block, a, 3 + source, self.sibling)

    def start(self):
        for a in range(len(self.ins)):
            self._own(a).start()
        for a in range(len(self.ins)):
            for axis in range(2):
                self._sent(a, axis).start()

    def middle(self):
        for a in range(len(self.ins)):
            for axis in range(2):
                self._landed(a, axis).wait_recv()
                if self._has_part(a, axis):
                    self._relayed(a, axis, incoming=False).start()
                self._passed_on(a, axis, self.c).start()

    def late(self):
        for a in range(len(self.ins)):
            for part in range(2):
                if self._has_part(a, part):
                    self._relayed(a, part, incoming=True).wait_recv()
            self._passed_on(a, 2, self.c).start()

    def finish(self):
        n = len(self.ins)
        for a in range(n):
            for source in range(3):
                self._passed_on(a, source, 1 - self.c).wait_recv()
            self._own(a).wait_recv()
        for a in range(n):
            self._own(a).wait_send()
            for axis in range(2):
                self._sent(a, axis).wait_send()
                if self._has_part(a, axis):
                    self._relayed(a, axis, incoming=False).wait_send()
            for source in range(3):
                self._passed_on(a, source, self.c).wait_send()


class _HalfExchange:
    @staticmethod
    def out_shapes(grads):
        return [jax.ShapeDtypeStruct((N_CHIPS, g.shape[1] // 2, g.shape[2]), g.dtype) for g in grads]

    @staticmethod
    def sems(n):
        return [pltpu.SemaphoreType.DMA((n,)), pltpu.SemaphoreType.DMA((n,))]

    def __init__(self, ins, outs, send_sems, recv_sems):
        self.ins, self.outs, self.send_sems, self.recv_sems = ins, outs, send_sems, recv_sems

    def _copies(self):
        x, y, c = _place()
        for a in range(len(self.ins)):
            h = self.ins[a].shape[1] // 2
            yield pltpu.make_async_remote_copy(
                src_ref=self.ins[a].at[:, pl.ds((1 - c) * h, h), :], dst_ref=self.outs[a],
                send_sem=self.send_sems.at[a], recv_sem=self.recv_sems.at[a], device_id=(x, y, 1 - c), device_id_type=MESH)

    def start(self):
        for cp in self._copies():
            cp.start()

    def finish(self):
        for cp in self._copies():
            cp.wait()


class _GatherColumns(_Gather):
    @staticmethod
    def out_shapes(shards):
        return [jax.ShapeDtypeStruct(s.shape[:2] + (N_CHIPS * s.shape[2],), s.dtype) for s in shards]


def _row_tile(rows, cols, budget=2 << 20):
    best = 16
    for tr in range(16, rows + 1, 16):
        if rows % tr == 0 and tr * cols * 4 <= budget:
            best = tr
    return best


def add_sibling_half(name, grad, got, place):
    _, rows, cols = grad.shape
    h = rows // 2
    tr = _row_tile(h, cols)
    per_half = h // tr

    def body(place_ref, g_ref, o_ref, own_ref, out16_ref):
        total = g_ref[...] + o_ref[...].astype(F32)
        out16_ref[...] = total.astype(BF16)

        @pl.when(pl.program_id(1) == place_ref[1])
        def _():
            own_ref[...] = total

    return pl.pallas_call(
        body, name=name,
        grid_spec=pltpu.PrefetchScalarGridSpec(
            num_scalar_prefetch=1, grid=(per_half, N_CHIPS),
            in_specs=[pl.BlockSpec((None, tr, cols), lambda r, j, pr: (j, pr[0] * per_half + r, 0)),
                      pl.BlockSpec((None, tr, cols), lambda r, j, pr: (j, r, 0))],
            out_specs=[pl.BlockSpec((tr, cols), lambda r, j, pr: (r, 0)),
                       pl.BlockSpec((None, tr, cols), lambda r, j, pr: (j, r, 0))]),
        out_shape=[jax.ShapeDtypeStruct((h, cols), F32), jax.ShapeDtypeStruct((N_CHIPS, h, cols), BF16)],
        compiler_params=_params(("arbitrary", "arbitrary")),
    )(place, grad, got)


class _ChipExchange:
    SLOTS = 6
    ROW_ALIGN = 16
    PHASES_AT = {"middle": (3, 8)}

    @staticmethod
    def out_shapes(partials):
        return ([jax.ShapeDtypeStruct((3,) + p.shape[1:], p.dtype) for p in partials]
                + [jax.ShapeDtypeStruct(p.shape[1:], p.dtype) for p in partials])

    @staticmethod
    def sems(n):
        return [pltpu.SemaphoreType.DMA((n, _ChipExchange.SLOTS)), pltpu.SemaphoreType.DMA((n, _ChipExchange.SLOTS))]

    def __init__(self, ins, outs, send_sems, recv_sems):
        n = len(ins)
        self.ins, self.outs, self.relays, self.send_sems, self.recv_sems = ins, outs[:n], outs[n:], send_sems, recv_sems

    def _copy(self, a, slot):
        x, y, c = _place()
        across = _other_chips(x, y)
        rows = self.ins[a].shape[1]
        first = (rows // 2) // self.ROW_ALIGN * self.ROW_ALIGN
        part = (pl.ds(0, first), pl.ds(first, rows - first))
        if slot < 2:
            px, py = across[slot]
            src, dst, to = self.ins[a].at[2 * px + py], self.outs[a].at[slot], across[slot]
        elif slot < 4:
            px, py = across[2]
            src, dst, to = self.ins[a].at[2 * px + py, part[slot - 2], :], self.relays[a].at[part[slot - 2], :], across[slot - 2]
        else:
            src, dst, to = self.relays[a].at[part[slot - 4], :], self.outs[a].at[2, part[slot - 4], :], across[5 - slot]
        return pltpu.make_async_remote_copy(src_ref=src, dst_ref=dst, send_sem=self.send_sems.at[a, slot],
                                            recv_sem=self.recv_sems.at[a, slot], device_id=(*to, c), device_id_type=MESH)

    def start(self):
        for slot in (2, 3, 0, 1):
            for a in range(len(self.ins)):
                self._copy(a, slot).start()

    def middle(self):
        for part in range(2):
            for a in range(len(self.ins)):
                self._copy(a, 2 + part).wait_recv()
                self._copy(a, 4 + part).start()

    def finish(self):
        for a in range(len(self.ins)):
            for slot in (0, 1, 4, 5):
                self._copy(a, slot).wait_recv()
        for a in range(len(self.ins)):
            for slot in range(self.SLOTS):
                self._copy(a, slot).wait_send()


def add_chip_blocks(name, own, got):
    h, cols = own.shape
    tr = _row_tile(h, cols)

    def body(p_ref, g0_ref, g1_ref, g2_ref, out_ref):
        out_ref[...] = ((p_ref[...] + g0_ref[...].astype(F32)) + g1_ref[...].astype(F32)) + g2_ref[...].astype(F32)

    def got_spec(slot):
        return pl.BlockSpec((None, tr, cols), lambda r: (slot, r, 0))

    return pl.pallas_call(
        body, name=name, grid=(h // tr,),
        in_specs=[_rows(tr, cols), got_spec(0), got_spec(1), got_spec(2)], out_specs=_rows(tr, cols),
        out_shape=jax.ShapeDtypeStruct((h, cols), F32),
        compiler_params=_params(("arbitrary",)),
    )(own, got, got, got)


class _SiblingSwap:
    @staticmethod
    def out_shapes(halves):
        return [jax.ShapeDtypeStruct(v.shape, v.dtype) for v in halves]

    @staticmethod
    def sems(n):
        return [pltpu.SemaphoreType.DMA((n,)), pltpu.SemaphoreType.DMA((n,))]

    def __init__(self, ins, outs, send_sems, recv_sems):
        self.ins, self.outs, self.send_sems, self.recv_sems = ins, outs, send_sems, recv_sems

    def _copies(self):
        x, y, c = _place()
        for a in range(len(self.ins)):
            yield pltpu.make_async_remote_copy(
                src_ref=self.ins[a], dst_ref=self.outs[a], send_sem=self.send_sems.at[a], recv_sem=self.recv_sems.at[a],
                device_id=(x, y, 1 - c), device_id_type=MESH)

    def start(self):
        for cp in self._copies():
            cp.start()

    def finish(self):
        for cp in self._copies():
            cp.wait()


def all_reduce_small(pack):
    rows = pack.shape[0]

    def body(pack_ref, out_ref, gathered, send_sems, recv_sems):
        x, y, c = _place()
        me = 4 * x + 2 * y + c
        gathered[me] = pack_ref[...]
        copies = []
        for rel in range(1, 8):
            fx, fy, fc = (rel >> 2) & 1, (rel >> 1) & 1, rel & 1
            peer = (x ^ fx, y ^ fy, c ^ fc)
            cp = pltpu.make_async_remote_copy(
                src_ref=pack_ref, dst_ref=gathered.at[me], send_sem=send_sems.at[rel - 1], recv_sem=recv_sems.at[rel - 1],
                device_id=peer, device_id_type=MESH)
            cp.start()
            copies.append(cp)
        for rel in range(1, 8):
            fx, fy, fc = (rel >> 2) & 1, (rel >> 1) & 1, rel & 1
            src = 4 * (x ^ fx) + 2 * (y ^ fy) + (c ^ fc)
            pltpu.make_async_remote_copy(
                src_ref=pack_ref, dst_ref=gathered.at[src], send_sem=send_sems.at[rel - 1], recv_sem=recv_sems.at[rel - 1],
                device_id=(x, y, c), device_id_type=MESH).wait_recv()
        for cp in copies:
            cp.wait_send()
        total = gathered[0]
        for dev in range(1, 8):
            total = total + gathered[dev]
        out_ref[...] = total

    return pl.pallas_call(
        body, name="all_reduce_small",
        in_specs=[pl.BlockSpec(memory_space=pltpu.VMEM)], out_specs=pl.BlockSpec(memory_space=pltpu.VMEM),
        out_shape=jax.ShapeDtypeStruct((rows, D), F32),
        scratch_shapes=[pltpu.VMEM((8, rows, D), F32), pltpu.SemaphoreType.DMA((7,)), pltpu.SemaphoreType.DMA((7,))],
    )(pack)


def _adamw_update(w, g, m, v):
    nm = ADAM_B1 * m + (1.0 - ADAM_B1) * g
    nv = ADAM_B2 * v + (1.0 - ADAM_B2) * (g * g)
    m_hat = nm / (1.0 - ADAM_B1 ** ADAM_STEP)
    v_hat = nv / (1.0 - ADAM_B2 ** ADAM_STEP)
    delta = -ADAM_LR * (m_hat / (jnp.sqrt(v_hat) + ADAM_EPS) + ADAM_WD * w)
    return delta, nm, nv


def adamw(name, w, g, m, v):
    def body(w_ref, g_ref, m_ref, v_ref, d_ref, nm_ref, nv_ref):
        d_ref[...], nm_ref[...], nv_ref[...] = _adamw_update(w_ref[...], g_ref[...], m_ref[...], v_ref[...])

    out = jax.ShapeDtypeStruct(w.shape, F32)
    return pl.pallas_call(body, name=name, out_shape=[out] * 3, compiler_params=_params())(w, g, m, v)


def adamw_halves(name, w, mine, theirs, m, v, core):
    h, cols = mine.shape
    tr = _row_tile(h, cols)
    per_half = h // tr

    def body(core_ref, w_ref, mine_ref, theirs_ref, m_ref, v_ref, g_ref, d_ref, nm_ref, nv_ref):
        g = jnp.where(pl.program_id(0) == core_ref[0], mine_ref[...], theirs_ref[...])
        g_ref[...] = g
        d_ref[...], nm_ref[...], nv_ref[...] = _adamw_update(w_ref[...], g, m_ref[...], v_ref[...])

    full = pl.BlockSpec((tr, cols), lambda hh, r, cr: (hh * per_half + r, 0))
    mine_spec = pl.BlockSpec((tr, cols), lambda hh, r, cr: (jnp.where(hh == cr[0], r, 0), 0))
    theirs_spec = pl.BlockSpec((tr, cols), lambda hh, r, cr: (jnp.where(hh == cr[0], 0, r), 0))
    out = jax.ShapeDtypeStruct((2 * h, cols), F32)
    return pl.pallas_call(
        body, name=name,
        grid_spec=pltpu.PrefetchScalarGridSpec(
            num_scalar_prefetch=1, grid=(2, per_half),
            in_specs=[full, mine_spec, theirs_spec, full, full], out_specs=[full] * 4),
        out_shape=[out] * 4,
        compiler_params=_params(("arbitrary", "arbitrary")),
    )(core, w, mine, theirs, m, v)


BIG = ("w_in", "w_conv_out", "w_pool", "w_kv", "w_xattn_out", "w_out", "w_gate", "w_up", "w_down")


def kernel(x, mem, norm_mix, w_in, conv_w, w_conv_out, w_pool, pool_scale, norm_mem, w_kv, w_xattn_out, w_out, norm_ffn, w_gate, w_up, w_down, norm_final, loss_target, m_norm_mix, m_w_in, m_conv_w, m_w_conv_out, m_w_pool, m_pool_scale, m_norm_mem, m_w_kv, m_w_xattn_out, m_w_out, m_norm_ffn, m_w_gate, m_w_up, m_w_down, m_norm_final, v_norm_mix, v_w_in, v_conv_w, v_w_conv_out, v_w_pool, v_pool_scale, v_norm_mem, v_w_kv, v_w_xattn_out, v_w_out, v_norm_ffn, v_w_gate, v_w_up, v_w_down, v_norm_final):
    t_len = x.shape[1]
    xi, yi, ci = lax.axis_index("x"), lax.axis_index("y"), lax.axis_index("c")
    chip = 2 * xi + yi
    core_arr = jnp.reshape(ci, (1,)).astype(jnp.int32)
    place_arr = jnp.stack([ci, chip]).astype(jnp.int32)

    conv_pad = jnp.concatenate([conv_w, jnp.zeros((1, 13, 256), F32)], axis=1)
    def t2(w):
        return jnp.swapaxes(w, 1, 2)

    (g_in,) = exchange_call("gather_w_in", _GatherColumns, [w_in.astype(BF16)])
    w_in_f = g_in[0]

    x2d = x[0]
    tgt = loss_target[0]
    (proj, h), (g_kv, g_conv_w, g_co, g_xo, g_out, g_pool, g_gate) = proj_fwd(
        x2d, norm_mix, w_in_f,
        carry=(_Gather, [w_kv.astype(BF16), conv_pad,
                         w_conv_out.astype(BF16), w_xattn_out.astype(BF16), w_out.astype(BF16),
                         w_pool[0].astype(BF16),
                         t2(w_gate).astype(BF16)]))
    w_kv4 = g_kv.reshape(N_CHIPS, D, D // 2)
    conv_full = jnp.transpose(g_conv_w[:, 0, 0:8, :], (1, 0, 2)).reshape(8, D)
    w_co_f, w_xo_f, w_out_f = g_co.reshape(D, D), g_xo.reshape(D, D), g_out.reshape(D, D)
    w_pool_f = jnp.transpose(g_pool, (1, 0, 2, 3)).reshape(GROUPS, GROUP_DIM, GROUP_DIM)
    memn, k, v = kv_fwd(mem[0], norm_mem, w_kv4)
    (a, pooled, ya, pp, yx, o, probs, x1), (g_up, g_down) = mixer_fwd(
        proj, x2d, conv_full, w_co_f, w_pool_f, pool_scale, k, v, w_xo_f, w_out_f,
        carry=(_Gather, [t2(w_up).astype(BF16), w_down.astype(BF16)]))
    w_gate_f, w_up_f, w_down_f = g_gate.reshape(D_FF, D), g_up.reshape(D_FF, D), g_down.reshape(D_FF, D)
    gate, up, dx2, stat_f = ffn_fwd(x1, tgt, norm_ffn, w_gate_f, w_up_f, w_down_f, norm_final.reshape(1, D))

    def by_chip(pair):
        return tuple(gw.reshape(N_CHIPS, gw.shape[0] // N_CHIPS, gw.shape[1]) for gw in pair)

    def chip_partials(names, grads, got):
        return [add_sibling_half("add_sibling_" + n, g32, o_, place_arr) for n, (g32, _), o_ in zip(names, grads, got)]

    def chip_sums(names, partials, got):
        return [add_chip_blocks("add_chips_" + n, own, g2) for n, (own, _), g2 in zip(names, partials, got)]

    dx1, dgate, dup, act, h2, stat_b1 = ffn_bwd(dx2, x1, gate, up, norm_ffn, w_gate_f, w_up_f, w_down_f)
    gw_gate = by_chip(matmul_tn("grad_w_gate", dgate, h2, 512))
    gw_up, got_gate = matmul_tn("grad_w_up", dup, h2, 512, carry=(_HalfExchange, [gw_gate[1]]))
    gw_up = by_chip(gw_up)
    gw_down, got_up = matmul_tn("grad_w_down", act, dx2, 512, carry=(_HalfExchange, [gw_up[1]]))
    gw_down = by_chip(gw_down)
    got_down = exchange_call("exchange_halves_w_down", _HalfExchange, [gw_down[1]])
    names_ffn = ("w_gate", "w_up", "w_down")
    part_ffn = chip_partials(names_ffn, [gw_gate, gw_up, gw_down], list(got_gate) + list(got_up) + list(got_down))

    (dproj, merged, dya, dpp, dyx, dk, dv, stat_b2), got_ffn = mixer_bwd(
        dx1, proj, ya, pp, yx, probs, conv_full, w_co_f, w_pool_f, pool_scale, k, v, w_xo_f, w_out_f,
        carry=(_ChipExchange, [p16 for _, p16 in part_ffn]))
    gw_kv32, gw_kv16, stat_kv = kv_bwd(dk, dv, memn, mem[0], norm_mem, w_kv4)
    gw_pool = tuple(jnp.transpose(gw.reshape(GROUPS, N_CHIPS, 64, GROUP_DIM), (1, 0, 2, 3)).reshape(N_CHIPS, 256, GROUP_DIM)
                    for gw in grad_w_pool(pooled, dpp))
    gw_co, got_kv_pool = matmul_tn("grad_w_conv_out", a, dya, 1024, carry=(_HalfExchange, [gw_kv16, gw_pool[1]]))
    gw_co = by_chip(gw_co)
    gw_xo, got_co = matmul_tn("grad_w_xattn_out", o, dyx, 1024, carry=(_HalfExchange, [gw_co[1]]))
    gw_xo = by_chip(gw_xo)
    gw_out, got_xo = matmul_tn("grad_w_out", merged, dx1, 1024, carry=(_HalfExchange, [gw_xo[1]]))
    gw_out = by_chip(gw_out)
    got_out = exchange_call("exchange_halves_w_out", _HalfExchange, [gw_out[1]])
    names_mix = ("w_kv", "w_pool", "w_conv_out", "w_xattn_out", "w_out")
    part_mix = chip_partials(names_mix, [(gw_kv32, gw_kv16), gw_pool, gw_co, gw_xo, gw_out],
                             list(got_kv_pool) + list(got_co) + list(got_xo) + list(got_out))

    gw_in, got_mix = matmul_tn("grad_w_in", h, dproj, 2048, col_blocks=N_CHIPS,
                               carry=(_ChipExchange, [p16 for _, p16 in part_mix]))
    part_in = chip_partials(("w_in",), [gw_in], exchange_call("exchange_halves_w_in", _HalfExchange, [gw_in[1]]))
    mine_early = chip_sums(names_ffn + names_mix, part_ffn + part_mix, list(got_ffn) + list(got_mix))
    (grad_x, stat_b3), (got_in, theirs_early) = in_bwd(
        dproj, w_in_f, x2d, dx1, norm_mix,
        carry=[(_ChipExchange, [p16 for _, p16 in part_in]), (_SiblingSwap, mine_early)])
    mine_in = chip_sums(("w_in",), part_in, got_in)
    theirs_in = exchange_call("swap_halves_w_in", _SiblingSwap, mine_in)
    reduced = dict(zip(names_ffn + names_mix + ("w_in",),
                       zip(mine_early + mine_in, list(theirs_early) + list(theirs_in))))
    mine = [reduced[n][0] for n in BIG]
    theirs = [reduced[n][1] for n in BIG]

    pack = jnp.concatenate([stat_b3[0:1], stat_b2[1:2], stat_kv[0:1], stat_b1[0:1], stat_f[0:1], stat_b2[5:8],
                            stat_f[1:2], jnp.zeros((7, D), F32)], axis=0)
    total = all_reduce_small(pack)
    loss = jnp.sum(total[8])
    g_conv_full = total[5:8]
    g_conv = lax.dynamic_slice_in_dim(g_conv_full, chip * 256, 256, axis=1)

    given = dict(w_in=(w_in, m_w_in, v_w_in), w_conv_out=(w_conv_out, m_w_conv_out, v_w_conv_out),
                 w_pool=(w_pool, m_w_pool, v_w_pool), w_kv=(w_kv, m_w_kv, v_w_kv),
                 w_xattn_out=(w_xattn_out, m_w_xattn_out, v_w_xattn_out), w_out=(w_out, m_w_out, v_w_out),
                 w_gate=(w_gate, m_w_gate, v_w_gate), w_up=(w_up, m_w_up, v_w_up), w_down=(w_down, m_w_down, v_w_down))
    out_g, out_d, out_m, out_v = {}, {}, {}, {}
    for n, mine_n, theirs_n in zip(BIG, mine, theirs):
        transposed = n in ("w_gate", "w_up")
        rows2d = (2 * mine_n.shape[0], mine_n.shape[1])
        w_, m_, v_ = ((t2(t) if transposed else t).reshape(rows2d) for t in given[n])
        res = adamw_halves("adamw_" + n, w_, mine_n, theirs_n, m_, v_, core_arr)
        if transposed:
            res = [t2(t.reshape(1, D_FF // N_CHIPS, D)) for t in res]
        out_g[n], out_d[n], out_m[n], out_v[n] = (t.reshape(given[n][0].shape) for t in res)

    def small_pack(vals, conv_part):
        conv_rows = jnp.concatenate([conv_part.reshape(3, 256), jnp.zeros((3, D - 256), F32)], axis=1)
        return jnp.concatenate([val.reshape(1, D) for val in vals] + [conv_rows], axis=0)

    sw = small_pack([norm_mix, pool_scale, norm_mem, norm_ffn, norm_final], conv_w)
    sm = small_pack([m_norm_mix, m_pool_scale, m_norm_mem, m_norm_ffn, m_norm_final], m_conv_w)
    sv = small_pack([v_norm_mix, v_pool_scale, v_norm_mem, v_norm_ffn, v_norm_final], v_conv_w)
    sg = small_pack([total[r] for r in range(5)], g_conv)
    sd, snm, snv = adamw("adamw_small", sw, sg, sm, sv)
    small_names = ("norm_mix", "pool_scale", "norm_mem", "norm_ffn", "norm_final")
    small_shapes = dict(norm_mix=(1, D), pool_scale=(1, D), norm_mem=(1, D), norm_ffn=(1, D), norm_final=(D,))
    for r, n in enumerate(small_names):
        out_g[n], out_d[n], out_m[n], out_v[n] = (t[r].reshape(small_shapes[n]) for t in (sg, sd, snm, snv))
    out_g["conv_w"], out_d["conv_w"], out_m["conv_w"], out_v["conv_w"] = (
        t[5:8, 0:256].reshape(1, 3, 256) for t in (sg, sd, snm, snv))

    order = ("norm_mix", "w_in", "conv_w", "w_conv_out", "w_pool", "pool_scale", "norm_mem", "w_kv", "w_xattn_out",
             "w_out", "norm_ffn", "w_gate", "w_up", "w_down", "norm_final")
    return (loss, grad_x.reshape(1, t_len, D), *[out_g[n] for n in order], *[out_d[n] for n in order],
            *[out_m[n] for n in order], *[out_v[n] for n in order])
```

```python
import functools

import jax
import jax.numpy as jnp
from jax import lax
from jax.experimental import pallas as pl
from jax.experimental.pallas import tpu as pltpu

F32 = jnp.float32
BF16 = jnp.bfloat16
MESH = pl.DeviceIdType.MESH

D = 1024
N_MEM = 256
HEADS = 4
HEAD_DIM = 256
GROUPS = 4
GROUP_DIM = 256
POOL_WINDOWS = (2, 4, 8, 16)
D_FF = 2816
D_IN = 8192
N_CHIPS = 4
EPS = 1e-6
HALO = 16
POOL_PAD = 128
ATT_SCALE = HEAD_DIM ** -0.5

ADAM_LR = 0.001
ADAM_B1 = 0.9
ADAM_B2 = 0.999
ADAM_EPS = 1e-08
ADAM_WD = 0.01
ADAM_STEP = 10

VMEM_LIMIT = 56 * 1024 * 1024

O_BA, O_CA, O_UA, O_UP, O_QX, O_GA, O_GP, O_GX = (k * D for k in range(8))

NT_DIMS = (((1,), (1,)), ((), ()))
TN_DIMS = (((0,), (0,)), ((), ()))


def _dot(a, b):
    return jnp.dot(a, b, preferred_element_type=F32)


def _dot_nt(a, b):
    return lax.dot_general(a, b, NT_DIMS, preferred_element_type=F32)


def _dot_tn(a, b):
    return lax.dot_general(a, b, TN_DIMS, preferred_element_type=F32)


def _sigmoid(z):
    return pl.reciprocal(1.0 + jnp.exp(-z), approx=True)


def _params(semantics=None):
    return pltpu.CompilerParams(dimension_semantics=semantics, vmem_limit_bytes=VMEM_LIMIT)


def _resident(shape):
    zeros = (0,) * len(shape)
    return pl.BlockSpec(shape, lambda *_: zeros, pipeline_mode=pl.Buffered(1))


def _const(shape):
    zeros = (0,) * len(shape)
    return pl.BlockSpec(shape, lambda *_: zeros)


def _rows(tm, width):
    return pl.BlockSpec((tm, width), lambda i: (i, 0))


def _inv_count(tile, tm, window):
    t = tile * tm + lax.broadcasted_iota(jnp.int32, (tm, 1), 0)
    return 1.0 / jnp.minimum(t + 1, window).astype(F32)


def _carried_call(body, name, grid, in_specs, out_specs, out_shape, scratch_shapes, semantics, args, carry):
    if carry is None:
        res = pl.pallas_call(body, name=name, grid=grid, in_specs=in_specs, out_specs=out_specs, out_shape=out_shape,
                             scratch_shapes=scratch_shapes, compiler_params=_params(semantics))(*args)
        return res, []
    carries = [carry] if isinstance(carry, tuple) else list(carry)
    comm_args = [arr for _, arrs in carries for arr in arrs]
    n, n_in, n_out, n_scratch = len(comm_args), len(in_specs), len(out_specs), len(scratch_shapes)
    shapes_of = [cls.out_shapes(arrs) for cls, arrs in carries]
    comm_shapes = [s for shapes in shapes_of for s in shapes]
    m = len(comm_shapes)
    comm_sems = [s for cls, arrs in carries for s in cls.sems(len(arrs))]

    def carrying(*refs):
        ins, comm_ins = refs[:n_in], refs[n_in:n_in + n]
        outs, comm_outs = refs[n_in + n:n_in + n + n_out], refs[n_in + n + n_out:n_in + n + n_out + m]
        scratch, sems = refs[n_in + n + n_out + m:n_in + n + n_out + m + n_scratch], refs[n_in + n + n_out + m + n_scratch:]
        steps = [pl.program_id(d) for d in range(len(grid))]
        first = functools.reduce(jnp.logical_and, [s == 0 for s in steps])
        last = functools.reduce(jnp.logical_and, [s == g - 1 for s, g in zip(steps, grid)])

        def exchanges():
            at_in = at_out = 0
            for k, (cls, arrs) in enumerate(carries):
                yield cls(comm_ins[at_in:at_in + len(arrs)], comm_outs[at_out:at_out + len(shapes_of[k])],
                          sems[2 * k], sems[2 * k + 1])
                at_in, at_out = at_in + len(arrs), at_out + len(shapes_of[k])

        @pl.when(first)
        def _():
            for exchange in exchanges():
                exchange.start()

        linear, total = 0, 1
        for s, g in zip(steps, grid):
            linear, total = linear * g + s, total * g
        for k, (cls, _) in enumerate(carries):
            for phase, (num, den) in getattr(cls, "PHASES_AT", {}).items():
                @pl.when(linear == (num * total) // den)
                def _(k=k, phase=phase):
                    getattr(list(exchanges())[k], phase)()

        body(*ins, *outs, *scratch)

        @pl.when(last)
        def _():
            for exchange in exchanges():
                exchange.finish()

    res = pl.pallas_call(
        carrying, name=name, grid=grid, in_specs=list(in_specs) + _any_specs(n), out_specs=list(out_specs) + _any_specs(m),
        out_shape=list(out_shape) + comm_shapes, scratch_shapes=list(scratch_shapes) + comm_sems,
        compiler_params=_params(semantics))(*args, *comm_args)
    comm_res, at = [], n_out
    for (_, arrs), shapes in zip(carries, shapes_of):
        comm_res.append(res[at:at + len(arrs)])
        at += len(shapes)
    return res[:n_out], (comm_res[0] if isinstance(carry, tuple) else comm_res)


def proj_fwd(x, g_mix, w_in, carry=None):
    t_len = x.shape[0]
    tm = min(1024, t_len)
    tn = D_IN // N_CHIPS

    def body(x_ref, g_ref, w_ref, proj_ref, h_ref):
        @pl.when(pl.program_id(1) == 0)
        def _():
            xv = x_ref[...]
            r = lax.rsqrt(jnp.mean(xv * xv, axis=-1, keepdims=True) + EPS)
            h_ref[...] = (xv * r * g_ref[...]).astype(BF16)

        proj_ref[...] = _dot(h_ref[...], w_ref[...]).astype(BF16)

    return _carried_call(
        body, "proj_fwd", (t_len // tm, N_CHIPS),
        in_specs=[pl.BlockSpec((tm, D), lambda i, j: (i, 0)),
                  pl.BlockSpec((1, D), lambda i, j: (0, 0)),
                  pl.BlockSpec((D, tn), lambda i, j: (0, j))],
        out_specs=[pl.BlockSpec((tm, tn), lambda i, j: (i, j)),
                   pl.BlockSpec((tm, D), lambda i, j: (i, 0))],
        out_shape=[jax.ShapeDtypeStruct((t_len, D_IN), BF16), jax.ShapeDtypeStruct((t_len, D), BF16)],
        scratch_shapes=[], semantics=("arbitrary", "arbitrary"), args=(x, g_mix, w_in), carry=carry)


def kv_fwd(mem, g_mem, w_kv4):
    half = D // 2

    def body(mem_ref, g_ref, w_ref, memn_ref, k_ref, v_ref):
        mv = mem_ref[...]
        r = lax.rsqrt(jnp.mean(mv * mv, axis=-1, keepdims=True) + EPS)
        mn = (mv * r * g_ref[...]).astype(BF16)
        memn_ref[...] = mn
        k_ref[:, 0:half] = _dot(mn, w_ref[0]).astype(BF16)
        k_ref[:, half:D] = _dot(mn, w_ref[1]).astype(BF16)
        v_ref[:, 0:half] = _dot(mn, w_ref[2]).astype(BF16)
        v_ref[:, half:D] = _dot(mn, w_ref[3]).astype(BF16)

    out = jax.ShapeDtypeStruct((N_MEM, D), BF16)
    return pl.pallas_call(body, name="kv_fwd", out_shape=[out, out, out], compiler_params=_params())(mem, g_mem, w_kv4)


def _softmax_rows(s):
    m = jnp.max(s, axis=-1, keepdims=True)
    e = jnp.exp(s - m)
    return e * pl.reciprocal(jnp.sum(e, axis=-1, keepdims=True), approx=True)


def _window_bands(tm, causal):
    t = lax.broadcasted_iota(jnp.int32, (tm, tm + POOL_PAD), 0)
    s = lax.broadcasted_iota(jnp.int32, (tm, tm + POOL_PAD), 1)
    d = (t + POOL_PAD - s) if causal else (s - t)
    return jnp.stack([((d >= 0) & (d < w)).astype(BF16) for w in POOL_WINDOWS])


def mixer_fwd(proj, x, conv_w8, w_co, w_pool, pool_scale, k, v, w_xo, w_out, carry=None):
    t_len = x.shape[0]
    tm = min(256, t_len)

    def body(proj_ref, x_ref, cw_ref, wco_ref, wpool_ref, ps_ref, k_ref, v_ref, wxo_ref, wout_ref,
             a_ref, pooled_ref, ya_ref, pp_ref, yx_ref, o_ref, p_ref, merged_ref, x1_ref, cu_ext, up_ext):
        i = pl.program_id(0)

        @pl.when(i == 0)
        def _():
            cu_ext[0:HALO, :] = jnp.zeros((HALO, D), F32)
            up_ext[0:HALO, :] = jnp.zeros((HALO, D), F32)

        cu = proj_ref[:, O_CA:O_CA + D].astype(F32) * proj_ref[:, O_UA:O_UA + D].astype(F32)
        cu_ext[HALO:HALO + tm, :] = cu
        conv = (cw_ref[2:3, :] * cu + cw_ref[1:2, :] * cu_ext[HALO - 1:HALO - 1 + tm, :]
                + cw_ref[0:1, :] * cu_ext[HALO - 2:HALO - 2 + tm, :])
        a = (proj_ref[:, O_BA:O_BA + D].astype(F32) * conv).astype(BF16)
        a_ref[...] = a
        ya = _dot(a, wco_ref[...])
        ya_ref[...] = ya.astype(BF16)

        up_ext[HALO:HALO + tm, :] = proj_ref[:, O_UP:O_UP + D].astype(F32)
        for g, window in enumerate(POOL_WINDOWS):
            cols = slice(g * GROUP_DIM, (g + 1) * GROUP_DIM)
            tok = up_ext[HALO:HALO + tm, cols]
            acc = tok
            for j in range(1, window):
                acc = acc + up_ext[HALO - j:HALO - j + tm, cols]
            pooled = (acc * _inv_count(i, tm, window) - tok).astype(BF16)
            pooled_ref[:, cols] = pooled
            pp_ref[:, cols] = _dot(pooled, wpool_ref[g]).astype(BF16)

        for hd in range(HEADS):
            cols = slice(hd * HEAD_DIM, (hd + 1) * HEAD_DIM)
            q = proj_ref[:, O_QX + hd * HEAD_DIM:O_QX + (hd + 1) * HEAD_DIM]
            p = _softmax_rows(_dot_nt(q, k_ref[:, cols]) * ATT_SCALE).astype(BF16)
            p_ref[:, hd * N_MEM:(hd + 1) * N_MEM] = p
            o_ref[:, cols] = _dot(p, v_ref[:, cols]).astype(BF16)
        yx = _dot(o_ref[...], wxo_ref[...])
        yx_ref[...] = yx.astype(BF16)

        merged = (_sigmoid(proj_ref[:, O_GA:O_GA + D].astype(F32)) * ya
                  + _sigmoid(proj_ref[:, O_GP:O_GP + D].astype(F32)) * (pp_ref[...].astype(F32) * ps_ref[...])
                  + _sigmoid(proj_ref[:, O_GX:O_GX + D].astype(F32)) * yx)
        merged_ref[...] = merged.astype(BF16)
        x1_ref[...] = x_ref[...] + _dot(merged_ref[...], wout_ref[...])

        cu_ext[0:HALO, :] = cu_ext[tm:tm + HALO, :]
        up_ext[0:HALO, :] = up_ext[tm:tm + HALO, :]

    act = jax.ShapeDtypeStruct((t_len, D), BF16)
    return _carried_call(
        body, "mixer_fwd", (t_len // tm,),
        in_specs=[_rows(tm, D_IN), _rows(tm, D), _resident((8, D)), _resident((D, D)),
                  _resident((GROUPS, GROUP_DIM, GROUP_DIM)), _resident((1, D)),
                  _resident((N_MEM, D)), _resident((N_MEM, D)), _resident((D, D)), _resident((D, D))],
        out_specs=[_rows(tm, D)] * 9,
        out_shape=[act] * 6 + [jax.ShapeDtypeStruct((t_len, HEADS * N_MEM), BF16), act, jax.ShapeDtypeStruct((t_len, D), F32)],
        scratch_shapes=[pltpu.VMEM((tm + HALO, D), F32), pltpu.VMEM((tm + HALO, D), F32)],
        semantics=("arbitrary",), args=(proj, x, conv_w8, w_co, w_pool, pool_scale, k, v, w_xo, w_out), carry=carry)


def ffn_fwd(x1, target, g_ffn, w_gate, w_up, w_down, g_final):
    t_len = x1.shape[0]
    tm = min(512, t_len)

    def body(x1_ref, tgt_ref, g_ref, wg_ref, wu_ref, wd_ref, gf_ref, gate_ref, up_ref, dx2_ref, stat_ref):
        @pl.when(pl.program_id(0) == 0)
        def _():
            stat_ref[...] = jnp.zeros((8, D), F32)

        x1v = x1_ref[...]
        r2 = lax.rsqrt(jnp.mean(x1v * x1v, axis=-1, keepdims=True) + EPS)
        h2 = (x1v * r2 * g_ref[...]).astype(BF16)
        gate = _dot_nt(h2, wg_ref[...])
        up = _dot_nt(h2, wu_ref[...])
        gate_ref[...] = gate.astype(BF16)
        up_ref[...] = up.astype(BF16)
        act = (gate * _sigmoid(gate) * up).astype(BF16)
        x2 = x1v + _dot(act, wd_ref[...])
        r3 = lax.rsqrt(jnp.mean(x2 * x2, axis=-1, keepdims=True) + EPS)
        xh = x2 * r3
        diff = xh * gf_ref[...] - tgt_ref[...]
        dy = diff * (1.0 / D)
        stat_ref[0:1, :] += jnp.sum(dy * xh, axis=0, keepdims=True)
        stat_ref[1:2, :] += (0.5 / D) * jnp.sum(diff * diff, axis=0, keepdims=True)
        dxh = dy * gf_ref[...]
        dx2_ref[...] = r3 * (dxh - xh * jnp.mean(dxh * xh, axis=-1, keepdims=True))

    return pl.pallas_call(
        body, name="ffn_fwd",
        grid=(t_len // tm,),
        in_specs=[_rows(tm, D), _rows(tm, D), _resident((1, D)), _resident((D_FF, D)), _resident((D_FF, D)),
                  _resident((D_FF, D)), _resident((1, D))],
        out_specs=[_rows(tm, D_FF), _rows(tm, D_FF), _rows(tm, D), _const((8, D))],
        out_shape=[jax.ShapeDtypeStruct((t_len, D_FF), BF16), jax.ShapeDtypeStruct((t_len, D_FF), BF16),
                   jax.ShapeDtypeStruct((t_len, D), F32), jax.ShapeDtypeStruct((8, D), F32)],
        compiler_params=_params(("arbitrary",)),
    )(x1, target, g_ffn, w_gate, w_up, w_down, g_final)


def ffn_bwd(dx2, x1, gate, up, g_ffn, w_gate, w_up, w_down):
    t_len = x1.shape[0]
    tm = min(256, t_len)

    def body(dx2_ref, x1_ref, gate_ref, up_ref, g_ref, wg_ref, wu_ref, wd_ref,
             dx1_ref, dgate_ref, dup_ref, act_ref, h2_ref, stat_ref):
        @pl.when(pl.program_id(0) == 0)
        def _():
            stat_ref[...] = jnp.zeros((8, D), F32)

        dx2v = dx2_ref[...]
        gate = gate_ref[...]
        upv = up_ref[...]
        sg = _sigmoid(gate.astype(F32)).astype(BF16)
        silu = gate * sg
        act_ref[...] = silu * upv
        dact = _dot_nt(dx2v.astype(BF16), wd_ref[...]).astype(BF16)
        dup = dact * silu
        dgate = dact * upv * (sg * (1.0 + gate * (1.0 - sg)))
        dup_ref[...] = dup
        dgate_ref[...] = dgate
        dh2 = _dot(dgate, wg_ref[...]) + _dot(dup, wu_ref[...])
        x1v = x1_ref[...]
        r2 = lax.rsqrt(jnp.mean(x1v * x1v, axis=-1, keepdims=True) + EPS)
        xh = x1v * r2
        h2_ref[...] = (xh * g_ref[...]).astype(BF16)
        stat_ref[0:1, :] += jnp.sum(dh2 * xh, axis=0, keepdims=True)
        dxh = dh2 * g_ref[...]
        dx1_ref[...] = dx2v + r2 * (dxh - xh * jnp.mean(dxh * xh, axis=-1, keepdims=True))

    ff = jax.ShapeDtypeStruct((t_len, D_FF), BF16)
    return pl.pallas_call(
        body, name="ffn_bwd",
        grid=(t_len // tm,),
        in_specs=[_rows(tm, D), _rows(tm, D), _rows(tm, D_FF), _rows(tm, D_FF), _resident((1, D)),
                  _resident((D_FF, D)), _resident((D_FF, D)), _resident((D_FF, D))],
        out_specs=[_rows(tm, D), _rows(tm, D_FF), _rows(tm, D_FF), _rows(tm, D_FF), _rows(tm, D), _const((8, D))],
        out_shape=[jax.ShapeDtypeStruct((t_len, D), F32), ff, ff, ff, jax.ShapeDtypeStruct((t_len, D), BF16),
                   jax.ShapeDtypeStruct((8, D), F32)],
        compiler_params=_params(("arbitrary",)),
    )(dx2, x1, gate, up, g_ffn, w_gate, w_up, w_down)


def mixer_bwd(dx1, proj, ya, pp, yx, probs, conv_w8, w_co, w_pool, pool_scale, k, v, w_xo, w_out, carry=None):
    t_len = dx1.shape[0]
    tm = min(256, t_len)
    n_tiles = t_len // tm
    halo_blocks = tm // HALO

    def body(dx1_ref, proj_ref, halo_ref, ya_ref, pp_ref, yx_ref, p_ref,
             cw_ref, wco_ref, wpool_ref, ps_ref, k_ref, v_ref, wxo_ref, wout_ref, band_ref,
             dproj_ref, dya_ref, dpp_ref, dyx_ref, dk_ref, dv_ref, stat_ref,
             cu_ext, dconv_ext, dpn_ext):
        step = pl.program_id(0)
        tile = n_tiles - 1 - step

        @pl.when(step == 0)
        def _():
            dk_ref[...] = jnp.zeros((N_MEM, D), F32)
            dv_ref[...] = jnp.zeros((N_MEM, D), F32)
            stat_ref[...] = jnp.zeros((8, D), F32)
            dconv_ext[tm:tm + HALO, :] = jnp.zeros((HALO, D), F32)
            dpn_ext[tm:tm + POOL_PAD, :] = jnp.zeros((POOL_PAD, D), BF16)

        dmerged = _dot_nt(dx1_ref[...].astype(BF16), wout_ref[...]).astype(BF16)
        sa = _sigmoid(proj_ref[:, O_GA:O_GA + D].astype(F32)).astype(BF16)
        sp = _sigmoid(proj_ref[:, O_GP:O_GP + D].astype(F32)).astype(BF16)
        sx = _sigmoid(proj_ref[:, O_GX:O_GX + D].astype(F32)).astype(BF16)
        ya = ya_ref[...]
        ppv = pp_ref[...]
        scale = ps_ref[...].astype(BF16)
        yp = ppv * scale
        yx = yx_ref[...]
        dproj_ref[:, O_GA:O_GA + D] = dmerged * ya * (sa * (1.0 - sa))
        dproj_ref[:, O_GP:O_GP + D] = dmerged * yp * (sp * (1.0 - sp))
        dproj_ref[:, O_GX:O_GX + D] = dmerged * yx * (sx * (1.0 - sx))
        dya = dmerged * sa
        dyp = dmerged * sp
        dyx = dmerged * sx
        dya_ref[...] = dya
        dyx_ref[...] = dyx
        stat_ref[1:2, :] += jnp.sum(dyp.astype(F32) * ppv.astype(F32), axis=0, keepdims=True)
        dpp = dyp * scale
        dpp_ref[...] = dpp

        da = _dot_nt(dya, wco_ref[...])
        c_a = proj_ref[:, O_CA:O_CA + D].astype(F32)
        u_a = proj_ref[:, O_UA:O_UA + D].astype(F32)
        cu = c_a * u_a
        halo_cu = halo_ref[:, O_CA:O_CA + D].astype(F32) * halo_ref[:, O_UA:O_UA + D].astype(F32)
        cu_ext[0:HALO, :] = jnp.where(tile > 0, halo_cu, 0.0)
        cu_ext[HALO:HALO + tm, :] = cu
        cu1 = cu_ext[HALO - 1:HALO - 1 + tm, :]
        cu2 = cu_ext[HALO - 2:HALO - 2 + tm, :]
        conv = cw_ref[2:3, :] * cu + cw_ref[1:2, :] * cu1 + cw_ref[0:1, :] * cu2
        dproj_ref[:, O_BA:O_BA + D] = (da * conv).astype(BF16)
        dconv = da * proj_ref[:, O_BA:O_BA + D].astype(F32)
        stat_ref[5:6, :] += jnp.sum(dconv * cu2, axis=0, keepdims=True)
        stat_ref[6:7, :] += jnp.sum(dconv * cu1, axis=0, keepdims=True)
        stat_ref[7:8, :] += jnp.sum(dconv * cu, axis=0, keepdims=True)
        dconv_ext[0:tm, :] = dconv
        dcu = (cw_ref[2:3, :] * dconv + cw_ref[1:2, :] * dconv_ext[1:1 + tm, :]
               + cw_ref[0:1, :] * dconv_ext[2:2 + tm, :])
        dproj_ref[:, O_CA:O_CA + D] = (dcu * u_a).astype(BF16)
        dproj_ref[:, O_UA:O_UA + D] = (dcu * c_a).astype(BF16)

        for g, window in enumerate(POOL_WINDOWS):
            cols = slice(g * GROUP_DIM, (g + 1) * GROUP_DIM)
            dpooled = _dot_nt(dpp[:, cols], wpool_ref[g])
            dpn_ext[0:tm, cols] = (dpooled * _inv_count(tile, tm, window)).astype(BF16)
            acc = _dot(band_ref[g], dpn_ext[:, cols])
            dproj_ref[:, O_UP + g * GROUP_DIM:O_UP + (g + 1) * GROUP_DIM] = (acc - dpooled).astype(BF16)

        do = _dot_nt(dyx, wxo_ref[...])
        for hd in range(HEADS):
            cols = slice(hd * HEAD_DIM, (hd + 1) * HEAD_DIM)
            q = proj_ref[:, O_QX + hd * HEAD_DIM:O_QX + (hd + 1) * HEAD_DIM]
            kh = k_ref[:, cols]
            p16 = p_ref[:, hd * N_MEM:(hd + 1) * N_MEM]
            p = p16.astype(F32)
            doh = do[:, cols].astype(BF16)
            dp = _dot_nt(doh, v_ref[:, cols])
            dv_ref[:, cols] += _dot_tn(p16, doh)
            ds = (p * (dp - jnp.sum(dp * p, axis=-1, keepdims=True)) * ATT_SCALE).astype(BF16)
            dproj_ref[:, O_QX + hd * HEAD_DIM:O_QX + (hd + 1) * HEAD_DIM] = _dot(ds, kh).astype(BF16)
            dk_ref[:, cols] += _dot_tn(ds, q)

        dconv_ext[tm:tm + HALO, :] = dconv_ext[0:HALO, :]
        dpn_ext[tm:tm + HALO, :] = dpn_ext[0:HALO, :]

    def rev(width):
        return pl.BlockSpec((tm, width), lambda s: (n_tiles - 1 - s, 0))

    halo_spec = pl.BlockSpec((HALO, D_IN), lambda s: (jnp.maximum((n_tiles - 1 - s) * halo_blocks - 1, 0), 0))
    act = jax.ShapeDtypeStruct((t_len, D), BF16)
    kv_grad = jax.ShapeDtypeStruct((N_MEM, D), F32)
    return _carried_call(
        body, "mixer_bwd", (n_tiles,),
        in_specs=[rev(D), rev(D_IN), halo_spec, rev(D), rev(D), rev(D), rev(D),
                  _resident((8, D)), _resident((D, D)), _resident((GROUPS, GROUP_DIM, GROUP_DIM)), _resident((1, D)),
                  _resident((N_MEM, D)), _resident((N_MEM, D)), _resident((D, D)), _resident((D, D)),
                  _resident((GROUPS, tm, tm + POOL_PAD))],
        out_specs=[rev(D_IN), rev(D), rev(D), rev(D),
                   _const((N_MEM, D)), _const((N_MEM, D)), _const((8, D))],
        out_shape=[jax.ShapeDtypeStruct((t_len, D_IN), BF16), act, act, act, kv_grad, kv_grad,
                   jax.ShapeDtypeStruct((8, D), F32)],
        scratch_shapes=[pltpu.VMEM((tm + HALO, D), F32)] * 2 + [pltpu.VMEM((tm + POOL_PAD, D), BF16)],
        semantics=("arbitrary",),
        args=(dx1, proj, proj, ya, pp, yx, probs, conv_w8, w_co, w_pool, pool_scale, k, v, w_xo, w_out,
              _window_bands(tm, False)), carry=carry)


def in_bwd(dproj, w_in, x, dx1, g_mix, carry=None):
    t_len = x.shape[0]
    tm = min(512, t_len)

    def body(dproj_ref, w_ref, x_ref, dx1_ref, g_ref, gx_ref, stat_ref):
        @pl.when(pl.program_id(0) == 0)
        def _():
            stat_ref[...] = jnp.zeros((8, D), F32)

        dh = _dot_nt(dproj_ref[...], w_ref[...])
        xv = x_ref[...]
        r = lax.rsqrt(jnp.mean(xv * xv, axis=-1, keepdims=True) + EPS)
        xh = xv * r
        stat_ref[0:1, :] += jnp.sum(dh * xh, axis=0, keepdims=True)
        dxh = dh * g_ref[...]
        gx_ref[...] = dx1_ref[...] + r * (dxh - xh * jnp.mean(dxh * xh, axis=-1, keepdims=True))

    return _carried_call(
        body, "in_bwd", (t_len // tm,),
        in_specs=[_rows(tm, D_IN), _resident((D, D_IN)), _rows(tm, D), _rows(tm, D), _resident((1, D))],
        out_specs=[_rows(tm, D), _const((8, D))],
        out_shape=[jax.ShapeDtypeStruct((t_len, D), F32), jax.ShapeDtypeStruct((8, D), F32)],
        scratch_shapes=[], semantics=("arbitrary",), args=(dproj, w_in, x, dx1, g_mix), carry=carry)


def kv_bwd(dk, dv, memn, mem, g_mem, w_kv4):
    half = D // 2

    def body(dk_ref, dv_ref, memn_ref, mem_ref, g_ref, w_ref, gw_ref, gw16_ref, stat_ref):
        mn = memn_ref[...]
        parts = (dk_ref[:, 0:half], dk_ref[:, half:D], dv_ref[:, 0:half], dv_ref[:, half:D])
        dmemn = jnp.zeros((N_MEM, D), F32)
        for j, part in enumerate(parts):
            part = part.astype(BF16)
            gw = _dot_tn(mn, part)
            gw_ref[j] = gw
            gw16_ref[j] = gw.astype(BF16)
            dmemn = dmemn + _dot_nt(part, w_ref[j])
        mv = mem_ref[...]
        r = lax.rsqrt(jnp.mean(mv * mv, axis=-1, keepdims=True) + EPS)
        stat_ref[...] = jnp.zeros((8, D), F32)
        stat_ref[0:1, :] = jnp.sum(dmemn * (mv * r), axis=0, keepdims=True)

    return pl.pallas_call(
        body, name="kv_bwd",
        out_shape=[jax.ShapeDtypeStruct((N_CHIPS, D, half), F32), jax.ShapeDtypeStruct((N_CHIPS, D, half), BF16),
                   jax.ShapeDtypeStruct((8, D), F32)],
        compiler_params=_params(),
    )(dk, dv, memn, mem, g_mem, w_kv4)


def matmul_tn(name, a, b, tn, col_blocks=1, carry=None):
    t_len, k_dim = a.shape
    n_dim = b.shape[1]
    tt = min(2048, t_len)
    per_block = n_dim // col_blocks // tn

    def body(a_ref, b_ref, out_ref, out16_ref):
        @pl.when(pl.program_id(1) == 0)
        def _():
            out_ref[...] = jnp.zeros((k_dim, tn), F32)

        out_ref[...] += _dot_tn(a_ref[...].astype(BF16), b_ref[...].astype(BF16))

        @pl.when(pl.program_id(1) == t_len // tt - 1)
        def _():
            out16_ref[...] = out_ref[...].astype(BF16)

    if col_blocks == 1:
        out_spec = pl.BlockSpec((k_dim, tn), lambda n, t: (0, n))
        shape = (k_dim, n_dim)
    else:
        out_spec = pl.BlockSpec((None, k_dim, tn), lambda n, t: (n // per_block, 0, n % per_block))
        shape = (col_blocks, k_dim, n_dim // col_blocks)
    outs, carried = _carried_call(
        body, name, (n_dim // tn, t_len // tt),
        in_specs=[pl.BlockSpec((tt, k_dim), lambda n, t: (t, 0)), pl.BlockSpec((tt, tn), lambda n, t: (t, n))],
        out_specs=[out_spec, out_spec], out_shape=[jax.ShapeDtypeStruct(shape, F32), jax.ShapeDtypeStruct(shape, BF16)],
        scratch_shapes=[], semantics=("arbitrary", "arbitrary"), args=(a, b), carry=carry)
    return (tuple(outs), carried) if carry is not None else tuple(outs)


def grad_w_pool(pooled, dpp):
    t_len = pooled.shape[0]
    tt = min(1024, t_len)
    steps = t_len // tt

    def body(a_ref, b_ref, out_ref, out16_ref):
        @pl.when(pl.program_id(0) == 0)
        def _():
            out_ref[...] = jnp.zeros((GROUPS, GROUP_DIM, GROUP_DIM), F32)

        for g in range(GROUPS):
            cols = slice(g * GROUP_DIM, (g + 1) * GROUP_DIM)
            out_ref[g] += _dot_tn(a_ref[:, cols], b_ref[:, cols])

        @pl.when(pl.program_id(0) == steps - 1)
        def _():
            out16_ref[...] = out_ref[...].astype(BF16)

    shape = (GROUPS, GROUP_DIM, GROUP_DIM)
    return pl.pallas_call(
        body, name="grad_w_pool", grid=(steps,),
        in_specs=[_rows(tt, D), _rows(tt, D)], out_specs=[_const(shape), _const(shape)],
        out_shape=[jax.ShapeDtypeStruct(shape, F32), jax.ShapeDtypeStruct(shape, BF16)],
        compiler_params=_params(("arbitrary",)),
    )(pooled, dpp)


def _place():
    x, y, c = lax.axis_index("x"), lax.axis_index("y"), lax.axis_index("c")
    return x, y, c


def _other_chips(x, y):
    return [(1 - x, y), (x, 1 - y), (1 - x, 1 - y)]


def _any_specs(n):
    return [pl.BlockSpec(memory_space=pl.ANY)] * n


def exchange_call(name, exchange_cls, arrays):
    n = len(arrays)
    shapes = exchange_cls.out_shapes(arrays)

    def body(*refs):
        exchange = exchange_cls(refs[:n], refs[n:n + len(shapes)], *refs[n + len(shapes):])
        exchange.start()
        for phase in getattr(exchange_cls, "PHASES_AT", {}):
            getattr(exchange, phase)()
        exchange.finish()

    res = pl.pallas_call(
        body, name=name, in_specs=_any_specs(n), out_specs=_any_specs(len(shapes)),
        out_shape=shapes, scratch_shapes=exchange_cls.sems(n),
    )(*arrays)
    return res[:n]


class _Gather:
    SLOTS = 8
    ROW_ALIGN = 16
    PHASES_AT = {"middle": (4, 8), "late": (7, 8)}

    @staticmethod
    def out_shapes(shards):
        return [jax.ShapeDtypeStruct((N_CHIPS,) + s.shape, s.dtype) for s in shards]

    @staticmethod
    def sems(n):
        return [pltpu.SemaphoreType.DMA((n, _Gather.SLOTS)), pltpu.SemaphoreType.DMA((n, _Gather.SLOTS))]

    def __init__(self, ins, outs, send_sems, recv_sems):
        self.ins, self.outs, self.send_sems, self.recv_sems = ins, outs, send_sems, recv_sems
        x, y, c = _place()
        self.c, self.me, self.sibling = c, 2 * x + y, (x, y, 1 - c)
        self.across = [(1 - x, y), (x, 1 - y), (1 - x, 1 - y)]

    def _rows(self, a, which, part=None):
        half = self.ins[a].shape[1] // 2
        first = (half // 2) // self.ROW_ALIGN * self.ROW_ALIGN
        if part is None:
            return pl.ds(which * half, half)
        return pl.ds(which * half, first) if part == 0 else pl.ds(which * half + first, half - first)

    def _has_part(self, a, part):
        half = self.ins[a].shape[1] // 2
        return part == 1 or (half // 2) // self.ROW_ALIGN > 0

    def _block(self, a, chip, rows=slice(None)):
        index = chip if not isinstance(chip, tuple) else 2 * chip[0] + chip[1]
        if len(self.outs[a].shape) == len(self.ins[a].shape):
            cols = self.ins[a].shape[2]
            return self.outs[a].at[:, rows, pl.ds(pl.multiple_of(index * cols, cols), cols)]
        return self.outs[a].at[index, :, rows, :]

    def _remote(self, src, dst, a, slot, to):
        return pltpu.make_async_remote_copy(src_ref=src, dst_ref=dst, send_sem=self.send_sems.at[a, slot],
                                            recv_sem=self.recv_sems.at[a, slot], device_id=to, device_id_type=MESH)

    def _own(self, a):
        return self._remote(self.ins[a], self._block(a, self.me), a, 6, self.sibling)

    def _sent(self, a, axis):
        rows = self._rows(a, self.c)
        return self._remote(self.ins[a].at[:, rows, :], self._block(a, self.me, rows), a, axis,
                            (*self.across[axis], self.c))

    def _landed(self, a, axis):
        block = self._block(a, self.across[axis], self._rows(a, self.c))
        return self._remote(block, block, a, axis, (*self.across[axis], self.c))

    def _relayed(self, a, part, incoming):
        source = self.across[2] if incoming else self.across[part]
        block = self._block(a, source, self._rows(a, self.c, part))
        return self._remote(block, block, a, (2, 7)[part], (*self.across[1 - part], self.c))

    def _passed_on(self, a, source, which):
        block = self._block(a, self.across[source], self._rows(a, which))
        return self._remote(block, block, a, 3 + source, self.sibling)

    def start(self):
        for a in range(len(self.ins)):
            self._own(a).start()
        for a in range(len(self.ins)):
            for axis in range(2):
                self._sent(a, axis).start()

    def middle(self):
        for a in range(len(self.ins)):
            for axis in range(2):
                self._landed(a, axis).wait_recv()
                if self._has_part(a, axis):
                    self._relayed(a, axis, incoming=False).start()
                self._passed_on(a, axis, self.c).start()

    def late(self):
        for a in range(len(self.ins)):
            for part in range(2):
                if self._has_part(a, part):
                    self._relayed(a, part, incoming=True).wait_recv()
            self._passed_on(a, 2, self.c).start()

    def finish(self):
        n = len(self.ins)
        for a in range(n):
            for source in range(3):
                self._passed_on(a, source, 1 - self.c).wait_recv()
            self._own(a).wait_recv()
        for a in range(n):
            self._own(a).wait_send()
            for axis in range(2):
                self._sent(a, axis).wait_send()
                if self._has_part(a, axis):
                    self._relayed(a, axis, incoming=False).wait_send()
            for source in range(3):
                self._passed_on(a, source, self.c).wait_send()


class _HalfExchange:
    @staticmethod
    def out_shapes(grads):
        return [jax.ShapeDtypeStruct((N_CHIPS, g.shape[1] // 2, g.shape[2]), g.dtype) for g in grads]

    @staticmethod
    def sems(n):
        return [pltpu.SemaphoreType.DMA((n,)), pltpu.SemaphoreType.DMA((n,))]

    def __init__(self, ins, outs, send_sems, recv_sems):
        self.ins, self.outs, self.send_sems, self.recv_sems = ins, outs, send_sems, recv_sems

    def _copies(self):
        x, y, c = _place()
        for a in range(len(self.ins)):
            h = self.ins[a].shape[1] // 2
            yield pltpu.make_async_remote_copy(
                src_ref=self.ins[a].at[:, pl.ds((1 - c) * h, h), :], dst_ref=self.outs[a],
                send_sem=self.send_sems.at[a], recv_sem=self.recv_sems.at[a], device_id=(x, y, 1 - c), device_id_type=MESH)

    def start(self):
        for cp in self._copies():
            cp.start()

    def finish(self):
        for cp in self._copies():
            cp.wait()


class _GatherColumns(_Gather):
    @staticmethod
    def out_shapes(shards):
        return [jax.ShapeDtypeStruct(s.shape[:2] + (N_CHIPS * s.shape[2],), s.dtype) for s in shards]


def _row_tile(rows, cols, budget=2 << 20):
    best = 16
    for tr in range(16, rows + 1, 16):
        if rows % tr == 0 and tr * cols * 4 <= budget:
            best = tr
    return best


def add_sibling_half(name, grad, got, place):
    _, rows, cols = grad.shape
    h = rows // 2
    tr = _row_tile(h, cols)
    per_half = h // tr

    def body(place_ref, g_ref, o_ref, own_ref, out16_ref):
        total = g_ref[...] + o_ref[...].astype(F32)
        out16_ref[...] = total.astype(BF16)

        @pl.when(pl.program_id(1) == place_ref[1])
        def _():
            own_ref[...] = total

    return pl.pallas_call(
        body, name=name,
        grid_spec=pltpu.PrefetchScalarGridSpec(
            num_scalar_prefetch=1, grid=(per_half, N_CHIPS),
            in_specs=[pl.BlockSpec((None, tr, cols), lambda r, j, pr: (j, pr[0] * per_half + r, 0)),
                      pl.BlockSpec((None, tr, cols), lambda r, j, pr: (j, r, 0))],
            out_specs=[pl.BlockSpec((tr, cols), lambda r, j, pr: (r, 0)),
                       pl.BlockSpec((None, tr, cols), lambda r, j, pr: (j, r, 0))]),
        out_shape=[jax.ShapeDtypeStruct((h, cols), F32), jax.ShapeDtypeStruct((N_CHIPS, h, cols), BF16)],
        compiler_params=_params(("arbitrary", "arbitrary")),
    )(place, grad, got)


class _ChipExchange:
    SLOTS = 6
    ROW_ALIGN = 16
    PHASES_AT = {"middle": (3, 8)}

    @staticmethod
    def out_shapes(partials):
        return ([jax.ShapeDtypeStruct((3,) + p.shape[1:], p.dtype) for p in partials]
                + [jax.ShapeDtypeStruct(p.shape[1:], p.dtype) for p in partials])

    @staticmethod
    def sems(n):
        return [pltpu.SemaphoreType.DMA((n, _ChipExchange.SLOTS)), pltpu.SemaphoreType.DMA((n, _ChipExchange.SLOTS))]

    def __init__(self, ins, outs, send_sems, recv_sems):
        n = len(ins)
        self.ins, self.outs, self.relays, self.send_sems, self.recv_sems = ins, outs[:n], outs[n:], send_sems, recv_sems

    def _copy(self, a, slot):
        x, y, c = _place()
        across = _other_chips(x, y)
        rows = self.ins[a].shape[1]
        first = (rows // 2) // self.ROW_ALIGN * self.ROW_ALIGN
        part = (pl.ds(0, first), pl.ds(first, rows - first))
        if slot < 2:
            px, py = across[slot]
            src, dst, to = self.ins[a].at[2 * px + py], self.outs[a].at[slot], across[slot]
        elif slot < 4:
            px, py = across[2]
            src, dst, to = self.ins[a].at[2 * px + py, part[slot - 2], :], self.relays[a].at[part[slot - 2], :], across[slot - 2]
        else:
            src, dst, to = self.relays[a].at[part[slot - 4], :], self.outs[a].at[2, part[slot - 4], :], across[5 - slot]
        return pltpu.make_async_remote_copy(src_ref=src, dst_ref=dst, send_sem=self.send_sems.at[a, slot],
                                            recv_sem=self.recv_sems.at[a, slot], device_id=(*to, c), device_id_type=MESH)

    def start(self):
        for slot in (2, 3, 0, 1):
            for a in range(len(self.ins)):
                self._copy(a, slot).start()

    def middle(self):
        for part in range(2):
            for a in range(len(self.ins)):
                self._copy(a, 2 + part).wait_recv()
                self._copy(a, 4 + part).start()

    def finish(self):
        for a in range(len(self.ins)):
            for slot in (0, 1, 4, 5):
                self._copy(a, slot).wait_recv()
        for a in range(len(self.ins)):
            for slot in range(self.SLOTS):
                self._copy(a, slot).wait_send()


def add_chip_blocks(name, own, got):
    h, cols = own.shape
    tr = _row_tile(h, cols)

    def body(p_ref, g0_ref, g1_ref, g2_ref, out_ref):
        out_ref[...] = ((p_ref[...] + g0_ref[...].astype(F32)) + g1_ref[...].astype(F32)) + g2_ref[...].astype(F32)

    def got_spec(slot):
        return pl.BlockSpec((None, tr, cols), lambda r: (slot, r, 0))

    return pl.pallas_call(
        body, name=name, grid=(h // tr,),
        in_specs=[_rows(tr, cols), got_spec(0), got_spec(1), got_spec(2)], out_specs=_rows(tr, cols),
        out_shape=jax.ShapeDtypeStruct((h, cols), F32),
        compiler_params=_params(("arbitrary",)),
    )(own, got, got, got)


class _SiblingSwap:
    @staticmethod
    def out_shapes(halves):
        return [jax.ShapeDtypeStruct(v.shape, v.dtype) for v in halves]

    @staticmethod
    def sems(n):
        return [pltpu.SemaphoreType.DMA((n,)), pltpu.SemaphoreType.DMA((n,))]

    def __init__(self, ins, outs, send_sems, recv_sems):
        self.ins, self.outs, self.send_sems, self.recv_sems = ins, outs, send_sems, recv_sems

    def _copies(self):
        x, y, c = _place()
        for a in range(len(self.ins)):
            yield pltpu.make_async_remote_copy(
                src_ref=self.ins[a], dst_ref=self.outs[a], send_sem=self.send_sems.at[a], recv_sem=self.recv_sems.at[a],
                device_id=(x, y, 1 - c), device_id_type=MESH)

    def start(self):
        for cp in self._copies():
            cp.start()

    def finish(self):
        for cp in self._copies():
            cp.wait()


def all_reduce_small(pack):
    rows = pack.shape[0]

    def body(pack_ref, out_ref, gathered, send_sems, recv_sems):
        x, y, c = _place()
        me = 4 * x + 2 * y + c
        gathered[me] = pack_ref[...]
        copies = []
        for rel in range(1, 8):
            fx, fy, fc = (rel >> 2) & 1, (rel >> 1) & 1, rel & 1
            peer = (x ^ fx, y ^ fy, c ^ fc)
            cp = pltpu.make_async_remote_copy(
                src_ref=pack_ref, dst_ref=gathered.at[me], send_sem=send_sems.at[rel - 1], recv_sem=recv_sems.at[rel - 1],
                device_id=peer, device_id_type=MESH)
            cp.start()
            copies.append(cp)
        for rel in range(1, 8):
            fx, fy, fc = (rel >> 2) & 1, (rel >> 1) & 1, rel & 1
            src = 4 * (x ^ fx) + 2 * (y ^ fy) + (c ^ fc)
            pltpu.make_async_remote_copy(
                src_ref=pack_ref, dst_ref=gathered.at[src], send_sem=send_sems.at[rel - 1], recv_sem=recv_sems.at[rel - 1],
                device_id=(x, y, c), device_id_type=MESH).wait_recv()
        for cp in copies:
            cp.wait_send()
        total = gathered[0]
        for dev in range(1, 8):
            total = total + gathered[dev]
        out_ref[...] = total

    return pl.pallas_call(
        body, name="all_reduce_small",
        in_specs=[pl.BlockSpec(memory_space=pltpu.VMEM)], out_specs=pl.BlockSpec(memory_space=pltpu.VMEM),
        out_shape=jax.ShapeDtypeStruct((rows, D), F32),
        scratch_shapes=[pltpu.VMEM((8, rows, D), F32), pltpu.SemaphoreType.DMA((7,)), pltpu.SemaphoreType.DMA((7,))],
    )(pack)


def _adamw_update(w, g, m, v):
    nm = ADAM_B1 * m + (1.0 - ADAM_B1) * g
    nv = ADAM_B2 * v + (1.0 - ADAM_B2) * (g * g)
    m_hat = nm / (1.0 - ADAM_B1 ** ADAM_STEP)
    v_hat = nv / (1.0 - ADAM_B2 ** ADAM_STEP)
    delta = -ADAM_LR * (m_hat / (jnp.sqrt(v_hat) + ADAM_EPS) + ADAM_WD * w)
    return delta, nm, nv


def adamw(name, w, g, m, v):
    def body(w_ref, g_ref, m_ref, v_ref, d_ref, nm_ref, nv_ref):
        d_ref[...], nm_ref[...], nv_ref[...] = _adamw_update(w_ref[...], g_ref[...], m_ref[...], v_ref[...])

    out = jax.ShapeDtypeStruct(w.shape, F32)
    return pl.pallas_call(body, name=name, out_shape=[out] * 3, compiler_params=_params())(w, g, m, v)


def adamw_halves(name, w, mine, theirs, m, v, core):
    h, cols = mine.shape
    tr = _row_tile(h, cols)
    per_half = h // tr

    def body(core_ref, w_ref, mine_ref, theirs_ref, m_ref, v_ref, g_ref, d_ref, nm_ref, nv_ref):
        g = jnp.where(pl.program_id(0) == core_ref[0], mine_ref[...], theirs_ref[...])
        g_ref[...] = g
        d_ref[...], nm_ref[...], nv_ref[...] = _adamw_update(w_ref[...], g, m_ref[...], v_ref[...])

    full = pl.BlockSpec((tr, cols), lambda hh, r, cr: (hh * per_half + r, 0))
    mine_spec = pl.BlockSpec((tr, cols), lambda hh, r, cr: (jnp.where(hh == cr[0], r, 0), 0))
    theirs_spec = pl.BlockSpec((tr, cols), lambda hh, r, cr: (jnp.where(hh == cr[0], 0, r), 0))
    out = jax.ShapeDtypeStruct((2 * h, cols), F32)
    return pl.pallas_call(
        body, name=name,
        grid_spec=pltpu.PrefetchScalarGridSpec(
            num_scalar_prefetch=1, grid=(2, per_half),
            in_specs=[full, mine_spec, theirs_spec, full, full], out_specs=[full] * 4),
        out_shape=[out] * 4,
        compiler_params=_params(("arbitrary", "arbitrary")),
    )(core, w, mine, theirs, m, v)


BIG = ("w_in", "w_conv_out", "w_pool", "w_kv", "w_xattn_out", "w_out", "w_gate", "w_up", "w_down")


def kernel(x, mem, norm_mix, w_in, conv_w, w_conv_out, w_pool, pool_scale, norm_mem, w_kv, w_xattn_out, w_out, norm_ffn, w_gate, w_up, w_down, norm_final, loss_target, m_norm_mix, m_w_in, m_conv_w, m_w_conv_out, m_w_pool, m_pool_scale, m_norm_mem, m_w_kv, m_w_xattn_out, m_w_out, m_norm_ffn, m_w_gate, m_w_up, m_w_down, m_norm_final, v_norm_mix, v_w_in, v_conv_w, v_w_conv_out, v_w_pool, v_pool_scale, v_norm_mem, v_w_kv, v_w_xattn_out, v_w_out, v_norm_ffn, v_w_gate, v_w_up, v_w_down, v_norm_final):
    t_len = x.shape[1]
    xi, yi, ci = lax.axis_index("x"), lax.axis_index("y"), lax.axis_index("c")
    chip = 2 * xi + yi
    core_arr = jnp.reshape(ci, (1,)).astype(jnp.int32)
    place_arr = jnp.stack([ci, chip]).astype(jnp.int32)

    conv_pad = jnp.concatenate([conv_w, jnp.zeros((1, 13, 256), F32)], axis=1)
    def t2(w):
        return jnp.swapaxes(w, 1, 2)

    (g_in,) = exchange_call("gather_w_in", _GatherColumns, [w_in.astype(BF16)])
    w_in_f = g_in[0]

    x2d = x[0]
    tgt = loss_target[0]
    (proj, h), (g_kv, g_conv_w, g_co, g_xo, g_out, g_pool, g_gate) = proj_fwd(
        x2d, norm_mix, w_in_f,
        carry=(_Gather, [w_kv.astype(BF16), conv_pad,
                         w_conv_out.astype(BF16), w_xattn_out.astype(BF16), w_out.astype(BF16),
                         w_pool[0].astype(BF16),
                         t2(w_gate).astype(BF16)]))
    w_kv4 = g_kv.reshape(N_CHIPS, D, D // 2)
    conv_full = jnp.transpose(g_conv_w[:, 0, 0:8, :], (1, 0, 2)).reshape(8, D)
    w_co_f, w_xo_f, w_out_f = g_co.reshape(D, D), g_xo.reshape(D, D), g_out.reshape(D, D)
    w_pool_f = jnp.transpose(g_pool, (1, 0, 2, 3)).reshape(GROUPS, GROUP_DIM, GROUP_DIM)
    memn, k, v = kv_fwd(mem[0], norm_mem, w_kv4)
    (a, pooled, ya, pp, yx, o, probs, merged, x1), (g_up, g_down) = mixer_fwd(
        proj, x2d, conv_full, w_co_f, w_pool_f, pool_scale, k, v, w_xo_f, w_out_f,
        carry=(_Gather, [t2(w_up).astype(BF16), w_down.astype(BF16)]))
    w_gate_f, w_up_f, w_down_f = g_gate.reshape(D_FF, D), g_up.reshape(D_FF, D), g_down.reshape(D_FF, D)
    gate, up, dx2, stat_f = ffn_fwd(x1, tgt, norm_ffn, w_gate_f, w_up_f, w_down_f, norm_final.reshape(1, D))

    def by_chip(pair):
        return tuple(gw.reshape(N_CHIPS, gw.shape[0] // N_CHIPS, gw.shape[1]) for gw in pair)

    def chip_partials(names, grads, got):
        return [add_sibling_half("add_sibling_" + n, g32, o_, place_arr) for n, (g32, _), o_ in zip(names, grads, got)]

    def chip_sums(names, partials, got):
        return [add_chip_blocks("add_chips_" + n, own, g2) for n, (own, _), g2 in zip(names, partials, got)]

    dx1, dgate, dup, act, h2, stat_b1 = ffn_bwd(dx2, x1, gate, up, norm_ffn, w_gate_f, w_up_f, w_down_f)
    gw_gate = by_chip(matmul_tn("grad_w_gate", dgate, h2, 512))
    gw_up, got_gate = matmul_tn("grad_w_up", dup, h2, 512, carry=(_HalfExchange, [gw_gate[1]]))
    gw_up = by_chip(gw_up)
    gw_down, got_up = matmul_tn("grad_w_down", act, dx2, 512, carry=(_HalfExchange, [gw_up[1]]))
    gw_down = by_chip(gw_down)
    got_down = exchange_call("exchange_halves_w_down", _HalfExchange, [gw_down[1]])
    names_ffn = ("w_gate", "w_up", "w_down")
    part_ffn = chip_partials(names_ffn, [gw_gate, gw_up, gw_down], list(got_gate) + list(got_up) + list(got_down))

    (dproj, dya, dpp, dyx, dk, dv, stat_b2), got_ffn = mixer_bwd(
        dx1, proj, ya, pp, yx, probs, conv_full, w_co_f, w_pool_f, pool_scale, k, v, w_xo_f, w_out_f,
        carry=(_ChipExchange, [p16 for _, p16 in part_ffn]))
    gw_kv32, gw_kv16, stat_kv = kv_bwd(dk, dv, memn, mem[0], norm_mem, w_kv4)
    gw_pool = tuple(jnp.transpose(gw.reshape(GROUPS, N_CHIPS, 64, GROUP_DIM), (1, 0, 2, 3)).reshape(N_CHIPS, 256, GROUP_DIM)
                    for gw in grad_w_pool(pooled, dpp))
    gw_co, got_kv_pool = matmul_tn("grad_w_conv_out", a, dya, 1024, carry=(_HalfExchange, [gw_kv16, gw_pool[1]]))
    gw_co = by_chip(gw_co)
    gw_xo, got_co = matmul_tn("grad_w_xattn_out", o, dyx, 1024, carry=(_HalfExchange, [gw_co[1]]))
    gw_xo = by_chip(gw_xo)
    gw_out, got_xo = matmul_tn("grad_w_out", merged, dx1, 1024, carry=(_HalfExchange, [gw_xo[1]]))
    gw_out = by_chip(gw_out)
    got_out = exchange_call("exchange_halves_w_out", _HalfExchange, [gw_out[1]])
    names_mix = ("w_kv", "w_pool", "w_conv_out", "w_xattn_out", "w_out")
    part_mix = chip_partials(names_mix, [(gw_kv32, gw_kv16), gw_pool, gw_co, gw_xo, gw_out],
                             list(got_kv_pool) + list(got_co) + list(got_xo) + list(got_out))

    gw_in, got_mix = matmul_tn("grad_w_in", h, dproj, 2048, col_blocks=N_CHIPS,
                               carry=(_ChipExchange, [p16 for _, p16 in part_mix]))
    part_in = chip_partials(("w_in",), [gw_in], exchange_call("exchange_halves_w_in", _HalfExchange, [gw_in[1]]))
    mine_early = chip_sums(names_ffn + names_mix, part_ffn + part_mix, list(got_ffn) + list(got_mix))
    (grad_x, stat_b3), (got_in, theirs_early) = in_bwd(
        dproj, w_in_f, x2d, dx1, norm_mix,
        carry=[(_ChipExchange, [p16 for _, p16 in part_in]), (_SiblingSwap, mine_early)])
    mine_in = chip_sums(("w_in",), part_in, got_in)
    theirs_in = exchange_call("swap_halves_w_in", _SiblingSwap, mine_in)
    reduced = dict(zip(names_ffn + names_mix + ("w_in",),
                       zip(mine_early + mine_in, list(theirs_early) + list(theirs_in))))
    mine = [reduced[n][0] for n in BIG]
    theirs = [reduced[n][1] for n in BIG]

    pack = jnp.concatenate([stat_b3[0:1], stat_b2[1:2], stat_kv[0:1], stat_b1[0:1], stat_f[0:1], stat_b2[5:8],
                            stat_f[1:2], jnp.zeros((7, D), F32)], axis=0)
    total = all_reduce_small(pack)
    loss = jnp.sum(total[8])
    g_conv_full = total[5:8]
    g_conv = lax.dynamic_slice_in_dim(g_conv_full, chip * 256, 256, axis=1)

    given = dict(w_in=(w_in, m_w_in, v_w_in), w_conv_out=(w_conv_out, m_w_conv_out, v_w_conv_out),
                 w_pool=(w_pool, m_w_pool, v_w_pool), w_kv=(w_kv, m_w_kv, v_w_kv),
                 w_xattn_out=(w_xattn_out, m_w_xattn_out, v_w_xattn_out), w_out=(w_out, m_w_out, v_w_out),
                 w_gate=(w_gate, m_w_gate, v_w_gate), w_up=(w_up, m_w_up, v_w_up), w_down=(w_down, m_w_down, v_w_down))
    out_g, out_d, out_m, out_v = {}, {}, {}, {}
    for n, mine_n, theirs_n in zip(BIG, mine, theirs):
        transposed = n in ("w_gate", "w_up")
        rows2d = (2 * mine_n.shape[0], mine_n.shape[1])
        w_, m_, v_ = ((t2(t) if transposed else t).reshape(rows2d) for t in given[n])
        res = adamw_halves("adamw_" + n, w_, mine_n, theirs_n, m_, v_, core_arr)
        if transposed:
            res = [t2(t.reshape(1, D_FF // N_CHIPS, D)) for t in res]
        out_g[n], out_d[n], out_m[n], out_v[n] = (t.reshape(given[n][0].shape) for t in res)

    def small_pack(vals, conv_part):
        conv_rows = jnp.concatenate([conv_part.reshape(3, 256), jnp.zeros((3, D - 256), F32)], axis=1)
        return jnp.concatenate([val.reshape(1, D) for val in vals] + [conv_rows], axis=0)

    sw = small_pack([norm_mix, pool_scale, norm_mem, norm_ffn, norm_final], conv_w)
    sm = small_pack([m_norm_mix, m_pool_scale, m_norm_mem, m_norm_ffn, m_norm_final], m_conv_w)
    sv = small_pack([v_norm_mix, v_pool_scale, v_norm_mem, v_norm_ffn, v_norm_final], v_conv_w)
    sg = small_pack([total[r] for r in range(5)], g_conv)
    sd, snm, snv = adamw("adamw_small", sw, sg, sm, sv)
    small_names = ("norm_mix", "pool_scale", "norm_mem", "norm_ffn", "norm_final")
    small_shapes = dict(norm_mix=(1, D), pool_scale=(1, D), norm_mem=(1, D), norm_ffn=(1, D), norm_final=(D,))
    for r, n in enumerate(small_names):
        out_g[n], out_d[n], out_m[n], out_v[n] = (t[r].reshape(small_shapes[n]) for t in (sg, sd, snm, snv))
    out_g["conv_w"], out_d["conv_w"], out_m["conv_w"], out_v["conv_w"] = (
        t[5:8, 0:256].reshape(1, 3, 256) for t in (sg, sd, snm, snv))

    order = ("norm_mix", "w_in", "conv_w", "w_conv_out", "w_pool", "pool_scale", "norm_mem", "w_kv", "w_xattn_out",
             "w_out", "norm_ffn", "w_gate", "w_up", "w_down", "norm_final")
    return (loss, grad_x.reshape(1, t_len, D), *[out_g[n] for n in order], *[out_d[n] for n in order],
            *[out_m[n] for n in order], *[out_v[n] for n in order])
```
